```python
import math
import jax, jax.numpy as jnp
from jax import lax
import numpy as np

D_MODEL = 1024
BATCH = 8
SEQ = 16384
DEPTH = 2

N_MIXERS = 2
N_GDN_LAYERS = (DEPTH + 1) // 2
N_SSD_LAYERS = DEPTH // 2
CONV_K = 4
EPS = 1e-6

GDN_H_QK = 8
GDN_H_V = 16
GDN_DK = 128
GDN_DV = 128
GDN_QK_W = GDN_H_QK * GDN_DK
GDN_V_W = GDN_H_V * GDN_DV
GDN_CONV_C = 2 * GDN_QK_W + GDN_V_W
GDN_IN_W = GDN_CONV_C + GDN_V_W + 2 * GDN_H_V
GDN_CHUNK = 64

SSD_D_INNER = 2 * D_MODEL
SSD_HEADDIM = 64
SSD_H = SSD_D_INNER // SSD_HEADDIM
SSD_N = 128
SSD_G = 8
SSD_R = SSD_H // SSD_G
SSD_CONV_C = SSD_D_INNER + 2 * SSD_G * SSD_N
SSD_IN_W = SSD_D_INNER + SSD_CONV_C + SSD_H
SSD_CHUNK = 128

kernel_name = "hybrid_gdn_mamba2_interleaved"


def _rmsnorm(x, w):
    xf = x.astype(jnp.float32)
    y = xf * lax.rsqrt(jnp.mean(xf * xf, axis=-1, keepdims=True) + EPS)
    return (y * w.astype(jnp.float32)).astype(x.dtype)


def _l2norm(x):
    xf = x.astype(jnp.float32)
    return xf * lax.rsqrt(jnp.sum(xf * xf, axis=-1, keepdims=True) + EPS)


def _causal_conv(x, w):
    k = w.shape[0]
    c = x.shape[-1]
    return lax.conv_general_dilated(
        x, w[:, None, :].astype(x.dtype), (1,), [(k - 1, 0)],
        dimension_numbers=("NWC", "WIO", "NWC"), feature_group_count=c)


def _chunk_gated_delta_rule(q, k, v, g, beta):
    b, t, h, dk = q.shape
    dv = v.shape[-1]
    c = GDN_CHUNK
    n = t // c
    f32 = jnp.float32
    def blk(a):
        return a.astype(f32).reshape(b, n, c, h, a.shape[-1]).transpose(0, 3, 1, 2, 4)
    q, k, v = blk(q), blk(k), blk(v)
    g = g.astype(f32).reshape(b, n, c, h).transpose(0, 3, 1, 2)
    beta = beta.astype(f32).reshape(b, n, c, h).transpose(0, 3, 1, 2)
    gc = jnp.cumsum(g, axis=-1)
    tril = jnp.tril(jnp.ones((c, c), dtype=bool))
    strict = jnp.tril(jnp.ones((c, c), dtype=bool), k=-1)
    L = jnp.exp(jnp.where(tril, gc[..., :, None] - gc[..., None, :], -jnp.inf))
    kb = k * beta[..., None]
    vb = v * beta[..., None]
    kkt = jnp.einsum("bhncd,bhnsd->bhncs", kb, k) * L
    a_mat = jnp.where(strict, kkt, 0.0) + jnp.eye(c, dtype=f32)
    rhs = jnp.concatenate([vb, kb * jnp.exp(gc)[..., None]], axis=-1)
    sol = lax.linalg.triangular_solve(a_mat, rhs, left_side=True, lower=True,
                                      unit_diagonal=True)
    u, w = sol[..., :dv], sol[..., dv:]
    attn = jnp.einsum("bhncd,bhnsd->bhncs", q, k) * L
    q_dec = q * jnp.exp(gc)[..., None]
    k_dec = k * jnp.exp(gc[..., -1:] - gc)[..., None]
    g_last = jnp.exp(gc[..., -1])

    def step(S, inp):
        qd, kd, u_c, w_c, a_c, gl = inp
        v_new = u_c - jnp.einsum("bhcd,bhde->bhce", w_c, S)
        o = jnp.einsum("bhcd,bhde->bhce", qd, S) + jnp.einsum("bhcs,bhse->bhce", a_c, v_new)
        S = S * gl[..., None, None] + jnp.einsum("bhcd,bhce->bhde", kd, v_new)
        return S, o

    mv = lambda a: jnp.moveaxis(a, 2, 0)
    S0 = jnp.zeros((b, h, dk, dv), f32)
    _, o = lax.scan(step, S0, (mv(q_dec), mv(k_dec), mv(u), mv(w), mv(attn), mv(g_last)))
    return o.transpose(1, 0, 3, 2, 4).reshape(b, t, h, dv)


def _gated_deltanet(hid, w_in, conv_w, a_log, dt_bias, norm_w, w_out):
    b, t, _ = hid.shape
    proj = hid @ w_in
    qkv, z, b_raw, a_raw = jnp.split(
        proj, [GDN_CONV_C, GDN_CONV_C + GDN_V_W, GDN_CONV_C + GDN_V_W + GDN_H_V], axis=-1)
    qkv = jax.nn.silu(_causal_conv(qkv, conv_w))
    q, k, v = jnp.split(qkv, [GDN_QK_W, 2 * GDN_QK_W], axis=-1)
    rep = GDN_H_V // GDN_H_QK
    q = jnp.repeat(_l2norm(q.reshape(b, t, GDN_H_QK, GDN_DK)) * (GDN_DK ** -0.5), rep, axis=2)
    k = jnp.repeat(_l2norm(k.reshape(b, t, GDN_H_QK, GDN_DK)), rep, axis=2)
    v = v.reshape(b, t, GDN_H_V, GDN_DV)
    beta = jax.nn.sigmoid(b_raw.astype(jnp.float32))
    g = -jnp.exp(a_log.astype(jnp.float32)) * jax.nn.softplus(
        a_raw.astype(jnp.float32) + dt_bias.astype(jnp.float32))
    o = _chunk_gated_delta_rule(q, k, v, g, beta).astype(hid.dtype)
    o = _rmsnorm(o, norm_w) * jax.nn.silu(z.reshape(b, t, GDN_H_V, GDN_DV))
    return o.reshape(b, t, GDN_V_W) @ w_out


def _ssd_chunked(X, adt, Bm, Cm):
    b, t, h, p = X.shape
    q = SSD_CHUNK
    n = t // q
    f32 = jnp.float32
    X = X.astype(f32).reshape(b, n, q, SSD_G, SSD_R, p)
    adt = adt.astype(f32).reshape(b, n, q, SSD_G, SSD_R).transpose(0, 3, 4, 1, 2)
    Bm = Bm.astype(f32).reshape(b, n, q, SSD_G, SSD_N)
    Cm = Cm.astype(f32).reshape(b, n, q, SSD_G, SSD_N)
    acs = jnp.cumsum(adt, axis=-1)
    tril = jnp.tril(jnp.ones((q, q), dtype=bool))
    Lm = jnp.exp(jnp.where(tril, acs[..., :, None] - acs[..., None, :], -jnp.inf))
    cb = jnp.einsum("bnlgd,bnsgd->bgnls", Cm, Bm)
    y_diag = jnp.einsum("bgrnls,bnsgrp->bnlgrp", cb[:, :, None] * Lm, X)
    decay_states = jnp.exp(acs[..., -1:] - acs)
    states = jnp.einsum("bnsgd,bgrns,bnsgrp->bngrpd", Bm, decay_states, X)
    chunk_decay = jnp.exp(acs[..., -1])

    def step(S, inp):
        st, dec = inp
        return S * dec[..., None, None] + st, S

    S0 = jnp.zeros((b, SSD_G, SSD_R, p, SSD_N), f32)
    _, prev = lax.scan(step, S0, (jnp.moveaxis(states, 1, 0), jnp.moveaxis(chunk_decay, 3, 0)))
    prev = jnp.moveaxis(prev, 0, 1)
    y_off = jnp.einsum("bnlgd,bngrpd,bgrnl->bnlgrp", Cm, prev, jnp.exp(acs))
    return (y_diag + y_off).reshape(b, t, h, p)


def _mamba2(hid, w_in, conv_w, conv_b, dt_bias, a_log, d_skip, norm_w, w_out):
    b, t, _ = hid.shape
    proj = hid @ w_in
    z, xbc, dt = jnp.split(proj, [SSD_D_INNER, SSD_D_INNER + SSD_CONV_C], axis=-1)
    xbc = jax.nn.silu(_causal_conv(xbc, conv_w) + conv_b)
    xs, Bm, Cm = jnp.split(xbc, [SSD_D_INNER, SSD_D_INNER + SSD_G * SSD_N], axis=-1)
    xs = xs.reshape(b, t, SSD_H, SSD_HEADDIM)
    Bm = Bm.reshape(b, t, SSD_G, SSD_N)
    Cm = Cm.reshape(b, t, SSD_G, SSD_N)
    dt = jax.nn.softplus(dt.astype(jnp.float32) + dt_bias.astype(jnp.float32))
    A = -jnp.exp(a_log.astype(jnp.float32))
    y = _ssd_chunked(xs.astype(jnp.float32) * dt[..., None], A * dt, Bm, Cm)
    y = (y + xs.astype(jnp.float32) * d_skip.astype(jnp.float32)[:, None]).astype(hid.dtype)
    y = y.reshape(b, t, SSD_D_INNER) * jax.nn.silu(z)
    gs = SSD_D_INNER // SSD_G
    y = _rmsnorm(y.reshape(b, t, SSD_G, gs), norm_w.reshape(SSD_G, gs)).reshape(b, t, SSD_D_INNER)
    return y @ w_out


def _dt_bias_init(key, n):
    dt = jnp.exp(jax.random.uniform(key, (n,), minval=math.log(1e-3), maxval=math.log(1e-1)))
    return dt + jnp.log(-jnp.expm1(-dt))


def _fwd_setup_inputs(seed: int = 0) -> dict:
    key = jax.random.key(seed)
    ks = jax.random.split(key, 20)
    nA, nB = N_GDN_LAYERS, N_SSD_LAYERS
    nrm = jax.random.normal
    return {
        "x": nrm(ks[0], (BATCH, SEQ, D_MODEL), jnp.float32),
        "norm_w": 1.0 + 0.01 * nrm(ks[1], (DEPTH, D_MODEL), jnp.float32),
        "gdn_w_in": nrm(ks[2], (nA, D_MODEL, GDN_IN_W), jnp.float32) * D_MODEL ** -0.5,
        "gdn_conv_w": nrm(ks[3], (nA, CONV_K, GDN_CONV_C), jnp.float32) * 0.5,
        "gdn_a_log": jnp.log(jax.random.uniform(ks[4], (nA, GDN_H_V), minval=1.0, maxval=16.0)),
        "gdn_dt_bias": jnp.stack([_dt_bias_init(kk, GDN_H_V) for kk in jax.random.split(ks[5], nA)]),
        "gdn_norm_w": 1.0 + 0.01 * nrm(ks[6], (nA, GDN_DV), jnp.float32),
        "gdn_w_out": nrm(ks[7], (nA, GDN_V_W, D_MODEL), jnp.float32) * GDN_V_W ** -0.5,
        "ssd_w_in": nrm(ks[8], (nB, D_MODEL, SSD_IN_W), jnp.float32) * D_MODEL ** -0.5,
        "ssd_conv_w": nrm(ks[9], (nB, CONV_K, SSD_CONV_C), jnp.float32) * 0.5,
        "ssd_conv_b": 0.01 * nrm(ks[10], (nB, SSD_CONV_C), jnp.float32),
        "ssd_dt_bias": jnp.stack([_dt_bias_init(kk, SSD_H) for kk in jax.random.split(ks[11], nB)]),
        "ssd_a_log": jnp.log(jax.random.uniform(ks[12], (nB, SSD_H), minval=1.0, maxval=16.0)),
        "ssd_d": 1.0 + 0.01 * nrm(ks[13], (nB, SSD_H), jnp.float32),
        "ssd_norm_w": 1.0 + 0.01 * nrm(ks[14], (nB, SSD_D_INNER), jnp.float32),
        "ssd_w_out": nrm(ks[15], (nB, SSD_D_INNER, D_MODEL), jnp.float32) * SSD_D_INNER ** -0.5,
        "final_norm_w": 1.0 + 0.01 * nrm(ks[16], (D_MODEL,), jnp.float32),
    }


def _fwd_reference(x, norm_w, gdn_w_in, gdn_conv_w, gdn_a_log, gdn_dt_bias, gdn_norm_w, gdn_w_out,
              ssd_w_in, ssd_conv_w, ssd_conv_b, ssd_dt_bias, ssd_a_log, ssd_d, ssd_norm_w,
              ssd_w_out, final_norm_w):
    for i in range(DEPTH):
        hid = _rmsnorm(x, norm_w[i])
        j = i // N_MIXERS
        if i % N_MIXERS == 0:
            x = x + _gated_deltanet(hid, gdn_w_in[j], gdn_conv_w[j], gdn_a_log[j],
                                    gdn_dt_bias[j], gdn_norm_w[j], gdn_w_out[j])
        else:
            x = x + _mamba2(hid, ssd_w_in[j], ssd_conv_w[j], ssd_conv_b[j], ssd_dt_bias[j],
                            ssd_a_log[j], ssd_d[j], ssd_norm_w[j], ssd_w_out[j])
    return _rmsnorm(x, final_norm_w)


import jax as _jax
import jax.numpy as _jnp

TWIN_FORMAT = 'train_step'
FWD_PARAMS = ['x', 'norm_w', 'gdn_w_in', 'gdn_conv_w', 'gdn_a_log', 'gdn_dt_bias', 'gdn_norm_w', 'gdn_w_out', 'ssd_w_in', 'ssd_conv_w', 'ssd_conv_b', 'ssd_dt_bias', 'ssd_a_log', 'ssd_d', 'ssd_norm_w', 'ssd_w_out', 'final_norm_w']
TWIN_WEIGHTS = ['norm_w', 'gdn_w_in', 'gdn_conv_w', 'gdn_a_log', 'gdn_dt_bias', 'gdn_norm_w', 'gdn_w_out', 'ssd_w_in', 'ssd_conv_w', 'ssd_conv_b', 'ssd_dt_bias', 'ssd_a_log', 'ssd_d', 'ssd_norm_w', 'ssd_w_out', 'final_norm_w']
TWIN_DIFF_INPUT = 'x'
TWIN_INPUTS = ['x', 'norm_w', 'gdn_w_in', 'gdn_conv_w', 'gdn_a_log', 'gdn_dt_bias', 'gdn_norm_w', 'gdn_w_out', 'ssd_w_in', 'ssd_conv_w', 'ssd_conv_b', 'ssd_dt_bias', 'ssd_a_log', 'ssd_d', 'ssd_norm_w', 'ssd_w_out', 'final_norm_w', 'loss_target', 'm_norm_w', 'm_gdn_w_in', 'm_gdn_conv_w', 'm_gdn_a_log', 'm_gdn_dt_bias', 'm_gdn_norm_w', 'm_gdn_w_out', 'm_ssd_w_in', 'm_ssd_conv_w', 'm_ssd_conv_b', 'm_ssd_dt_bias', 'm_ssd_a_log', 'm_ssd_d', 'm_ssd_norm_w', 'm_ssd_w_out', 'm_final_norm_w', 'v_norm_w', 'v_gdn_w_in', 'v_gdn_conv_w', 'v_gdn_a_log', 'v_gdn_dt_bias', 'v_gdn_norm_w', 'v_gdn_w_out', 'v_ssd_w_in', 'v_ssd_conv_w', 'v_ssd_conv_b', 'v_ssd_dt_bias', 'v_ssd_a_log', 'v_ssd_d', 'v_ssd_norm_w', 'v_ssd_w_out', 'v_final_norm_w']
TWIN_OUTPUTS = ['loss', 'grad_x', 'grad_norm_w', 'grad_gdn_w_in', 'grad_gdn_conv_w', 'grad_gdn_a_log', 'grad_gdn_dt_bias', 'grad_gdn_norm_w', 'grad_gdn_w_out', 'grad_ssd_w_in', 'grad_ssd_conv_w', 'grad_ssd_conv_b', 'grad_ssd_dt_bias', 'grad_ssd_a_log', 'grad_ssd_d', 'grad_ssd_norm_w', 'grad_ssd_w_out', 'grad_final_norm_w', 'delta_norm_w', 'delta_gdn_w_in', 'delta_gdn_conv_w', 'delta_gdn_a_log', 'delta_gdn_dt_bias', 'delta_gdn_norm_w', 'delta_gdn_w_out', 'delta_ssd_w_in', 'delta_ssd_conv_w', 'delta_ssd_conv_b', 'delta_ssd_dt_bias', 'delta_ssd_a_log', 'delta_ssd_d', 'delta_ssd_norm_w', 'delta_ssd_w_out', 'delta_final_norm_w', 'new_m_norm_w', 'new_m_gdn_w_in', 'new_m_gdn_conv_w', 'new_m_gdn_a_log', 'new_m_gdn_dt_bias', 'new_m_gdn_norm_w', 'new_m_gdn_w_out', 'new_m_ssd_w_in', 'new_m_ssd_conv_w', 'new_m_ssd_conv_b', 'new_m_ssd_dt_bias', 'new_m_ssd_a_log', 'new_m_ssd_d', 'new_m_ssd_norm_w', 'new_m_ssd_w_out', 'new_m_final_norm_w', 'new_v_norm_w', 'new_v_gdn_w_in', 'new_v_gdn_conv_w', 'new_v_gdn_a_log', 'new_v_gdn_dt_bias', 'new_v_gdn_norm_w', 'new_v_gdn_w_out', 'new_v_ssd_w_in', 'new_v_ssd_conv_w', 'new_v_ssd_conv_b', 'new_v_ssd_dt_bias', 'new_v_ssd_a_log', 'new_v_ssd_d', 'new_v_ssd_norm_w', 'new_v_ssd_w_out', 'new_v_final_norm_w']
TWIN_LEAF_KINDS = {'loss': 'loss', 'grad_x': 'grad_x', 'grad_norm_w': 'grad_w', 'grad_gdn_w_in': 'grad_w', 'grad_gdn_conv_w': 'grad_w', 'grad_gdn_a_log': 'grad_w', 'grad_gdn_dt_bias': 'grad_w', 'grad_gdn_norm_w': 'grad_w', 'grad_gdn_w_out': 'grad_w', 'grad_ssd_w_in': 'grad_w', 'grad_ssd_conv_w': 'grad_w', 'grad_ssd_conv_b': 'grad_w', 'grad_ssd_dt_bias': 'grad_w', 'grad_ssd_a_log': 'grad_w', 'grad_ssd_d': 'grad_w', 'grad_ssd_norm_w': 'grad_w', 'grad_ssd_w_out': 'grad_w', 'grad_final_norm_w': 'grad_w', 'delta_norm_w': 'delta_w', 'delta_gdn_w_in': 'delta_w', 'delta_gdn_conv_w': 'delta_w', 'delta_gdn_a_log': 'delta_w', 'delta_gdn_dt_bias': 'delta_w', 'delta_gdn_norm_w': 'delta_w', 'delta_gdn_w_out': 'delta_w', 'delta_ssd_w_in': 'delta_w', 'delta_ssd_conv_w': 'delta_w', 'delta_ssd_conv_b': 'delta_w', 'delta_ssd_dt_bias': 'delta_w', 'delta_ssd_a_log': 'delta_w', 'delta_ssd_d': 'delta_w', 'delta_ssd_norm_w': 'delta_w', 'delta_ssd_w_out': 'delta_w', 'delta_final_norm_w': 'delta_w', 'new_m_norm_w': 'new_m', 'new_m_gdn_w_in': 'new_m', 'new_m_gdn_conv_w': 'new_m', 'new_m_gdn_a_log': 'new_m', 'new_m_gdn_dt_bias': 'new_m', 'new_m_gdn_norm_w': 'new_m', 'new_m_gdn_w_out': 'new_m', 'new_m_ssd_w_in': 'new_m', 'new_m_ssd_conv_w': 'new_m', 'new_m_ssd_conv_b': 'new_m', 'new_m_ssd_dt_bias': 'new_m', 'new_m_ssd_a_log': 'new_m', 'new_m_ssd_d': 'new_m', 'new_m_ssd_norm_w': 'new_m', 'new_m_ssd_w_out': 'new_m', 'new_m_final_norm_w': 'new_m', 'new_v_norm_w': 'new_v', 'new_v_gdn_w_in': 'new_v', 'new_v_gdn_conv_w': 'new_v', 'new_v_gdn_a_log': 'new_v', 'new_v_gdn_dt_bias': 'new_v', 'new_v_gdn_norm_w': 'new_v', 'new_v_gdn_w_out': 'new_v', 'new_v_ssd_w_in': 'new_v', 'new_v_ssd_conv_w': 'new_v', 'new_v_ssd_conv_b': 'new_v', 'new_v_ssd_dt_bias': 'new_v', 'new_v_ssd_a_log': 'new_v', 'new_v_ssd_d': 'new_v', 'new_v_ssd_norm_w': 'new_v', 'new_v_ssd_w_out': 'new_v', 'new_v_final_norm_w': 'new_v'}


def _forward(args):
    return _fwd_reference(*[args[k] for k in FWD_PARAMS])


def _output_shape():
    def fwd():
        inp = _fwd_setup_inputs(0)
        return _fwd_reference(*[inp[k] for k in FWD_PARAMS])
    out = _jax.eval_shape(fwd)
    return out.shape, out.dtype

N_MICROBATCH = 1
ADAM_LR = 0.001
ADAM_B1 = 0.9
ADAM_B2 = 0.999
ADAM_EPS = 1e-08
ADAM_WD = 0.01
ADAM_STEP = 10
PER_EXAMPLE_BATCH_AXIS = {'x': 0, 'loss_target': 0}
SHARED_INPUTS = []
_WEIGHT_DTYPES = {'norm_w': _jnp.float32, 'gdn_w_in': _jnp.float32, 'gdn_conv_w': _jnp.float32, 'gdn_a_log': _jnp.float32, 'gdn_dt_bias': _jnp.float32, 'gdn_norm_w': _jnp.float32, 'gdn_w_out': _jnp.float32, 'ssd_w_in': _jnp.float32, 'ssd_conv_w': _jnp.float32, 'ssd_conv_b': _jnp.float32, 'ssd_dt_bias': _jnp.float32, 'ssd_a_log': _jnp.float32, 'ssd_d': _jnp.float32, 'ssd_norm_w': _jnp.float32, 'ssd_w_out': _jnp.float32, 'final_norm_w': _jnp.float32}
MOMENT_SCALE = {'norm_w': 3.733485e-01, 'gdn_w_in': 1.520189e-01, 'gdn_conv_w': 1.484077e-01, 'gdn_a_log': 1.605753e+00, 'gdn_dt_bias': 1.539277e+00, 'gdn_norm_w': 7.605583e-01, 'gdn_w_out': 2.260703e-01, 'ssd_w_in': 1.426327e-01, 'ssd_conv_w': 1.228580e-01, 'ssd_conv_b': 1.772236e-01, 'ssd_dt_bias': 3.051799e-01, 'ssd_a_log': 5.986243e-01, 'ssd_d': 7.893548e-01, 'ssd_norm_w': 1.679426e-01, 'ssd_w_out': 2.318793e-01, 'final_norm_w': 1.280421e+02}


def _to_microbatches(a, axis):
    t = _jnp.moveaxis(a, axis, 0)
    t = t.reshape((N_MICROBATCH, t.shape[0] // N_MICROBATCH) + t.shape[1:])
    return _jnp.moveaxis(t, 1, axis + 1)


def setup_inputs(seed: int = 0) -> dict:
    inp = _fwd_setup_inputs(seed)
    key = _jax.random.fold_in(_jax.random.key(seed), 7919)
    shape, _ = _output_shape()
    out = dict(inp)
    out["loss_target"] = _jax.random.normal(_jax.random.fold_in(key, 0), shape, _jnp.float32)
    for i, name in enumerate(TWIN_WEIGHTS):
        w = inp[name].astype(_jnp.float32)
        if MOMENT_SCALE is None:
            s = _jnp.sqrt(_jnp.mean(_jnp.square(w)) + 1e-30)
        else:
            s = MOMENT_SCALE[name]
        km, kv = _jax.random.split(_jax.random.fold_in(key, i + 1))
        out[name] = w
        out["m_" + name] = s * _jax.random.normal(km, w.shape, _jnp.float32)
        out["v_" + name] = (s * s) * _jax.random.uniform(kv, w.shape, _jnp.float32, 0.5, 1.5)
    if N_MICROBATCH > 1:
        for name, axis in PER_EXAMPLE_BATCH_AXIS.items():
            out[name] = _to_microbatches(out[name], axis)
    return {'x': out['x'], 'norm_w': out['norm_w'], 'gdn_w_in': out['gdn_w_in'], 'gdn_conv_w': out['gdn_conv_w'], 'gdn_a_log': out['gdn_a_log'], 'gdn_dt_bias': out['gdn_dt_bias'], 'gdn_norm_w': out['gdn_norm_w'], 'gdn_w_out': out['gdn_w_out'], 'ssd_w_in': out['ssd_w_in'], 'ssd_conv_w': out['ssd_conv_w'], 'ssd_conv_b': out['ssd_conv_b'], 'ssd_dt_bias': out['ssd_dt_bias'], 'ssd_a_log': out['ssd_a_log'], 'ssd_d': out['ssd_d'], 'ssd_norm_w': out['ssd_norm_w'], 'ssd_w_out': out['ssd_w_out'], 'final_norm_w': out['final_norm_w'], 'loss_target': out['loss_target'], 'm_norm_w': out['m_norm_w'], 'm_gdn_w_in': out['m_gdn_w_in'], 'm_gdn_conv_w': out['m_gdn_conv_w'], 'm_gdn_a_log': out['m_gdn_a_log'], 'm_gdn_dt_bias': out['m_gdn_dt_bias'], 'm_gdn_norm_w': out['m_gdn_norm_w'], 'm_gdn_w_out': out['m_gdn_w_out'], 'm_ssd_w_in': out['m_ssd_w_in'], 'm_ssd_conv_w': out['m_ssd_conv_w'], 'm_ssd_conv_b': out['m_ssd_conv_b'], 'm_ssd_dt_bias': out['m_ssd_dt_bias'], 'm_ssd_a_log': out['m_ssd_a_log'], 'm_ssd_d': out['m_ssd_d'], 'm_ssd_norm_w': out['m_ssd_norm_w'], 'm_ssd_w_out': out['m_ssd_w_out'], 'm_final_norm_w': out['m_final_norm_w'], 'v_norm_w': out['v_norm_w'], 'v_gdn_w_in': out['v_gdn_w_in'], 'v_gdn_conv_w': out['v_gdn_conv_w'], 'v_gdn_a_log': out['v_gdn_a_log'], 'v_gdn_dt_bias': out['v_gdn_dt_bias'], 'v_gdn_norm_w': out['v_gdn_norm_w'], 'v_gdn_w_out': out['v_gdn_w_out'], 'v_ssd_w_in': out['v_ssd_w_in'], 'v_ssd_conv_w': out['v_ssd_conv_w'], 'v_ssd_conv_b': out['v_ssd_conv_b'], 'v_ssd_dt_bias': out['v_ssd_dt_bias'], 'v_ssd_a_log': out['v_ssd_a_log'], 'v_ssd_d': out['v_ssd_d'], 'v_ssd_norm_w': out['v_ssd_norm_w'], 'v_ssd_w_out': out['v_ssd_w_out'], 'v_final_norm_w': out['v_final_norm_w']}


def _loss(weights, diff, rest, loss_target):
    with _jax.named_scope("forward"):
        args = {**rest, TWIN_DIFF_INPUT: diff, **{k: w.astype(_WEIGHT_DTYPES[k]) for k, w in weights.items()}}
        y = _forward(args)
    with _jax.named_scope("loss_head"):
        err = _jnp.square(y.astype(_jnp.float32) - loss_target)
        return 0.5 * _jnp.sum(_jnp.mean(err, axis=-1)) if err.ndim else 0.5 * err


def _adamw(w, g, m, v):
    m = ADAM_B1 * m + (1.0 - ADAM_B1) * g
    v = ADAM_B2 * v + (1.0 - ADAM_B2) * _jnp.square(g)
    m_hat = m / (1.0 - ADAM_B1 ** ADAM_STEP)
    v_hat = v / (1.0 - ADAM_B2 ** ADAM_STEP)
    delta = -ADAM_LR * (m_hat / (_jnp.sqrt(v_hat) + ADAM_EPS) + ADAM_WD * w)
    return delta, m, v


def reference(x, norm_w, gdn_w_in, gdn_conv_w, gdn_a_log, gdn_dt_bias, gdn_norm_w, gdn_w_out, ssd_w_in, ssd_conv_w, ssd_conv_b, ssd_dt_bias, ssd_a_log, ssd_d, ssd_norm_w, ssd_w_out, final_norm_w, loss_target, m_norm_w, m_gdn_w_in, m_gdn_conv_w, m_gdn_a_log, m_gdn_dt_bias, m_gdn_norm_w, m_gdn_w_out, m_ssd_w_in, m_ssd_conv_w, m_ssd_conv_b, m_ssd_dt_bias, m_ssd_a_log, m_ssd_d, m_ssd_norm_w, m_ssd_w_out, m_final_norm_w, v_norm_w, v_gdn_w_in, v_gdn_conv_w, v_gdn_a_log, v_gdn_dt_bias, v_gdn_norm_w, v_gdn_w_out, v_ssd_w_in, v_ssd_conv_w, v_ssd_conv_b, v_ssd_dt_bias, v_ssd_a_log, v_ssd_d, v_ssd_norm_w, v_ssd_w_out, v_final_norm_w):
    given = dict(x=x, norm_w=norm_w, gdn_w_in=gdn_w_in, gdn_conv_w=gdn_conv_w, gdn_a_log=gdn_a_log, gdn_dt_bias=gdn_dt_bias, gdn_norm_w=gdn_norm_w, gdn_w_out=gdn_w_out, ssd_w_in=ssd_w_in, ssd_conv_w=ssd_conv_w, ssd_conv_b=ssd_conv_b, ssd_dt_bias=ssd_dt_bias, ssd_a_log=ssd_a_log, ssd_d=ssd_d, ssd_norm_w=ssd_norm_w, ssd_w_out=ssd_w_out, final_norm_w=final_norm_w, loss_target=loss_target, m_norm_w=m_norm_w, m_gdn_w_in=m_gdn_w_in, m_gdn_conv_w=m_gdn_conv_w, m_gdn_a_log=m_gdn_a_log, m_gdn_dt_bias=m_gdn_dt_bias, m_gdn_norm_w=m_gdn_norm_w, m_gdn_w_out=m_gdn_w_out, m_ssd_w_in=m_ssd_w_in, m_ssd_conv_w=m_ssd_conv_w, m_ssd_conv_b=m_ssd_conv_b, m_ssd_dt_bias=m_ssd_dt_bias, m_ssd_a_log=m_ssd_a_log, m_ssd_d=m_ssd_d, m_ssd_norm_w=m_ssd_norm_w, m_ssd_w_out=m_ssd_w_out, m_final_norm_w=m_final_norm_w, v_norm_w=v_norm_w, v_gdn_w_in=v_gdn_w_in, v_gdn_conv_w=v_gdn_conv_w, v_gdn_a_log=v_gdn_a_log, v_gdn_dt_bias=v_gdn_dt_bias, v_gdn_norm_w=v_gdn_norm_w, v_gdn_w_out=v_gdn_w_out, v_ssd_w_in=v_ssd_w_in, v_ssd_conv_w=v_ssd_conv_w, v_ssd_conv_b=v_ssd_conv_b, v_ssd_dt_bias=v_ssd_dt_bias, v_ssd_a_log=v_ssd_a_log, v_ssd_d=v_ssd_d, v_ssd_norm_w=v_ssd_norm_w, v_ssd_w_out=v_ssd_w_out, v_final_norm_w=v_final_norm_w)
    weights = {n: given[n] for n in TWIN_WEIGHTS}
    shared = {n: given[n] for n in SHARED_INPUTS}
    per_example = {n: given[n] for n in ['x']}
    grad_fn = _jax.value_and_grad(_loss, argnums=(0, 1))

    def one_microbatch(ex, loss_target):
        ex = dict(ex)
        diff = ex.pop(TWIN_DIFF_INPUT)
        return grad_fn(weights, diff, {**shared, **ex}, loss_target)

    if N_MICROBATCH == 1:
        loss, (grad_w, grad_x) = one_microbatch(per_example, given["loss_target"])
    else:
        def body(carry, xs):
            loss_sum, grad_sum = carry
            l_k, (gw_k, gx_k) = one_microbatch(xs[0], xs[1])
            with _jax.named_scope("update"):
                return (loss_sum + l_k, _jax.tree.map(_jnp.add, grad_sum, gw_k)), gx_k

        init = (_jnp.zeros((), _jnp.float32), _jax.tree.map(_jnp.zeros_like, weights))
        (loss, grad_w), grad_x = _jax.lax.scan(body, init, (per_example, given["loss_target"]))
    with _jax.named_scope("update"):
        delta_w, new_m, new_v = {}, {}, {}
        for n in TWIN_WEIGHTS:
            delta_w[n], new_m[n], new_v[n] = _adamw(weights[n], grad_w[n], given["m_" + n], given["v_" + n])
    return (loss, grad_x, *[grad_w[n] for n in TWIN_WEIGHTS], *[delta_w[n] for n in TWIN_WEIGHTS],
            *[new_m[n] for n in TWIN_WEIGHTS], *[new_v[n] for n in TWIN_WEIGHTS])
```

```python
import jax
import jax.numpy as jnp
from jax import lax
from jax.experimental import pallas as pl
from jax.experimental.pallas import tpu as pltpu

F32 = jnp.float32
MXU_DTYPE = jnp.bfloat16
HI = lax.Precision.HIGHEST
EPS = 1e-6
VMEM_LIMIT_BYTES = 56 * 1024 * 1024
N_DEV = 8
MESH = pl.DeviceIdType.MESH

D_MODEL = 1024
CONV_K = 4
GDN_HV = 16
GDN_DK = 128
GDN_CHUNK = 64
SSD_H = 32
SSD_P = 64
SSD_N = 128
SSD_G = 8
SSD_R = SSD_H // SSD_G
SSD_CHUNK = 128
D_INNER = 2048
PAD_W = 128

ADAM_LR = 0.001
ADAM_B1 = 0.9
ADAM_B2 = 0.999
ADAM_EPS = 1e-08
ADAM_WD = 0.01
ADAM_STEP = 10


def _params(*sem):
    return pltpu.CompilerParams(dimension_semantics=sem, vmem_limit_bytes=VMEM_LIMIT_BYTES)


def _mx(a):
    return a.astype(MXU_DTYPE)


def _dot(a, b):
    return jnp.dot(_mx(a), _mx(b), preferred_element_type=F32)


def _dot_nt(a, b):
    return lax.dot_general(_mx(a), _mx(b), (((1,), (1,)), ((), ())), preferred_element_type=F32)


def _dot_tn(a, b):
    return lax.dot_general(_mx(a), _mx(b), (((0,), (0,)), ((), ())), preferred_element_type=F32)


def _dot_hi(a, b):
    return jnp.dot(a, b, precision=HI, preferred_element_type=F32)


def _sigmoid(x):
    return 1.0 / (1.0 + jnp.exp(-x))


def _silu(x):
    return x * _sigmoid(x)


def _dsilu(x):
    s = _sigmoid(x)
    return s * (1.0 + x * (1.0 - s))


def _softplus(x):
    return jnp.maximum(x, 0.0) + jnp.log1p(jnp.exp(-jnp.abs(x)))


def _col(r, eye):
    return jnp.sum(jnp.where(eye, r, 0.0), axis=1, keepdims=True)


def _row(c, eye):
    return jnp.sum(jnp.where(eye, c, 0.0), axis=0, keepdims=True)


def _masks(n):
    r = lax.broadcasted_iota(jnp.int32, (n, n), 0)
    c = lax.broadcasted_iota(jnp.int32, (n, n), 1)
    return r >= c, r > c, r == c, r, c


def _norm_inproj(x, nw, wparts, name):
    T = x.shape[0]
    tt = min(T, 256)
    n = len(wparts)

    def body(x_ref, nw_ref, *refs):
        w_refs, h_ref, o_refs = refs[:n], refs[n], refs[n + 1:]
        xv = x_ref[...]
        r = lax.rsqrt(jnp.mean(xv * xv, axis=-1, keepdims=True) + EPS)
        h = _mx(xv * r * nw_ref[...])
        h_ref[...] = h
        for w_ref, o_ref in zip(w_refs, o_refs):
            o_ref[...] = jnp.dot(h, w_ref[...], preferred_element_type=F32)

    row = lambda width: pl.BlockSpec((tt, width), lambda i: (i, 0))
    full = lambda a: pl.BlockSpec(a.shape, lambda i: (0, 0))
    outs = pl.pallas_call(
        body, grid=(T // tt,),
        in_specs=[row(D_MODEL), full(nw)] + [full(w) for w in wparts],
        out_specs=[row(D_MODEL)] + [row(w.shape[1]) for w in wparts],
        out_shape=[jax.ShapeDtypeStruct((T, D_MODEL), MXU_DTYPE)]
        + [jax.ShapeDtypeStruct((T, w.shape[1]), F32) for w in wparts],
        compiler_params=_params("parallel"), name=name,
    )(x, nw, *wparts)
    return outs[0], outs[1:]


def _inproj_bwd(x, nw, dparts, wparts, dres, name):
    T = x.shape[0]
    tt = min(T, 256)
    n = len(wparts)

    def body(x_ref, nw_ref, dres_ref, *refs):
        d_refs, w_refs, dx_ref, dnw_ref = refs[:n], refs[n:2 * n], refs[2 * n], refs[2 * n + 1]

        @pl.when(pl.program_id(0) == 0)
        def _():
            dnw_ref[...] = jnp.zeros_like(dnw_ref)

        dh = _dot_nt(d_refs[0][...], w_refs[0][...])
        for d_ref, w_ref in zip(d_refs[1:], w_refs[1:]):
            dh = dh + _dot_nt(d_ref[...], w_ref[...])
        xv = x_ref[...]
        r = lax.rsqrt(jnp.mean(xv * xv, axis=-1, keepdims=True) + EPS)
        xh = xv * r
        dnw_ref[...] += jnp.sum(dh * xh, axis=0, keepdims=True)
        dxn = dh * nw_ref[...]
        dx_ref[...] = dres_ref[...] + r * (dxn - xh * jnp.mean(dxn * xh, axis=-1, keepdims=True))

    row = lambda width: pl.BlockSpec((tt, width), lambda i: (i, 0))
    full = lambda a: pl.BlockSpec(a.shape, lambda i: (0, 0))
    return pl.pallas_call(
        body, grid=(T // tt,),
        in_specs=[row(D_MODEL), full(nw), row(D_MODEL)] + [row(d.shape[1]) for d in dparts]
        + [full(w) for w in wparts],
        out_specs=[row(D_MODEL), pl.BlockSpec((1, D_MODEL), lambda i: (0, 0))],
        out_shape=[jax.ShapeDtypeStruct((T, D_MODEL), F32), jax.ShapeDtypeStruct((1, D_MODEL), F32)],
        compiler_params=_params("arbitrary"), name=name,
    )(x, nw, dres, *dparts, *wparts)


def _matmul_tn(a, b, name):
    T, K = a.shape
    N = b.shape[1]
    tt = min(T, 512)
    tn = min(N, 1024)

    def body(a_ref, b_ref, o_ref):
        @pl.when(pl.program_id(1) == 0)
        def _():
            o_ref[...] = jnp.zeros_like(o_ref)

        o_ref[...] += _dot_tn(a_ref[...], b_ref[...])

    return pl.pallas_call(
        body, grid=(N // tn, T // tt),
        in_specs=[pl.BlockSpec((tt, K), lambda n, t: (t, 0)), pl.BlockSpec((tt, tn), lambda n, t: (t, n))],
        out_specs=pl.BlockSpec((K, tn), lambda n, t: (0, n)),
        out_shape=jax.ShapeDtypeStruct((K, N), F32),
        compiler_params=_params("parallel", "arbitrary"), name=name,
    )(a, b)


def _out_fwd(o, z, w, wout, xres, gs, gate_first, name):
    T = o.shape[0]
    tt = min(T, 256)
    wide = w.shape[1] == D_INNER

    def body(o_ref, z_ref, w_ref, wout_ref, x_ref, out_ref, yn):
        for g0 in range(0, D_INNER, gs):
            sl = slice(g0, g0 + gs)
            og, zg = o_ref[:, sl], z_ref[:, sl]
            wg = w_ref[:, sl] if wide else w_ref[...]
            if gate_first:
                u = og * _silu(zg)
                r = lax.rsqrt(jnp.mean(u * u, axis=-1, keepdims=True) + EPS)
                yn[:, sl] = _mx(u * r * wg)
            else:
                r = lax.rsqrt(jnp.mean(og * og, axis=-1, keepdims=True) + EPS)
                yn[:, sl] = _mx(og * r * wg * _silu(zg))
        out_ref[...] = x_ref[...] + jnp.dot(yn[...], wout_ref[...], preferred_element_type=F32)

    row = lambda width: pl.BlockSpec((tt, width), lambda i: (i, 0))
    full = lambda a: pl.BlockSpec(a.shape, lambda i: (0, 0))
    return pl.pallas_call(
        body, grid=(T // tt,),
        in_specs=[row(D_INNER), row(D_INNER), full(w), full(wout), row(D_MODEL)],
        out_specs=row(D_MODEL),
        out_shape=jax.ShapeDtypeStruct((T, D_MODEL), F32),
        scratch_shapes=[pltpu.VMEM((tt, D_INNER), MXU_DTYPE)],
        compiler_params=_params("parallel"), name=name,
    )(o, z, w, wout, xres)


def _out_bwd(dx, o, z, w, wout, gs, gate_first, name):
    T = o.shape[0]
    tt = min(T, 256)
    wide = w.shape[1] == D_INNER

    def body(dx_ref, o_ref, z_ref, w_ref, wout_ref, do_ref, dz_ref, dw_ref, yn_ref):
        @pl.when(pl.program_id(0) == 0)
        def _():
            dw_ref[...] = jnp.zeros_like(dw_ref)

        dyn = _dot_nt(dx_ref[...], wout_ref[...])
        dw_acc = jnp.zeros((1, gs), F32)
        for g0 in range(0, D_INNER, gs):
            sl = slice(g0, g0 + gs)
            og, zg, dg = o_ref[:, sl], z_ref[:, sl], dyn[:, sl]
            wg = w_ref[:, sl] if wide else w_ref[...]
            sz = _silu(zg)
            if gate_first:
                u = og * sz
                r = lax.rsqrt(jnp.mean(u * u, axis=-1, keepdims=True) + EPS)
                uh = u * r
                yn_ref[:, sl] = _mx(uh * wg)
                dw_g = jnp.sum(dg * uh, axis=0, keepdims=True)
                duh = dg * wg
                du = r * (duh - uh * jnp.mean(duh * uh, axis=-1, keepdims=True))
                do_ref[:, sl] = du * sz
                dz_ref[:, sl] = du * og * _dsilu(zg)
            else:
                r = lax.rsqrt(jnp.mean(og * og, axis=-1, keepdims=True) + EPS)
                oh = og * r
                yn_ref[:, sl] = _mx(oh * wg * sz)
                dw_g = jnp.sum(dg * oh * sz, axis=0, keepdims=True)
                doh = dg * wg * sz
                dz_ref[:, sl] = dg * oh * wg * _dsilu(zg)
                do_ref[:, sl] = r * (doh - oh * jnp.mean(doh * oh, axis=-1, keepdims=True))
            if wide:
                dw_ref[:, sl] += dw_g
            else:
                dw_acc = dw_acc + dw_g
        if not wide:
            dw_ref[...] += dw_acc

    row = lambda width: pl.BlockSpec((tt, width), lambda i: (i, 0))
    full = lambda a: pl.BlockSpec(a.shape, lambda i: (0, 0))
    return pl.pallas_call(
        body, grid=(T // tt,),
        in_specs=[row(D_MODEL), row(D_INNER), row(D_INNER), full(w), full(wout)],
        out_specs=[row(D_INNER), row(D_INNER), full(w), row(D_INNER)],
        out_shape=[jax.ShapeDtypeStruct((T, D_INNER), F32), jax.ShapeDtypeStruct((T, D_INNER), F32),
                   jax.ShapeDtypeStruct(w.shape, F32), jax.ShapeDtypeStruct((T, D_INNER), MXU_DTYPE)],
        compiler_params=_params("arbitrary"), name=name,
    )(dx, o, z, w, wout)


HALO = 8


def _conv_fwd(pre, w, b, l2, scale, name):
    T, C = pre.shape
    tt = min(T, 512)
    tc = min(C, 1024)

    def body(pre_ref, halo_ref, w_ref, b_ref, out_ref, P):
        i = pl.program_id(0)
        P[0:HALO, :] = jnp.where(i > 0, halo_ref[...], 0.0)
        P[HALO:HALO + tt, :] = pre_ref[...]
        acc = b_ref[...] + w_ref[0:1, :] * P[pl.ds(HALO - 3, tt), :]
        for j in range(1, CONV_K):
            acc = acc + w_ref[j:j + 1, :] * P[pl.ds(HALO - 3 + j, tt), :]
        s = _silu(acc)
        if l2:
            for g0 in range(0, tc, GDN_DK):
                sg = s[:, g0:g0 + GDN_DK]
                rr = lax.rsqrt(jnp.sum(sg * sg, axis=-1, keepdims=True) + EPS)
                out_ref[:, g0:g0 + GDN_DK] = sg * rr * scale
        else:
            out_ref[...] = s

    return pl.pallas_call(
        body, grid=(T // tt, C // tc),
        in_specs=[pl.BlockSpec((tt, tc), lambda i, j: (i, j)),
                  pl.BlockSpec((HALO, tc), lambda i, j: (jnp.maximum(i * (tt // HALO) - 1, 0), j)),
                  pl.BlockSpec((CONV_K, tc), lambda i, j: (0, j)),
                  pl.BlockSpec((1, tc), lambda i, j: (0, j))],
        out_specs=pl.BlockSpec((tt, tc), lambda i, j: (i, j)),
        out_shape=jax.ShapeDtypeStruct((T, C), F32),
        scratch_shapes=[pltpu.VMEM((HALO + tt, tc), F32)],
        compiler_params=_params("parallel", "parallel"), name=name,
    )(pre, pre, w, b)


def _conv_bwd(pre, w, b, dpost, l2, scale, name):
    T, C = pre.shape
    tt = min(T, 512)
    tc = min(C, 1024)
    nT = T // tt
    ext = tt + HALO

    def body(pre_ref, hp_ref, hn_ref, dpost_ref, dn_ref, w_ref, b_ref, dpre_ref, dw_ref, db_ref, P, Q):
        i = pl.program_id(1)

        @pl.when(i == 0)
        def _():
            dw_ref[...] = jnp.zeros_like(dw_ref)
            db_ref[...] = jnp.zeros_like(db_ref)

        P[0:HALO, :] = jnp.where(i > 0, hp_ref[...], 0.0)
        P[HALO:HALO + tt, :] = pre_ref[...]
        P[HALO + tt:HALO + ext, :] = hn_ref[...]
        cpre = b_ref[...] + w_ref[0:1, :] * P[pl.ds(HALO - 3, ext), :]
        for j in range(1, CONV_K):
            cpre = cpre + w_ref[j:j + 1, :] * P[pl.ds(HALO - 3 + j, ext), :]
        dy = jnp.concatenate([dpost_ref[...], dn_ref[...]], axis=0)
        rows = lax.broadcasted_iota(jnp.int32, (ext, 1), 0)
        valid = jnp.logical_or(rows < tt, i < nT - 1)
        s = _silu(cpre)
        ds_c = _dsilu(cpre)
        if l2:
            for g0 in range(0, tc, GDN_DK):
                sl = slice(g0, g0 + GDN_DK)
                sg, dg = s[:, sl], dy[:, sl]
                rr = lax.rsqrt(jnp.sum(sg * sg, axis=-1, keepdims=True) + EPS)
                yh = sg * rr
                dsg = scale * rr * (dg - yh * jnp.sum(dg * yh, axis=-1, keepdims=True))
                Q[:, sl] = jnp.where(valid, dsg * ds_c[:, sl], 0.0)
        else:
            Q[...] = jnp.where(valid, dy * ds_c, 0.0)
        dpre = w_ref[0:1, :] * Q[pl.ds(3, tt), :]
        for j in range(1, CONV_K):
            dpre = dpre + w_ref[j:j + 1, :] * Q[pl.ds(3 - j, tt), :]
        dpre_ref[...] = dpre
        dyc = Q[0:tt, :]
        for j in range(CONV_K):
            dw_ref[j:j + 1, :] += jnp.sum(dyc * P[pl.ds(HALO - 3 + j, tt), :], axis=0, keepdims=True)
        db_ref[...] += jnp.sum(dyc, axis=0, keepdims=True)

    tile = pl.BlockSpec((tt, tc), lambda j, i: (i, j))
    prev = pl.BlockSpec((HALO, tc), lambda j, i: (jnp.maximum(i * (tt // HALO) - 1, 0), j))
    nxt = pl.BlockSpec((HALO, tc), lambda j, i: (jnp.minimum((i + 1) * (tt // HALO), T // HALO - 1), j))
    return pl.pallas_call(
        body, grid=(C // tc, nT),
        in_specs=[tile, prev, nxt, tile, nxt,
                  pl.BlockSpec((CONV_K, tc), lambda j, i: (0, j)), pl.BlockSpec((1, tc), lambda j, i: (0, j))],
        out_specs=[tile, pl.BlockSpec((CONV_K, tc), lambda j, i: (0, j)), pl.BlockSpec((1, tc), lambda j, i: (0, j))],
        out_shape=[jax.ShapeDtypeStruct((T, C), F32), jax.ShapeDtypeStruct((CONV_K, C), F32),
                   jax.ShapeDtypeStruct((1, C), F32)],
        scratch_shapes=[pltpu.VMEM((HALO + ext, tc), F32), pltpu.VMEM((ext, tc), F32)],
        compiler_params=_params("parallel", "arbitrary"), name=name,
    )(pre, pre, pre, dpost, dpost, w, b)


def _inv_unit_lower(nm, eye, n):
    x = jnp.where(eye, 1.0, 0.0) - nm
    p = nm
    k = 2
    while k < n:
        p = _dot_hi(p, p)
        x = x + _dot_hi(x, p)
        k *= 2
    return x


def _gdn_chunk_common(qc, kc, vc, gcr, br, kk, qk, tri, strict, eye, C):
    gcc = _col(gcr, eye)
    bc = _col(br, eye)
    glast = gcr[:, C - 1:C]
    lm = jnp.exp(jnp.where(tri, gcc - gcr, -1e30))
    nm = jnp.where(strict, kk * bc * lm, 0.0)
    tinv = _inv_unit_lower(nm, eye, C)
    e_c = jnp.exp(gcc)
    el_c = jnp.exp(glast - gcc)
    kb = kc * bc
    sol = _dot(tinv, jnp.concatenate([vc * bc, kb * e_c], axis=1))
    pm = jnp.where(tri, qk * lm, 0.0)
    return dict(gcc=gcc, bc=bc, glast=glast, lm=lm, nm=nm, tinv=tinv, e_c=e_c, el_c=el_c, kb=kb,
                u=sol[:, :GDN_DK], w=sol[:, GDN_DK:], pm=pm, qd=qc * e_c, kd=kc * el_c)


def _gdn_scan_fwd(q, k, v, araw, braw, alog, dtb, name):
    T = q.shape[0]
    C = GDN_CHUNK
    tt = min(T, 512)
    cpt, nC = tt // C, T // C

    def body(alog_ref, dtb_ref, q_ref, k_ref, v_ref, a_ref, b_ref, o_ref, sall_ref, g_ref, beta_ref, S, gc_s, bt_s):
        j, i = pl.program_id(0), pl.program_id(1)

        @pl.when(i == 0)
        def _():
            S[...] = jnp.zeros_like(S)

        tri, strict, eye, r_i, c_i = _masks(C)
        upper = jnp.where(r_i <= c_i, 1.0, 0.0)
        for hh in range(2):
            h = 2 * j + hh
            g = -jnp.exp(alog_ref[h]) * _softplus(a_ref[hh] + dtb_ref[h])
            bt = _sigmoid(b_ref[hh])
            g_ref[hh] = g
            beta_ref[hh] = bt
            gc_s[hh] = _dot_hi(g, upper)
            bt_s[hh] = bt

        def chunk(c, carry):
            rows = pl.ds(pl.multiple_of(c * C, C), C)
            qc, kc = q_ref[rows, :], k_ref[rows, :]
            kk, qk = _dot_nt(kc, kc), _dot_nt(qc, kc)
            for hh in range(2):
                vc = v_ref[rows, hh * GDN_DK:(hh + 1) * GDN_DK]
                t = _gdn_chunk_common(qc, kc, vc, gc_s[hh, pl.ds(c, 1), :], bt_s[hh, pl.ds(c, 1), :],
                                      kk, qk, tri, strict, eye, C)
                sv = S[hh]
                sall_ref[hh, c] = sv
                vn = t["u"] - _dot(t["w"], sv)
                o_ref[rows, hh * GDN_DK:(hh + 1) * GDN_DK] = _dot(t["qd"], sv) + _dot(t["pm"], vn)
                S[hh] = sv * jnp.exp(t["glast"]) + _dot_tn(t["kd"], vn)
            return carry

        lax.fori_loop(0, cpt, chunk, 0)

    smem = pl.BlockSpec(memory_space=pltpu.SMEM)
    rows_spec = pl.BlockSpec((2, cpt, C), lambda j, i: (j, i, 0))
    return pl.pallas_call(
        body, grid=(GDN_HV // 2, T // tt),
        in_specs=[smem, smem,
                  pl.BlockSpec((tt, GDN_DK), lambda j, i: (i, j)), pl.BlockSpec((tt, GDN_DK), lambda j, i: (i, j)),
                  pl.BlockSpec((tt, 2 * GDN_DK), lambda j, i: (i, j)), rows_spec, rows_spec],
        out_specs=[pl.BlockSpec((tt, 2 * GDN_DK), lambda j, i: (i, j)),
                   pl.BlockSpec((2, cpt, GDN_DK, GDN_DK), lambda j, i: (j, i, 0, 0)), rows_spec, rows_spec],
        out_shape=[jax.ShapeDtypeStruct((T, D_INNER), F32),
                   jax.ShapeDtypeStruct((GDN_HV, nC, GDN_DK, GDN_DK), F32),
                   jax.ShapeDtypeStruct((GDN_HV, nC, C), F32), jax.ShapeDtypeStruct((GDN_HV, nC, C), F32)],
        scratch_shapes=[pltpu.VMEM((2, GDN_DK, GDN_DK), F32), pltpu.VMEM((2, cpt, C), F32),
                        pltpu.VMEM((2, cpt, C), F32)],
        compiler_params=_params("parallel", "arbitrary"), name=name,
    )(alog, dtb, q, k, v, araw, braw)


def _gdn_scan_bwd(q, k, v, g, beta, sall, do, name):
    T = q.shape[0]
    C = GDN_CHUNK
    tt = min(T, 512)
    cpt, nC, nT = tt // C, T // C, T // tt

    def body(q_ref, k_ref, v_ref, g_ref, b_ref, sall_ref, do_ref, dq_ref, dk_ref, dv_ref, dg_ref, dbeta_ref,
             dS, gc_s, dgc_s, dbt_s):
        i = pl.program_id(1)

        @pl.when(i == 0)
        def _():
            dS[...] = jnp.zeros_like(dS)

        tri, strict, eye, r_i, c_i = _masks(C)
        upper = jnp.where(r_i <= c_i, 1.0, 0.0)
        lower = jnp.where(r_i >= c_i, 1.0, 0.0)
        lane = lax.broadcasted_iota(jnp.int32, (1, C), 1)
        for hh in range(2):
            gc_s[hh] = _dot_hi(g_ref[hh], upper)

        def chunk(ci, carry):
            c = cpt - 1 - ci
            rows = pl.ds(pl.multiple_of(c * C, C), C)
            qc, kc = q_ref[rows, :], k_ref[rows, :]
            kk, qk = _dot_nt(kc, kc), _dot_nt(qc, kc)
            dq_acc = jnp.zeros((C, GDN_DK), F32)
            dk_acc = jnp.zeros((C, GDN_DK), F32)
            for hh in range(2):
                hs = slice(hh * GDN_DK, (hh + 1) * GDN_DK)
                vc, doc = v_ref[rows, hs], do_ref[rows, hs]
                t = _gdn_chunk_common(qc, kc, vc, gc_s[hh, pl.ds(c, 1), :], b_ref[hh, pl.ds(c, 1), :],
                                      kk, qk, tri, strict, eye, C)
                bc, lm, e_c, el_c, kb = t["bc"], t["lm"], t["e_c"], t["el_c"], t["kb"]
                sv = sall_ref[hh, c]
                dsn = dS[hh]
                gl = jnp.exp(t["glast"])
                vn = t["u"] - _dot(t["w"], sv)
                dqd = _dot_nt(doc, sv)
                dpm = jnp.where(tri, _dot_nt(doc, vn), 0.0)
                dvn = _dot_tn(t["pm"], doc) + _dot(t["kd"], dsn)
                dkd = _dot_nt(vn, dsn)
                dgl = jnp.sum(jnp.sum(dsn * sv, axis=1, keepdims=True), axis=0, keepdims=True)
                dS[hh] = gl * dsn + _dot_tn(t["qd"], doc) - _dot_tn(t["w"], dvn)
                dw = -_dot_nt(dvn, sv)
                drhs = _dot_tn(t["tinv"], jnp.concatenate([dvn, dw], axis=1))
                dvb, dkbe = drhs[:, :GDN_DK], drhs[:, GDN_DK:]
                dnm = -jnp.where(strict, _dot_nt(drhs, jnp.concatenate([t["u"], t["w"]], axis=1)), 0.0)
                dkb = dkbe * e_c
                de_c = jnp.sum(dkbe * kb, axis=1, keepdims=True) + jnp.sum(dqd * qc, axis=1, keepdims=True)
                del_c = jnp.sum(dkd * kc, axis=1, keepdims=True)
                dbc = (jnp.sum(dnm * kk * lm, axis=1, keepdims=True) + jnp.sum(dkb * kc, axis=1, keepdims=True)
                       + jnp.sum(dvb * vc, axis=1, keepdims=True))
                dkk = dnm * bc * lm
                dqk = dpm * lm
                dq_acc = dq_acc + _dot(dqk, kc) + dqd * e_c
                dk_acc = (dk_acc + _dot(dkk, kc) + _dot_tn(dkk, kc) + _dot_tn(dqk, qc)
                          + dkd * el_c + dkb * bc)
                dv_ref[rows, hs] = dvb * bc
                gm = dnm * t["nm"] + dpm * t["pm"]
                dgc_col = jnp.sum(gm, axis=1, keepdims=True) + de_c * e_c - del_c * el_c
                dglast = jnp.sum(del_c * el_c, axis=0, keepdims=True) + dgl * gl
                dgc_row = (_row(dgc_col, eye) - jnp.sum(gm, axis=0, keepdims=True)
                           + jnp.where(lane == C - 1, dglast, 0.0))
                dgc_s[hh, pl.ds(c, 1), :] = dgc_row
                dbt_s[hh, pl.ds(c, 1), :] = _row(dbc, eye)
            dq_ref[rows, :] = dq_acc
            dk_ref[rows, :] = dk_acc
            return carry

        lax.fori_loop(0, cpt, chunk, 0)
        for hh in range(2):
            dg_ref[hh] = _dot_hi(dgc_s[hh], lower)
            dbeta_ref[hh] = dbt_s[hh]

    rev = lambda i: nT - 1 - i
    rows_spec = pl.BlockSpec((2, cpt, C), lambda j, i: (j, rev(i), 0))
    qk_spec = pl.BlockSpec((tt, GDN_DK), lambda j, i: (rev(i), j))
    v_spec = pl.BlockSpec((tt, 2 * GDN_DK), lambda j, i: (rev(i), j))
    return pl.pallas_call(
        body, grid=(GDN_HV // 2, nT),
        in_specs=[qk_spec, qk_spec, v_spec, rows_spec, rows_spec,
                  pl.BlockSpec((2, cpt, GDN_DK, GDN_DK), lambda j, i: (j, rev(i), 0, 0)), v_spec],
        out_specs=[qk_spec, qk_spec, v_spec, rows_spec, rows_spec],
        out_shape=[jax.ShapeDtypeStruct((T, GDN_HV // 2 * GDN_DK), F32),
                   jax.ShapeDtypeStruct((T, GDN_HV // 2 * GDN_DK), F32),
                   jax.ShapeDtypeStruct((T, D_INNER), F32),
                   jax.ShapeDtypeStruct((GDN_HV, nC, C), F32), jax.ShapeDtypeStruct((GDN_HV, nC, C), F32)],
        scratch_shapes=[pltpu.VMEM((2, GDN_DK, GDN_DK), F32), pltpu.VMEM((2, cpt, C), F32),
                        pltpu.VMEM((2, cpt, C), F32), pltpu.VMEM((2, cpt, C), F32)],
        compiler_params=_params("parallel", "arbitrary"), name=name,
    )(q, k, v, g, beta, sall, do)


def _gdn_gate_bwd(araw, braw, dg, dbeta, alog, dtb, name):
    H, T = araw.shape

    def body(a_ref, b_ref, dg_ref, dbt_ref, alog_ref, dtb_ref, da_ref, db_ref, dalog_ref, ddtb_ref):
        xa = a_ref[...] + dtb_ref[...]
        ea = jnp.exp(alog_ref[...])
        dgv = dg_ref[...]
        da = -dgv * ea * _sigmoid(xa)
        da_ref[...] = da
        dalog_ref[...] = jnp.sum(-dgv * ea * _softplus(xa), axis=1, keepdims=True)
        ddtb_ref[...] = jnp.sum(da, axis=1, keepdims=True)
        bt = _sigmoid(b_ref[...])
        db_ref[...] = dbt_ref[...] * bt * (1.0 - bt)

    return pl.pallas_call(
        body,
        out_shape=[jax.ShapeDtypeStruct((H, T), F32), jax.ShapeDtypeStruct((H, T), F32),
                   jax.ShapeDtypeStruct((H, 1), F32), jax.ShapeDtypeStruct((H, 1), F32)],
        compiler_params=pltpu.CompilerParams(vmem_limit_bytes=VMEM_LIMIT_BYTES), name=name,
    )(araw, braw, dg, dbeta, alog, dtb)


def _ssd_scan_fwd(xs, bm, cm, dtraw, alog, dtb, dskip, name):
    T = xs.shape[0]
    Q = SSD_CHUNK
    tt = min(T, 1024)
    cpt, nC = tt // Q, T // Q
    GW = SSD_R * SSD_P

    def body(alog_ref, dtb_ref, dsk_ref, xs_ref, b_ref, c_ref, dt_ref, y_ref, sall_ref, dto_ref, S, dt_s, acs_s):
        gi, i = pl.program_id(0), pl.program_id(1)

        @pl.when(i == 0)
        def _():
            S[...] = jnp.zeros_like(S)

        tri, _, eye, r_i, c_i = _masks(Q)
        upper = jnp.where(r_i <= c_i, 1.0, 0.0)
        for r in range(SSD_R):
            h = SSD_R * gi + r
            dt = _softplus(dt_ref[r] + dtb_ref[h])
            dto_ref[r] = dt
            dt_s[r] = dt
            acs_s[r] = _dot_hi(-jnp.exp(alog_ref[h]) * dt, upper)

        def chunk(c, carry):
            rows = pl.ds(pl.multiple_of(c * Q, Q), Q)
            bc_, cc_ = b_ref[rows, :], c_ref[rows, :]
            cb = _dot_nt(cc_, bc_)
            sall_ref[0, c] = S[...]
            for r in range(SSD_R):
                h = SSD_R * gi + r
                ps = slice(r * SSD_P, (r + 1) * SSD_P)
                xr = xs_ref[rows, ps]
                acr = acs_s[r, pl.ds(c, 1), :]
                dtc, acc = _col(dt_s[r, pl.ds(c, 1), :], eye), _col(acr, eye)
                alast = acr[:, Q - 1:Q]
                xd = xr * dtc
                mm = cb * jnp.exp(jnp.where(tri, acc - acr, -1e30))
                sr = S[:, ps]
                y_ref[rows, ps] = _dot(mm, xd) + _dot(cc_ * jnp.exp(acc), sr) + dsk_ref[h] * xr
                S[:, ps] = sr * jnp.exp(alast) + _dot_tn(bc_ * jnp.exp(alast - acc), xd)
            return carry

        lax.fori_loop(0, cpt, chunk, 0)

    smem = pl.BlockSpec(memory_space=pltpu.SMEM)
    rows_spec = pl.BlockSpec((SSD_R, cpt, Q), lambda g, i: (g, i, 0))
    return pl.pallas_call(
        body, grid=(SSD_G, T // tt),
        in_specs=[smem, smem, smem,
                  pl.BlockSpec((tt, GW), lambda g, i: (i, g)), pl.BlockSpec((tt, SSD_N), lambda g, i: (i, g)),
                  pl.BlockSpec((tt, SSD_N), lambda g, i: (i, g)), rows_spec],
        out_specs=[pl.BlockSpec((tt, GW), lambda g, i: (i, g)),
                   pl.BlockSpec((1, cpt, SSD_N, GW), lambda g, i: (g, i, 0, 0)), rows_spec],
        out_shape=[jax.ShapeDtypeStruct((T, D_INNER), F32), jax.ShapeDtypeStruct((SSD_G, nC, SSD_N, GW), F32),
                   jax.ShapeDtypeStruct((SSD_H, nC, Q), F32)],
        scratch_shapes=[pltpu.VMEM((SSD_N, GW), F32), pltpu.VMEM((SSD_R, cpt, Q), F32),
                        pltpu.VMEM((SSD_R, cpt, Q), F32)],
        compiler_params=_params("parallel", "arbitrary"), name=name,
    )(alog, dtb, dskip, xs, bm, cm, dtraw)


def _ssd_scan_bwd(xs, bm, cm, dt, sall, dy, alog, dskip, name):
    T = xs.shape[0]
    Q = SSD_CHUNK
    tt = min(T, 1024)
    cpt, nC, nT = tt // Q, T // Q, T // tt
    GW = SSD_R * SSD_P

    def body(alog_ref, dsk_ref, xs_ref, b_ref, c_ref, dt_ref, sall_ref, dy_ref,
             dxs_ref, db_ref, dc_ref, da_ref, ddt_ref, dd_ref, dS, acs_s, dacs_s, ddt_s, dd_s):
        gi, i = pl.program_id(0), pl.program_id(1)

        @pl.when(i == 0)
        def _():
            dS[...] = jnp.zeros_like(dS)

        tri, _, eye, r_i, c_i = _masks(Q)
        upper = jnp.where(r_i <= c_i, 1.0, 0.0)
        lower = jnp.where(r_i >= c_i, 1.0, 0.0)
        lane = lax.broadcasted_iota(jnp.int32, (1, Q), 1)
        for r in range(SSD_R):
            acs_s[r] = _dot_hi(-jnp.exp(alog_ref[SSD_R * gi + r]) * dt_ref[r], upper)

        def chunk(ci, carry):
            c = cpt - 1 - ci
            rows = pl.ds(pl.multiple_of(c * Q, Q), Q)
            bc_, cc_ = b_ref[rows, :], c_ref[rows, :]
            cb = _dot_nt(cc_, bc_)
            dcb = jnp.zeros((Q, Q), F32)
            db_acc = jnp.zeros((Q, SSD_N), F32)
            dc_acc = jnp.zeros((Q, SSD_N), F32)
            for r in range(SSD_R):
                h = SSD_R * gi + r
                ps = slice(r * SSD_P, (r + 1) * SSD_P)
                xr, dyr = xs_ref[rows, ps], dy_ref[rows, ps]
                acr = acs_s[r, pl.ds(c, 1), :]
                dtc, acc = _col(dt_ref[r, pl.ds(c, 1), :], eye), _col(acr, eye)
                alast = acr[:, Q - 1:Q]
                e_c, dl_c, gl = jnp.exp(acc), jnp.exp(alast - acc), jnp.exp(alast)
                xd = xr * dtc
                lm = jnp.exp(jnp.where(tri, acc - acr, -1e30))
                mm = cb * lm
                sr = sall_ref[0, c, :, ps]
                dsn = dS[:, ps]
                ce, bd = cc_ * e_c, bc_ * dl_c
                dmm = jnp.where(tri, _dot_nt(dyr, xd), 0.0)
                dxd = _dot_tn(mm, dyr) + _dot(bd, dsn)
                dce = _dot_nt(dyr, sr)
                dbd = _dot_nt(xd, dsn)
                dgl = jnp.sum(jnp.sum(dsn * sr, axis=1, keepdims=True), axis=0, keepdims=True)
                dS[:, ps] = gl * dsn + _dot_tn(ce, dyr)
                dc_acc = dc_acc + dce * e_c
                db_acc = db_acc + dbd * dl_c
                de = jnp.sum(dce * cc_, axis=1, keepdims=True)
                ddl = jnp.sum(dbd * bc_, axis=1, keepdims=True)
                dcb = dcb + dmm * lm
                gm = dmm * mm
                dacs_col = jnp.sum(gm, axis=1, keepdims=True) + de * e_c - ddl * dl_c
                dalast = jnp.sum(ddl * dl_c, axis=0, keepdims=True) + dgl * gl
                dacs_s[r, pl.ds(c, 1), :] = (_row(dacs_col, eye) - jnp.sum(gm, axis=0, keepdims=True)
                                             + jnp.where(lane == Q - 1, dalast, 0.0))
                ddt_s[r, pl.ds(c, 1), :] = _row(jnp.sum(dxd * xr, axis=1, keepdims=True), eye)
                dd_s[r, pl.ds(c, 1), :] = _row(jnp.sum(dyr * xr, axis=1, keepdims=True), eye)
                dxs_ref[rows, ps] = dxd * dtc + dsk_ref[h] * dyr
            dc_ref[rows, :] = dc_acc + _dot(dcb, bc_)
            db_ref[rows, :] = db_acc + _dot_tn(dcb, cc_)
            return carry

        lax.fori_loop(0, cpt, chunk, 0)
        for r in range(SSD_R):
            da_ref[r] = _dot_hi(dacs_s[r], lower)
            ddt_ref[r] = ddt_s[r]
            dd_ref[r] = dd_s[r]

    rev = lambda i: nT - 1 - i
    smem = pl.BlockSpec(memory_space=pltpu.SMEM)
    rows_spec = pl.BlockSpec((SSD_R, cpt, Q), lambda g, i: (g, rev(i), 0))
    x_spec = pl.BlockSpec((tt, GW), lambda g, i: (rev(i), g))
    n_spec = pl.BlockSpec((tt, SSD_N), lambda g, i: (rev(i), g))
    rows_shape = jax.ShapeDtypeStruct((SSD_H, nC, Q), F32)
    return pl.pallas_call(
        body, grid=(SSD_G, nT),
        in_specs=[smem, smem, x_spec, n_spec, n_spec, rows_spec,
                  pl.BlockSpec((1, cpt, SSD_N, GW), lambda g, i: (g, rev(i), 0, 0)), x_spec],
        out_specs=[x_spec, n_spec, n_spec, rows_spec, rows_spec, rows_spec],
        out_shape=[jax.ShapeDtypeStruct((T, D_INNER), F32), jax.ShapeDtypeStruct((T, SSD_G * SSD_N), F32),
                   jax.ShapeDtypeStruct((T, SSD_G * SSD_N), F32), rows_shape, rows_shape, rows_shape],
        scratch_shapes=[pltpu.VMEM((SSD_N, GW), F32)] + [pltpu.VMEM((SSD_R, cpt, Q), F32)] * 4,
        compiler_params=_params("parallel", "arbitrary"), name=name,
    )(alog, dskip, xs, bm, cm, dt, sall, dy)


def _ssd_gate_bwd(dtraw, dt, da, ddt_direct, ddrow, alog, dtb, name):
    H, T = dtraw.shape

    def body(raw_ref, dt_ref, da_ref, ddt_ref, dd_ref, alog_ref, dtb_ref, draw_ref, dalog_ref, ddtb_ref, dD_ref):
        a = -jnp.exp(alog_ref[...])
        dav = da_ref[...]
        ddt = ddt_ref[...] + dav * a
        draw = ddt * _sigmoid(raw_ref[...] + dtb_ref[...])
        draw_ref[...] = draw
        dalog_ref[...] = jnp.sum(dav * dt_ref[...], axis=1, keepdims=True) * a
        ddtb_ref[...] = jnp.sum(draw, axis=1, keepdims=True)
        dD_ref[...] = jnp.sum(dd_ref[...], axis=1, keepdims=True)

    return pl.pallas_call(
        body,
        out_shape=[jax.ShapeDtypeStruct((H, T), F32)] + [jax.ShapeDtypeStruct((H, 1), F32)] * 3,
        compiler_params=pltpu.CompilerParams(vmem_limit_bytes=VMEM_LIMIT_BYTES), name=name,
    )(dtraw, dt, da, ddt_direct, ddrow, alog, dtb)


def _final_loss(x, fw, tgt, name):
    T = x.shape[0]
    tt = min(T, 512)
    nT = T // tt

    def body(x_ref, w_ref, t_ref, dx_ref, dw_ref, loss_ref, acc):
        i = pl.program_id(0)

        @pl.when(i == 0)
        def _():
            dw_ref[...] = jnp.zeros_like(dw_ref)
            acc[...] = jnp.zeros_like(acc)

        xv = x_ref[...]
        r = lax.rsqrt(jnp.mean(xv * xv, axis=-1, keepdims=True) + EPS)
        xh = xv * r
        err = xh * w_ref[...] - t_ref[...]
        acc[...] += jnp.sum(err * err, axis=0, keepdims=True)
        dout = err * (1.0 / D_MODEL)
        dw_ref[...] += jnp.sum(dout * xh, axis=0, keepdims=True)
        dxn = dout * w_ref[...]
        dx_ref[...] = r * (dxn - xh * jnp.mean(dxn * xh, axis=-1, keepdims=True))

        @pl.when(i == nT - 1)
        def _():
            loss_ref[...] = (0.5 / D_MODEL) * jnp.sum(acc[...], axis=1, keepdims=True)

    row = pl.BlockSpec((tt, D_MODEL), lambda i: (i, 0))
    vec = pl.BlockSpec((1, D_MODEL), lambda i: (0, 0))
    return pl.pallas_call(
        body, grid=(nT,),
        in_specs=[row, vec, row],
        out_specs=[row, vec, pl.BlockSpec((1, 1), lambda i: (0, 0))],
        out_shape=[jax.ShapeDtypeStruct((T, D_MODEL), F32), jax.ShapeDtypeStruct((1, D_MODEL), F32),
                   jax.ShapeDtypeStruct((1, 1), F32)],
        scratch_shapes=[pltpu.VMEM((1, D_MODEL), F32)],
        compiler_params=_params("arbitrary"), name=name,
    )(x, fw, tgt)


def _adamw(parts, w, m, v, name):
    R, C = w.shape
    tr = 128 if R % 128 == 0 else R

    def body(p_ref, w_ref, m_ref, v_ref, g_ref, d_ref, nm_ref, nv_ref):
        g = p_ref[0]
        for s in range(1, N_DEV):
            g = g + p_ref[s]
        mn = ADAM_B1 * m_ref[...] + (1.0 - ADAM_B1) * g
        vn = ADAM_B2 * v_ref[...] + (1.0 - ADAM_B2) * (g * g)
        mh = mn / (1.0 - ADAM_B1 ** ADAM_STEP)
        vh = vn / (1.0 - ADAM_B2 ** ADAM_STEP)
        g_ref[...] = g
        d_ref[...] = -ADAM_LR * (mh / (jnp.sqrt(vh) + ADAM_EPS) + ADAM_WD * w_ref[...])
        nm_ref[...] = mn
        nv_ref[...] = vn

    blk = pl.BlockSpec((tr, C), lambda i: (i, 0))
    return pl.pallas_call(
        body, grid=(R // tr,),
        in_specs=[pl.BlockSpec((N_DEV, tr, C), lambda i: (0, i, 0)), blk, blk, blk],
        out_specs=[blk] * 4,
        out_shape=[jax.ShapeDtypeStruct((R, C), F32)] * 4,
        compiler_params=_params("parallel"), name=name,
    )(parts, w, m, v)


def _me():
    x, y, c = lax.axis_index("x"), lax.axis_index("y"), lax.axis_index("c")
    return x, y, c


def _peer(d):
    x, y, c = _me()
    px = 1 - x if d & 4 else x
    py = 1 - y if d & 2 else y
    pc = 1 - c if d & 1 else c
    return (px, py, pc), 4 * px + 2 * py + pc


def _exchange(arrs, bcast, name):
    n = len(arrs)

    def body(*refs):
        ins, outs = refs[:n], refs[n:2 * n]
        ssem, rsem, lsem = refs[2 * n:]
        x, y, c = _me()
        me = 4 * x + 2 * y + c

        def src(a, dest):
            return ins[a] if bcast[a] else ins[a].at[dest]

        local = [pltpu.make_async_copy(src(a, me), outs[a].at[me], lsem.at[a]) for a in range(n)]
        for cp in local:
            cp.start()
        sends, recvs = [], []
        for a in range(n):
            for d in range(1, N_DEV):
                peer, pid = _peer(d)
                sends.append(pltpu.make_async_remote_copy(
                    src_ref=src(a, pid), dst_ref=outs[a].at[me], send_sem=ssem.at[a, d - 1],
                    recv_sem=rsem.at[a, d - 1], device_id=peer, device_id_type=MESH))
                recvs.append(pltpu.make_async_remote_copy(
                    src_ref=src(a, pid), dst_ref=outs[a].at[pid], send_sem=ssem.at[a, d - 1],
                    recv_sem=rsem.at[a, d - 1], device_id=peer, device_id_type=MESH))
        for cp in sends:
            cp.start()
        for cp in recvs:
            cp.wait_recv()
        for cp in sends:
            cp.wait_send()
        for cp in local:
            cp.wait()

    def out_shape(a, is_b):
        return jax.ShapeDtypeStruct((N_DEV,) + (a.shape if is_b else a.shape[1:]), a.dtype)

    anyspec = pl.BlockSpec(memory_space=pl.ANY)
    return pl.pallas_call(
        body,
        in_specs=[anyspec] * n, out_specs=[anyspec] * n,
        out_shape=[out_shape(a, b) for a, b in zip(arrs, bcast)],
        scratch_shapes=[pltpu.SemaphoreType.DMA((n, N_DEV - 1)), pltpu.SemaphoreType.DMA((n, N_DEV - 1)),
                        pltpu.SemaphoreType.DMA((n,))],
        name=name,
    )(*arrs)


def _to_rows(cols, chunk):
    T, H = cols.shape
    return cols.T.reshape(H, T // chunk, chunk)


def _from_rows(rows):
    return rows.T


def _pad_cols(a, width):
    return jnp.pad(a, ((0, 0), (0, width - a.shape[1])))


def _local_step(x, tgt, p):
    T = x.shape[0]
    zb = lambda n: jnp.zeros((1, n), F32)
    gw = p["gdn_w_in"]
    g_wparts = [gw[:, 0:1024], gw[:, 1024:2048], gw[:, 2048:4096], gw[:, 4096:6144], _pad_cols(gw[:, 6144:6176], PAD_W)]
    nw0, nw1 = p["norm_w"][0:1], p["norm_w"][1:2]
    h0, (q_pre, k_pre, v_pre, z0, ab) = _norm_inproj(x, nw0, g_wparts, "gdn_inproj")
    gcw = p["gdn_conv_w"]
    cw_q, cw_k, cw_v = gcw[:, 0:1024], gcw[:, 1024:2048], gcw[:, 2048:4096]
    q = _conv_fwd(q_pre, cw_q, zb(1024), True, GDN_DK ** -0.5, "gdn_conv_q")
    k = _conv_fwd(k_pre, cw_k, zb(1024), True, 1.0, "gdn_conv_k")
    v = _conv_fwd(v_pre, cw_v, zb(2048), False, 1.0, "gdn_conv_v")
    braw = _to_rows(ab[:, 0:GDN_HV], GDN_CHUNK)
    araw = _to_rows(ab[:, GDN_HV:2 * GDN_HV], GDN_CHUNK)
    g_alog, g_dtb = p["gdn_a_log"].reshape(-1), p["gdn_dt_bias"].reshape(-1)
    o0, g_sall, g_rows, beta_rows = _gdn_scan_fwd(q, k, v, araw, braw, g_alog, g_dtb, "gdn_scan_fwd")
    x1 = _out_fwd(o0, z0, p["gdn_norm_w"], p["gdn_w_out"], x, GDN_DK, False, "gdn_out")
    sw = p["ssd_w_in"]
    s_wparts = [sw[:, 0:2048], sw[:, 2048:4096], sw[:, 4096:5120], sw[:, 5120:6144], _pad_cols(sw[:, 6144:6176], PAD_W)]
    h1, (z1, xs_pre, b_pre, c_pre, dtp) = _norm_inproj(x1, nw1, s_wparts, "ssd_inproj")
    scw, scb = p["ssd_conv_w"], p["ssd_conv_b"]
    xs = _conv_fwd(xs_pre, scw[:, 0:2048], scb[:, 0:2048], False, 1.0, "ssd_conv_x")
    bm = _conv_fwd(b_pre, scw[:, 2048:3072], scb[:, 2048:3072], False, 1.0, "ssd_conv_b")
    cm = _conv_fwd(c_pre, scw[:, 3072:4096], scb[:, 3072:4096], False, 1.0, "ssd_conv_c")
    dtraw = _to_rows(dtp[:, 0:SSD_H], SSD_CHUNK)
    s_alog, s_dtb, s_d = p["ssd_a_log"].reshape(-1), p["ssd_dt_bias"].reshape(-1), p["ssd_d"].reshape(-1)
    y1, s_sall, dt_rows = _ssd_scan_fwd(xs, bm, cm, dtraw, s_alog, s_dtb, s_d, "ssd_scan_fwd")
    x2 = _out_fwd(y1, z1, p["ssd_norm_w"], p["ssd_w_out"], x1, D_INNER // SSD_G, True, "ssd_out")
    dx2, d_fw, loss = _final_loss(x2, p["final_norm_w"].reshape(1, -1), tgt, "final_loss")
    dy1, dz1, d_snw, yn1 = _out_bwd(dx2, y1, z1, p["ssd_norm_w"], p["ssd_w_out"], D_INNER // SSD_G, True, "ssd_out_bwd")
    d_swout = _matmul_tn(yn1, dx2, "ssd_wout_grad")
    dxs, dbm, dcm, da_rows, ddt_rows, dd_rows = _ssd_scan_bwd(xs, bm, cm, dt_rows, s_sall, dy1, s_alog, s_d, "ssd_scan_bwd")
    col = lambda a: a.reshape(-1, 1)
    dtraw_g, d_salog, d_sdtb, d_sd = _ssd_gate_bwd(
        dtraw.reshape(SSD_H, T), dt_rows.reshape(SSD_H, T), da_rows.reshape(SSD_H, T),
        ddt_rows.reshape(SSD_H, T), dd_rows.reshape(SSD_H, T), col(s_alog), col(s_dtb), "ssd_gate_bwd")
    dxs_pre, dcw_x, dcb_x = _conv_bwd(xs_pre, scw[:, 0:2048], scb[:, 0:2048], dxs, False, 1.0, "ssd_conv_x_bwd")
    db_pre, dcw_b, dcb_b = _conv_bwd(b_pre, scw[:, 2048:3072], scb[:, 2048:3072], dbm, False, 1.0, "ssd_conv_b_bwd")
    dc_pre, dcw_c, dcb_c = _conv_bwd(c_pre, scw[:, 3072:4096], scb[:, 3072:4096], dcm, False, 1.0, "ssd_conv_c_bwd")
    ddtp = _pad_cols(_from_rows(dtraw_g), PAD_W)
    s_dparts = [dz1, dxs_pre, db_pre, dc_pre, ddtp]
    dx1, d_nw1 = _inproj_bwd(x1, nw1, s_dparts, s_wparts, dx2, "ssd_inproj_bwd")
    s_dw = [_matmul_tn(h1, d, "ssd_win_grad_%d" % n) for n, d in enumerate(s_dparts)]
    d_swin = jnp.concatenate(s_dw[:4] + [s_dw[4][:, 0:SSD_H]], axis=1)
    do0, dz0, d_gnw, yn0 = _out_bwd(dx1, o0, z0, p["gdn_norm_w"], p["gdn_w_out"], GDN_DK, False, "gdn_out_bwd")
    d_gwout = _matmul_tn(yn0, dx1, "gdn_wout_grad")
    dq, dk, dv, dg_rows, dbeta_rows = _gdn_scan_bwd(q, k, v, g_rows, beta_rows, g_sall, do0, "gdn_scan_bwd")
    da_g, db_g, d_galog, d_gdtb = _gdn_gate_bwd(
        araw.reshape(GDN_HV, T), braw.reshape(GDN_HV, T), dg_rows.reshape(GDN_HV, T),
        dbeta_rows.reshape(GDN_HV, T), col(g_alog), col(g_dtb), "gdn_gate_bwd")
    dq_pre, dcw_q, _ = _conv_bwd(q_pre, cw_q, zb(1024), dq, True, GDN_DK ** -0.5, "gdn_conv_q_bwd")
    dk_pre, dcw_k, _ = _conv_bwd(k_pre, cw_k, zb(1024), dk, True, 1.0, "gdn_conv_k_bwd")
    dv_pre, dcw_v, _ = _conv_bwd(v_pre, cw_v, zb(2048), dv, False, 1.0, "gdn_conv_v_bwd")
    dab = _pad_cols(jnp.concatenate([_from_rows(db_g), _from_rows(da_g)], axis=1), PAD_W)
    g_dparts = [dq_pre, dk_pre, dv_pre, dz0, dab]
    dx0, d_nw0 = _inproj_bwd(x, nw0, g_dparts, g_wparts, dx1, "gdn_inproj_bwd")
    g_dw = [_matmul_tn(h0, d, "gdn_win_grad_%d" % n) for n, d in enumerate(g_dparts)]
    d_gwin = jnp.concatenate(g_dw[:4] + [g_dw[4][:, 0:2 * GDN_HV]], axis=1)
    grads = {
        "norm_w": jnp.concatenate([d_nw0, d_nw1], axis=0),
        "gdn_w_in": d_gwin,
        "gdn_conv_w": jnp.concatenate([dcw_q, dcw_k, dcw_v], axis=1),
        "gdn_a_log": d_galog.reshape(1, -1),
        "gdn_dt_bias": d_gdtb.reshape(1, -1),
        "gdn_norm_w": d_gnw,
        "gdn_w_out": d_gwout,
        "ssd_w_in": d_swin,
        "ssd_conv_w": jnp.concatenate([dcw_x, dcw_b, dcw_c], axis=1),
        "ssd_conv_b": jnp.concatenate([dcb_x, dcb_b, dcb_c], axis=1),
        "ssd_dt_bias": d_sdtb.reshape(1, -1),
        "ssd_a_log": d_salog.reshape(1, -1),
        "ssd_d": d_sd.reshape(1, -1),
        "ssd_norm_w": d_snw,
        "ssd_w_out": d_swout,
        "final_norm_w": d_fw,
    }
    return loss, dx0, grads


WEIGHTS = ["norm_w", "gdn_w_in", "gdn_conv_w", "gdn_a_log", "gdn_dt_bias", "gdn_norm_w", "gdn_w_out", "ssd_w_in",
           "ssd_conv_w", "ssd_conv_b", "ssd_dt_bias", "ssd_a_log", "ssd_d", "ssd_norm_w", "ssd_w_out", "final_norm_w"]
COL_SHARDED = ["gdn_w_in", "ssd_w_in"]
ROW_SHARDED = ["gdn_w_out", "ssd_w_out"]
SMALL_SHARDED = ["gdn_conv_w", "ssd_conv_w", "ssd_conv_b", "ssd_norm_w"]
REPLICATED = ["norm_w", "gdn_a_log", "gdn_dt_bias", "gdn_norm_w", "ssd_dt_bias", "ssd_a_log", "ssd_d", "final_norm_w"]


def _pack(arrs):
    return jnp.concatenate([a.reshape(-1) for a in arrs]).reshape(1, -1)


def _unpack(flat, shapes):
    out, pos = [], 0
    for s in shapes:
        n = 1
        for dim in s:
            n *= dim
        out.append(flat[pos:pos + n].reshape(s))
        pos += n
    return out


def _cols_to_shards(full):
    R, C = full.shape
    return full.reshape(R, N_DEV, C // N_DEV).transpose(1, 0, 2)


def _shards_to_cols(shards):
    n, R, c = shards.shape
    return shards.transpose(1, 0, 2).reshape(R, n * c)


def kernel(x, norm_w, gdn_w_in, gdn_conv_w, gdn_a_log, gdn_dt_bias, gdn_norm_w, gdn_w_out, ssd_w_in, ssd_conv_w, ssd_conv_b, ssd_dt_bias, ssd_a_log, ssd_d, ssd_norm_w, ssd_w_out, final_norm_w, loss_target, m_norm_w, m_gdn_w_in, m_gdn_conv_w, m_gdn_a_log, m_gdn_dt_bias, m_gdn_norm_w, m_gdn_w_out, m_ssd_w_in, m_ssd_conv_w, m_ssd_conv_b, m_ssd_dt_bias, m_ssd_a_log, m_ssd_d, m_ssd_norm_w, m_ssd_w_out, m_final_norm_w, v_norm_w, v_gdn_w_in, v_gdn_conv_w, v_gdn_a_log, v_gdn_dt_bias, v_gdn_norm_w, v_gdn_w_out, v_ssd_w_in, v_ssd_conv_w, v_ssd_conv_b, v_ssd_dt_bias, v_ssd_a_log, v_ssd_d, v_ssd_norm_w, v_ssd_w_out, v_final_norm_w):
    w = dict(norm_w=norm_w, gdn_w_in=gdn_w_in[0], gdn_conv_w=gdn_conv_w[0], gdn_a_log=gdn_a_log,
             gdn_dt_bias=gdn_dt_bias, gdn_norm_w=gdn_norm_w, gdn_w_out=gdn_w_out[0], ssd_w_in=ssd_w_in[0],
             ssd_conv_w=ssd_conv_w[0], ssd_conv_b=ssd_conv_b, ssd_dt_bias=ssd_dt_bias, ssd_a_log=ssd_a_log,
             ssd_d=ssd_d, ssd_norm_w=ssd_norm_w, ssd_w_out=ssd_w_out[0], final_norm_w=final_norm_w.reshape(1, -1))
    m = dict(norm_w=m_norm_w, gdn_w_in=m_gdn_w_in[0], gdn_conv_w=m_gdn_conv_w[0], gdn_a_log=m_gdn_a_log,
             gdn_dt_bias=m_gdn_dt_bias, gdn_norm_w=m_gdn_norm_w, gdn_w_out=m_gdn_w_out[0], ssd_w_in=m_ssd_w_in[0],
             ssd_conv_w=m_ssd_conv_w[0], ssd_conv_b=m_ssd_conv_b, ssd_dt_bias=m_ssd_dt_bias, ssd_a_log=m_ssd_a_log,
             ssd_d=m_ssd_d, ssd_norm_w=m_ssd_norm_w, ssd_w_out=m_ssd_w_out[0], final_norm_w=m_final_norm_w.reshape(1, -1))
    v = dict(norm_w=v_norm_w, gdn_w_in=v_gdn_w_in[0], gdn_conv_w=v_gdn_conv_w[0], gdn_a_log=v_gdn_a_log,
             gdn_dt_bias=v_gdn_dt_bias, gdn_norm_w=v_gdn_norm_w, gdn_w_out=v_gdn_w_out[0], ssd_w_in=v_ssd_w_in[0],
             ssd_conv_w=v_ssd_conv_w[0], ssd_conv_b=v_ssd_conv_b, ssd_dt_bias=v_ssd_dt_bias, ssd_a_log=v_ssd_a_log,
             ssd_d=v_ssd_d, ssd_norm_w=v_ssd_norm_w, ssd_w_out=v_ssd_w_out[0], final_norm_w=v_final_norm_w.reshape(1, -1))
    out_shapes = {n: a.shape for n, a in zip(
        WEIGHTS, [norm_w, gdn_w_in, gdn_conv_w, gdn_a_log, gdn_dt_bias, gdn_norm_w, gdn_w_out, ssd_w_in, ssd_conv_w,
                  ssd_conv_b, ssd_dt_bias, ssd_a_log, ssd_d, ssd_norm_w, ssd_w_out, final_norm_w])}

    small_shapes = [w[n].shape for n in SMALL_SHARDED]
    gathered = _exchange([_mx(w[n]) for n in COL_SHARDED + ROW_SHARDED] + [_pack([w[n] for n in SMALL_SHARDED])],
                         [True] * 5, "gather_weights")
    full = dict(w)
    for n, gth in zip(COL_SHARDED, gathered[0:2]):
        full[n] = _shards_to_cols(gth)
    for n, gth in zip(ROW_SHARDED, gathered[2:4]):
        full[n] = gth.reshape(-1, gth.shape[-1])
    small_all = [_unpack(gathered[4][s, 0], small_shapes) for s in range(N_DEV)]
    for idx, n in enumerate(SMALL_SHARDED):
        full[n] = jnp.concatenate([small_all[s][idx] for s in range(N_DEV)], axis=-1)

    loss, dx, grads = _local_step(x[0], loss_target[0], full)

    send_small = jnp.concatenate(
        [_cols_to_shards(grads[n]).reshape(N_DEV, -1) for n in SMALL_SHARDED], axis=1)[:, None, :]
    rep_shapes = [w[n].shape for n in REPLICATED]
    recv = _exchange(
        [_cols_to_shards(grads[n]) for n in COL_SHARDED]
        + [grads[n].reshape(N_DEV, -1, D_MODEL) for n in ROW_SHARDED]
        + [send_small, _pack([grads[n] for n in REPLICATED])],
        [False] * 5 + [True], "exchange_grads")

    res = {}
    for n, parts in zip(COL_SHARDED + ROW_SHARDED, recv[0:4]):
        res[n] = _adamw(parts, w[n], m[n], v[n], "adamw_" + n)
    small_res = _adamw(recv[4], *[_pack([t[n] for n in SMALL_SHARDED]) for t in (w, m, v)], "adamw_small")
    rep_res = _adamw(recv[5], *[_pack([t[n] for n in REPLICATED]) for t in (w, m, v)], "adamw_replicated")
    for k4 in range(4):
        for n, a in zip(SMALL_SHARDED, _unpack(small_res[k4][0], small_shapes)):
            res.setdefault(n, [None] * 4)[k4] = a
        for n, a in zip(REPLICATED, _unpack(rep_res[k4][0], rep_shapes)):
            res.setdefault(n, [None] * 4)[k4] = a

    loss = lax.psum(loss[0, 0], ("x", "y", "c"))
    outs = [loss, dx[None]]
    for k4 in range(4):
        outs += [res[n][k4].reshape(out_shapes[n]) for n in WEIGHTS]
    return tuple(outs)
```

```python
import jax
import jax.numpy as jnp
from jax import lax
from jax.experimental import pallas as pl
from jax.experimental.pallas import tpu as pltpu

F32 = jnp.float32
MXU_DTYPE = jnp.bfloat16
HI = lax.Precision.HIGHEST
EPS = 1e-6
VMEM_LIMIT_BYTES = 56 * 1024 * 1024
N_DEV = 8
MESH = pl.DeviceIdType.MESH

D_MODEL = 1024
CONV_K = 4
GDN_HV = 16
GDN_DK = 128
GDN_CHUNK = 64
SSD_H = 32
SSD_P = 64
SSD_N = 128
SSD_G = 8
SSD_R = SSD_H // SSD_G
SSD_CHUNK = 128
D_INNER = 2048
PAD_W = 128

ADAM_LR = 0.001
ADAM_B1 = 0.9
ADAM_B2 = 0.999
ADAM_EPS = 1e-08
ADAM_WD = 0.01
ADAM_STEP = 10


def _params(*sem):
    return pltpu.CompilerParams(dimension_semantics=sem, vmem_limit_bytes=VMEM_LIMIT_BYTES)


def _mx(a):
    return a.astype(MXU_DTYPE)


def _dot(a, b):
    return jnp.dot(_mx(a), _mx(b), preferred_element_type=F32)


def _dot_nt(a, b):
    return lax.dot_general(_mx(a), _mx(b), (((1,), (1,)), ((), ())), preferred_element_type=F32)


def _dot_tn(a, b):
    return lax.dot_general(_mx(a), _mx(b), (((0,), (0,)), ((), ())), preferred_element_type=F32)


def _dot_hi(a, b):
    return jnp.dot(a, b, precision=HI, preferred_element_type=F32)


def _sigmoid(x):
    return 1.0 / (1.0 + jnp.exp(-x))


def _silu(x):
    return x * _sigmoid(x)


def _dsilu(x):
    s = _sigmoid(x)
    return s * (1.0 + x * (1.0 - s))


def _softplus(x):
    return jnp.maximum(x, 0.0) + jnp.log1p(jnp.exp(-jnp.abs(x)))


def _col(r, eye):
    return jnp.sum(jnp.where(eye, r, 0.0), axis=1, keepdims=True)


def _row(c, eye):
    return jnp.sum(jnp.where(eye, c, 0.0), axis=0, keepdims=True)


def _masks(n):
    r = lax.broadcasted_iota(jnp.int32, (n, n), 0)
    c = lax.broadcasted_iota(jnp.int32, (n, n), 1)
    return r >= c, r > c, r == c, r, c


def _norm_inproj(x, nw, wparts, name):
    T = x.shape[0]
    tt = min(T, 256)
    n = len(wparts)

    def body(x_ref, nw_ref, *refs):
        w_refs, h_ref, o_refs = refs[:n], refs[n], refs[n + 1:]
        xv = x_ref[...]
        r = lax.rsqrt(jnp.mean(xv * xv, axis=-1, keepdims=True) + EPS)
        h = _mx(xv * r * nw_ref[...])
        h_ref[...] = h
        for w_ref, o_ref in zip(w_refs, o_refs):
            o_ref[...] = jnp.dot(h, w_ref[...], preferred_element_type=F32)

    row = lambda width: pl.BlockSpec((tt, width), lambda i: (i, 0))
    full = lambda a: pl.BlockSpec(a.shape, lambda i: (0, 0))
    outs = pl.pallas_call(
        body, grid=(T // tt,),
        in_specs=[row(D_MODEL), full(nw)] + [full(w) for w in wparts],
        out_specs=[row(D_MODEL)] + [row(w.shape[1]) for w in wparts],
        out_shape=[jax.ShapeDtypeStruct((T, D_MODEL), MXU_DTYPE)]
        + [jax.ShapeDtypeStruct((T, w.shape[1]), F32) for w in wparts],
        compiler_params=_params("parallel"), name=name,
    )(x, nw, *wparts)
    return outs[0], outs[1:]


def _inproj_bwd(x, nw, dparts, wparts, dres, name):
    T = x.shape[0]
    tt = min(T, 256)
    n = len(wparts)

    def body(x_ref, nw_ref, dres_ref, *refs):
        d_refs, w_refs, dx_ref, dnw_ref = refs[:n], refs[n:2 * n], refs[2 * n], refs[2 * n + 1]

        @pl.when(pl.program_id(0) == 0)
        def _():
            dnw_ref[...] = jnp.zeros_like(dnw_ref)

        dh = _dot_nt(d_refs[0][...], w_refs[0][...])
        for d_ref, w_ref in zip(d_refs[1:], w_refs[1:]):
            dh = dh + _dot_nt(d_ref[...], w_ref[...])
        xv = x_ref[...]
        r = lax.rsqrt(jnp.mean(xv * xv, axis=-1, keepdims=True) + EPS)
        xh = xv * r
        dnw_ref[...] += jnp.sum(dh * xh, axis=0, keepdims=True)
        dxn = dh * nw_ref[...]
        dx_ref[...] = dres_ref[...] + r * (dxn - xh * jnp.mean(dxn * xh, axis=-1, keepdims=True))

    row = lambda width: pl.BlockSpec((tt, width), lambda i: (i, 0))
    full = lambda a: pl.BlockSpec(a.shape, lambda i: (0, 0))
    return pl.pallas_call(
        body, grid=(T // tt,),
        in_specs=[row(D_MODEL), full(nw), row(D_MODEL)] + [row(d.shape[1]) for d in dparts]
        + [full(w) for w in wparts],
        out_specs=[row(D_MODEL), pl.BlockSpec((1, D_MODEL), lambda i: (0, 0))],
        out_shape=[jax.ShapeDtypeStruct((T, D_MODEL), F32), jax.ShapeDtypeStruct((1, D_MODEL), F32)],
        compiler_params=_params("arbitrary"), name=name,
    )(x, nw, dres, *dparts, *wparts)


def _matmul_tn(a, b, name):
    T, K = a.shape
    N = b.shape[1]
    tt = min(T, 512)
    tn = min(N, 1024)

    def body(a_ref, b_ref, o_ref):
        @pl.when(pl.program_id(1) == 0)
        def _():
            o_ref[...] = jnp.zeros_like(o_ref)

        o_ref[...] += _dot_tn(a_ref[...], b_ref[...])

    return pl.pallas_call(
        body, grid=(N // tn, T // tt),
        in_specs=[pl.BlockSpec((tt, K), lambda n, t: (t, 0)), pl.BlockSpec((tt, tn), lambda n, t: (t, n))],
        out_specs=pl.BlockSpec((K, tn), lambda n, t: (0, n)),
        out_shape=jax.ShapeDtypeStruct((K, N), F32),
        compiler_params=_params("parallel", "arbitrary"), name=name,
    )(a, b)


def _out_fwd(o, z, w, wout, xres, gs, gate_first, name):
    T = o.shape[0]
    tt = min(T, 256)
    wide = w.shape[1] == D_INNER

    def body(o_ref, z_ref, w_ref, wout_ref, x_ref, out_ref, yn):
        for g0 in range(0, D_INNER, gs):
            sl = slice(g0, g0 + gs)
            og, zg = o_ref[:, sl], z_ref[:, sl]
            wg = w_ref[:, sl] if wide else w_ref[...]
            if gate_first:
                u = og * _silu(zg)
                r = lax.rsqrt(jnp.mean(u * u, axis=-1, keepdims=True) + EPS)
                yn[:, sl] = _mx(u * r * wg)
            else:
                r = lax.rsqrt(jnp.mean(og * og, axis=-1, keepdims=True) + EPS)
                yn[:, sl] = _mx(og * r * wg * _silu(zg))
        out_ref[...] = x_ref[...] + jnp.dot(yn[...], wout_ref[...], preferred_element_type=F32)

    row = lambda width: pl.BlockSpec((tt, width), lambda i: (i, 0))
    full = lambda a: pl.BlockSpec(a.shape, lambda i: (0, 0))
    return pl.pallas_call(
        body, grid=(T // tt,),
        in_specs=[row(D_INNER), row(D_INNER), full(w), full(wout), row(D_MODEL)],
        out_specs=row(D_MODEL),
        out_shape=jax.ShapeDtypeStruct((T, D_MODEL), F32),
        scratch_shapes=[pltpu.VMEM((tt, D_INNER), MXU_DTYPE)],
        compiler_params=_params("parallel"), name=name,
    )(o, z, w, wout, xres)


def _out_bwd(dx, o, z, w, wout, gs, gate_first, name):
    T = o.shape[0]
    tt = min(T, 256)
    wide = w.shape[1] == D_INNER

    def body(dx_ref, o_ref, z_ref, w_ref, wout_ref, do_ref, dz_ref, dw_ref, yn_ref):
        @pl.when(pl.program_id(0) == 0)
        def _():
            dw_ref[...] = jnp.zeros_like(dw_ref)

        dyn = _dot_nt(dx_ref[...], wout_ref[...])
        dw_acc = jnp.zeros((1, gs), F32)
        for g0 in range(0, D_INNER, gs):
            sl = slice(g0, g0 + gs)
            og, zg, dg = o_ref[:, sl], z_ref[:, sl], dyn[:, sl]
            wg = w_ref[:, sl] if wide else w_ref[...]
            sz = _silu(zg)
            if gate_first:
                u = og * sz
                r = lax.rsqrt(jnp.mean(u * u, axis=-1, keepdims=True) + EPS)
                uh = u * r
                yn_ref[:, sl] = _mx(uh * wg)
                dw_g = jnp.sum(dg * uh, axis=0, keepdims=True)
                duh = dg * wg
                du = r * (duh - uh * jnp.mean(duh * uh, axis=-1, keepdims=True))
                do_ref[:, sl] = du * sz
                dz_ref[:, sl] = du * og * _dsilu(zg)
            else:
                r = lax.rsqrt(jnp.mean(og * og, axis=-1, keepdims=True) + EPS)
                oh = og * r
                yn_ref[:, sl] = _mx(oh * wg * sz)
                dw_g = jnp.sum(dg * oh * sz, axis=0, keepdims=True)
                doh = dg * wg * sz
                dz_ref[:, sl] = dg * oh * wg * _dsilu(zg)
                do_ref[:, sl] = r * (doh - oh * jnp.mean(doh * oh, axis=-1, keepdims=True))
            if wide:
                dw_ref[:, sl] += dw_g
            else:
                dw_acc = dw_acc + dw_g
        if not wide:
            dw_ref[...] += dw_acc

    row = lambda width: pl.BlockSpec((tt, width), lambda i: (i, 0))
    full = lambda a: pl.BlockSpec(a.shape, lambda i: (0, 0))
    return pl.pallas_call(
        body, grid=(T // tt,),
        in_specs=[row(D_MODEL), row(D_INNER), row(D_INNER), full(w), full(wout)],
        out_specs=[row(D_INNER), row(D_INNER), full(w), row(D_INNER)],
        out_shape=[jax.ShapeDtypeStruct((T, D_INNER), F32), jax.ShapeDtypeStruct((T, D_INNER), F32),
                   jax.ShapeDtypeStruct(w.shape, F32), jax.ShapeDtypeStruct((T, D_INNER), MXU_DTYPE)],
        compiler_params=_params("arbitrary"), name=name,
    )(dx, o, z, w, wout)


HALO = 8


def _conv_fwd(pre, w, b, l2, scale, name):
    T, C = pre.shape
    tt = min(T, 512)
    tc = min(C, 1024)

    def body(pre_ref, halo_ref, w_ref, b_ref, out_ref, P):
        i = pl.program_id(0)
        P[0:HALO, :] = jnp.where(i > 0, halo_ref[...], 0.0)
        P[HALO:HALO + tt, :] = pre_ref[...]
        acc = b_ref[...] + w_ref[0:1, :] * P[pl.ds(HALO - 3, tt), :]
        for j in range(1, CONV_K):
            acc = acc + w_ref[j:j + 1, :] * P[pl.ds(HALO - 3 + j, tt), :]
        s = _silu(acc)
        if l2:
            for g0 in range(0, tc, GDN_DK):
                sg = s[:, g0:g0 + GDN_DK]
                rr = lax.rsqrt(jnp.sum(sg * sg, axis=-1, keepdims=True) + EPS)
                out_ref[:, g0:g0 + GDN_DK] = sg * rr * scale
        else:
            out_ref[...] = s

    return pl.pallas_call(
        body, grid=(T // tt, C // tc),
        in_specs=[pl.BlockSpec((tt, tc), lambda i, j: (i, j)),
                  pl.BlockSpec((HALO, tc), lambda i, j: (jnp.maximum(i * (tt // HALO) - 1, 0), j)),
                  pl.BlockSpec((CONV_K, tc), lambda i, j: (0, j)),
                  pl.BlockSpec((1, tc), lambda i, j: (0, j))],
        out_specs=pl.BlockSpec((tt, tc), lambda i, j: (i, j)),
        out_shape=jax.ShapeDtypeStruct((T, C), F32),
        scratch_shapes=[pltpu.VMEM((HALO + tt, tc), F32)],
        compiler_params=_params("parallel", "parallel"), name=name,
    )(pre, pre, w, b)


def _conv_bwd(pre, w, b, dpost, l2, scale, name):
    T, C = pre.shape
    tt = min(T, 512)
    tc = min(C, 1024)
    nT = T // tt
    ext = tt + HALO

    def body(pre_ref, hp_ref, hn_ref, dpost_ref, dn_ref, w_ref, b_ref, dpre_ref, dw_ref, db_ref, P, Q):
        i = pl.program_id(1)

        @pl.when(i == 0)
        def _():
            dw_ref[...] = jnp.zeros_like(dw_ref)
            db_ref[...] = jnp.zeros_like(db_ref)

        P[0:HALO, :] = jnp.where(i > 0, hp_ref[...], 0.0)
        P[HALO:HALO + tt, :] = pre_ref[...]
        P[HALO + tt:HALO + ext, :] = hn_ref[...]
        cpre = b_ref[...] + w_ref[0:1, :] * P[pl.ds(HALO - 3, ext), :]
        for j in range(1, CONV_K):
            cpre = cpre + w_ref[j:j + 1, :] * P[pl.ds(HALO - 3 + j, ext), :]
        dy = jnp.concatenate([dpost_ref[...], dn_ref[...]], axis=0)
        rows = lax.broadcasted_iota(jnp.int32, (ext, 1), 0)
        valid = jnp.logical_or(rows < tt, i < nT - 1)
        s = _silu(cpre)
        ds_c = _dsilu(cpre)
        if l2:
            for g0 in range(0, tc, GDN_DK):
                sl = slice(g0, g0 + GDN_DK)
                sg, dg = s[:, sl], dy[:, sl]
                rr = lax.rsqrt(jnp.sum(sg * sg, axis=-1, keepdims=True) + EPS)
                yh = sg * rr
                dsg = scale * rr * (dg - yh * jnp.sum(dg * yh, axis=-1, keepdims=True))
                Q[:, sl] = jnp.where(valid, dsg * ds_c[:, sl], 0.0)
        else:
            Q[...] = jnp.where(valid, dy * ds_c, 0.0)
        dpre = w_ref[0:1, :] * Q[pl.ds(3, tt), :]
        for j in range(1, CONV_K):
            dpre = dpre + w_ref[j:j + 1, :] * Q[pl.ds(3 - j, tt), :]
        dpre_ref[...] = dpre
        dyc = Q[0:tt, :]
        for j in range(CONV_K):
            dw_ref[j:j + 1, :] += jnp.sum(dyc * P[pl.ds(HALO - 3 + j, tt), :], axis=0, keepdims=True)
        db_ref[...] += jnp.sum(dyc, axis=0, keepdims=True)

    tile = pl.BlockSpec((tt, tc), lambda j, i: (i, j))
    prev = pl.BlockSpec((HALO, tc), lambda j, i: (jnp.maximum(i * (tt // HALO) - 1, 0), j))
    nxt = pl.BlockSpec((HALO, tc), lambda j, i: (jnp.minimum((i + 1) * (tt // HALO), T // HALO - 1), j))
    return pl.pallas_call(
        body, grid=(C // tc, nT),
        in_specs=[tile, prev, nxt, tile, nxt,
                  pl.BlockSpec((CONV_K, tc), lambda j, i: (0, j)), pl.BlockSpec((1, tc), lambda j, i: (0, j))],
        out_specs=[tile, pl.BlockSpec((CONV_K, tc), lambda j, i: (0, j)), pl.BlockSpec((1, tc), lambda j, i: (0, j))],
        out_shape=[jax.ShapeDtypeStruct((T, C), F32), jax.ShapeDtypeStruct((CONV_K, C), F32),
                   jax.ShapeDtypeStruct((1, C), F32)],
        scratch_shapes=[pltpu.VMEM((HALO + ext, tc), F32), pltpu.VMEM((ext, tc), F32)],
        compiler_params=_params("parallel", "arbitrary"), name=name,
    )(pre, pre, pre, dpost, dpost, w, b)


GDN_LOCKSTEP_CHUNKS = 4
GDN_SCAN_HEADS = 8


def _inv_unit_lower_many(nms, eye, n):
    xs = [jnp.where(eye, 1.0, 0.0) - nm for nm in nms]
    ps = list(nms)
    k = 2
    while k < n:
        ps = [_dot(p, p) for p in ps]
        xs = [x + _dot(x, p) for x, p in zip(xs, ps)]
        k *= 2
    return xs


def _gdn_prep(q, k, v, araw, braw, alog, dtb, name):
    T = q.shape[0]
    C = GDN_CHUNK
    tt = min(T, 512)
    cpt, nC = tt // C, T // C
    grp = min(cpt, GDN_LOCKSTEP_CHUNKS)

    def body(alog_ref, dtb_ref, q_ref, k_ref, v_ref, a_ref, b_ref,
             u_ref, w_ref, pm_ref, ti_ref, g_ref, beta_ref, gc_ref):
        j = pl.program_id(0)
        tri, strict, eye, r_i, c_i = _masks(C)
        upper = jnp.where(r_i <= c_i, 1.0, 0.0)
        gcs, bts = [], []
        for hh in range(2):
            h = 2 * j + hh
            g = -jnp.exp(alog_ref[h]) * _softplus(a_ref[hh] + dtb_ref[h])
            bt = _sigmoid(b_ref[hh])
            gc = _dot_hi(g, upper)
            g_ref[hh], beta_ref[hh], gc_ref[hh] = g, bt, gc
            gcs.append(gc)
            bts.append(bt)
        for c0 in range(0, cpt, grp):
            cs = list(range(c0, c0 + grp))
            inst = [(c, hh) for c in cs for hh in range(2)]
            rows = {c: slice(c * C, (c + 1) * C) for c in cs}
            qc = {c: q_ref[rows[c], :] for c in cs}
            kc = {c: k_ref[rows[c], :] for c in cs}
            kk = {c: _dot_nt(kc[c], kc[c]) for c in cs}
            qk = {c: _dot_nt(qc[c], kc[c]) for c in cs}
            gcr = [gcs[hh][c:c + 1, :] for c, hh in inst]
            gcc = [_col(r, eye) for r in gcr]
            bc = [_col(bts[hh][c:c + 1, :], eye) for c, hh in inst]
            lm = [jnp.exp(jnp.where(tri, cc - r, -1e30)) for cc, r in zip(gcc, gcr)]
            nm = [jnp.where(strict, kk[c] * b * l, 0.0) for (c, hh), b, l in zip(inst, bc, lm)]
            tinv = _inv_unit_lower_many(nm, eye, C)
            rhs = [jnp.concatenate([v_ref[rows[c], hh * GDN_DK:(hh + 1) * GDN_DK] * b, kc[c] * (b * jnp.exp(cc))], axis=1)
                   for (c, hh), b, cc in zip(inst, bc, gcc)]
            sol = [_dot(t, r) for t, r in zip(tinv, rhs)]
            for (c, hh), s, t, l in zip(inst, sol, tinv, lm):
                hs = slice(hh * GDN_DK, (hh + 1) * GDN_DK)
                u_ref[rows[c], hs] = s[:, :GDN_DK]
                w_ref[rows[c], hs] = _mx(s[:, GDN_DK:])
                pm_ref[hh, c] = _mx(jnp.where(tri, qk[c] * l, 0.0))
                ti_ref[hh, c] = _mx(t)

    smem = pl.BlockSpec(memory_space=pltpu.SMEM)
    rows_spec = pl.BlockSpec((2, cpt, C), lambda j, i: (j, i, 0))
    qk_spec = pl.BlockSpec((tt, GDN_DK), lambda j, i: (i, j))
    v_spec = pl.BlockSpec((tt, 2 * GDN_DK), lambda j, i: (i, j))
    cc_spec = pl.BlockSpec((2, cpt, C, C), lambda j, i: (j, i, 0, 0))
    rows_shape = jax.ShapeDtypeStruct((GDN_HV, nC, C), F32)
    cc_shape = jax.ShapeDtypeStruct((GDN_HV, nC, C, C), MXU_DTYPE)
    return pl.pallas_call(
        body, grid=(GDN_HV // 2, T // tt),
        in_specs=[smem, smem, qk_spec, qk_spec, v_spec, rows_spec, rows_spec],
        out_specs=[v_spec, v_spec, cc_spec, cc_spec, rows_spec, rows_spec, rows_spec],
        out_shape=[jax.ShapeDtypeStruct((T, D_INNER), F32), jax.ShapeDtypeStruct((T, D_INNER), MXU_DTYPE),
                   cc_shape, cc_shape, rows_shape, rows_shape, rows_shape],
        compiler_params=_params("parallel", "parallel"), name=name,
    )(alog, dtb, q, k, v, araw, braw)


def _gdn_decays(gc_ref, h, c, eye, C):
    gcr = gc_ref[h, pl.ds(c, 1), :]
    gcc = _col(gcr, eye)
    glast = gcr[:, C - 1:C]
    return jnp.exp(gcc), jnp.exp(glast - gcc), jnp.exp(glast)


def _gdn_state_fwd(q, k, u, w, pm, gc, name):
    T = q.shape[0]
    C = GDN_CHUNK
    HG = GDN_SCAN_HEADS
    tt = min(T, 512)
    cpt, nC = tt // C, T // C

    def body(q_ref, k_ref, u_ref, w_ref, pm_ref, gc_ref, o_ref, vn_ref, sall_ref, S):
        @pl.when(pl.program_id(1) == 0)
        def _():
            S[...] = jnp.zeros_like(S)

        eye = _masks(C)[2]
        heads = list(range(HG))

        def chunk(c, carry):
            rows = pl.ds(pl.multiple_of(c * C, C), C)
            hs = [slice(h * GDN_DK, (h + 1) * GDN_DK) for h in heads]
            qs = [slice((h // 2) * GDN_DK, (h // 2 + 1) * GDN_DK) for h in heads]
            dec = [_gdn_decays(gc_ref, h, c, eye, C) for h in heads]
            sv = [S[h] for h in heads]
            for h in heads:
                sall_ref[h, c] = sv[h]
            ws = [_dot(w_ref[rows, hs[h]], sv[h]) for h in heads]
            qsv = [_dot(q_ref[rows, qs[h]], sv[h]) for h in heads]
            vn = [u_ref[rows, hs[h]] - ws[h] for h in heads]
            pv = [_dot(pm_ref[h, c], vn[h]) for h in heads]
            kv = [_dot_tn(k_ref[rows, qs[h]], vn[h] * dec[h][1]) for h in heads]
            for h in heads:
                vn_ref[rows, hs[h]] = _mx(vn[h])
                o_ref[rows, hs[h]] = qsv[h] * dec[h][0] + pv[h]
                S[h] = sv[h] * dec[h][2] + kv[h]
            return carry

        lax.fori_loop(0, cpt, chunk, 0)

    qk_spec = pl.BlockSpec((tt, HG // 2 * GDN_DK), lambda g, i: (i, g))
    v_spec = pl.BlockSpec((tt, HG * GDN_DK), lambda g, i: (i, g))
    return pl.pallas_call(
        body, grid=(GDN_HV // HG, T // tt),
        in_specs=[qk_spec, qk_spec, v_spec, v_spec,
                  pl.BlockSpec((HG, cpt, C, C), lambda g, i: (g, i, 0, 0)),
                  pl.BlockSpec((HG, cpt, C), lambda g, i: (g, i, 0))],
        out_specs=[v_spec, v_spec, pl.BlockSpec((HG, cpt, GDN_DK, GDN_DK), lambda g, i: (g, i, 0, 0))],
        out_shape=[jax.ShapeDtypeStruct((T, D_INNER), F32), jax.ShapeDtypeStruct((T, D_INNER), MXU_DTYPE),
                   jax.ShapeDtypeStruct((GDN_HV, nC, GDN_DK, GDN_DK), F32)],
        scratch_shapes=[pltpu.VMEM((HG, GDN_DK, GDN_DK), F32)],
        compiler_params=_params("parallel", "arbitrary"), name=name,
    )(q, k, u, w, pm, gc)


def _gdn_state_bwd(q, k, w, pm, vn, sall, gc, do, name):
    T = q.shape[0]
    C = GDN_CHUNK
    HG = GDN_SCAN_HEADS
    tt = min(T, 512)
    cpt, nC, nT = tt // C, T // C, T // tt

    def body(q_ref, k_ref, w_ref, pm_ref, vn_ref, sall_ref, gc_ref, do_ref, dvn_ref, dkd_ref, dgl_ref, dS):
        @pl.when(pl.program_id(1) == 0)
        def _():
            dS[...] = jnp.zeros_like(dS)

        eye = _masks(C)[2]
        heads = list(range(HG))

        def chunk(ci, carry):
            c = cpt - 1 - ci
            rows = pl.ds(pl.multiple_of(c * C, C), C)
            hs = [slice(h * GDN_DK, (h + 1) * GDN_DK) for h in heads]
            qs = [slice((h // 2) * GDN_DK, (h // 2 + 1) * GDN_DK) for h in heads]
            dec = [_gdn_decays(gc_ref, h, c, eye, C) for h in heads]
            dsn = [dS[h] for h in heads]
            doc = [do_ref[rows, hs[h]] for h in heads]
            kds = [_dot(k_ref[rows, qs[h]], dsn[h]) for h in heads]
            pdo = [_dot_tn(pm_ref[h, c], doc[h]) for h in heads]
            dkd = [_dot_nt(vn_ref[rows, hs[h]], dsn[h]) for h in heads]
            qdo = [_dot_tn(q_ref[rows, qs[h]], doc[h] * dec[h][0]) for h in heads]
            dvn = [pdo[h] + kds[h] * dec[h][1] for h in heads]
            wdv = [_dot_tn(w_ref[rows, hs[h]], dvn[h]) for h in heads]
            for h in heads:
                dgl = jnp.sum(jnp.sum(dsn[h] * sall_ref[h, c], axis=1, keepdims=True), axis=0, keepdims=True)
                dgl_ref[h, pl.ds(c, 1), :] = jnp.broadcast_to(dgl, (1, C))
                dvn_ref[rows, hs[h]] = dvn[h]
                dkd_ref[rows, hs[h]] = dkd[h]
                dS[h] = dsn[h] * dec[h][2] + qdo[h] - wdv[h]
            return carry

        lax.fori_loop(0, cpt, chunk, 0)

    rev = lambda i: nT - 1 - i
    qk_spec = pl.BlockSpec((tt, HG // 2 * GDN_DK), lambda g, i: (rev(i), g))
    v_spec = pl.BlockSpec((tt, HG * GDN_DK), lambda g, i: (rev(i), g))
    rows_spec = pl.BlockSpec((HG, cpt, C), lambda g, i: (g, rev(i), 0))
    return pl.pallas_call(
        body, grid=(GDN_HV // HG, nT),
        in_specs=[qk_spec, qk_spec, v_spec, pl.BlockSpec((HG, cpt, C, C), lambda g, i: (g, rev(i), 0, 0)), v_spec,
                  pl.BlockSpec((HG, cpt, GDN_DK, GDN_DK), lambda g, i: (g, rev(i), 0, 0)), rows_spec, v_spec],
        out_specs=[v_spec, v_spec, rows_spec],
        out_shape=[jax.ShapeDtypeStruct((T, D_INNER), F32), jax.ShapeDtypeStruct((T, D_INNER), F32),
                   jax.ShapeDtypeStruct((GDN_HV, nC, C), F32)],
        scratch_shapes=[pltpu.VMEM((HG, GDN_DK, GDN_DK), F32)],
        compiler_params=_params("parallel", "arbitrary"), name=name,
    )(q, k, w, pm, vn, sall, gc, do)


def _gdn_local_bwd(q, k, v, gc, beta, tinv, u, w, pm, vn, sall, do, dvn, dkd, dgl, name):
    T = q.shape[0]
    C = GDN_CHUNK
    tt = min(T, 512)
    cpt, nC = tt // C, T // C
    grp = min(cpt, GDN_LOCKSTEP_CHUNKS)

    def body(q_ref, k_ref, v_ref, gc_ref, b_ref, ti_ref, u_ref, w_ref, pm_ref, vn_ref, sall_ref, do_ref,
             dvn_ref, dkd_ref, dgl_ref, dq_ref, dk_ref, dv_ref, dg_ref, dbeta_ref, dgc_s):
        tri, strict, eye, r_i, c_i = _masks(C)
        lower = jnp.where(r_i >= c_i, 1.0, 0.0)
        lane = lax.broadcasted_iota(jnp.int32, (1, C), 1)
        rsum = lambda a: jnp.sum(a, axis=1, keepdims=True)
        for c0 in range(0, cpt, grp):
            cs = list(range(c0, c0 + grp))
            inst = [(c, hh) for c in cs for hh in range(2)]
            n = len(inst)
            rows = {c: slice(c * C, (c + 1) * C) for c in cs}
            hsl = [slice(hh * GDN_DK, (hh + 1) * GDN_DK) for c, hh in inst]
            qc = {c: q_ref[rows[c], :] for c in cs}
            kc = {c: k_ref[rows[c], :] for c in cs}
            kk = {c: _dot_nt(kc[c], kc[c]) for c in cs}
            gcr = [gc_ref[hh, c:c + 1, :] for c, hh in inst]
            gcc = [_col(r, eye) for r in gcr]
            bc = [_col(b_ref[hh, c:c + 1, :], eye) for c, hh in inst]
            lm = [jnp.exp(jnp.where(tri, cc - r, -1e30)) for cc, r in zip(gcc, gcr)]
            e_c = [jnp.exp(cc) for cc in gcc]
            el_c = [jnp.exp(r[:, C - 1:C] - cc) for cc, r in zip(gcc, gcr)]
            gl = [jnp.exp(r[:, C - 1:C]) for r in gcr]
            doc = [do_ref[rows[c], hsl[i]] for i, (c, hh) in enumerate(inst)]
            dvn = [dvn_ref[rows[c], hsl[i]] for i, (c, hh) in enumerate(inst)]
            sv = [sall_ref[hh, c] for c, hh in inst]
            aa = [_dot_nt(jnp.concatenate([_mx(doc[i]), _mx(dvn[i])], axis=0), sv[i]) for i in range(n)]
            dpm = [jnp.where(tri, _dot_nt(doc[i], vn_ref[rows[c], hsl[i]]), 0.0) for i, (c, hh) in enumerate(inst)]
            dqd = [a[:C] for a in aa]
            drhs = [_dot_tn(ti_ref[hh, c], jnp.concatenate([dvn[i], -aa[i][C:]], axis=1))
                    for i, (c, hh) in enumerate(inst)]
            sol = [jnp.concatenate([_mx(u_ref[rows[c], hsl[i]]), w_ref[rows[c], hsl[i]]], axis=1)
                   for i, (c, hh) in enumerate(inst)]
            dnm = [-jnp.where(strict, _dot_nt(drhs[i], sol[i]), 0.0) for i in range(n)]
            dkk = [dnm[i] * bc[i] * lm[i] for i in range(n)]
            dqk = [dpm[i] * lm[i] for i in range(n)]
            dq1 = [_dot(dqk[i], kc[c]) for i, (c, hh) in enumerate(inst)]
            dk1 = [_dot(dkk[i], kc[c]) for i, (c, hh) in enumerate(inst)]
            dk2 = [_dot_tn(dkk[i], kc[c]) for i, (c, hh) in enumerate(inst)]
            dk3 = [_dot_tn(dqk[i], qc[c]) for i, (c, hh) in enumerate(inst)]
            dq_acc = {c: jnp.zeros((C, GDN_DK), F32) for c in cs}
            dk_acc = {c: jnp.zeros((C, GDN_DK), F32) for c in cs}
            for i, (c, hh) in enumerate(inst):
                k_, q_, v_ = kc[c], qc[c], v_ref[rows[c], hsl[i]]
                dvb, dkbe = drhs[i][:, :GDN_DK], drhs[i][:, GDN_DK:]
                dkd = dkd_ref[rows[c], hsl[i]]
                kb = k_ * bc[i]
                dkb = dkbe * e_c[i]
                de_c = rsum(dkbe * kb) + rsum(dqd[i] * q_)
                del_c = rsum(dkd * k_)
                dbc = rsum(dnm[i] * kk[c] * lm[i]) + rsum(dkb * k_) + rsum(dvb * v_)
                dq_acc[c] = dq_acc[c] + dq1[i] + dqd[i] * e_c[i]
                dk_acc[c] = dk_acc[c] + dk1[i] + dk2[i] + dk3[i] + dkd * el_c[i] + dkb * bc[i]
                dv_ref[rows[c], hsl[i]] = dvb * bc[i]
                nm = jnp.where(strict, kk[c] * bc[i] * lm[i], 0.0)
                gm = dnm[i] * nm + dpm[i] * pm_ref[hh, c].astype(F32)
                dgc_col = rsum(gm) + de_c * e_c[i] - del_c * el_c[i]
                dglast = jnp.sum(del_c * el_c[i], axis=0, keepdims=True) + dgl_ref[hh, c:c + 1, 0:1] * gl[i]
                dgc_s[hh, c:c + 1, :] = (_row(dgc_col, eye) - jnp.sum(gm, axis=0, keepdims=True)
                                         + jnp.where(lane == C - 1, dglast, 0.0))
                dbeta_ref[hh, c:c + 1, :] = _row(dbc, eye)
            for c in cs:
                dq_ref[rows[c], :] = dq_acc[c]
                dk_ref[rows[c], :] = dk_acc[c]
        for hh in range(2):
            dg_ref[hh] = _dot_hi(dgc_s[hh], lower)

    rows_spec = pl.BlockSpec((2, cpt, C), lambda j, i: (j, i, 0))
    qk_spec = pl.BlockSpec((tt, GDN_DK), lambda j, i: (i, j))
    v_spec = pl.BlockSpec((tt, 2 * GDN_DK), lambda j, i: (i, j))
    cc_spec = pl.BlockSpec((2, cpt, C, C), lambda j, i: (j, i, 0, 0))
    rows_shape = jax.ShapeDtypeStruct((GDN_HV, nC, C), F32)
    return pl.pallas_call(
        body, grid=(GDN_HV // 2, T // tt),
        in_specs=[qk_spec, qk_spec, v_spec, rows_spec, rows_spec, cc_spec, v_spec, v_spec, cc_spec, v_spec,
                  pl.BlockSpec((2, cpt, GDN_DK, GDN_DK), lambda j, i: (j, i, 0, 0)), v_spec, v_spec, v_spec, rows_spec],
        out_specs=[qk_spec, qk_spec, v_spec, rows_spec, rows_spec],
        out_shape=[jax.ShapeDtypeStruct((T, GDN_HV // 2 * GDN_DK), F32),
                   jax.ShapeDtypeStruct((T, GDN_HV // 2 * GDN_DK), F32),
                   jax.ShapeDtypeStruct((T, D_INNER), F32), rows_shape, rows_shape],
        scratch_shapes=[pltpu.VMEM((2, cpt, C), F32)],
        compiler_params=_params("parallel", "parallel"), name=name,
    )(q, k, v, gc, beta, tinv, u, w, pm, vn, sall, do, dvn, dkd, dgl)


def _gdn_gate_bwd(araw, braw, dg, dbeta, alog, dtb, name):
    H, T = araw.shape

    def body(a_ref, b_ref, dg_ref, dbt_ref, alog_ref, dtb_ref, da_ref, db_ref, dalog_ref, ddtb_ref):
        xa = a_ref[...] + dtb_ref[...]
        ea = jnp.exp(alog_ref[...])
        dgv = dg_ref[...]
        da = -dgv * ea * _sigmoid(xa)
        da_ref[...] = da
        dalog_ref[...] = jnp.sum(-dgv * ea * _softplus(xa), axis=1, keepdims=True)
        ddtb_ref[...] = jnp.sum(da, axis=1, keepdims=True)
        bt = _sigmoid(b_ref[...])
        db_ref[...] = dbt_ref[...] * bt * (1.0 - bt)

    return pl.pallas_call(
        body,
        out_shape=[jax.ShapeDtypeStruct((H, T), F32), jax.ShapeDtypeStruct((H, T), F32),
                   jax.ShapeDtypeStruct((H, 1), F32), jax.ShapeDtypeStruct((H, 1), F32)],
        compiler_params=pltpu.CompilerParams(vmem_limit_bytes=VMEM_LIMIT_BYTES), name=name,
    )(araw, braw, dg, dbeta, alog, dtb)


SSD_LOCKSTEP_CHUNKS = 2


def _ssd_scan_fwd(xs, bm, cm, dtraw, alog, dtb, dskip, name):
    T = xs.shape[0]
    Q = SSD_CHUNK
    tt = min(T, 1024)
    cpt, nC = tt // Q, T // Q
    GW = SSD_R * SSD_P

    def body(alog_ref, dtb_ref, dsk_ref, xs_ref, b_ref, c_ref, dt_ref, y_ref, sall_ref, dto_ref, S, dt_s, acs_s):
        gi, i = pl.program_id(0), pl.program_id(1)

        @pl.when(i == 0)
        def _():
            S[...] = jnp.zeros_like(S)

        tri, _, eye, r_i, c_i = _masks(Q)
        upper = jnp.where(r_i <= c_i, 1.0, 0.0)
        for r in range(SSD_R):
            h = SSD_R * gi + r
            dt = _softplus(dt_ref[r] + dtb_ref[h])
            dto_ref[r] = dt
            dt_s[r] = dt
            acs_s[r] = _dot_hi(-jnp.exp(alog_ref[h]) * dt, upper)

        ps = [slice(r * SSD_P, (r + 1) * SSD_P) for r in range(SSD_R)]
        s_cur = [S[:, ps[r]] for r in range(SSD_R)]
        grp = min(cpt, SSD_LOCKSTEP_CHUNKS)
        for c0 in range(0, cpt, grp):
            cs = list(range(c0, c0 + grp))
            inst = [(c, r) for c in cs for r in range(SSD_R)]
            rows = {c: slice(c * Q, (c + 1) * Q) for c in cs}
            bc_ = {c: b_ref[rows[c], :] for c in cs}
            cc_ = {c: c_ref[rows[c], :] for c in cs}
            cb = {c: _dot_nt(cc_[c], bc_[c]) for c in cs}
            xr = [xs_ref[rows[c], ps[r]] for c, r in inst]
            acr = [acs_s[r, c:c + 1, :] for c, r in inst]
            acc = [_col(a, eye) for a in acr]
            dtc = [_col(dt_s[r, c:c + 1, :], eye) for c, r in inst]
            xd = [x * d for x, d in zip(xr, dtc)]
            mm = [cb[c] * jnp.exp(jnp.where(tri, acc[i] - acr[i], -1e30)) for i, (c, r) in enumerate(inst)]
            st = [_dot_tn(bc_[c] * jnp.exp(acr[i][:, Q - 1:Q] - acc[i]), xd[i]) for i, (c, r) in enumerate(inst)]
            yd = [_dot(mm[i], xd[i]) for i in range(len(inst))]
            s_prev = []
            for i, (c, r) in enumerate(inst):
                s_prev.append(s_cur[r])
                s_cur[r] = s_cur[r] * jnp.exp(acr[i][:, Q - 1:Q]) + st[i]
            yo = [_dot(cc_[c] * jnp.exp(acc[i]), s_prev[i]) for i, (c, r) in enumerate(inst)]
            for i, (c, r) in enumerate(inst):
                sall_ref[0, c, :, ps[r]] = s_prev[i]
                y_ref[rows[c], ps[r]] = yd[i] + yo[i] + dsk_ref[SSD_R * gi + r] * xr[i]
        for r in range(SSD_R):
            S[:, ps[r]] = s_cur[r]

    smem = pl.BlockSpec(memory_space=pltpu.SMEM)
    rows_spec = pl.BlockSpec((SSD_R, cpt, Q), lambda g, i: (g, i, 0))
    return pl.pallas_call(
        body, grid=(SSD_G, T // tt),
        in_specs=[smem, smem, smem,
                  pl.BlockSpec((tt, GW), lambda g, i: (i, g)), pl.BlockSpec((tt, SSD_N), lambda g, i: (i, g)),
                  pl.BlockSpec((tt, SSD_N), lambda g, i: (i, g)), rows_spec],
        out_specs=[pl.BlockSpec((tt, GW), lambda g, i: (i, g)),
                   pl.BlockSpec((1, cpt, SSD_N, GW), lambda g, i: (g, i, 0, 0)), rows_spec],
        out_shape=[jax.ShapeDtypeStruct((T, D_INNER), F32), jax.ShapeDtypeStruct((SSD_G, nC, SSD_N, GW), F32),
                   jax.ShapeDtypeStruct((SSD_H, nC, Q), F32)],
        scratch_shapes=[pltpu.VMEM((SSD_N, GW), F32), pltpu.VMEM((SSD_R, cpt, Q), F32),
                        pltpu.VMEM((SSD_R, cpt, Q), F32)],
        compiler_params=_params("parallel", "arbitrary"), name=name,
    )(alog, dtb, dskip, xs, bm, cm, dtraw)


def _ssd_scan_bwd(xs, bm, cm, dt, sall, dy, alog, dskip, name):
    T = xs.shape[0]
    Q = SSD_CHUNK
    tt = min(T, 1024)
    cpt, nC, nT = tt // Q, T // Q, T // tt
    GW = SSD_R * SSD_P

    def body(alog_ref, dsk_ref, xs_ref, b_ref, c_ref, dt_ref, sall_ref, dy_ref,
             dxs_ref, db_ref, dc_ref, da_ref, ddt_ref, dd_ref, dS, acs_s, dacs_s, ddt_s, dd_s):
        gi, i = pl.program_id(0), pl.program_id(1)

        @pl.when(i == 0)
        def _():
            dS[...] = jnp.zeros_like(dS)

        tri, _, eye, r_i, c_i = _masks(Q)
        upper = jnp.where(r_i <= c_i, 1.0, 0.0)
        lower = jnp.where(r_i >= c_i, 1.0, 0.0)
        lane = lax.broadcasted_iota(jnp.int32, (1, Q), 1)
        for r in range(SSD_R):
            acs_s[r] = _dot_hi(-jnp.exp(alog_ref[SSD_R * gi + r]) * dt_ref[r], upper)

        ps = [slice(r * SSD_P, (r + 1) * SSD_P) for r in range(SSD_R)]
        ds_cur = [dS[:, ps[r]] for r in range(SSD_R)]
        grp = min(cpt, SSD_LOCKSTEP_CHUNKS)
        rsum = lambda a: jnp.sum(a, axis=1, keepdims=True)
        for c0 in range(cpt - grp, -1, -grp):
            cs = list(range(c0 + grp - 1, c0 - 1, -1))
            inst = [(c, r) for c in cs for r in range(SSD_R)]
            n = len(inst)
            rows = {c: slice(c * Q, (c + 1) * Q) for c in cs}
            bc_ = {c: b_ref[rows[c], :] for c in cs}
            cc_ = {c: c_ref[rows[c], :] for c in cs}
            cb = {c: _dot_nt(cc_[c], bc_[c]) for c in cs}
            xr = [xs_ref[rows[c], ps[r]] for c, r in inst]
            dyr = [dy_ref[rows[c], ps[r]] for c, r in inst]
            acr = [acs_s[r, c:c + 1, :] for c, r in inst]
            acc = [_col(a, eye) for a in acr]
            dtc = [_col(dt_ref[r, c:c + 1, :], eye) for c, r in inst]
            e_c = [jnp.exp(a) for a in acc]
            dl_c = [jnp.exp(acr[i][:, Q - 1:Q] - acc[i]) for i in range(n)]
            gl = [jnp.exp(acr[i][:, Q - 1:Q]) for i in range(n)]
            xd = [x * d for x, d in zip(xr, dtc)]
            lm = [jnp.exp(jnp.where(tri, acc[i] - acr[i], -1e30)) for i in range(n)]
            mm = [cb[c] * lm[i] for i, (c, r) in enumerate(inst)]
            sr = [sall_ref[0, c, :, ps[r]] for c, r in inst]
            dmm = [jnp.where(tri, _dot_nt(dyr[i], xd[i]), 0.0) for i in range(n)]
            dxd1 = [_dot_tn(mm[i], dyr[i]) for i in range(n)]
            dce = [_dot_nt(dyr[i], sr[i]) for i in range(n)]
            cdy = [_dot_tn(cc_[c] * e_c[i], dyr[i]) for i, (c, r) in enumerate(inst)]
            dsn = []
            for i, (c, r) in enumerate(inst):
                dsn.append(ds_cur[r])
                ds_cur[r] = gl[i] * ds_cur[r] + cdy[i]
            dxd = [dxd1[i] + _dot(bc_[c] * dl_c[i], dsn[i]) for i, (c, r) in enumerate(inst)]
            dbd = [_dot_nt(xd[i], dsn[i]) for i in range(n)]
            dcb = {c: jnp.zeros((Q, Q), F32) for c in cs}
            db_acc = {c: jnp.zeros((Q, SSD_N), F32) for c in cs}
            dc_acc = {c: jnp.zeros((Q, SSD_N), F32) for c in cs}
            for i, (c, r) in enumerate(inst):
                dgl = jnp.sum(rsum(dsn[i] * sr[i]), axis=0, keepdims=True)
                dc_acc[c] = dc_acc[c] + dce[i] * e_c[i]
                db_acc[c] = db_acc[c] + dbd[i] * dl_c[i]
                de = rsum(dce[i] * cc_[c])
                ddl = rsum(dbd[i] * bc_[c])
                dcb[c] = dcb[c] + dmm[i] * lm[i]
                gm = dmm[i] * mm[i]
                dacs_col = rsum(gm) + de * e_c[i] - ddl * dl_c[i]
                dalast = jnp.sum(ddl * dl_c[i], axis=0, keepdims=True) + dgl * gl[i]
                dacs_s[r, c:c + 1, :] = (_row(dacs_col, eye) - jnp.sum(gm, axis=0, keepdims=True)
                                         + jnp.where(lane == Q - 1, dalast, 0.0))
                ddt_s[r, c:c + 1, :] = _row(rsum(dxd[i] * xr[i]), eye)
                dd_s[r, c:c + 1, :] = _row(rsum(dyr[i] * xr[i]), eye)
                dxs_ref[rows[c], ps[r]] = dxd[i] * dtc[i] + dsk_ref[SSD_R * gi + r] * dyr[i]
            for c in cs:
                dc_ref[rows[c], :] = dc_acc[c] + _dot(dcb[c], bc_[c])
                db_ref[rows[c], :] = db_acc[c] + _dot_tn(dcb[c], cc_[c])
        for r in range(SSD_R):
            dS[:, ps[r]] = ds_cur[r]
        for r in range(SSD_R):
            da_ref[r] = _dot_hi(dacs_s[r], lower)
            ddt_ref[r] = ddt_s[r]
            dd_ref[r] = dd_s[r]

    rev = lambda i: nT - 1 - i
    smem = pl.BlockSpec(memory_space=pltpu.SMEM)
    rows_spec = pl.BlockSpec((SSD_R, cpt, Q), lambda g, i: (g, rev(i), 0))
    x_spec = pl.BlockSpec((tt, GW), lambda g, i: (rev(i), g))
    n_spec = pl.BlockSpec((tt, SSD_N), lambda g, i: (rev(i), g))
    rows_shape = jax.ShapeDtypeStruct((SSD_H, nC, Q), F32)
    return pl.pallas_call(
        body, grid=(SSD_G, nT),
        in_specs=[smem, smem, x_spec, n_spec, n_spec, rows_spec,
                  pl.BlockSpec((1, cpt, SSD_N, GW), lambda g, i: (g, rev(i), 0, 0)), x_spec],
        out_specs=[x_spec, n_spec, n_spec, rows_spec, rows_spec, rows_spec],
        out_shape=[jax.ShapeDtypeStruct((T, D_INNER), F32), jax.ShapeDtypeStruct((T, SSD_G * SSD_N), F32),
                   jax.ShapeDtypeStruct((T, SSD_G * SSD_N), F32), rows_shape, rows_shape, rows_shape],
        scratch_shapes=[pltpu.VMEM((SSD_N, GW), F32)] + [pltpu.VMEM((SSD_R, cpt, Q), F32)] * 4,
        compiler_params=_params("parallel", "arbitrary"), name=name,
    )(alog, dskip, xs, bm, cm, dt, sall, dy)


def _ssd_gate_bwd(dtraw, dt, da, ddt_direct, ddrow, alog, dtb, name):
    H, T = dtraw.shape

    def body(raw_ref, dt_ref, da_ref, ddt_ref, dd_ref, alog_ref, dtb_ref, draw_ref, dalog_ref, ddtb_ref, dD_ref):
        a = -jnp.exp(alog_ref[...])
        dav = da_ref[...]
        ddt = ddt_ref[...] + dav * a
        draw = ddt * _sigmoid(raw_ref[...] + dtb_ref[...])
        draw_ref[...] = draw
        dalog_ref[...] = jnp.sum(dav * dt_ref[...], axis=1, keepdims=True) * a
        ddtb_ref[...] = jnp.sum(draw, axis=1, keepdims=True)
        dD_ref[...] = jnp.sum(dd_ref[...], axis=1, keepdims=True)

    return pl.pallas_call(
        body,
        out_shape=[jax.ShapeDtypeStruct((H, T), F32)] + [jax.ShapeDtypeStruct((H, 1), F32)] * 3,
        compiler_params=pltpu.CompilerParams(vmem_limit_bytes=VMEM_LIMIT_BYTES), name=name,
    )(dtraw, dt, da, ddt_direct, ddrow, alog, dtb)


def _final_loss(x, fw, tgt, name):
    T = x.shape[0]
    tt = min(T, 512)
    nT = T // tt

    def body(x_ref, w_ref, t_ref, dx_ref, dw_ref, loss_ref, acc):
        i = pl.program_id(0)

        @pl.when(i == 0)
        def _():
            dw_ref[...] = jnp.zeros_like(dw_ref)
            acc[...] = jnp.zeros_like(acc)

        xv = x_ref[...]
        r = lax.rsqrt(jnp.mean(xv * xv, axis=-1, keepdims=True) + EPS)
        xh = xv * r
        err = xh * w_ref[...] - t_ref[...]
        acc[...] += jnp.sum(err * err, axis=0, keepdims=True)
        dout = err * (1.0 / D_MODEL)
        dw_ref[...] += jnp.sum(dout * xh, axis=0, keepdims=True)
        dxn = dout * w_ref[...]
        dx_ref[...] = r * (dxn - xh * jnp.mean(dxn * xh, axis=-1, keepdims=True))

        @pl.when(i == nT - 1)
        def _():
            loss_ref[...] = (0.5 / D_MODEL) * jnp.sum(acc[...], axis=1, keepdims=True)

    row = pl.BlockSpec((tt, D_MODEL), lambda i: (i, 0))
    vec = pl.BlockSpec((1, D_MODEL), lambda i: (0, 0))
    return pl.pallas_call(
        body, grid=(nT,),
        in_specs=[row, vec, row],
        out_specs=[row, vec, pl.BlockSpec((1, 1), lambda i: (0, 0))],
        out_shape=[jax.ShapeDtypeStruct((T, D_MODEL), F32), jax.ShapeDtypeStruct((1, D_MODEL), F32),
                   jax.ShapeDtypeStruct((1, 1), F32)],
        scratch_shapes=[pltpu.VMEM((1, D_MODEL), F32)],
        compiler_params=_params("arbitrary"), name=name,
    )(x, fw, tgt)


def _adamw(parts, w, m, v, name):
    R, C = w.shape
    tr = 128 if R % 128 == 0 else R

    def body(p_ref, w_ref, m_ref, v_ref, g_ref, d_ref, nm_ref, nv_ref):
        g = p_ref[0]
        for s in range(1, N_DEV):
            g = g + p_ref[s]
        mn = ADAM_B1 * m_ref[...] + (1.0 - ADAM_B1) * g
        vn = ADAM_B2 * v_ref[...] + (1.0 - ADAM_B2) * (g * g)
        mh = mn / (1.0 - ADAM_B1 ** ADAM_STEP)
        vh = vn / (1.0 - ADAM_B2 ** ADAM_STEP)
        g_ref[...] = g
        d_ref[...] = -ADAM_LR * (mh / (jnp.sqrt(vh) + ADAM_EPS) + ADAM_WD * w_ref[...])
        nm_ref[...] = mn
        nv_ref[...] = vn

    blk = pl.BlockSpec((tr, C), lambda i: (i, 0))
    return pl.pallas_call(
        body, grid=(R // tr,),
        in_specs=[pl.BlockSpec((N_DEV, tr, C), lambda i: (0, i, 0)), blk, blk, blk],
        out_specs=[blk] * 4,
        out_shape=[jax.ShapeDtypeStruct((R, C), F32)] * 4,
        compiler_params=_params("parallel"), name=name,
    )(parts, w, m, v)


def _me():
    x, y, c = lax.axis_index("x"), lax.axis_index("y"), lax.axis_index("c")
    return x, y, c


def _peer(d):
    x, y, c = _me()
    px = 1 - x if d & 4 else x
    py = 1 - y if d & 2 else y
    pc = 1 - c if d & 1 else c
    return (px, py, pc), 4 * px + 2 * py + pc


def _exchange(arrs, bcast, name):
    n = len(arrs)

    def body(*refs):
        ins, outs = refs[:n], refs[n:2 * n]
        ssem, rsem, lsem = refs[2 * n:]
        x, y, c = _me()
        me = 4 * x + 2 * y + c

        def src(a, dest):
            return ins[a] if bcast[a] else ins[a].at[dest]

        local = [pltpu.make_async_copy(src(a, me), outs[a].at[me], lsem.at[a]) for a in range(n)]
        for cp in local:
            cp.start()
        sends, recvs = [], []
        for a in range(n):
            for d in range(1, N_DEV):
                peer, pid = _peer(d)
                sends.append(pltpu.make_async_remote_copy(
                    src_ref=src(a, pid), dst_ref=outs[a].at[me], send_sem=ssem.at[a, d - 1],
                    recv_sem=rsem.at[a, d - 1], device_id=peer, device_id_type=MESH))
                recvs.append(pltpu.make_async_remote_copy(
                    src_ref=src(a, pid), dst_ref=outs[a].at[pid], send_sem=ssem.at[a, d - 1],
                    recv_sem=rsem.at[a, d - 1], device_id=peer, device_id_type=MESH))
        for cp in sends:
            cp.start()
        for cp in recvs:
            cp.wait_recv()
        for cp in sends:
            cp.wait_send()
        for cp in local:
            cp.wait()

    def out_shape(a, is_b):
        return jax.ShapeDtypeStruct((N_DEV,) + (a.shape if is_b else a.shape[1:]), a.dtype)

    anyspec = pl.BlockSpec(memory_space=pl.ANY)
    return pl.pallas_call(
        body,
        in_specs=[anyspec] * n, out_specs=[anyspec] * n,
        out_shape=[out_shape(a, b) for a, b in zip(arrs, bcast)],
        scratch_shapes=[pltpu.SemaphoreType.DMA((n, N_DEV - 1)), pltpu.SemaphoreType.DMA((n, N_DEV - 1)),
                        pltpu.SemaphoreType.DMA((n,))],
        name=name,
    )(*arrs)


def _to_rows(cols, chunk):
    T, H = cols.shape
    return cols.T.reshape(H, T // chunk, chunk)


def _from_rows(rows):
    return rows.T


def _pad_cols(a, width):
    return jnp.pad(a, ((0, 0), (0, width - a.shape[1])))


def _local_step(x, tgt, p):
    T = x.shape[0]
    zb = lambda n: jnp.zeros((1, n), F32)
    gw = p["gdn_w_in"]
    g_wparts = [gw[:, 0:1024], gw[:, 1024:2048], gw[:, 2048:4096], gw[:, 4096:6144], _pad_cols(gw[:, 6144:6176], PAD_W)]
    nw0, nw1 = p["norm_w"][0:1], p["norm_w"][1:2]
    h0, (q_pre, k_pre, v_pre, z0, ab) = _norm_inproj(x, nw0, g_wparts, "gdn_inproj")
    gcw = p["gdn_conv_w"]
    cw_q, cw_k, cw_v = gcw[:, 0:1024], gcw[:, 1024:2048], gcw[:, 2048:4096]
    q = _conv_fwd(q_pre, cw_q, zb(1024), True, GDN_DK ** -0.5, "gdn_conv_q")
    k = _conv_fwd(k_pre, cw_k, zb(1024), True, 1.0, "gdn_conv_k")
    v = _conv_fwd(v_pre, cw_v, zb(2048), False, 1.0, "gdn_conv_v")
    braw = _to_rows(ab[:, 0:GDN_HV], GDN_CHUNK)
    araw = _to_rows(ab[:, GDN_HV:2 * GDN_HV], GDN_CHUNK)
    g_alog, g_dtb = p["gdn_a_log"].reshape(-1), p["gdn_dt_bias"].reshape(-1)
    g_u, g_w, g_pm, g_ti, g_rows, beta_rows, gc_rows = _gdn_prep(q, k, v, araw, braw, g_alog, g_dtb, "gdn_prep")
    o0, g_vn, g_sall = _gdn_state_fwd(q, k, g_u, g_w, g_pm, gc_rows, "gdn_state_fwd")
    x1 = _out_fwd(o0, z0, p["gdn_norm_w"], p["gdn_w_out"], x, GDN_DK, False, "gdn_out")
    sw = p["ssd_w_in"]
    s_wparts = [sw[:, 0:2048], sw[:, 2048:4096], sw[:, 4096:5120], sw[:, 5120:6144], _pad_cols(sw[:, 6144:6176], PAD_W)]
    h1, (z1, xs_pre, b_pre, c_pre, dtp) = _norm_inproj(x1, nw1, s_wparts, "ssd_inproj")
    scw, scb = p["ssd_conv_w"], p["ssd_conv_b"]
    xs = _conv_fwd(xs_pre, scw[:, 0:2048], scb[:, 0:2048], False, 1.0, "ssd_conv_x")
    bm = _conv_fwd(b_pre, scw[:, 2048:3072], scb[:, 2048:3072], False, 1.0, "ssd_conv_b")
    cm = _conv_fwd(c_pre, scw[:, 3072:4096], scb[:, 3072:4096], False, 1.0, "ssd_conv_c")
    dtraw = _to_rows(dtp[:, 0:SSD_H], SSD_CHUNK)
    s_alog, s_dtb, s_d = p["ssd_a_log"].reshape(-1), p["ssd_dt_bias"].reshape(-1), p["ssd_d"].reshape(-1)
    y1, s_sall, dt_rows = _ssd_scan_fwd(xs, bm, cm, dtraw, s_alog, s_dtb, s_d, "ssd_scan_fwd")
    x2 = _out_fwd(y1, z1, p["ssd_norm_w"], p["ssd_w_out"], x1, D_INNER // SSD_G, True, "ssd_out")
    dx2, d_fw, loss = _final_loss(x2, p["final_norm_w"].reshape(1, -1), tgt, "final_loss")
    dy1, dz1, d_snw, yn1 = _out_bwd(dx2, y1, z1, p["ssd_norm_w"], p["ssd_w_out"], D_INNER // SSD_G, True, "ssd_out_bwd")
    d_swout = _matmul_tn(yn1, dx2, "ssd_wout_grad")
    dxs, dbm, dcm, da_rows, ddt_rows, dd_rows = _ssd_scan_bwd(xs, bm, cm, dt_rows, s_sall, dy1, s_alog, s_d, "ssd_scan_bwd")
    col = lambda a: a.reshape(-1, 1)
    dtraw_g, d_salog, d_sdtb, d_sd = _ssd_gate_bwd(
        dtraw.reshape(SSD_H, T), dt_rows.reshape(SSD_H, T), da_rows.reshape(SSD_H, T),
        ddt_rows.reshape(SSD_H, T), dd_rows.reshape(SSD_H, T), col(s_alog), col(s_dtb), "ssd_gate_bwd")
    dxs_pre, dcw_x, dcb_x = _conv_bwd(xs_pre, scw[:, 0:2048], scb[:, 0:2048], dxs, False, 1.0, "ssd_conv_x_bwd")
    db_pre, dcw_b, dcb_b = _conv_bwd(b_pre, scw[:, 2048:3072], scb[:, 2048:3072], dbm, False, 1.0, "ssd_conv_b_bwd")
    dc_pre, dcw_c, dcb_c = _conv_bwd(c_pre, scw[:, 3072:4096], scb[:, 3072:4096], dcm, False, 1.0, "ssd_conv_c_bwd")
    ddtp = _pad_cols(_from_rows(dtraw_g), PAD_W)
    s_dparts = [dz1, dxs_pre, db_pre, dc_pre, ddtp]
    dx1, d_nw1 = _inproj_bwd(x1, nw1, s_dparts, s_wparts, dx2, "ssd_inproj_bwd")
    s_dw = [_matmul_tn(h1, d, "ssd_win_grad_%d" % n) for n, d in enumerate(s_dparts)]
    d_swin = jnp.concatenate(s_dw[:4] + [s_dw[4][:, 0:SSD_H]], axis=1)
    do0, dz0, d_gnw, yn0 = _out_bwd(dx1, o0, z0, p["gdn_norm_w"], p["gdn_w_out"], GDN_DK, False, "gdn_out_bwd")
    d_gwout = _matmul_tn(yn0, dx1, "gdn_wout_grad")
    g_dvn, g_dkd, g_dgl = _gdn_state_bwd(q, k, g_w, g_pm, g_vn, g_sall, gc_rows, do0, "gdn_state_bwd")
    dq, dk, dv, dg_rows, dbeta_rows = _gdn_local_bwd(q, k, v, gc_rows, beta_rows, g_ti, g_u, g_w, g_pm, g_vn, g_sall,
                                                     do0, g_dvn, g_dkd, g_dgl, "gdn_local_bwd")
    da_g, db_g, d_galog, d_gdtb = _gdn_gate_bwd(
        araw.reshape(GDN_HV, T), braw.reshape(GDN_HV, T), dg_rows.reshape(GDN_HV, T),
        dbeta_rows.reshape(GDN_HV, T), col(g_alog), col(g_dtb), "gdn_gate_bwd")
    dq_pre, dcw_q, _ = _conv_bwd(q_pre, cw_q, zb(1024), dq, True, GDN_DK ** -0.5, "gdn_conv_q_bwd")
    dk_pre, dcw_k, _ = _conv_bwd(k_pre, cw_k, zb(1024), dk, True, 1.0, "gdn_conv_k_bwd")
    dv_pre, dcw_v, _ = _conv_bwd(v_pre, cw_v, zb(2048), dv, False, 1.0, "gdn_conv_v_bwd")
    dab = _pad_cols(jnp.concatenate([_from_rows(db_g), _from_rows(da_g)], axis=1), PAD_W)
    g_dparts = [dq_pre, dk_pre, dv_pre, dz0, dab]
    dx0, d_nw0 = _inproj_bwd(x, nw0, g_dparts, g_wparts, dx1, "gdn_inproj_bwd")
    g_dw = [_matmul_tn(h0, d, "gdn_win_grad_%d" % n) for n, d in enumerate(g_dparts)]
    d_gwin = jnp.concatenate(g_dw[:4] + [g_dw[4][:, 0:2 * GDN_HV]], axis=1)
    grads = {
        "norm_w": jnp.concatenate([d_nw0, d_nw1], axis=0),
        "gdn_w_in": d_gwin,
        "gdn_conv_w": jnp.concatenate([dcw_q, dcw_k, dcw_v], axis=1),
        "gdn_a_log": d_galog.reshape(1, -1),
        "gdn_dt_bias": d_gdtb.reshape(1, -1),
        "gdn_norm_w": d_gnw,
        "gdn_w_out": d_gwout,
        "ssd_w_in": d_swin,
        "ssd_conv_w": jnp.concatenate([dcw_x, dcw_b, dcw_c], axis=1),
        "ssd_conv_b": jnp.concatenate([dcb_x, dcb_b, dcb_c], axis=1),
        "ssd_dt_bias": d_sdtb.reshape(1, -1),
        "ssd_a_log": d_salog.reshape(1, -1),
        "ssd_d": d_sd.reshape(1, -1),
        "ssd_norm_w": d_snw,
        "ssd_w_out": d_swout,
        "final_norm_w": d_fw,
    }
    return loss, dx0, grads


WEIGHTS = ["norm_w", "gdn_w_in", "gdn_conv_w", "gdn_a_log", "gdn_dt_bias", "gdn_norm_w", "gdn_w_out", "ssd_w_in",
           "ssd_conv_w", "ssd_conv_b", "ssd_dt_bias", "ssd_a_log", "ssd_d", "ssd_norm_w", "ssd_w_out", "final_norm_w"]
COL_SHARDED = ["gdn_w_in", "ssd_w_in"]
ROW_SHARDED = ["gdn_w_out", "ssd_w_out"]
SMALL_SHARDED = ["gdn_conv_w", "ssd_conv_w", "ssd_conv_b", "ssd_norm_w"]
REPLICATED = ["norm_w", "gdn_a_log", "gdn_dt_bias", "gdn_norm_w", "ssd_dt_bias", "ssd_a_log", "ssd_d", "final_norm_w"]


def _pack(arrs):
    return jnp.concatenate([a.reshape(-1) for a in arrs]).reshape(1, -1)


def _unpack(flat, shapes):
    out, pos = [], 0
    for s in shapes:
        n = 1
        for dim in s:
            n *= dim
        out.append(flat[pos:pos + n].reshape(s))
        pos += n
    return out


def _cols_to_shards(full):
    R, C = full.shape
    return full.reshape(R, N_DEV, C // N_DEV).transpose(1, 0, 2)


def _shards_to_cols(shards):
    n, R, c = shards.shape
    return shards.transpose(1, 0, 2).reshape(R, n * c)


def kernel(x, norm_w, gdn_w_in, gdn_conv_w, gdn_a_log, gdn_dt_bias, gdn_norm_w, gdn_w_out, ssd_w_in, ssd_conv_w, ssd_conv_b, ssd_dt_bias, ssd_a_log, ssd_d, ssd_norm_w, ssd_w_out, final_norm_w, loss_target, m_norm_w, m_gdn_w_in, m_gdn_conv_w, m_gdn_a_log, m_gdn_dt_bias, m_gdn_norm_w, m_gdn_w_out, m_ssd_w_in, m_ssd_conv_w, m_ssd_conv_b, m_ssd_dt_bias, m_ssd_a_log, m_ssd_d, m_ssd_norm_w, m_ssd_w_out, m_final_norm_w, v_norm_w, v_gdn_w_in, v_gdn_conv_w, v_gdn_a_log, v_gdn_dt_bias, v_gdn_norm_w, v_gdn_w_out, v_ssd_w_in, v_ssd_conv_w, v_ssd_conv_b, v_ssd_dt_bias, v_ssd_a_log, v_ssd_d, v_ssd_norm_w, v_ssd_w_out, v_final_norm_w):
    w = dict(norm_w=norm_w, gdn_w_in=gdn_w_in[0], gdn_conv_w=gdn_conv_w[0], gdn_a_log=gdn_a_log,
             gdn_dt_bias=gdn_dt_bias, gdn_norm_w=gdn_norm_w, gdn_w_out=gdn_w_out[0], ssd_w_in=ssd_w_in[0],
             ssd_conv_w=ssd_conv_w[0], ssd_conv_b=ssd_conv_b, ssd_dt_bias=ssd_dt_bias, ssd_a_log=ssd_a_log,
             ssd_d=ssd_d, ssd_norm_w=ssd_norm_w, ssd_w_out=ssd_w_out[0], final_norm_w=final_norm_w.reshape(1, -1))
    m = dict(norm_w=m_norm_w, gdn_w_in=m_gdn_w_in[0], gdn_conv_w=m_gdn_conv_w[0], gdn_a_log=m_gdn_a_log,
             gdn_dt_bias=m_gdn_dt_bias, gdn_norm_w=m_gdn_norm_w, gdn_w_out=m_gdn_w_out[0], ssd_w_in=m_ssd_w_in[0],
             ssd_conv_w=m_ssd_conv_w[0], ssd_conv_b=m_ssd_conv_b, ssd_dt_bias=m_ssd_dt_bias, ssd_a_log=m_ssd_a_log,
             ssd_d=m_ssd_d, ssd_norm_w=m_ssd_norm_w, ssd_w_out=m_ssd_w_out[0], final_norm_w=m_final_norm_w.reshape(1, -1))
    v = dict(norm_w=v_norm_w, gdn_w_in=v_gdn_w_in[0], gdn_conv_w=v_gdn_conv_w[0], gdn_a_log=v_gdn_a_log,
             gdn_dt_bias=v_gdn_dt_bias, gdn_norm_w=v_gdn_norm_w, gdn_w_out=v_gdn_w_out[0], ssd_w_in=v_ssd_w_in[0],
             ssd_conv_w=v_ssd_conv_w[0], ssd_conv_b=v_ssd_conv_b, ssd_dt_bias=v_ssd_dt_bias, ssd_a_log=v_ssd_a_log,
             ssd_d=v_ssd_d, ssd_norm_w=v_ssd_norm_w, ssd_w_out=v_ssd_w_out[0], final_norm_w=v_final_norm_w.reshape(1, -1))
    out_shapes = {n: a.shape for n, a in zip(
        WEIGHTS, [norm_w, gdn_w_in, gdn_conv_w, gdn_a_log, gdn_dt_bias, gdn_norm_w, gdn_w_out, ssd_w_in, ssd_conv_w,
                  ssd_conv_b, ssd_dt_bias, ssd_a_log, ssd_d, ssd_norm_w, ssd_w_out, final_norm_w])}

    small_shapes = [w[n].shape for n in SMALL_SHARDED]
    gathered = _exchange([_mx(w[n]) for n in COL_SHARDED + ROW_SHARDED] + [_pack([w[n] for n in SMALL_SHARDED])],
                         [True] * 5, "gather_weights")
    full = dict(w)
    for n, gth in zip(COL_SHARDED, gathered[0:2]):
        full[n] = _shards_to_cols(gth)
    for n, gth in zip(ROW_SHARDED, gathered[2:4]):
        full[n] = gth.reshape(-1, gth.shape[-1])
    small_all = [_unpack(gathered[4][s, 0], small_shapes) for s in range(N_DEV)]
    for idx, n in enumerate(SMALL_SHARDED):
        full[n] = jnp.concatenate([small_all[s][idx] for s in range(N_DEV)], axis=-1)

    loss, dx, grads = _local_step(x[0], loss_target[0], full)

    send_small = jnp.concatenate(
        [_cols_to_shards(grads[n]).reshape(N_DEV, -1) for n in SMALL_SHARDED], axis=1)[:, None, :]
    rep_shapes = [w[n].shape for n in REPLICATED]
    recv = _exchange(
        [_cols_to_shards(grads[n]) for n in COL_SHARDED]
        + [grads[n].reshape(N_DEV, -1, D_MODEL) for n in ROW_SHARDED]
        + [send_small, _pack([grads[n] for n in REPLICATED])],
        [False] * 5 + [True], "exchange_grads")

    res = {}
    for n, parts in zip(COL_SHARDED + ROW_SHARDED, recv[0:4]):
        res[n] = _adamw(parts, w[n], m[n], v[n], "adamw_" + n)
    small_res = _adamw(recv[4], *[_pack([t[n] for n in SMALL_SHARDED]) for t in (w, m, v)], "adamw_small")
    rep_res = _adamw(recv[5], *[_pack([t[n] for n in REPLICATED]) for t in (w, m, v)], "adamw_replicated")
    for k4 in range(4):
        for n, a in zip(SMALL_SHARDED, _unpack(small_res[k4][0], small_shapes)):
            res.setdefault(n, [None] * 4)[k4] = a
        for n, a in zip(REPLICATED, _unpack(rep_res[k4][0], rep_shapes)):
            res.setdefault(n, [None] * 4)[k4] = a

    loss = lax.psum(loss[0, 0], ("x", "y", "c"))
    outs = [loss, dx[None]]
    for k4 in range(4):
        outs += [res[n][k4].reshape(out_shapes[n]) for n in WEIGHTS]
    return tuple(outs)
```

```python
import jax
import jax.numpy as jnp
from jax import lax
from jax.experimental import pallas as pl
from jax.experimental.pallas import tpu as pltpu

F32 = jnp.float32
MXU_DTYPE = jnp.bfloat16
HI = lax.Precision.HIGHEST
EPS = 1e-6
VMEM_LIMIT_BYTES = 56 * 1024 * 1024
N_DEV = 8
MESH = pl.DeviceIdType.MESH

D_MODEL = 1024
CONV_K = 4
GDN_HV = 16
GDN_DK = 128
GDN_CHUNK = 64
SSD_H = 32
SSD_P = 64
SSD_N = 128
SSD_G = 8
SSD_R = SSD_H // SSD_G
SSD_CHUNK = 128
D_INNER = 2048
PAD_W = 128

ADAM_LR = 0.001
ADAM_B1 = 0.9
ADAM_B2 = 0.999
ADAM_EPS = 1e-08
ADAM_WD = 0.01
ADAM_STEP = 10


def _params(*sem):
    return pltpu.CompilerParams(dimension_semantics=sem, vmem_limit_bytes=VMEM_LIMIT_BYTES)


def _mx(a):
    return a.astype(MXU_DTYPE)


def _dot(a, b):
    return jnp.dot(_mx(a), _mx(b), preferred_element_type=F32)


def _dot_nt(a, b):
    return lax.dot_general(_mx(a), _mx(b), (((1,), (1,)), ((), ())), preferred_element_type=F32)


def _dot_tn(a, b):
    return lax.dot_general(_mx(a), _mx(b), (((0,), (0,)), ((), ())), preferred_element_type=F32)


def _dot_hi(a, b):
    return jnp.dot(a, b, precision=HI, preferred_element_type=F32)


def _sigmoid(x):
    return 1.0 / (1.0 + jnp.exp(-x))


def _silu(x):
    return x * _sigmoid(x)


def _dsilu(x):
    s = _sigmoid(x)
    return s * (1.0 + x * (1.0 - s))


def _softplus(x):
    return jnp.maximum(x, 0.0) + jnp.log1p(jnp.exp(-jnp.abs(x)))


def _col(r, eye):
    return jnp.sum(jnp.where(eye, r, 0.0), axis=1, keepdims=True)


def _row(c, eye):
    return jnp.sum(jnp.where(eye, c, 0.0), axis=0, keepdims=True)


def _col_bcast(r, n):
    return jnp.broadcast_to(r, (n, n)).T


def _masks(n):
    r = lax.broadcasted_iota(jnp.int32, (n, n), 0)
    c = lax.broadcasted_iota(jnp.int32, (n, n), 1)
    return r >= c, r > c, r == c, r, c


def _norm_inproj(x, nw, wparts, name):
    T = x.shape[0]
    tt = min(T, 256)
    n = len(wparts)

    def body(x_ref, nw_ref, *refs):
        w_refs, h_ref, o_refs = refs[:n], refs[n], refs[n + 1:]
        xv = x_ref[...]
        r = lax.rsqrt(jnp.mean(xv * xv, axis=-1, keepdims=True) + EPS)
        h = _mx(xv * r * nw_ref[...])
        h_ref[...] = h
        for w_ref, o_ref in zip(w_refs, o_refs):
            o_ref[...] = jnp.dot(h, w_ref[...], preferred_element_type=F32)

    row = lambda width: pl.BlockSpec((tt, width), lambda i: (i, 0))
    full = lambda a: pl.BlockSpec(a.shape, lambda i: (0, 0))
    outs = pl.pallas_call(
        body, grid=(T // tt,),
        in_specs=[row(D_MODEL), full(nw)] + [full(w) for w in wparts],
        out_specs=[row(D_MODEL)] + [row(w.shape[1]) for w in wparts],
        out_shape=[jax.ShapeDtypeStruct((T, D_MODEL), MXU_DTYPE)]
        + [jax.ShapeDtypeStruct((T, w.shape[1]), F32) for w in wparts],
        compiler_params=_params("parallel"), name=name,
    )(x, nw, *wparts)
    return outs[0], outs[1:]


def _inproj_bwd(x, nw, dparts, wparts, dres, name):
    T = x.shape[0]
    tt = min(T, 256)
    n = len(wparts)

    def body(x_ref, nw_ref, dres_ref, *refs):
        d_refs, w_refs, dx_ref, dnw_ref = refs[:n], refs[n:2 * n], refs[2 * n], refs[2 * n + 1]

        @pl.when(pl.program_id(0) == 0)
        def _():
            dnw_ref[...] = jnp.zeros_like(dnw_ref)

        dh = _dot_nt(d_refs[0][...], w_refs[0][...])
        for d_ref, w_ref in zip(d_refs[1:], w_refs[1:]):
            dh = dh + _dot_nt(d_ref[...], w_ref[...])
        xv = x_ref[...]
        r = lax.rsqrt(jnp.mean(xv * xv, axis=-1, keepdims=True) + EPS)
        xh = xv * r
        dnw_ref[...] += jnp.sum(dh * xh, axis=0, keepdims=True)
        dxn = dh * nw_ref[...]
        dx_ref[...] = dres_ref[...] + r * (dxn - xh * jnp.mean(dxn * xh, axis=-1, keepdims=True))

    row = lambda width: pl.BlockSpec((tt, width), lambda i: (i, 0))
    full = lambda a: pl.BlockSpec(a.shape, lambda i: (0, 0))
    return pl.pallas_call(
        body, grid=(T // tt,),
        in_specs=[row(D_MODEL), full(nw), row(D_MODEL)] + [row(d.shape[1]) for d in dparts]
        + [full(w) for w in wparts],
        out_specs=[row(D_MODEL), pl.BlockSpec((1, D_MODEL), lambda i: (0, 0))],
        out_shape=[jax.ShapeDtypeStruct((T, D_MODEL), F32), jax.ShapeDtypeStruct((1, D_MODEL), F32)],
        compiler_params=_params("arbitrary"), name=name,
    )(x, nw, dres, *dparts, *wparts)


def _matmul_tn(a, b, name):
    T, K = a.shape
    N = b.shape[1]
    tt = min(T, 512)
    tn = min(N, 1024)

    def body(a_ref, b_ref, o_ref):
        @pl.when(pl.program_id(1) == 0)
        def _():
            o_ref[...] = jnp.zeros_like(o_ref)

        o_ref[...] += _dot_tn(a_ref[...], b_ref[...])

    return pl.pallas_call(
        body, grid=(N // tn, T // tt),
        in_specs=[pl.BlockSpec((tt, K), lambda n, t: (t, 0)), pl.BlockSpec((tt, tn), lambda n, t: (t, n))],
        out_specs=pl.BlockSpec((K, tn), lambda n, t: (0, n)),
        out_shape=jax.ShapeDtypeStruct((K, N), F32),
        compiler_params=_params("parallel", "arbitrary"), name=name,
    )(a, b)


def _out_fwd(o, z, w, wout, xres, gs, gate_first, name):
    T = o.shape[0]
    tt = min(T, 256)
    wide = w.shape[1] == D_INNER

    def body(o_ref, z_ref, w_ref, wout_ref, x_ref, out_ref, yn):
        for g0 in range(0, D_INNER, gs):
            sl = slice(g0, g0 + gs)
            og, zg = o_ref[:, sl], z_ref[:, sl]
            wg = w_ref[:, sl] if wide else w_ref[...]
            if gate_first:
                u = og * _silu(zg)
                r = lax.rsqrt(jnp.mean(u * u, axis=-1, keepdims=True) + EPS)
                yn[:, sl] = _mx(u * r * wg)
            else:
                r = lax.rsqrt(jnp.mean(og * og, axis=-1, keepdims=True) + EPS)
                yn[:, sl] = _mx(og * r * wg * _silu(zg))
        out_ref[...] = x_ref[...] + jnp.dot(yn[...], wout_ref[...], preferred_element_type=F32)

    row = lambda width: pl.BlockSpec((tt, width), lambda i: (i, 0))
    full = lambda a: pl.BlockSpec(a.shape, lambda i: (0, 0))
    return pl.pallas_call(
        body, grid=(T // tt,),
        in_specs=[row(D_INNER), row(D_INNER), full(w), full(wout), row(D_MODEL)],
        out_specs=row(D_MODEL),
        out_shape=jax.ShapeDtypeStruct((T, D_MODEL), F32),
        scratch_shapes=[pltpu.VMEM((tt, D_INNER), MXU_DTYPE)],
        compiler_params=_params("parallel"), name=name,
    )(o, z, w, wout, xres)


def _out_bwd(dx, o, z, w, wout, gs, gate_first, name):
    T = o.shape[0]
    tt = min(T, 256)
    wide = w.shape[1] == D_INNER

    def body(dx_ref, o_ref, z_ref, w_ref, wout_ref, do_ref, dz_ref, dw_ref, yn_ref):
        @pl.when(pl.program_id(0) == 0)
        def _():
            dw_ref[...] = jnp.zeros_like(dw_ref)

        dyn = _dot_nt(dx_ref[...], wout_ref[...])
        dw_acc = jnp.zeros((1, gs), F32)
        for g0 in range(0, D_INNER, gs):
            sl = slice(g0, g0 + gs)
            og, zg, dg = o_ref[:, sl], z_ref[:, sl], dyn[:, sl]
            wg = w_ref[:, sl] if wide else w_ref[...]
            sz = _silu(zg)
            if gate_first:
                u = og * sz
                r = lax.rsqrt(jnp.mean(u * u, axis=-1, keepdims=True) + EPS)
                uh = u * r
                yn_ref[:, sl] = _mx(uh * wg)
                dw_g = jnp.sum(dg * uh, axis=0, keepdims=True)
                duh = dg * wg
                du = r * (duh - uh * jnp.mean(duh * uh, axis=-1, keepdims=True))
                do_ref[:, sl] = du * sz
                dz_ref[:, sl] = du * og * _dsilu(zg)
            else:
                r = lax.rsqrt(jnp.mean(og * og, axis=-1, keepdims=True) + EPS)
                oh = og * r
                yn_ref[:, sl] = _mx(oh * wg * sz)
                dw_g = jnp.sum(dg * oh * sz, axis=0, keepdims=True)
                doh = dg * wg * sz
                dz_ref[:, sl] = dg * oh * wg * _dsilu(zg)
                do_ref[:, sl] = r * (doh - oh * jnp.mean(doh * oh, axis=-1, keepdims=True))
            if wide:
                dw_ref[:, sl] += dw_g
            else:
                dw_acc = dw_acc + dw_g
        if not wide:
            dw_ref[...] += dw_acc

    row = lambda width: pl.BlockSpec((tt, width), lambda i: (i, 0))
    full = lambda a: pl.BlockSpec(a.shape, lambda i: (0, 0))
    return pl.pallas_call(
        body, grid=(T // tt,),
        in_specs=[row(D_MODEL), row(D_INNER), row(D_INNER), full(w), full(wout)],
        out_specs=[row(D_INNER), row(D_INNER), full(w), row(D_INNER)],
        out_shape=[jax.ShapeDtypeStruct((T, D_INNER), F32), jax.ShapeDtypeStruct((T, D_INNER), F32),
                   jax.ShapeDtypeStruct(w.shape, F32), jax.ShapeDtypeStruct((T, D_INNER), MXU_DTYPE)],
        compiler_params=_params("arbitrary"), name=name,
    )(dx, o, z, w, wout)


HALO = 8


def _conv_fwd(pre, w, b, l2, scale, name):
    T, C = pre.shape
    tt = min(T, 512)
    tc = min(C, 1024)

    def body(pre_ref, halo_ref, w_ref, b_ref, out_ref, P):
        i = pl.program_id(0)
        P[0:HALO, :] = jnp.where(i > 0, halo_ref[...], 0.0)
        P[HALO:HALO + tt, :] = pre_ref[...]
        acc = b_ref[...] + w_ref[0:1, :] * P[pl.ds(HALO - 3, tt), :]
        for j in range(1, CONV_K):
            acc = acc + w_ref[j:j + 1, :] * P[pl.ds(HALO - 3 + j, tt), :]
        s = _silu(acc)
        if l2:
            for g0 in range(0, tc, GDN_DK):
                sg = s[:, g0:g0 + GDN_DK]
                rr = lax.rsqrt(jnp.sum(sg * sg, axis=-1, keepdims=True) + EPS)
                out_ref[:, g0:g0 + GDN_DK] = sg * rr * scale
        else:
            out_ref[...] = s

    return pl.pallas_call(
        body, grid=(T // tt, C // tc),
        in_specs=[pl.BlockSpec((tt, tc), lambda i, j: (i, j)),
                  pl.BlockSpec((HALO, tc), lambda i, j: (jnp.maximum(i * (tt // HALO) - 1, 0), j)),
                  pl.BlockSpec((CONV_K, tc), lambda i, j: (0, j)),
                  pl.BlockSpec((1, tc), lambda i, j: (0, j))],
        out_specs=pl.BlockSpec((tt, tc), lambda i, j: (i, j)),
        out_shape=jax.ShapeDtypeStruct((T, C), F32),
        scratch_shapes=[pltpu.VMEM((HALO + tt, tc), F32)],
        compiler_params=_params("parallel", "parallel"), name=name,
    )(pre, pre, w, b)


def _conv_bwd(pre, w, b, dpost, l2, scale, name):
    T, C = pre.shape
    tt = min(T, 512)
    tc = min(C, 1024)
    nT = T // tt
    ext = tt + HALO

    def body(pre_ref, hp_ref, hn_ref, dpost_ref, dn_ref, w_ref, b_ref, dpre_ref, dw_ref, db_ref, P, Q):
        i = pl.program_id(1)

        @pl.when(i == 0)
        def _():
            dw_ref[...] = jnp.zeros_like(dw_ref)
            db_ref[...] = jnp.zeros_like(db_ref)

        P[0:HALO, :] = jnp.where(i > 0, hp_ref[...], 0.0)
        P[HALO:HALO + tt, :] = pre_ref[...]
        P[HALO + tt:HALO + ext, :] = hn_ref[...]
        cpre = b_ref[...] + w_ref[0:1, :] * P[pl.ds(HALO - 3, ext), :]
        for j in range(1, CONV_K):
            cpre = cpre + w_ref[j:j + 1, :] * P[pl.ds(HALO - 3 + j, ext), :]
        dy = jnp.concatenate([dpost_ref[...], dn_ref[...]], axis=0)
        rows = lax.broadcasted_iota(jnp.int32, (ext, 1), 0)
        valid = jnp.logical_or(rows < tt, i < nT - 1)
        s = _silu(cpre)
        ds_c = _dsilu(cpre)
        if l2:
            for g0 in range(0, tc, GDN_DK):
                sl = slice(g0, g0 + GDN_DK)
                sg, dg = s[:, sl], dy[:, sl]
                rr = lax.rsqrt(jnp.sum(sg * sg, axis=-1, keepdims=True) + EPS)
                yh = sg * rr
                dsg = scale * rr * (dg - yh * jnp.sum(dg * yh, axis=-1, keepdims=True))
                Q[:, sl] = jnp.where(valid, dsg * ds_c[:, sl], 0.0)
        else:
            Q[...] = jnp.where(valid, dy * ds_c, 0.0)
        dpre = w_ref[0:1, :] * Q[pl.ds(3, tt), :]
        for j in range(1, CONV_K):
            dpre = dpre + w_ref[j:j + 1, :] * Q[pl.ds(3 - j, tt), :]
        dpre_ref[...] = dpre
        dyc = Q[0:tt, :]
        for j in range(CONV_K):
            dw_ref[j:j + 1, :] += jnp.sum(dyc * P[pl.ds(HALO - 3 + j, tt), :], axis=0, keepdims=True)
        db_ref[...] += jnp.sum(dyc, axis=0, keepdims=True)

    tile = pl.BlockSpec((tt, tc), lambda j, i: (i, j))
    prev = pl.BlockSpec((HALO, tc), lambda j, i: (jnp.maximum(i * (tt // HALO) - 1, 0), j))
    nxt = pl.BlockSpec((HALO, tc), lambda j, i: (jnp.minimum((i + 1) * (tt // HALO), T // HALO - 1), j))
    return pl.pallas_call(
        body, grid=(C // tc, nT),
        in_specs=[tile, prev, nxt, tile, nxt,
                  pl.BlockSpec((CONV_K, tc), lambda j, i: (0, j)), pl.BlockSpec((1, tc), lambda j, i: (0, j))],
        out_specs=[tile, pl.BlockSpec((CONV_K, tc), lambda j, i: (0, j)), pl.BlockSpec((1, tc), lambda j, i: (0, j))],
        out_shape=[jax.ShapeDtypeStruct((T, C), F32), jax.ShapeDtypeStruct((CONV_K, C), F32),
                   jax.ShapeDtypeStruct((1, C), F32)],
        scratch_shapes=[pltpu.VMEM((HALO + ext, tc), F32), pltpu.VMEM((ext, tc), F32)],
        compiler_params=_params("parallel", "arbitrary"), name=name,
    )(pre, pre, pre, dpost, dpost, w, b)


GDN_LOCKSTEP_CHUNKS = 4
GDN_SCAN_HEADS = 8


def _inv_unit_lower_many(nms, eye, n):
    xs = [jnp.where(eye, 1.0, 0.0) - nm for nm in nms]
    ps = list(nms)
    k = 2
    while k < n:
        ps = [_dot(p, p) for p in ps]
        xs = [x + _dot(x, p) for x, p in zip(xs, ps)]
        k *= 2
    return xs


def _gdn_prep(q, k, v, araw, braw, alog, dtb, name):
    T = q.shape[0]
    C = GDN_CHUNK
    tt = min(T, 512)
    cpt, nC = tt // C, T // C
    grp = min(cpt, GDN_LOCKSTEP_CHUNKS)

    def body(alog_ref, dtb_ref, q_ref, k_ref, v_ref, a_ref, b_ref,
             u_ref, w_ref, pm_ref, ti_ref, g_ref, beta_ref, gc_ref):
        j = pl.program_id(0)
        tri, strict, eye, r_i, c_i = _masks(C)
        upper = jnp.where(r_i <= c_i, 1.0, 0.0)
        gcs, bts = [], []
        for hh in range(2):
            h = 2 * j + hh
            g = -jnp.exp(alog_ref[h]) * _softplus(a_ref[hh] + dtb_ref[h])
            bt = _sigmoid(b_ref[hh])
            gc = _dot_hi(g, upper)
            g_ref[hh], beta_ref[hh], gc_ref[hh] = g, bt, gc
            gcs.append(gc)
            bts.append(bt)
        for c0 in range(0, cpt, grp):
            cs = list(range(c0, c0 + grp))
            inst = [(c, hh) for c in cs for hh in range(2)]
            rows = {c: slice(c * C, (c + 1) * C) for c in cs}
            qc = {c: q_ref[rows[c], :] for c in cs}
            kc = {c: k_ref[rows[c], :] for c in cs}
            kk = {c: _dot_nt(kc[c], kc[c]) for c in cs}
            qk = {c: _dot_nt(qc[c], kc[c]) for c in cs}
            gcr = [gcs[hh][c:c + 1, :] for c, hh in inst]
            gcc = [_col(r, eye) for r in gcr]
            bc = [_col(bts[hh][c:c + 1, :], eye) for c, hh in inst]
            lm = [jnp.exp(jnp.where(tri, cc - r, -1e30)) for cc, r in zip(gcc, gcr)]
            nm = [jnp.where(strict, kk[c] * b * l, 0.0) for (c, hh), b, l in zip(inst, bc, lm)]
            tinv = _inv_unit_lower_many(nm, eye, C)
            rhs = [jnp.concatenate([v_ref[rows[c], hh * GDN_DK:(hh + 1) * GDN_DK] * b, kc[c] * (b * jnp.exp(cc))], axis=1)
                   for (c, hh), b, cc in zip(inst, bc, gcc)]
            sol = [_dot(t, r) for t, r in zip(tinv, rhs)]
            for (c, hh), s, t, l in zip(inst, sol, tinv, lm):
                hs = slice(hh * GDN_DK, (hh + 1) * GDN_DK)
                u_ref[rows[c], hs] = s[:, :GDN_DK]
                w_ref[rows[c], hs] = _mx(s[:, GDN_DK:])
                pm_ref[hh, c] = _mx(jnp.where(tri, qk[c] * l, 0.0))
                ti_ref[hh, c] = _mx(t)

    smem = pl.BlockSpec(memory_space=pltpu.SMEM)
    rows_spec = pl.BlockSpec((2, cpt, C), lambda j, i: (j, i, 0))
    qk_spec = pl.BlockSpec((tt, GDN_DK), lambda j, i: (i, j))
    v_spec = pl.BlockSpec((tt, 2 * GDN_DK), lambda j, i: (i, j))
    cc_spec = pl.BlockSpec((2, cpt, C, C), lambda j, i: (j, i, 0, 0))
    rows_shape = jax.ShapeDtypeStruct((GDN_HV, nC, C), F32)
    cc_shape = jax.ShapeDtypeStruct((GDN_HV, nC, C, C), MXU_DTYPE)
    return pl.pallas_call(
        body, grid=(GDN_HV // 2, T // tt),
        in_specs=[smem, smem, qk_spec, qk_spec, v_spec, rows_spec, rows_spec],
        out_specs=[v_spec, v_spec, cc_spec, cc_spec, rows_spec, rows_spec, rows_spec],
        out_shape=[jax.ShapeDtypeStruct((T, D_INNER), F32), jax.ShapeDtypeStruct((T, D_INNER), MXU_DTYPE),
                   cc_shape, cc_shape, rows_shape, rows_shape, rows_shape],
        compiler_params=_params("parallel", "parallel"), name=name,
    )(alog, dtb, q, k, v, araw, braw)


def _gdn_decays(gc_ref, h, c, eye, C):
    gcr = gc_ref[h, pl.ds(c, 1), :]
    gcc = _col(gcr, eye)
    glast = gcr[:, C - 1:C]
    return jnp.exp(gcc), jnp.exp(glast - gcc), jnp.exp(glast)


def _gdn_state_fwd(q, k, u, w, pm, gc, name):
    T = q.shape[0]
    C = GDN_CHUNK
    HG = GDN_SCAN_HEADS
    tt = min(T, 512)
    cpt, nC = tt // C, T // C

    def body(q_ref, k_ref, u_ref, w_ref, pm_ref, gc_ref, o_ref, vn_ref, sall_ref, S):
        @pl.when(pl.program_id(1) == 0)
        def _():
            S[...] = jnp.zeros_like(S)

        eye = _masks(C)[2]
        heads = list(range(HG))

        def chunk(c, carry):
            rows = pl.ds(pl.multiple_of(c * C, C), C)
            hs = [slice(h * GDN_DK, (h + 1) * GDN_DK) for h in heads]
            qs = [slice((h // 2) * GDN_DK, (h // 2 + 1) * GDN_DK) for h in heads]
            dec = [_gdn_decays(gc_ref, h, c, eye, C) for h in heads]
            sv = [S[h] for h in heads]
            for h in heads:
                sall_ref[h, c] = sv[h]
            ws = [_dot(w_ref[rows, hs[h]], sv[h]) for h in heads]
            qsv = [_dot(q_ref[rows, qs[h]], sv[h]) for h in heads]
            vn = [u_ref[rows, hs[h]] - ws[h] for h in heads]
            pv = [_dot(pm_ref[h, c], vn[h]) for h in heads]
            kv = [_dot_tn(k_ref[rows, qs[h]], vn[h] * dec[h][1]) for h in heads]
            for h in heads:
                vn_ref[rows, hs[h]] = _mx(vn[h])
                o_ref[rows, hs[h]] = qsv[h] * dec[h][0] + pv[h]
                S[h] = sv[h] * dec[h][2] + kv[h]
            return carry

        lax.fori_loop(0, cpt, chunk, 0)

    qk_spec = pl.BlockSpec((tt, HG // 2 * GDN_DK), lambda g, i: (i, g))
    v_spec = pl.BlockSpec((tt, HG * GDN_DK), lambda g, i: (i, g))
    return pl.pallas_call(
        body, grid=(GDN_HV // HG, T // tt),
        in_specs=[qk_spec, qk_spec, v_spec, v_spec,
                  pl.BlockSpec((HG, cpt, C, C), lambda g, i: (g, i, 0, 0)),
                  pl.BlockSpec((HG, cpt, C), lambda g, i: (g, i, 0))],
        out_specs=[v_spec, v_spec, pl.BlockSpec((HG, cpt, GDN_DK, GDN_DK), lambda g, i: (g, i, 0, 0))],
        out_shape=[jax.ShapeDtypeStruct((T, D_INNER), F32), jax.ShapeDtypeStruct((T, D_INNER), MXU_DTYPE),
                   jax.ShapeDtypeStruct((GDN_HV, nC, GDN_DK, GDN_DK), F32)],
        scratch_shapes=[pltpu.VMEM((HG, GDN_DK, GDN_DK), F32)],
        compiler_params=_params("parallel", "arbitrary"), name=name,
    )(q, k, u, w, pm, gc)


def _gdn_state_bwd(q, k, w, pm, vn, sall, gc, do, name):
    T = q.shape[0]
    C = GDN_CHUNK
    HG = GDN_SCAN_HEADS
    tt = min(T, 512)
    cpt, nC, nT = tt // C, T // C, T // tt

    def body(q_ref, k_ref, w_ref, pm_ref, vn_ref, sall_ref, gc_ref, do_ref, dvn_ref, dkd_ref, dgl_ref, dS):
        @pl.when(pl.program_id(1) == 0)
        def _():
            dS[...] = jnp.zeros_like(dS)

        eye = _masks(C)[2]
        heads = list(range(HG))

        def chunk(ci, carry):
            c = cpt - 1 - ci
            rows = pl.ds(pl.multiple_of(c * C, C), C)
            hs = [slice(h * GDN_DK, (h + 1) * GDN_DK) for h in heads]
            qs = [slice((h // 2) * GDN_DK, (h // 2 + 1) * GDN_DK) for h in heads]
            dec = [_gdn_decays(gc_ref, h, c, eye, C) for h in heads]
            dsn = [dS[h] for h in heads]
            doc = [do_ref[rows, hs[h]] for h in heads]
            kds = [_dot(k_ref[rows, qs[h]], dsn[h]) for h in heads]
            pdo = [_dot_tn(pm_ref[h, c], doc[h]) for h in heads]
            dkd = [_dot_nt(vn_ref[rows, hs[h]], dsn[h]) for h in heads]
            qdo = [_dot_tn(q_ref[rows, qs[h]], doc[h] * dec[h][0]) for h in heads]
            dvn = [pdo[h] + kds[h] * dec[h][1] for h in heads]
            wdv = [_dot_tn(w_ref[rows, hs[h]], dvn[h]) for h in heads]
            for h in heads:
                dgl = jnp.sum(jnp.sum(dsn[h] * sall_ref[h, c], axis=1, keepdims=True), axis=0, keepdims=True)
                dgl_ref[h, pl.ds(c, 1), :] = jnp.broadcast_to(dgl, (1, C))
                dvn_ref[rows, hs[h]] = dvn[h]
                dkd_ref[rows, hs[h]] = dkd[h]
                dS[h] = dsn[h] * dec[h][2] + qdo[h] - wdv[h]
            return carry

        lax.fori_loop(0, cpt, chunk, 0)

    rev = lambda i: nT - 1 - i
    qk_spec = pl.BlockSpec((tt, HG // 2 * GDN_DK), lambda g, i: (rev(i), g))
    v_spec = pl.BlockSpec((tt, HG * GDN_DK), lambda g, i: (rev(i), g))
    rows_spec = pl.BlockSpec((HG, cpt, C), lambda g, i: (g, rev(i), 0))
    return pl.pallas_call(
        body, grid=(GDN_HV // HG, nT),
        in_specs=[qk_spec, qk_spec, v_spec, pl.BlockSpec((HG, cpt, C, C), lambda g, i: (g, rev(i), 0, 0)), v_spec,
                  pl.BlockSpec((HG, cpt, GDN_DK, GDN_DK), lambda g, i: (g, rev(i), 0, 0)), rows_spec, v_spec],
        out_specs=[v_spec, v_spec, rows_spec],
        out_shape=[jax.ShapeDtypeStruct((T, D_INNER), F32), jax.ShapeDtypeStruct((T, D_INNER), F32),
                   jax.ShapeDtypeStruct((GDN_HV, nC, C), F32)],
        scratch_shapes=[pltpu.VMEM((HG, GDN_DK, GDN_DK), F32)],
        compiler_params=_params("parallel", "arbitrary"), name=name,
    )(q, k, w, pm, vn, sall, gc, do)


def _gdn_local_bwd(q, k, v, gc, beta, tinv, u, w, pm, vn, sall, do, dvn, dkd, dgl, name):
    T = q.shape[0]
    C = GDN_CHUNK
    tt = min(T, 512)
    cpt, nC = tt // C, T // C
    grp = min(cpt, GDN_LOCKSTEP_CHUNKS)

    def body(q_ref, k_ref, v_ref, gc_ref, b_ref, ti_ref, u_ref, w_ref, pm_ref, vn_ref, sall_ref, do_ref,
             dvn_ref, dkd_ref, dgl_ref, dq_ref, dk_ref, dv_ref, dg_ref, dbeta_ref, dgc_s):
        tri, strict, eye, r_i, c_i = _masks(C)
        lower = jnp.where(r_i >= c_i, 1.0, 0.0)
        lane = lax.broadcasted_iota(jnp.int32, (1, C), 1)
        rsum = lambda a: jnp.sum(a, axis=1, keepdims=True)
        for c0 in range(0, cpt, grp):
            cs = list(range(c0, c0 + grp))
            inst = [(c, hh) for c in cs for hh in range(2)]
            n = len(inst)
            rows = {c: slice(c * C, (c + 1) * C) for c in cs}
            hsl = [slice(hh * GDN_DK, (hh + 1) * GDN_DK) for c, hh in inst]
            qc = {c: q_ref[rows[c], :] for c in cs}
            kc = {c: k_ref[rows[c], :] for c in cs}
            kk = {c: _dot_nt(kc[c], kc[c]) for c in cs}
            gcr = [gc_ref[hh, c:c + 1, :] for c, hh in inst]
            gcc = [_col(r, eye) for r in gcr]
            bc = [_col(b_ref[hh, c:c + 1, :], eye) for c, hh in inst]
            lm = [jnp.exp(jnp.where(tri, cc - r, -1e30)) for cc, r in zip(gcc, gcr)]
            e_c = [jnp.exp(cc) for cc in gcc]
            el_c = [jnp.exp(r[:, C - 1:C] - cc) for cc, r in zip(gcc, gcr)]
            gl = [jnp.exp(r[:, C - 1:C]) for r in gcr]
            doc = [do_ref[rows[c], hsl[i]] for i, (c, hh) in enumerate(inst)]
            dvn = [dvn_ref[rows[c], hsl[i]] for i, (c, hh) in enumerate(inst)]
            sv = [sall_ref[hh, c] for c, hh in inst]
            aa = [_dot_nt(jnp.concatenate([_mx(doc[i]), _mx(dvn[i])], axis=0), sv[i]) for i in range(n)]
            dpm = [jnp.where(tri, _dot_nt(doc[i], vn_ref[rows[c], hsl[i]]), 0.0) for i, (c, hh) in enumerate(inst)]
            dqd = [a[:C] for a in aa]
            drhs = [_dot_tn(ti_ref[hh, c], jnp.concatenate([dvn[i], -aa[i][C:]], axis=1))
                    for i, (c, hh) in enumerate(inst)]
            sol = [jnp.concatenate([_mx(u_ref[rows[c], hsl[i]]), w_ref[rows[c], hsl[i]]], axis=1)
                   for i, (c, hh) in enumerate(inst)]
            dnm = [-jnp.where(strict, _dot_nt(drhs[i], sol[i]), 0.0) for i in range(n)]
            dkk = [dnm[i] * bc[i] * lm[i] for i in range(n)]
            dqk = [dpm[i] * lm[i] for i in range(n)]
            dq1 = [_dot(dqk[i], kc[c]) for i, (c, hh) in enumerate(inst)]
            dk1 = [_dot(dkk[i], kc[c]) for i, (c, hh) in enumerate(inst)]
            dk2 = [_dot_tn(dkk[i], kc[c]) for i, (c, hh) in enumerate(inst)]
            dk3 = [_dot_tn(dqk[i], qc[c]) for i, (c, hh) in enumerate(inst)]
            dq_acc = {c: jnp.zeros((C, GDN_DK), F32) for c in cs}
            dk_acc = {c: jnp.zeros((C, GDN_DK), F32) for c in cs}
            for i, (c, hh) in enumerate(inst):
                k_, q_, v_ = kc[c], qc[c], v_ref[rows[c], hsl[i]]
                dvb, dkbe = drhs[i][:, :GDN_DK], drhs[i][:, GDN_DK:]
                dkd = dkd_ref[rows[c], hsl[i]]
                kb = k_ * bc[i]
                dkb = dkbe * e_c[i]
                de_c = rsum(dkbe * kb) + rsum(dqd[i] * q_)
                del_c = rsum(dkd * k_)
                dbc = rsum(dnm[i] * kk[c] * lm[i]) + rsum(dkb * k_) + rsum(dvb * v_)
                dq_acc[c] = dq_acc[c] + dq1[i] + dqd[i] * e_c[i]
                dk_acc[c] = dk_acc[c] + dk1[i] + dk2[i] + dk3[i] + dkd * el_c[i] + dkb * bc[i]
                dv_ref[rows[c], hsl[i]] = dvb * bc[i]
                nm = jnp.where(strict, kk[c] * bc[i] * lm[i], 0.0)
                gm = dnm[i] * nm + dpm[i] * pm_ref[hh, c].astype(F32)
                dgc_col = rsum(gm) + de_c * e_c[i] - del_c * el_c[i]
                dglast = jnp.sum(del_c * el_c[i], axis=0, keepdims=True) + dgl_ref[hh, c:c + 1, 0:1] * gl[i]
                dgc_s[hh, c:c + 1, :] = (_row(dgc_col, eye) - jnp.sum(gm, axis=0, keepdims=True)
                                         + jnp.where(lane == C - 1, dglast, 0.0))
                dbeta_ref[hh, c:c + 1, :] = _row(dbc, eye)
            for c in cs:
                dq_ref[rows[c], :] = dq_acc[c]
                dk_ref[rows[c], :] = dk_acc[c]
        for hh in range(2):
            dg_ref[hh] = _dot_hi(dgc_s[hh], lower)

    rows_spec = pl.BlockSpec((2, cpt, C), lambda j, i: (j, i, 0))
    qk_spec = pl.BlockSpec((tt, GDN_DK), lambda j, i: (i, j))
    v_spec = pl.BlockSpec((tt, 2 * GDN_DK), lambda j, i: (i, j))
    cc_spec = pl.BlockSpec((2, cpt, C, C), lambda j, i: (j, i, 0, 0))
    rows_shape = jax.ShapeDtypeStruct((GDN_HV, nC, C), F32)
    return pl.pallas_call(
        body, grid=(GDN_HV // 2, T // tt),
        in_specs=[qk_spec, qk_spec, v_spec, rows_spec, rows_spec, cc_spec, v_spec, v_spec, cc_spec, v_spec,
                  pl.BlockSpec((2, cpt, GDN_DK, GDN_DK), lambda j, i: (j, i, 0, 0)), v_spec, v_spec, v_spec, rows_spec],
        out_specs=[qk_spec, qk_spec, v_spec, rows_spec, rows_spec],
        out_shape=[jax.ShapeDtypeStruct((T, GDN_HV // 2 * GDN_DK), F32),
                   jax.ShapeDtypeStruct((T, GDN_HV // 2 * GDN_DK), F32),
                   jax.ShapeDtypeStruct((T, D_INNER), F32), rows_shape, rows_shape],
        scratch_shapes=[pltpu.VMEM((2, cpt, C), F32)],
        compiler_params=_params("parallel", "parallel"), name=name,
    )(q, k, v, gc, beta, tinv, u, w, pm, vn, sall, do, dvn, dkd, dgl)


def _gdn_gate_bwd(araw, braw, dg, dbeta, alog, dtb, name):
    H, T = araw.shape

    def body(a_ref, b_ref, dg_ref, dbt_ref, alog_ref, dtb_ref, da_ref, db_ref, dalog_ref, ddtb_ref):
        xa = a_ref[...] + dtb_ref[...]
        ea = jnp.exp(alog_ref[...])
        dgv = dg_ref[...]
        da = -dgv * ea * _sigmoid(xa)
        da_ref[...] = da
        dalog_ref[...] = jnp.sum(-dgv * ea * _softplus(xa), axis=1, keepdims=True)
        ddtb_ref[...] = jnp.sum(da, axis=1, keepdims=True)
        bt = _sigmoid(b_ref[...])
        db_ref[...] = dbt_ref[...] * bt * (1.0 - bt)

    return pl.pallas_call(
        body,
        out_shape=[jax.ShapeDtypeStruct((H, T), F32), jax.ShapeDtypeStruct((H, T), F32),
                   jax.ShapeDtypeStruct((H, 1), F32), jax.ShapeDtypeStruct((H, 1), F32)],
        compiler_params=pltpu.CompilerParams(vmem_limit_bytes=VMEM_LIMIT_BYTES), name=name,
    )(araw, braw, dg, dbeta, alog, dtb)


SSD_LOCKSTEP_CHUNKS = 2
SSD_LOCKSTEP_CHUNKS_BWD = 1


def _ssd_scan_fwd(xs, bm, cm, dtraw, alog, dtb, dskip, name):
    T = xs.shape[0]
    Q = SSD_CHUNK
    tt = min(T, 1024)
    cpt, nC = tt // Q, T // Q
    GW = SSD_R * SSD_P

    def body(alog_ref, dtb_ref, dsk_ref, xs_ref, b_ref, c_ref, dt_ref, y_ref, sall_ref, dto_ref, S, dt_s, acs_s):
        gi, i = pl.program_id(0), pl.program_id(1)

        @pl.when(i == 0)
        def _():
            S[...] = jnp.zeros_like(S)

        tri, _, eye, r_i, c_i = _masks(Q)
        upper = jnp.where(r_i <= c_i, 1.0, 0.0)
        for r in range(SSD_R):
            h = SSD_R * gi + r
            dt = _softplus(dt_ref[r] + dtb_ref[h])
            dto_ref[r] = dt
            dt_s[r] = dt
            acs_s[r] = _dot_hi(-jnp.exp(alog_ref[h]) * dt, upper)

        ps = [slice(r * SSD_P, (r + 1) * SSD_P) for r in range(SSD_R)]
        s_cur = [S[:, ps[r]] for r in range(SSD_R)]
        grp = min(cpt, SSD_LOCKSTEP_CHUNKS)
        for c0 in range(0, cpt, grp):
            cs = list(range(c0, c0 + grp))
            inst = [(c, r) for c in cs for r in range(SSD_R)]
            rows = {c: slice(c * Q, (c + 1) * Q) for c in cs}
            bc_ = {c: b_ref[rows[c], :] for c in cs}
            cc_ = {c: c_ref[rows[c], :] for c in cs}
            cb = {c: _dot_nt(cc_[c], bc_[c]) for c in cs}
            xr = [xs_ref[rows[c], ps[r]] for c, r in inst]
            acr = [acs_s[r, c:c + 1, :] for c, r in inst]
            acc = [_col_bcast(a, Q) for a in acr]
            dtc = [_col_bcast(dt_s[r, c:c + 1, :], Q)[:, :SSD_P] for c, r in inst]
            xd = [x * d for x, d in zip(xr, dtc)]
            mm = [cb[c] * jnp.exp(jnp.where(tri, acc[i] - acr[i], -1e30)) for i, (c, r) in enumerate(inst)]
            bct = {c: bc_[c].T for c in cs}
            st = [_dot(bct[c] * jnp.exp(acr[i][:, Q - 1:Q] - acr[i]), xd[i]) for i, (c, r) in enumerate(inst)]
            yd = [_dot(mm[i], xd[i]) for i in range(len(inst))]
            s_prev = []
            for i, (c, r) in enumerate(inst):
                s_prev.append(s_cur[r])
                s_cur[r] = s_cur[r] * jnp.exp(acr[i][:, Q - 1:Q]) + st[i]
            yo = [_dot(cc_[c] * jnp.exp(acc[i]), s_prev[i]) for i, (c, r) in enumerate(inst)]
            for i, (c, r) in enumerate(inst):
                sall_ref[0, c, :, ps[r]] = s_prev[i]
                y_ref[rows[c], ps[r]] = yd[i] + yo[i] + dsk_ref[SSD_R * gi + r] * xr[i]
        for r in range(SSD_R):
            S[:, ps[r]] = s_cur[r]

    smem = pl.BlockSpec(memory_space=pltpu.SMEM)
    rows_spec = pl.BlockSpec((SSD_R, cpt, Q), lambda g, i: (g, i, 0))
    return pl.pallas_call(
        body, grid=(SSD_G, T // tt),
        in_specs=[smem, smem, smem,
                  pl.BlockSpec((tt, GW), lambda g, i: (i, g)), pl.BlockSpec((tt, SSD_N), lambda g, i: (i, g)),
                  pl.BlockSpec((tt, SSD_N), lambda g, i: (i, g)), rows_spec],
        out_specs=[pl.BlockSpec((tt, GW), lambda g, i: (i, g)),
                   pl.BlockSpec((1, cpt, SSD_N, GW), lambda g, i: (g, i, 0, 0)), rows_spec],
        out_shape=[jax.ShapeDtypeStruct((T, D_INNER), F32), jax.ShapeDtypeStruct((SSD_G, nC, SSD_N, GW), F32),
                   jax.ShapeDtypeStruct((SSD_H, nC, Q), F32)],
        scratch_shapes=[pltpu.VMEM((SSD_N, GW), F32), pltpu.VMEM((SSD_R, cpt, Q), F32),
                        pltpu.VMEM((SSD_R, cpt, Q), F32)],
        compiler_params=_params("parallel", "arbitrary"), name=name,
    )(alog, dtb, dskip, xs, bm, cm, dtraw)


def _ssd_scan_bwd(xs, bm, cm, dt, sall, dy, alog, dskip, name):
    T = xs.shape[0]
    Q = SSD_CHUNK
    tt = min(T, 1024)
    cpt, nC, nT = tt // Q, T // Q, T // tt
    GW = SSD_R * SSD_P

    def body(alog_ref, dsk_ref, xs_ref, b_ref, c_ref, dt_ref, sall_ref, dy_ref,
             dxs_ref, db_ref, dc_ref, da_ref, ddt_ref, dd_ref, dS, acs_s, dacs_s, ddt_s, dd_s):
        gi, i = pl.program_id(0), pl.program_id(1)

        @pl.when(i == 0)
        def _():
            dS[...] = jnp.zeros_like(dS)

        tri, _, eye, r_i, c_i = _masks(Q)
        upper = jnp.where(r_i <= c_i, 1.0, 0.0)
        lower = jnp.where(r_i >= c_i, 1.0, 0.0)
        lane = lax.broadcasted_iota(jnp.int32, (1, Q), 1)
        for r in range(SSD_R):
            acs_s[r] = _dot_hi(-jnp.exp(alog_ref[SSD_R * gi + r]) * dt_ref[r], upper)

        ps = [slice(r * SSD_P, (r + 1) * SSD_P) for r in range(SSD_R)]
        ds_cur = [dS[:, ps[r]] for r in range(SSD_R)]
        grp = min(cpt, SSD_LOCKSTEP_CHUNKS_BWD)
        csum = lambda a: jnp.sum(a, axis=0, keepdims=True)
        tsum = lambda a: jnp.sum(csum(a), axis=1, keepdims=True)
        ones8 = jnp.ones((8, SSD_P), F32)
        for c0 in range(cpt - grp, -1, -grp):
            cs = list(range(c0 + grp - 1, c0 - 1, -1))
            inst = [(c, r) for c in cs for r in range(SSD_R)]
            n = len(inst)
            rows = {c: slice(c * Q, (c + 1) * Q) for c in cs}
            bc_ = {c: b_ref[rows[c], :] for c in cs}
            cc_ = {c: c_ref[rows[c], :] for c in cs}
            cb = {c: _dot_nt(cc_[c], bc_[c]) for c in cs}
            xr = [xs_ref[rows[c], ps[r]] for c, r in inst]
            dyr = [dy_ref[rows[c], ps[r]] for c, r in inst]
            acr = [acs_s[r, c:c + 1, :] for c, r in inst]
            dtr = [dt_ref[r, c:c + 1, :] for c, r in inst]
            acc = [_col_bcast(a, Q) for a in acr]
            dtb = [_col_bcast(d, Q) for d in dtr]
            al = [a[:, Q - 1:Q] for a in acr]
            e_c = [jnp.exp(a) for a in acc]
            dl_c = [jnp.exp(al[i] - acc[i]) for i in range(n)]
            e_r = [jnp.exp(a) for a in acr]
            dl_r = [jnp.exp(al[i] - acr[i]) for i in range(n)]
            gl = [jnp.exp(a) for a in al]
            bct = {c: bc_[c].T for c in cs}
            cct = {c: cc_[c].T for c in cs}
            cbt = {c: _dot_nt(bc_[c], cc_[c]) for c in cs}
            lm = [jnp.exp(jnp.where(tri, acc[i] - acr[i], -1e30)) for i in range(n)]
            lmt = [jnp.exp(jnp.where(r_i <= c_i, acr[i] - acc[i], -1e30)) for i in range(n)]
            mm = [cb[c] * lm[i] for i, (c, r) in enumerate(inst)]
            mmt = [cbt[c] * lmt[i] for i, (c, r) in enumerate(inst)]
            sr = [sall_ref[0, c, :, ps[r]] for c, r in inst]
            dmm0 = [jnp.where(tri, _dot_nt(dyr[i], xr[i]), 0.0) for i in range(n)]
            dmm0t = [jnp.where(r_i <= c_i, _dot_nt(xr[i], dyr[i]), 0.0) for i in range(n)]
            dxd1 = [_dot(mmt[i], dyr[i]) for i in range(n)]
            dce = [_dot_nt(dyr[i], sr[i]) for i in range(n)]
            dcet = [_dot_nt(sr[i], dyr[i]) for i in range(n)]
            cdy = [_dot(cct[c] * e_r[i], dyr[i]) for i, (c, r) in enumerate(inst)]
            dsn = []
            for i, (c, r) in enumerate(inst):
                dsn.append(ds_cur[r])
                ds_cur[r] = gl[i] * ds_cur[r] + cdy[i]
            dxd = [dxd1[i] + _dot(bc_[c] * dl_c[i], dsn[i]) for i, (c, r) in enumerate(inst)]
            dbd0 = [_dot_nt(xr[i], dsn[i]) for i in range(n)]
            dbd0t = [_dot_nt(dsn[i], xr[i]) for i in range(n)]
            dcb = {c: jnp.zeros((Q, Q), F32) for c in cs}
            dcbt = {c: jnp.zeros((Q, Q), F32) for c in cs}
            db_acc = {c: jnp.zeros((Q, SSD_N), F32) for c in cs}
            dc_acc = {c: jnp.zeros((Q, SSD_N), F32) for c in cs}
            for i, (c, r) in enumerate(inst):
                dgl = tsum(dsn[i] * sr[i])
                dc_acc[c] = dc_acc[c] + dce[i] * e_c[i]
                db_acc[c] = db_acc[c] + dbd0[i] * (dtb[i] * dl_c[i])
                dcb[c] = dcb[c] + dmm0[i] * (lm[i] * dtr[i])
                dcbt[c] = dcbt[c] + dmm0t[i] * (lmt[i] * dtb[i])
                csum_gm0 = csum(dmm0[i] * mm[i])
                rsum_gm = csum(dmm0t[i] * mmt[i] * dtb[i])
                r_de = csum(dcet[i] * cct[c]) * e_r[i]
                r_dl = csum(dbd0t[i] * bct[c]) * dl_r[i]
                dalast = jnp.sum(r_dl * dtr[i], axis=1, keepdims=True) + dgl * gl[i]
                dacs_s[r, c:c + 1, :] = (rsum_gm + r_de - (r_dl + csum_gm0) * dtr[i]
                                         + jnp.where(lane == Q - 1, dalast, 0.0))
                ddt_s[r, c:c + 1, :] = csum_gm0 + r_dl
                dd_s[r, c:c + 1, :] = _dot_nt(ones8, dyr[i] * xr[i])[0:1]
                dxs_ref[rows[c], ps[r]] = dxd[i] * dtb[i][:, :SSD_P] + dsk_ref[SSD_R * gi + r] * dyr[i]
            for c in cs:
                dc_ref[rows[c], :] = dc_acc[c] + _dot(dcb[c], bc_[c])
                db_ref[rows[c], :] = db_acc[c] + _dot(dcbt[c], cc_[c])
        for r in range(SSD_R):
            dS[:, ps[r]] = ds_cur[r]
        for r in range(SSD_R):
            da_ref[r] = _dot_hi(dacs_s[r], lower)
            ddt_ref[r] = ddt_s[r]
            dd_ref[r] = dd_s[r]

    rev = lambda i: nT - 1 - i
    smem = pl.BlockSpec(memory_space=pltpu.SMEM)
    rows_spec = pl.BlockSpec((SSD_R, cpt, Q), lambda g, i: (g, rev(i), 0))
    x_spec = pl.BlockSpec((tt, GW), lambda g, i: (rev(i), g))
    n_spec = pl.BlockSpec((tt, SSD_N), lambda g, i: (rev(i), g))
    rows_shape = jax.ShapeDtypeStruct((SSD_H, nC, Q), F32)
    return pl.pallas_call(
        body, grid=(SSD_G, nT),
        in_specs=[smem, smem, x_spec, n_spec, n_spec, rows_spec,
                  pl.BlockSpec((1, cpt, SSD_N, GW), lambda g, i: (g, rev(i), 0, 0)), x_spec],
        out_specs=[x_spec, n_spec, n_spec, rows_spec, rows_spec, rows_spec],
        out_shape=[jax.ShapeDtypeStruct((T, D_INNER), F32), jax.ShapeDtypeStruct((T, SSD_G * SSD_N), F32),
                   jax.ShapeDtypeStruct((T, SSD_G * SSD_N), F32), rows_shape, rows_shape, rows_shape],
        scratch_shapes=[pltpu.VMEM((SSD_N, GW), F32)] + [pltpu.VMEM((SSD_R, cpt, Q), F32)] * 4,
        compiler_params=_params("parallel", "arbitrary"), name=name,
    )(alog, dskip, xs, bm, cm, dt, sall, dy)


def _ssd_gate_bwd(dtraw, dt, da, ddt_direct, ddrow, alog, dtb, name):
    H, T = dtraw.shape

    def body(raw_ref, dt_ref, da_ref, ddt_ref, dd_ref, alog_ref, dtb_ref, draw_ref, dalog_ref, ddtb_ref, dD_ref):
        a = -jnp.exp(alog_ref[...])
        dav = da_ref[...]
        ddt = ddt_ref[...] + dav * a
        draw = ddt * _sigmoid(raw_ref[...] + dtb_ref[...])
        draw_ref[...] = draw
        dalog_ref[...] = jnp.sum(dav * dt_ref[...], axis=1, keepdims=True) * a
        ddtb_ref[...] = jnp.sum(draw, axis=1, keepdims=True)
        dD_ref[...] = jnp.sum(dd_ref[...], axis=1, keepdims=True)

    return pl.pallas_call(
        body,
        out_shape=[jax.ShapeDtypeStruct((H, T), F32)] + [jax.ShapeDtypeStruct((H, 1), F32)] * 3,
        compiler_params=pltpu.CompilerParams(vmem_limit_bytes=VMEM_LIMIT_BYTES), name=name,
    )(dtraw, dt, da, ddt_direct, ddrow, alog, dtb)


def _final_loss(x, fw, tgt, name):
    T = x.shape[0]
    tt = min(T, 512)
    nT = T // tt

    def body(x_ref, w_ref, t_ref, dx_ref, dw_ref, loss_ref, acc):
        i = pl.program_id(0)

        @pl.when(i == 0)
        def _():
            dw_ref[...] = jnp.zeros_like(dw_ref)
            acc[...] = jnp.zeros_like(acc)

        xv = x_ref[...]
        r = lax.rsqrt(jnp.mean(xv * xv, axis=-1, keepdims=True) + EPS)
        xh = xv * r
        err = xh * w_ref[...] - t_ref[...]
        acc[...] += jnp.sum(err * err, axis=0, keepdims=True)
        dout = err * (1.0 / D_MODEL)
        dw_ref[...] += jnp.sum(dout * xh, axis=0, keepdims=True)
        dxn = dout * w_ref[...]
        dx_ref[...] = r * (dxn - xh * jnp.mean(dxn * xh, axis=-1, keepdims=True))

        @pl.when(i == nT - 1)
        def _():
            loss_ref[...] = (0.5 / D_MODEL) * jnp.sum(acc[...], axis=1, keepdims=True)

    row = pl.BlockSpec((tt, D_MODEL), lambda i: (i, 0))
    vec = pl.BlockSpec((1, D_MODEL), lambda i: (0, 0))
    return pl.pallas_call(
        body, grid=(nT,),
        in_specs=[row, vec, row],
        out_specs=[row, vec, pl.BlockSpec((1, 1), lambda i: (0, 0))],
        out_shape=[jax.ShapeDtypeStruct((T, D_MODEL), F32), jax.ShapeDtypeStruct((1, D_MODEL), F32),
                   jax.ShapeDtypeStruct((1, 1), F32)],
        scratch_shapes=[pltpu.VMEM((1, D_MODEL), F32)],
        compiler_params=_params("arbitrary"), name=name,
    )(x, fw, tgt)


def _adamw(parts, w, m, v, name):
    R, C = w.shape
    tr = 128 if R % 128 == 0 else R

    def body(p_ref, w_ref, m_ref, v_ref, g_ref, d_ref, nm_ref, nv_ref):
        g = p_ref[0]
        for s in range(1, N_DEV):
            g = g + p_ref[s]
        mn = ADAM_B1 * m_ref[...] + (1.0 - ADAM_B1) * g
        vn = ADAM_B2 * v_ref[...] + (1.0 - ADAM_B2) * (g * g)
        mh = mn / (1.0 - ADAM_B1 ** ADAM_STEP)
        vh = vn / (1.0 - ADAM_B2 ** ADAM_STEP)
        g_ref[...] = g
        d_ref[...] = -ADAM_LR * (mh / (jnp.sqrt(vh) + ADAM_EPS) + ADAM_WD * w_ref[...])
        nm_ref[...] = mn
        nv_ref[...] = vn

    blk = pl.BlockSpec((tr, C), lambda i: (i, 0))
    return pl.pallas_call(
        body, grid=(R // tr,),
        in_specs=[pl.BlockSpec((N_DEV, tr, C), lambda i: (0, i, 0)), blk, blk, blk],
        out_specs=[blk] * 4,
        out_shape=[jax.ShapeDtypeStruct((R, C), F32)] * 4,
        compiler_params=_params("parallel"), name=name,
    )(parts, w, m, v)


def _me():
    x, y, c = lax.axis_index("x"), lax.axis_index("y"), lax.axis_index("c")
    return x, y, c


def _peer(d):
    x, y, c = _me()
    px = 1 - x if d & 4 else x
    py = 1 - y if d & 2 else y
    pc = 1 - c if d & 1 else c
    return (px, py, pc), 4 * px + 2 * py + pc


def _exchange(arrs, bcast, name):
    n = len(arrs)

    def body(*refs):
        ins, outs = refs[:n], refs[n:2 * n]
        ssem, rsem, lsem = refs[2 * n:]
        x, y, c = _me()
        me = 4 * x + 2 * y + c

        def src(a, dest):
            return ins[a] if bcast[a] else ins[a].at[dest]

        local = [pltpu.make_async_copy(src(a, me), outs[a].at[me], lsem.at[a]) for a in range(n)]
        for cp in local:
            cp.start()
        sends, recvs = [], []
        for a in range(n):
            for d in range(1, N_DEV):
                peer, pid = _peer(d)
                sends.append(pltpu.make_async_remote_copy(
                    src_ref=src(a, pid), dst_ref=outs[a].at[me], send_sem=ssem.at[a, d - 1],
                    recv_sem=rsem.at[a, d - 1], device_id=peer, device_id_type=MESH))
                recvs.append(pltpu.make_async_remote_copy(
                    src_ref=src(a, pid), dst_ref=outs[a].at[pid], send_sem=ssem.at[a, d - 1],
                    recv_sem=rsem.at[a, d - 1], device_id=peer, device_id_type=MESH))
        for cp in sends:
            cp.start()
        for cp in recvs:
            cp.wait_recv()
        for cp in sends:
            cp.wait_send()
        for cp in local:
            cp.wait()

    def out_shape(a, is_b):
        return jax.ShapeDtypeStruct((N_DEV,) + (a.shape if is_b else a.shape[1:]), a.dtype)

    anyspec = pl.BlockSpec(memory_space=pl.ANY)
    return pl.pallas_call(
        body,
        in_specs=[anyspec] * n, out_specs=[anyspec] * n,
        out_shape=[out_shape(a, b) for a, b in zip(arrs, bcast)],
        scratch_shapes=[pltpu.SemaphoreType.DMA((n, N_DEV - 1)), pltpu.SemaphoreType.DMA((n, N_DEV - 1)),
                        pltpu.SemaphoreType.DMA((n,))],
        name=name,
    )(*arrs)


def _to_rows(cols, chunk):
    T, H = cols.shape
    return cols.T.reshape(H, T // chunk, chunk)


def _from_rows(rows):
    return rows.T


def _pad_cols(a, width):
    return jnp.pad(a, ((0, 0), (0, width - a.shape[1])))


def _local_step(x, tgt, p):
    T = x.shape[0]
    zb = lambda n: jnp.zeros((1, n), F32)
    gw = p["gdn_w_in"]
    g_wparts = [gw[:, 0:1024], gw[:, 1024:2048], gw[:, 2048:4096], gw[:, 4096:6144], _pad_cols(gw[:, 6144:6176], PAD_W)]
    nw0, nw1 = p["norm_w"][0:1], p["norm_w"][1:2]
    h0, (q_pre, k_pre, v_pre, z0, ab) = _norm_inproj(x, nw0, g_wparts, "gdn_inproj")
    gcw = p["gdn_conv_w"]
    cw_q, cw_k, cw_v = gcw[:, 0:1024], gcw[:, 1024:2048], gcw[:, 2048:4096]
    q = _conv_fwd(q_pre, cw_q, zb(1024), True, GDN_DK ** -0.5, "gdn_conv_q")
    k = _conv_fwd(k_pre, cw_k, zb(1024), True, 1.0, "gdn_conv_k")
    v = _conv_fwd(v_pre, cw_v, zb(2048), False, 1.0, "gdn_conv_v")
    braw = _to_rows(ab[:, 0:GDN_HV], GDN_CHUNK)
    araw = _to_rows(ab[:, GDN_HV:2 * GDN_HV], GDN_CHUNK)
    g_alog, g_dtb = p["gdn_a_log"].reshape(-1), p["gdn_dt_bias"].reshape(-1)
    g_u, g_w, g_pm, g_ti, g_rows, beta_rows, gc_rows = _gdn_prep(q, k, v, araw, braw, g_alog, g_dtb, "gdn_prep")
    o0, g_vn, g_sall = _gdn_state_fwd(q, k, g_u, g_w, g_pm, gc_rows, "gdn_state_fwd")
    x1 = _out_fwd(o0, z0, p["gdn_norm_w"], p["gdn_w_out"], x, GDN_DK, False, "gdn_out")
    sw = p["ssd_w_in"]
    s_wparts = [sw[:, 0:2048], sw[:, 2048:4096], sw[:, 4096:5120], sw[:, 5120:6144], _pad_cols(sw[:, 6144:6176], PAD_W)]
    h1, (z1, xs_pre, b_pre, c_pre, dtp) = _norm_inproj(x1, nw1, s_wparts, "ssd_inproj")
    scw, scb = p["ssd_conv_w"], p["ssd_conv_b"]
    xs = _conv_fwd(xs_pre, scw[:, 0:2048], scb[:, 0:2048], False, 1.0, "ssd_conv_x")
    bm = _conv_fwd(b_pre, scw[:, 2048:3072], scb[:, 2048:3072], False, 1.0, "ssd_conv_b")
    cm = _conv_fwd(c_pre, scw[:, 3072:4096], scb[:, 3072:4096], False, 1.0, "ssd_conv_c")
    dtraw = _to_rows(dtp[:, 0:SSD_H], SSD_CHUNK)
    s_alog, s_dtb, s_d = p["ssd_a_log"].reshape(-1), p["ssd_dt_bias"].reshape(-1), p["ssd_d"].reshape(-1)
    y1, s_sall, dt_rows = _ssd_scan_fwd(xs, bm, cm, dtraw, s_alog, s_dtb, s_d, "ssd_scan_fwd")
    x2 = _out_fwd(y1, z1, p["ssd_norm_w"], p["ssd_w_out"], x1, D_INNER // SSD_G, True, "ssd_out")
    dx2, d_fw, loss = _final_loss(x2, p["final_norm_w"].reshape(1, -1), tgt, "final_loss")
    dy1, dz1, d_snw, yn1 = _out_bwd(dx2, y1, z1, p["ssd_norm_w"], p["ssd_w_out"], D_INNER // SSD_G, True, "ssd_out_bwd")
    d_swout = _matmul_tn(yn1, dx2, "ssd_wout_grad")
    dxs, dbm, dcm, da_rows, ddt_rows, dd_rows = _ssd_scan_bwd(xs, bm, cm, dt_rows, s_sall, dy1, s_alog, s_d, "ssd_scan_bwd")
    col = lambda a: a.reshape(-1, 1)
    dtraw_g, d_salog, d_sdtb, d_sd = _ssd_gate_bwd(
        dtraw.reshape(SSD_H, T), dt_rows.reshape(SSD_H, T), da_rows.reshape(SSD_H, T),
        ddt_rows.reshape(SSD_H, T), dd_rows.reshape(SSD_H, T), col(s_alog), col(s_dtb), "ssd_gate_bwd")
    dxs_pre, dcw_x, dcb_x = _conv_bwd(xs_pre, scw[:, 0:2048], scb[:, 0:2048], dxs, False, 1.0, "ssd_conv_x_bwd")
    db_pre, dcw_b, dcb_b = _conv_bwd(b_pre, scw[:, 2048:3072], scb[:, 2048:3072], dbm, False, 1.0, "ssd_conv_b_bwd")
    dc_pre, dcw_c, dcb_c = _conv_bwd(c_pre, scw[:, 3072:4096], scb[:, 3072:4096], dcm, False, 1.0, "ssd_conv_c_bwd")
    ddtp = _pad_cols(_from_rows(dtraw_g), PAD_W)
    s_dparts = [dz1, dxs_pre, db_pre, dc_pre, ddtp]
    dx1, d_nw1 = _inproj_bwd(x1, nw1, s_dparts, s_wparts, dx2, "ssd_inproj_bwd")
    s_dw = [_matmul_tn(h1, d, "ssd_win_grad_%d" % n) for n, d in enumerate(s_dparts)]
    d_swin = jnp.concatenate(s_dw[:4] + [s_dw[4][:, 0:SSD_H]], axis=1)
    do0, dz0, d_gnw, yn0 = _out_bwd(dx1, o0, z0, p["gdn_norm_w"], p["gdn_w_out"], GDN_DK, False, "gdn_out_bwd")
    d_gwout = _matmul_tn(yn0, dx1, "gdn_wout_grad")
    g_dvn, g_dkd, g_dgl = _gdn_state_bwd(q, k, g_w, g_pm, g_vn, g_sall, gc_rows, do0, "gdn_state_bwd")
    dq, dk, dv, dg_rows, dbeta_rows = _gdn_local_bwd(q, k, v, gc_rows, beta_rows, g_ti, g_u, g_w, g_pm, g_vn, g_sall,
                                                     do0, g_dvn, g_dkd, g_dgl, "gdn_local_bwd")
    da_g, db_g, d_galog, d_gdtb = _gdn_gate_bwd(
        araw.reshape(GDN_HV, T), braw.reshape(GDN_HV, T), dg_rows.reshape(GDN_HV, T),
        dbeta_rows.reshape(GDN_HV, T), col(g_alog), col(g_dtb), "gdn_gate_bwd")
    dq_pre, dcw_q, _ = _conv_bwd(q_pre, cw_q, zb(1024), dq, True, GDN_DK ** -0.5, "gdn_conv_q_bwd")
    dk_pre, dcw_k, _ = _conv_bwd(k_pre, cw_k, zb(1024), dk, True, 1.0, "gdn_conv_k_bwd")
    dv_pre, dcw_v, _ = _conv_bwd(v_pre, cw_v, zb(2048), dv, False, 1.0, "gdn_conv_v_bwd")
    dab = _pad_cols(jnp.concatenate([_from_rows(db_g), _from_rows(da_g)], axis=1), PAD_W)
    g_dparts = [dq_pre, dk_pre, dv_pre, dz0, dab]
    dx0, d_nw0 = _inproj_bwd(x, nw0, g_dparts, g_wparts, dx1, "gdn_inproj_bwd")
    g_dw = [_matmul_tn(h0, d, "gdn_win_grad_%d" % n) for n, d in enumerate(g_dparts)]
    d_gwin = jnp.concatenate(g_dw[:4] + [g_dw[4][:, 0:2 * GDN_HV]], axis=1)
    grads = {
        "norm_w": jnp.concatenate([d_nw0, d_nw1], axis=0),
        "gdn_w_in": d_gwin,
        "gdn_conv_w": jnp.concatenate([dcw_q, dcw_k, dcw_v], axis=1),
        "gdn_a_log": d_galog.reshape(1, -1),
        "gdn_dt_bias": d_gdtb.reshape(1, -1),
        "gdn_norm_w": d_gnw,
        "gdn_w_out": d_gwout,
        "ssd_w_in": d_swin,
        "ssd_conv_w": jnp.concatenate([dcw_x, dcw_b, dcw_c], axis=1),
        "ssd_conv_b": jnp.concatenate([dcb_x, dcb_b, dcb_c], axis=1),
        "ssd_dt_bias": d_sdtb.reshape(1, -1),
        "ssd_a_log": d_salog.reshape(1, -1),
        "ssd_d": d_sd.reshape(1, -1),
        "ssd_norm_w": d_snw,
        "ssd_w_out": d_swout,
        "final_norm_w": d_fw,
    }
    return loss, dx0, grads


WEIGHTS = ["norm_w", "gdn_w_in", "gdn_conv_w", "gdn_a_log", "gdn_dt_bias", "gdn_norm_w", "gdn_w_out", "ssd_w_in",
           "ssd_conv_w", "ssd_conv_b", "ssd_dt_bias", "ssd_a_log", "ssd_d", "ssd_norm_w", "ssd_w_out", "final_norm_w"]
COL_SHARDED = ["gdn_w_in", "ssd_w_in"]
ROW_SHARDED = ["gdn_w_out", "ssd_w_out"]
SMALL_SHARDED = ["gdn_conv_w", "ssd_conv_w", "ssd_conv_b", "ssd_norm_w"]
REPLICATED = ["norm_w", "gdn_a_log", "gdn_dt_bias", "gdn_norm_w", "ssd_dt_bias", "ssd_a_log", "ssd_d", "final_norm_w"]


def _pack(arrs):
    return jnp.concatenate([a.reshape(-1) for a in arrs]).reshape(1, -1)


def _unpack(flat, shapes):
    out, pos = [], 0
    for s in shapes:
        n = 1
        for dim in s:
            n *= dim
        out.append(flat[pos:pos + n].reshape(s))
        pos += n
    return out


def _cols_to_shards(full):
    R, C = full.shape
    return full.reshape(R, N_DEV, C // N_DEV).transpose(1, 0, 2)


def _shards_to_cols(shards):
    n, R, c = shards.shape
    return shards.transpose(1, 0, 2).reshape(R, n * c)


def kernel(x, norm_w, gdn_w_in, gdn_conv_w, gdn_a_log, gdn_dt_bias, gdn_norm_w, gdn_w_out, ssd_w_in, ssd_conv_w, ssd_conv_b, ssd_dt_bias, ssd_a_log, ssd_d, ssd_norm_w, ssd_w_out, final_norm_w, loss_target, m_norm_w, m_gdn_w_in, m_gdn_conv_w, m_gdn_a_log, m_gdn_dt_bias, m_gdn_norm_w, m_gdn_w_out, m_ssd_w_in, m_ssd_conv_w, m_ssd_conv_b, m_ssd_dt_bias, m_ssd_a_log, m_ssd_d, m_ssd_norm_w, m_ssd_w_out, m_final_norm_w, v_norm_w, v_gdn_w_in, v_gdn_conv_w, v_gdn_a_log, v_gdn_dt_bias, v_gdn_norm_w, v_gdn_w_out, v_ssd_w_in, v_ssd_conv_w, v_ssd_conv_b, v_ssd_dt_bias, v_ssd_a_log, v_ssd_d, v_ssd_norm_w, v_ssd_w_out, v_final_norm_w):
    w = dict(norm_w=norm_w, gdn_w_in=gdn_w_in[0], gdn_conv_w=gdn_conv_w[0], gdn_a_log=gdn_a_log,
             gdn_dt_bias=gdn_dt_bias, gdn_norm_w=gdn_norm_w, gdn_w_out=gdn_w_out[0], ssd_w_in=ssd_w_in[0],
             ssd_conv_w=ssd_conv_w[0], ssd_conv_b=ssd_conv_b, ssd_dt_bias=ssd_dt_bias, ssd_a_log=ssd_a_log,
             ssd_d=ssd_d, ssd_norm_w=ssd_norm_w, ssd_w_out=ssd_w_out[0], final_norm_w=final_norm_w.reshape(1, -1))
    m = dict(norm_w=m_norm_w, gdn_w_in=m_gdn_w_in[0], gdn_conv_w=m_gdn_conv_w[0], gdn_a_log=m_gdn_a_log,
             gdn_dt_bias=m_gdn_dt_bias, gdn_norm_w=m_gdn_norm_w, gdn_w_out=m_gdn_w_out[0], ssd_w_in=m_ssd_w_in[0],
             ssd_conv_w=m_ssd_conv_w[0], ssd_conv_b=m_ssd_conv_b, ssd_dt_bias=m_ssd_dt_bias, ssd_a_log=m_ssd_a_log,
             ssd_d=m_ssd_d, ssd_norm_w=m_ssd_norm_w, ssd_w_out=m_ssd_w_out[0], final_norm_w=m_final_norm_w.reshape(1, -1))
    v = dict(norm_w=v_norm_w, gdn_w_in=v_gdn_w_in[0], gdn_conv_w=v_gdn_conv_w[0], gdn_a_log=v_gdn_a_log,
             gdn_dt_bias=v_gdn_dt_bias, gdn_norm_w=v_gdn_norm_w, gdn_w_out=v_gdn_w_out[0], ssd_w_in=v_ssd_w_in[0],
             ssd_conv_w=v_ssd_conv_w[0], ssd_conv_b=v_ssd_conv_b, ssd_dt_bias=v_ssd_dt_bias, ssd_a_log=v_ssd_a_log,
             ssd_d=v_ssd_d, ssd_norm_w=v_ssd_norm_w, ssd_w_out=v_ssd_w_out[0], final_norm_w=v_final_norm_w.reshape(1, -1))
    out_shapes = {n: a.shape for n, a in zip(
        WEIGHTS, [norm_w, gdn_w_in, gdn_conv_w, gdn_a_log, gdn_dt_bias, gdn_norm_w, gdn_w_out, ssd_w_in, ssd_conv_w,
                  ssd_conv_b, ssd_dt_bias, ssd_a_log, ssd_d, ssd_norm_w, ssd_w_out, final_norm_w])}

    small_shapes = [w[n].shape for n in SMALL_SHARDED]
    gathered = _exchange([_mx(w[n]) for n in COL_SHARDED + ROW_SHARDED] + [_pack([w[n] for n in SMALL_SHARDED])],
                         [True] * 5, "gather_weights")
    full = dict(w)
    for n, gth in zip(COL_SHARDED, gathered[0:2]):
        full[n] = _shards_to_cols(gth)
    for n, gth in zip(ROW_SHARDED, gathered[2:4]):
        full[n] = gth.reshape(-1, gth.shape[-1])
    small_all = [_unpack(gathered[4][s, 0], small_shapes) for s in range(N_DEV)]
    for idx, n in enumerate(SMALL_SHARDED):
        full[n] = jnp.concatenate([small_all[s][idx] for s in range(N_DEV)], axis=-1)

    loss, dx, grads = _local_step(x[0], loss_target[0], full)

    send_small = jnp.concatenate(
        [_cols_to_shards(grads[n]).reshape(N_DEV, -1) for n in SMALL_SHARDED], axis=1)[:, None, :]
    rep_shapes = [w[n].shape for n in REPLICATED]
    recv = _exchange(
        [_cols_to_shards(grads[n]) for n in COL_SHARDED]
        + [grads[n].reshape(N_DEV, -1, D_MODEL) for n in ROW_SHARDED]
        + [send_small, _pack([grads[n] for n in REPLICATED])],
        [False] * 5 + [True], "exchange_grads")

    res = {}
    for n, parts in zip(COL_SHARDED + ROW_SHARDED, recv[0:4]):
        res[n] = _adamw(parts, w[n], m[n], v[n], "adamw_" + n)
    small_res = _adamw(recv[4], *[_pack([t[n] for n in SMALL_SHARDED]) for t in (w, m, v)], "adamw_small")
    rep_res = _adamw(recv[5], *[_pack([t[n] for n in REPLICATED]) for t in (w, m, v)], "adamw_replicated")
    for k4 in range(4):
        for n, a in zip(SMALL_SHARDED, _unpack(small_res[k4][0], small_shapes)):
            res.setdefault(n, [None] * 4)[k4] = a
        for n, a in zip(REPLICATED, _unpack(rep_res[k4][0], rep_shapes)):
            res.setdefault(n, [None] * 4)[k4] = a

    loss = lax.psum(loss[0, 0], ("x", "y", "c"))
    outs = [loss, dx[None]]
    for k4 in range(4):
        outs += [res[n][k4].reshape(out_shapes[n]) for n in WEIGHTS]
    return tuple(outs)
```

```python
import jax
import jax.numpy as jnp
from jax import lax
from jax.experimental import pallas as pl
from jax.experimental.pallas import tpu as pltpu

F32 = jnp.float32
MXU_DTYPE = jnp.bfloat16
GRAD_WIRE_DTYPE = jnp.bfloat16
HI = lax.Precision.HIGHEST
EPS = 1e-6
VMEM_LIMIT_BYTES = 56 * 1024 * 1024
N_DEV = 8
MESH = pl.DeviceIdType.MESH

D_MODEL = 1024
CONV_K = 4
GDN_HV = 16
GDN_DK = 128
GDN_CHUNK = 64
SSD_H = 32
SSD_P = 64
SSD_N = 128
SSD_G = 8
SSD_R = SSD_H // SSD_G
SSD_CHUNK = 128
D_INNER = 2048
PAD_W = 128

ADAM_LR = 0.001
ADAM_B1 = 0.9
ADAM_B2 = 0.999
ADAM_EPS = 1e-08
ADAM_WD = 0.01
ADAM_STEP = 10


def _params(*sem):
    return pltpu.CompilerParams(dimension_semantics=sem, vmem_limit_bytes=VMEM_LIMIT_BYTES)


def _mx(a):
    return a.astype(MXU_DTYPE)


def _dot(a, b):
    return jnp.dot(_mx(a), _mx(b), preferred_element_type=F32)


def _dot_nt(a, b):
    return lax.dot_general(_mx(a), _mx(b), (((1,), (1,)), ((), ())), preferred_element_type=F32)


def _dot_tn(a, b):
    return lax.dot_general(_mx(a), _mx(b), (((0,), (0,)), ((), ())), preferred_element_type=F32)


def _dot_hi(a, b):
    return jnp.dot(a, b, precision=HI, preferred_element_type=F32)


def _sigmoid(x):
    return 1.0 / (1.0 + jnp.exp(-x))


def _silu(x):
    return x * _sigmoid(x)


def _dsilu(x):
    s = _sigmoid(x)
    return s * (1.0 + x * (1.0 - s))


def _softplus(x):
    return jnp.maximum(x, 0.0) + jnp.log1p(jnp.exp(-jnp.abs(x)))


def _col(r, eye):
    return jnp.sum(jnp.where(eye, r, 0.0), axis=1, keepdims=True)


def _row(c, eye):
    return jnp.sum(jnp.where(eye, c, 0.0), axis=0, keepdims=True)


def _col_bcast(r, n):
    return jnp.broadcast_to(r, (n, n)).T


def _masks(n):
    r = lax.broadcasted_iota(jnp.int32, (n, n), 0)
    c = lax.broadcasted_iota(jnp.int32, (n, n), 1)
    return r >= c, r > c, r == c, r, c


def _norm_inproj(x, nw, wparts, name):
    T = x.shape[0]
    tt = min(T, 256)
    n = len(wparts)

    def body(x_ref, nw_ref, *refs):
        w_refs, h_ref, o_refs = refs[:n], refs[n], refs[n + 1:]
        xv = x_ref[...]
        r = lax.rsqrt(jnp.mean(xv * xv, axis=-1, keepdims=True) + EPS)
        h = _mx(xv * r * nw_ref[...])
        h_ref[...] = h
        for w_ref, o_ref in zip(w_refs, o_refs):
            o_ref[...] = jnp.dot(h, w_ref[...], preferred_element_type=F32)

    row = lambda width: pl.BlockSpec((tt, width), lambda i: (i, 0))
    full = lambda a: pl.BlockSpec(a.shape, lambda i: (0, 0))
    outs = pl.pallas_call(
        body, grid=(T // tt,),
        in_specs=[row(D_MODEL), full(nw)] + [full(w) for w in wparts],
        out_specs=[row(D_MODEL)] + [row(w.shape[1]) for w in wparts],
        out_shape=[jax.ShapeDtypeStruct((T, D_MODEL), MXU_DTYPE)]
        + [jax.ShapeDtypeStruct((T, w.shape[1]), F32) for w in wparts],
        compiler_params=_params("parallel"), name=name,
    )(x, nw, *wparts)
    return outs[0], outs[1:]


def _inproj_bwd(x, nw, dparts, wparts, dres, name):
    T = x.shape[0]
    tt = min(T, 256)
    n = len(wparts)

    def body(x_ref, nw_ref, dres_ref, *refs):
        d_refs, w_refs, dx_ref, dnw_ref = refs[:n], refs[n:2 * n], refs[2 * n], refs[2 * n + 1]

        @pl.when(pl.program_id(0) == 0)
        def _():
            dnw_ref[...] = jnp.zeros_like(dnw_ref)

        dh = _dot_nt(d_refs[0][...], w_refs[0][...])
        for d_ref, w_ref in zip(d_refs[1:], w_refs[1:]):
            dh = dh + _dot_nt(d_ref[...], w_ref[...])
        xv = x_ref[...]
        r = lax.rsqrt(jnp.mean(xv * xv, axis=-1, keepdims=True) + EPS)
        xh = xv * r
        dnw_ref[...] += jnp.sum(dh * xh, axis=0, keepdims=True)
        dxn = dh * nw_ref[...]
        dx_ref[...] = dres_ref[...] + r * (dxn - xh * jnp.mean(dxn * xh, axis=-1, keepdims=True))

    row = lambda width: pl.BlockSpec((tt, width), lambda i: (i, 0))
    full = lambda a: pl.BlockSpec(a.shape, lambda i: (0, 0))
    return pl.pallas_call(
        body, grid=(T // tt,),
        in_specs=[row(D_MODEL), full(nw), row(D_MODEL)] + [row(d.shape[1]) for d in dparts]
        + [full(w) for w in wparts],
        out_specs=[row(D_MODEL), pl.BlockSpec((1, D_MODEL), lambda i: (0, 0))],
        out_shape=[jax.ShapeDtypeStruct((T, D_MODEL), F32), jax.ShapeDtypeStruct((1, D_MODEL), F32)],
        compiler_params=_params("arbitrary"), name=name,
    )(x, nw, dres, *dparts, *wparts)


def _matmul_tn(a, b, name):
    T, K = a.shape
    N = b.shape[1]
    tt = min(T, 512)
    tn = min(N, 1024)

    def body(a_ref, b_ref, o_ref):
        @pl.when(pl.program_id(1) == 0)
        def _():
            o_ref[...] = jnp.zeros_like(o_ref)

        o_ref[...] += _dot_tn(a_ref[...], b_ref[...])

    return pl.pallas_call(
        body, grid=(N // tn, T // tt),
        in_specs=[pl.BlockSpec((tt, K), lambda n, t: (t, 0)), pl.BlockSpec((tt, tn), lambda n, t: (t, n))],
        out_specs=pl.BlockSpec((K, tn), lambda n, t: (0, n)),
        out_shape=jax.ShapeDtypeStruct((K, N), F32),
        compiler_params=_params("parallel", "arbitrary"), name=name,
    )(a, b)


def _out_fwd(o, z, w, wout, xres, gs, gate_first, name):
    T = o.shape[0]
    tt = min(T, 256)
    wide = w.shape[1] == D_INNER

    def body(o_ref, z_ref, w_ref, wout_ref, x_ref, out_ref, yn):
        for g0 in range(0, D_INNER, gs):
            sl = slice(g0, g0 + gs)
            og, zg = o_ref[:, sl], z_ref[:, sl]
            wg = w_ref[:, sl] if wide else w_ref[...]
            if gate_first:
                u = og * _silu(zg)
                r = lax.rsqrt(jnp.mean(u * u, axis=-1, keepdims=True) + EPS)
                yn[:, sl] = _mx(u * r * wg)
            else:
                r = lax.rsqrt(jnp.mean(og * og, axis=-1, keepdims=True) + EPS)
                yn[:, sl] = _mx(og * r * wg * _silu(zg))
        out_ref[...] = x_ref[...] + jnp.dot(yn[...], wout_ref[...], preferred_element_type=F32)

    row = lambda width: pl.BlockSpec((tt, width), lambda i: (i, 0))
    full = lambda a: pl.BlockSpec(a.shape, lambda i: (0, 0))
    return pl.pallas_call(
        body, grid=(T // tt,),
        in_specs=[row(D_INNER), row(D_INNER), full(w), full(wout), row(D_MODEL)],
        out_specs=row(D_MODEL),
        out_shape=jax.ShapeDtypeStruct((T, D_MODEL), F32),
        scratch_shapes=[pltpu.VMEM((tt, D_INNER), MXU_DTYPE)],
        compiler_params=_params("parallel"), name=name,
    )(o, z, w, wout, xres)


def _out_bwd(dx, o, z, w, wout, gs, gate_first, name):
    T = o.shape[0]
    tt = min(T, 256)
    wide = w.shape[1] == D_INNER

    def body(dx_ref, o_ref, z_ref, w_ref, wout_ref, do_ref, dz_ref, dw_ref, yn_ref):
        @pl.when(pl.program_id(0) == 0)
        def _():
            dw_ref[...] = jnp.zeros_like(dw_ref)

        dyn = _dot_nt(dx_ref[...], wout_ref[...])
        dw_acc = jnp.zeros((1, gs), F32)
        for g0 in range(0, D_INNER, gs):
            sl = slice(g0, g0 + gs)
            og, zg, dg = o_ref[:, sl], z_ref[:, sl], dyn[:, sl]
            wg = w_ref[:, sl] if wide else w_ref[...]
            sz = _silu(zg)
            if gate_first:
                u = og * sz
                r = lax.rsqrt(jnp.mean(u * u, axis=-1, keepdims=True) + EPS)
                uh = u * r
                yn_ref[:, sl] = _mx(uh * wg)
                dw_g = jnp.sum(dg * uh, axis=0, keepdims=True)
                duh = dg * wg
                du = r * (duh - uh * jnp.mean(duh * uh, axis=-1, keepdims=True))
                do_ref[:, sl] = du * sz
                dz_ref[:, sl] = du * og * _dsilu(zg)
            else:
                r = lax.rsqrt(jnp.mean(og * og, axis=-1, keepdims=True) + EPS)
                oh = og * r
                yn_ref[:, sl] = _mx(oh * wg * sz)
                dw_g = jnp.sum(dg * oh * sz, axis=0, keepdims=True)
                doh = dg * wg * sz
                dz_ref[:, sl] = dg * oh * wg * _dsilu(zg)
                do_ref[:, sl] = r * (doh - oh * jnp.mean(doh * oh, axis=-1, keepdims=True))
            if wide:
                dw_ref[:, sl] += dw_g
            else:
                dw_acc = dw_acc + dw_g
        if not wide:
            dw_ref[...] += dw_acc

    row = lambda width: pl.BlockSpec((tt, width), lambda i: (i, 0))
    full = lambda a: pl.BlockSpec(a.shape, lambda i: (0, 0))
    return pl.pallas_call(
        body, grid=(T // tt,),
        in_specs=[row(D_MODEL), row(D_INNER), row(D_INNER), full(w), full(wout)],
        out_specs=[row(D_INNER), row(D_INNER), full(w), row(D_INNER)],
        out_shape=[jax.ShapeDtypeStruct((T, D_INNER), F32), jax.ShapeDtypeStruct((T, D_INNER), F32),
                   jax.ShapeDtypeStruct(w.shape, F32), jax.ShapeDtypeStruct((T, D_INNER), MXU_DTYPE)],
        compiler_params=_params("arbitrary"), name=name,
    )(dx, o, z, w, wout)


HALO = 8
CONV_STRIP = 16


def _conv_fwd(pre, w, b, l2, scale, name):
    T, C = pre.shape
    tt = min(T, 512)
    tc = min(C, 1024 if l2 else 512)
    strip = tt if l2 else CONV_STRIP

    def body(pre_ref, halo_ref, w_ref, b_ref, out_ref, P):
        i = pl.program_id(0)
        P[0:HALO, :] = jnp.where(i > 0, halo_ref[...], 0.0)
        P[HALO:HALO + tt, :] = pre_ref[...]
        wj = [w_ref[j:j + 1, :] for j in range(CONV_K)]
        bias = b_ref[...]
        for r0 in range(0, tt, strip):
            acc = bias + wj[0] * P[pl.ds(HALO - 3 + r0, strip), :]
            for j in range(1, CONV_K):
                acc = acc + wj[j] * P[pl.ds(HALO - 3 + j + r0, strip), :]
            s = _silu(acc)
            if l2:
                sls = [slice(g0, g0 + GDN_DK) for g0 in range(0, tc, GDN_DK)]
                rr = [lax.rsqrt(jnp.sum(s[:, sl] * s[:, sl], axis=-1, keepdims=True) + EPS) for sl in sls]
                for sl, r in zip(sls, rr):
                    out_ref[r0:r0 + strip, sl] = s[:, sl] * r * scale
            else:
                out_ref[r0:r0 + strip, :] = s

    return pl.pallas_call(
        body, grid=(T // tt, C // tc),
        in_specs=[pl.BlockSpec((tt, tc), lambda i, j: (i, j)),
                  pl.BlockSpec((HALO, tc), lambda i, j: (jnp.maximum(i * (tt // HALO) - 1, 0), j)),
                  pl.BlockSpec((CONV_K, tc), lambda i, j: (0, j)),
                  pl.BlockSpec((1, tc), lambda i, j: (0, j))],
        out_specs=pl.BlockSpec((tt, tc), lambda i, j: (i, j)),
        out_shape=jax.ShapeDtypeStruct((T, C), F32),
        scratch_shapes=[pltpu.VMEM((HALO + tt, tc), F32)],
        compiler_params=_params("parallel", "parallel"), name=name,
    )(pre, pre, w, b)


def _conv_bwd(pre, w, b, dpost, l2, scale, name):
    T, C = pre.shape
    tt = min(T, 512)
    tc = min(C, 1024 if l2 else 512)
    strip = tt if l2 else CONV_STRIP
    nT = T // tt
    ext = tt + HALO

    def body(pre_ref, hp_ref, hn_ref, dpost_ref, dn_ref, w_ref, b_ref, dpre_ref, dw_ref, db_ref, P, Q):
        i = pl.program_id(1)

        @pl.when(i == 0)
        def _():
            dw_ref[...] = jnp.zeros_like(dw_ref)
            db_ref[...] = jnp.zeros_like(db_ref)

        P[0:HALO, :] = jnp.where(i > 0, hp_ref[...], 0.0)
        P[HALO:HALO + tt, :] = pre_ref[...]
        P[HALO + tt:HALO + ext, :] = hn_ref[...]
        wj = [w_ref[j:j + 1, :] for j in range(CONV_K)]
        bias = b_ref[...]
        keep_next = jnp.where(i < nT - 1, 1.0, 0.0)
        for r0 in list(range(0, tt, strip)) + [tt]:
            n = strip if r0 < tt else HALO
            cpre = bias + wj[0] * P[pl.ds(HALO - 3 + r0, n), :]
            for j in range(1, CONV_K):
                cpre = cpre + wj[j] * P[pl.ds(HALO - 3 + j + r0, n), :]
            dy = dpost_ref[r0:r0 + n, :] if r0 < tt else dn_ref[...] * keep_next
            sg = _sigmoid(cpre)
            ds_c = sg * (1.0 + cpre * (1.0 - sg))
            if l2:
                s = cpre * sg
                sls = [slice(g0, g0 + GDN_DK) for g0 in range(0, tc, GDN_DK)]
                rr = [lax.rsqrt(jnp.sum(s[:, sl] * s[:, sl], axis=-1, keepdims=True) + EPS) for sl in sls]
                yh = [s[:, sl] * r for sl, r in zip(sls, rr)]
                pr = [jnp.sum(dy[:, sl] * y, axis=-1, keepdims=True) for sl, y in zip(sls, yh)]
                for sl, r, y, p in zip(sls, rr, yh, pr):
                    Q[r0:r0 + n, sl] = (scale * r) * (dy[:, sl] - y * p) * ds_c[:, sl]
            else:
                Q[r0:r0 + n, :] = dy * ds_c
        fold = lambda a: jnp.sum(a.reshape(strip // 8, 8, tc), axis=0)
        dw_acc = [jnp.zeros((8, tc), F32) for _ in range(CONV_K)]
        db_acc = jnp.zeros((8, tc), F32)
        for r0 in range(0, tt, strip):
            dpre = wj[0] * Q[pl.ds(3 + r0, strip), :]
            for j in range(1, CONV_K):
                dpre = dpre + wj[j] * Q[pl.ds(3 - j + r0, strip), :]
            dpre_ref[r0:r0 + strip, :] = dpre
            dyc = Q[r0:r0 + strip, :]
            for j in range(CONV_K):
                dw_acc[j] = dw_acc[j] + fold(dyc * P[pl.ds(HALO - 3 + j + r0, strip), :])
            db_acc = db_acc + fold(dyc)
        for j in range(CONV_K):
            dw_ref[j:j + 1, :] += jnp.sum(dw_acc[j], axis=0, keepdims=True)
        db_ref[...] += jnp.sum(db_acc, axis=0, keepdims=True)

    tile = pl.BlockSpec((tt, tc), lambda j, i: (i, j))
    prev = pl.BlockSpec((HALO, tc), lambda j, i: (jnp.maximum(i * (tt // HALO) - 1, 0), j))
    nxt = pl.BlockSpec((HALO, tc), lambda j, i: (jnp.minimum((i + 1) * (tt // HALO), T // HALO - 1), j))
    return pl.pallas_call(
        body, grid=(C // tc, nT),
        in_specs=[tile, prev, nxt, tile, nxt,
                  pl.BlockSpec((CONV_K, tc), lambda j, i: (0, j)), pl.BlockSpec((1, tc), lambda j, i: (0, j))],
        out_specs=[tile, pl.BlockSpec((CONV_K, tc), lambda j, i: (0, j)), pl.BlockSpec((1, tc), lambda j, i: (0, j))],
        out_shape=[jax.ShapeDtypeStruct((T, C), F32), jax.ShapeDtypeStruct((CONV_K, C), F32),
                   jax.ShapeDtypeStruct((1, C), F32)],
        scratch_shapes=[pltpu.VMEM((HALO + ext, tc), F32), pltpu.VMEM((ext, tc), F32)],
        compiler_params=_params("parallel", "arbitrary"), name=name,
    )(pre, pre, pre, dpost, dpost, w, b)


GDN_LOCKSTEP_CHUNKS = 16
GDN_SCAN_HEADS = 8


def _inv_unit_lower_many(nms, eye, n):
    xs = [jnp.where(eye, 1.0, 0.0) - nm for nm in nms]
    ps = list(nms)
    k = 2
    while k < n:
        ps = [_dot(p, p) for p in ps]
        xs = [x + _dot(x, p) for x, p in zip(xs, ps)]
        k *= 2
    return xs


def _gdn_prep(q, k, v, araw, braw, alog, dtb, name):
    T = q.shape[0]
    C = GDN_CHUNK
    tt = min(T, 1024)
    cpt, nC = tt // C, T // C
    grp = min(cpt, GDN_LOCKSTEP_CHUNKS)

    def body(alog_ref, dtb_ref, q_ref, k_ref, v_ref, a_ref, b_ref,
             u_ref, w_ref, pm_ref, ti_ref, g_ref, beta_ref, gc_ref):
        j = pl.program_id(0)
        tri, strict, eye, r_i, c_i = _masks(C)
        upper = jnp.where(r_i <= c_i, 1.0, 0.0)
        gcs, bts = [], []
        for hh in range(2):
            h = 2 * j + hh
            g = -jnp.exp(alog_ref[h]) * _softplus(a_ref[hh] + dtb_ref[h])
            bt = _sigmoid(b_ref[hh])
            gc = _dot_hi(g, upper)
            g_ref[hh], beta_ref[hh], gc_ref[hh] = g, bt, gc
            gcs.append(gc)
            bts.append(bt)
        for c0 in range(0, cpt, grp):
            cs = list(range(c0, c0 + grp))
            inst = [(c, hh) for c in cs for hh in range(2)]
            rows = {c: slice(c * C, (c + 1) * C) for c in cs}
            qc = {c: q_ref[rows[c], :] for c in cs}
            kc = {c: k_ref[rows[c], :] for c in cs}
            kk = {c: _dot_nt(kc[c], kc[c]) for c in cs}
            qk = {c: _dot_nt(qc[c], kc[c]) for c in cs}
            gcr = [gcs[hh][c:c + 1, :] for c, hh in inst]
            gcc = [_col(r, eye) for r in gcr]
            bc = [_col(bts[hh][c:c + 1, :], eye) for c, hh in inst]
            lm = [jnp.exp(jnp.where(tri, cc - r, -1e30)) for cc, r in zip(gcc, gcr)]
            nm = [jnp.where(strict, kk[c] * b * l, 0.0) for (c, hh), b, l in zip(inst, bc, lm)]
            tinv = _inv_unit_lower_many(nm, eye, C)
            rhs = [jnp.concatenate([v_ref[rows[c], hh * GDN_DK:(hh + 1) * GDN_DK] * b, kc[c] * (b * jnp.exp(cc))], axis=1)
                   for (c, hh), b, cc in zip(inst, bc, gcc)]
            sol = [_dot(t, r) for t, r in zip(tinv, rhs)]
            for (c, hh), s, t, l in zip(inst, sol, tinv, lm):
                hs = slice(hh * GDN_DK, (hh + 1) * GDN_DK)
                u_ref[rows[c], hs] = s[:, :GDN_DK]
                w_ref[rows[c], hs] = _mx(s[:, GDN_DK:])
                pm_ref[hh, c] = _mx(jnp.where(tri, qk[c] * l, 0.0))
                ti_ref[hh, c] = _mx(t)

    smem = pl.BlockSpec(memory_space=pltpu.SMEM)
    rows_spec = pl.BlockSpec((2, cpt, C), lambda j, i: (j, i, 0))
    qk_spec = pl.BlockSpec((tt, GDN_DK), lambda j, i: (i, j))
    v_spec = pl.BlockSpec((tt, 2 * GDN_DK), lambda j, i: (i, j))
    cc_spec = pl.BlockSpec((2, cpt, C, C), lambda j, i: (j, i, 0, 0))
    rows_shape = jax.ShapeDtypeStruct((GDN_HV, nC, C), F32)
    cc_shape = jax.ShapeDtypeStruct((GDN_HV, nC, C, C), MXU_DTYPE)
    return pl.pallas_call(
        body, grid=(GDN_HV // 2, T // tt),
        in_specs=[smem, smem, qk_spec, qk_spec, v_spec, rows_spec, rows_spec],
        out_specs=[v_spec, v_spec, cc_spec, cc_spec, rows_spec, rows_spec, rows_spec],
        out_shape=[jax.ShapeDtypeStruct((T, D_INNER), F32), jax.ShapeDtypeStruct((T, D_INNER), MXU_DTYPE),
                   cc_shape, cc_shape, rows_shape, rows_shape, rows_shape],
        compiler_params=_params("parallel", "parallel"), name=name,
    )(alog, dtb, q, k, v, araw, braw)


def _gdn_decays(gc_ref, h, c, eye, C):
    gcr = gc_ref[h, pl.ds(c, 1), :]
    gcc = _col(gcr, eye)
    glast = gcr[:, C - 1:C]
    return jnp.exp(gcc), jnp.exp(glast - gcc), jnp.exp(glast)


def _gdn_state_fwd(q, k, u, w, pm, gc, name):
    T = q.shape[0]
    C = GDN_CHUNK
    HG = GDN_SCAN_HEADS
    tt = min(T, 512)
    cpt, nC = tt // C, T // C

    def body(q_ref, k_ref, u_ref, w_ref, pm_ref, gc_ref, o_ref, vn_ref, sall_ref, S):
        @pl.when(pl.program_id(1) == 0)
        def _():
            S[...] = jnp.zeros_like(S)

        eye = _masks(C)[2]
        heads = list(range(HG))

        def chunk(c, carry):
            rows = pl.ds(pl.multiple_of(c * C, C), C)
            hs = [slice(h * GDN_DK, (h + 1) * GDN_DK) for h in heads]
            qs = [slice((h // 2) * GDN_DK, (h // 2 + 1) * GDN_DK) for h in heads]
            dec = [_gdn_decays(gc_ref, h, c, eye, C) for h in heads]
            sv = [S[h] for h in heads]
            for h in heads:
                sall_ref[h, c] = sv[h]
            ws = [_dot(w_ref[rows, hs[h]], sv[h]) for h in heads]
            qsv = [_dot(q_ref[rows, qs[h]], sv[h]) for h in heads]
            vn = [u_ref[rows, hs[h]] - ws[h] for h in heads]
            pv = [_dot(pm_ref[h, c], vn[h]) for h in heads]
            kv = [_dot_tn(k_ref[rows, qs[h]], vn[h] * dec[h][1]) for h in heads]
            for h in heads:
                vn_ref[rows, hs[h]] = _mx(vn[h])
                o_ref[rows, hs[h]] = qsv[h] * dec[h][0] + pv[h]
                S[h] = sv[h] * dec[h][2] + kv[h]
            return carry

        lax.fori_loop(0, cpt, chunk, 0)

    qk_spec = pl.BlockSpec((tt, HG // 2 * GDN_DK), lambda g, i: (i, g))
    v_spec = pl.BlockSpec((tt, HG * GDN_DK), lambda g, i: (i, g))
    return pl.pallas_call(
        body, grid=(GDN_HV // HG, T // tt),
        in_specs=[qk_spec, qk_spec, v_spec, v_spec,
                  pl.BlockSpec((HG, cpt, C, C), lambda g, i: (g, i, 0, 0)),
                  pl.BlockSpec((HG, cpt, C), lambda g, i: (g, i, 0))],
        out_specs=[v_spec, v_spec, pl.BlockSpec((HG, cpt, GDN_DK, GDN_DK), lambda g, i: (g, i, 0, 0))],
        out_shape=[jax.ShapeDtypeStruct((T, D_INNER), F32), jax.ShapeDtypeStruct((T, D_INNER), MXU_DTYPE),
                   jax.ShapeDtypeStruct((GDN_HV, nC, GDN_DK, GDN_DK), F32)],
        scratch_shapes=[pltpu.VMEM((HG, GDN_DK, GDN_DK), F32)],
        compiler_params=_params("parallel", "arbitrary"), name=name,
    )(q, k, u, w, pm, gc)


def _gdn_state_bwd(q, k, w, pm, vn, sall, gc, do, name):
    T = q.shape[0]
    C = GDN_CHUNK
    HG = GDN_SCAN_HEADS
    tt = min(T, 512)
    cpt, nC, nT = tt // C, T // C, T // tt

    def body(q_ref, k_ref, w_ref, pm_ref, vn_ref, sall_ref, gc_ref, do_ref, dvn_ref, dkd_ref, dgl_ref, dS):
        @pl.when(pl.program_id(1) == 0)
        def _():
            dS[...] = jnp.zeros_like(dS)

        eye = _masks(C)[2]
        heads = list(range(HG))

        def chunk(ci, carry):
            c = cpt - 1 - ci
            rows = pl.ds(pl.multiple_of(c * C, C), C)
            hs = [slice(h * GDN_DK, (h + 1) * GDN_DK) for h in heads]
            qs = [slice((h // 2) * GDN_DK, (h // 2 + 1) * GDN_DK) for h in heads]
            dec = [_gdn_decays(gc_ref, h, c, eye, C) for h in heads]
            dsn = [dS[h] for h in heads]
            doc = [do_ref[rows, hs[h]] for h in heads]
            kds = [_dot(k_ref[rows, qs[h]], dsn[h]) for h in heads]
            pdo = [_dot_tn(pm_ref[h, c], doc[h]) for h in heads]
            dkd = [_dot_nt(vn_ref[rows, hs[h]], dsn[h]) for h in heads]
            qdo = [_dot_tn(q_ref[rows, qs[h]], doc[h] * dec[h][0]) for h in heads]
            dvn = [pdo[h] + kds[h] * dec[h][1] for h in heads]
            wdv = [_dot_tn(w_ref[rows, hs[h]], dvn[h]) for h in heads]
            for h in heads:
                dgl = jnp.sum(jnp.sum(dsn[h] * sall_ref[h, c], axis=1, keepdims=True), axis=0, keepdims=True)
                dgl_ref[h, pl.ds(c, 1), :] = jnp.broadcast_to(dgl, (1, C))
                dvn_ref[rows, hs[h]] = dvn[h]
                dkd_ref[rows, hs[h]] = dkd[h]
                dS[h] = dsn[h] * dec[h][2] + qdo[h] - wdv[h]
            return carry

        lax.fori_loop(0, cpt, chunk, 0)

    rev = lambda i: nT - 1 - i
    qk_spec = pl.BlockSpec((tt, HG // 2 * GDN_DK), lambda g, i: (rev(i), g))
    v_spec = pl.BlockSpec((tt, HG * GDN_DK), lambda g, i: (rev(i), g))
    rows_spec = pl.BlockSpec((HG, cpt, C), lambda g, i: (g, rev(i), 0))
    return pl.pallas_call(
        body, grid=(GDN_HV // HG, nT),
        in_specs=[qk_spec, qk_spec, v_spec, pl.BlockSpec((HG, cpt, C, C), lambda g, i: (g, rev(i), 0, 0)), v_spec,
                  pl.BlockSpec((HG, cpt, GDN_DK, GDN_DK), lambda g, i: (g, rev(i), 0, 0)), rows_spec, v_spec],
        out_specs=[v_spec, v_spec, rows_spec],
        out_shape=[jax.ShapeDtypeStruct((T, D_INNER), F32), jax.ShapeDtypeStruct((T, D_INNER), F32),
                   jax.ShapeDtypeStruct((GDN_HV, nC, C), F32)],
        scratch_shapes=[pltpu.VMEM((HG, GDN_DK, GDN_DK), F32)],
        compiler_params=_params("parallel", "arbitrary"), name=name,
    )(q, k, w, pm, vn, sall, gc, do)


def _gdn_local_bwd(q, k, v, gc, beta, tinv, u, w, pm, vn, sall, do, dvn, dkd, dgl, name):
    T = q.shape[0]
    C = GDN_CHUNK
    tt = min(T, 1024)
    cpt, nC = tt // C, T // C
    grp = min(cpt, GDN_LOCKSTEP_CHUNKS)

    def body(q_ref, k_ref, v_ref, gc_ref, b_ref, ti_ref, u_ref, w_ref, pm_ref, vn_ref, sall_ref, do_ref,
             dvn_ref, dkd_ref, dgl_ref, dq_ref, dk_ref, dv_ref, dg_ref, dbeta_ref, dgc_s):
        tri, strict, eye, r_i, c_i = _masks(C)
        lower = jnp.where(r_i >= c_i, 1.0, 0.0)
        lane = lax.broadcasted_iota(jnp.int32, (1, C), 1)
        rsum = lambda a: jnp.sum(a, axis=1, keepdims=True)
        for c0 in range(0, cpt, grp):
            cs = list(range(c0, c0 + grp))
            inst = [(c, hh) for c in cs for hh in range(2)]
            n = len(inst)
            rows = {c: slice(c * C, (c + 1) * C) for c in cs}
            hsl = [slice(hh * GDN_DK, (hh + 1) * GDN_DK) for c, hh in inst]
            qc = {c: q_ref[rows[c], :] for c in cs}
            kc = {c: k_ref[rows[c], :] for c in cs}
            kk = {c: _dot_nt(kc[c], kc[c]) for c in cs}
            gcr = [gc_ref[hh, c:c + 1, :] for c, hh in inst]
            gcc = [_col(r, eye) for r in gcr]
            bc = [_col(b_ref[hh, c:c + 1, :], eye) for c, hh in inst]
            lm = [jnp.exp(jnp.where(tri, cc - r, -1e30)) for cc, r in zip(gcc, gcr)]
            e_c = [jnp.exp(cc) for cc in gcc]
            el_c = [jnp.exp(r[:, C - 1:C] - cc) for cc, r in zip(gcc, gcr)]
            gl = [jnp.exp(r[:, C - 1:C]) for r in gcr]
            doc = [do_ref[rows[c], hsl[i]] for i, (c, hh) in enumerate(inst)]
            dvn = [dvn_ref[rows[c], hsl[i]] for i, (c, hh) in enumerate(inst)]
            sv = [sall_ref[hh, c] for c, hh in inst]
            aa = [_dot_nt(jnp.concatenate([_mx(doc[i]), _mx(dvn[i])], axis=0), sv[i]) for i in range(n)]
            dpm = [jnp.where(tri, _dot_nt(doc[i], vn_ref[rows[c], hsl[i]]), 0.0) for i, (c, hh) in enumerate(inst)]
            dqd = [a[:C] for a in aa]
            drhs = [_dot_tn(ti_ref[hh, c], jnp.concatenate([dvn[i], -aa[i][C:]], axis=1))
                    for i, (c, hh) in enumerate(inst)]
            sol = [jnp.concatenate([_mx(u_ref[rows[c], hsl[i]]), w_ref[rows[c], hsl[i]]], axis=1)
                   for i, (c, hh) in enumerate(inst)]
            dnm = [-jnp.where(strict, _dot_nt(drhs[i], sol[i]), 0.0) for i in range(n)]
            dkk = [dnm[i] * bc[i] * lm[i] for i in range(n)]
            dqk = [dpm[i] * lm[i] for i in range(n)]
            dq1 = [_dot(dqk[i], kc[c]) for i, (c, hh) in enumerate(inst)]
            dk1 = [_dot(dkk[i], kc[c]) for i, (c, hh) in enumerate(inst)]
            dk2 = [_dot_tn(dkk[i], kc[c]) for i, (c, hh) in enumerate(inst)]
            dk3 = [_dot_tn(dqk[i], qc[c]) for i, (c, hh) in enumerate(inst)]
            dq_acc = {c: jnp.zeros((C, GDN_DK), F32) for c in cs}
            dk_acc = {c: jnp.zeros((C, GDN_DK), F32) for c in cs}
            for i, (c, hh) in enumerate(inst):
                k_, q_, v_ = kc[c], qc[c], v_ref[rows[c], hsl[i]]
                dvb, dkbe = drhs[i][:, :GDN_DK], drhs[i][:, GDN_DK:]
                dkd = dkd_ref[rows[c], hsl[i]]
                kb = k_ * bc[i]
                dkb = dkbe * e_c[i]
                de_c = rsum(dkbe * kb) + rsum(dqd[i] * q_)
                del_c = rsum(dkd * k_)
                dbc = rsum(dnm[i] * kk[c] * lm[i]) + rsum(dkb * k_) + rsum(dvb * v_)
                dq_acc[c] = dq_acc[c] + dq1[i] + dqd[i] * e_c[i]
                dk_acc[c] = dk_acc[c] + dk1[i] + dk2[i] + dk3[i] + dkd * el_c[i] + dkb * bc[i]
                dv_ref[rows[c], hsl[i]] = dvb * bc[i]
                nm = jnp.where(strict, kk[c] * bc[i] * lm[i], 0.0)
                gm = dnm[i] * nm + dpm[i] * pm_ref[hh, c].astype(F32)
                dgc_col = rsum(gm) + de_c * e_c[i] - del_c * el_c[i]
                dglast = jnp.sum(del_c * el_c[i], axis=0, keepdims=True) + dgl_ref[hh, c:c + 1, 0:1] * gl[i]
                dgc_s[hh, c:c + 1, :] = (_row(dgc_col, eye) - jnp.sum(gm, axis=0, keepdims=True)
                                         + jnp.where(lane == C - 1, dglast, 0.0))
                dbeta_ref[hh, c:c + 1, :] = _row(dbc, eye)
            for c in cs:
                dq_ref[rows[c], :] = dq_acc[c]
                dk_ref[rows[c], :] = dk_acc[c]
        for hh in range(2):
            dg_ref[hh] = _dot_hi(dgc_s[hh], lower)

    rows_spec = pl.BlockSpec((2, cpt, C), lambda j, i: (j, i, 0))
    qk_spec = pl.BlockSpec((tt, GDN_DK), lambda j, i: (i, j))
    v_spec = pl.BlockSpec((tt, 2 * GDN_DK), lambda j, i: (i, j))
    cc_spec = pl.BlockSpec((2, cpt, C, C), lambda j, i: (j, i, 0, 0))
    rows_shape = jax.ShapeDtypeStruct((GDN_HV, nC, C), F32)
    return pl.pallas_call(
        body, grid=(GDN_HV // 2, T // tt),
        in_specs=[qk_spec, qk_spec, v_spec, rows_spec, rows_spec, cc_spec, v_spec, v_spec, cc_spec, v_spec,
                  pl.BlockSpec((2, cpt, GDN_DK, GDN_DK), lambda j, i: (j, i, 0, 0)), v_spec, v_spec, v_spec, rows_spec],
        out_specs=[qk_spec, qk_spec, v_spec, rows_spec, rows_spec],
        out_shape=[jax.ShapeDtypeStruct((T, GDN_HV // 2 * GDN_DK), F32),
                   jax.ShapeDtypeStruct((T, GDN_HV // 2 * GDN_DK), F32),
                   jax.ShapeDtypeStruct((T, D_INNER), F32), rows_shape, rows_shape],
        scratch_shapes=[pltpu.VMEM((2, cpt, C), F32)],
        compiler_params=_params("parallel", "parallel"), name=name,
    )(q, k, v, gc, beta, tinv, u, w, pm, vn, sall, do, dvn, dkd, dgl)


def _gdn_gate_bwd(araw, braw, dg, dbeta, alog, dtb, name):
    H, T = araw.shape

    def body(a_ref, b_ref, dg_ref, dbt_ref, alog_ref, dtb_ref, da_ref, db_ref, dalog_ref, ddtb_ref):
        xa = a_ref[...] + dtb_ref[...]
        ea = jnp.exp(alog_ref[...])
        dgv = dg_ref[...]
        da = -dgv * ea * _sigmoid(xa)
        da_ref[...] = da
        dalog_ref[...] = jnp.sum(-dgv * ea * _softplus(xa), axis=1, keepdims=True)
        ddtb_ref[...] = jnp.sum(da, axis=1, keepdims=True)
        bt = _sigmoid(b_ref[...])
        db_ref[...] = dbt_ref[...] * bt * (1.0 - bt)

    return pl.pallas_call(
        body,
        out_shape=[jax.ShapeDtypeStruct((H, T), F32), jax.ShapeDtypeStruct((H, T), F32),
                   jax.ShapeDtypeStruct((H, 1), F32), jax.ShapeDtypeStruct((H, 1), F32)],
        compiler_params=pltpu.CompilerParams(vmem_limit_bytes=VMEM_LIMIT_BYTES), name=name,
    )(araw, braw, dg, dbeta, alog, dtb)


SSD_LOCKSTEP_CHUNKS = 2
SSD_LOCKSTEP_CHUNKS_BWD = 1


def _ssd_scan_fwd(xs, bm, cm, dtraw, alog, dtb, dskip, name):
    T = xs.shape[0]
    Q = SSD_CHUNK
    tt = min(T, 1024)
    cpt, nC = tt // Q, T // Q
    GW = SSD_R * SSD_P

    def body(alog_ref, dtb_ref, dsk_ref, xs_ref, b_ref, c_ref, dt_ref, y_ref, sall_ref, dto_ref, S, dt_s, acs_s):
        gi, i = pl.program_id(0), pl.program_id(1)

        @pl.when(i == 0)
        def _():
            S[...] = jnp.zeros_like(S)

        tri, _, eye, r_i, c_i = _masks(Q)
        upper = jnp.where(r_i <= c_i, 1.0, 0.0)
        for r in range(SSD_R):
            h = SSD_R * gi + r
            dt = _softplus(dt_ref[r] + dtb_ref[h])
            dto_ref[r] = dt
            dt_s[r] = dt
            acs_s[r] = _dot_hi(-jnp.exp(alog_ref[h]) * dt, upper)

        ps = [slice(r * SSD_P, (r + 1) * SSD_P) for r in range(SSD_R)]
        s_cur = [S[:, ps[r]] for r in range(SSD_R)]
        grp = min(cpt, SSD_LOCKSTEP_CHUNKS)
        for c0 in range(0, cpt, grp):
            cs = list(range(c0, c0 + grp))
            inst = [(c, r) for c in cs for r in range(SSD_R)]
            rows = {c: slice(c * Q, (c + 1) * Q) for c in cs}
            bc_ = {c: b_ref[rows[c], :] for c in cs}
            cc_ = {c: c_ref[rows[c], :] for c in cs}
            cb = {c: _dot_nt(cc_[c], bc_[c]) for c in cs}
            xr = [xs_ref[rows[c], ps[r]] for c, r in inst]
            acr = [acs_s[r, c:c + 1, :] for c, r in inst]
            acc = [_col_bcast(a, Q) for a in acr]
            dtc = [_col_bcast(dt_s[r, c:c + 1, :], Q)[:, :SSD_P] for c, r in inst]
            xd = [x * d for x, d in zip(xr, dtc)]
            mm = [cb[c] * jnp.exp(jnp.where(tri, acc[i] - acr[i], -1e30)) for i, (c, r) in enumerate(inst)]
            bct = {c: bc_[c].T for c in cs}
            st = [_dot(bct[c] * jnp.exp(acr[i][:, Q - 1:Q] - acr[i]), xd[i]) for i, (c, r) in enumerate(inst)]
            yd = [_dot(mm[i], xd[i]) for i in range(len(inst))]
            s_prev = []
            for i, (c, r) in enumerate(inst):
                s_prev.append(s_cur[r])
                s_cur[r] = s_cur[r] * jnp.exp(acr[i][:, Q - 1:Q]) + st[i]
            yo = [_dot(cc_[c] * jnp.exp(acc[i]), s_prev[i]) for i, (c, r) in enumerate(inst)]
            for i, (c, r) in enumerate(inst):
                sall_ref[0, c, :, ps[r]] = s_prev[i]
                y_ref[rows[c], ps[r]] = yd[i] + yo[i] + dsk_ref[SSD_R * gi + r] * xr[i]
        for r in range(SSD_R):
            S[:, ps[r]] = s_cur[r]

    smem = pl.BlockSpec(memory_space=pltpu.SMEM)
    rows_spec = pl.BlockSpec((SSD_R, cpt, Q), lambda g, i: (g, i, 0))
    return pl.pallas_call(
        body, grid=(SSD_G, T // tt),
        in_specs=[smem, smem, smem,
                  pl.BlockSpec((tt, GW), lambda g, i: (i, g)), pl.BlockSpec((tt, SSD_N), lambda g, i: (i, g)),
                  pl.BlockSpec((tt, SSD_N), lambda g, i: (i, g)), rows_spec],
        out_specs=[pl.BlockSpec((tt, GW), lambda g, i: (i, g)),
                   pl.BlockSpec((1, cpt, SSD_N, GW), lambda g, i: (g, i, 0, 0)), rows_spec],
        out_shape=[jax.ShapeDtypeStruct((T, D_INNER), F32), jax.ShapeDtypeStruct((SSD_G, nC, SSD_N, GW), F32),
                   jax.ShapeDtypeStruct((SSD_H, nC, Q), F32)],
        scratch_shapes=[pltpu.VMEM((SSD_N, GW), F32), pltpu.VMEM((SSD_R, cpt, Q), F32),
                        pltpu.VMEM((SSD_R, cpt, Q), F32)],
        compiler_params=_params("parallel", "arbitrary"), name=name,
    )(alog, dtb, dskip, xs, bm, cm, dtraw)


def _ssd_scan_bwd(xs, bm, cm, dt, sall, dy, alog, dskip, name):
    T = xs.shape[0]
    Q = SSD_CHUNK
    tt = min(T, 1024)
    cpt, nC, nT = tt // Q, T // Q, T // tt
    GW = SSD_R * SSD_P

    def body(alog_ref, dsk_ref, xs_ref, b_ref, c_ref, dt_ref, sall_ref, dy_ref,
             dxs_ref, db_ref, dc_ref, da_ref, ddt_ref, dd_ref, dS, acs_s, dacs_s, ddt_s, dd_s):
        gi, i = pl.program_id(0), pl.program_id(1)

        @pl.when(i == 0)
        def _():
            dS[...] = jnp.zeros_like(dS)

        tri, _, eye, r_i, c_i = _masks(Q)
        upper = jnp.where(r_i <= c_i, 1.0, 0.0)
        lower = jnp.where(r_i >= c_i, 1.0, 0.0)
        lane = lax.broadcasted_iota(jnp.int32, (1, Q), 1)
        for r in range(SSD_R):
            acs_s[r] = _dot_hi(-jnp.exp(alog_ref[SSD_R * gi + r]) * dt_ref[r], upper)

        ps = [slice(r * SSD_P, (r + 1) * SSD_P) for r in range(SSD_R)]
        ds_cur = [dS[:, ps[r]] for r in range(SSD_R)]
        grp = min(cpt, SSD_LOCKSTEP_CHUNKS_BWD)
        csum = lambda a: jnp.sum(a, axis=0, keepdims=True)
        tsum = lambda a: jnp.sum(csum(a), axis=1, keepdims=True)
        ones8 = jnp.ones((8, SSD_P), F32)
        for c0 in range(cpt - grp, -1, -grp):
            cs = list(range(c0 + grp - 1, c0 - 1, -1))
            inst = [(c, r) for c in cs for r in range(SSD_R)]
            n = len(inst)
            rows = {c: slice(c * Q, (c + 1) * Q) for c in cs}
            bc_ = {c: b_ref[rows[c], :] for c in cs}
            cc_ = {c: c_ref[rows[c], :] for c in cs}
            cb = {c: _dot_nt(cc_[c], bc_[c]) for c in cs}
            xr = [xs_ref[rows[c], ps[r]] for c, r in inst]
            dyr = [dy_ref[rows[c], ps[r]] for c, r in inst]
            acr = [acs_s[r, c:c + 1, :] for c, r in inst]
            dtr = [dt_ref[r, c:c + 1, :] for c, r in inst]
            acc = [_col_bcast(a, Q) for a in acr]
            dtb = [_col_bcast(d, Q) for d in dtr]
            al = [a[:, Q - 1:Q] for a in acr]
            e_c = [jnp.exp(a) for a in acc]
            dl_c = [jnp.exp(al[i] - acc[i]) for i in range(n)]
            e_r = [jnp.exp(a) for a in acr]
            dl_r = [jnp.exp(al[i] - acr[i]) for i in range(n)]
            gl = [jnp.exp(a) for a in al]
            bct = {c: bc_[c].T for c in cs}
            cct = {c: cc_[c].T for c in cs}
            cbt = {c: _dot_nt(bc_[c], cc_[c]) for c in cs}
            lm = [jnp.exp(jnp.where(tri, acc[i] - acr[i], -1e30)) for i in range(n)]
            lmt = [jnp.exp(jnp.where(r_i <= c_i, acr[i] - acc[i], -1e30)) for i in range(n)]
            mm = [cb[c] * lm[i] for i, (c, r) in enumerate(inst)]
            mmt = [cbt[c] * lmt[i] for i, (c, r) in enumerate(inst)]
            sr = [sall_ref[0, c, :, ps[r]] for c, r in inst]
            dmm0 = [jnp.where(tri, _dot_nt(dyr[i], xr[i]), 0.0) for i in range(n)]
            dmm0t = [jnp.where(r_i <= c_i, _dot_nt(xr[i], dyr[i]), 0.0) for i in range(n)]
            dxd1 = [_dot(mmt[i], dyr[i]) for i in range(n)]
            dce = [_dot_nt(dyr[i], sr[i]) for i in range(n)]
            dcet = [_dot_nt(sr[i], dyr[i]) for i in range(n)]
            cdy = [_dot(cct[c] * e_r[i], dyr[i]) for i, (c, r) in enumerate(inst)]
            dsn = []
            for i, (c, r) in enumerate(inst):
                dsn.append(ds_cur[r])
                ds_cur[r] = gl[i] * ds_cur[r] + cdy[i]
            dxd = [dxd1[i] + _dot(bc_[c] * dl_c[i], dsn[i]) for i, (c, r) in enumerate(inst)]
            dbd0 = [_dot_nt(xr[i], dsn[i]) for i in range(n)]
            dbd0t = [_dot_nt(dsn[i], xr[i]) for i in range(n)]
            dcb = {c: jnp.zeros((Q, Q), F32) for c in cs}
            dcbt = {c: jnp.zeros((Q, Q), F32) for c in cs}
            db_acc = {c: jnp.zeros((Q, SSD_N), F32) for c in cs}
            dc_acc = {c: jnp.zeros((Q, SSD_N), F32) for c in cs}
            for i, (c, r) in enumerate(inst):
                dgl = tsum(dsn[i] * sr[i])
                dc_acc[c] = dc_acc[c] + dce[i] * e_c[i]
                db_acc[c] = db_acc[c] + dbd0[i] * (dtb[i] * dl_c[i])
                dcb[c] = dcb[c] + dmm0[i] * (lm[i] * dtr[i])
                dcbt[c] = dcbt[c] + dmm0t[i] * (lmt[i] * dtb[i])
                csum_gm0 = csum(dmm0[i] * mm[i])
                rsum_gm = csum(dmm0t[i] * mmt[i] * dtb[i])
                r_de = csum(dcet[i] * cct[c]) * e_r[i]
                r_dl = csum(dbd0t[i] * bct[c]) * dl_r[i]
                dalast = jnp.sum(r_dl * dtr[i], axis=1, keepdims=True) + dgl * gl[i]
                dacs_s[r, c:c + 1, :] = (rsum_gm + r_de - (r_dl + csum_gm0) * dtr[i]
                                         + jnp.where(lane == Q - 1, dalast, 0.0))
                ddt_s[r, c:c + 1, :] = csum_gm0 + r_dl
                dd_s[r, c:c + 1, :] = _dot_nt(ones8, dyr[i] * xr[i])[0:1]
                dxs_ref[rows[c], ps[r]] = dxd[i] * dtb[i][:, :SSD_P] + dsk_ref[SSD_R * gi + r] * dyr[i]
            for c in cs:
                dc_ref[rows[c], :] = dc_acc[c] + _dot(dcb[c], bc_[c])
                db_ref[rows[c], :] = db_acc[c] + _dot(dcbt[c], cc_[c])
        for r in range(SSD_R):
            dS[:, ps[r]] = ds_cur[r]
        for r in range(SSD_R):
            da_ref[r] = _dot_hi(dacs_s[r], lower)
            ddt_ref[r] = ddt_s[r]
            dd_ref[r] = dd_s[r]

    rev = lambda i: nT - 1 - i
    smem = pl.BlockSpec(memory_space=pltpu.SMEM)
    rows_spec = pl.BlockSpec((SSD_R, cpt, Q), lambda g, i: (g, rev(i), 0))
    x_spec = pl.BlockSpec((tt, GW), lambda g, i: (rev(i), g))
    n_spec = pl.BlockSpec((tt, SSD_N), lambda g, i: (rev(i), g))
    rows_shape = jax.ShapeDtypeStruct((SSD_H, nC, Q), F32)
    return pl.pallas_call(
        body, grid=(SSD_G, nT),
        in_specs=[smem, smem, x_spec, n_spec, n_spec, rows_spec,
                  pl.BlockSpec((1, cpt, SSD_N, GW), lambda g, i: (g, rev(i), 0, 0)), x_spec],
        out_specs=[x_spec, n_spec, n_spec, rows_spec, rows_spec, rows_spec],
        out_shape=[jax.ShapeDtypeStruct((T, D_INNER), F32), jax.ShapeDtypeStruct((T, SSD_G * SSD_N), F32),
                   jax.ShapeDtypeStruct((T, SSD_G * SSD_N), F32), rows_shape, rows_shape, rows_shape],
        scratch_shapes=[pltpu.VMEM((SSD_N, GW), F32)] + [pltpu.VMEM((SSD_R, cpt, Q), F32)] * 4,
        compiler_params=_params("parallel", "arbitrary"), name=name,
    )(alog, dskip, xs, bm, cm, dt, sall, dy)


def _ssd_gate_bwd(dtraw, dt, da, ddt_direct, ddrow, alog, dtb, name):
    H, T = dtraw.shape

    def body(raw_ref, dt_ref, da_ref, ddt_ref, dd_ref, alog_ref, dtb_ref, draw_ref, dalog_ref, ddtb_ref, dD_ref):
        a = -jnp.exp(alog_ref[...])
        dav = da_ref[...]
        ddt = ddt_ref[...] + dav * a
        draw = ddt * _sigmoid(raw_ref[...] + dtb_ref[...])
        draw_ref[...] = draw
        dalog_ref[...] = jnp.sum(dav * dt_ref[...], axis=1, keepdims=True) * a
        ddtb_ref[...] = jnp.sum(draw, axis=1, keepdims=True)
        dD_ref[...] = jnp.sum(dd_ref[...], axis=1, keepdims=True)

    return pl.pallas_call(
        body,
        out_shape=[jax.ShapeDtypeStruct((H, T), F32)] + [jax.ShapeDtypeStruct((H, 1), F32)] * 3,
        compiler_params=pltpu.CompilerParams(vmem_limit_bytes=VMEM_LIMIT_BYTES), name=name,
    )(dtraw, dt, da, ddt_direct, ddrow, alog, dtb)


def _final_loss(x, fw, tgt, name):
    T = x.shape[0]
    tt = min(T, 512)
    nT = T // tt

    def body(x_ref, w_ref, t_ref, dx_ref, dw_ref, loss_ref, acc):
        i = pl.program_id(0)

        @pl.when(i == 0)
        def _():
            dw_ref[...] = jnp.zeros_like(dw_ref)
            acc[...] = jnp.zeros_like(acc)

        xv = x_ref[...]
        r = lax.rsqrt(jnp.mean(xv * xv, axis=-1, keepdims=True) + EPS)
        xh = xv * r
        err = xh * w_ref[...] - t_ref[...]
        acc[...] += jnp.sum(err * err, axis=0, keepdims=True)
        dout = err * (1.0 / D_MODEL)
        dw_ref[...] += jnp.sum(dout * xh, axis=0, keepdims=True)
        dxn = dout * w_ref[...]
        dx_ref[...] = r * (dxn - xh * jnp.mean(dxn * xh, axis=-1, keepdims=True))

        @pl.when(i == nT - 1)
        def _():
            loss_ref[...] = (0.5 / D_MODEL) * jnp.sum(acc[...], axis=1, keepdims=True)

    row = pl.BlockSpec((tt, D_MODEL), lambda i: (i, 0))
    vec = pl.BlockSpec((1, D_MODEL), lambda i: (0, 0))
    return pl.pallas_call(
        body, grid=(nT,),
        in_specs=[row, vec, row],
        out_specs=[row, vec, pl.BlockSpec((1, 1), lambda i: (0, 0))],
        out_shape=[jax.ShapeDtypeStruct((T, D_MODEL), F32), jax.ShapeDtypeStruct((1, D_MODEL), F32),
                   jax.ShapeDtypeStruct((1, 1), F32)],
        scratch_shapes=[pltpu.VMEM((1, D_MODEL), F32)],
        compiler_params=_params("arbitrary"), name=name,
    )(x, fw, tgt)


def _adamw(parts, w, m, v, name):
    R, C = w.shape
    tr = 128 if R % 128 == 0 else R

    def body(p_ref, w_ref, m_ref, v_ref, g_ref, d_ref, nm_ref, nv_ref):
        g = p_ref[0].astype(F32)
        for s in range(1, N_DEV):
            g = g + p_ref[s].astype(F32)
        mn = ADAM_B1 * m_ref[...] + (1.0 - ADAM_B1) * g
        vn = ADAM_B2 * v_ref[...] + (1.0 - ADAM_B2) * (g * g)
        mh = mn / (1.0 - ADAM_B1 ** ADAM_STEP)
        vh = vn / (1.0 - ADAM_B2 ** ADAM_STEP)
        g_ref[...] = g
        d_ref[...] = -ADAM_LR * (mh / (jnp.sqrt(vh) + ADAM_EPS) + ADAM_WD * w_ref[...])
        nm_ref[...] = mn
        nv_ref[...] = vn

    blk = pl.BlockSpec((tr, C), lambda i: (i, 0))
    return pl.pallas_call(
        body, grid=(R // tr,),
        in_specs=[pl.BlockSpec((N_DEV, tr, C), lambda i: (0, i, 0)), blk, blk, blk],
        out_specs=[blk] * 4,
        out_shape=[jax.ShapeDtypeStruct((R, C), F32)] * 4,
        compiler_params=_params("parallel"), name=name,
    )(parts, w, m, v)


def _me():
    x, y, c = lax.axis_index("x"), lax.axis_index("y"), lax.axis_index("c")
    return x, y, c


def _peer(d):
    x, y, c = _me()
    px = 1 - x if d & 4 else x
    py = 1 - y if d & 2 else y
    pc = 1 - c if d & 1 else c
    return (px, py, pc), 4 * px + 2 * py + pc


def _exchange(arrs, bcast, name):
    n = len(arrs)

    def body(*refs):
        ins, outs = refs[:n], refs[n:2 * n]
        ssem, rsem, lsem = refs[2 * n:]
        x, y, c = _me()
        me = 4 * x + 2 * y + c

        def src(a, dest):
            return ins[a] if bcast[a] else ins[a].at[dest]

        local = [pltpu.make_async_copy(src(a, me), outs[a].at[me], lsem.at[a]) for a in range(n)]
        for cp in local:
            cp.start()
        sends, recvs = [], []
        for a in range(n):
            for d in range(1, N_DEV):
                peer, pid = _peer(d)
                sends.append(pltpu.make_async_remote_copy(
                    src_ref=src(a, pid), dst_ref=outs[a].at[me], send_sem=ssem.at[a, d - 1],
                    recv_sem=rsem.at[a, d - 1], device_id=peer, device_id_type=MESH))
                recvs.append(pltpu.make_async_remote_copy(
                    src_ref=src(a, pid), dst_ref=outs[a].at[pid], send_sem=ssem.at[a, d - 1],
                    recv_sem=rsem.at[a, d - 1], device_id=peer, device_id_type=MESH))
        for cp in sends:
            cp.start()
        for cp in recvs:
            cp.wait_recv()
        for cp in sends:
            cp.wait_send()
        for cp in local:
            cp.wait()

    def out_shape(a, is_b):
        return jax.ShapeDtypeStruct((N_DEV,) + (a.shape if is_b else a.shape[1:]), a.dtype)

    anyspec = pl.BlockSpec(memory_space=pl.ANY)
    return pl.pallas_call(
        body,
        in_specs=[anyspec] * n, out_specs=[anyspec] * n,
        out_shape=[out_shape(a, b) for a, b in zip(arrs, bcast)],
        scratch_shapes=[pltpu.SemaphoreType.DMA((n, N_DEV - 1)), pltpu.SemaphoreType.DMA((n, N_DEV - 1)),
                        pltpu.SemaphoreType.DMA((n,))],
        name=name,
    )(*arrs)


def _to_rows(cols, chunk):
    T, H = cols.shape
    return cols.T.reshape(H, T // chunk, chunk)


def _from_rows(rows):
    return rows.T


def _pad_cols(a, width):
    return jnp.pad(a, ((0, 0), (0, width - a.shape[1])))


def _local_step(x, tgt, p):
    T = x.shape[0]
    zb = lambda n: jnp.zeros((1, n), F32)
    gw = p["gdn_w_in"]
    g_wparts = [gw[:, 0:1024], gw[:, 1024:2048], gw[:, 2048:4096], gw[:, 4096:6144], _pad_cols(gw[:, 6144:6176], PAD_W)]
    nw0, nw1 = p["norm_w"][0:1], p["norm_w"][1:2]
    h0, (q_pre, k_pre, v_pre, z0, ab) = _norm_inproj(x, nw0, g_wparts, "gdn_inproj")
    gcw = p["gdn_conv_w"]
    cw_q, cw_k, cw_v = gcw[:, 0:1024], gcw[:, 1024:2048], gcw[:, 2048:4096]
    q = _conv_fwd(q_pre, cw_q, zb(1024), True, GDN_DK ** -0.5, "gdn_conv_q")
    k = _conv_fwd(k_pre, cw_k, zb(1024), True, 1.0, "gdn_conv_k")
    v = _conv_fwd(v_pre, cw_v, zb(2048), False, 1.0, "gdn_conv_v")
    braw = _to_rows(ab[:, 0:GDN_HV], GDN_CHUNK)
    araw = _to_rows(ab[:, GDN_HV:2 * GDN_HV], GDN_CHUNK)
    g_alog, g_dtb = p["gdn_a_log"].reshape(-1), p["gdn_dt_bias"].reshape(-1)
    g_u, g_w, g_pm, g_ti, g_rows, beta_rows, gc_rows = _gdn_prep(q, k, v, araw, braw, g_alog, g_dtb, "gdn_prep")
    o0, g_vn, g_sall = _gdn_state_fwd(q, k, g_u, g_w, g_pm, gc_rows, "gdn_state_fwd")
    x1 = _out_fwd(o0, z0, p["gdn_norm_w"], p["gdn_w_out"], x, GDN_DK, False, "gdn_out")
    sw = p["ssd_w_in"]
    s_wparts = [sw[:, 0:2048], sw[:, 2048:4096], sw[:, 4096:5120], sw[:, 5120:6144], _pad_cols(sw[:, 6144:6176], PAD_W)]
    h1, (z1, xs_pre, b_pre, c_pre, dtp) = _norm_inproj(x1, nw1, s_wparts, "ssd_inproj")
    scw, scb = p["ssd_conv_w"], p["ssd_conv_b"]
    xs = _conv_fwd(xs_pre, scw[:, 0:2048], scb[:, 0:2048], False, 1.0, "ssd_conv_x")
    bm = _conv_fwd(b_pre, scw[:, 2048:3072], scb[:, 2048:3072], False, 1.0, "ssd_conv_b")
    cm = _conv_fwd(c_pre, scw[:, 3072:4096], scb[:, 3072:4096], False, 1.0, "ssd_conv_c")
    dtraw = _to_rows(dtp[:, 0:SSD_H], SSD_CHUNK)
    s_alog, s_dtb, s_d = p["ssd_a_log"].reshape(-1), p["ssd_dt_bias"].reshape(-1), p["ssd_d"].reshape(-1)
    y1, s_sall, dt_rows = _ssd_scan_fwd(xs, bm, cm, dtraw, s_alog, s_dtb, s_d, "ssd_scan_fwd")
    x2 = _out_fwd(y1, z1, p["ssd_norm_w"], p["ssd_w_out"], x1, D_INNER // SSD_G, True, "ssd_out")
    dx2, d_fw, loss = _final_loss(x2, p["final_norm_w"].reshape(1, -1), tgt, "final_loss")
    dy1, dz1, d_snw, yn1 = _out_bwd(dx2, y1, z1, p["ssd_norm_w"], p["ssd_w_out"], D_INNER // SSD_G, True, "ssd_out_bwd")
    d_swout = _matmul_tn(yn1, dx2, "ssd_wout_grad")
    dxs, dbm, dcm, da_rows, ddt_rows, dd_rows = _ssd_scan_bwd(xs, bm, cm, dt_rows, s_sall, dy1, s_alog, s_d, "ssd_scan_bwd")
    col = lambda a: a.reshape(-1, 1)
    dtraw_g, d_salog, d_sdtb, d_sd = _ssd_gate_bwd(
        dtraw.reshape(SSD_H, T), dt_rows.reshape(SSD_H, T), da_rows.reshape(SSD_H, T),
        ddt_rows.reshape(SSD_H, T), dd_rows.reshape(SSD_H, T), col(s_alog), col(s_dtb), "ssd_gate_bwd")
    dxs_pre, dcw_x, dcb_x = _conv_bwd(xs_pre, scw[:, 0:2048], scb[:, 0:2048], dxs, False, 1.0, "ssd_conv_x_bwd")
    db_pre, dcw_b, dcb_b = _conv_bwd(b_pre, scw[:, 2048:3072], scb[:, 2048:3072], dbm, False, 1.0, "ssd_conv_b_bwd")
    dc_pre, dcw_c, dcb_c = _conv_bwd(c_pre, scw[:, 3072:4096], scb[:, 3072:4096], dcm, False, 1.0, "ssd_conv_c_bwd")
    ddtp = _pad_cols(_from_rows(dtraw_g), PAD_W)
    s_dparts = [dz1, dxs_pre, db_pre, dc_pre, ddtp]
    dx1, d_nw1 = _inproj_bwd(x1, nw1, s_dparts, s_wparts, dx2, "ssd_inproj_bwd")
    s_dw = [_matmul_tn(h1, d, "ssd_win_grad_%d" % n) for n, d in enumerate(s_dparts)]
    d_swin = jnp.concatenate(s_dw[:4] + [s_dw[4][:, 0:SSD_H]], axis=1)
    do0, dz0, d_gnw, yn0 = _out_bwd(dx1, o0, z0, p["gdn_norm_w"], p["gdn_w_out"], GDN_DK, False, "gdn_out_bwd")
    d_gwout = _matmul_tn(yn0, dx1, "gdn_wout_grad")
    g_dvn, g_dkd, g_dgl = _gdn_state_bwd(q, k, g_w, g_pm, g_vn, g_sall, gc_rows, do0, "gdn_state_bwd")
    dq, dk, dv, dg_rows, dbeta_rows = _gdn_local_bwd(q, k, v, gc_rows, beta_rows, g_ti, g_u, g_w, g_pm, g_vn, g_sall,
                                                     do0, g_dvn, g_dkd, g_dgl, "gdn_local_bwd")
    da_g, db_g, d_galog, d_gdtb = _gdn_gate_bwd(
        araw.reshape(GDN_HV, T), braw.reshape(GDN_HV, T), dg_rows.reshape(GDN_HV, T),
        dbeta_rows.reshape(GDN_HV, T), col(g_alog), col(g_dtb), "gdn_gate_bwd")
    dq_pre, dcw_q, _ = _conv_bwd(q_pre, cw_q, zb(1024), dq, True, GDN_DK ** -0.5, "gdn_conv_q_bwd")
    dk_pre, dcw_k, _ = _conv_bwd(k_pre, cw_k, zb(1024), dk, True, 1.0, "gdn_conv_k_bwd")
    dv_pre, dcw_v, _ = _conv_bwd(v_pre, cw_v, zb(2048), dv, False, 1.0, "gdn_conv_v_bwd")
    dab = _pad_cols(jnp.concatenate([_from_rows(db_g), _from_rows(da_g)], axis=1), PAD_W)
    g_dparts = [dq_pre, dk_pre, dv_pre, dz0, dab]
    dx0, d_nw0 = _inproj_bwd(x, nw0, g_dparts, g_wparts, dx1, "gdn_inproj_bwd")
    g_dw = [_matmul_tn(h0, d, "gdn_win_grad_%d" % n) for n, d in enumerate(g_dparts)]
    d_gwin = jnp.concatenate(g_dw[:4] + [g_dw[4][:, 0:2 * GDN_HV]], axis=1)
    grads = {
        "norm_w": jnp.concatenate([d_nw0, d_nw1], axis=0),
        "gdn_w_in": d_gwin,
        "gdn_conv_w": jnp.concatenate([dcw_q, dcw_k, dcw_v], axis=1),
        "gdn_a_log": d_galog.reshape(1, -1),
        "gdn_dt_bias": d_gdtb.reshape(1, -1),
        "gdn_norm_w": d_gnw,
        "gdn_w_out": d_gwout,
        "ssd_w_in": d_swin,
        "ssd_conv_w": jnp.concatenate([dcw_x, dcw_b, dcw_c], axis=1),
        "ssd_conv_b": jnp.concatenate([dcb_x, dcb_b, dcb_c], axis=1),
        "ssd_dt_bias": d_sdtb.reshape(1, -1),
        "ssd_a_log": d_salog.reshape(1, -1),
        "ssd_d": d_sd.reshape(1, -1),
        "ssd_norm_w": d_snw,
        "ssd_w_out": d_swout,
        "final_norm_w": d_fw,
    }
    return loss, dx0, grads


WEIGHTS = ["norm_w", "gdn_w_in", "gdn_conv_w", "gdn_a_log", "gdn_dt_bias", "gdn_norm_w", "gdn_w_out", "ssd_w_in",
           "ssd_conv_w", "ssd_conv_b", "ssd_dt_bias", "ssd_a_log", "ssd_d", "ssd_norm_w", "ssd_w_out", "final_norm_w"]
COL_SHARDED = ["gdn_w_in", "ssd_w_in"]
ROW_SHARDED = ["gdn_w_out", "ssd_w_out"]
SMALL_SHARDED = ["gdn_conv_w", "ssd_conv_w", "ssd_conv_b", "ssd_norm_w"]
REPLICATED = ["norm_w", "gdn_a_log", "gdn_dt_bias", "gdn_norm_w", "ssd_dt_bias", "ssd_a_log", "ssd_d", "final_norm_w"]


def _pack(arrs):
    return jnp.concatenate([a.reshape(-1) for a in arrs]).reshape(1, -1)


def _unpack(flat, shapes):
    out, pos = [], 0
    for s in shapes:
        n = 1
        for dim in s:
            n *= dim
        out.append(flat[pos:pos + n].reshape(s))
        pos += n
    return out


def _cols_to_shards(full):
    R, C = full.shape
    return full.reshape(R, N_DEV, C // N_DEV).transpose(1, 0, 2)


def _shards_to_cols(shards):
    n, R, c = shards.shape
    return shards.transpose(1, 0, 2).reshape(R, n * c)


def kernel(x, norm_w, gdn_w_in, gdn_conv_w, gdn_a_log, gdn_dt_bias, gdn_norm_w, gdn_w_out, ssd_w_in, ssd_conv_w, ssd_conv_b, ssd_dt_bias, ssd_a_log, ssd_d, ssd_norm_w, ssd_w_out, final_norm_w, loss_target, m_norm_w, m_gdn_w_in, m_gdn_conv_w, m_gdn_a_log, m_gdn_dt_bias, m_gdn_norm_w, m_gdn_w_out, m_ssd_w_in, m_ssd_conv_w, m_ssd_conv_b, m_ssd_dt_bias, m_ssd_a_log, m_ssd_d, m_ssd_norm_w, m_ssd_w_out, m_final_norm_w, v_norm_w, v_gdn_w_in, v_gdn_conv_w, v_gdn_a_log, v_gdn_dt_bias, v_gdn_norm_w, v_gdn_w_out, v_ssd_w_in, v_ssd_conv_w, v_ssd_conv_b, v_ssd_dt_bias, v_ssd_a_log, v_ssd_d, v_ssd_norm_w, v_ssd_w_out, v_final_norm_w):
    w = dict(norm_w=norm_w, gdn_w_in=gdn_w_in[0], gdn_conv_w=gdn_conv_w[0], gdn_a_log=gdn_a_log,
             gdn_dt_bias=gdn_dt_bias, gdn_norm_w=gdn_norm_w, gdn_w_out=gdn_w_out[0], ssd_w_in=ssd_w_in[0],
             ssd_conv_w=ssd_conv_w[0], ssd_conv_b=ssd_conv_b, ssd_dt_bias=ssd_dt_bias, ssd_a_log=ssd_a_log,
             ssd_d=ssd_d, ssd_norm_w=ssd_norm_w, ssd_w_out=ssd_w_out[0], final_norm_w=final_norm_w.reshape(1, -1))
    m = dict(norm_w=m_norm_w, gdn_w_in=m_gdn_w_in[0], gdn_conv_w=m_gdn_conv_w[0], gdn_a_log=m_gdn_a_log,
             gdn_dt_bias=m_gdn_dt_bias, gdn_norm_w=m_gdn_norm_w, gdn_w_out=m_gdn_w_out[0], ssd_w_in=m_ssd_w_in[0],
             ssd_conv_w=m_ssd_conv_w[0], ssd_conv_b=m_ssd_conv_b, ssd_dt_bias=m_ssd_dt_bias, ssd_a_log=m_ssd_a_log,
             ssd_d=m_ssd_d, ssd_norm_w=m_ssd_norm_w, ssd_w_out=m_ssd_w_out[0], final_norm_w=m_final_norm_w.reshape(1, -1))
    v = dict(norm_w=v_norm_w, gdn_w_in=v_gdn_w_in[0], gdn_conv_w=v_gdn_conv_w[0], gdn_a_log=v_gdn_a_log,
             gdn_dt_bias=v_gdn_dt_bias, gdn_norm_w=v_gdn_norm_w, gdn_w_out=v_gdn_w_out[0], ssd_w_in=v_ssd_w_in[0],
             ssd_conv_w=v_ssd_conv_w[0], ssd_conv_b=v_ssd_conv_b, ssd_dt_bias=v_ssd_dt_bias, ssd_a_log=v_ssd_a_log,
             ssd_d=v_ssd_d, ssd_norm_w=v_ssd_norm_w, ssd_w_out=v_ssd_w_out[0], final_norm_w=v_final_norm_w.reshape(1, -1))
    out_shapes = {n: a.shape for n, a in zip(
        WEIGHTS, [norm_w, gdn_w_in, gdn_conv_w, gdn_a_log, gdn_dt_bias, gdn_norm_w, gdn_w_out, ssd_w_in, ssd_conv_w,
                  ssd_conv_b, ssd_dt_bias, ssd_a_log, ssd_d, ssd_norm_w, ssd_w_out, final_norm_w])}

    small_shapes = [w[n].shape for n in SMALL_SHARDED]
    gathered = _exchange([_mx(w[n]) for n in COL_SHARDED + ROW_SHARDED] + [_pack([w[n] for n in SMALL_SHARDED])],
                         [True] * 5, "gather_weights")
    full = dict(w)
    for n, gth in zip(COL_SHARDED, gathered[0:2]):
        full[n] = _shards_to_cols(gth)
    for n, gth in zip(ROW_SHARDED, gathered[2:4]):
        full[n] = gth.reshape(-1, gth.shape[-1])
    small_all = [_unpack(gathered[4][s, 0], small_shapes) for s in range(N_DEV)]
    for idx, n in enumerate(SMALL_SHARDED):
        full[n] = jnp.concatenate([small_all[s][idx] for s in range(N_DEV)], axis=-1)

    loss, dx, grads = _local_step(x[0], loss_target[0], full)

    send_small = jnp.concatenate(
        [_cols_to_shards(grads[n]).reshape(N_DEV, -1) for n in SMALL_SHARDED], axis=1)[:, None, :]
    rep_shapes = [w[n].shape for n in REPLICATED]
    recv = _exchange(
        [_cols_to_shards(grads[n]).astype(GRAD_WIRE_DTYPE) for n in COL_SHARDED]
        + [grads[n].reshape(N_DEV, -1, D_MODEL).astype(GRAD_WIRE_DTYPE) for n in ROW_SHARDED]
        + [send_small, _pack([grads[n] for n in REPLICATED])],
        [False] * 5 + [True], "exchange_grads")

    res = {}
    for n, parts in zip(COL_SHARDED + ROW_SHARDED, recv[0:4]):
        res[n] = _adamw(parts, w[n], m[n], v[n], "adamw_" + n)
    small_res = _adamw(recv[4], *[_pack([t[n] for n in SMALL_SHARDED]) for t in (w, m, v)], "adamw_small")
    rep_res = _adamw(recv[5], *[_pack([t[n] for n in REPLICATED]) for t in (w, m, v)], "adamw_replicated")
    for k4 in range(4):
        for n, a in zip(SMALL_SHARDED, _unpack(small_res[k4][0], small_shapes)):
            res.setdefault(n, [None] * 4)[k4] = a
        for n, a in zip(REPLICATED, _unpack(rep_res[k4][0], rep_shapes)):
            res.setdefault(n, [None] * 4)[k4] = a

    loss = lax.psum(loss[0, 0], ("x", "y", "c"))
    outs = [loss, dx[None]]
    for k4 in range(4):
        outs += [res[n][k4].reshape(out_shapes[n]) for n in WEIGHTS]
    return tuple(outs)
```

```python
import jax
import jax.numpy as jnp
from jax import lax
from jax.experimental import pallas as pl
from jax.experimental.pallas import tpu as pltpu

F32 = jnp.float32
MXU_DTYPE = jnp.bfloat16
GRAD_WIRE_DTYPE = jnp.bfloat16
HI = lax.Precision.HIGHEST
EPS = 1e-6
VMEM_LIMIT_BYTES = 56 * 1024 * 1024
N_DEV = 8
MESH = pl.DeviceIdType.MESH

D_MODEL = 1024
CONV_K = 4
GDN_HV = 16
GDN_DK = 128
GDN_CHUNK = 64
SSD_H = 32
SSD_P = 64
SSD_N = 128
SSD_G = 8
SSD_R = SSD_H // SSD_G
SSD_CHUNK = 128
D_INNER = 2048
PAD_W = 128

ADAM_LR = 0.001
ADAM_B1 = 0.9
ADAM_B2 = 0.999
ADAM_EPS = 1e-08
ADAM_WD = 0.01
ADAM_STEP = 10


def _params(*sem):
    return pltpu.CompilerParams(dimension_semantics=sem, vmem_limit_bytes=VMEM_LIMIT_BYTES)


def _mx(a):
    return a.astype(MXU_DTYPE)


def _dot(a, b):
    return jnp.dot(_mx(a), _mx(b), preferred_element_type=F32)


def _dot_nt(a, b):
    return lax.dot_general(_mx(a), _mx(b), (((1,), (1,)), ((), ())), preferred_element_type=F32)


def _dot_tn(a, b):
    return lax.dot_general(_mx(a), _mx(b), (((0,), (0,)), ((), ())), preferred_element_type=F32)


def _dot_hi(a, b):
    return jnp.dot(a, b, precision=HI, preferred_element_type=F32)


def _sigmoid(x):
    return 1.0 / (1.0 + jnp.exp(-x))


def _silu(x):
    return x * _sigmoid(x)


def _dsilu(x):
    s = _sigmoid(x)
    return s * (1.0 + x * (1.0 - s))


def _softplus(x):
    return jnp.maximum(x, 0.0) + jnp.log1p(jnp.exp(-jnp.abs(x)))


def _col(r, eye):
    return jnp.sum(jnp.where(eye, r, 0.0), axis=1, keepdims=True)


def _row(c, eye):
    return jnp.sum(jnp.where(eye, c, 0.0), axis=0, keepdims=True)


def _col_bcast(r, n):
    return jnp.broadcast_to(r, (n, n)).T


def _masks(n):
    r = lax.broadcasted_iota(jnp.int32, (n, n), 0)
    c = lax.broadcasted_iota(jnp.int32, (n, n), 1)
    return r >= c, r > c, r == c, r, c


def _with_exchange(comm):
    arrs, bcast = comm if comm else ([], [])
    nc = len(arrs)
    anyspec = pl.BlockSpec(memory_space=pl.ANY)

    def wrap(compute, n_in, n_out):
        def body(*refs):
            cin, cout = refs[n_in:n_in + nc], refs[n_in + nc + n_out:n_in + 2 * nc + n_out]
            sems = refs[n_in + 2 * nc + n_out:n_in + 2 * nc + n_out + 3]
            rest = refs[:n_in] + refs[n_in + nc:n_in + nc + n_out] + refs[n_in + 2 * nc + n_out + (3 if nc else 0):]
            if nc:
                @pl.when(pl.program_id(0) == 0)
                def _():
                    _Exchange(cin, cout, bcast, *sems).begin()
            compute(*rest)
            if nc:
                @pl.when(pl.program_id(0) == pl.num_programs(0) - 1)
                def _():
                    _Exchange(cin, cout, bcast, *sems).finish()
        return body

    return dict(arrs=list(arrs), nc=nc, wrap=wrap, in_specs=[anyspec] * nc, out_specs=[anyspec] * nc,
                out_shape=_exchange_out_shapes(arrs, bcast), scratch=_exchange_semaphores(nc) if nc else [])


def _norm_inproj(x, nw, wparts, name, comm=None):
    T = x.shape[0]
    tt = min(T, 256)
    n = len(wparts)
    ex = _with_exchange(comm)

    def compute(x_ref, nw_ref, *refs):
        w_refs, h_ref, o_refs = refs[:n], refs[n], refs[n + 1:]
        xv = x_ref[...]
        r = lax.rsqrt(jnp.mean(xv * xv, axis=-1, keepdims=True) + EPS)
        h = _mx(xv * r * nw_ref[...])
        h_ref[...] = h
        for w_ref, o_ref in zip(w_refs, o_refs):
            o_ref[...] = jnp.dot(h, w_ref[...], preferred_element_type=F32)

    row = lambda width: pl.BlockSpec((tt, width), lambda i: (i, 0))
    full = lambda a: pl.BlockSpec(a.shape, lambda i: (0, 0))
    outs = pl.pallas_call(
        ex["wrap"](compute, 2 + n, 1 + n), grid=(T // tt,),
        in_specs=[row(D_MODEL), full(nw)] + [full(w) for w in wparts] + ex["in_specs"],
        out_specs=[row(D_MODEL)] + [row(w.shape[1]) for w in wparts] + ex["out_specs"],
        out_shape=[jax.ShapeDtypeStruct((T, D_MODEL), MXU_DTYPE)]
        + [jax.ShapeDtypeStruct((T, w.shape[1]), F32) for w in wparts] + ex["out_shape"],
        scratch_shapes=ex["scratch"],
        compiler_params=_params("arbitrary" if comm else "parallel"), name=name,
    )(x, nw, *wparts, *ex["arrs"])
    if comm:
        return outs[0], outs[1:1 + n], outs[1 + n:]
    return outs[0], outs[1:]


def _inproj_bwd(x, nw, dparts, wparts, dres, name):
    T = x.shape[0]
    tt = min(T, 256)
    n = len(wparts)

    def body(x_ref, nw_ref, dres_ref, *refs):
        d_refs, w_refs, dx_ref, dnw_ref = refs[:n], refs[n:2 * n], refs[2 * n], refs[2 * n + 1]

        @pl.when(pl.program_id(0) == 0)
        def _():
            dnw_ref[...] = jnp.zeros_like(dnw_ref)

        dh = _dot_nt(d_refs[0][...], w_refs[0][...])
        for d_ref, w_ref in zip(d_refs[1:], w_refs[1:]):
            dh = dh + _dot_nt(d_ref[...], w_ref[...])
        xv = x_ref[...]
        r = lax.rsqrt(jnp.mean(xv * xv, axis=-1, keepdims=True) + EPS)
        xh = xv * r
        dnw_ref[...] += jnp.sum(dh * xh, axis=0, keepdims=True)
        dxn = dh * nw_ref[...]
        dx_ref[...] = dres_ref[...] + r * (dxn - xh * jnp.mean(dxn * xh, axis=-1, keepdims=True))

    row = lambda width: pl.BlockSpec((tt, width), lambda i: (i, 0))
    full = lambda a: pl.BlockSpec(a.shape, lambda i: (0, 0))
    return pl.pallas_call(
        body, grid=(T // tt,),
        in_specs=[row(D_MODEL), full(nw), row(D_MODEL)] + [row(d.shape[1]) for d in dparts]
        + [full(w) for w in wparts],
        out_specs=[row(D_MODEL), pl.BlockSpec((1, D_MODEL), lambda i: (0, 0))],
        out_shape=[jax.ShapeDtypeStruct((T, D_MODEL), F32), jax.ShapeDtypeStruct((1, D_MODEL), F32)],
        compiler_params=_params("arbitrary"), name=name,
    )(x, nw, dres, *dparts, *wparts)


def _matmul_tn(a, b, name):
    T, K = a.shape
    N = b.shape[1]
    tt = min(T, 512)
    tn = min(N, 1024)

    def body(a_ref, b_ref, o_ref):
        @pl.when(pl.program_id(1) == 0)
        def _():
            o_ref[...] = jnp.zeros_like(o_ref)

        o_ref[...] += _dot_tn(a_ref[...], b_ref[...])

    return pl.pallas_call(
        body, grid=(N // tn, T // tt),
        in_specs=[pl.BlockSpec((tt, K), lambda n, t: (t, 0)), pl.BlockSpec((tt, tn), lambda n, t: (t, n))],
        out_specs=pl.BlockSpec((K, tn), lambda n, t: (0, n)),
        out_shape=jax.ShapeDtypeStruct((K, N), F32),
        compiler_params=_params("parallel", "arbitrary"), name=name,
    )(a, b)


def _out_fwd(o, z, w, wout, xres, gs, gate_first, name):
    T = o.shape[0]
    tt = min(T, 256)
    wide = w.shape[1] == D_INNER

    def body(o_ref, z_ref, w_ref, wout_ref, x_ref, out_ref, yn):
        for g0 in range(0, D_INNER, gs):
            sl = slice(g0, g0 + gs)
            og, zg = o_ref[:, sl], z_ref[:, sl]
            wg = w_ref[:, sl] if wide else w_ref[...]
            if gate_first:
                u = og * _silu(zg)
                r = lax.rsqrt(jnp.mean(u * u, axis=-1, keepdims=True) + EPS)
                yn[:, sl] = _mx(u * r * wg)
            else:
                r = lax.rsqrt(jnp.mean(og * og, axis=-1, keepdims=True) + EPS)
                yn[:, sl] = _mx(og * r * wg * _silu(zg))
        out_ref[...] = x_ref[...] + jnp.dot(yn[...], wout_ref[...], preferred_element_type=F32)

    row = lambda width: pl.BlockSpec((tt, width), lambda i: (i, 0))
    full = lambda a: pl.BlockSpec(a.shape, lambda i: (0, 0))
    return pl.pallas_call(
        body, grid=(T // tt,),
        in_specs=[row(D_INNER), row(D_INNER), full(w), full(wout), row(D_MODEL)],
        out_specs=row(D_MODEL),
        out_shape=jax.ShapeDtypeStruct((T, D_MODEL), F32),
        scratch_shapes=[pltpu.VMEM((tt, D_INNER), MXU_DTYPE)],
        compiler_params=_params("parallel"), name=name,
    )(o, z, w, wout, xres)


def _out_bwd(dx, o, z, w, wout, gs, gate_first, name, comm=None):
    T = o.shape[0]
    tt = min(T, 256)
    wide = w.shape[1] == D_INNER

    def body(dx_ref, o_ref, z_ref, w_ref, wout_ref, do_ref, dz_ref, dw_ref, yn_ref):
        @pl.when(pl.program_id(0) == 0)
        def _():
            dw_ref[...] = jnp.zeros_like(dw_ref)

        dyn = _dot_nt(dx_ref[...], wout_ref[...])
        dw_acc = jnp.zeros((1, gs), F32)
        for g0 in range(0, D_INNER, gs):
            sl = slice(g0, g0 + gs)
            og, zg, dg = o_ref[:, sl], z_ref[:, sl], dyn[:, sl]
            wg = w_ref[:, sl] if wide else w_ref[...]
            sz = _silu(zg)
            if gate_first:
                u = og * sz
                r = lax.rsqrt(jnp.mean(u * u, axis=-1, keepdims=True) + EPS)
                uh = u * r
                yn_ref[:, sl] = _mx(uh * wg)
                dw_g = jnp.sum(dg * uh, axis=0, keepdims=True)
                duh = dg * wg
                du = r * (duh - uh * jnp.mean(duh * uh, axis=-1, keepdims=True))
                do_ref[:, sl] = du * sz
                dz_ref[:, sl] = du * og * _dsilu(zg)
            else:
                r = lax.rsqrt(jnp.mean(og * og, axis=-1, keepdims=True) + EPS)
                oh = og * r
                yn_ref[:, sl] = _mx(oh * wg * sz)
                dw_g = jnp.sum(dg * oh * sz, axis=0, keepdims=True)
                doh = dg * wg * sz
                dz_ref[:, sl] = dg * oh * wg * _dsilu(zg)
                do_ref[:, sl] = r * (doh - oh * jnp.mean(doh * oh, axis=-1, keepdims=True))
            if wide:
                dw_ref[:, sl] += dw_g
            else:
                dw_acc = dw_acc + dw_g
        if not wide:
            dw_ref[...] += dw_acc

    row = lambda width: pl.BlockSpec((tt, width), lambda i: (i, 0))
    full = lambda a: pl.BlockSpec(a.shape, lambda i: (0, 0))
    ex = _with_exchange(comm)
    outs = pl.pallas_call(
        ex["wrap"](body, 5, 4), grid=(T // tt,),
        in_specs=[row(D_MODEL), row(D_INNER), row(D_INNER), full(w), full(wout)] + ex["in_specs"],
        out_specs=[row(D_INNER), row(D_INNER), full(w), row(D_INNER)] + ex["out_specs"],
        out_shape=[jax.ShapeDtypeStruct((T, D_INNER), F32), jax.ShapeDtypeStruct((T, D_INNER), F32),
                   jax.ShapeDtypeStruct(w.shape, F32), jax.ShapeDtypeStruct((T, D_INNER), MXU_DTYPE)]
        + ex["out_shape"],
        scratch_shapes=ex["scratch"],
        compiler_params=_params("arbitrary"), name=name,
    )(dx, o, z, w, wout, *ex["arrs"])
    outs = list(outs)
    return outs[:4] + ([outs[4:]] if comm else [])


HALO = 8
CONV_STRIP = 16


def _conv_fwd(pre, w, b, l2, scale, name):
    T, C = pre.shape
    tt = min(T, 512)
    tc = min(C, 1024)
    strip = tt

    def body(pre_ref, halo_ref, w_ref, b_ref, out_ref, P):
        i = pl.program_id(0)
        P[0:HALO, :] = jnp.where(i > 0, halo_ref[...], 0.0)
        P[HALO:HALO + tt, :] = pre_ref[...]
        wj = [w_ref[j:j + 1, :] for j in range(CONV_K)]
        bias = b_ref[...]
        for r0 in range(0, tt, strip):
            acc = bias + wj[0] * P[pl.ds(HALO - 3 + r0, strip), :]
            for j in range(1, CONV_K):
                acc = acc + wj[j] * P[pl.ds(HALO - 3 + j + r0, strip), :]
            s = _silu(acc)
            if l2:
                sls = [slice(g0, g0 + GDN_DK) for g0 in range(0, tc, GDN_DK)]
                rr = [lax.rsqrt(jnp.sum(s[:, sl] * s[:, sl], axis=-1, keepdims=True) + EPS) for sl in sls]
                for sl, r in zip(sls, rr):
                    out_ref[r0:r0 + strip, sl] = s[:, sl] * r * scale
            else:
                out_ref[r0:r0 + strip, :] = s

    return pl.pallas_call(
        body, grid=(T // tt, C // tc),
        in_specs=[pl.BlockSpec((tt, tc), lambda i, j: (i, j)),
                  pl.BlockSpec((HALO, tc), lambda i, j: (jnp.maximum(i * (tt // HALO) - 1, 0), j)),
                  pl.BlockSpec((CONV_K, tc), lambda i, j: (0, j)),
                  pl.BlockSpec((1, tc), lambda i, j: (0, j))],
        out_specs=pl.BlockSpec((tt, tc), lambda i, j: (i, j)),
        out_shape=jax.ShapeDtypeStruct((T, C), F32),
        scratch_shapes=[pltpu.VMEM((HALO + tt, tc), F32)],
        compiler_params=_params("parallel", "parallel"), name=name,
    )(pre, pre, w, b)


def _conv_bwd(pre, w, b, dpost, l2, scale, name):
    T, C = pre.shape
    tt = min(T, 512)
    tc = min(C, 1024 if l2 else 512)
    strip = tt if l2 else CONV_STRIP
    nT = T // tt
    ext = tt + HALO

    def body(pre_ref, hp_ref, hn_ref, dpost_ref, dn_ref, w_ref, b_ref, dpre_ref, dw_ref, db_ref, P, Q):
        i = pl.program_id(1)

        @pl.when(i == 0)
        def _():
            dw_ref[...] = jnp.zeros_like(dw_ref)
            db_ref[...] = jnp.zeros_like(db_ref)

        P[0:HALO, :] = jnp.where(i > 0, hp_ref[...], 0.0)
        P[HALO:HALO + tt, :] = pre_ref[...]
        P[HALO + tt:HALO + ext, :] = hn_ref[...]
        wj = [w_ref[j:j + 1, :] for j in range(CONV_K)]
        bias = b_ref[...]
        keep_next = jnp.where(i < nT - 1, 1.0, 0.0)
        for r0 in list(range(0, tt, strip)) + [tt]:
            n = strip if r0 < tt else HALO
            cpre = bias + wj[0] * P[pl.ds(HALO - 3 + r0, n), :]
            for j in range(1, CONV_K):
                cpre = cpre + wj[j] * P[pl.ds(HALO - 3 + j + r0, n), :]
            dy = dpost_ref[r0:r0 + n, :] if r0 < tt else dn_ref[...] * keep_next
            sg = _sigmoid(cpre)
            ds_c = sg * (1.0 + cpre * (1.0 - sg))
            if l2:
                s = cpre * sg
                sls = [slice(g0, g0 + GDN_DK) for g0 in range(0, tc, GDN_DK)]
                rr = [lax.rsqrt(jnp.sum(s[:, sl] * s[:, sl], axis=-1, keepdims=True) + EPS) for sl in sls]
                yh = [s[:, sl] * r for sl, r in zip(sls, rr)]
                pr = [jnp.sum(dy[:, sl] * y, axis=-1, keepdims=True) for sl, y in zip(sls, yh)]
                for sl, r, y, p in zip(sls, rr, yh, pr):
                    Q[r0:r0 + n, sl] = (scale * r) * (dy[:, sl] - y * p) * ds_c[:, sl]
            else:
                Q[r0:r0 + n, :] = dy * ds_c
        fold = lambda a: jnp.sum(a.reshape(strip // 8, 8, tc), axis=0)
        dw_acc = [jnp.zeros((8, tc), F32) for _ in range(CONV_K)]
        db_acc = jnp.zeros((8, tc), F32)
        for r0 in range(0, tt, strip):
            dpre = wj[0] * Q[pl.ds(3 + r0, strip), :]
            for j in range(1, CONV_K):
                dpre = dpre + wj[j] * Q[pl.ds(3 - j + r0, strip), :]
            dpre_ref[r0:r0 + strip, :] = dpre
            dyc = Q[r0:r0 + strip, :]
            for j in range(CONV_K):
                dw_acc[j] = dw_acc[j] + fold(dyc * P[pl.ds(HALO - 3 + j + r0, strip), :])
            db_acc = db_acc + fold(dyc)
        for j in range(CONV_K):
            dw_ref[j:j + 1, :] += jnp.sum(dw_acc[j], axis=0, keepdims=True)
        db_ref[...] += jnp.sum(db_acc, axis=0, keepdims=True)

    tile = pl.BlockSpec((tt, tc), lambda j, i: (i, j))
    prev = pl.BlockSpec((HALO, tc), lambda j, i: (jnp.maximum(i * (tt // HALO) - 1, 0), j))
    nxt = pl.BlockSpec((HALO, tc), lambda j, i: (jnp.minimum((i + 1) * (tt // HALO), T // HALO - 1), j))
    return pl.pallas_call(
        body, grid=(C // tc, nT),
        in_specs=[tile, prev, nxt, tile, nxt,
                  pl.BlockSpec((CONV_K, tc), lambda j, i: (0, j)), pl.BlockSpec((1, tc), lambda j, i: (0, j))],
        out_specs=[tile, pl.BlockSpec((CONV_K, tc), lambda j, i: (0, j)), pl.BlockSpec((1, tc), lambda j, i: (0, j))],
        out_shape=[jax.ShapeDtypeStruct((T, C), F32), jax.ShapeDtypeStruct((CONV_K, C), F32),
                   jax.ShapeDtypeStruct((1, C), F32)],
        scratch_shapes=[pltpu.VMEM((HALO + ext, tc), F32), pltpu.VMEM((ext, tc), F32)],
        compiler_params=_params("parallel", "arbitrary"), name=name,
    )(pre, pre, pre, dpost, dpost, w, b)


GDN_LOCKSTEP_CHUNKS = 16
GDN_SCAN_HEADS = 16


def _inv_unit_lower_many(nms, eye, n):
    xs = [jnp.where(eye, 1.0, 0.0) - nm for nm in nms]
    ps = list(nms)
    k = 2
    while k < n:
        ps = [_dot(p, p) for p in ps]
        xs = [x + _dot(x, p) for x, p in zip(xs, ps)]
        k *= 2
    return xs


def _gdn_prep(q, k, v, araw, braw, alog, dtb, name):
    T = q.shape[0]
    C = GDN_CHUNK
    tt = min(T, 1024)
    cpt, nC = tt // C, T // C
    grp = min(cpt, GDN_LOCKSTEP_CHUNKS)

    def body(alog_ref, dtb_ref, q_ref, k_ref, v_ref, a_ref, b_ref,
             u_ref, w_ref, pm_ref, ti_ref, g_ref, beta_ref, gc_ref):
        j = pl.program_id(0)
        tri, strict, eye, r_i, c_i = _masks(C)
        upper = jnp.where(r_i <= c_i, 1.0, 0.0)
        gcs, bts = [], []
        for hh in range(2):
            h = 2 * j + hh
            g = -jnp.exp(alog_ref[h]) * _softplus(a_ref[hh] + dtb_ref[h])
            bt = _sigmoid(b_ref[hh])
            gc = _dot_hi(g, upper)
            g_ref[hh], beta_ref[hh], gc_ref[hh] = g, bt, gc
            gcs.append(gc)
            bts.append(bt)
        for c0 in range(0, cpt, grp):
            cs = list(range(c0, c0 + grp))
            inst = [(c, hh) for c in cs for hh in range(2)]
            rows = {c: slice(c * C, (c + 1) * C) for c in cs}
            qc = {c: q_ref[rows[c], :] for c in cs}
            kc = {c: k_ref[rows[c], :] for c in cs}
            kk = {c: _dot_nt(kc[c], kc[c]) for c in cs}
            qk = {c: _dot_nt(qc[c], kc[c]) for c in cs}
            gcr = [gcs[hh][c:c + 1, :] for c, hh in inst]
            gcc = [_col(r, eye) for r in gcr]
            bc = [_col(bts[hh][c:c + 1, :], eye) for c, hh in inst]
            lm = [jnp.exp(jnp.where(tri, cc - r, -1e30)) for cc, r in zip(gcc, gcr)]
            nm = [jnp.where(strict, kk[c] * b * l, 0.0) for (c, hh), b, l in zip(inst, bc, lm)]
            tinv = _inv_unit_lower_many(nm, eye, C)
            rhs = [jnp.concatenate([v_ref[rows[c], hh * GDN_DK:(hh + 1) * GDN_DK] * b, kc[c] * (b * jnp.exp(cc))], axis=1)
                   for (c, hh), b, cc in zip(inst, bc, gcc)]
            sol = [_dot(t, r) for t, r in zip(tinv, rhs)]
            for (c, hh), s, t, l in zip(inst, sol, tinv, lm):
                hs = slice(hh * GDN_DK, (hh + 1) * GDN_DK)
                u_ref[rows[c], hs] = s[:, :GDN_DK]
                w_ref[rows[c], hs] = _mx(s[:, GDN_DK:])
                pm_ref[hh, c] = _mx(jnp.where(tri, qk[c] * l, 0.0))
                ti_ref[hh, c] = _mx(t)

    smem = pl.BlockSpec(memory_space=pltpu.SMEM)
    rows_spec = pl.BlockSpec((2, cpt, C), lambda j, i: (j, i, 0))
    qk_spec = pl.BlockSpec((tt, GDN_DK), lambda j, i: (i, j))
    v_spec = pl.BlockSpec((tt, 2 * GDN_DK), lambda j, i: (i, j))
    cc_spec = pl.BlockSpec((2, cpt, C, C), lambda j, i: (j, i, 0, 0))
    rows_shape = jax.ShapeDtypeStruct((GDN_HV, nC, C), F32)
    cc_shape = jax.ShapeDtypeStruct((GDN_HV, nC, C, C), MXU_DTYPE)
    return pl.pallas_call(
        body, grid=(GDN_HV // 2, T // tt),
        in_specs=[smem, smem, qk_spec, qk_spec, v_spec, rows_spec, rows_spec],
        out_specs=[v_spec, v_spec, cc_spec, cc_spec, rows_spec, rows_spec, rows_spec],
        out_shape=[jax.ShapeDtypeStruct((T, D_INNER), F32), jax.ShapeDtypeStruct((T, D_INNER), MXU_DTYPE),
                   cc_shape, cc_shape, rows_shape, rows_shape, rows_shape],
        compiler_params=_params("parallel", "parallel"), name=name,
    )(alog, dtb, q, k, v, araw, braw)


def _gdn_decays(gc_ref, h, c, eye, C):
    gcr = gc_ref[h, pl.ds(c, 1), :]
    gcc = _col(gcr, eye)
    glast = gcr[:, C - 1:C]
    return jnp.exp(gcc), jnp.exp(glast - gcc), jnp.exp(glast)


def _gdn_state_fwd(q, k, u, w, pm, gc, name):
    T = q.shape[0]
    C = GDN_CHUNK
    HG = GDN_SCAN_HEADS
    tt = min(T, 512)
    cpt, nC = tt // C, T // C

    def body(q_ref, k_ref, u_ref, w_ref, pm_ref, gc_ref, o_ref, vn_ref, sall_ref, S):
        @pl.when(pl.program_id(1) == 0)
        def _():
            S[...] = jnp.zeros_like(S)

        eye = _masks(C)[2]
        heads = list(range(HG))

        def chunk(c, carry):
            rows = pl.ds(pl.multiple_of(c * C, C), C)
            hs = [slice(h * GDN_DK, (h + 1) * GDN_DK) for h in heads]
            qs = [slice((h // 2) * GDN_DK, (h // 2 + 1) * GDN_DK) for h in heads]
            dec = [_gdn_decays(gc_ref, h, c, eye, C) for h in heads]
            sv = [S[h] for h in heads]
            for h in heads:
                sall_ref[h, c] = _mx(sv[h])
            ws = [_dot(w_ref[rows, hs[h]], sv[h]) for h in heads]
            qsv = [_dot(q_ref[rows, qs[h]], sv[h]) for h in heads]
            vn = [u_ref[rows, hs[h]] - ws[h] for h in heads]
            pv = [_dot(pm_ref[h, c], vn[h]) for h in heads]
            kv = [_dot_tn(k_ref[rows, qs[h]], vn[h] * dec[h][1]) for h in heads]
            for h in heads:
                vn_ref[rows, hs[h]] = _mx(vn[h])
                o_ref[rows, hs[h]] = qsv[h] * dec[h][0] + pv[h]
                S[h] = sv[h] * dec[h][2] + kv[h]
            return carry

        lax.fori_loop(0, cpt, chunk, 0)

    qk_spec = pl.BlockSpec((tt, HG // 2 * GDN_DK), lambda g, i: (i, g))
    v_spec = pl.BlockSpec((tt, HG * GDN_DK), lambda g, i: (i, g))
    return pl.pallas_call(
        body, grid=(GDN_HV // HG, T // tt),
        in_specs=[qk_spec, qk_spec, v_spec, v_spec,
                  pl.BlockSpec((HG, cpt, C, C), lambda g, i: (g, i, 0, 0)),
                  pl.BlockSpec((HG, cpt, C), lambda g, i: (g, i, 0))],
        out_specs=[v_spec, v_spec, pl.BlockSpec((HG, cpt, GDN_DK, GDN_DK), lambda g, i: (g, i, 0, 0))],
        out_shape=[jax.ShapeDtypeStruct((T, D_INNER), F32), jax.ShapeDtypeStruct((T, D_INNER), MXU_DTYPE),
                   jax.ShapeDtypeStruct((GDN_HV, nC, GDN_DK, GDN_DK), MXU_DTYPE)],
        scratch_shapes=[pltpu.VMEM((HG, GDN_DK, GDN_DK), F32)],
        compiler_params=_params("parallel", "arbitrary"), name=name,
    )(q, k, u, w, pm, gc)


def _gdn_state_bwd(q, k, w, pm, vn, sall, gc, do, name):
    T = q.shape[0]
    C = GDN_CHUNK
    HG = GDN_SCAN_HEADS
    tt = min(T, 512)
    cpt, nC, nT = tt // C, T // C, T // tt

    def body(q_ref, k_ref, w_ref, pm_ref, vn_ref, sall_ref, gc_ref, do_ref, dvn_ref, dkd_ref, dgl_ref, dS):
        @pl.when(pl.program_id(1) == 0)
        def _():
            dS[...] = jnp.zeros_like(dS)

        eye = _masks(C)[2]
        heads = list(range(HG))

        def chunk(ci, carry):
            c = cpt - 1 - ci
            rows = pl.ds(pl.multiple_of(c * C, C), C)
            hs = [slice(h * GDN_DK, (h + 1) * GDN_DK) for h in heads]
            qs = [slice((h // 2) * GDN_DK, (h // 2 + 1) * GDN_DK) for h in heads]
            dec = [_gdn_decays(gc_ref, h, c, eye, C) for h in heads]
            dsn = [dS[h] for h in heads]
            doc = [do_ref[rows, hs[h]] for h in heads]
            kds = [_dot(k_ref[rows, qs[h]], dsn[h]) for h in heads]
            pdo = [_dot_tn(pm_ref[h, c], doc[h]) for h in heads]
            dkd = [_dot_nt(vn_ref[rows, hs[h]], dsn[h]) for h in heads]
            qdo = [_dot_tn(q_ref[rows, qs[h]], doc[h] * dec[h][0]) for h in heads]
            dvn = [pdo[h] + kds[h] * dec[h][1] for h in heads]
            wdv = [_dot_tn(w_ref[rows, hs[h]], dvn[h]) for h in heads]
            for h in heads:
                dgl = jnp.sum(jnp.sum(dsn[h] * sall_ref[h, c].astype(F32), axis=0, keepdims=True), axis=1, keepdims=True)
                dgl_ref[h, pl.ds(c, 1), :] = jnp.broadcast_to(dgl, (1, C))
                dvn_ref[rows, hs[h]] = dvn[h]
                dkd_ref[rows, hs[h]] = dkd[h]
                dS[h] = dsn[h] * dec[h][2] + qdo[h] - wdv[h]
            return carry

        lax.fori_loop(0, cpt, chunk, 0)

    rev = lambda i: nT - 1 - i
    qk_spec = pl.BlockSpec((tt, HG // 2 * GDN_DK), lambda g, i: (rev(i), g))
    v_spec = pl.BlockSpec((tt, HG * GDN_DK), lambda g, i: (rev(i), g))
    rows_spec = pl.BlockSpec((HG, cpt, C), lambda g, i: (g, rev(i), 0))
    return pl.pallas_call(
        body, grid=(GDN_HV // HG, nT),
        in_specs=[qk_spec, qk_spec, v_spec, pl.BlockSpec((HG, cpt, C, C), lambda g, i: (g, rev(i), 0, 0)), v_spec,
                  pl.BlockSpec((HG, cpt, GDN_DK, GDN_DK), lambda g, i: (g, rev(i), 0, 0)), rows_spec, v_spec],
        out_specs=[v_spec, v_spec, rows_spec],
        out_shape=[jax.ShapeDtypeStruct((T, D_INNER), F32), jax.ShapeDtypeStruct((T, D_INNER), F32),
                   jax.ShapeDtypeStruct((GDN_HV, nC, C), F32)],
        scratch_shapes=[pltpu.VMEM((HG, GDN_DK, GDN_DK), F32)],
        compiler_params=_params("parallel", "arbitrary"), name=name,
    )(q, k, w, pm, vn, sall, gc, do)


def _gdn_local_bwd(q, k, v, gc, beta, tinv, u, w, pm, vn, sall, do, dvn, dkd, dgl, name):
    T = q.shape[0]
    C = GDN_CHUNK
    tt = min(T, 1024)
    cpt, nC = tt // C, T // C
    grp = min(cpt, GDN_LOCKSTEP_CHUNKS)

    def body(q_ref, k_ref, v_ref, gc_ref, b_ref, ti_ref, u_ref, w_ref, pm_ref, vn_ref, sall_ref, do_ref,
             dvn_ref, dkd_ref, dgl_ref, dq_ref, dk_ref, dv_ref, dg_ref, dbeta_ref, dgc_s):
        tri, strict, eye, r_i, c_i = _masks(C)
        lower = jnp.where(r_i >= c_i, 1.0, 0.0)
        lane = lax.broadcasted_iota(jnp.int32, (1, C), 1)
        rsum = lambda a: jnp.sum(a, axis=1, keepdims=True)
        for c0 in range(0, cpt, grp):
            cs = list(range(c0, c0 + grp))
            inst = [(c, hh) for c in cs for hh in range(2)]
            n = len(inst)
            rows = {c: slice(c * C, (c + 1) * C) for c in cs}
            hsl = [slice(hh * GDN_DK, (hh + 1) * GDN_DK) for c, hh in inst]
            qc = {c: q_ref[rows[c], :] for c in cs}
            kc = {c: k_ref[rows[c], :] for c in cs}
            kk = {c: _dot_nt(kc[c], kc[c]) for c in cs}
            gcr = [gc_ref[hh, c:c + 1, :] for c, hh in inst]
            gcc = [_col(r, eye) for r in gcr]
            bc = [_col(b_ref[hh, c:c + 1, :], eye) for c, hh in inst]
            lm = [jnp.exp(jnp.where(tri, cc - r, -1e30)) for cc, r in zip(gcc, gcr)]
            e_c = [jnp.exp(cc) for cc in gcc]
            el_c = [jnp.exp(r[:, C - 1:C] - cc) for cc, r in zip(gcc, gcr)]
            gl = [jnp.exp(r[:, C - 1:C]) for r in gcr]
            doc = [do_ref[rows[c], hsl[i]] for i, (c, hh) in enumerate(inst)]
            dvn = [dvn_ref[rows[c], hsl[i]] for i, (c, hh) in enumerate(inst)]
            sv = [sall_ref[hh, c] for c, hh in inst]
            aa = [_dot_nt(jnp.concatenate([_mx(doc[i]), _mx(dvn[i])], axis=0), sv[i]) for i in range(n)]
            dpm = [jnp.where(tri, _dot_nt(doc[i], vn_ref[rows[c], hsl[i]]), 0.0) for i, (c, hh) in enumerate(inst)]
            dqd = [a[:C] for a in aa]
            drhs = [_dot_tn(ti_ref[hh, c], jnp.concatenate([dvn[i], -aa[i][C:]], axis=1))
                    for i, (c, hh) in enumerate(inst)]
            sol = [jnp.concatenate([_mx(u_ref[rows[c], hsl[i]]), w_ref[rows[c], hsl[i]]], axis=1)
                   for i, (c, hh) in enumerate(inst)]
            dnm = [-jnp.where(strict, _dot_nt(drhs[i], sol[i]), 0.0) for i in range(n)]
            dkk = [dnm[i] * bc[i] * lm[i] for i in range(n)]
            dqk = [dpm[i] * lm[i] for i in range(n)]
            dq1 = [_dot(dqk[i], kc[c]) for i, (c, hh) in enumerate(inst)]
            dk1 = [_dot(dkk[i], kc[c]) for i, (c, hh) in enumerate(inst)]
            dk2 = [_dot_tn(dkk[i], kc[c]) for i, (c, hh) in enumerate(inst)]
            dk3 = [_dot_tn(dqk[i], qc[c]) for i, (c, hh) in enumerate(inst)]
            dq_acc = {c: jnp.zeros((C, GDN_DK), F32) for c in cs}
            dk_acc = {c: jnp.zeros((C, GDN_DK), F32) for c in cs}
            for i, (c, hh) in enumerate(inst):
                k_, q_, v_ = kc[c], qc[c], v_ref[rows[c], hsl[i]]
                dvb, dkbe = drhs[i][:, :GDN_DK], drhs[i][:, GDN_DK:]
                dkd = dkd_ref[rows[c], hsl[i]]
                kb = k_ * bc[i]
                dkb = dkbe * e_c[i]
                de_c = rsum(dkbe * kb) + rsum(dqd[i] * q_)
                del_c = rsum(dkd * k_)
                dbc = rsum(dnm[i] * kk[c] * lm[i]) + rsum(dkb * k_) + rsum(dvb * v_)
                dq_acc[c] = dq_acc[c] + dq1[i] + dqd[i] * e_c[i]
                dk_acc[c] = dk_acc[c] + dk1[i] + dk2[i] + dk3[i] + dkd * el_c[i] + dkb * bc[i]
                dv_ref[rows[c], hsl[i]] = dvb * bc[i]
                nm = jnp.where(strict, kk[c] * bc[i] * lm[i], 0.0)
                gm = dnm[i] * nm + dpm[i] * pm_ref[hh, c].astype(F32)
                dgc_col = rsum(gm) + de_c * e_c[i] - del_c * el_c[i]
                dglast = jnp.sum(del_c * el_c[i], axis=0, keepdims=True) + dgl_ref[hh, c:c + 1, 0:1] * gl[i]
                dgc_s[hh, c:c + 1, :] = (_row(dgc_col, eye) - jnp.sum(gm, axis=0, keepdims=True)
                                         + jnp.where(lane == C - 1, dglast, 0.0))
                dbeta_ref[hh, c:c + 1, :] = _row(dbc, eye)
            for c in cs:
                dq_ref[rows[c], :] = dq_acc[c]
                dk_ref[rows[c], :] = dk_acc[c]
        for hh in range(2):
            dg_ref[hh] = _dot_hi(dgc_s[hh], lower)

    rows_spec = pl.BlockSpec((2, cpt, C), lambda j, i: (j, i, 0))
    qk_spec = pl.BlockSpec((tt, GDN_DK), lambda j, i: (i, j))
    v_spec = pl.BlockSpec((tt, 2 * GDN_DK), lambda j, i: (i, j))
    cc_spec = pl.BlockSpec((2, cpt, C, C), lambda j, i: (j, i, 0, 0))
    rows_shape = jax.ShapeDtypeStruct((GDN_HV, nC, C), F32)
    return pl.pallas_call(
        body, grid=(GDN_HV // 2, T // tt),
        in_specs=[qk_spec, qk_spec, v_spec, rows_spec, rows_spec, cc_spec, v_spec, v_spec, cc_spec, v_spec,
                  pl.BlockSpec((2, cpt, GDN_DK, GDN_DK), lambda j, i: (j, i, 0, 0)), v_spec, v_spec, v_spec, rows_spec],
        out_specs=[qk_spec, qk_spec, v_spec, rows_spec, rows_spec],
        out_shape=[jax.ShapeDtypeStruct((T, GDN_HV // 2 * GDN_DK), F32),
                   jax.ShapeDtypeStruct((T, GDN_HV // 2 * GDN_DK), F32),
                   jax.ShapeDtypeStruct((T, D_INNER), F32), rows_shape, rows_shape],
        scratch_shapes=[pltpu.VMEM((2, cpt, C), F32)],
        compiler_params=_params("parallel", "parallel"), name=name,
    )(q, k, v, gc, beta, tinv, u, w, pm, vn, sall, do, dvn, dkd, dgl)


def _gdn_gate_bwd(araw, braw, dg, dbeta, alog, dtb, name):
    H, T = araw.shape

    def body(a_ref, b_ref, dg_ref, dbt_ref, alog_ref, dtb_ref, da_ref, db_ref, dalog_ref, ddtb_ref):
        xa = a_ref[...] + dtb_ref[...]
        ea = jnp.exp(alog_ref[...])
        dgv = dg_ref[...]
        da = -dgv * ea * _sigmoid(xa)
        da_ref[...] = da
        dalog_ref[...] = jnp.sum(-dgv * ea * _softplus(xa), axis=1, keepdims=True)
        ddtb_ref[...] = jnp.sum(da, axis=1, keepdims=True)
        bt = _sigmoid(b_ref[...])
        db_ref[...] = dbt_ref[...] * bt * (1.0 - bt)

    return pl.pallas_call(
        body,
        out_shape=[jax.ShapeDtypeStruct((H, T), F32), jax.ShapeDtypeStruct((H, T), F32),
                   jax.ShapeDtypeStruct((H, 1), F32), jax.ShapeDtypeStruct((H, 1), F32)],
        compiler_params=pltpu.CompilerParams(vmem_limit_bytes=VMEM_LIMIT_BYTES), name=name,
    )(araw, braw, dg, dbeta, alog, dtb)


SSD_LOCKSTEP_CHUNKS = 2
SSD_LOCKSTEP_CHUNKS_BWD = 1


def _ssd_scan_fwd(xs, bm, cm, dtraw, alog, dtb, dskip, name):
    T = xs.shape[0]
    Q = SSD_CHUNK
    tt = min(T, 1024)
    cpt, nC = tt // Q, T // Q
    GW = SSD_R * SSD_P

    def body(alog_ref, dtb_ref, dsk_ref, xs_ref, b_ref, c_ref, dt_ref, y_ref, sall_ref, dto_ref, S, dt_s, acs_s):
        gi, i = pl.program_id(0), pl.program_id(1)

        @pl.when(i == 0)
        def _():
            S[...] = jnp.zeros_like(S)

        tri, _, eye, r_i, c_i = _masks(Q)
        upper = jnp.where(r_i <= c_i, 1.0, 0.0)
        for r in range(SSD_R):
            h = SSD_R * gi + r
            dt = _softplus(dt_ref[r] + dtb_ref[h])
            dto_ref[r] = dt
            dt_s[r] = dt
            acs_s[r] = _dot_hi(-jnp.exp(alog_ref[h]) * dt, upper)

        ps = [slice(r * SSD_P, (r + 1) * SSD_P) for r in range(SSD_R)]
        s_cur = [S[:, ps[r]] for r in range(SSD_R)]
        grp = min(cpt, SSD_LOCKSTEP_CHUNKS)
        for c0 in range(0, cpt, grp):
            cs = list(range(c0, c0 + grp))
            inst = [(c, r) for c in cs for r in range(SSD_R)]
            rows = {c: slice(c * Q, (c + 1) * Q) for c in cs}
            bc_ = {c: b_ref[rows[c], :] for c in cs}
            cc_ = {c: c_ref[rows[c], :] for c in cs}
            cb = {c: _dot_nt(cc_[c], bc_[c]) for c in cs}
            xr = [xs_ref[rows[c], ps[r]] for c, r in inst]
            acr = [acs_s[r, c:c + 1, :] for c, r in inst]
            acc = [_col_bcast(a, Q) for a in acr]
            dtc = [_col_bcast(dt_s[r, c:c + 1, :], Q)[:, :SSD_P] for c, r in inst]
            xd = [x * d for x, d in zip(xr, dtc)]
            mm = [cb[c] * jnp.exp(jnp.where(tri, acc[i] - acr[i], -1e30)) for i, (c, r) in enumerate(inst)]
            bct = {c: bc_[c].T for c in cs}
            st = [_dot(bct[c] * jnp.exp(acr[i][:, Q - 1:Q] - acr[i]), xd[i]) for i, (c, r) in enumerate(inst)]
            yd = [_dot(mm[i], xd[i]) for i in range(len(inst))]
            s_prev = []
            for i, (c, r) in enumerate(inst):
                s_prev.append(s_cur[r])
                s_cur[r] = s_cur[r] * jnp.exp(acr[i][:, Q - 1:Q]) + st[i]
            yo = [_dot(cc_[c] * jnp.exp(acc[i]), s_prev[i]) for i, (c, r) in enumerate(inst)]
            for i, (c, r) in enumerate(inst):
                sall_ref[0, c, :, ps[r]] = s_prev[i]
                y_ref[rows[c], ps[r]] = yd[i] + yo[i] + dsk_ref[SSD_R * gi + r] * xr[i]
        for r in range(SSD_R):
            S[:, ps[r]] = s_cur[r]

    smem = pl.BlockSpec(memory_space=pltpu.SMEM)
    rows_spec = pl.BlockSpec((SSD_R, cpt, Q), lambda g, i: (g, i, 0))
    return pl.pallas_call(
        body, grid=(SSD_G, T // tt),
        in_specs=[smem, smem, smem,
                  pl.BlockSpec((tt, GW), lambda g, i: (i, g)), pl.BlockSpec((tt, SSD_N), lambda g, i: (i, g)),
                  pl.BlockSpec((tt, SSD_N), lambda g, i: (i, g)), rows_spec],
        out_specs=[pl.BlockSpec((tt, GW), lambda g, i: (i, g)),
                   pl.BlockSpec((1, cpt, SSD_N, GW), lambda g, i: (g, i, 0, 0)), rows_spec],
        out_shape=[jax.ShapeDtypeStruct((T, D_INNER), F32), jax.ShapeDtypeStruct((SSD_G, nC, SSD_N, GW), F32),
                   jax.ShapeDtypeStruct((SSD_H, nC, Q), F32)],
        scratch_shapes=[pltpu.VMEM((SSD_N, GW), F32), pltpu.VMEM((SSD_R, cpt, Q), F32),
                        pltpu.VMEM((SSD_R, cpt, Q), F32)],
        compiler_params=_params("parallel", "arbitrary"), name=name,
    )(alog, dtb, dskip, xs, bm, cm, dtraw)


def _ssd_scan_bwd(xs, bm, cm, dt, sall, dy, alog, dskip, name):
    T = xs.shape[0]
    Q = SSD_CHUNK
    tt = min(T, 1024)
    cpt, nC, nT = tt // Q, T // Q, T // tt
    GW = SSD_R * SSD_P

    def body(alog_ref, dsk_ref, xs_ref, b_ref, c_ref, dt_ref, sall_ref, dy_ref,
             dxs_ref, db_ref, dc_ref, da_ref, ddt_ref, dd_ref, dS, acs_s, dacs_s, ddt_s, dd_s):
        gi, i = pl.program_id(0), pl.program_id(1)

        @pl.when(i == 0)
        def _():
            dS[...] = jnp.zeros_like(dS)

        tri, _, eye, r_i, c_i = _masks(Q)
        upper = jnp.where(r_i <= c_i, 1.0, 0.0)
        lower = jnp.where(r_i >= c_i, 1.0, 0.0)
        lane = lax.broadcasted_iota(jnp.int32, (1, Q), 1)
        for r in range(SSD_R):
            acs_s[r] = _dot_hi(-jnp.exp(alog_ref[SSD_R * gi + r]) * dt_ref[r], upper)

        ps = [slice(r * SSD_P, (r + 1) * SSD_P) for r in range(SSD_R)]
        ds_cur = [dS[:, ps[r]] for r in range(SSD_R)]
        grp = min(cpt, SSD_LOCKSTEP_CHUNKS_BWD)
        csum = lambda a: jnp.sum(a, axis=0, keepdims=True)
        tsum = lambda a: jnp.sum(csum(a), axis=1, keepdims=True)
        ones8 = jnp.ones((8, SSD_P), F32)
        for c0 in range(cpt - grp, -1, -grp):
            cs = list(range(c0 + grp - 1, c0 - 1, -1))
            inst = [(c, r) for c in cs for r in range(SSD_R)]
            n = len(inst)
            rows = {c: slice(c * Q, (c + 1) * Q) for c in cs}
            bc_ = {c: b_ref[rows[c], :] for c in cs}
            cc_ = {c: c_ref[rows[c], :] for c in cs}
            cb = {c: _dot_nt(cc_[c], bc_[c]) for c in cs}
            xr = [xs_ref[rows[c], ps[r]] for c, r in inst]
            dyr = [dy_ref[rows[c], ps[r]] for c, r in inst]
            acr = [acs_s[r, c:c + 1, :] for c, r in inst]
            dtr = [dt_ref[r, c:c + 1, :] for c, r in inst]
            acc = [_col_bcast(a, Q) for a in acr]
            dtb = [_col_bcast(d, Q) for d in dtr]
            al = [a[:, Q - 1:Q] for a in acr]
            e_c = [jnp.exp(a) for a in acc]
            dl_c = [jnp.exp(al[i] - acc[i]) for i in range(n)]
            e_r = [jnp.exp(a) for a in acr]
            dl_r = [jnp.exp(al[i] - acr[i]) for i in range(n)]
            gl = [jnp.exp(a) for a in al]
            bct = {c: bc_[c].T for c in cs}
            cct = {c: cc_[c].T for c in cs}
            cbt = {c: _dot_nt(bc_[c], cc_[c]) for c in cs}
            lm = [jnp.exp(jnp.where(tri, acc[i] - acr[i], -1e30)) for i in range(n)]
            lmt = [jnp.exp(jnp.where(r_i <= c_i, acr[i] - acc[i], -1e30)) for i in range(n)]
            mm = [cb[c] * lm[i] for i, (c, r) in enumerate(inst)]
            mmt = [cbt[c] * lmt[i] for i, (c, r) in enumerate(inst)]
            sr = [sall_ref[0, c, :, ps[r]] for c, r in inst]
            dmm0 = [jnp.where(tri, _dot_nt(dyr[i], xr[i]), 0.0) for i in range(n)]
            dmm0t = [jnp.where(r_i <= c_i, _dot_nt(xr[i], dyr[i]), 0.0) for i in range(n)]
            dxd1 = [_dot(mmt[i], dyr[i]) for i in range(n)]
            dce = [_dot_nt(dyr[i], sr[i]) for i in range(n)]
            dcet = [_dot_nt(sr[i], dyr[i]) for i in range(n)]
            cdy = [_dot(cct[c] * e_r[i], dyr[i]) for i, (c, r) in enumerate(inst)]
            dsn = []
            for i, (c, r) in enumerate(inst):
                dsn.append(ds_cur[r])
                ds_cur[r] = gl[i] * ds_cur[r] + cdy[i]
            dxd = [dxd1[i] + _dot(bc_[c] * dl_c[i], dsn[i]) for i, (c, r) in enumerate(inst)]
            dbd0 = [_dot_nt(xr[i], dsn[i]) for i in range(n)]
            dbd0t = [_dot_nt(dsn[i], xr[i]) for i in range(n)]
            dcb = {c: jnp.zeros((Q, Q), F32) for c in cs}
            dcbt = {c: jnp.zeros((Q, Q), F32) for c in cs}
            db_acc = {c: jnp.zeros((Q, SSD_N), F32) for c in cs}
            dc_acc = {c: jnp.zeros((Q, SSD_N), F32) for c in cs}
            for i, (c, r) in enumerate(inst):
                dgl = tsum(dsn[i] * sr[i])
                dc_acc[c] = dc_acc[c] + dce[i] * e_c[i]
                db_acc[c] = db_acc[c] + dbd0[i] * (dtb[i] * dl_c[i])
                dcb[c] = dcb[c] + dmm0[i] * (lm[i] * dtr[i])
                dcbt[c] = dcbt[c] + dmm0t[i] * (lmt[i] * dtb[i])
                csum_gm0 = csum(dmm0[i] * mm[i])
                rsum_gm = csum(dmm0t[i] * mmt[i] * dtb[i])
                r_de = csum(dcet[i] * cct[c]) * e_r[i]
                r_dl = csum(dbd0t[i] * bct[c]) * dl_r[i]
                dalast = jnp.sum(r_dl * dtr[i], axis=1, keepdims=True) + dgl * gl[i]
                dacs_s[r, c:c + 1, :] = (rsum_gm + r_de - (r_dl + csum_gm0) * dtr[i]
                                         + jnp.where(lane == Q - 1, dalast, 0.0))
                ddt_s[r, c:c + 1, :] = csum_gm0 + r_dl
                dd_s[r, c:c + 1, :] = _dot_nt(ones8, dyr[i] * xr[i])[0:1]
                dxs_ref[rows[c], ps[r]] = dxd[i] * dtb[i][:, :SSD_P] + dsk_ref[SSD_R * gi + r] * dyr[i]
            for c in cs:
                dc_ref[rows[c], :] = dc_acc[c] + _dot(dcb[c], bc_[c])
                db_ref[rows[c], :] = db_acc[c] + _dot(dcbt[c], cc_[c])
        for r in range(SSD_R):
            dS[:, ps[r]] = ds_cur[r]
        for r in range(SSD_R):
            da_ref[r] = _dot_hi(dacs_s[r], lower)
            ddt_ref[r] = ddt_s[r]
            dd_ref[r] = dd_s[r]

    rev = lambda i: nT - 1 - i
    smem = pl.BlockSpec(memory_space=pltpu.SMEM)
    rows_spec = pl.BlockSpec((SSD_R, cpt, Q), lambda g, i: (g, rev(i), 0))
    x_spec = pl.BlockSpec((tt, GW), lambda g, i: (rev(i), g))
    n_spec = pl.BlockSpec((tt, SSD_N), lambda g, i: (rev(i), g))
    rows_shape = jax.ShapeDtypeStruct((SSD_H, nC, Q), F32)
    return pl.pallas_call(
        body, grid=(SSD_G, nT),
        in_specs=[smem, smem, x_spec, n_spec, n_spec, rows_spec,
                  pl.BlockSpec((1, cpt, SSD_N, GW), lambda g, i: (g, rev(i), 0, 0)), x_spec],
        out_specs=[x_spec, n_spec, n_spec, rows_spec, rows_spec, rows_spec],
        out_shape=[jax.ShapeDtypeStruct((T, D_INNER), F32), jax.ShapeDtypeStruct((T, SSD_G * SSD_N), F32),
                   jax.ShapeDtypeStruct((T, SSD_G * SSD_N), F32), rows_shape, rows_shape, rows_shape],
        scratch_shapes=[pltpu.VMEM((SSD_N, GW), F32)] + [pltpu.VMEM((SSD_R, cpt, Q), F32)] * 4,
        compiler_params=_params("parallel", "arbitrary"), name=name,
    )(alog, dskip, xs, bm, cm, dt, sall, dy)


def _ssd_gate_bwd(dtraw, dt, da, ddt_direct, ddrow, alog, dtb, name):
    H, T = dtraw.shape

    def body(raw_ref, dt_ref, da_ref, ddt_ref, dd_ref, alog_ref, dtb_ref, draw_ref, dalog_ref, ddtb_ref, dD_ref):
        a = -jnp.exp(alog_ref[...])
        dav = da_ref[...]
        ddt = ddt_ref[...] + dav * a
        draw = ddt * _sigmoid(raw_ref[...] + dtb_ref[...])
        draw_ref[...] = draw
        dalog_ref[...] = jnp.sum(dav * dt_ref[...], axis=1, keepdims=True) * a
        ddtb_ref[...] = jnp.sum(draw, axis=1, keepdims=True)
        dD_ref[...] = jnp.sum(dd_ref[...], axis=1, keepdims=True)

    return pl.pallas_call(
        body,
        out_shape=[jax.ShapeDtypeStruct((H, T), F32)] + [jax.ShapeDtypeStruct((H, 1), F32)] * 3,
        compiler_params=pltpu.CompilerParams(vmem_limit_bytes=VMEM_LIMIT_BYTES), name=name,
    )(dtraw, dt, da, ddt_direct, ddrow, alog, dtb)


def _final_loss(x, fw, tgt, name):
    T = x.shape[0]
    tt = min(T, 512)
    nT = T // tt

    def body(x_ref, w_ref, t_ref, dx_ref, dw_ref, loss_ref, acc):
        i = pl.program_id(0)

        @pl.when(i == 0)
        def _():
            dw_ref[...] = jnp.zeros_like(dw_ref)
            acc[...] = jnp.zeros_like(acc)

        xv = x_ref[...]
        r = lax.rsqrt(jnp.mean(xv * xv, axis=-1, keepdims=True) + EPS)
        xh = xv * r
        err = xh * w_ref[...] - t_ref[...]
        acc[...] += jnp.sum(err * err, axis=0, keepdims=True)
        dout = err * (1.0 / D_MODEL)
        dw_ref[...] += jnp.sum(dout * xh, axis=0, keepdims=True)
        dxn = dout * w_ref[...]
        dx_ref[...] = r * (dxn - xh * jnp.mean(dxn * xh, axis=-1, keepdims=True))

        @pl.when(i == nT - 1)
        def _():
            loss_ref[...] = (0.5 / D_MODEL) * jnp.sum(acc[...], axis=1, keepdims=True)

    row = pl.BlockSpec((tt, D_MODEL), lambda i: (i, 0))
    vec = pl.BlockSpec((1, D_MODEL), lambda i: (0, 0))
    return pl.pallas_call(
        body, grid=(nT,),
        in_specs=[row, vec, row],
        out_specs=[row, vec, pl.BlockSpec((1, 1), lambda i: (0, 0))],
        out_shape=[jax.ShapeDtypeStruct((T, D_MODEL), F32), jax.ShapeDtypeStruct((1, D_MODEL), F32),
                   jax.ShapeDtypeStruct((1, 1), F32)],
        scratch_shapes=[pltpu.VMEM((1, D_MODEL), F32)],
        compiler_params=_params("arbitrary"), name=name,
    )(x, fw, tgt)


def _adamw(parts, w, m, v, name):
    R, C = w.shape
    tr = 128 if R % 128 == 0 else R

    def body(p_ref, w_ref, m_ref, v_ref, g_ref, d_ref, nm_ref, nv_ref):
        g = p_ref[0].astype(F32)
        for s in range(1, N_DEV):
            g = g + p_ref[s].astype(F32)
        mn = ADAM_B1 * m_ref[...] + (1.0 - ADAM_B1) * g
        vn = ADAM_B2 * v_ref[...] + (1.0 - ADAM_B2) * (g * g)
        mh = mn / (1.0 - ADAM_B1 ** ADAM_STEP)
        vh = vn / (1.0 - ADAM_B2 ** ADAM_STEP)
        g_ref[...] = g
        d_ref[...] = -ADAM_LR * (mh / (jnp.sqrt(vh) + ADAM_EPS) + ADAM_WD * w_ref[...])
        nm_ref[...] = mn
        nv_ref[...] = vn

    blk = pl.BlockSpec((tr, C), lambda i: (i, 0))
    return pl.pallas_call(
        body, grid=(R // tr,),
        in_specs=[pl.BlockSpec((N_DEV, tr, C), lambda i: (0, i, 0)), blk, blk, blk],
        out_specs=[blk] * 4,
        out_shape=[jax.ShapeDtypeStruct((R, C), F32)] * 4,
        compiler_params=_params("parallel"), name=name,
    )(parts, w, m, v)


def _me():
    x, y, c = lax.axis_index("x"), lax.axis_index("y"), lax.axis_index("c")
    return x, y, c


def _peer(d):
    x, y, c = _me()
    px = 1 - x if d & 4 else x
    py = 1 - y if d & 2 else y
    pc = 1 - c if d & 1 else c
    return (px, py, pc), 4 * px + 2 * py + pc


def _exchange(arrs, bcast, name):
    n = len(arrs)

    def body(*refs):
        ex = _Exchange(refs[:n], refs[n:2 * n], bcast, *refs[2 * n:])
        ex.begin()
        ex.finish()

    anyspec = pl.BlockSpec(memory_space=pl.ANY)
    return pl.pallas_call(
        body,
        in_specs=[anyspec] * n, out_specs=[anyspec] * n,
        out_shape=_exchange_out_shapes(arrs, bcast),
        scratch_shapes=_exchange_semaphores(n),
        name=name,
    )(*arrs)


def _exchange_out_shapes(arrs, bcast):
    return [jax.ShapeDtypeStruct((N_DEV,) + (a.shape if b else a.shape[1:]), a.dtype) for a, b in zip(arrs, bcast)]


def _exchange_semaphores(n):
    return [pltpu.SemaphoreType.DMA((n, N_DEV - 1)), pltpu.SemaphoreType.DMA((n, N_DEV - 1)),
            pltpu.SemaphoreType.DMA((n,))]


class _Exchange:
    def __init__(self, ins, outs, bcast, ssem, rsem, lsem):
        n = len(ins)
        x, y, c = _me()
        me = 4 * x + 2 * y + c

        def src(a, dest):
            return ins[a] if bcast[a] else ins[a].at[dest]

        self.local = [pltpu.make_async_copy(src(a, me), outs[a].at[me], lsem.at[a]) for a in range(n)]
        self.sends, self.recvs = [], []
        for a in range(n):
            for d in range(1, N_DEV):
                peer, pid = _peer(d)
                self.sends.append(pltpu.make_async_remote_copy(
                    src_ref=src(a, pid), dst_ref=outs[a].at[me], send_sem=ssem.at[a, d - 1],
                    recv_sem=rsem.at[a, d - 1], device_id=peer, device_id_type=MESH))
                self.recvs.append(pltpu.make_async_remote_copy(
                    src_ref=src(a, pid), dst_ref=outs[a].at[pid], send_sem=ssem.at[a, d - 1],
                    recv_sem=rsem.at[a, d - 1], device_id=peer, device_id_type=MESH))

    def begin(self):
        for cp in self.local + self.sends:
            cp.start()

    def finish(self):
        for cp in self.recvs:
            cp.wait_recv()
        for cp in self.sends:
            cp.wait_send()
        for cp in self.local:
            cp.wait()


def _to_rows(cols, chunk):
    T, H = cols.shape
    return cols.T.reshape(H, T // chunk, chunk)


def _from_rows(rows):
    return rows.T


def _pad_cols(a, width):
    return jnp.pad(a, ((0, 0), (0, width - a.shape[1])))


def _local_step(x, tgt, p, late_weights=None, early_grads=None):
    T = x.shape[0]
    zb = lambda n: jnp.zeros((1, n), F32)
    gw = p["gdn_w_in"]
    g_wparts = [gw[:, 0:1024], gw[:, 1024:2048], gw[:, 2048:4096], gw[:, 4096:6144], _pad_cols(gw[:, 6144:6176], PAD_W)]
    nw0, nw1 = p["norm_w"][0:1], p["norm_w"][1:2]
    if late_weights is None:
        h0, (q_pre, k_pre, v_pre, z0, ab) = _norm_inproj(x, nw0, g_wparts, "gdn_inproj")
    else:
        comm, assemble = late_weights
        h0, (q_pre, k_pre, v_pre, z0, ab), gathered = _norm_inproj(x, nw0, g_wparts, "gdn_inproj", comm)
        p = dict(p, **assemble(gathered))
    gcw = p["gdn_conv_w"]
    cw_q, cw_k, cw_v = gcw[:, 0:1024], gcw[:, 1024:2048], gcw[:, 2048:4096]
    q = _conv_fwd(q_pre, cw_q, zb(1024), True, GDN_DK ** -0.5, "gdn_conv_q")
    k = _conv_fwd(k_pre, cw_k, zb(1024), True, 1.0, "gdn_conv_k")
    v = _conv_fwd(v_pre, cw_v, zb(2048), False, 1.0, "gdn_conv_v")
    braw = _to_rows(ab[:, 0:GDN_HV], GDN_CHUNK)
    araw = _to_rows(ab[:, GDN_HV:2 * GDN_HV], GDN_CHUNK)
    g_alog, g_dtb = p["gdn_a_log"].reshape(-1), p["gdn_dt_bias"].reshape(-1)
    g_u, g_w, g_pm, g_ti, g_rows, beta_rows, gc_rows = _gdn_prep(q, k, v, araw, braw, g_alog, g_dtb, "gdn_prep")
    o0, g_vn, g_sall = _gdn_state_fwd(q, k, g_u, g_w, g_pm, gc_rows, "gdn_state_fwd")
    x1 = _out_fwd(o0, z0, p["gdn_norm_w"], p["gdn_w_out"], x, GDN_DK, False, "gdn_out")
    sw = p["ssd_w_in"]
    s_wparts = [sw[:, 0:2048], sw[:, 2048:4096], sw[:, 4096:5120], sw[:, 5120:6144], _pad_cols(sw[:, 6144:6176], PAD_W)]
    h1, (z1, xs_pre, b_pre, c_pre, dtp) = _norm_inproj(x1, nw1, s_wparts, "ssd_inproj")
    scw, scb = p["ssd_conv_w"], p["ssd_conv_b"]
    xs = _conv_fwd(xs_pre, scw[:, 0:2048], scb[:, 0:2048], False, 1.0, "ssd_conv_x")
    bm = _conv_fwd(b_pre, scw[:, 2048:3072], scb[:, 2048:3072], False, 1.0, "ssd_conv_b")
    cm = _conv_fwd(c_pre, scw[:, 3072:4096], scb[:, 3072:4096], False, 1.0, "ssd_conv_c")
    dtraw = _to_rows(dtp[:, 0:SSD_H], SSD_CHUNK)
    s_alog, s_dtb, s_d = p["ssd_a_log"].reshape(-1), p["ssd_dt_bias"].reshape(-1), p["ssd_d"].reshape(-1)
    y1, s_sall, dt_rows = _ssd_scan_fwd(xs, bm, cm, dtraw, s_alog, s_dtb, s_d, "ssd_scan_fwd")
    x2 = _out_fwd(y1, z1, p["ssd_norm_w"], p["ssd_w_out"], x1, D_INNER // SSD_G, True, "ssd_out")
    dx2, d_fw, loss = _final_loss(x2, p["final_norm_w"].reshape(1, -1), tgt, "final_loss")
    dy1, dz1, d_snw, yn1 = _out_bwd(dx2, y1, z1, p["ssd_norm_w"], p["ssd_w_out"], D_INNER // SSD_G, True, "ssd_out_bwd")
    d_swout = _matmul_tn(yn1, dx2, "ssd_wout_grad")
    dxs, dbm, dcm, da_rows, ddt_rows, dd_rows = _ssd_scan_bwd(xs, bm, cm, dt_rows, s_sall, dy1, s_alog, s_d, "ssd_scan_bwd")
    col = lambda a: a.reshape(-1, 1)
    dtraw_g, d_salog, d_sdtb, d_sd = _ssd_gate_bwd(
        dtraw.reshape(SSD_H, T), dt_rows.reshape(SSD_H, T), da_rows.reshape(SSD_H, T),
        ddt_rows.reshape(SSD_H, T), dd_rows.reshape(SSD_H, T), col(s_alog), col(s_dtb), "ssd_gate_bwd")
    dxs_pre, dcw_x, dcb_x = _conv_bwd(xs_pre, scw[:, 0:2048], scb[:, 0:2048], dxs, False, 1.0, "ssd_conv_x_bwd")
    db_pre, dcw_b, dcb_b = _conv_bwd(b_pre, scw[:, 2048:3072], scb[:, 2048:3072], dbm, False, 1.0, "ssd_conv_b_bwd")
    dc_pre, dcw_c, dcb_c = _conv_bwd(c_pre, scw[:, 3072:4096], scb[:, 3072:4096], dcm, False, 1.0, "ssd_conv_c_bwd")
    ddtp = _pad_cols(_from_rows(dtraw_g), PAD_W)
    s_dparts = [dz1, dxs_pre, db_pre, dc_pre, ddtp]
    dx1, d_nw1 = _inproj_bwd(x1, nw1, s_dparts, s_wparts, dx2, "ssd_inproj_bwd")
    s_dw = [_matmul_tn(h1, d, "ssd_win_grad_%d" % n) for n, d in enumerate(s_dparts)]
    d_swin = jnp.concatenate(s_dw[:4] + [s_dw[4][:, 0:SSD_H]], axis=1)
    early_recv = None
    if early_grads is None:
        do0, dz0, d_gnw, yn0 = _out_bwd(dx1, o0, z0, p["gdn_norm_w"], p["gdn_w_out"], GDN_DK, False, "gdn_out_bwd")
    else:
        do0, dz0, d_gnw, yn0, early_recv = _out_bwd(dx1, o0, z0, p["gdn_norm_w"], p["gdn_w_out"], GDN_DK, False,
                                                    "gdn_out_bwd", early_grads(d_swin, d_swout))
    d_gwout = _matmul_tn(yn0, dx1, "gdn_wout_grad")
    g_dvn, g_dkd, g_dgl = _gdn_state_bwd(q, k, g_w, g_pm, g_vn, g_sall, gc_rows, do0, "gdn_state_bwd")
    dq, dk, dv, dg_rows, dbeta_rows = _gdn_local_bwd(q, k, v, gc_rows, beta_rows, g_ti, g_u, g_w, g_pm, g_vn, g_sall,
                                                     do0, g_dvn, g_dkd, g_dgl, "gdn_local_bwd")
    da_g, db_g, d_galog, d_gdtb = _gdn_gate_bwd(
        araw.reshape(GDN_HV, T), braw.reshape(GDN_HV, T), dg_rows.reshape(GDN_HV, T),
        dbeta_rows.reshape(GDN_HV, T), col(g_alog), col(g_dtb), "gdn_gate_bwd")
    dq_pre, dcw_q, _ = _conv_bwd(q_pre, cw_q, zb(1024), dq, True, GDN_DK ** -0.5, "gdn_conv_q_bwd")
    dk_pre, dcw_k, _ = _conv_bwd(k_pre, cw_k, zb(1024), dk, True, 1.0, "gdn_conv_k_bwd")
    dv_pre, dcw_v, _ = _conv_bwd(v_pre, cw_v, zb(2048), dv, False, 1.0, "gdn_conv_v_bwd")
    dab = _pad_cols(jnp.concatenate([_from_rows(db_g), _from_rows(da_g)], axis=1), PAD_W)
    g_dparts = [dq_pre, dk_pre, dv_pre, dz0, dab]
    dx0, d_nw0 = _inproj_bwd(x, nw0, g_dparts, g_wparts, dx1, "gdn_inproj_bwd")
    g_dw = [_matmul_tn(h0, d, "gdn_win_grad_%d" % n) for n, d in enumerate(g_dparts)]
    d_gwin = jnp.concatenate(g_dw[:4] + [g_dw[4][:, 0:2 * GDN_HV]], axis=1)
    grads = {
        "norm_w": jnp.concatenate([d_nw0, d_nw1], axis=0),
        "gdn_w_in": d_gwin,
        "gdn_conv_w": jnp.concatenate([dcw_q, dcw_k, dcw_v], axis=1),
        "gdn_a_log": d_galog.reshape(1, -1),
        "gdn_dt_bias": d_gdtb.reshape(1, -1),
        "gdn_norm_w": d_gnw,
        "gdn_w_out": d_gwout,
        "ssd_w_in": d_swin,
        "ssd_conv_w": jnp.concatenate([dcw_x, dcw_b, dcw_c], axis=1),
        "ssd_conv_b": jnp.concatenate([dcb_x, dcb_b, dcb_c], axis=1),
        "ssd_dt_bias": d_sdtb.reshape(1, -1),
        "ssd_a_log": d_salog.reshape(1, -1),
        "ssd_d": d_sd.reshape(1, -1),
        "ssd_norm_w": d_snw,
        "ssd_w_out": d_swout,
        "final_norm_w": d_fw,
    }
    if early_grads is not None:
        return loss, dx0, grads, early_recv
    return loss, dx0, grads


WEIGHTS = ["norm_w", "gdn_w_in", "gdn_conv_w", "gdn_a_log", "gdn_dt_bias", "gdn_norm_w", "gdn_w_out", "ssd_w_in",
           "ssd_conv_w", "ssd_conv_b", "ssd_dt_bias", "ssd_a_log", "ssd_d", "ssd_norm_w", "ssd_w_out", "final_norm_w"]
COL_SHARDED = ["gdn_w_in", "ssd_w_in"]
ROW_SHARDED = ["gdn_w_out", "ssd_w_out"]
SMALL_SHARDED = ["gdn_conv_w", "ssd_conv_w", "ssd_conv_b", "ssd_norm_w"]
REPLICATED = ["norm_w", "gdn_a_log", "gdn_dt_bias", "gdn_norm_w", "ssd_dt_bias", "ssd_a_log", "ssd_d", "final_norm_w"]


def _pack(arrs):
    return jnp.concatenate([a.reshape(-1) for a in arrs]).reshape(1, -1)


def _unpack(flat, shapes):
    out, pos = [], 0
    for s in shapes:
        n = 1
        for dim in s:
            n *= dim
        out.append(flat[pos:pos + n].reshape(s))
        pos += n
    return out


def _cols_to_shards(full):
    R, C = full.shape
    return full.reshape(R, N_DEV, C // N_DEV).transpose(1, 0, 2)


def _shards_to_cols(shards):
    n, R, c = shards.shape
    return shards.transpose(1, 0, 2).reshape(R, n * c)


def kernel(x, norm_w, gdn_w_in, gdn_conv_w, gdn_a_log, gdn_dt_bias, gdn_norm_w, gdn_w_out, ssd_w_in, ssd_conv_w, ssd_conv_b, ssd_dt_bias, ssd_a_log, ssd_d, ssd_norm_w, ssd_w_out, final_norm_w, loss_target, m_norm_w, m_gdn_w_in, m_gdn_conv_w, m_gdn_a_log, m_gdn_dt_bias, m_gdn_norm_w, m_gdn_w_out, m_ssd_w_in, m_ssd_conv_w, m_ssd_conv_b, m_ssd_dt_bias, m_ssd_a_log, m_ssd_d, m_ssd_norm_w, m_ssd_w_out, m_final_norm_w, v_norm_w, v_gdn_w_in, v_gdn_conv_w, v_gdn_a_log, v_gdn_dt_bias, v_gdn_norm_w, v_gdn_w_out, v_ssd_w_in, v_ssd_conv_w, v_ssd_conv_b, v_ssd_dt_bias, v_ssd_a_log, v_ssd_d, v_ssd_norm_w, v_ssd_w_out, v_final_norm_w):
    w = dict(norm_w=norm_w, gdn_w_in=gdn_w_in[0], gdn_conv_w=gdn_conv_w[0], gdn_a_log=gdn_a_log,
             gdn_dt_bias=gdn_dt_bias, gdn_norm_w=gdn_norm_w, gdn_w_out=gdn_w_out[0], ssd_w_in=ssd_w_in[0],
             ssd_conv_w=ssd_conv_w[0], ssd_conv_b=ssd_conv_b, ssd_dt_bias=ssd_dt_bias, ssd_a_log=ssd_a_log,
             ssd_d=ssd_d, ssd_norm_w=ssd_norm_w, ssd_w_out=ssd_w_out[0], final_norm_w=final_norm_w.reshape(1, -1))
    m = dict(norm_w=m_norm_w, gdn_w_in=m_gdn_w_in[0], gdn_conv_w=m_gdn_conv_w[0], gdn_a_log=m_gdn_a_log,
             gdn_dt_bias=m_gdn_dt_bias, gdn_norm_w=m_gdn_norm_w, gdn_w_out=m_gdn_w_out[0], ssd_w_in=m_ssd_w_in[0],
             ssd_conv_w=m_ssd_conv_w[0], ssd_conv_b=m_ssd_conv_b, ssd_dt_bias=m_ssd_dt_bias, ssd_a_log=m_ssd_a_log,
             ssd_d=m_ssd_d, ssd_norm_w=m_ssd_norm_w, ssd_w_out=m_ssd_w_out[0], final_norm_w=m_final_norm_w.reshape(1, -1))
    v = dict(norm_w=v_norm_w, gdn_w_in=v_gdn_w_in[0], gdn_conv_w=v_gdn_conv_w[0], gdn_a_log=v_gdn_a_log,
             gdn_dt_bias=v_gdn_dt_bias, gdn_norm_w=v_gdn_norm_w, gdn_w_out=v_gdn_w_out[0], ssd_w_in=v_ssd_w_in[0],
             ssd_conv_w=v_ssd_conv_w[0], ssd_conv_b=v_ssd_conv_b, ssd_dt_bias=v_ssd_dt_bias, ssd_a_log=v_ssd_a_log,
             ssd_d=v_ssd_d, ssd_norm_w=v_ssd_norm_w, ssd_w_out=v_ssd_w_out[0], final_norm_w=v_final_norm_w.reshape(1, -1))
    out_shapes = {n: a.shape for n, a in zip(
        WEIGHTS, [norm_w, gdn_w_in, gdn_conv_w, gdn_a_log, gdn_dt_bias, gdn_norm_w, gdn_w_out, ssd_w_in, ssd_conv_w,
                  ssd_conv_b, ssd_dt_bias, ssd_a_log, ssd_d, ssd_norm_w, ssd_w_out, final_norm_w])}

    small_shapes = [w[n].shape for n in SMALL_SHARDED]
    first = _exchange([_mx(w["gdn_w_in"]), _pack([w[n] for n in SMALL_SHARDED])], [True] * 2, "gather_first")
    full = dict(w)
    full["gdn_w_in"] = _shards_to_cols(first[0])
    small_all = [_unpack(first[1][s, 0], small_shapes) for s in range(N_DEV)]
    for idx, n in enumerate(SMALL_SHARDED):
        full[n] = jnp.concatenate([small_all[s][idx] for s in range(N_DEV)], axis=-1)
    late = ["gdn_w_out", "ssd_w_in", "ssd_w_out"]

    def assemble(gathered):
        return {"gdn_w_out": gathered[0].reshape(-1, D_MODEL), "ssd_w_in": _shards_to_cols(gathered[1]),
                "ssd_w_out": gathered[2].reshape(-1, D_MODEL)}

    def early_grads(d_ssd_w_in, d_ssd_w_out):
        return ([_cols_to_shards(d_ssd_w_in).astype(GRAD_WIRE_DTYPE),
                 d_ssd_w_out.reshape(N_DEV, -1, D_MODEL).astype(GRAD_WIRE_DTYPE)], [False] * 2)

    loss, dx, grads, ssd_recv = _local_step(x[0], loss_target[0], full,
                                            (([_mx(w[n]) for n in late], [True] * 3), assemble), early_grads)

    send_small = jnp.concatenate(
        [_cols_to_shards(grads[n]).reshape(N_DEV, -1) for n in SMALL_SHARDED], axis=1)[:, None, :]
    rep_shapes = [w[n].shape for n in REPLICATED]
    recv = _exchange(
        [_cols_to_shards(grads["gdn_w_in"]).astype(GRAD_WIRE_DTYPE),
         grads["gdn_w_out"].reshape(N_DEV, -1, D_MODEL).astype(GRAD_WIRE_DTYPE),
         send_small, _pack([grads[n] for n in REPLICATED])],
        [False] * 3 + [True], "exchange_grads")

    res = {}
    for n, parts in zip(["gdn_w_in", "gdn_w_out", "ssd_w_in", "ssd_w_out"], [recv[0], recv[1]] + list(ssd_recv)):
        res[n] = _adamw(parts, w[n], m[n], v[n], "adamw_" + n)
    small_res = _adamw(recv[2], *[_pack([t[n] for n in SMALL_SHARDED]) for t in (w, m, v)], "adamw_small")
    rep_res = _adamw(recv[3], *[_pack([t[n] for n in REPLICATED]) for t in (w, m, v)], "adamw_replicated")
    for k4 in range(4):
        for n, a in zip(SMALL_SHARDED, _unpack(small_res[k4][0], small_shapes)):
            res.setdefault(n, [None] * 4)[k4] = a
        for n, a in zip(REPLICATED, _unpack(rep_res[k4][0], rep_shapes)):
            res.setdefault(n, [None] * 4)[k4] = a

    loss = lax.psum(loss[0, 0], ("x", "y", "c"))
    outs = [loss, dx[None]]
    for k4 in range(4):
        outs += [res[n][k4].reshape(out_shapes[n]) for n in WEIGHTS]
    return tuple(outs)
```

```python
import jax
import jax.numpy as jnp
from jax import lax
from jax.experimental import pallas as pl
from jax.experimental.pallas import tpu as pltpu

F32 = jnp.float32
MXU_DTYPE = jnp.bfloat16
GRAD_WIRE_DTYPE = jnp.bfloat16
HI = lax.Precision.HIGHEST
EPS = 1e-6
VMEM_LIMIT_BYTES = 56 * 1024 * 1024
N_DEV = 8
MESH = pl.DeviceIdType.MESH

D_MODEL = 1024
CONV_K = 4
GDN_HV = 16
GDN_DK = 128
GDN_CHUNK = 64
SSD_H = 32
SSD_P = 64
SSD_N = 128
SSD_G = 8
SSD_R = SSD_H // SSD_G
SSD_CHUNK = 128
D_INNER = 2048
PAD_W = 128

ADAM_LR = 0.001
ADAM_B1 = 0.9
ADAM_B2 = 0.999
ADAM_EPS = 1e-08
ADAM_WD = 0.01
ADAM_STEP = 10


def _params(*sem):
    return pltpu.CompilerParams(dimension_semantics=sem, vmem_limit_bytes=VMEM_LIMIT_BYTES)


def _mx(a):
    return a.astype(MXU_DTYPE)


def _dot(a, b):
    return jnp.dot(_mx(a), _mx(b), preferred_element_type=F32)


def _dot_nt(a, b):
    return lax.dot_general(_mx(a), _mx(b), (((1,), (1,)), ((), ())), preferred_element_type=F32)


def _dot_tn(a, b):
    return lax.dot_general(_mx(a), _mx(b), (((0,), (0,)), ((), ())), preferred_element_type=F32)


def _dot_hi(a, b):
    return jnp.dot(a, b, precision=HI, preferred_element_type=F32)


def _sigmoid(x):
    return 1.0 / (1.0 + jnp.exp(-x))


def _silu(x):
    return x * _sigmoid(x)


def _dsilu(x):
    s = _sigmoid(x)
    return s * (1.0 + x * (1.0 - s))


def _softplus(x):
    return jnp.maximum(x, 0.0) + jnp.log1p(jnp.exp(-jnp.abs(x)))


def _col(r, eye):
    return jnp.sum(jnp.where(eye, r, 0.0), axis=1, keepdims=True)


def _row(c, eye):
    return jnp.sum(jnp.where(eye, c, 0.0), axis=0, keepdims=True)


def _col_bcast(r, n):
    return jnp.broadcast_to(r, (n, n)).T


def _masks(n):
    r = lax.broadcasted_iota(jnp.int32, (n, n), 0)
    c = lax.broadcasted_iota(jnp.int32, (n, n), 1)
    return r >= c, r > c, r == c, r, c


def _with_exchange(comm):
    arrs, bcast = comm if comm else ([], [])
    nc = len(arrs)
    anyspec = pl.BlockSpec(memory_space=pl.ANY)

    def wrap(compute, n_in, n_out):
        def body(*refs):
            cin, cout = refs[n_in:n_in + nc], refs[n_in + nc + n_out:n_in + 2 * nc + n_out]
            sems = refs[n_in + 2 * nc + n_out:n_in + 2 * nc + n_out + 3]
            rest = refs[:n_in] + refs[n_in + nc:n_in + nc + n_out] + refs[n_in + 2 * nc + n_out + (3 if nc else 0):]
            if nc:
                @pl.when(pl.program_id(0) == 0)
                def _():
                    _Exchange(cin, cout, bcast, *sems).begin()
            compute(*rest)
            if nc:
                @pl.when(pl.program_id(0) == pl.num_programs(0) - 1)
                def _():
                    _Exchange(cin, cout, bcast, *sems).finish()
        return body

    return dict(arrs=list(arrs), nc=nc, wrap=wrap, in_specs=[anyspec] * nc, out_specs=[anyspec] * nc,
                out_shape=_exchange_out_shapes(arrs, bcast), scratch=_exchange_semaphores(nc) if nc else [])


def _norm_inproj(x, nw, wparts, name, comm=None):
    T = x.shape[0]
    tt = min(T, 256)
    n = len(wparts)
    ex = _with_exchange(comm)

    def compute(x_ref, nw_ref, *refs):
        w_refs, h_ref, o_refs = refs[:n], refs[n], refs[n + 1:]
        xv = x_ref[...]
        r = lax.rsqrt(jnp.mean(xv * xv, axis=-1, keepdims=True) + EPS)
        h = _mx(xv * r * nw_ref[...])
        h_ref[...] = h
        for w_ref, o_ref in zip(w_refs, o_refs):
            o_ref[...] = jnp.dot(h, w_ref[...], preferred_element_type=F32)

    row = lambda width: pl.BlockSpec((tt, width), lambda i: (i, 0))
    full = lambda a: pl.BlockSpec(a.shape, lambda i: (0, 0))
    outs = pl.pallas_call(
        ex["wrap"](compute, 2 + n, 1 + n), grid=(T // tt,),
        in_specs=[row(D_MODEL), full(nw)] + [full(w) for w in wparts] + ex["in_specs"],
        out_specs=[row(D_MODEL)] + [row(w.shape[1]) for w in wparts] + ex["out_specs"],
        out_shape=[jax.ShapeDtypeStruct((T, D_MODEL), MXU_DTYPE)]
        + [jax.ShapeDtypeStruct((T, w.shape[1]), F32) for w in wparts] + ex["out_shape"],
        scratch_shapes=ex["scratch"],
        compiler_params=_params("arbitrary" if comm else "parallel"), name=name,
    )(x, nw, *wparts, *ex["arrs"])
    if comm:
        return outs[0], outs[1:1 + n], outs[1 + n:]
    return outs[0], outs[1:]


def _inproj_bwd(x, nw, dparts, wparts, dres, name):
    T = x.shape[0]
    tt = min(T, 256)
    n = len(wparts)

    def body(x_ref, nw_ref, dres_ref, *refs):
        d_refs, w_refs, dx_ref, dnw_ref = refs[:n], refs[n:2 * n], refs[2 * n], refs[2 * n + 1]

        @pl.when(pl.program_id(0) == 0)
        def _():
            dnw_ref[...] = jnp.zeros_like(dnw_ref)

        dh = _dot_nt(d_refs[0][...], w_refs[0][...])
        for d_ref, w_ref in zip(d_refs[1:], w_refs[1:]):
            dh = dh + _dot_nt(d_ref[...], w_ref[...])
        xv = x_ref[...]
        r = lax.rsqrt(jnp.mean(xv * xv, axis=-1, keepdims=True) + EPS)
        xh = xv * r
        dnw_ref[...] += jnp.sum(dh * xh, axis=0, keepdims=True)
        dxn = dh * nw_ref[...]
        dx_ref[...] = dres_ref[...] + r * (dxn - xh * jnp.mean(dxn * xh, axis=-1, keepdims=True))

    row = lambda width: pl.BlockSpec((tt, width), lambda i: (i, 0))
    full = lambda a: pl.BlockSpec(a.shape, lambda i: (0, 0))
    return pl.pallas_call(
        body, grid=(T // tt,),
        in_specs=[row(D_MODEL), full(nw), row(D_MODEL)] + [row(d.shape[1]) for d in dparts]
        + [full(w) for w in wparts],
        out_specs=[row(D_MODEL), pl.BlockSpec((1, D_MODEL), lambda i: (0, 0))],
        out_shape=[jax.ShapeDtypeStruct((T, D_MODEL), F32), jax.ShapeDtypeStruct((1, D_MODEL), F32)],
        compiler_params=_params("arbitrary"), name=name,
    )(x, nw, dres, *dparts, *wparts)


def _matmul_tn(a, b, name):
    T, K = a.shape
    N = b.shape[1]
    tt = min(T, 512)
    tn = min(N, 1024)

    def body(a_ref, b_ref, o_ref):
        @pl.when(pl.program_id(1) == 0)
        def _():
            o_ref[...] = jnp.zeros_like(o_ref)

        o_ref[...] += _dot_tn(a_ref[...], b_ref[...])

    return pl.pallas_call(
        body, grid=(N // tn, T // tt),
        in_specs=[pl.BlockSpec((tt, K), lambda n, t: (t, 0)), pl.BlockSpec((tt, tn), lambda n, t: (t, n))],
        out_specs=pl.BlockSpec((K, tn), lambda n, t: (0, n)),
        out_shape=jax.ShapeDtypeStruct((K, N), F32),
        compiler_params=_params("parallel", "arbitrary"), name=name,
    )(a, b)


def _out_fwd(o, z, w, wout, xres, gs, gate_first, name):
    T = o.shape[0]
    tt = min(T, 256)
    wide = w.shape[1] == D_INNER

    def body(o_ref, z_ref, w_ref, wout_ref, x_ref, out_ref, yn):
        for g0 in range(0, D_INNER, gs):
            sl = slice(g0, g0 + gs)
            og, zg = o_ref[:, sl], z_ref[:, sl]
            wg = w_ref[:, sl] if wide else w_ref[...]
            if gate_first:
                u = og * _silu(zg)
                r = lax.rsqrt(jnp.mean(u * u, axis=-1, keepdims=True) + EPS)
                yn[:, sl] = _mx(u * r * wg)
            else:
                r = lax.rsqrt(jnp.mean(og * og, axis=-1, keepdims=True) + EPS)
                yn[:, sl] = _mx(og * r * wg * _silu(zg))
        out_ref[...] = x_ref[...] + jnp.dot(yn[...], wout_ref[...], preferred_element_type=F32)

    row = lambda width: pl.BlockSpec((tt, width), lambda i: (i, 0))
    full = lambda a: pl.BlockSpec(a.shape, lambda i: (0, 0))
    return pl.pallas_call(
        body, grid=(T // tt,),
        in_specs=[row(D_INNER), row(D_INNER), full(w), full(wout), row(D_MODEL)],
        out_specs=row(D_MODEL),
        out_shape=jax.ShapeDtypeStruct((T, D_MODEL), F32),
        scratch_shapes=[pltpu.VMEM((tt, D_INNER), MXU_DTYPE)],
        compiler_params=_params("parallel"), name=name,
    )(o, z, w, wout, xres)


def _out_bwd(dx, o, z, w, wout, gs, gate_first, name, comm=None):
    T = o.shape[0]
    tt = min(T, 256)
    wide = w.shape[1] == D_INNER

    def body(dx_ref, o_ref, z_ref, w_ref, wout_ref, do_ref, dz_ref, dw_ref, yn_ref):
        @pl.when(pl.program_id(0) == 0)
        def _():
            dw_ref[...] = jnp.zeros_like(dw_ref)

        dyn = _dot_nt(dx_ref[...], wout_ref[...])
        dw_acc = jnp.zeros((1, gs), F32)
        for g0 in range(0, D_INNER, gs):
            sl = slice(g0, g0 + gs)
            og, zg, dg = o_ref[:, sl], z_ref[:, sl], dyn[:, sl]
            wg = w_ref[:, sl] if wide else w_ref[...]
            sz = _silu(zg)
            if gate_first:
                u = og * sz
                r = lax.rsqrt(jnp.mean(u * u, axis=-1, keepdims=True) + EPS)
                uh = u * r
                yn_ref[:, sl] = _mx(uh * wg)
                dw_g = jnp.sum(dg * uh, axis=0, keepdims=True)
                duh = dg * wg
                du = r * (duh - uh * jnp.mean(duh * uh, axis=-1, keepdims=True))
                do_ref[:, sl] = du * sz
                dz_ref[:, sl] = _mx(du * og * _dsilu(zg))
            else:
                r = lax.rsqrt(jnp.mean(og * og, axis=-1, keepdims=True) + EPS)
                oh = og * r
                yn_ref[:, sl] = _mx(oh * wg * sz)
                dw_g = jnp.sum(dg * oh * sz, axis=0, keepdims=True)
                doh = dg * wg * sz
                dz_ref[:, sl] = _mx(dg * oh * wg * _dsilu(zg))
                do_ref[:, sl] = r * (doh - oh * jnp.mean(doh * oh, axis=-1, keepdims=True))
            if wide:
                dw_ref[:, sl] += dw_g
            else:
                dw_acc = dw_acc + dw_g
        if not wide:
            dw_ref[...] += dw_acc

    row = lambda width: pl.BlockSpec((tt, width), lambda i: (i, 0))
    full = lambda a: pl.BlockSpec(a.shape, lambda i: (0, 0))
    ex = _with_exchange(comm)
    outs = pl.pallas_call(
        ex["wrap"](body, 5, 4), grid=(T // tt,),
        in_specs=[row(D_MODEL), row(D_INNER), row(D_INNER), full(w), full(wout)] + ex["in_specs"],
        out_specs=[row(D_INNER), row(D_INNER), full(w), row(D_INNER)] + ex["out_specs"],
        out_shape=[jax.ShapeDtypeStruct((T, D_INNER), F32), jax.ShapeDtypeStruct((T, D_INNER), MXU_DTYPE),
                   jax.ShapeDtypeStruct(w.shape, F32), jax.ShapeDtypeStruct((T, D_INNER), MXU_DTYPE)]
        + ex["out_shape"],
        scratch_shapes=ex["scratch"],
        compiler_params=_params("arbitrary"), name=name,
    )(dx, o, z, w, wout, *ex["arrs"])
    outs = list(outs)
    return outs[:4] + ([outs[4:]] if comm else [])


HALO = 8
CONV_STRIP = 16


def _conv_fwd(pre, w, b, l2, scale, name):
    T, C = pre.shape
    tt = min(T, 512)
    tc = min(C, 1024)
    strip = tt

    def body(pre_ref, halo_ref, w_ref, b_ref, out_ref, P):
        i = pl.program_id(0)
        P[0:HALO, :] = jnp.where(i > 0, halo_ref[...], 0.0)
        P[HALO:HALO + tt, :] = pre_ref[...]
        wj = [w_ref[j:j + 1, :] for j in range(CONV_K)]
        bias = b_ref[...]
        for r0 in range(0, tt, strip):
            acc = bias + wj[0] * P[pl.ds(HALO - 3 + r0, strip), :]
            for j in range(1, CONV_K):
                acc = acc + wj[j] * P[pl.ds(HALO - 3 + j + r0, strip), :]
            s = _silu(acc)
            if l2:
                sls = [slice(g0, g0 + GDN_DK) for g0 in range(0, tc, GDN_DK)]
                rr = [lax.rsqrt(jnp.sum(s[:, sl] * s[:, sl], axis=-1, keepdims=True) + EPS) for sl in sls]
                for sl, r in zip(sls, rr):
                    out_ref[r0:r0 + strip, sl] = s[:, sl] * r * scale
            else:
                out_ref[r0:r0 + strip, :] = s

    return pl.pallas_call(
        body, grid=(T // tt, C // tc),
        in_specs=[pl.BlockSpec((tt, tc), lambda i, j: (i, j)),
                  pl.BlockSpec((HALO, tc), lambda i, j: (jnp.maximum(i * (tt // HALO) - 1, 0), j)),
                  pl.BlockSpec((CONV_K, tc), lambda i, j: (0, j)),
                  pl.BlockSpec((1, tc), lambda i, j: (0, j))],
        out_specs=pl.BlockSpec((tt, tc), lambda i, j: (i, j)),
        out_shape=jax.ShapeDtypeStruct((T, C), F32),
        scratch_shapes=[pltpu.VMEM((HALO + tt, tc), F32)],
        compiler_params=_params("parallel", "parallel"), name=name,
    )(pre, pre, w, b)


def _conv_bwd(pre, w, b, dpost, l2, scale, name):
    T, C = pre.shape
    tt = min(T, 512)
    tc = min(C, 1024 if l2 else 512)
    strip = tt if l2 else CONV_STRIP
    nT = T // tt
    ext = tt + HALO

    def body(pre_ref, hp_ref, hn_ref, dpost_ref, dn_ref, w_ref, b_ref, dpre_ref, dw_ref, db_ref, P, Q):
        i = pl.program_id(1)

        @pl.when(i == 0)
        def _():
            dw_ref[...] = jnp.zeros_like(dw_ref)
            db_ref[...] = jnp.zeros_like(db_ref)

        P[0:HALO, :] = jnp.where(i > 0, hp_ref[...], 0.0)
        P[HALO:HALO + tt, :] = pre_ref[...]
        P[HALO + tt:HALO + ext, :] = hn_ref[...]
        wj = [w_ref[j:j + 1, :] for j in range(CONV_K)]
        bias = b_ref[...]
        keep_next = jnp.where(i < nT - 1, 1.0, 0.0)
        for r0 in list(range(0, tt, strip)) + [tt]:
            n = strip if r0 < tt else HALO
            cpre = bias + wj[0] * P[pl.ds(HALO - 3 + r0, n), :]
            for j in range(1, CONV_K):
                cpre = cpre + wj[j] * P[pl.ds(HALO - 3 + j + r0, n), :]
            dy = dpost_ref[r0:r0 + n, :] if r0 < tt else dn_ref[...] * keep_next
            sg = _sigmoid(cpre)
            ds_c = sg * (1.0 + cpre * (1.0 - sg))
            if l2:
                s = cpre * sg
                sls = [slice(g0, g0 + GDN_DK) for g0 in range(0, tc, GDN_DK)]
                rr = [lax.rsqrt(jnp.sum(s[:, sl] * s[:, sl], axis=-1, keepdims=True) + EPS) for sl in sls]
                yh = [s[:, sl] * r for sl, r in zip(sls, rr)]
                pr = [jnp.sum(dy[:, sl] * y, axis=-1, keepdims=True) for sl, y in zip(sls, yh)]
                for sl, r, y, p in zip(sls, rr, yh, pr):
                    Q[r0:r0 + n, sl] = (scale * r) * (dy[:, sl] - y * p) * ds_c[:, sl]
            else:
                Q[r0:r0 + n, :] = dy * ds_c
        fold = lambda a: jnp.sum(a.reshape(strip // 8, 8, tc), axis=0)
        dw_acc = [jnp.zeros((8, tc), F32) for _ in range(CONV_K)]
        db_acc = jnp.zeros((8, tc), F32)
        for r0 in range(0, tt, strip):
            dpre = wj[0] * Q[pl.ds(3 + r0, strip), :]
            for j in range(1, CONV_K):
                dpre = dpre + wj[j] * Q[pl.ds(3 - j + r0, strip), :]
            dpre_ref[r0:r0 + strip, :] = _mx(dpre)
            dyc = Q[r0:r0 + strip, :]
            for j in range(CONV_K):
                dw_acc[j] = dw_acc[j] + fold(dyc * P[pl.ds(HALO - 3 + j + r0, strip), :])
            db_acc = db_acc + fold(dyc)
        for j in range(CONV_K):
            dw_ref[j:j + 1, :] += jnp.sum(dw_acc[j], axis=0, keepdims=True)
        db_ref[...] += jnp.sum(db_acc, axis=0, keepdims=True)

    tile = pl.BlockSpec((tt, tc), lambda j, i: (i, j))
    prev = pl.BlockSpec((HALO, tc), lambda j, i: (jnp.maximum(i * (tt // HALO) - 1, 0), j))
    nxt = pl.BlockSpec((HALO, tc), lambda j, i: (jnp.minimum((i + 1) * (tt // HALO), T // HALO - 1), j))
    return pl.pallas_call(
        body, grid=(C // tc, nT),
        in_specs=[tile, prev, nxt, tile, nxt,
                  pl.BlockSpec((CONV_K, tc), lambda j, i: (0, j)), pl.BlockSpec((1, tc), lambda j, i: (0, j))],
        out_specs=[tile, pl.BlockSpec((CONV_K, tc), lambda j, i: (0, j)), pl.BlockSpec((1, tc), lambda j, i: (0, j))],
        out_shape=[jax.ShapeDtypeStruct((T, C), MXU_DTYPE), jax.ShapeDtypeStruct((CONV_K, C), F32),
                   jax.ShapeDtypeStruct((1, C), F32)],
        scratch_shapes=[pltpu.VMEM((HALO + ext, tc), F32), pltpu.VMEM((ext, tc), F32)],
        compiler_params=_params("parallel", "arbitrary"), name=name,
    )(pre, pre, pre, dpost, dpost, w, b)


GDN_LOCKSTEP_CHUNKS = 16
GDN_SCAN_HEADS = 16


def _inv_unit_lower_many(nms, eye, n):
    xs = [jnp.where(eye, 1.0, 0.0) - nm for nm in nms]
    ps = list(nms)
    k = 2
    while k < n:
        ps = [_dot(p, p) for p in ps]
        xs = [x + _dot(x, p) for x, p in zip(xs, ps)]
        k *= 2
    return xs


def _gdn_prep(q, k, v, araw, braw, alog, dtb, name):
    T = q.shape[0]
    C = GDN_CHUNK
    tt = min(T, 1024)
    cpt, nC = tt // C, T // C
    grp = min(cpt, GDN_LOCKSTEP_CHUNKS)

    def body(alog_ref, dtb_ref, q_ref, k_ref, v_ref, a_ref, b_ref,
             u_ref, w_ref, pm_ref, ti_ref, g_ref, beta_ref, gc_ref):
        j = pl.program_id(0)
        tri, strict, eye, r_i, c_i = _masks(C)
        upper = jnp.where(r_i <= c_i, 1.0, 0.0)
        gcs, bts = [], []
        for hh in range(2):
            h = 2 * j + hh
            g = -jnp.exp(alog_ref[h]) * _softplus(a_ref[hh] + dtb_ref[h])
            bt = _sigmoid(b_ref[hh])
            gc = _dot_hi(g, upper)
            g_ref[hh], beta_ref[hh], gc_ref[hh] = g, bt, gc
            gcs.append(gc)
            bts.append(bt)
        for c0 in range(0, cpt, grp):
            cs = list(range(c0, c0 + grp))
            inst = [(c, hh) for c in cs for hh in range(2)]
            rows = {c: slice(c * C, (c + 1) * C) for c in cs}
            qc = {c: q_ref[rows[c], :] for c in cs}
            kc = {c: k_ref[rows[c], :] for c in cs}
            kk = {c: _dot_nt(kc[c], kc[c]) for c in cs}
            qk = {c: _dot_nt(qc[c], kc[c]) for c in cs}
            gcr = [gcs[hh][c:c + 1, :] for c, hh in inst]
            gcc = [_col(r, eye) for r in gcr]
            bc = [_col(bts[hh][c:c + 1, :], eye) for c, hh in inst]
            lm = [jnp.exp(jnp.where(tri, cc - r, -1e30)) for cc, r in zip(gcc, gcr)]
            nm = [jnp.where(strict, kk[c] * b * l, 0.0) for (c, hh), b, l in zip(inst, bc, lm)]
            tinv = _inv_unit_lower_many(nm, eye, C)
            rhs = [jnp.concatenate([v_ref[rows[c], hh * GDN_DK:(hh + 1) * GDN_DK] * b, kc[c] * (b * jnp.exp(cc))], axis=1)
                   for (c, hh), b, cc in zip(inst, bc, gcc)]
            sol = [_dot(t, r) for t, r in zip(tinv, rhs)]
            for (c, hh), s, t, l in zip(inst, sol, tinv, lm):
                hs = slice(hh * GDN_DK, (hh + 1) * GDN_DK)
                u_ref[rows[c], hs] = s[:, :GDN_DK]
                w_ref[rows[c], hs] = _mx(s[:, GDN_DK:])
                pm_ref[hh, c] = _mx(jnp.where(tri, qk[c] * l, 0.0))
                ti_ref[hh, c] = _mx(t)

    smem = pl.BlockSpec(memory_space=pltpu.SMEM)
    rows_spec = pl.BlockSpec((2, cpt, C), lambda j, i: (j, i, 0))
    qk_spec = pl.BlockSpec((tt, GDN_DK), lambda j, i: (i, j))
    v_spec = pl.BlockSpec((tt, 2 * GDN_DK), lambda j, i: (i, j))
    cc_spec = pl.BlockSpec((2, cpt, C, C), lambda j, i: (j, i, 0, 0))
    rows_shape = jax.ShapeDtypeStruct((GDN_HV, nC, C), F32)
    cc_shape = jax.ShapeDtypeStruct((GDN_HV, nC, C, C), MXU_DTYPE)
    return pl.pallas_call(
        body, grid=(GDN_HV // 2, T // tt),
        in_specs=[smem, smem, qk_spec, qk_spec, v_spec, rows_spec, rows_spec],
        out_specs=[v_spec, v_spec, cc_spec, cc_spec, rows_spec, rows_spec, rows_spec],
        out_shape=[jax.ShapeDtypeStruct((T, D_INNER), F32), jax.ShapeDtypeStruct((T, D_INNER), MXU_DTYPE),
                   cc_shape, cc_shape, rows_shape, rows_shape, rows_shape],
        compiler_params=_params("parallel", "parallel"), name=name,
    )(alog, dtb, q, k, v, araw, braw)


def _gdn_decays(gc_ref, h, c, eye, C):
    gcr = gc_ref[h, pl.ds(c, 1), :]
    gcc = _col(gcr, eye)
    glast = gcr[:, C - 1:C]
    return jnp.exp(gcc), jnp.exp(glast - gcc), jnp.exp(glast)


def _gdn_state_fwd(q, k, u, w, pm, gc, name):
    T = q.shape[0]
    C = GDN_CHUNK
    HG = GDN_SCAN_HEADS
    tt = min(T, 512)
    cpt, nC = tt // C, T // C

    def body(q_ref, k_ref, u_ref, w_ref, pm_ref, gc_ref, o_ref, vn_ref, sall_ref, S):
        @pl.when(pl.program_id(1) == 0)
        def _():
            S[...] = jnp.zeros_like(S)

        eye = _masks(C)[2]
        heads = list(range(HG))

        def chunk(c, carry):
            rows = pl.ds(pl.multiple_of(c * C, C), C)
            hs = [slice(h * GDN_DK, (h + 1) * GDN_DK) for h in heads]
            qs = [slice((h // 2) * GDN_DK, (h // 2 + 1) * GDN_DK) for h in heads]
            dec = [_gdn_decays(gc_ref, h, c, eye, C) for h in heads]
            sv = [S[h] for h in heads]
            for h in heads:
                sall_ref[h, c] = _mx(sv[h])
            ws = [_dot(w_ref[rows, hs[h]], sv[h]) for h in heads]
            qsv = [_dot(q_ref[rows, qs[h]], sv[h]) for h in heads]
            vn = [u_ref[rows, hs[h]] - ws[h] for h in heads]
            pv = [_dot(pm_ref[h, c], vn[h]) for h in heads]
            kv = [_dot_tn(k_ref[rows, qs[h]], vn[h] * dec[h][1]) for h in heads]
            for h in heads:
                vn_ref[rows, hs[h]] = _mx(vn[h])
                o_ref[rows, hs[h]] = qsv[h] * dec[h][0] + pv[h]
                S[h] = sv[h] * dec[h][2] + kv[h]
            return carry

        lax.fori_loop(0, cpt, chunk, 0)

    qk_spec = pl.BlockSpec((tt, HG // 2 * GDN_DK), lambda g, i: (i, g))
    v_spec = pl.BlockSpec((tt, HG * GDN_DK), lambda g, i: (i, g))
    return pl.pallas_call(
        body, grid=(GDN_HV // HG, T // tt),
        in_specs=[qk_spec, qk_spec, v_spec, v_spec,
                  pl.BlockSpec((HG, cpt, C, C), lambda g, i: (g, i, 0, 0)),
                  pl.BlockSpec((HG, cpt, C), lambda g, i: (g, i, 0))],
        out_specs=[v_spec, v_spec, pl.BlockSpec((HG, cpt, GDN_DK, GDN_DK), lambda g, i: (g, i, 0, 0))],
        out_shape=[jax.ShapeDtypeStruct((T, D_INNER), F32), jax.ShapeDtypeStruct((T, D_INNER), MXU_DTYPE),
                   jax.ShapeDtypeStruct((GDN_HV, nC, GDN_DK, GDN_DK), MXU_DTYPE)],
        scratch_shapes=[pltpu.VMEM((HG, GDN_DK, GDN_DK), F32)],
        compiler_params=_params("parallel", "arbitrary"), name=name,
    )(q, k, u, w, pm, gc)


def _gdn_state_bwd(q, k, w, pm, vn, sall, gc, do, name):
    T = q.shape[0]
    C = GDN_CHUNK
    HG = GDN_SCAN_HEADS
    tt = min(T, 512)
    cpt, nC, nT = tt // C, T // C, T // tt

    def body(q_ref, k_ref, w_ref, pm_ref, vn_ref, sall_ref, gc_ref, do_ref, dvn_ref, dkd_ref, dgl_ref, dS):
        @pl.when(pl.program_id(1) == 0)
        def _():
            dS[...] = jnp.zeros_like(dS)

        eye = _masks(C)[2]
        heads = list(range(HG))

        def chunk(ci, carry):
            c = cpt - 1 - ci
            rows = pl.ds(pl.multiple_of(c * C, C), C)
            hs = [slice(h * GDN_DK, (h + 1) * GDN_DK) for h in heads]
            qs = [slice((h // 2) * GDN_DK, (h // 2 + 1) * GDN_DK) for h in heads]
            dec = [_gdn_decays(gc_ref, h, c, eye, C) for h in heads]
            dsn = [dS[h] for h in heads]
            doc = [do_ref[rows, hs[h]] for h in heads]
            kds = [_dot(k_ref[rows, qs[h]], dsn[h]) for h in heads]
            pdo = [_dot_tn(pm_ref[h, c], doc[h]) for h in heads]
            dkd = [_dot_nt(vn_ref[rows, hs[h]], dsn[h]) for h in heads]
            qdo = [_dot_tn(q_ref[rows, qs[h]], doc[h] * dec[h][0]) for h in heads]
            dvn = [pdo[h] + kds[h] * dec[h][1] for h in heads]
            wdv = [_dot_tn(w_ref[rows, hs[h]], dvn[h]) for h in heads]
            for h in heads:
                dgl = jnp.sum(jnp.sum(dsn[h] * sall_ref[h, c].astype(F32), axis=0, keepdims=True), axis=1, keepdims=True)
                dgl_ref[h, pl.ds(c, 1), :] = jnp.broadcast_to(dgl, (1, C))
                dvn_ref[rows, hs[h]] = dvn[h]
                dkd_ref[rows, hs[h]] = dkd[h]
                dS[h] = dsn[h] * dec[h][2] + qdo[h] - wdv[h]
            return carry

        lax.fori_loop(0, cpt, chunk, 0)

    rev = lambda i: nT - 1 - i
    qk_spec = pl.BlockSpec((tt, HG // 2 * GDN_DK), lambda g, i: (rev(i), g))
    v_spec = pl.BlockSpec((tt, HG * GDN_DK), lambda g, i: (rev(i), g))
    rows_spec = pl.BlockSpec((HG, cpt, C), lambda g, i: (g, rev(i), 0))
    return pl.pallas_call(
        body, grid=(GDN_HV // HG, nT),
        in_specs=[qk_spec, qk_spec, v_spec, pl.BlockSpec((HG, cpt, C, C), lambda g, i: (g, rev(i), 0, 0)), v_spec,
                  pl.BlockSpec((HG, cpt, GDN_DK, GDN_DK), lambda g, i: (g, rev(i), 0, 0)), rows_spec, v_spec],
        out_specs=[v_spec, v_spec, rows_spec],
        out_shape=[jax.ShapeDtypeStruct((T, D_INNER), F32), jax.ShapeDtypeStruct((T, D_INNER), F32),
                   jax.ShapeDtypeStruct((GDN_HV, nC, C), F32)],
        scratch_shapes=[pltpu.VMEM((HG, GDN_DK, GDN_DK), F32)],
        compiler_params=_params("parallel", "arbitrary"), name=name,
    )(q, k, w, pm, vn, sall, gc, do)


def _gdn_local_bwd(q, k, v, gc, beta, tinv, u, w, pm, vn, sall, do, dvn, dkd, dgl, name):
    T = q.shape[0]
    C = GDN_CHUNK
    tt = min(T, 1024)
    cpt, nC = tt // C, T // C
    grp = min(cpt, GDN_LOCKSTEP_CHUNKS)

    def body(q_ref, k_ref, v_ref, gc_ref, b_ref, ti_ref, u_ref, w_ref, pm_ref, vn_ref, sall_ref, do_ref,
             dvn_ref, dkd_ref, dgl_ref, dq_ref, dk_ref, dv_ref, dg_ref, dbeta_ref, dgc_s):
        tri, strict, eye, r_i, c_i = _masks(C)
        lower = jnp.where(r_i >= c_i, 1.0, 0.0)
        lane = lax.broadcasted_iota(jnp.int32, (1, C), 1)
        rsum = lambda a: jnp.sum(a, axis=1, keepdims=True)
        for c0 in range(0, cpt, grp):
            cs = list(range(c0, c0 + grp))
            inst = [(c, hh) for c in cs for hh in range(2)]
            n = len(inst)
            rows = {c: slice(c * C, (c + 1) * C) for c in cs}
            hsl = [slice(hh * GDN_DK, (hh + 1) * GDN_DK) for c, hh in inst]
            qc = {c: q_ref[rows[c], :] for c in cs}
            kc = {c: k_ref[rows[c], :] for c in cs}
            kk = {c: _dot_nt(kc[c], kc[c]) for c in cs}
            gcr = [gc_ref[hh, c:c + 1, :] for c, hh in inst]
            gcc = [_col(r, eye) for r in gcr]
            bc = [_col(b_ref[hh, c:c + 1, :], eye) for c, hh in inst]
            lm = [jnp.exp(jnp.where(tri, cc - r, -1e30)) for cc, r in zip(gcc, gcr)]
            e_c = [jnp.exp(cc) for cc in gcc]
            el_c = [jnp.exp(r[:, C - 1:C] - cc) for cc, r in zip(gcc, gcr)]
            gl = [jnp.exp(r[:, C - 1:C]) for r in gcr]
            doc = [do_ref[rows[c], hsl[i]] for i, (c, hh) in enumerate(inst)]
            dvn = [dvn_ref[rows[c], hsl[i]] for i, (c, hh) in enumerate(inst)]
            sv = [sall_ref[hh, c] for c, hh in inst]
            aa = [_dot_nt(jnp.concatenate([_mx(doc[i]), _mx(dvn[i])], axis=0), sv[i]) for i in range(n)]
            dpm = [jnp.where(tri, _dot_nt(doc[i], vn_ref[rows[c], hsl[i]]), 0.0) for i, (c, hh) in enumerate(inst)]
            dqd = [a[:C] for a in aa]
            drhs = [_dot_tn(ti_ref[hh, c], jnp.concatenate([dvn[i], -aa[i][C:]], axis=1))
                    for i, (c, hh) in enumerate(inst)]
            sol = [jnp.concatenate([_mx(u_ref[rows[c], hsl[i]]), w_ref[rows[c], hsl[i]]], axis=1)
                   for i, (c, hh) in enumerate(inst)]
            dnm = [-jnp.where(strict, _dot_nt(drhs[i], sol[i]), 0.0) for i in range(n)]
            dkk = [dnm[i] * bc[i] * lm[i] for i in range(n)]
            dqk = [dpm[i] * lm[i] for i in range(n)]
            dq1 = [_dot(dqk[i], kc[c]) for i, (c, hh) in enumerate(inst)]
            dk1 = [_dot(dkk[i], kc[c]) for i, (c, hh) in enumerate(inst)]
            dk2 = [_dot_tn(dkk[i], kc[c]) for i, (c, hh) in enumerate(inst)]
            dk3 = [_dot_tn(dqk[i], qc[c]) for i, (c, hh) in enumerate(inst)]
            dq_acc = {c: jnp.zeros((C, GDN_DK), F32) for c in cs}
            dk_acc = {c: jnp.zeros((C, GDN_DK), F32) for c in cs}
            for i, (c, hh) in enumerate(inst):
                k_, q_, v_ = kc[c], qc[c], v_ref[rows[c], hsl[i]]
                dvb, dkbe = drhs[i][:, :GDN_DK], drhs[i][:, GDN_DK:]
                dkd = dkd_ref[rows[c], hsl[i]]
                kb = k_ * bc[i]
                dkb = dkbe * e_c[i]
                del_el = dkd * k_ * el_c[i]
                dbc = rsum(dnm[i] * kk[c] * lm[i]) + rsum(dkb * k_ + dvb * v_)
                dq_acc[c] = dq_acc[c] + dq1[i] + dqd[i] * e_c[i]
                dk_acc[c] = dk_acc[c] + dk1[i] + dk2[i] + dk3[i] + dkd * el_c[i] + dkb * bc[i]
                dv_ref[rows[c], hsl[i]] = dvb * bc[i]
                nm = jnp.where(strict, kk[c] * bc[i] * lm[i], 0.0)
                gm = dnm[i] * nm + dpm[i] * pm_ref[hh, c].astype(F32)
                dgc_col = rsum(gm) + rsum((dkbe * kb + dqd[i] * q_) * e_c[i] - del_el)
                dglast = (jnp.sum(jnp.sum(del_el, axis=0, keepdims=True), axis=1, keepdims=True)
                          + dgl_ref[hh, c:c + 1, 0:1] * gl[i])
                dgc_s[hh, c:c + 1, :] = (_row(dgc_col, eye) - jnp.sum(gm, axis=0, keepdims=True)
                                         + jnp.where(lane == C - 1, dglast, 0.0))
                dbeta_ref[hh, c:c + 1, :] = _row(dbc, eye)
            for c in cs:
                dq_ref[rows[c], :] = dq_acc[c]
                dk_ref[rows[c], :] = dk_acc[c]
        for hh in range(2):
            dg_ref[hh] = _dot_hi(dgc_s[hh], lower)

    rows_spec = pl.BlockSpec((2, cpt, C), lambda j, i: (j, i, 0))
    qk_spec = pl.BlockSpec((tt, GDN_DK), lambda j, i: (i, j))
    v_spec = pl.BlockSpec((tt, 2 * GDN_DK), lambda j, i: (i, j))
    cc_spec = pl.BlockSpec((2, cpt, C, C), lambda j, i: (j, i, 0, 0))
    rows_shape = jax.ShapeDtypeStruct((GDN_HV, nC, C), F32)
    return pl.pallas_call(
        body, grid=(GDN_HV // 2, T // tt),
        in_specs=[qk_spec, qk_spec, v_spec, rows_spec, rows_spec, cc_spec, v_spec, v_spec, cc_spec, v_spec,
                  pl.BlockSpec((2, cpt, GDN_DK, GDN_DK), lambda j, i: (j, i, 0, 0)), v_spec, v_spec, v_spec, rows_spec],
        out_specs=[qk_spec, qk_spec, v_spec, rows_spec, rows_spec],
        out_shape=[jax.ShapeDtypeStruct((T, GDN_HV // 2 * GDN_DK), F32),
                   jax.ShapeDtypeStruct((T, GDN_HV // 2 * GDN_DK), F32),
                   jax.ShapeDtypeStruct((T, D_INNER), F32), rows_shape, rows_shape],
        scratch_shapes=[pltpu.VMEM((2, cpt, C), F32)],
        compiler_params=_params("parallel", "parallel"), name=name,
    )(q, k, v, gc, beta, tinv, u, w, pm, vn, sall, do, dvn, dkd, dgl)


def _gdn_gate_bwd(araw, braw, dg, dbeta, alog, dtb, name):
    H, T = araw.shape

    def body(a_ref, b_ref, dg_ref, dbt_ref, alog_ref, dtb_ref, da_ref, db_ref, dalog_ref, ddtb_ref):
        xa = a_ref[...] + dtb_ref[...]
        ea = jnp.exp(alog_ref[...])
        dgv = dg_ref[...]
        da = -dgv * ea * _sigmoid(xa)
        da_ref[...] = da
        dalog_ref[...] = jnp.sum(-dgv * ea * _softplus(xa), axis=1, keepdims=True)
        ddtb_ref[...] = jnp.sum(da, axis=1, keepdims=True)
        bt = _sigmoid(b_ref[...])
        db_ref[...] = dbt_ref[...] * bt * (1.0 - bt)

    return pl.pallas_call(
        body,
        out_shape=[jax.ShapeDtypeStruct((H, T), F32), jax.ShapeDtypeStruct((H, T), F32),
                   jax.ShapeDtypeStruct((H, 1), F32), jax.ShapeDtypeStruct((H, 1), F32)],
        compiler_params=pltpu.CompilerParams(vmem_limit_bytes=VMEM_LIMIT_BYTES), name=name,
    )(araw, braw, dg, dbeta, alog, dtb)


SSD_LOCKSTEP_CHUNKS = 2
SSD_LOCKSTEP_CHUNKS_BWD = 1


def _ssd_scan_fwd(xs, bm, cm, dtraw, alog, dtb, dskip, name):
    T = xs.shape[0]
    Q = SSD_CHUNK
    tt = min(T, 1024)
    cpt, nC = tt // Q, T // Q
    GW = SSD_R * SSD_P

    def body(alog_ref, dtb_ref, dsk_ref, xs_ref, b_ref, c_ref, dt_ref, y_ref, sall_ref, dto_ref, S, dt_s, acs_s):
        gi, i = pl.program_id(0), pl.program_id(1)

        @pl.when(i == 0)
        def _():
            S[...] = jnp.zeros_like(S)

        tri, _, eye, r_i, c_i = _masks(Q)
        upper = jnp.where(r_i <= c_i, 1.0, 0.0)
        for r in range(SSD_R):
            h = SSD_R * gi + r
            dt = _softplus(dt_ref[r] + dtb_ref[h])
            dto_ref[r] = dt
            dt_s[r] = dt
            acs_s[r] = _dot_hi(-jnp.exp(alog_ref[h]) * dt, upper)

        ps = [slice(r * SSD_P, (r + 1) * SSD_P) for r in range(SSD_R)]
        s_cur = [S[:, ps[r]] for r in range(SSD_R)]
        grp = min(cpt, SSD_LOCKSTEP_CHUNKS)
        for c0 in range(0, cpt, grp):
            cs = list(range(c0, c0 + grp))
            inst = [(c, r) for c in cs for r in range(SSD_R)]
            rows = {c: slice(c * Q, (c + 1) * Q) for c in cs}
            bc_ = {c: b_ref[rows[c], :] for c in cs}
            cc_ = {c: c_ref[rows[c], :] for c in cs}
            cb = {c: _dot_nt(cc_[c], bc_[c]) for c in cs}
            xr = [xs_ref[rows[c], ps[r]] for c, r in inst]
            acr = [acs_s[r, c:c + 1, :] for c, r in inst]
            acc = [_col_bcast(a, Q) for a in acr]
            dtc = [_col_bcast(dt_s[r, c:c + 1, :], Q)[:, :SSD_P] for c, r in inst]
            xd = [x * d for x, d in zip(xr, dtc)]
            mm = [cb[c] * jnp.exp(jnp.where(tri, acc[i] - acr[i], -1e30)) for i, (c, r) in enumerate(inst)]
            bct = {c: bc_[c].T for c in cs}
            st = [_dot(bct[c] * jnp.exp(acr[i][:, Q - 1:Q] - acr[i]), xd[i]) for i, (c, r) in enumerate(inst)]
            yd = [_dot(mm[i], xd[i]) for i in range(len(inst))]
            s_prev = []
            for i, (c, r) in enumerate(inst):
                s_prev.append(s_cur[r])
                s_cur[r] = s_cur[r] * jnp.exp(acr[i][:, Q - 1:Q]) + st[i]
            yo = [_dot(cc_[c] * jnp.exp(acc[i]), s_prev[i]) for i, (c, r) in enumerate(inst)]
            for i, (c, r) in enumerate(inst):
                sall_ref[0, c, :, ps[r]] = s_prev[i]
                y_ref[rows[c], ps[r]] = yd[i] + yo[i] + dsk_ref[SSD_R * gi + r] * xr[i]
        for r in range(SSD_R):
            S[:, ps[r]] = s_cur[r]

    smem = pl.BlockSpec(memory_space=pltpu.SMEM)
    rows_spec = pl.BlockSpec((SSD_R, cpt, Q), lambda g, i: (g, i, 0))
    return pl.pallas_call(
        body, grid=(SSD_G, T // tt),
        in_specs=[smem, smem, smem,
                  pl.BlockSpec((tt, GW), lambda g, i: (i, g)), pl.BlockSpec((tt, SSD_N), lambda g, i: (i, g)),
                  pl.BlockSpec((tt, SSD_N), lambda g, i: (i, g)), rows_spec],
        out_specs=[pl.BlockSpec((tt, GW), lambda g, i: (i, g)),
                   pl.BlockSpec((1, cpt, SSD_N, GW), lambda g, i: (g, i, 0, 0)), rows_spec],
        out_shape=[jax.ShapeDtypeStruct((T, D_INNER), F32), jax.ShapeDtypeStruct((SSD_G, nC, SSD_N, GW), F32),
                   jax.ShapeDtypeStruct((SSD_H, nC, Q), F32)],
        scratch_shapes=[pltpu.VMEM((SSD_N, GW), F32), pltpu.VMEM((SSD_R, cpt, Q), F32),
                        pltpu.VMEM((SSD_R, cpt, Q), F32)],
        compiler_params=_params("parallel", "arbitrary"), name=name,
    )(alog, dtb, dskip, xs, bm, cm, dtraw)


def _ssd_scan_bwd(xs, bm, cm, dt, sall, dy, alog, dskip, name):
    T = xs.shape[0]
    Q = SSD_CHUNK
    tt = min(T, 1024)
    cpt, nC, nT = tt // Q, T // Q, T // tt
    GW = SSD_R * SSD_P

    def body(alog_ref, dsk_ref, xs_ref, b_ref, c_ref, dt_ref, sall_ref, dy_ref,
             dxs_ref, db_ref, dc_ref, da_ref, ddt_ref, dd_ref, dS, acs_s, dacs_s, ddt_s, dd_s):
        gi, i = pl.program_id(0), pl.program_id(1)

        @pl.when(i == 0)
        def _():
            dS[...] = jnp.zeros_like(dS)

        tri, _, eye, r_i, c_i = _masks(Q)
        upper = jnp.where(r_i <= c_i, 1.0, 0.0)
        lower = jnp.where(r_i >= c_i, 1.0, 0.0)
        lane = lax.broadcasted_iota(jnp.int32, (1, Q), 1)
        for r in range(SSD_R):
            acs_s[r] = _dot_hi(-jnp.exp(alog_ref[SSD_R * gi + r]) * dt_ref[r], upper)

        ps = [slice(r * SSD_P, (r + 1) * SSD_P) for r in range(SSD_R)]
        ds_cur = [dS[:, ps[r]] for r in range(SSD_R)]
        grp = min(cpt, SSD_LOCKSTEP_CHUNKS_BWD)
        csum = lambda a: jnp.sum(a, axis=0, keepdims=True)
        tsum = lambda a: jnp.sum(csum(a), axis=1, keepdims=True)
        ones8 = jnp.ones((8, SSD_P), F32)
        for c0 in range(cpt - grp, -1, -grp):
            cs = list(range(c0 + grp - 1, c0 - 1, -1))
            inst = [(c, r) for c in cs for r in range(SSD_R)]
            n = len(inst)
            rows = {c: slice(c * Q, (c + 1) * Q) for c in cs}
            bc_ = {c: b_ref[rows[c], :] for c in cs}
            cc_ = {c: c_ref[rows[c], :] for c in cs}
            cb = {c: _dot_nt(cc_[c], bc_[c]) for c in cs}
            xr = [xs_ref[rows[c], ps[r]] for c, r in inst]
            dyr = [dy_ref[rows[c], ps[r]] for c, r in inst]
            acr = [acs_s[r, c:c + 1, :] for c, r in inst]
            dtr = [dt_ref[r, c:c + 1, :] for c, r in inst]
            acc = [_col_bcast(a, Q) for a in acr]
            dtb = [_col_bcast(d, Q) for d in dtr]
            al = [a[:, Q - 1:Q] for a in acr]
            e_c = [jnp.exp(a) for a in acc]
            dl_c = [jnp.exp(al[i] - acc[i]) for i in range(n)]
            e_r = [jnp.exp(a) for a in acr]
            dl_r = [jnp.exp(al[i] - acr[i]) for i in range(n)]
            gl = [jnp.exp(a) for a in al]
            bct = {c: bc_[c].T for c in cs}
            cct = {c: cc_[c].T for c in cs}
            cbt = {c: _dot_nt(bc_[c], cc_[c]) for c in cs}
            lm = [jnp.exp(jnp.where(tri, acc[i] - acr[i], -1e30)) for i in range(n)]
            lmt = [jnp.exp(jnp.where(r_i <= c_i, acr[i] - acc[i], -1e30)) for i in range(n)]
            mmt = [cbt[c] * lmt[i] for i, (c, r) in enumerate(inst)]
            sr = [sall_ref[0, c, :, ps[r]] for c, r in inst]
            dmm0 = [_dot_nt(dyr[i], xr[i]) for i in range(n)]
            dmm0t = [_dot_nt(xr[i], dyr[i]) for i in range(n)]
            dxd1 = [_dot(mmt[i], dyr[i]) for i in range(n)]
            dce = [_dot_nt(dyr[i], sr[i]) for i in range(n)]
            dcet = [_dot_nt(sr[i], dyr[i]) for i in range(n)]
            cdy = [_dot(cct[c] * e_r[i], dyr[i]) for i, (c, r) in enumerate(inst)]
            dsn = []
            for i, (c, r) in enumerate(inst):
                dsn.append(ds_cur[r])
                ds_cur[r] = gl[i] * ds_cur[r] + cdy[i]
            dxd = [dxd1[i] + _dot(bc_[c] * dl_c[i], dsn[i]) for i, (c, r) in enumerate(inst)]
            dbd0 = [_dot_nt(xr[i], dsn[i]) for i in range(n)]
            dbd0t = [_dot_nt(dsn[i], xr[i]) for i in range(n)]
            dcb = {c: jnp.zeros((Q, Q), F32) for c in cs}
            dcbt = {c: jnp.zeros((Q, Q), F32) for c in cs}
            db_acc = {c: jnp.zeros((Q, SSD_N), F32) for c in cs}
            dc_acc = {c: jnp.zeros((Q, SSD_N), F32) for c in cs}
            for i, (c, r) in enumerate(inst):
                dgl = tsum(dsn[i] * sr[i])
                dc_acc[c] = dc_acc[c] + dce[i] * e_c[i]
                db_acc[c] = db_acc[c] + dbd0[i] * (dtb[i] * dl_c[i])
                dl0 = dmm0[i] * lm[i]
                dl0t = dmm0t[i] * (lmt[i] * dtb[i])
                dcb[c] = dcb[c] + dl0 * dtr[i]
                dcbt[c] = dcbt[c] + dl0t
                csum_gm0 = csum(dl0 * cb[c])
                rsum_gm = csum(dl0t * cbt[c])
                r_de = csum(dcet[i] * cct[c]) * e_r[i]
                r_dl = csum(dbd0t[i] * bct[c]) * dl_r[i]
                dalast = jnp.sum(r_dl * dtr[i], axis=1, keepdims=True) + dgl * gl[i]
                dacs_s[r, c:c + 1, :] = (rsum_gm + r_de - (r_dl + csum_gm0) * dtr[i]
                                         + jnp.where(lane == Q - 1, dalast, 0.0))
                ddt_s[r, c:c + 1, :] = csum_gm0 + r_dl
                dd_s[r, c:c + 1, :] = _dot_nt(ones8, dyr[i] * xr[i])[0:1]
                dxs_ref[rows[c], ps[r]] = dxd[i] * dtb[i][:, :SSD_P] + dsk_ref[SSD_R * gi + r] * dyr[i]
            for c in cs:
                dc_ref[rows[c], :] = dc_acc[c] + _dot(dcb[c], bc_[c])
                db_ref[rows[c], :] = db_acc[c] + _dot(dcbt[c], cc_[c])
        for r in range(SSD_R):
            dS[:, ps[r]] = ds_cur[r]
        for r in range(SSD_R):
            da_ref[r] = _dot_hi(dacs_s[r], lower)
            ddt_ref[r] = ddt_s[r]
            dd_ref[r] = dd_s[r]

    rev = lambda i: nT - 1 - i
    smem = pl.BlockSpec(memory_space=pltpu.SMEM)
    rows_spec = pl.BlockSpec((SSD_R, cpt, Q), lambda g, i: (g, rev(i), 0))
    x_spec = pl.BlockSpec((tt, GW), lambda g, i: (rev(i), g))
    n_spec = pl.BlockSpec((tt, SSD_N), lambda g, i: (rev(i), g))
    rows_shape = jax.ShapeDtypeStruct((SSD_H, nC, Q), F32)
    return pl.pallas_call(
        body, grid=(SSD_G, nT),
        in_specs=[smem, smem, x_spec, n_spec, n_spec, rows_spec,
                  pl.BlockSpec((1, cpt, SSD_N, GW), lambda g, i: (g, rev(i), 0, 0)), x_spec],
        out_specs=[x_spec, n_spec, n_spec, rows_spec, rows_spec, rows_spec],
        out_shape=[jax.ShapeDtypeStruct((T, D_INNER), F32), jax.ShapeDtypeStruct((T, SSD_G * SSD_N), F32),
                   jax.ShapeDtypeStruct((T, SSD_G * SSD_N), F32), rows_shape, rows_shape, rows_shape],
        scratch_shapes=[pltpu.VMEM((SSD_N, GW), F32)] + [pltpu.VMEM((SSD_R, cpt, Q), F32)] * 4,
        compiler_params=_params("parallel", "arbitrary"), name=name,
    )(alog, dskip, xs, bm, cm, dt, sall, dy)


def _ssd_gate_bwd(dtraw, dt, da, ddt_direct, ddrow, alog, dtb, name):
    H, T = dtraw.shape

    def body(raw_ref, dt_ref, da_ref, ddt_ref, dd_ref, alog_ref, dtb_ref, draw_ref, dalog_ref, ddtb_ref, dD_ref):
        a = -jnp.exp(alog_ref[...])
        dav = da_ref[...]
        ddt = ddt_ref[...] + dav * a
        draw = ddt * _sigmoid(raw_ref[...] + dtb_ref[...])
        draw_ref[...] = draw
        dalog_ref[...] = jnp.sum(dav * dt_ref[...], axis=1, keepdims=True) * a
        ddtb_ref[...] = jnp.sum(draw, axis=1, keepdims=True)
        dD_ref[...] = jnp.sum(dd_ref[...], axis=1, keepdims=True)

    return pl.pallas_call(
        body,
        out_shape=[jax.ShapeDtypeStruct((H, T), F32)] + [jax.ShapeDtypeStruct((H, 1), F32)] * 3,
        compiler_params=pltpu.CompilerParams(vmem_limit_bytes=VMEM_LIMIT_BYTES), name=name,
    )(dtraw, dt, da, ddt_direct, ddrow, alog, dtb)


def _final_loss(x, fw, tgt, name):
    T = x.shape[0]
    tt = min(T, 512)
    nT = T // tt

    def body(x_ref, w_ref, t_ref, dx_ref, dw_ref, loss_ref, acc):
        i = pl.program_id(0)

        @pl.when(i == 0)
        def _():
            dw_ref[...] = jnp.zeros_like(dw_ref)
            acc[...] = jnp.zeros_like(acc)

        xv = x_ref[...]
        r = lax.rsqrt(jnp.mean(xv * xv, axis=-1, keepdims=True) + EPS)
        xh = xv * r
        err = xh * w_ref[...] - t_ref[...]
        acc[...] += jnp.sum(err * err, axis=0, keepdims=True)
        dout = err * (1.0 / D_MODEL)
        dw_ref[...] += jnp.sum(dout * xh, axis=0, keepdims=True)
        dxn = dout * w_ref[...]
        dx_ref[...] = r * (dxn - xh * jnp.mean(dxn * xh, axis=-1, keepdims=True))

        @pl.when(i == nT - 1)
        def _():
            loss_ref[...] = (0.5 / D_MODEL) * jnp.sum(acc[...], axis=1, keepdims=True)

    row = pl.BlockSpec((tt, D_MODEL), lambda i: (i, 0))
    vec = pl.BlockSpec((1, D_MODEL), lambda i: (0, 0))
    return pl.pallas_call(
        body, grid=(nT,),
        in_specs=[row, vec, row],
        out_specs=[row, vec, pl.BlockSpec((1, 1), lambda i: (0, 0))],
        out_shape=[jax.ShapeDtypeStruct((T, D_MODEL), F32), jax.ShapeDtypeStruct((1, D_MODEL), F32),
                   jax.ShapeDtypeStruct((1, 1), F32)],
        scratch_shapes=[pltpu.VMEM((1, D_MODEL), F32)],
        compiler_params=_params("arbitrary"), name=name,
    )(x, fw, tgt)


def _adamw(parts, w, m, v, name):
    R, C = w.shape
    tr = 128 if R % 128 == 0 else R

    def body(p_ref, w_ref, m_ref, v_ref, g_ref, d_ref, nm_ref, nv_ref):
        g = p_ref[0].astype(F32)
        for s in range(1, N_DEV):
            g = g + p_ref[s].astype(F32)
        mn = ADAM_B1 * m_ref[...] + (1.0 - ADAM_B1) * g
        vn = ADAM_B2 * v_ref[...] + (1.0 - ADAM_B2) * (g * g)
        mh = mn / (1.0 - ADAM_B1 ** ADAM_STEP)
        vh = vn / (1.0 - ADAM_B2 ** ADAM_STEP)
        g_ref[...] = g
        d_ref[...] = -ADAM_LR * (mh / (jnp.sqrt(vh) + ADAM_EPS) + ADAM_WD * w_ref[...])
        nm_ref[...] = mn
        nv_ref[...] = vn

    blk = pl.BlockSpec((tr, C), lambda i: (i, 0))
    return pl.pallas_call(
        body, grid=(R // tr,),
        in_specs=[pl.BlockSpec((N_DEV, tr, C), lambda i: (0, i, 0)), blk, blk, blk],
        out_specs=[blk] * 4,
        out_shape=[jax.ShapeDtypeStruct((R, C), F32)] * 4,
        compiler_params=_params("parallel"), name=name,
    )(parts, w, m, v)


def _me():
    x, y, c = lax.axis_index("x"), lax.axis_index("y"), lax.axis_index("c")
    return x, y, c


def _peer(d):
    x, y, c = _me()
    px = 1 - x if d & 4 else x
    py = 1 - y if d & 2 else y
    pc = 1 - c if d & 1 else c
    return (px, py, pc), 4 * px + 2 * py + pc


def _exchange(arrs, bcast, name):
    n = len(arrs)

    def body(*refs):
        ex = _Exchange(refs[:n], refs[n:2 * n], bcast, *refs[2 * n:])
        ex.begin()
        ex.finish()

    anyspec = pl.BlockSpec(memory_space=pl.ANY)
    return pl.pallas_call(
        body,
        in_specs=[anyspec] * n, out_specs=[anyspec] * n,
        out_shape=_exchange_out_shapes(arrs, bcast),
        scratch_shapes=_exchange_semaphores(n),
        name=name,
    )(*arrs)


def _exchange_out_shapes(arrs, bcast):
    return [jax.ShapeDtypeStruct((N_DEV,) + (a.shape if b else a.shape[1:]), a.dtype) for a, b in zip(arrs, bcast)]


def _exchange_semaphores(n):
    return [pltpu.SemaphoreType.DMA((n, N_DEV - 1)), pltpu.SemaphoreType.DMA((n, N_DEV - 1)),
            pltpu.SemaphoreType.DMA((n,))]


class _Exchange:
    def __init__(self, ins, outs, bcast, ssem, rsem, lsem):
        n = len(ins)
        x, y, c = _me()
        me = 4 * x + 2 * y + c

        def src(a, dest):
            return ins[a] if bcast[a] else ins[a].at[dest]

        self.local = [pltpu.make_async_copy(src(a, me), outs[a].at[me], lsem.at[a]) for a in range(n)]
        self.sends, self.recvs = [], []
        for a in range(n):
            for d in range(1, N_DEV):
                peer, pid = _peer(d)
                self.sends.append(pltpu.make_async_remote_copy(
                    src_ref=src(a, pid), dst_ref=outs[a].at[me], send_sem=ssem.at[a, d - 1],
                    recv_sem=rsem.at[a, d - 1], device_id=peer, device_id_type=MESH))
                self.recvs.append(pltpu.make_async_remote_copy(
                    src_ref=src(a, pid), dst_ref=outs[a].at[pid], send_sem=ssem.at[a, d - 1],
                    recv_sem=rsem.at[a, d - 1], device_id=peer, device_id_type=MESH))

    def begin(self):
        for cp in self.local + self.sends:
            cp.start()

    def finish(self):
        for cp in self.recvs:
            cp.wait_recv()
        for cp in self.sends:
            cp.wait_send()
        for cp in self.local:
            cp.wait()


def _to_rows(cols, chunk):
    T, H = cols.shape
    return cols.T.reshape(H, T // chunk, chunk)


def _from_rows(rows):
    return rows.T


def _pad_cols(a, width):
    return jnp.pad(a, ((0, 0), (0, width - a.shape[1])))


def _local_step(x, tgt, p, late_weights=None, early_grads=None):
    T = x.shape[0]
    zb = lambda n: jnp.zeros((1, n), F32)
    gw = p["gdn_w_in"]
    g_wparts = [gw[:, 0:1024], gw[:, 1024:2048], gw[:, 2048:4096], gw[:, 4096:6144], _pad_cols(gw[:, 6144:6176], PAD_W)]
    nw0, nw1 = p["norm_w"][0:1], p["norm_w"][1:2]
    if late_weights is None:
        h0, (q_pre, k_pre, v_pre, z0, ab) = _norm_inproj(x, nw0, g_wparts, "gdn_inproj")
    else:
        comm, assemble = late_weights
        h0, (q_pre, k_pre, v_pre, z0, ab), gathered = _norm_inproj(x, nw0, g_wparts, "gdn_inproj", comm)
        p = dict(p, **assemble(gathered))
    gcw = p["gdn_conv_w"]
    cw_q, cw_k, cw_v = gcw[:, 0:1024], gcw[:, 1024:2048], gcw[:, 2048:4096]
    q = _conv_fwd(q_pre, cw_q, zb(1024), True, GDN_DK ** -0.5, "gdn_conv_q")
    k = _conv_fwd(k_pre, cw_k, zb(1024), True, 1.0, "gdn_conv_k")
    v = _conv_fwd(v_pre, cw_v, zb(2048), False, 1.0, "gdn_conv_v")
    braw = _to_rows(ab[:, 0:GDN_HV], GDN_CHUNK)
    araw = _to_rows(ab[:, GDN_HV:2 * GDN_HV], GDN_CHUNK)
    g_alog, g_dtb = p["gdn_a_log"].reshape(-1), p["gdn_dt_bias"].reshape(-1)
    g_u, g_w, g_pm, g_ti, g_rows, beta_rows, gc_rows = _gdn_prep(q, k, v, araw, braw, g_alog, g_dtb, "gdn_prep")
    o0, g_vn, g_sall = _gdn_state_fwd(q, k, g_u, g_w, g_pm, gc_rows, "gdn_state_fwd")
    x1 = _out_fwd(o0, z0, p["gdn_norm_w"], p["gdn_w_out"], x, GDN_DK, False, "gdn_out")
    sw = p["ssd_w_in"]
    s_wparts = [sw[:, 0:2048], sw[:, 2048:4096], sw[:, 4096:5120], sw[:, 5120:6144], _pad_cols(sw[:, 6144:6176], PAD_W)]
    h1, (z1, xs_pre, b_pre, c_pre, dtp) = _norm_inproj(x1, nw1, s_wparts, "ssd_inproj")
    scw, scb = p["ssd_conv_w"], p["ssd_conv_b"]
    xs = _conv_fwd(xs_pre, scw[:, 0:2048], scb[:, 0:2048], False, 1.0, "ssd_conv_x")
    bm = _conv_fwd(b_pre, scw[:, 2048:3072], scb[:, 2048:3072], False, 1.0, "ssd_conv_b")
    cm = _conv_fwd(c_pre, scw[:, 3072:4096], scb[:, 3072:4096], False, 1.0, "ssd_conv_c")
    dtraw = _to_rows(dtp[:, 0:SSD_H], SSD_CHUNK)
    s_alog, s_dtb, s_d = p["ssd_a_log"].reshape(-1), p["ssd_dt_bias"].reshape(-1), p["ssd_d"].reshape(-1)
    y1, s_sall, dt_rows = _ssd_scan_fwd(xs, bm, cm, dtraw, s_alog, s_dtb, s_d, "ssd_scan_fwd")
    x2 = _out_fwd(y1, z1, p["ssd_norm_w"], p["ssd_w_out"], x1, D_INNER // SSD_G, True, "ssd_out")
    dx2, d_fw, loss = _final_loss(x2, p["final_norm_w"].reshape(1, -1), tgt, "final_loss")
    dy1, dz1, d_snw, yn1 = _out_bwd(dx2, y1, z1, p["ssd_norm_w"], p["ssd_w_out"], D_INNER // SSD_G, True, "ssd_out_bwd")
    d_swout = _matmul_tn(yn1, dx2, "ssd_wout_grad")
    dxs, dbm, dcm, da_rows, ddt_rows, dd_rows = _ssd_scan_bwd(xs, bm, cm, dt_rows, s_sall, dy1, s_alog, s_d, "ssd_scan_bwd")
    col = lambda a: a.reshape(-1, 1)
    dtraw_g, d_salog, d_sdtb, d_sd = _ssd_gate_bwd(
        dtraw.reshape(SSD_H, T), dt_rows.reshape(SSD_H, T), da_rows.reshape(SSD_H, T),
        ddt_rows.reshape(SSD_H, T), dd_rows.reshape(SSD_H, T), col(s_alog), col(s_dtb), "ssd_gate_bwd")
    dxs_pre, dcw_x, dcb_x = _conv_bwd(xs_pre, scw[:, 0:2048], scb[:, 0:2048], dxs, False, 1.0, "ssd_conv_x_bwd")
    db_pre, dcw_b, dcb_b = _conv_bwd(b_pre, scw[:, 2048:3072], scb[:, 2048:3072], dbm, False, 1.0, "ssd_conv_b_bwd")
    dc_pre, dcw_c, dcb_c = _conv_bwd(c_pre, scw[:, 3072:4096], scb[:, 3072:4096], dcm, False, 1.0, "ssd_conv_c_bwd")
    ddtp = _pad_cols(_from_rows(dtraw_g), PAD_W)
    s_dparts = [dz1, dxs_pre, db_pre, dc_pre, ddtp]
    dx1, d_nw1 = _inproj_bwd(x1, nw1, s_dparts, s_wparts, dx2, "ssd_inproj_bwd")
    s_dw = [_matmul_tn(h1, d, "ssd_win_grad_%d" % n) for n, d in enumerate(s_dparts)]
    d_swin = jnp.concatenate(s_dw[:4] + [s_dw[4][:, 0:SSD_H]], axis=1)
    early_recv = None
    if early_grads is None:
        do0, dz0, d_gnw, yn0 = _out_bwd(dx1, o0, z0, p["gdn_norm_w"], p["gdn_w_out"], GDN_DK, False, "gdn_out_bwd")
    else:
        do0, dz0, d_gnw, yn0, early_recv = _out_bwd(dx1, o0, z0, p["gdn_norm_w"], p["gdn_w_out"], GDN_DK, False,
                                                    "gdn_out_bwd", early_grads(d_swin, d_swout))
    d_gwout = _matmul_tn(yn0, dx1, "gdn_wout_grad")
    g_dvn, g_dkd, g_dgl = _gdn_state_bwd(q, k, g_w, g_pm, g_vn, g_sall, gc_rows, do0, "gdn_state_bwd")
    dq, dk, dv, dg_rows, dbeta_rows = _gdn_local_bwd(q, k, v, gc_rows, beta_rows, g_ti, g_u, g_w, g_pm, g_vn, g_sall,
                                                     do0, g_dvn, g_dkd, g_dgl, "gdn_local_bwd")
    da_g, db_g, d_galog, d_gdtb = _gdn_gate_bwd(
        araw.reshape(GDN_HV, T), braw.reshape(GDN_HV, T), dg_rows.reshape(GDN_HV, T),
        dbeta_rows.reshape(GDN_HV, T), col(g_alog), col(g_dtb), "gdn_gate_bwd")
    dq_pre, dcw_q, _ = _conv_bwd(q_pre, cw_q, zb(1024), dq, True, GDN_DK ** -0.5, "gdn_conv_q_bwd")
    dk_pre, dcw_k, _ = _conv_bwd(k_pre, cw_k, zb(1024), dk, True, 1.0, "gdn_conv_k_bwd")
    dv_pre, dcw_v, _ = _conv_bwd(v_pre, cw_v, zb(2048), dv, False, 1.0, "gdn_conv_v_bwd")
    dab = _pad_cols(jnp.concatenate([_from_rows(db_g), _from_rows(da_g)], axis=1), PAD_W)
    g_dparts = [dq_pre, dk_pre, dv_pre, dz0, dab]
    dx0, d_nw0 = _inproj_bwd(x, nw0, g_dparts, g_wparts, dx1, "gdn_inproj_bwd")
    g_dw = [_matmul_tn(h0, d, "gdn_win_grad_%d" % n) for n, d in enumerate(g_dparts)]
    d_gwin = jnp.concatenate(g_dw[:4] + [g_dw[4][:, 0:2 * GDN_HV]], axis=1)
    grads = {
        "norm_w": jnp.concatenate([d_nw0, d_nw1], axis=0),
        "gdn_w_in": d_gwin,
        "gdn_conv_w": jnp.concatenate([dcw_q, dcw_k, dcw_v], axis=1),
        "gdn_a_log": d_galog.reshape(1, -1),
        "gdn_dt_bias": d_gdtb.reshape(1, -1),
        "gdn_norm_w": d_gnw,
        "gdn_w_out": d_gwout,
        "ssd_w_in": d_swin,
        "ssd_conv_w": jnp.concatenate([dcw_x, dcw_b, dcw_c], axis=1),
        "ssd_conv_b": jnp.concatenate([dcb_x, dcb_b, dcb_c], axis=1),
        "ssd_dt_bias": d_sdtb.reshape(1, -1),
        "ssd_a_log": d_salog.reshape(1, -1),
        "ssd_d": d_sd.reshape(1, -1),
        "ssd_norm_w": d_snw,
        "ssd_w_out": d_swout,
        "final_norm_w": d_fw,
    }
    if early_grads is not None:
        return loss, dx0, grads, early_recv
    return loss, dx0, grads


WEIGHTS = ["norm_w", "gdn_w_in", "gdn_conv_w", "gdn_a_log", "gdn_dt_bias", "gdn_norm_w", "gdn_w_out", "ssd_w_in",
           "ssd_conv_w", "ssd_conv_b", "ssd_dt_bias", "ssd_a_log", "ssd_d", "ssd_norm_w", "ssd_w_out", "final_norm_w"]
COL_SHARDED = ["gdn_w_in", "ssd_w_in"]
ROW_SHARDED = ["gdn_w_out", "ssd_w_out"]
SMALL_SHARDED = ["gdn_conv_w", "ssd_conv_w", "ssd_conv_b", "ssd_norm_w"]
REPLICATED = ["norm_w", "gdn_a_log", "gdn_dt_bias", "gdn_norm_w", "ssd_dt_bias", "ssd_a_log", "ssd_d", "final_norm_w"]


def _pack(arrs):
    return jnp.concatenate([a.reshape(-1) for a in arrs]).reshape(1, -1)


def _unpack(flat, shapes):
    out, pos = [], 0
    for s in shapes:
        n = 1
        for dim in s:
            n *= dim
        out.append(flat[pos:pos + n].reshape(s))
        pos += n
    return out


def _cols_to_shards(full):
    R, C = full.shape
    return full.reshape(R, N_DEV, C // N_DEV).transpose(1, 0, 2)


def _shards_to_cols(shards):
    n, R, c = shards.shape
    return shards.transpose(1, 0, 2).reshape(R, n * c)


def kernel(x, norm_w, gdn_w_in, gdn_conv_w, gdn_a_log, gdn_dt_bias, gdn_norm_w, gdn_w_out, ssd_w_in, ssd_conv_w, ssd_conv_b, ssd_dt_bias, ssd_a_log, ssd_d, ssd_norm_w, ssd_w_out, final_norm_w, loss_target, m_norm_w, m_gdn_w_in, m_gdn_conv_w, m_gdn_a_log, m_gdn_dt_bias, m_gdn_norm_w, m_gdn_w_out, m_ssd_w_in, m_ssd_conv_w, m_ssd_conv_b, m_ssd_dt_bias, m_ssd_a_log, m_ssd_d, m_ssd_norm_w, m_ssd_w_out, m_final_norm_w, v_norm_w, v_gdn_w_in, v_gdn_conv_w, v_gdn_a_log, v_gdn_dt_bias, v_gdn_norm_w, v_gdn_w_out, v_ssd_w_in, v_ssd_conv_w, v_ssd_conv_b, v_ssd_dt_bias, v_ssd_a_log, v_ssd_d, v_ssd_norm_w, v_ssd_w_out, v_final_norm_w):
    w = dict(norm_w=norm_w, gdn_w_in=gdn_w_in[0], gdn_conv_w=gdn_conv_w[0], gdn_a_log=gdn_a_log,
             gdn_dt_bias=gdn_dt_bias, gdn_norm_w=gdn_norm_w, gdn_w_out=gdn_w_out[0], ssd_w_in=ssd_w_in[0],
             ssd_conv_w=ssd_conv_w[0], ssd_conv_b=ssd_conv_b, ssd_dt_bias=ssd_dt_bias, ssd_a_log=ssd_a_log,
             ssd_d=ssd_d, ssd_norm_w=ssd_norm_w, ssd_w_out=ssd_w_out[0], final_norm_w=final_norm_w.reshape(1, -1))
    m = dict(norm_w=m_norm_w, gdn_w_in=m_gdn_w_in[0], gdn_conv_w=m_gdn_conv_w[0], gdn_a_log=m_gdn_a_log,
             gdn_dt_bias=m_gdn_dt_bias, gdn_norm_w=m_gdn_norm_w, gdn_w_out=m_gdn_w_out[0], ssd_w_in=m_ssd_w_in[0],
             ssd_conv_w=m_ssd_conv_w[0], ssd_conv_b=m_ssd_conv_b, ssd_dt_bias=m_ssd_dt_bias, ssd_a_log=m_ssd_a_log,
             ssd_d=m_ssd_d, ssd_norm_w=m_ssd_norm_w, ssd_w_out=m_ssd_w_out[0], final_norm_w=m_final_norm_w.reshape(1, -1))
    v = dict(norm_w=v_norm_w, gdn_w_in=v_gdn_w_in[0], gdn_conv_w=v_gdn_conv_w[0], gdn_a_log=v_gdn_a_log,
             gdn_dt_bias=v_gdn_dt_bias, gdn_norm_w=v_gdn_norm_w, gdn_w_out=v_gdn_w_out[0], ssd_w_in=v_ssd_w_in[0],
             ssd_conv_w=v_ssd_conv_w[0], ssd_conv_b=v_ssd_conv_b, ssd_dt_bias=v_ssd_dt_bias, ssd_a_log=v_ssd_a_log,
             ssd_d=v_ssd_d, ssd_norm_w=v_ssd_norm_w, ssd_w_out=v_ssd_w_out[0], final_norm_w=v_final_norm_w.reshape(1, -1))
    out_shapes = {n: a.shape for n, a in zip(
        WEIGHTS, [norm_w, gdn_w_in, gdn_conv_w, gdn_a_log, gdn_dt_bias, gdn_norm_w, gdn_w_out, ssd_w_in, ssd_conv_w,
                  ssd_conv_b, ssd_dt_bias, ssd_a_log, ssd_d, ssd_norm_w, ssd_w_out, final_norm_w])}

    small_shapes = [w[n].shape for n in SMALL_SHARDED]
    first = _exchange([_mx(w["gdn_w_in"]), _pack([w[n] for n in SMALL_SHARDED])], [True] * 2, "gather_first")
    full = dict(w)
    full["gdn_w_in"] = _shards_to_cols(first[0])
    small_all = [_unpack(first[1][s, 0], small_shapes) for s in range(N_DEV)]
    for idx, n in enumerate(SMALL_SHARDED):
        full[n] = jnp.concatenate([small_all[s][idx] for s in range(N_DEV)], axis=-1)
    late = ["gdn_w_out", "ssd_w_in", "ssd_w_out"]

    def assemble(gathered):
        return {"gdn_w_out": gathered[0].reshape(-1, D_MODEL), "ssd_w_in": _shards_to_cols(gathered[1]),
                "ssd_w_out": gathered[2].reshape(-1, D_MODEL)}

    def early_grads(d_ssd_w_in, d_ssd_w_out):
        return ([_cols_to_shards(d_ssd_w_in).astype(GRAD_WIRE_DTYPE),
                 d_ssd_w_out.reshape(N_DEV, -1, D_MODEL).astype(GRAD_WIRE_DTYPE)], [False] * 2)

    loss, dx, grads, ssd_recv = _local_step(x[0], loss_target[0], full,
                                            (([_mx(w[n]) for n in late], [True] * 3), assemble), early_grads)

    send_small = jnp.concatenate(
        [_cols_to_shards(grads[n]).reshape(N_DEV, -1) for n in SMALL_SHARDED], axis=1)[:, None, :]
    rep_shapes = [w[n].shape for n in REPLICATED]
    recv = _exchange(
        [_cols_to_shards(grads["gdn_w_in"]).astype(GRAD_WIRE_DTYPE),
         grads["gdn_w_out"].reshape(N_DEV, -1, D_MODEL).astype(GRAD_WIRE_DTYPE),
         send_small, _pack([grads[n] for n in REPLICATED])],
        [False] * 3 + [True], "exchange_grads")

    res = {}
    for n, parts in zip(["gdn_w_in", "gdn_w_out", "ssd_w_in", "ssd_w_out"], [recv[0], recv[1]] + list(ssd_recv)):
        res[n] = _adamw(parts, w[n], m[n], v[n], "adamw_" + n)
    small_res = _adamw(recv[2], *[_pack([t[n] for n in SMALL_SHARDED]) for t in (w, m, v)], "adamw_small")
    rep_res = _adamw(recv[3], *[_pack([t[n] for n in REPLICATED]) for t in (w, m, v)], "adamw_replicated")
    for k4 in range(4):
        for n, a in zip(SMALL_SHARDED, _unpack(small_res[k4][0], small_shapes)):
            res.setdefault(n, [None] * 4)[k4] = a
        for n, a in zip(REPLICATED, _unpack(rep_res[k4][0], rep_shapes)):
            res.setdefault(n, [None] * 4)[k4] = a

    loss = lax.psum(loss[0, 0], ("x", "y", "c"))
    outs = [loss, dx[None]]
    for k4 in range(4):
        outs += [res[n][k4].reshape(out_shapes[n]) for n in WEIGHTS]
    return tuple(outs)
```

```python
import jax
import jax.numpy as jnp
from jax import lax
from jax.experimental import pallas as pl
from jax.experimental.pallas import tpu as pltpu

F32 = jnp.float32
MXU_DTYPE = jnp.bfloat16
GRAD_WIRE_DTYPE = jnp.bfloat16
HI = lax.Precision.HIGHEST
EPS = 1e-6
VMEM_LIMIT_BYTES = 56 * 1024 * 1024
N_DEV = 8
MESH = pl.DeviceIdType.MESH

D_MODEL = 1024
CONV_K = 4
GDN_HV = 16
GDN_DK = 128
GDN_CHUNK = 64
SSD_H = 32
SSD_P = 64
SSD_N = 128
SSD_G = 8
SSD_R = SSD_H // SSD_G
SSD_CHUNK = 128
D_INNER = 2048
PAD_W = 128

ADAM_LR = 0.001
ADAM_B1 = 0.9
ADAM_B2 = 0.999
ADAM_EPS = 1e-08
ADAM_WD = 0.01
ADAM_STEP = 10


def _params(*sem):
    return pltpu.CompilerParams(dimension_semantics=sem, vmem_limit_bytes=VMEM_LIMIT_BYTES)


def _mx(a):
    return a.astype(MXU_DTYPE)


def _dot(a, b):
    return jnp.dot(_mx(a), _mx(b), preferred_element_type=F32)


def _dot_nt(a, b):
    return lax.dot_general(_mx(a), _mx(b), (((1,), (1,)), ((), ())), preferred_element_type=F32)


def _dot_tn(a, b):
    return lax.dot_general(_mx(a), _mx(b), (((0,), (0,)), ((), ())), preferred_element_type=F32)


def _dot_hi(a, b):
    return jnp.dot(a, b, precision=HI, preferred_element_type=F32)


def _sigmoid(x):
    return 1.0 / (1.0 + jnp.exp(-x))


def _silu(x):
    return x * _sigmoid(x)


def _dsilu(x):
    s = _sigmoid(x)
    return s * (1.0 + x * (1.0 - s))


def _softplus(x):
    return jnp.maximum(x, 0.0) + jnp.log1p(jnp.exp(-jnp.abs(x)))


def _col(r, eye):
    return jnp.sum(jnp.where(eye, r, 0.0), axis=1, keepdims=True)


def _row(c, eye):
    return jnp.sum(jnp.where(eye, c, 0.0), axis=0, keepdims=True)


def _col_bcast(r, n):
    return jnp.broadcast_to(r, (n, n)).T


def _masks(n):
    r = lax.broadcasted_iota(jnp.int32, (n, n), 0)
    c = lax.broadcasted_iota(jnp.int32, (n, n), 1)
    return r >= c, r > c, r == c, r, c


def _with_exchange(comm):
    arrs, bcast = comm if comm else ([], [])
    nc = len(arrs)
    anyspec = pl.BlockSpec(memory_space=pl.ANY)

    def wrap(compute, n_in, n_out):
        def body(*refs):
            cin, cout = refs[n_in:n_in + nc], refs[n_in + nc + n_out:n_in + 2 * nc + n_out]
            sems = refs[n_in + 2 * nc + n_out:n_in + 2 * nc + n_out + 3]
            rest = refs[:n_in] + refs[n_in + nc:n_in + nc + n_out] + refs[n_in + 2 * nc + n_out + (3 if nc else 0):]
            if nc:
                @pl.when(pl.program_id(0) == 0)
                def _():
                    _Exchange(cin, cout, bcast, *sems).begin()
            compute(*rest)
            if nc:
                @pl.when(pl.program_id(0) == pl.num_programs(0) - 1)
                def _():
                    _Exchange(cin, cout, bcast, *sems).finish()
        return body

    return dict(arrs=list(arrs), nc=nc, wrap=wrap, in_specs=[anyspec] * nc, out_specs=[anyspec] * nc,
                out_shape=_exchange_out_shapes(arrs, bcast), scratch=_exchange_semaphores(nc) if nc else [])


def _norm_inproj(x, nw, wparts, name, comm=None):
    T = x.shape[0]
    tt = min(T, 256)
    n = len(wparts)
    ex = _with_exchange(comm)

    def compute(x_ref, nw_ref, *refs):
        w_refs, h_ref, o_refs = refs[:n], refs[n], refs[n + 1:]
        xv = x_ref[...]
        r = lax.rsqrt(jnp.mean(xv * xv, axis=-1, keepdims=True) + EPS)
        h = _mx(xv * r * nw_ref[...])
        h_ref[...] = h
        for w_ref, o_ref in zip(w_refs, o_refs):
            o_ref[...] = jnp.dot(h, w_ref[...], preferred_element_type=F32)

    row = lambda width: pl.BlockSpec((tt, width), lambda i: (i, 0))
    full = lambda a: pl.BlockSpec(a.shape, lambda i: (0, 0))
    outs = pl.pallas_call(
        ex["wrap"](compute, 2 + n, 1 + n), grid=(T // tt,),
        in_specs=[row(D_MODEL), full(nw)] + [full(w) for w in wparts] + ex["in_specs"],
        out_specs=[row(D_MODEL)] + [row(w.shape[1]) for w in wparts] + ex["out_specs"],
        out_shape=[jax.ShapeDtypeStruct((T, D_MODEL), MXU_DTYPE)]
        + [jax.ShapeDtypeStruct((T, w.shape[1]), F32) for w in wparts] + ex["out_shape"],
        scratch_shapes=ex["scratch"],
        compiler_params=_params("arbitrary" if comm else "parallel"), name=name,
    )(x, nw, *wparts, *ex["arrs"])
    if comm:
        return outs[0], outs[1:1 + n], outs[1 + n:]
    return outs[0], outs[1:]


def _inproj_bwd(x, nw, dparts, wparts, dres, name):
    T = x.shape[0]
    tt = min(T, 256)
    n = len(wparts)

    def body(x_ref, nw_ref, dres_ref, *refs):
        d_refs, w_refs, dx_ref, dnw_ref = refs[:n], refs[n:2 * n], refs[2 * n], refs[2 * n + 1]

        @pl.when(pl.program_id(0) == 0)
        def _():
            dnw_ref[...] = jnp.zeros_like(dnw_ref)

        dh = _dot_nt(d_refs[0][...], w_refs[0][...])
        for d_ref, w_ref in zip(d_refs[1:], w_refs[1:]):
            dh = dh + _dot_nt(d_ref[...], w_ref[...])
        xv = x_ref[...]
        r = lax.rsqrt(jnp.mean(xv * xv, axis=-1, keepdims=True) + EPS)
        xh = xv * r
        dnw_ref[...] += jnp.sum(dh * xh, axis=0, keepdims=True)
        dxn = dh * nw_ref[...]
        dx_ref[...] = dres_ref[...] + r * (dxn - xh * jnp.mean(dxn * xh, axis=-1, keepdims=True))

    row = lambda width: pl.BlockSpec((tt, width), lambda i: (i, 0))
    full = lambda a: pl.BlockSpec(a.shape, lambda i: (0, 0))
    return pl.pallas_call(
        body, grid=(T // tt,),
        in_specs=[row(D_MODEL), full(nw), row(D_MODEL)] + [row(d.shape[1]) for d in dparts]
        + [full(w) for w in wparts],
        out_specs=[row(D_MODEL), pl.BlockSpec((1, D_MODEL), lambda i: (0, 0))],
        out_shape=[jax.ShapeDtypeStruct((T, D_MODEL), F32), jax.ShapeDtypeStruct((1, D_MODEL), F32)],
        compiler_params=_params("arbitrary"), name=name,
    )(x, nw, dres, *dparts, *wparts)


def _matmul_tn(a, b, name):
    T, K = a.shape
    N = b.shape[1]
    tt = min(T, 512)
    tn = min(N, 1024)

    def body(a_ref, b_ref, o_ref):
        @pl.when(pl.program_id(1) == 0)
        def _():
            o_ref[...] = jnp.zeros_like(o_ref)

        o_ref[...] += _dot_tn(a_ref[...], b_ref[...])

    return pl.pallas_call(
        body, grid=(N // tn, T // tt),
        in_specs=[pl.BlockSpec((tt, K), lambda n, t: (t, 0)), pl.BlockSpec((tt, tn), lambda n, t: (t, n))],
        out_specs=pl.BlockSpec((K, tn), lambda n, t: (0, n)),
        out_shape=jax.ShapeDtypeStruct((K, N), F32),
        compiler_params=_params("parallel", "arbitrary"), name=name,
    )(a, b)


def _out_fwd(o, z, w, wout, xres, gs, gate_first, name):
    T = o.shape[0]
    tt = min(T, 256)
    wide = w.shape[1] == D_INNER

    def body(o_ref, z_ref, w_ref, wout_ref, x_ref, out_ref, yn):
        for g0 in range(0, D_INNER, gs):
            sl = slice(g0, g0 + gs)
            og, zg = o_ref[:, sl], z_ref[:, sl]
            wg = w_ref[:, sl] if wide else w_ref[...]
            if gate_first:
                u = og * _silu(zg)
                r = lax.rsqrt(jnp.mean(u * u, axis=-1, keepdims=True) + EPS)
                yn[:, sl] = _mx(u * r * wg)
            else:
                r = lax.rsqrt(jnp.mean(og * og, axis=-1, keepdims=True) + EPS)
                yn[:, sl] = _mx(og * r * wg * _silu(zg))
        out_ref[...] = x_ref[...] + jnp.dot(yn[...], wout_ref[...], preferred_element_type=F32)

    row = lambda width: pl.BlockSpec((tt, width), lambda i: (i, 0))
    full = lambda a: pl.BlockSpec(a.shape, lambda i: (0, 0))
    return pl.pallas_call(
        body, grid=(T // tt,),
        in_specs=[row(D_INNER), row(D_INNER), full(w), full(wout), row(D_MODEL)],
        out_specs=row(D_MODEL),
        out_shape=jax.ShapeDtypeStruct((T, D_MODEL), F32),
        scratch_shapes=[pltpu.VMEM((tt, D_INNER), MXU_DTYPE)],
        compiler_params=_params("parallel"), name=name,
    )(o, z, w, wout, xres)


def _out_bwd(dx, o, z, w, wout, gs, gate_first, name, comm=None):
    T = o.shape[0]
    tt = min(T, 256)
    wide = w.shape[1] == D_INNER

    def body(dx_ref, o_ref, z_ref, w_ref, wout_ref, do_ref, dz_ref, dw_ref, yn_ref):
        @pl.when(pl.program_id(0) == 0)
        def _():
            dw_ref[...] = jnp.zeros_like(dw_ref)

        dyn = _dot_nt(dx_ref[...], wout_ref[...])
        dw_acc = jnp.zeros((1, gs), F32)
        for g0 in range(0, D_INNER, gs):
            sl = slice(g0, g0 + gs)
            og, zg, dg = o_ref[:, sl], z_ref[:, sl], dyn[:, sl]
            wg = w_ref[:, sl] if wide else w_ref[...]
            sz = _silu(zg)
            if gate_first:
                u = og * sz
                r = lax.rsqrt(jnp.mean(u * u, axis=-1, keepdims=True) + EPS)
                uh = u * r
                yn_ref[:, sl] = _mx(uh * wg)
                dw_g = jnp.sum(dg * uh, axis=0, keepdims=True)
                duh = dg * wg
                du = r * (duh - uh * jnp.mean(duh * uh, axis=-1, keepdims=True))
                do_ref[:, sl] = du * sz
                dz_ref[:, sl] = _mx(du * og * _dsilu(zg))
            else:
                r = lax.rsqrt(jnp.mean(og * og, axis=-1, keepdims=True) + EPS)
                oh = og * r
                yn_ref[:, sl] = _mx(oh * wg * sz)
                dw_g = jnp.sum(dg * oh * sz, axis=0, keepdims=True)
                doh = dg * wg * sz
                dz_ref[:, sl] = _mx(dg * oh * wg * _dsilu(zg))
                do_ref[:, sl] = r * (doh - oh * jnp.mean(doh * oh, axis=-1, keepdims=True))
            if wide:
                dw_ref[:, sl] += dw_g
            else:
                dw_acc = dw_acc + dw_g
        if not wide:
            dw_ref[...] += dw_acc

    row = lambda width: pl.BlockSpec((tt, width), lambda i: (i, 0))
    full = lambda a: pl.BlockSpec(a.shape, lambda i: (0, 0))
    ex = _with_exchange(comm)
    outs = pl.pallas_call(
        ex["wrap"](body, 5, 4), grid=(T // tt,),
        in_specs=[row(D_MODEL), row(D_INNER), row(D_INNER), full(w), full(wout)] + ex["in_specs"],
        out_specs=[row(D_INNER), row(D_INNER), full(w), row(D_INNER)] + ex["out_specs"],
        out_shape=[jax.ShapeDtypeStruct((T, D_INNER), F32), jax.ShapeDtypeStruct((T, D_INNER), MXU_DTYPE),
                   jax.ShapeDtypeStruct(w.shape, F32), jax.ShapeDtypeStruct((T, D_INNER), MXU_DTYPE)]
        + ex["out_shape"],
        scratch_shapes=ex["scratch"],
        compiler_params=_params("arbitrary"), name=name,
    )(dx, o, z, w, wout, *ex["arrs"])
    outs = list(outs)
    return outs[:4] + ([outs[4:]] if comm else [])


HALO = 8
CONV_STRIP = 16


def _conv_fwd(pre, w, b, l2, scale, name):
    T, C = pre.shape
    tt = min(T, 512)
    tc = min(C, 1024)
    strip = tt

    def body(pre_ref, halo_ref, w_ref, b_ref, out_ref, P):
        i = pl.program_id(0)
        P[0:HALO, :] = jnp.where(i > 0, halo_ref[...], 0.0)
        P[HALO:HALO + tt, :] = pre_ref[...]
        wj = [w_ref[j:j + 1, :] for j in range(CONV_K)]
        bias = b_ref[...]
        for r0 in range(0, tt, strip):
            acc = bias + wj[0] * P[pl.ds(HALO - 3 + r0, strip), :]
            for j in range(1, CONV_K):
                acc = acc + wj[j] * P[pl.ds(HALO - 3 + j + r0, strip), :]
            s = _silu(acc)
            if l2:
                sls = [slice(g0, g0 + GDN_DK) for g0 in range(0, tc, GDN_DK)]
                rr = [lax.rsqrt(jnp.sum(s[:, sl] * s[:, sl], axis=-1, keepdims=True) + EPS) for sl in sls]
                for sl, r in zip(sls, rr):
                    out_ref[r0:r0 + strip, sl] = s[:, sl] * r * scale
            else:
                out_ref[r0:r0 + strip, :] = s

    return pl.pallas_call(
        body, grid=(T // tt, C // tc),
        in_specs=[pl.BlockSpec((tt, tc), lambda i, j: (i, j)),
                  pl.BlockSpec((HALO, tc), lambda i, j: (jnp.maximum(i * (tt // HALO) - 1, 0), j)),
                  pl.BlockSpec((CONV_K, tc), lambda i, j: (0, j)),
                  pl.BlockSpec((1, tc), lambda i, j: (0, j))],
        out_specs=pl.BlockSpec((tt, tc), lambda i, j: (i, j)),
        out_shape=jax.ShapeDtypeStruct((T, C), F32),
        scratch_shapes=[pltpu.VMEM((HALO + tt, tc), F32)],
        compiler_params=_params("parallel", "parallel"), name=name,
    )(pre, pre, w, b)


def _conv_bwd(pre, w, b, dpost, l2, scale, name):
    T, C = pre.shape
    tt = min(T, 512)
    tc = min(C, 1024 if l2 else 512)
    strip = tt if l2 else CONV_STRIP
    nT = T // tt
    ext = tt + HALO

    def body(pre_ref, hp_ref, hn_ref, dpost_ref, dn_ref, w_ref, b_ref, dpre_ref, dw_ref, db_ref, P, Q):
        i = pl.program_id(1)

        @pl.when(i == 0)
        def _():
            dw_ref[...] = jnp.zeros_like(dw_ref)
            db_ref[...] = jnp.zeros_like(db_ref)

        P[0:HALO, :] = jnp.where(i > 0, hp_ref[...], 0.0)
        P[HALO:HALO + tt, :] = pre_ref[...]
        P[HALO + tt:HALO + ext, :] = hn_ref[...]
        wj = [w_ref[j:j + 1, :] for j in range(CONV_K)]
        bias = b_ref[...]
        keep_next = jnp.where(i < nT - 1, 1.0, 0.0)
        for r0 in list(range(0, tt, strip)) + [tt]:
            n = strip if r0 < tt else HALO
            cpre = bias + wj[0] * P[pl.ds(HALO - 3 + r0, n), :]
            for j in range(1, CONV_K):
                cpre = cpre + wj[j] * P[pl.ds(HALO - 3 + j + r0, n), :]
            dy = dpost_ref[r0:r0 + n, :] if r0 < tt else dn_ref[...] * keep_next
            sg = _sigmoid(cpre)
            ds_c = sg * (1.0 + cpre * (1.0 - sg))
            if l2:
                s = cpre * sg
                sls = [slice(g0, g0 + GDN_DK) for g0 in range(0, tc, GDN_DK)]
                rr = [lax.rsqrt(jnp.sum(s[:, sl] * s[:, sl], axis=-1, keepdims=True) + EPS) for sl in sls]
                yh = [s[:, sl] * r for sl, r in zip(sls, rr)]
                pr = [jnp.sum(dy[:, sl] * y, axis=-1, keepdims=True) for sl, y in zip(sls, yh)]
                for sl, r, y, p in zip(sls, rr, yh, pr):
                    Q[r0:r0 + n, sl] = (scale * r) * (dy[:, sl] - y * p) * ds_c[:, sl]
            else:
                Q[r0:r0 + n, :] = dy * ds_c
        fold = lambda a: jnp.sum(a.reshape(strip // 8, 8, tc), axis=0)
        dw_acc = [jnp.zeros((8, tc), F32) for _ in range(CONV_K)]
        db_acc = jnp.zeros((8, tc), F32)
        for r0 in range(0, tt, strip):
            dpre = wj[0] * Q[pl.ds(3 + r0, strip), :]
            for j in range(1, CONV_K):
                dpre = dpre + wj[j] * Q[pl.ds(3 - j + r0, strip), :]
            dpre_ref[r0:r0 + strip, :] = _mx(dpre)
            dyc = Q[r0:r0 + strip, :]
            for j in range(CONV_K):
                dw_acc[j] = dw_acc[j] + fold(dyc * P[pl.ds(HALO - 3 + j + r0, strip), :])
            db_acc = db_acc + fold(dyc)
        for j in range(CONV_K):
            dw_ref[j:j + 1, :] += jnp.sum(dw_acc[j], axis=0, keepdims=True)
        db_ref[...] += jnp.sum(db_acc, axis=0, keepdims=True)

    tile = pl.BlockSpec((tt, tc), lambda j, i: (i, j))
    prev = pl.BlockSpec((HALO, tc), lambda j, i: (jnp.maximum(i * (tt // HALO) - 1, 0), j))
    nxt = pl.BlockSpec((HALO, tc), lambda j, i: (jnp.minimum((i + 1) * (tt // HALO), T // HALO - 1), j))
    return pl.pallas_call(
        body, grid=(C // tc, nT),
        in_specs=[tile, prev, nxt, tile, nxt,
                  pl.BlockSpec((CONV_K, tc), lambda j, i: (0, j)), pl.BlockSpec((1, tc), lambda j, i: (0, j))],
        out_specs=[tile, pl.BlockSpec((CONV_K, tc), lambda j, i: (0, j)), pl.BlockSpec((1, tc), lambda j, i: (0, j))],
        out_shape=[jax.ShapeDtypeStruct((T, C), MXU_DTYPE), jax.ShapeDtypeStruct((CONV_K, C), F32),
                   jax.ShapeDtypeStruct((1, C), F32)],
        scratch_shapes=[pltpu.VMEM((HALO + ext, tc), F32), pltpu.VMEM((ext, tc), F32)],
        compiler_params=_params("parallel", "arbitrary"), name=name,
    )(pre, pre, pre, dpost, dpost, w, b)


GDN_LOCKSTEP_CHUNKS = 16
GDN_SCAN_HEADS = 16


def _inv_unit_lower_many(nms, eye, n):
    xs = [jnp.where(eye, 1.0, 0.0) - nm for nm in nms]
    ps = list(nms)
    k = 2
    while k < n:
        ps = [_dot(p, p) for p in ps]
        xs = [x + _dot(x, p) for x, p in zip(xs, ps)]
        k *= 2
    return xs


def _gdn_prep(q, k, v, araw, braw, alog, dtb, name):
    T = q.shape[0]
    C = GDN_CHUNK
    tt = min(T, 1024)
    cpt, nC = tt // C, T // C
    grp = min(cpt, GDN_LOCKSTEP_CHUNKS)

    def body(alog_ref, dtb_ref, q_ref, k_ref, v_ref, a_ref, b_ref,
             u_ref, w_ref, pm_ref, ti_ref, g_ref, beta_ref, gc_ref):
        j = pl.program_id(0)
        tri, strict, eye, r_i, c_i = _masks(C)
        upper = jnp.where(r_i <= c_i, 1.0, 0.0)
        gcs, bts = [], []
        for hh in range(2):
            h = 2 * j + hh
            g = -jnp.exp(alog_ref[h]) * _softplus(a_ref[hh] + dtb_ref[h])
            bt = _sigmoid(b_ref[hh])
            gc = _dot_hi(g, upper)
            g_ref[hh], beta_ref[hh], gc_ref[hh] = g, bt, gc
            gcs.append(gc)
            bts.append(bt)
        for c0 in range(0, cpt, grp):
            cs = list(range(c0, c0 + grp))
            inst = [(c, hh) for c in cs for hh in range(2)]
            rows = {c: slice(c * C, (c + 1) * C) for c in cs}
            qc = {c: q_ref[rows[c], :] for c in cs}
            kc = {c: k_ref[rows[c], :] for c in cs}
            kk = {c: _dot_nt(kc[c], kc[c]) for c in cs}
            qk = {c: _dot_nt(qc[c], kc[c]) for c in cs}
            gcr = [gcs[hh][c:c + 1, :] for c, hh in inst]
            gcc = [_col(r, eye) for r in gcr]
            bc = [_col(bts[hh][c:c + 1, :], eye) for c, hh in inst]
            lm = [jnp.exp(jnp.where(tri, cc - r, -1e30)) for cc, r in zip(gcc, gcr)]
            nm = [jnp.where(strict, kk[c] * b * l, 0.0) for (c, hh), b, l in zip(inst, bc, lm)]
            tinv = _inv_unit_lower_many(nm, eye, C)
            rhs = [jnp.concatenate([v_ref[rows[c], hh * GDN_DK:(hh + 1) * GDN_DK] * b, kc[c] * (b * jnp.exp(cc))], axis=1)
                   for (c, hh), b, cc in zip(inst, bc, gcc)]
            sol = [_dot(t, r) for t, r in zip(tinv, rhs)]
            for (c, hh), s, t, l in zip(inst, sol, tinv, lm):
                hs = slice(hh * GDN_DK, (hh + 1) * GDN_DK)
                u_ref[rows[c], hs] = s[:, :GDN_DK]
                w_ref[rows[c], hs] = _mx(s[:, GDN_DK:])
                pm_ref[hh, c] = _mx(jnp.where(tri, qk[c] * l, 0.0))
                ti_ref[hh, c] = _mx(t)

    smem = pl.BlockSpec(memory_space=pltpu.SMEM)
    rows_spec = pl.BlockSpec((2, cpt, C), lambda j, i: (j, i, 0))
    qk_spec = pl.BlockSpec((tt, GDN_DK), lambda j, i: (i, j))
    v_spec = pl.BlockSpec((tt, 2 * GDN_DK), lambda j, i: (i, j))
    cc_spec = pl.BlockSpec((2, cpt, C, C), lambda j, i: (j, i, 0, 0))
    rows_shape = jax.ShapeDtypeStruct((GDN_HV, nC, C), F32)
    cc_shape = jax.ShapeDtypeStruct((GDN_HV, nC, C, C), MXU_DTYPE)
    return pl.pallas_call(
        body, grid=(GDN_HV // 2, T // tt),
        in_specs=[smem, smem, qk_spec, qk_spec, v_spec, rows_spec, rows_spec],
        out_specs=[v_spec, v_spec, cc_spec, cc_spec, rows_spec, rows_spec, rows_spec],
        out_shape=[jax.ShapeDtypeStruct((T, D_INNER), F32), jax.ShapeDtypeStruct((T, D_INNER), MXU_DTYPE),
                   cc_shape, cc_shape, rows_shape, rows_shape, rows_shape],
        compiler_params=_params("parallel", "parallel"), name=name,
    )(alog, dtb, q, k, v, araw, braw)


def _gdn_decays(gc_ref, h, c, eye, C):
    gcr = gc_ref[h, pl.ds(c, 1), :]
    gcc = _col(gcr, eye)
    glast = gcr[:, C - 1:C]
    return jnp.exp(gcc), jnp.exp(glast - gcc), jnp.exp(glast)


def _gdn_state_fwd(q, k, u, w, pm, gc, name):
    T = q.shape[0]
    C = GDN_CHUNK
    HG = GDN_SCAN_HEADS
    tt = min(T, 512)
    cpt, nC = tt // C, T // C

    def body(q_ref, k_ref, u_ref, w_ref, pm_ref, gc_ref, o_ref, vn_ref, sall_ref, S):
        @pl.when(pl.program_id(1) == 0)
        def _():
            S[...] = jnp.zeros_like(S)

        eye = _masks(C)[2]
        heads = list(range(HG))

        def chunk(c, carry):
            rows = pl.ds(pl.multiple_of(c * C, C), C)
            hs = [slice(h * GDN_DK, (h + 1) * GDN_DK) for h in heads]
            qs = [slice((h // 2) * GDN_DK, (h // 2 + 1) * GDN_DK) for h in heads]
            dec = [_gdn_decays(gc_ref, h, c, eye, C) for h in heads]
            sv = [S[h] for h in heads]
            for h in heads:
                sall_ref[h, c] = _mx(sv[h])
            ws = [_dot(w_ref[rows, hs[h]], sv[h]) for h in heads]
            qsv = [_dot(q_ref[rows, qs[h]], sv[h]) for h in heads]
            vn = [u_ref[rows, hs[h]] - ws[h] for h in heads]
            pv = [_dot(pm_ref[h, c], vn[h]) for h in heads]
            kv = [_dot_tn(k_ref[rows, qs[h]], vn[h] * dec[h][1]) for h in heads]
            for h in heads:
                vn_ref[rows, hs[h]] = _mx(vn[h])
                o_ref[rows, hs[h]] = qsv[h] * dec[h][0] + pv[h]
                S[h] = sv[h] * dec[h][2] + kv[h]
            return carry

        lax.fori_loop(0, cpt, chunk, 0)

    qk_spec = pl.BlockSpec((tt, HG // 2 * GDN_DK), lambda g, i: (i, g))
    v_spec = pl.BlockSpec((tt, HG * GDN_DK), lambda g, i: (i, g))
    return pl.pallas_call(
        body, grid=(GDN_HV // HG, T // tt),
        in_specs=[qk_spec, qk_spec, v_spec, v_spec,
                  pl.BlockSpec((HG, cpt, C, C), lambda g, i: (g, i, 0, 0)),
                  pl.BlockSpec((HG, cpt, C), lambda g, i: (g, i, 0))],
        out_specs=[v_spec, v_spec, pl.BlockSpec((HG, cpt, GDN_DK, GDN_DK), lambda g, i: (g, i, 0, 0))],
        out_shape=[jax.ShapeDtypeStruct((T, D_INNER), F32), jax.ShapeDtypeStruct((T, D_INNER), MXU_DTYPE),
                   jax.ShapeDtypeStruct((GDN_HV, nC, GDN_DK, GDN_DK), MXU_DTYPE)],
        scratch_shapes=[pltpu.VMEM((HG, GDN_DK, GDN_DK), F32)],
        compiler_params=_params("parallel", "arbitrary"), name=name,
    )(q, k, u, w, pm, gc)


def _gdn_state_bwd(q, k, w, pm, vn, sall, gc, do, name):
    T = q.shape[0]
    C = GDN_CHUNK
    HG = GDN_SCAN_HEADS
    tt = min(T, 512)
    cpt, nC, nT = tt // C, T // C, T // tt

    def body(q_ref, k_ref, w_ref, pm_ref, vn_ref, sall_ref, gc_ref, do_ref, dvn_ref, dkd_ref, dgl_ref, dS):
        @pl.when(pl.program_id(1) == 0)
        def _():
            dS[...] = jnp.zeros_like(dS)

        eye = _masks(C)[2]
        heads = list(range(HG))

        def chunk(ci, carry):
            c = cpt - 1 - ci
            rows = pl.ds(pl.multiple_of(c * C, C), C)
            hs = [slice(h * GDN_DK, (h + 1) * GDN_DK) for h in heads]
            qs = [slice((h // 2) * GDN_DK, (h // 2 + 1) * GDN_DK) for h in heads]
            dec = [_gdn_decays(gc_ref, h, c, eye, C) for h in heads]
            dsn = [dS[h] for h in heads]
            doc = [do_ref[rows, hs[h]] for h in heads]
            kds = [_dot(k_ref[rows, qs[h]], dsn[h]) for h in heads]
            pdo = [_dot_tn(pm_ref[h, c], doc[h]) for h in heads]
            dkd = [_dot_nt(vn_ref[rows, hs[h]], dsn[h]) for h in heads]
            qdo = [_dot_tn(q_ref[rows, qs[h]], doc[h] * dec[h][0]) for h in heads]
            dvn = [pdo[h] + kds[h] * dec[h][1] for h in heads]
            wdv = [_dot_tn(w_ref[rows, hs[h]], dvn[h]) for h in heads]
            for h in heads:
                dgl = jnp.sum(jnp.sum(dsn[h] * sall_ref[h, c].astype(F32), axis=0, keepdims=True), axis=1, keepdims=True)
                dgl_ref[h, pl.ds(c, 1), :] = jnp.broadcast_to(dgl, (1, C))
                dvn_ref[rows, hs[h]] = dvn[h]
                dkd_ref[rows, hs[h]] = dkd[h]
                dS[h] = dsn[h] * dec[h][2] + qdo[h] - wdv[h]
            return carry

        lax.fori_loop(0, cpt, chunk, 0)

    rev = lambda i: nT - 1 - i
    qk_spec = pl.BlockSpec((tt, HG // 2 * GDN_DK), lambda g, i: (rev(i), g))
    v_spec = pl.BlockSpec((tt, HG * GDN_DK), lambda g, i: (rev(i), g))
    rows_spec = pl.BlockSpec((HG, cpt, C), lambda g, i: (g, rev(i), 0))
    return pl.pallas_call(
        body, grid=(GDN_HV // HG, nT),
        in_specs=[qk_spec, qk_spec, v_spec, pl.BlockSpec((HG, cpt, C, C), lambda g, i: (g, rev(i), 0, 0)), v_spec,
                  pl.BlockSpec((HG, cpt, GDN_DK, GDN_DK), lambda g, i: (g, rev(i), 0, 0)), rows_spec, v_spec],
        out_specs=[v_spec, v_spec, rows_spec],
        out_shape=[jax.ShapeDtypeStruct((T, D_INNER), F32), jax.ShapeDtypeStruct((T, D_INNER), F32),
                   jax.ShapeDtypeStruct((GDN_HV, nC, C), F32)],
        scratch_shapes=[pltpu.VMEM((HG, GDN_DK, GDN_DK), F32)],
        compiler_params=_params("parallel", "arbitrary"), name=name,
    )(q, k, w, pm, vn, sall, gc, do)


def _gdn_local_bwd(q, k, v, gc, beta, tinv, u, w, pm, vn, sall, do, dvn, dkd, dgl, name):
    T = q.shape[0]
    C = GDN_CHUNK
    tt = min(T, 1024)
    cpt, nC = tt // C, T // C
    grp = min(cpt, GDN_LOCKSTEP_CHUNKS)

    def body(q_ref, k_ref, v_ref, gc_ref, b_ref, ti_ref, u_ref, w_ref, pm_ref, vn_ref, sall_ref, do_ref,
             dvn_ref, dkd_ref, dgl_ref, dq_ref, dk_ref, dv_ref, dg_ref, dbeta_ref, dgc_s):
        tri, strict, eye, r_i, c_i = _masks(C)
        lower = jnp.where(r_i >= c_i, 1.0, 0.0)
        lane = lax.broadcasted_iota(jnp.int32, (1, C), 1)
        rsum = lambda a: jnp.sum(a, axis=1, keepdims=True)
        for c0 in range(0, cpt, grp):
            cs = list(range(c0, c0 + grp))
            inst = [(c, hh) for c in cs for hh in range(2)]
            n = len(inst)
            rows = {c: slice(c * C, (c + 1) * C) for c in cs}
            hsl = [slice(hh * GDN_DK, (hh + 1) * GDN_DK) for c, hh in inst]
            qc = {c: q_ref[rows[c], :] for c in cs}
            kc = {c: k_ref[rows[c], :] for c in cs}
            kk = {c: _dot_nt(kc[c], kc[c]) for c in cs}
            gcr = [gc_ref[hh, c:c + 1, :] for c, hh in inst]
            gcc = [_col(r, eye) for r in gcr]
            bc = [_col(b_ref[hh, c:c + 1, :], eye) for c, hh in inst]
            lm = [jnp.exp(jnp.where(tri, cc - r, -1e30)) for cc, r in zip(gcc, gcr)]
            e_c = [jnp.exp(cc) for cc in gcc]
            el_c = [jnp.exp(r[:, C - 1:C] - cc) for cc, r in zip(gcc, gcr)]
            gl = [jnp.exp(r[:, C - 1:C]) for r in gcr]
            doc = [do_ref[rows[c], hsl[i]] for i, (c, hh) in enumerate(inst)]
            dvn = [dvn_ref[rows[c], hsl[i]] for i, (c, hh) in enumerate(inst)]
            sv = [sall_ref[hh, c] for c, hh in inst]
            aa = [_dot_nt(jnp.concatenate([_mx(doc[i]), _mx(dvn[i])], axis=0), sv[i]) for i in range(n)]
            dpm = [jnp.where(tri, _dot_nt(doc[i], vn_ref[rows[c], hsl[i]]), 0.0) for i, (c, hh) in enumerate(inst)]
            dqd = [a[:C] for a in aa]
            drhs = [_dot_tn(ti_ref[hh, c], jnp.concatenate([dvn[i], -aa[i][C:]], axis=1))
                    for i, (c, hh) in enumerate(inst)]
            sol = [jnp.concatenate([_mx(u_ref[rows[c], hsl[i]]), w_ref[rows[c], hsl[i]]], axis=1)
                   for i, (c, hh) in enumerate(inst)]
            dnm = [-jnp.where(strict, _dot_nt(drhs[i], sol[i]), 0.0) for i in range(n)]
            dkk = [dnm[i] * bc[i] * lm[i] for i in range(n)]
            dqk = [dpm[i] * lm[i] for i in range(n)]
            dq1 = [_dot(dqk[i], kc[c]) for i, (c, hh) in enumerate(inst)]
            dk1 = [_dot(dkk[i], kc[c]) for i, (c, hh) in enumerate(inst)]
            dk2 = [_dot_tn(dkk[i], kc[c]) for i, (c, hh) in enumerate(inst)]
            dk3 = [_dot_tn(dqk[i], qc[c]) for i, (c, hh) in enumerate(inst)]
            dq_acc = {c: jnp.zeros((C, GDN_DK), F32) for c in cs}
            dk_acc = {c: jnp.zeros((C, GDN_DK), F32) for c in cs}
            for i, (c, hh) in enumerate(inst):
                k_, q_, v_ = kc[c], qc[c], v_ref[rows[c], hsl[i]]
                dvb, dkbe = drhs[i][:, :GDN_DK], drhs[i][:, GDN_DK:]
                dkd = dkd_ref[rows[c], hsl[i]]
                kb = k_ * bc[i]
                dkb = dkbe * e_c[i]
                del_el = dkd * k_ * el_c[i]
                dbc = rsum(dnm[i] * kk[c] * lm[i]) + rsum(dkb * k_ + dvb * v_)
                dq_acc[c] = dq_acc[c] + dq1[i] + dqd[i] * e_c[i]
                dk_acc[c] = dk_acc[c] + dk1[i] + dk2[i] + dk3[i] + dkd * el_c[i] + dkb * bc[i]
                dv_ref[rows[c], hsl[i]] = dvb * bc[i]
                nm = jnp.where(strict, kk[c] * bc[i] * lm[i], 0.0)
                gm = dnm[i] * nm + dpm[i] * pm_ref[hh, c].astype(F32)
                dgc_col = rsum(gm) + rsum((dkbe * kb + dqd[i] * q_) * e_c[i] - del_el)
                dglast = (jnp.sum(jnp.sum(del_el, axis=0, keepdims=True), axis=1, keepdims=True)
                          + dgl_ref[hh, c:c + 1, 0:1] * gl[i])
                dgc_s[hh, c:c + 1, :] = (_row(dgc_col, eye) - jnp.sum(gm, axis=0, keepdims=True)
                                         + jnp.where(lane == C - 1, dglast, 0.0))
                dbeta_ref[hh, c:c + 1, :] = _row(dbc, eye)
            for c in cs:
                dq_ref[rows[c], :] = dq_acc[c]
                dk_ref[rows[c], :] = dk_acc[c]
        for hh in range(2):
            dg_ref[hh] = _dot_hi(dgc_s[hh], lower)

    rows_spec = pl.BlockSpec((2, cpt, C), lambda j, i: (j, i, 0))
    qk_spec = pl.BlockSpec((tt, GDN_DK), lambda j, i: (i, j))
    v_spec = pl.BlockSpec((tt, 2 * GDN_DK), lambda j, i: (i, j))
    cc_spec = pl.BlockSpec((2, cpt, C, C), lambda j, i: (j, i, 0, 0))
    rows_shape = jax.ShapeDtypeStruct((GDN_HV, nC, C), F32)
    return pl.pallas_call(
        body, grid=(GDN_HV // 2, T // tt),
        in_specs=[qk_spec, qk_spec, v_spec, rows_spec, rows_spec, cc_spec, v_spec, v_spec, cc_spec, v_spec,
                  pl.BlockSpec((2, cpt, GDN_DK, GDN_DK), lambda j, i: (j, i, 0, 0)), v_spec, v_spec, v_spec, rows_spec],
        out_specs=[qk_spec, qk_spec, v_spec, rows_spec, rows_spec],
        out_shape=[jax.ShapeDtypeStruct((T, GDN_HV // 2 * GDN_DK), F32),
                   jax.ShapeDtypeStruct((T, GDN_HV // 2 * GDN_DK), F32),
                   jax.ShapeDtypeStruct((T, D_INNER), F32), rows_shape, rows_shape],
        scratch_shapes=[pltpu.VMEM((2, cpt, C), F32)],
        compiler_params=_params("parallel", "parallel"), name=name,
    )(q, k, v, gc, beta, tinv, u, w, pm, vn, sall, do, dvn, dkd, dgl)


def _gdn_gate_bwd(araw, braw, dg, dbeta, alog, dtb, name):
    H, T = araw.shape

    def body(a_ref, b_ref, dg_ref, dbt_ref, alog_ref, dtb_ref, da_ref, db_ref, dalog_ref, ddtb_ref):
        xa = a_ref[...] + dtb_ref[...]
        ea = jnp.exp(alog_ref[...])
        dgv = dg_ref[...]
        da = -dgv * ea * _sigmoid(xa)
        da_ref[...] = da
        dalog_ref[...] = jnp.sum(-dgv * ea * _softplus(xa), axis=1, keepdims=True)
        ddtb_ref[...] = jnp.sum(da, axis=1, keepdims=True)
        bt = _sigmoid(b_ref[...])
        db_ref[...] = dbt_ref[...] * bt * (1.0 - bt)

    return pl.pallas_call(
        body,
        out_shape=[jax.ShapeDtypeStruct((H, T), F32), jax.ShapeDtypeStruct((H, T), F32),
                   jax.ShapeDtypeStruct((H, 1), F32), jax.ShapeDtypeStruct((H, 1), F32)],
        compiler_params=pltpu.CompilerParams(vmem_limit_bytes=VMEM_LIMIT_BYTES), name=name,
    )(araw, braw, dg, dbeta, alog, dtb)


SSD_LOCKSTEP_CHUNKS = 2
SSD_LOCKSTEP_CHUNKS_BWD = 1
SSD_LOCKSTEP_HEADS_BWD = 2


def _ssd_scan_fwd(xs, bm, cm, dtraw, alog, dtb, dskip, name):
    T = xs.shape[0]
    Q = SSD_CHUNK
    tt = min(T, 1024)
    cpt, nC = tt // Q, T // Q
    GW = SSD_R * SSD_P

    def body(alog_ref, dtb_ref, dsk_ref, xs_ref, b_ref, c_ref, dt_ref, y_ref, sall_ref, dto_ref, S, dt_s, acs_s):
        gi, i = pl.program_id(0), pl.program_id(1)

        @pl.when(i == 0)
        def _():
            S[...] = jnp.zeros_like(S)

        tri, _, eye, r_i, c_i = _masks(Q)
        upper = jnp.where(r_i <= c_i, 1.0, 0.0)
        for r in range(SSD_R):
            h = SSD_R * gi + r
            dt = _softplus(dt_ref[r] + dtb_ref[h])
            dto_ref[r] = dt
            dt_s[r] = dt
            acs_s[r] = _dot_hi(-jnp.exp(alog_ref[h]) * dt, upper)

        ps = [slice(r * SSD_P, (r + 1) * SSD_P) for r in range(SSD_R)]
        s_cur = [S[:, ps[r]] for r in range(SSD_R)]
        grp = min(cpt, SSD_LOCKSTEP_CHUNKS)
        for c0 in range(0, cpt, grp):
            cs = list(range(c0, c0 + grp))
            inst = [(c, r) for c in cs for r in range(SSD_R)]
            rows = {c: slice(c * Q, (c + 1) * Q) for c in cs}
            bc_ = {c: b_ref[rows[c], :] for c in cs}
            cc_ = {c: c_ref[rows[c], :] for c in cs}
            cb = {c: _dot_nt(cc_[c], bc_[c]) for c in cs}
            xr = [xs_ref[rows[c], ps[r]] for c, r in inst]
            acr = [acs_s[r, c:c + 1, :] for c, r in inst]
            acc = [_col_bcast(a, Q) for a in acr]
            dtr = [dt_s[r, c:c + 1, :] for c, r in inst]
            mm = [cb[c] * (jnp.exp(jnp.where(tri, acc[i] - acr[i], -1e30)) * dtr[i]) for i, (c, r) in enumerate(inst)]
            bct = {c: bc_[c].T for c in cs}
            st = [_dot(bct[c] * (jnp.exp(acr[i][:, Q - 1:Q] - acr[i]) * dtr[i]), xr[i]) for i, (c, r) in enumerate(inst)]
            yd = [_dot(mm[i], xr[i]) for i in range(len(inst))]
            s_prev = []
            for i, (c, r) in enumerate(inst):
                s_prev.append(s_cur[r])
                s_cur[r] = s_cur[r] * jnp.exp(acr[i][:, Q - 1:Q]) + st[i]
            yo = [_dot(cc_[c] * jnp.exp(acc[i]), s_prev[i]) for i, (c, r) in enumerate(inst)]
            for i, (c, r) in enumerate(inst):
                sall_ref[0, c, :, ps[r]] = s_prev[i]
                y_ref[rows[c], ps[r]] = yd[i] + yo[i] + dsk_ref[SSD_R * gi + r] * xr[i]
        for r in range(SSD_R):
            S[:, ps[r]] = s_cur[r]

    smem = pl.BlockSpec(memory_space=pltpu.SMEM)
    rows_spec = pl.BlockSpec((SSD_R, cpt, Q), lambda g, i: (g, i, 0))
    return pl.pallas_call(
        body, grid=(SSD_G, T // tt),
        in_specs=[smem, smem, smem,
                  pl.BlockSpec((tt, GW), lambda g, i: (i, g)), pl.BlockSpec((tt, SSD_N), lambda g, i: (i, g)),
                  pl.BlockSpec((tt, SSD_N), lambda g, i: (i, g)), rows_spec],
        out_specs=[pl.BlockSpec((tt, GW), lambda g, i: (i, g)),
                   pl.BlockSpec((1, cpt, SSD_N, GW), lambda g, i: (g, i, 0, 0)), rows_spec],
        out_shape=[jax.ShapeDtypeStruct((T, D_INNER), F32), jax.ShapeDtypeStruct((SSD_G, nC, SSD_N, GW), F32),
                   jax.ShapeDtypeStruct((SSD_H, nC, Q), F32)],
        scratch_shapes=[pltpu.VMEM((SSD_N, GW), F32), pltpu.VMEM((SSD_R, cpt, Q), F32),
                        pltpu.VMEM((SSD_R, cpt, Q), F32)],
        compiler_params=_params("parallel", "arbitrary"), name=name,
    )(alog, dtb, dskip, xs, bm, cm, dtraw)


def _ssd_scan_bwd(xs, bm, cm, dt, sall, dy, alog, dskip, name):
    T = xs.shape[0]
    Q = SSD_CHUNK
    tt = min(T, 1024)
    cpt, nC, nT = tt // Q, T // Q, T // tt
    GW = SSD_R * SSD_P

    def body(alog_ref, dsk_ref, xs_ref, b_ref, c_ref, dt_ref, sall_ref, dy_ref,
             dxs_ref, db_ref, dc_ref, da_ref, ddt_ref, dd_ref, dS, acs_s, dacs_s, ddt_s, dd_s):
        gi, i = pl.program_id(0), pl.program_id(1)

        @pl.when(i == 0)
        def _():
            dS[...] = jnp.zeros_like(dS)

        tri, _, eye, r_i, c_i = _masks(Q)
        upper = jnp.where(r_i <= c_i, 1.0, 0.0)
        lower = jnp.where(r_i >= c_i, 1.0, 0.0)
        lane = lax.broadcasted_iota(jnp.int32, (1, Q), 1)
        for r in range(SSD_R):
            acs_s[r] = _dot_hi(-jnp.exp(alog_ref[SSD_R * gi + r]) * dt_ref[r], upper)

        ps = [slice(r * SSD_P, (r + 1) * SSD_P) for r in range(SSD_R)]
        ds_cur = [dS[:, ps[r]] for r in range(SSD_R)]
        grp = min(cpt, SSD_LOCKSTEP_CHUNKS_BWD)
        csum = lambda a: jnp.sum(a, axis=0, keepdims=True)
        tsum = lambda a: jnp.sum(csum(a), axis=1, keepdims=True)
        ones8 = jnp.ones((8, SSD_P), F32)
        for c0 in range(cpt - grp, -1, -grp):
            cs = list(range(c0 + grp - 1, c0 - 1, -1))
            rows = {c: slice(c * Q, (c + 1) * Q) for c in cs}
            bc_ = {c: b_ref[rows[c], :] for c in cs}
            cc_ = {c: c_ref[rows[c], :] for c in cs}
            cb = {c: _dot_nt(cc_[c], bc_[c]) for c in cs}
            cbt = {c: _dot_nt(bc_[c], cc_[c]) for c in cs}
            bct = {c: bc_[c].T for c in cs}
            cct = {c: cc_[c].T for c in cs}
            dcb = {c: jnp.zeros((Q, Q), F32) for c in cs}
            dcbt = {c: jnp.zeros((Q, Q), F32) for c in cs}
            db_acc = {c: jnp.zeros((Q, SSD_N), F32) for c in cs}
            dc_acc = {c: jnp.zeros((Q, SSD_N), F32) for c in cs}
            for h0 in range(0, SSD_R, SSD_LOCKSTEP_HEADS_BWD):
                inst = [(c, r) for c in cs for r in range(h0, h0 + SSD_LOCKSTEP_HEADS_BWD)]
                n = len(inst)
                xr = [xs_ref[rows[c], ps[r]] for c, r in inst]
                dyr = [dy_ref[rows[c], ps[r]] for c, r in inst]
                acr = [acs_s[r, c:c + 1, :] for c, r in inst]
                dtr = [dt_ref[r, c:c + 1, :] for c, r in inst]
                acc = [_col_bcast(a, Q) for a in acr]
                dtb = [_col_bcast(d, Q) for d in dtr]
                al = [a[:, Q - 1:Q] for a in acr]
                e_c = [jnp.exp(a) for a in acc]
                dl_c = [jnp.exp(al[i] - acc[i]) for i in range(n)]
                e_r = [jnp.exp(a) for a in acr]
                dl_r = [jnp.exp(al[i] - acr[i]) for i in range(n)]
                gl = [jnp.exp(a) for a in al]
                lm = [jnp.exp(jnp.where(tri, acc[i] - acr[i], -1e30)) for i in range(n)]
                lmt = [jnp.exp(jnp.where(r_i <= c_i, acr[i] - acc[i], -1e30)) for i in range(n)]
                mmt = [cbt[c] * lmt[i] for i, (c, r) in enumerate(inst)]
                sr = [sall_ref[0, c, :, ps[r]] for c, r in inst]
                dmm0 = [_dot_nt(dyr[i], xr[i]) for i in range(n)]
                dmm0t = [_dot_nt(xr[i], dyr[i]) for i in range(n)]
                dxd1 = [_dot(mmt[i], dyr[i]) for i in range(n)]
                dce = [_dot_nt(dyr[i], sr[i]) for i in range(n)]
                dcet = [_dot_nt(sr[i], dyr[i]) for i in range(n)]
                cdy = [_dot(cct[c] * e_r[i], dyr[i]) for i, (c, r) in enumerate(inst)]
                dsn = []
                for i, (c, r) in enumerate(inst):
                    dsn.append(ds_cur[r])
                    ds_cur[r] = gl[i] * ds_cur[r] + cdy[i]
                dxd = [dxd1[i] + _dot(bc_[c] * dl_c[i], dsn[i]) for i, (c, r) in enumerate(inst)]
                dbd0 = [_dot_nt(xr[i], dsn[i]) for i in range(n)]
                dbd0t = [_dot_nt(dsn[i], xr[i]) for i in range(n)]
                for i, (c, r) in enumerate(inst):
                    dgl = tsum(dsn[i] * sr[i])
                    dc_acc[c] = dc_acc[c] + dce[i] * e_c[i]
                    db_acc[c] = db_acc[c] + dbd0[i] * (dtb[i] * dl_c[i])
                    dl0 = dmm0[i] * lm[i]
                    dl0t = dmm0t[i] * (lmt[i] * dtb[i])
                    dcb[c] = dcb[c] + dl0 * dtr[i]
                    dcbt[c] = dcbt[c] + dl0t
                    csum_gm0 = csum(dl0 * cb[c])
                    rsum_gm = csum(dl0t * cbt[c])
                    r_de = csum(dcet[i] * cct[c]) * e_r[i]
                    r_dl = csum(dbd0t[i] * bct[c]) * dl_r[i]
                    dalast = jnp.sum(r_dl * dtr[i], axis=1, keepdims=True) + dgl * gl[i]
                    dacs_s[r, c:c + 1, :] = (rsum_gm + r_de - (r_dl + csum_gm0) * dtr[i]
                                             + jnp.where(lane == Q - 1, dalast, 0.0))
                    ddt_s[r, c:c + 1, :] = csum_gm0 + r_dl
                    dd_s[r, c:c + 1, :] = _dot_nt(ones8, dyr[i] * xr[i])[0:1]
                    dxs_ref[rows[c], ps[r]] = dxd[i] * dtb[i][:, :SSD_P] + dsk_ref[SSD_R * gi + r] * dyr[i]
            for c in cs:
                dc_ref[rows[c], :] = dc_acc[c] + _dot(dcb[c], bc_[c])
                db_ref[rows[c], :] = db_acc[c] + _dot(dcbt[c], cc_[c])
        for r in range(SSD_R):
            dS[:, ps[r]] = ds_cur[r]
        for r in range(SSD_R):
            da_ref[r] = _dot_hi(dacs_s[r], lower)
            ddt_ref[r] = ddt_s[r]
            dd_ref[r] = dd_s[r]

    rev = lambda i: nT - 1 - i
    smem = pl.BlockSpec(memory_space=pltpu.SMEM)
    rows_spec = pl.BlockSpec((SSD_R, cpt, Q), lambda g, i: (g, rev(i), 0))
    x_spec = pl.BlockSpec((tt, GW), lambda g, i: (rev(i), g))
    n_spec = pl.BlockSpec((tt, SSD_N), lambda g, i: (rev(i), g))
    rows_shape = jax.ShapeDtypeStruct((SSD_H, nC, Q), F32)
    return pl.pallas_call(
        body, grid=(SSD_G, nT),
        in_specs=[smem, smem, x_spec, n_spec, n_spec, rows_spec,
                  pl.BlockSpec((1, cpt, SSD_N, GW), lambda g, i: (g, rev(i), 0, 0)), x_spec],
        out_specs=[x_spec, n_spec, n_spec, rows_spec, rows_spec, rows_spec],
        out_shape=[jax.ShapeDtypeStruct((T, D_INNER), F32), jax.ShapeDtypeStruct((T, SSD_G * SSD_N), F32),
                   jax.ShapeDtypeStruct((T, SSD_G * SSD_N), F32), rows_shape, rows_shape, rows_shape],
        scratch_shapes=[pltpu.VMEM((SSD_N, GW), F32)] + [pltpu.VMEM((SSD_R, cpt, Q), F32)] * 4,
        compiler_params=_params("parallel", "arbitrary"), name=name,
    )(alog, dskip, xs, bm, cm, dt, sall, dy)


def _ssd_gate_bwd(dtraw, dt, da, ddt_direct, ddrow, alog, dtb, name):
    H, T = dtraw.shape

    def body(raw_ref, dt_ref, da_ref, ddt_ref, dd_ref, alog_ref, dtb_ref, draw_ref, dalog_ref, ddtb_ref, dD_ref):
        a = -jnp.exp(alog_ref[...])
        dav = da_ref[...]
        ddt = ddt_ref[...] + dav * a
        draw = ddt * _sigmoid(raw_ref[...] + dtb_ref[...])
        draw_ref[...] = draw
        dalog_ref[...] = jnp.sum(dav * dt_ref[...], axis=1, keepdims=True) * a
        ddtb_ref[...] = jnp.sum(draw, axis=1, keepdims=True)
        dD_ref[...] = jnp.sum(dd_ref[...], axis=1, keepdims=True)

    return pl.pallas_call(
        body,
        out_shape=[jax.ShapeDtypeStruct((H, T), F32)] + [jax.ShapeDtypeStruct((H, 1), F32)] * 3,
        compiler_params=pltpu.CompilerParams(vmem_limit_bytes=VMEM_LIMIT_BYTES), name=name,
    )(dtraw, dt, da, ddt_direct, ddrow, alog, dtb)


def _final_loss(x, fw, tgt, name):
    T = x.shape[0]
    tt = min(T, 512)
    nT = T // tt

    def body(x_ref, w_ref, t_ref, dx_ref, dw_ref, loss_ref, acc):
        i = pl.program_id(0)

        @pl.when(i == 0)
        def _():
            dw_ref[...] = jnp.zeros_like(dw_ref)
            acc[...] = jnp.zeros_like(acc)

        xv = x_ref[...]
        r = lax.rsqrt(jnp.mean(xv * xv, axis=-1, keepdims=True) + EPS)
        xh = xv * r
        err = xh * w_ref[...] - t_ref[...]
        acc[...] += jnp.sum(err * err, axis=0, keepdims=True)
        dout = err * (1.0 / D_MODEL)
        dw_ref[...] += jnp.sum(dout * xh, axis=0, keepdims=True)
        dxn = dout * w_ref[...]
        dx_ref[...] = r * (dxn - xh * jnp.mean(dxn * xh, axis=-1, keepdims=True))

        @pl.when(i == nT - 1)
        def _():
            loss_ref[...] = (0.5 / D_MODEL) * jnp.sum(acc[...], axis=1, keepdims=True)

    row = pl.BlockSpec((tt, D_MODEL), lambda i: (i, 0))
    vec = pl.BlockSpec((1, D_MODEL), lambda i: (0, 0))
    return pl.pallas_call(
        body, grid=(nT,),
        in_specs=[row, vec, row],
        out_specs=[row, vec, pl.BlockSpec((1, 1), lambda i: (0, 0))],
        out_shape=[jax.ShapeDtypeStruct((T, D_MODEL), F32), jax.ShapeDtypeStruct((1, D_MODEL), F32),
                   jax.ShapeDtypeStruct((1, 1), F32)],
        scratch_shapes=[pltpu.VMEM((1, D_MODEL), F32)],
        compiler_params=_params("arbitrary"), name=name,
    )(x, fw, tgt)


def _adamw(parts, w, m, v, name):
    R, C = w.shape
    tr = 128 if R % 128 == 0 else R

    def body(p_ref, w_ref, m_ref, v_ref, g_ref, d_ref, nm_ref, nv_ref):
        g = p_ref[0].astype(F32)
        for s in range(1, N_DEV):
            g = g + p_ref[s].astype(F32)
        mn = ADAM_B1 * m_ref[...] + (1.0 - ADAM_B1) * g
        vn = ADAM_B2 * v_ref[...] + (1.0 - ADAM_B2) * (g * g)
        mh = mn / (1.0 - ADAM_B1 ** ADAM_STEP)
        vh = vn / (1.0 - ADAM_B2 ** ADAM_STEP)
        g_ref[...] = g
        d_ref[...] = -ADAM_LR * (mh / (jnp.sqrt(vh) + ADAM_EPS) + ADAM_WD * w_ref[...])
        nm_ref[...] = mn
        nv_ref[...] = vn

    blk = pl.BlockSpec((tr, C), lambda i: (i, 0))
    return pl.pallas_call(
        body, grid=(R // tr,),
        in_specs=[pl.BlockSpec((N_DEV, tr, C), lambda i: (0, i, 0)), blk, blk, blk],
        out_specs=[blk] * 4,
        out_shape=[jax.ShapeDtypeStruct((R, C), F32)] * 4,
        compiler_params=_params("parallel"), name=name,
    )(parts, w, m, v)


def _me():
    x, y, c = lax.axis_index("x"), lax.axis_index("y"), lax.axis_index("c")
    return x, y, c


def _peer(d):
    x, y, c = _me()
    px = 1 - x if d & 4 else x
    py = 1 - y if d & 2 else y
    pc = 1 - c if d & 1 else c
    return (px, py, pc), 4 * px + 2 * py + pc


def _exchange(arrs, bcast, name):
    n = len(arrs)

    def body(*refs):
        ex = _Exchange(refs[:n], refs[n:2 * n], bcast, *refs[2 * n:])
        ex.begin()
        ex.finish()

    anyspec = pl.BlockSpec(memory_space=pl.ANY)
    return pl.pallas_call(
        body,
        in_specs=[anyspec] * n, out_specs=[anyspec] * n,
        out_shape=_exchange_out_shapes(arrs, bcast),
        scratch_shapes=_exchange_semaphores(n),
        name=name,
    )(*arrs)


def _exchange_out_shapes(arrs, bcast):
    return [jax.ShapeDtypeStruct((N_DEV,) + (a.shape if b else a.shape[1:]), a.dtype) for a, b in zip(arrs, bcast)]


def _exchange_semaphores(n):
    return [pltpu.SemaphoreType.DMA((n, N_DEV - 1)), pltpu.SemaphoreType.DMA((n, N_DEV - 1)),
            pltpu.SemaphoreType.DMA((n,))]


class _Exchange:
    def __init__(self, ins, outs, bcast, ssem, rsem, lsem):
        n = len(ins)
        x, y, c = _me()
        me = 4 * x + 2 * y + c

        def src(a, dest):
            return ins[a] if bcast[a] else ins[a].at[dest]

        self.local = [pltpu.make_async_copy(src(a, me), outs[a].at[me], lsem.at[a]) for a in range(n)]
        self.sends, self.recvs = [], []
        for a in range(n):
            for d in range(1, N_DEV):
                peer, pid = _peer(d)
                self.sends.append(pltpu.make_async_remote_copy(
                    src_ref=src(a, pid), dst_ref=outs[a].at[me], send_sem=ssem.at[a, d - 1],
                    recv_sem=rsem.at[a, d - 1], device_id=peer, device_id_type=MESH))
                self.recvs.append(pltpu.make_async_remote_copy(
                    src_ref=src(a, pid), dst_ref=outs[a].at[pid], send_sem=ssem.at[a, d - 1],
                    recv_sem=rsem.at[a, d - 1], device_id=peer, device_id_type=MESH))

    def begin(self):
        for cp in self.local + self.sends:
            cp.start()

    def finish(self):
        for cp in self.recvs:
            cp.wait_recv()
        for cp in self.sends:
            cp.wait_send()
        for cp in self.local:
            cp.wait()


def _to_rows(cols, chunk):
    T, H = cols.shape
    return cols.T.reshape(H, T // chunk, chunk)


def _from_rows(rows):
    return rows.T


def _pad_cols(a, width):
    return jnp.pad(a, ((0, 0), (0, width - a.shape[1])))


def _local_step(x, tgt, p, late_weights=None, early_grads=None):
    T = x.shape[0]
    zb = lambda n: jnp.zeros((1, n), F32)
    gw = p["gdn_w_in"]
    g_wparts = [gw[:, 0:1024], gw[:, 1024:2048], gw[:, 2048:4096], gw[:, 4096:6144], _pad_cols(gw[:, 6144:6176], PAD_W)]
    nw0, nw1 = p["norm_w"][0:1], p["norm_w"][1:2]
    if late_weights is None:
        h0, (q_pre, k_pre, v_pre, z0, ab) = _norm_inproj(x, nw0, g_wparts, "gdn_inproj")
    else:
        comm, assemble = late_weights
        h0, (q_pre, k_pre, v_pre, z0, ab), gathered = _norm_inproj(x, nw0, g_wparts, "gdn_inproj", comm)
        p = dict(p, **assemble(gathered))
    gcw = p["gdn_conv_w"]
    cw_q, cw_k, cw_v = gcw[:, 0:1024], gcw[:, 1024:2048], gcw[:, 2048:4096]
    q = _conv_fwd(q_pre, cw_q, zb(1024), True, GDN_DK ** -0.5, "gdn_conv_q")
    k = _conv_fwd(k_pre, cw_k, zb(1024), True, 1.0, "gdn_conv_k")
    v = _conv_fwd(v_pre, cw_v, zb(2048), False, 1.0, "gdn_conv_v")
    braw = _to_rows(ab[:, 0:GDN_HV], GDN_CHUNK)
    araw = _to_rows(ab[:, GDN_HV:2 * GDN_HV], GDN_CHUNK)
    g_alog, g_dtb = p["gdn_a_log"].reshape(-1), p["gdn_dt_bias"].reshape(-1)
    g_u, g_w, g_pm, g_ti, g_rows, beta_rows, gc_rows = _gdn_prep(q, k, v, araw, braw, g_alog, g_dtb, "gdn_prep")
    o0, g_vn, g_sall = _gdn_state_fwd(q, k, g_u, g_w, g_pm, gc_rows, "gdn_state_fwd")
    x1 = _out_fwd(o0, z0, p["gdn_norm_w"], p["gdn_w_out"], x, GDN_DK, False, "gdn_out")
    sw = p["ssd_w_in"]
    s_wparts = [sw[:, 0:2048], sw[:, 2048:4096], sw[:, 4096:5120], sw[:, 5120:6144], _pad_cols(sw[:, 6144:6176], PAD_W)]
    h1, (z1, xs_pre, b_pre, c_pre, dtp) = _norm_inproj(x1, nw1, s_wparts, "ssd_inproj")
    scw, scb = p["ssd_conv_w"], p["ssd_conv_b"]
    xs = _conv_fwd(xs_pre, scw[:, 0:2048], scb[:, 0:2048], False, 1.0, "ssd_conv_x")
    bm = _conv_fwd(b_pre, scw[:, 2048:3072], scb[:, 2048:3072], False, 1.0, "ssd_conv_b")
    cm = _conv_fwd(c_pre, scw[:, 3072:4096], scb[:, 3072:4096], False, 1.0, "ssd_conv_c")
    dtraw = _to_rows(dtp[:, 0:SSD_H], SSD_CHUNK)
    s_alog, s_dtb, s_d = p["ssd_a_log"].reshape(-1), p["ssd_dt_bias"].reshape(-1), p["ssd_d"].reshape(-1)
    y1, s_sall, dt_rows = _ssd_scan_fwd(xs, bm, cm, dtraw, s_alog, s_dtb, s_d, "ssd_scan_fwd")
    x2 = _out_fwd(y1, z1, p["ssd_norm_w"], p["ssd_w_out"], x1, D_INNER // SSD_G, True, "ssd_out")
    dx2, d_fw, loss = _final_loss(x2, p["final_norm_w"].reshape(1, -1), tgt, "final_loss")
    dy1, dz1, d_snw, yn1 = _out_bwd(dx2, y1, z1, p["ssd_norm_w"], p["ssd_w_out"], D_INNER // SSD_G, True, "ssd_out_bwd")
    d_swout = _matmul_tn(yn1, dx2, "ssd_wout_grad")
    dxs, dbm, dcm, da_rows, ddt_rows, dd_rows = _ssd_scan_bwd(xs, bm, cm, dt_rows, s_sall, dy1, s_alog, s_d, "ssd_scan_bwd")
    col = lambda a: a.reshape(-1, 1)
    dtraw_g, d_salog, d_sdtb, d_sd = _ssd_gate_bwd(
        dtraw.reshape(SSD_H, T), dt_rows.reshape(SSD_H, T), da_rows.reshape(SSD_H, T),
        ddt_rows.reshape(SSD_H, T), dd_rows.reshape(SSD_H, T), col(s_alog), col(s_dtb), "ssd_gate_bwd")
    dxs_pre, dcw_x, dcb_x = _conv_bwd(xs_pre, scw[:, 0:2048], scb[:, 0:2048], dxs, False, 1.0, "ssd_conv_x_bwd")
    db_pre, dcw_b, dcb_b = _conv_bwd(b_pre, scw[:, 2048:3072], scb[:, 2048:3072], dbm, False, 1.0, "ssd_conv_b_bwd")
    dc_pre, dcw_c, dcb_c = _conv_bwd(c_pre, scw[:, 3072:4096], scb[:, 3072:4096], dcm, False, 1.0, "ssd_conv_c_bwd")
    ddtp = _pad_cols(_from_rows(dtraw_g), PAD_W)
    s_dparts = [dz1, dxs_pre, db_pre, dc_pre, ddtp]
    dx1, d_nw1 = _inproj_bwd(x1, nw1, s_dparts, s_wparts, dx2, "ssd_inproj_bwd")
    s_dw = [_matmul_tn(h1, d, "ssd_win_grad_%d" % n) for n, d in enumerate(s_dparts)]
    d_swin = jnp.concatenate(s_dw[:4] + [s_dw[4][:, 0:SSD_H]], axis=1)
    early_recv = None
    if early_grads is None:
        do0, dz0, d_gnw, yn0 = _out_bwd(dx1, o0, z0, p["gdn_norm_w"], p["gdn_w_out"], GDN_DK, False, "gdn_out_bwd")
    else:
        do0, dz0, d_gnw, yn0, early_recv = _out_bwd(dx1, o0, z0, p["gdn_norm_w"], p["gdn_w_out"], GDN_DK, False,
                                                    "gdn_out_bwd", early_grads(d_swin, d_swout))
    d_gwout = _matmul_tn(yn0, dx1, "gdn_wout_grad")
    g_dvn, g_dkd, g_dgl = _gdn_state_bwd(q, k, g_w, g_pm, g_vn, g_sall, gc_rows, do0, "gdn_state_bwd")
    dq, dk, dv, dg_rows, dbeta_rows = _gdn_local_bwd(q, k, v, gc_rows, beta_rows, g_ti, g_u, g_w, g_pm, g_vn, g_sall,
                                                     do0, g_dvn, g_dkd, g_dgl, "gdn_local_bwd")
    da_g, db_g, d_galog, d_gdtb = _gdn_gate_bwd(
        araw.reshape(GDN_HV, T), braw.reshape(GDN_HV, T), dg_rows.reshape(GDN_HV, T),
        dbeta_rows.reshape(GDN_HV, T), col(g_alog), col(g_dtb), "gdn_gate_bwd")
    dq_pre, dcw_q, _ = _conv_bwd(q_pre, cw_q, zb(1024), dq, True, GDN_DK ** -0.5, "gdn_conv_q_bwd")
    dk_pre, dcw_k, _ = _conv_bwd(k_pre, cw_k, zb(1024), dk, True, 1.0, "gdn_conv_k_bwd")
    dv_pre, dcw_v, _ = _conv_bwd(v_pre, cw_v, zb(2048), dv, False, 1.0, "gdn_conv_v_bwd")
    dab = _pad_cols(jnp.concatenate([_from_rows(db_g), _from_rows(da_g)], axis=1), PAD_W)
    g_dparts = [dq_pre, dk_pre, dv_pre, dz0, dab]
    dx0, d_nw0 = _inproj_bwd(x, nw0, g_dparts, g_wparts, dx1, "gdn_inproj_bwd")
    g_dw = [_matmul_tn(h0, d, "gdn_win_grad_%d" % n) for n, d in enumerate(g_dparts)]
    d_gwin = jnp.concatenate(g_dw[:4] + [g_dw[4][:, 0:2 * GDN_HV]], axis=1)
    grads = {
        "norm_w": jnp.concatenate([d_nw0, d_nw1], axis=0),
        "gdn_w_in": d_gwin,
        "gdn_conv_w": jnp.concatenate([dcw_q, dcw_k, dcw_v], axis=1),
        "gdn_a_log": d_galog.reshape(1, -1),
        "gdn_dt_bias": d_gdtb.reshape(1, -1),
        "gdn_norm_w": d_gnw,
        "gdn_w_out": d_gwout,
        "ssd_w_in": d_swin,
        "ssd_conv_w": jnp.concatenate([dcw_x, dcw_b, dcw_c], axis=1),
        "ssd_conv_b": jnp.concatenate([dcb_x, dcb_b, dcb_c], axis=1),
        "ssd_dt_bias": d_sdtb.reshape(1, -1),
        "ssd_a_log": d_salog.reshape(1, -1),
        "ssd_d": d_sd.reshape(1, -1),
        "ssd_norm_w": d_snw,
        "ssd_w_out": d_swout,
        "final_norm_w": d_fw,
    }
    if early_grads is not None:
        return loss, dx0, grads, early_recv
    return loss, dx0, grads


WEIGHTS = ["norm_w", "gdn_w_in", "gdn_conv_w", "gdn_a_log", "gdn_dt_bias", "gdn_norm_w", "gdn_w_out", "ssd_w_in",
           "ssd_conv_w", "ssd_conv_b", "ssd_dt_bias", "ssd_a_log", "ssd_d", "ssd_norm_w", "ssd_w_out", "final_norm_w"]
COL_SHARDED = ["gdn_w_in", "ssd_w_in"]
ROW_SHARDED = ["gdn_w_out", "ssd_w_out"]
SMALL_SHARDED = ["gdn_conv_w", "ssd_conv_w", "ssd_conv_b", "ssd_norm_w"]
REPLICATED = ["norm_w", "gdn_a_log", "gdn_dt_bias", "gdn_norm_w", "ssd_dt_bias", "ssd_a_log", "ssd_d", "final_norm_w"]


def _pack(arrs):
    return jnp.concatenate([a.reshape(-1) for a in arrs]).reshape(1, -1)


def _unpack(flat, shapes):
    out, pos = [], 0
    for s in shapes:
        n = 1
        for dim in s:
            n *= dim
        out.append(flat[pos:pos + n].reshape(s))
        pos += n
    return out


def _cols_to_shards(full):
    R, C = full.shape
    return full.reshape(R, N_DEV, C // N_DEV).transpose(1, 0, 2)


def _shards_to_cols(shards):
    n, R, c = shards.shape
    return shards.transpose(1, 0, 2).reshape(R, n * c)


def kernel(x, norm_w, gdn_w_in, gdn_conv_w, gdn_a_log, gdn_dt_bias, gdn_norm_w, gdn_w_out, ssd_w_in, ssd_conv_w, ssd_conv_b, ssd_dt_bias, ssd_a_log, ssd_d, ssd_norm_w, ssd_w_out, final_norm_w, loss_target, m_norm_w, m_gdn_w_in, m_gdn_conv_w, m_gdn_a_log, m_gdn_dt_bias, m_gdn_norm_w, m_gdn_w_out, m_ssd_w_in, m_ssd_conv_w, m_ssd_conv_b, m_ssd_dt_bias, m_ssd_a_log, m_ssd_d, m_ssd_norm_w, m_ssd_w_out, m_final_norm_w, v_norm_w, v_gdn_w_in, v_gdn_conv_w, v_gdn_a_log, v_gdn_dt_bias, v_gdn_norm_w, v_gdn_w_out, v_ssd_w_in, v_ssd_conv_w, v_ssd_conv_b, v_ssd_dt_bias, v_ssd_a_log, v_ssd_d, v_ssd_norm_w, v_ssd_w_out, v_final_norm_w):
    w = dict(norm_w=norm_w, gdn_w_in=gdn_w_in[0], gdn_conv_w=gdn_conv_w[0], gdn_a_log=gdn_a_log,
             gdn_dt_bias=gdn_dt_bias, gdn_norm_w=gdn_norm_w, gdn_w_out=gdn_w_out[0], ssd_w_in=ssd_w_in[0],
             ssd_conv_w=ssd_conv_w[0], ssd_conv_b=ssd_conv_b, ssd_dt_bias=ssd_dt_bias, ssd_a_log=ssd_a_log,
             ssd_d=ssd_d, ssd_norm_w=ssd_norm_w, ssd_w_out=ssd_w_out[0], final_norm_w=final_norm_w.reshape(1, -1))
    m = dict(norm_w=m_norm_w, gdn_w_in=m_gdn_w_in[0], gdn_conv_w=m_gdn_conv_w[0], gdn_a_log=m_gdn_a_log,
             gdn_dt_bias=m_gdn_dt_bias, gdn_norm_w=m_gdn_norm_w, gdn_w_out=m_gdn_w_out[0], ssd_w_in=m_ssd_w_in[0],
             ssd_conv_w=m_ssd_conv_w[0], ssd_conv_b=m_ssd_conv_b, ssd_dt_bias=m_ssd_dt_bias, ssd_a_log=m_ssd_a_log,
             ssd_d=m_ssd_d, ssd_norm_w=m_ssd_norm_w, ssd_w_out=m_ssd_w_out[0], final_norm_w=m_final_norm_w.reshape(1, -1))
    v = dict(norm_w=v_norm_w, gdn_w_in=v_gdn_w_in[0], gdn_conv_w=v_gdn_conv_w[0], gdn_a_log=v_gdn_a_log,
             gdn_dt_bias=v_gdn_dt_bias, gdn_norm_w=v_gdn_norm_w, gdn_w_out=v_gdn_w_out[0], ssd_w_in=v_ssd_w_in[0],
             ssd_conv_w=v_ssd_conv_w[0], ssd_conv_b=v_ssd_conv_b, ssd_dt_bias=v_ssd_dt_bias, ssd_a_log=v_ssd_a_log,
             ssd_d=v_ssd_d, ssd_norm_w=v_ssd_norm_w, ssd_w_out=v_ssd_w_out[0], final_norm_w=v_final_norm_w.reshape(1, -1))
    out_shapes = {n: a.shape for n, a in zip(
        WEIGHTS, [norm_w, gdn_w_in, gdn_conv_w, gdn_a_log, gdn_dt_bias, gdn_norm_w, gdn_w_out, ssd_w_in, ssd_conv_w,
                  ssd_conv_b, ssd_dt_bias, ssd_a_log, ssd_d, ssd_norm_w, ssd_w_out, final_norm_w])}

    small_shapes = [w[n].shape for n in SMALL_SHARDED]
    first = _exchange([_mx(w["gdn_w_in"]), _pack([w[n] for n in SMALL_SHARDED])], [True] * 2, "gather_first")
    full = dict(w)
    full["gdn_w_in"] = _shards_to_cols(first[0])
    small_all = [_unpack(first[1][s, 0], small_shapes) for s in range(N_DEV)]
    for idx, n in enumerate(SMALL_SHARDED):
        full[n] = jnp.concatenate([small_all[s][idx] for s in range(N_DEV)], axis=-1)
    late = ["gdn_w_out", "ssd_w_in", "ssd_w_out"]

    def assemble(gathered):
        return {"gdn_w_out": gathered[0].reshape(-1, D_MODEL), "ssd_w_in": _shards_to_cols(gathered[1]),
                "ssd_w_out": gathered[2].reshape(-1, D_MODEL)}

    def early_grads(d_ssd_w_in, d_ssd_w_out):
        return ([_cols_to_shards(d_ssd_w_in).astype(GRAD_WIRE_DTYPE),
                 d_ssd_w_out.reshape(N_DEV, -1, D_MODEL).astype(GRAD_WIRE_DTYPE)], [False] * 2)

    loss, dx, grads, ssd_recv = _local_step(x[0], loss_target[0], full,
                                            (([_mx(w[n]) for n in late], [True] * 3), assemble), early_grads)

    send_small = jnp.concatenate(
        [_cols_to_shards(grads[n]).reshape(N_DEV, -1) for n in SMALL_SHARDED], axis=1)[:, None, :]
    rep_shapes = [w[n].shape for n in REPLICATED]
    recv = _exchange(
        [_cols_to_shards(grads["gdn_w_in"]).astype(GRAD_WIRE_DTYPE),
         grads["gdn_w_out"].reshape(N_DEV, -1, D_MODEL).astype(GRAD_WIRE_DTYPE),
         send_small, _pack([grads[n] for n in REPLICATED])],
        [False] * 3 + [True], "exchange_grads")

    res = {}
    for n, parts in zip(["gdn_w_in", "gdn_w_out", "ssd_w_in", "ssd_w_out"], [recv[0], recv[1]] + list(ssd_recv)):
        res[n] = _adamw(parts, w[n], m[n], v[n], "adamw_" + n)
    small_res = _adamw(recv[2], *[_pack([t[n] for n in SMALL_SHARDED]) for t in (w, m, v)], "adamw_small")
    rep_res = _adamw(recv[3], *[_pack([t[n] for n in REPLICATED]) for t in (w, m, v)], "adamw_replicated")
    for k4 in range(4):
        for n, a in zip(SMALL_SHARDED, _unpack(small_res[k4][0], small_shapes)):
            res.setdefault(n, [None] * 4)[k4] = a
        for n, a in zip(REPLICATED, _unpack(rep_res[k4][0], rep_shapes)):
            res.setdefault(n, [None] * 4)[k4] = a

    loss = lax.psum(loss[0, 0], ("x", "y", "c"))
    outs = [loss, dx[None]]
    for k4 in range(4):
        outs += [res[n][k4].reshape(out_shapes[n]) for n in WEIGHTS]
    return tuple(outs)
```

```python
import jax
import jax.numpy as jnp
from jax import lax
from jax.experimental import pallas as pl
from jax.experimental.pallas import tpu as pltpu

F32 = jnp.float32
MXU_DTYPE = jnp.bfloat16
GRAD_WIRE_DTYPE = jnp.bfloat16
HI = lax.Precision.HIGHEST
EPS = 1e-6
VMEM_LIMIT_BYTES = 56 * 1024 * 1024
N_DEV = 8
MESH = pl.DeviceIdType.MESH

D_MODEL = 1024
CONV_K = 4
GDN_HV = 16
GDN_DK = 128
GDN_CHUNK = 64
SSD_H = 32
SSD_P = 64
SSD_N = 128
SSD_G = 8
SSD_R = SSD_H // SSD_G
SSD_CHUNK = 128
D_INNER = 2048
PAD_W = 128

ADAM_LR = 0.001
ADAM_B1 = 0.9
ADAM_B2 = 0.999
ADAM_EPS = 1e-08
ADAM_WD = 0.01
ADAM_STEP = 10


def _params(*sem):
    return pltpu.CompilerParams(dimension_semantics=sem, vmem_limit_bytes=VMEM_LIMIT_BYTES)


def _mx(a):
    return a.astype(MXU_DTYPE)


def _dot(a, b):
    return jnp.dot(_mx(a), _mx(b), preferred_element_type=F32)


def _dot_nt(a, b):
    return lax.dot_general(_mx(a), _mx(b), (((1,), (1,)), ((), ())), preferred_element_type=F32)


def _dot_tn(a, b):
    return lax.dot_general(_mx(a), _mx(b), (((0,), (0,)), ((), ())), preferred_element_type=F32)


def _dot_hi(a, b):
    return jnp.dot(a, b, precision=HI, preferred_element_type=F32)


def _sigmoid(x):
    return 0.5 * jnp.tanh(0.5 * x) + 0.5


def _silu(x):
    return x * _sigmoid(x)


def _dsilu(x):
    s = _sigmoid(x)
    return s * (1.0 + x * (1.0 - s))


def _softplus(x):
    return jnp.maximum(x, 0.0) + jnp.log1p(jnp.exp(-jnp.abs(x)))


def _col(r, eye):
    return jnp.sum(jnp.where(eye, r, 0.0), axis=1, keepdims=True)


def _row(c, eye):
    return jnp.sum(jnp.where(eye, c, 0.0), axis=0, keepdims=True)


def _col_bcast(r, n):
    return jnp.broadcast_to(r, (n, n)).T


def _masks(n):
    r = lax.broadcasted_iota(jnp.int32, (n, n), 0)
    c = lax.broadcasted_iota(jnp.int32, (n, n), 1)
    return r >= c, r > c, r == c, r, c


def _with_exchange(comm):
    arrs, bcast = comm if comm else ([], [])
    nc = len(arrs)
    anyspec = pl.BlockSpec(memory_space=pl.ANY)

    def wrap(compute, n_in, n_out):
        def body(*refs):
            cin, cout = refs[n_in:n_in + nc], refs[n_in + nc + n_out:n_in + 2 * nc + n_out]
            sems = refs[n_in + 2 * nc + n_out:n_in + 2 * nc + n_out + 3]
            rest = refs[:n_in] + refs[n_in + nc:n_in + nc + n_out] + refs[n_in + 2 * nc + n_out + (3 if nc else 0):]
            if nc:
                @pl.when(pl.program_id(0) == 0)
                def _():
                    _Exchange(cin, cout, bcast, *sems).begin()
            compute(*rest)
            if nc:
                @pl.when(pl.program_id(0) == pl.num_programs(0) - 1)
                def _():
                    _Exchange(cin, cout, bcast, *sems).finish()
        return body

    return dict(arrs=list(arrs), nc=nc, wrap=wrap, in_specs=[anyspec] * nc, out_specs=[anyspec] * nc,
                out_shape=_exchange_out_shapes(arrs, bcast), scratch=_exchange_semaphores(nc) if nc else [])


INPROJ_COL_BLOCK = 512


def _norm_inproj(x, nw, wparts, convs, name, comm=None):
    T = x.shape[0]
    tt = min(T, 256)
    n = len(wparts)
    ck = [k for k in range(n) if convs[k] is not None]
    nconv = len(ck)
    ex = _with_exchange(comm)

    def compute(x_ref, nw_ref, *refs):
        w_refs, cw_refs = refs[:n], refs[n:n + 2 * nconv]
        h_ref, o_refs = refs[n + 2 * nconv], refs[n + 2 * nconv + 1:2 * n + 2 * nconv + 1]
        post_refs = refs[2 * n + 2 * nconv + 1:2 * n + 3 * nconv + 1]
        p_refs = refs[2 * n + 3 * nconv + 1:]
        xv = x_ref[...]
        r = lax.rsqrt(jnp.mean(xv * xv, axis=-1, keepdims=True) + EPS)
        h = _mx(xv * r * nw_ref[...])
        h_ref[...] = h
        for m in range(nconv):
            @pl.when(pl.program_id(0) == 0)
            def _():
                p_refs[m][0:HALO, :] = jnp.zeros((HALO, p_refs[m].shape[1]), F32)

        def conv_block(k, c0, cw):
            m = ck.index(k)
            _, _, l2, scale = convs[k]
            cw_ref, cb_ref, out_ref, P = cw_refs[2 * m], cw_refs[2 * m + 1], post_refs[m], p_refs[m]
            cs = slice(c0, c0 + cw)
            acc = cb_ref[:, cs] + cw_ref[0:1, cs] * P[pl.ds(HALO - 3, tt), cs]
            for j in range(1, CONV_K):
                acc = acc + cw_ref[j:j + 1, cs] * P[pl.ds(HALO - 3 + j, tt), cs]
            s = _silu(acc)
            if l2:
                sls = [slice(g0, g0 + GDN_DK) for g0 in range(0, cw, GDN_DK)]
                rr = [lax.rsqrt(jnp.sum(s[:, sl] * s[:, sl], axis=-1, keepdims=True) + EPS) for sl in sls]
                for sl, rg in zip(sls, rr):
                    out_ref[:, c0 + sl.start:c0 + sl.stop] = s[:, sl] * rg * scale
            else:
                out_ref[:, cs] = s
            P[0:HALO, cs] = P[tt:tt + HALO, cs]

        pending = None
        for k in range(n):
            for c0 in range(0, widths[k], INPROJ_COL_BLOCK):
                cw = min(INPROJ_COL_BLOCK, widths[k] - c0)
                pre = jnp.dot(h, w_refs[k][:, c0:c0 + cw], preferred_element_type=F32)
                o_refs[k][:, c0:c0 + cw] = pre
                if convs[k] is not None:
                    p_refs[ck.index(k)][HALO:HALO + tt, c0:c0 + cw] = pre
                if pending is not None:
                    conv_block(*pending)
                pending = (k, c0, cw) if convs[k] is not None else None
        if pending is not None:
            conv_block(*pending)

    row = lambda width: pl.BlockSpec((tt, width), lambda i: (i, 0))
    full = lambda a: pl.BlockSpec(a.shape, lambda i: (0, 0))
    once = lambda a: pl.BlockSpec(a.shape, lambda i: (0, 0), pipeline_mode=pl.Buffered(1))
    conv_args = [a for k in ck for a in convs[k][:2]]
    widths = [w.shape[1] for w in wparts]
    outs = pl.pallas_call(
        ex["wrap"](compute, 2 + n + 2 * nconv, 1 + n + nconv), grid=(T // tt,),
        in_specs=[row(D_MODEL), full(nw)] + [once(w) for w in wparts] + [full(a) for a in conv_args] + ex["in_specs"],
        out_specs=[row(D_MODEL)] + [row(wd) for wd in widths] + [row(widths[k]) for k in ck] + ex["out_specs"],
        out_shape=[jax.ShapeDtypeStruct((T, D_MODEL), MXU_DTYPE)]
        + [jax.ShapeDtypeStruct((T, wd), F32) for wd in widths]
        + [jax.ShapeDtypeStruct((T, widths[k]), F32) for k in ck] + ex["out_shape"],
        scratch_shapes=ex["scratch"] + [pltpu.VMEM((HALO + tt, widths[k]), F32) for k in ck],
        compiler_params=_params("arbitrary"), name=name,
    )(x, nw, *wparts, *conv_args, *ex["arrs"])
    outs = list(outs)
    res = (outs[0], outs[1:1 + n], outs[1 + n:1 + n + nconv])
    return res + (outs[1 + n + nconv:],) if comm else res


def _inproj_bwd(x, nw, dparts, wparts, dres, name):
    T = x.shape[0]
    tt = min(T, 256)
    n = len(wparts)

    def body(x_ref, nw_ref, dres_ref, *refs):
        d_refs, w_refs, dx_ref, dnw_ref = refs[:n], refs[n:2 * n], refs[2 * n], refs[2 * n + 1]

        @pl.when(pl.program_id(0) == 0)
        def _():
            dnw_ref[...] = jnp.zeros_like(dnw_ref)

        dh = _dot_nt(d_refs[0][...], w_refs[0][...])
        for d_ref, w_ref in zip(d_refs[1:], w_refs[1:]):
            dh = dh + _dot_nt(d_ref[...], w_ref[...])
        xv = x_ref[...]
        r = lax.rsqrt(jnp.mean(xv * xv, axis=-1, keepdims=True) + EPS)
        xh = xv * r
        dnw_ref[...] += jnp.sum(dh * xh, axis=0, keepdims=True)
        dxn = dh * nw_ref[...]
        dx_ref[...] = dres_ref[...] + r * (dxn - xh * jnp.mean(dxn * xh, axis=-1, keepdims=True))

    row = lambda width: pl.BlockSpec((tt, width), lambda i: (i, 0))
    full = lambda a: pl.BlockSpec(a.shape, lambda i: (0, 0))
    return pl.pallas_call(
        body, grid=(T // tt,),
        in_specs=[row(D_MODEL), full(nw), row(D_MODEL)] + [row(d.shape[1]) for d in dparts]
        + [full(w) for w in wparts],
        out_specs=[row(D_MODEL), pl.BlockSpec((1, D_MODEL), lambda i: (0, 0))],
        out_shape=[jax.ShapeDtypeStruct((T, D_MODEL), F32), jax.ShapeDtypeStruct((1, D_MODEL), F32)],
        compiler_params=_params("arbitrary"), name=name,
    )(x, nw, dres, *dparts, *wparts)


def _matmul_tn(a, b, name):
    T, K = a.shape
    N = b.shape[1]
    tt = min(T, 512)
    tn = min(N, 1024)

    def body(a_ref, b_ref, o_ref):
        @pl.when(pl.program_id(1) == 0)
        def _():
            o_ref[...] = jnp.zeros_like(o_ref)

        o_ref[...] += _dot_tn(a_ref[...], b_ref[...])

    return pl.pallas_call(
        body, grid=(N // tn, T // tt),
        in_specs=[pl.BlockSpec((tt, K), lambda n, t: (t, 0)), pl.BlockSpec((tt, tn), lambda n, t: (t, n))],
        out_specs=pl.BlockSpec((K, tn), lambda n, t: (0, n)),
        out_shape=jax.ShapeDtypeStruct((K, N), F32),
        compiler_params=_params("parallel", "arbitrary"), name=name,
    )(a, b)


def _out_fwd(o, z, w, wout, xres, gs, gate_first, name):
    T = o.shape[0]
    tt = min(T, 256)
    wide = w.shape[1] == D_INNER

    def body(o_ref, z_ref, w_ref, wout_ref, x_ref, out_ref, yn):
        for g0 in range(0, D_INNER, gs):
            sl = slice(g0, g0 + gs)
            og, zg = o_ref[:, sl], z_ref[:, sl]
            wg = w_ref[:, sl] if wide else w_ref[...]
            if gate_first:
                u = og * _silu(zg)
                r = lax.rsqrt(jnp.mean(u * u, axis=-1, keepdims=True) + EPS)
                yn[:, sl] = _mx(u * r * wg)
            else:
                r = lax.rsqrt(jnp.mean(og * og, axis=-1, keepdims=True) + EPS)
                yn[:, sl] = _mx(og * r * wg * _silu(zg))
        out_ref[...] = x_ref[...] + jnp.dot(yn[...], wout_ref[...], preferred_element_type=F32)

    row = lambda width: pl.BlockSpec((tt, width), lambda i: (i, 0))
    full = lambda a: pl.BlockSpec(a.shape, lambda i: (0, 0))
    return pl.pallas_call(
        body, grid=(T // tt,),
        in_specs=[row(D_INNER), row(D_INNER), full(w), full(wout), row(D_MODEL)],
        out_specs=row(D_MODEL),
        out_shape=jax.ShapeDtypeStruct((T, D_MODEL), F32),
        scratch_shapes=[pltpu.VMEM((tt, D_INNER), MXU_DTYPE)],
        compiler_params=_params("parallel"), name=name,
    )(o, z, w, wout, xres)


def _out_bwd(dx, o, z, w, wout, gs, gate_first, name, comm=None):
    T = o.shape[0]
    tt = min(T, 256)
    wide = w.shape[1] == D_INNER

    def body(dx_ref, o_ref, z_ref, w_ref, wout_ref, do_ref, dz_ref, dw_ref, yn_ref):
        @pl.when(pl.program_id(0) == 0)
        def _():
            dw_ref[...] = jnp.zeros_like(dw_ref)

        dyn = _dot_nt(dx_ref[...], wout_ref[...])
        dw_acc = jnp.zeros((1, gs), F32)
        for g0 in range(0, D_INNER, gs):
            sl = slice(g0, g0 + gs)
            og, zg, dg = o_ref[:, sl], z_ref[:, sl], dyn[:, sl]
            wg = w_ref[:, sl] if wide else w_ref[...]
            sz = _silu(zg)
            if gate_first:
                u = og * sz
                r = lax.rsqrt(jnp.mean(u * u, axis=-1, keepdims=True) + EPS)
                uh = u * r
                yn_ref[:, sl] = _mx(uh * wg)
                dw_g = jnp.sum(dg * uh, axis=0, keepdims=True)
                duh = dg * wg
                du = r * (duh - uh * jnp.mean(duh * uh, axis=-1, keepdims=True))
                do_ref[:, sl] = du * sz
                dz_ref[:, sl] = _mx(du * og * _dsilu(zg))
            else:
                r = lax.rsqrt(jnp.mean(og * og, axis=-1, keepdims=True) + EPS)
                oh = og * r
                yn_ref[:, sl] = _mx(oh * wg * sz)
                dw_g = jnp.sum(dg * oh * sz, axis=0, keepdims=True)
                doh = dg * wg * sz
                dz_ref[:, sl] = _mx(dg * oh * wg * _dsilu(zg))
                do_ref[:, sl] = r * (doh - oh * jnp.mean(doh * oh, axis=-1, keepdims=True))
            if wide:
                dw_ref[:, sl] += dw_g
            else:
                dw_acc = dw_acc + dw_g
        if not wide:
            dw_ref[...] += dw_acc

    row = lambda width: pl.BlockSpec((tt, width), lambda i: (i, 0))
    full = lambda a: pl.BlockSpec(a.shape, lambda i: (0, 0))
    ex = _with_exchange(comm)
    outs = pl.pallas_call(
        ex["wrap"](body, 5, 4), grid=(T // tt,),
        in_specs=[row(D_MODEL), row(D_INNER), row(D_INNER), full(w), full(wout)] + ex["in_specs"],
        out_specs=[row(D_INNER), row(D_INNER), full(w), row(D_INNER)] + ex["out_specs"],
        out_shape=[jax.ShapeDtypeStruct((T, D_INNER), F32), jax.ShapeDtypeStruct((T, D_INNER), MXU_DTYPE),
                   jax.ShapeDtypeStruct(w.shape, F32), jax.ShapeDtypeStruct((T, D_INNER), MXU_DTYPE)]
        + ex["out_shape"],
        scratch_shapes=ex["scratch"],
        compiler_params=_params("arbitrary"), name=name,
    )(dx, o, z, w, wout, *ex["arrs"])
    outs = list(outs)
    return outs[:4] + ([outs[4:]] if comm else [])


HALO = 8
CONV_STRIP = 16


def _conv_bwd(pre, w, b, dpost, l2, scale, name):
    T, C = pre.shape
    tt = min(T, 512)
    tc = min(C, 1024 if l2 else 512)
    strip = tt if l2 else CONV_STRIP
    nT = T // tt
    ext = tt + HALO

    def body(pre_ref, hp_ref, hn_ref, dpost_ref, dn_ref, w_ref, b_ref, dpre_ref, dw_ref, db_ref, P, Q):
        i = pl.program_id(1)

        @pl.when(i == 0)
        def _():
            dw_ref[...] = jnp.zeros_like(dw_ref)
            db_ref[...] = jnp.zeros_like(db_ref)

        P[0:HALO, :] = jnp.where(i > 0, hp_ref[...], 0.0)
        P[HALO:HALO + tt, :] = pre_ref[...]
        P[HALO + tt:HALO + ext, :] = hn_ref[...]
        wj = [w_ref[j:j + 1, :] for j in range(CONV_K)]
        bias = b_ref[...]
        keep_next = jnp.where(i < nT - 1, 1.0, 0.0)
        fold = lambda a: jnp.sum(a.reshape(strip // 8, 8, tc), axis=0)
        dw_acc = [jnp.zeros((8, tc), F32) for _ in range(CONV_K)]
        db_acc = jnp.zeros((8, tc), F32)
        for r0 in list(range(0, tt, strip)) + [tt]:
            n = strip if r0 < tt else HALO
            taps = [P[pl.ds(HALO - 3 + j + r0, n), :] for j in range(CONV_K)]
            cpre = bias + wj[0] * taps[0]
            for j in range(1, CONV_K):
                cpre = cpre + wj[j] * taps[j]
            dy = dpost_ref[r0:r0 + n, :] if r0 < tt else dn_ref[...] * keep_next
            sg = _sigmoid(cpre)
            ds_c = sg * (1.0 + cpre * (1.0 - sg))
            if l2:
                s = cpre * sg
                sls = [slice(g0, g0 + GDN_DK) for g0 in range(0, tc, GDN_DK)]
                rr = [lax.rsqrt(jnp.sum(s[:, sl] * s[:, sl], axis=-1, keepdims=True) + EPS) for sl in sls]
                yh = [s[:, sl] * r for sl, r in zip(sls, rr)]
                pr = [jnp.sum(dy[:, sl] * y, axis=-1, keepdims=True) for sl, y in zip(sls, yh)]
                for sl, r, y, p in zip(sls, rr, yh, pr):
                    Q[r0:r0 + n, sl] = (scale * r) * (dy[:, sl] - y * p) * ds_c[:, sl]
                dyc = Q[r0:r0 + n, :]
            else:
                dyc = dy * ds_c
                Q[r0:r0 + n, :] = dyc
            if r0 < tt:
                for j in range(CONV_K):
                    dw_acc[j] = dw_acc[j] + fold(dyc * taps[j])
                db_acc = db_acc + fold(dyc)
        for r0 in range(0, tt, strip):
            dpre = wj[0] * Q[pl.ds(3 + r0, strip), :]
            for j in range(1, CONV_K):
                dpre = dpre + wj[j] * Q[pl.ds(3 - j + r0, strip), :]
            dpre_ref[r0:r0 + strip, :] = _mx(dpre)
        for j in range(CONV_K):
            dw_ref[j:j + 1, :] += jnp.sum(dw_acc[j], axis=0, keepdims=True)
        db_ref[...] += jnp.sum(db_acc, axis=0, keepdims=True)

    tile = pl.BlockSpec((tt, tc), lambda j, i: (i, j))
    prev = pl.BlockSpec((HALO, tc), lambda j, i: (jnp.maximum(i * (tt // HALO) - 1, 0), j))
    nxt = pl.BlockSpec((HALO, tc), lambda j, i: (jnp.minimum((i + 1) * (tt // HALO), T // HALO - 1), j))
    return pl.pallas_call(
        body, grid=(C // tc, nT),
        in_specs=[tile, prev, nxt, tile, nxt,
                  pl.BlockSpec((CONV_K, tc), lambda j, i: (0, j)), pl.BlockSpec((1, tc), lambda j, i: (0, j))],
        out_specs=[tile, pl.BlockSpec((CONV_K, tc), lambda j, i: (0, j)), pl.BlockSpec((1, tc), lambda j, i: (0, j))],
        out_shape=[jax.ShapeDtypeStruct((T, C), MXU_DTYPE), jax.ShapeDtypeStruct((CONV_K, C), F32),
                   jax.ShapeDtypeStruct((1, C), F32)],
        scratch_shapes=[pltpu.VMEM((HALO + ext, tc), F32), pltpu.VMEM((ext, tc), F32)],
        compiler_params=_params("parallel", "arbitrary"), name=name,
    )(pre, pre, pre, dpost, dpost, w, b)


GDN_LOCKSTEP_CHUNKS = 16
GDN_SCAN_HEADS = 16


def _inv_unit_lower_many(nms, eye, n):
    xs = [jnp.where(eye, 1.0, 0.0) - nm for nm in nms]
    ps = list(nms)
    k = 2
    while k < n:
        ps = [_dot(p, p) for p in ps]
        xs = [x + _dot(x, p) for x, p in zip(xs, ps)]
        k *= 2
    return xs


def _gdn_prep(q, k, v, araw, braw, alog, dtb, name):
    T = q.shape[0]
    C = GDN_CHUNK
    tt = min(T, 1024)
    cpt, nC = tt // C, T // C
    grp = min(cpt, GDN_LOCKSTEP_CHUNKS)

    def body(alog_ref, dtb_ref, q_ref, k_ref, v_ref, a_ref, b_ref,
             u_ref, w_ref, pm_ref, ti_ref, g_ref, beta_ref, gc_ref):
        j = pl.program_id(0)
        tri, strict, eye, r_i, c_i = _masks(C)
        upper = jnp.where(r_i <= c_i, 1.0, 0.0)
        gcs, bts = [], []
        for hh in range(2):
            h = 2 * j + hh
            g = -jnp.exp(alog_ref[h]) * _softplus(a_ref[hh] + dtb_ref[h])
            bt = _sigmoid(b_ref[hh])
            gc = _dot_hi(g, upper)
            g_ref[hh], beta_ref[hh], gc_ref[hh] = g, bt, gc
            gcs.append(gc)
            bts.append(bt)
        for c0 in range(0, cpt, grp):
            cs = list(range(c0, c0 + grp))
            inst = [(c, hh) for c in cs for hh in range(2)]
            rows = {c: slice(c * C, (c + 1) * C) for c in cs}
            qc = {c: q_ref[rows[c], :] for c in cs}
            kc = {c: k_ref[rows[c], :] for c in cs}
            kk = {c: _dot_nt(kc[c], kc[c]) for c in cs}
            qk = {c: _dot_nt(qc[c], kc[c]) for c in cs}
            gcr = [gcs[hh][c:c + 1, :] for c, hh in inst]
            gcc = [_col(r, eye) for r in gcr]
            bc = [_col(bts[hh][c:c + 1, :], eye) for c, hh in inst]
            lm = [jnp.exp(jnp.where(tri, cc - r, -1e30)) for cc, r in zip(gcc, gcr)]
            nm = [jnp.where(strict, kk[c] * b * l, 0.0) for (c, hh), b, l in zip(inst, bc, lm)]
            tinv = _inv_unit_lower_many(nm, eye, C)
            rhs = [jnp.concatenate([v_ref[rows[c], hh * GDN_DK:(hh + 1) * GDN_DK] * b, kc[c] * (b * jnp.exp(cc))], axis=1)
                   for (c, hh), b, cc in zip(inst, bc, gcc)]
            sol = [_dot(t, r) for t, r in zip(tinv, rhs)]
            for (c, hh), s, t, l in zip(inst, sol, tinv, lm):
                hs = slice(hh * GDN_DK, (hh + 1) * GDN_DK)
                u_ref[rows[c], hs] = s[:, :GDN_DK]
                w_ref[rows[c], hs] = _mx(s[:, GDN_DK:])
                pm_ref[hh, c] = _mx(jnp.where(tri, qk[c] * l, 0.0))
                ti_ref[hh, c] = _mx(t)

    smem = pl.BlockSpec(memory_space=pltpu.SMEM)
    rows_spec = pl.BlockSpec((2, cpt, C), lambda j, i: (j, i, 0))
    qk_spec = pl.BlockSpec((tt, GDN_DK), lambda j, i: (i, j))
    v_spec = pl.BlockSpec((tt, 2 * GDN_DK), lambda j, i: (i, j))
    cc_spec = pl.BlockSpec((2, cpt, C, C), lambda j, i: (j, i, 0, 0))
    rows_shape = jax.ShapeDtypeStruct((GDN_HV, nC, C), F32)
    cc_shape = jax.ShapeDtypeStruct((GDN_HV, nC, C, C), MXU_DTYPE)
    return pl.pallas_call(
        body, grid=(GDN_HV // 2, T // tt),
        in_specs=[smem, smem, qk_spec, qk_spec, v_spec, rows_spec, rows_spec],
        out_specs=[v_spec, v_spec, cc_spec, cc_spec, rows_spec, rows_spec, rows_spec],
        out_shape=[jax.ShapeDtypeStruct((T, D_INNER), F32), jax.ShapeDtypeStruct((T, D_INNER), MXU_DTYPE),
                   cc_shape, cc_shape, rows_shape, rows_shape, rows_shape],
        compiler_params=_params("parallel", "parallel"), name=name,
    )(alog, dtb, q, k, v, araw, braw)


def _gdn_decays(gc_ref, h, c, eye, C):
    gcr = gc_ref[h, pl.ds(c, 1), :]
    gcc = _col(gcr, eye)
    glast = gcr[:, C - 1:C]
    return jnp.exp(gcc), jnp.exp(glast - gcc), jnp.exp(glast)


def _gdn_state_fwd(q, k, u, w, pm, gc, name):
    T = q.shape[0]
    C = GDN_CHUNK
    HG = GDN_SCAN_HEADS
    tt = min(T, 512)
    cpt, nC = tt // C, T // C

    def body(q_ref, k_ref, u_ref, w_ref, pm_ref, gc_ref, o_ref, vn_ref, sall_ref, S):
        @pl.when(pl.program_id(1) == 0)
        def _():
            S[...] = jnp.zeros_like(S)

        eye = _masks(C)[2]
        heads = list(range(HG))

        def chunk(c, carry):
            rows = pl.ds(pl.multiple_of(c * C, C), C)
            hs = [slice(h * GDN_DK, (h + 1) * GDN_DK) for h in heads]
            qs = [slice((h // 2) * GDN_DK, (h // 2 + 1) * GDN_DK) for h in heads]
            dec = [_gdn_decays(gc_ref, h, c, eye, C) for h in heads]
            sv = [S[h] for h in heads]
            for h in heads:
                sall_ref[h, c] = _mx(sv[h])
            ws = [_dot(w_ref[rows, hs[h]], sv[h]) for h in heads]
            qsv = [_dot(q_ref[rows, qs[h]], sv[h]) for h in heads]
            vn = [u_ref[rows, hs[h]] - ws[h] for h in heads]
            pv = [_dot(pm_ref[h, c], vn[h]) for h in heads]
            kv = [_dot_tn(k_ref[rows, qs[h]], vn[h] * dec[h][1]) for h in heads]
            for h in heads:
                vn_ref[rows, hs[h]] = _mx(vn[h])
                o_ref[rows, hs[h]] = qsv[h] * dec[h][0] + pv[h]
                S[h] = sv[h] * dec[h][2] + kv[h]
            return carry

        lax.fori_loop(0, cpt, chunk, 0)

    qk_spec = pl.BlockSpec((tt, HG // 2 * GDN_DK), lambda g, i: (i, g))
    v_spec = pl.BlockSpec((tt, HG * GDN_DK), lambda g, i: (i, g))
    return pl.pallas_call(
        body, grid=(GDN_HV // HG, T // tt),
        in_specs=[qk_spec, qk_spec, v_spec, v_spec,
                  pl.BlockSpec((HG, cpt, C, C), lambda g, i: (g, i, 0, 0)),
                  pl.BlockSpec((HG, cpt, C), lambda g, i: (g, i, 0))],
        out_specs=[v_spec, v_spec, pl.BlockSpec((HG, cpt, GDN_DK, GDN_DK), lambda g, i: (g, i, 0, 0))],
        out_shape=[jax.ShapeDtypeStruct((T, D_INNER), F32), jax.ShapeDtypeStruct((T, D_INNER), MXU_DTYPE),
                   jax.ShapeDtypeStruct((GDN_HV, nC, GDN_DK, GDN_DK), MXU_DTYPE)],
        scratch_shapes=[pltpu.VMEM((HG, GDN_DK, GDN_DK), F32)],
        compiler_params=_params("parallel", "arbitrary"), name=name,
    )(q, k, u, w, pm, gc)


def _gdn_state_bwd(q, k, w, pm, vn, sall, gc, do, name):
    T = q.shape[0]
    C = GDN_CHUNK
    HG = GDN_SCAN_HEADS
    tt = min(T, 512)
    cpt, nC, nT = tt // C, T // C, T // tt

    def body(q_ref, k_ref, w_ref, pm_ref, vn_ref, sall_ref, gc_ref, do_ref, dvn_ref, dkd_ref, dgl_ref, dS):
        @pl.when(pl.program_id(1) == 0)
        def _():
            dS[...] = jnp.zeros_like(dS)

        eye = _masks(C)[2]
        heads = list(range(HG))

        def chunk(ci, carry):
            c = cpt - 1 - ci
            rows = pl.ds(pl.multiple_of(c * C, C), C)
            hs = [slice(h * GDN_DK, (h + 1) * GDN_DK) for h in heads]
            qs = [slice((h // 2) * GDN_DK, (h // 2 + 1) * GDN_DK) for h in heads]
            dec = [_gdn_decays(gc_ref, h, c, eye, C) for h in heads]
            dsn = [dS[h] for h in heads]
            doc = [do_ref[rows, hs[h]] for h in heads]
            kds = [_dot(k_ref[rows, qs[h]], dsn[h]) for h in heads]
            pdo = [_dot_tn(pm_ref[h, c], doc[h]) for h in heads]
            dkd = [_dot_nt(vn_ref[rows, hs[h]], dsn[h]) for h in heads]
            qdo = [_dot_tn(q_ref[rows, qs[h]], doc[h] * dec[h][0]) for h in heads]
            dvn = [pdo[h] + kds[h] * dec[h][1] for h in heads]
            wdv = [_dot_tn(w_ref[rows, hs[h]], dvn[h]) for h in heads]
            for h in heads:
                dgl = jnp.sum(jnp.sum(dsn[h] * sall_ref[h, c].astype(F32), axis=0, keepdims=True), axis=1, keepdims=True)
                dgl_ref[h, pl.ds(c, 1), :] = jnp.broadcast_to(dgl, (1, C))
                dvn_ref[rows, hs[h]] = dvn[h]
                dkd_ref[rows, hs[h]] = dkd[h]
                dS[h] = dsn[h] * dec[h][2] + qdo[h] - wdv[h]
            return carry

        lax.fori_loop(0, cpt, chunk, 0)

    rev = lambda i: nT - 1 - i
    qk_spec = pl.BlockSpec((tt, HG // 2 * GDN_DK), lambda g, i: (rev(i), g))
    v_spec = pl.BlockSpec((tt, HG * GDN_DK), lambda g, i: (rev(i), g))
    rows_spec = pl.BlockSpec((HG, cpt, C), lambda g, i: (g, rev(i), 0))
    return pl.pallas_call(
        body, grid=(GDN_HV // HG, nT),
        in_specs=[qk_spec, qk_spec, v_spec, pl.BlockSpec((HG, cpt, C, C), lambda g, i: (g, rev(i), 0, 0)), v_spec,
                  pl.BlockSpec((HG, cpt, GDN_DK, GDN_DK), lambda g, i: (g, rev(i), 0, 0)), rows_spec, v_spec],
        out_specs=[v_spec, v_spec, rows_spec],
        out_shape=[jax.ShapeDtypeStruct((T, D_INNER), F32), jax.ShapeDtypeStruct((T, D_INNER), F32),
                   jax.ShapeDtypeStruct((GDN_HV, nC, C), F32)],
        scratch_shapes=[pltpu.VMEM((HG, GDN_DK, GDN_DK), F32)],
        compiler_params=_params("parallel", "arbitrary"), name=name,
    )(q, k, w, pm, vn, sall, gc, do)


def _gdn_local_bwd(q, k, v, gc, beta, tinv, u, w, pm, vn, sall, do, dvn, dkd, dgl, name):
    T = q.shape[0]
    C = GDN_CHUNK
    tt = min(T, 1024)
    cpt, nC = tt // C, T // C
    grp = min(cpt, GDN_LOCKSTEP_CHUNKS)

    def body(q_ref, k_ref, v_ref, gc_ref, b_ref, ti_ref, u_ref, w_ref, pm_ref, vn_ref, sall_ref, do_ref,
             dvn_ref, dkd_ref, dgl_ref, dq_ref, dk_ref, dv_ref, dg_ref, dbeta_ref, dgc_s):
        tri, strict, eye, r_i, c_i = _masks(C)
        lower = jnp.where(r_i >= c_i, 1.0, 0.0)
        lane = lax.broadcasted_iota(jnp.int32, (1, C), 1)
        rsum = lambda a: jnp.sum(a, axis=1, keepdims=True)
        for c0 in range(0, cpt, grp):
            cs = list(range(c0, c0 + grp))
            inst = [(c, hh) for c in cs for hh in range(2)]
            n = len(inst)
            rows = {c: slice(c * C, (c + 1) * C) for c in cs}
            hsl = [slice(hh * GDN_DK, (hh + 1) * GDN_DK) for c, hh in inst]
            qc = {c: q_ref[rows[c], :] for c in cs}
            kc = {c: k_ref[rows[c], :] for c in cs}
            kk = {c: _dot_nt(kc[c], kc[c]) for c in cs}
            gcr = [gc_ref[hh, c:c + 1, :] for c, hh in inst]
            gcc = [_col(r, eye) for r in gcr]
            bc = [_col(b_ref[hh, c:c + 1, :], eye) for c, hh in inst]
            lm = [jnp.exp(jnp.where(tri, cc - r, -1e30)) for cc, r in zip(gcc, gcr)]
            e_c = [jnp.exp(cc) for cc in gcc]
            el_c = [jnp.exp(r[:, C - 1:C] - cc) for cc, r in zip(gcc, gcr)]
            gl = [jnp.exp(r[:, C - 1:C]) for r in gcr]
            doc = [do_ref[rows[c], hsl[i]] for i, (c, hh) in enumerate(inst)]
            dvn = [dvn_ref[rows[c], hsl[i]] for i, (c, hh) in enumerate(inst)]
            sv = [sall_ref[hh, c] for c, hh in inst]
            aa = [_dot_nt(jnp.concatenate([_mx(doc[i]), _mx(dvn[i])], axis=0), sv[i]) for i in range(n)]
            dpm = [jnp.where(tri, _dot_nt(doc[i], vn_ref[rows[c], hsl[i]]), 0.0) for i, (c, hh) in enumerate(inst)]
            dqd = [a[:C] for a in aa]
            drhs = [_dot_tn(ti_ref[hh, c], jnp.concatenate([dvn[i], -aa[i][C:]], axis=1))
                    for i, (c, hh) in enumerate(inst)]
            sol = [jnp.concatenate([_mx(u_ref[rows[c], hsl[i]]), w_ref[rows[c], hsl[i]]], axis=1)
                   for i, (c, hh) in enumerate(inst)]
            dnm = [-jnp.where(strict, _dot_nt(drhs[i], sol[i]), 0.0) for i in range(n)]
            dkk = [dnm[i] * bc[i] * lm[i] for i in range(n)]
            dqk = [dpm[i] * lm[i] for i in range(n)]
            dq1 = [_dot(dqk[i], kc[c]) for i, (c, hh) in enumerate(inst)]
            dk1 = [_dot(dkk[i], kc[c]) for i, (c, hh) in enumerate(inst)]
            dk2 = [_dot_tn(dkk[i], kc[c]) for i, (c, hh) in enumerate(inst)]
            dk3 = [_dot_tn(dqk[i], qc[c]) for i, (c, hh) in enumerate(inst)]
            dq_acc = {c: jnp.zeros((C, GDN_DK), F32) for c in cs}
            dk_acc = {c: jnp.zeros((C, GDN_DK), F32) for c in cs}
            for i, (c, hh) in enumerate(inst):
                k_, q_, v_ = kc[c], qc[c], v_ref[rows[c], hsl[i]]
                dvb, dkbe = drhs[i][:, :GDN_DK], drhs[i][:, GDN_DK:]
                dkd = dkd_ref[rows[c], hsl[i]]
                kb = k_ * bc[i]
                dkb = dkbe * e_c[i]
                del_el = dkd * k_ * el_c[i]
                dbc = rsum(dnm[i] * kk[c] * lm[i]) + rsum(dkb * k_ + dvb * v_)
                dq_acc[c] = dq_acc[c] + dq1[i] + dqd[i] * e_c[i]
                dk_acc[c] = dk_acc[c] + dk1[i] + dk2[i] + dk3[i] + dkd * el_c[i] + dkb * bc[i]
                dv_ref[rows[c], hsl[i]] = dvb * bc[i]
                nm = jnp.where(strict, kk[c] * bc[i] * lm[i], 0.0)
                gm = dnm[i] * nm + dpm[i] * pm_ref[hh, c].astype(F32)
                dgc_col = rsum(gm) + rsum((dkbe * kb + dqd[i] * q_) * e_c[i] - del_el)
                dglast = (jnp.sum(jnp.sum(del_el, axis=0, keepdims=True), axis=1, keepdims=True)
                          + dgl_ref[hh, c:c + 1, 0:1] * gl[i])
                dgc_s[hh, c:c + 1, :] = (_row(dgc_col, eye) - jnp.sum(gm, axis=0, keepdims=True)
                                         + jnp.where(lane == C - 1, dglast, 0.0))
                dbeta_ref[hh, c:c + 1, :] = _row(dbc, eye)
            for c in cs:
                dq_ref[rows[c], :] = dq_acc[c]
                dk_ref[rows[c], :] = dk_acc[c]
        for hh in range(2):
            dg_ref[hh] = _dot_hi(dgc_s[hh], lower)

    rows_spec = pl.BlockSpec((2, cpt, C), lambda j, i: (j, i, 0))
    qk_spec = pl.BlockSpec((tt, GDN_DK), lambda j, i: (i, j))
    v_spec = pl.BlockSpec((tt, 2 * GDN_DK), lambda j, i: (i, j))
    cc_spec = pl.BlockSpec((2, cpt, C, C), lambda j, i: (j, i, 0, 0))
    rows_shape = jax.ShapeDtypeStruct((GDN_HV, nC, C), F32)
    return pl.pallas_call(
        body, grid=(GDN_HV // 2, T // tt),
        in_specs=[qk_spec, qk_spec, v_spec, rows_spec, rows_spec, cc_spec, v_spec, v_spec, cc_spec, v_spec,
                  pl.BlockSpec((2, cpt, GDN_DK, GDN_DK), lambda j, i: (j, i, 0, 0)), v_spec, v_spec, v_spec, rows_spec],
        out_specs=[qk_spec, qk_spec, v_spec, rows_spec, rows_spec],
        out_shape=[jax.ShapeDtypeStruct((T, GDN_HV // 2 * GDN_DK), F32),
                   jax.ShapeDtypeStruct((T, GDN_HV // 2 * GDN_DK), F32),
                   jax.ShapeDtypeStruct((T, D_INNER), F32), rows_shape, rows_shape],
        scratch_shapes=[pltpu.VMEM((2, cpt, C), F32)],
        compiler_params=_params("parallel", "parallel"), name=name,
    )(q, k, v, gc, beta, tinv, u, w, pm, vn, sall, do, dvn, dkd, dgl)


def _gdn_gate_bwd(araw, braw, dg, dbeta, alog, dtb, name):
    H, T = araw.shape

    def body(a_ref, b_ref, dg_ref, dbt_ref, alog_ref, dtb_ref, da_ref, db_ref, dalog_ref, ddtb_ref):
        xa = a_ref[...] + dtb_ref[...]
        ea = jnp.exp(alog_ref[...])
        dgv = dg_ref[...]
        da = -dgv * ea * _sigmoid(xa)
        da_ref[...] = da
        dalog_ref[...] = jnp.sum(-dgv * ea * _softplus(xa), axis=1, keepdims=True)
        ddtb_ref[...] = jnp.sum(da, axis=1, keepdims=True)
        bt = _sigmoid(b_ref[...])
        db_ref[...] = dbt_ref[...] * bt * (1.0 - bt)

    return pl.pallas_call(
        body,
        out_shape=[jax.ShapeDtypeStruct((H, T), F32), jax.ShapeDtypeStruct((H, T), F32),
                   jax.ShapeDtypeStruct((H, 1), F32), jax.ShapeDtypeStruct((H, 1), F32)],
        compiler_params=pltpu.CompilerParams(vmem_limit_bytes=VMEM_LIMIT_BYTES), name=name,
    )(araw, braw, dg, dbeta, alog, dtb)


SSD_LOCKSTEP_CHUNKS = 2
SSD_LOCKSTEP_CHUNKS_BWD = 1
SSD_LOCKSTEP_HEADS_BWD = 2


def _ssd_scan_fwd(xs, bm, cm, dtraw, alog, dtb, dskip, name):
    T = xs.shape[0]
    Q = SSD_CHUNK
    tt = min(T, 1024)
    cpt, nC = tt // Q, T // Q
    GW = SSD_R * SSD_P

    def body(alog_ref, dtb_ref, dsk_ref, xs_ref, b_ref, c_ref, dt_ref, y_ref, sall_ref, dto_ref, S, dt_s, acs_s):
        gi, i = pl.program_id(0), pl.program_id(1)

        @pl.when(i == 0)
        def _():
            S[...] = jnp.zeros_like(S)

        tri, _, eye, r_i, c_i = _masks(Q)
        upper = jnp.where(r_i <= c_i, 1.0, 0.0)
        for r in range(SSD_R):
            h = SSD_R * gi + r
            dt = _softplus(dt_ref[r] + dtb_ref[h])
            dto_ref[r] = dt
            dt_s[r] = dt
            acs_s[r] = _dot_hi(-jnp.exp(alog_ref[h]) * dt, upper)

        ps = [slice(r * SSD_P, (r + 1) * SSD_P) for r in range(SSD_R)]
        s_cur = [S[:, ps[r]] for r in range(SSD_R)]
        grp = min(cpt, SSD_LOCKSTEP_CHUNKS)
        for c0 in range(0, cpt, grp):
            cs = list(range(c0, c0 + grp))
            inst = [(c, r) for c in cs for r in range(SSD_R)]
            rows = {c: slice(c * Q, (c + 1) * Q) for c in cs}
            bc_ = {c: b_ref[rows[c], :] for c in cs}
            cc_ = {c: c_ref[rows[c], :] for c in cs}
            cb = {c: _dot_nt(cc_[c], bc_[c]) for c in cs}
            xr = [xs_ref[rows[c], ps[r]] for c, r in inst]
            acr = [acs_s[r, c:c + 1, :] for c, r in inst]
            acc = [_col_bcast(a, Q) for a in acr]
            dtr = [dt_s[r, c:c + 1, :] for c, r in inst]
            mm = [cb[c] * (jnp.exp(jnp.where(tri, acc[i] - acr[i], -1e30)) * dtr[i]) for i, (c, r) in enumerate(inst)]
            bct = {c: bc_[c].T for c in cs}
            st = [_dot(bct[c] * (jnp.exp(acr[i][:, Q - 1:Q] - acr[i]) * dtr[i]), xr[i]) for i, (c, r) in enumerate(inst)]
            yd = [_dot(mm[i], xr[i]) for i in range(len(inst))]
            s_prev = []
            for i, (c, r) in enumerate(inst):
                s_prev.append(s_cur[r])
                s_cur[r] = s_cur[r] * jnp.exp(acr[i][:, Q - 1:Q]) + st[i]
            yo = [_dot(cc_[c] * jnp.exp(acc[i]), s_prev[i]) for i, (c, r) in enumerate(inst)]
            for i, (c, r) in enumerate(inst):
                sall_ref[0, c, :, ps[r]] = s_prev[i]
                y_ref[rows[c], ps[r]] = yd[i] + yo[i] + dsk_ref[SSD_R * gi + r] * xr[i]
        for r in range(SSD_R):
            S[:, ps[r]] = s_cur[r]

    smem = pl.BlockSpec(memory_space=pltpu.SMEM)
    rows_spec = pl.BlockSpec((SSD_R, cpt, Q), lambda g, i: (g, i, 0))
    return pl.pallas_call(
        body, grid=(SSD_G, T // tt),
        in_specs=[smem, smem, smem,
                  pl.BlockSpec((tt, GW), lambda g, i: (i, g)), pl.BlockSpec((tt, SSD_N), lambda g, i: (i, g)),
                  pl.BlockSpec((tt, SSD_N), lambda g, i: (i, g)), rows_spec],
        out_specs=[pl.BlockSpec((tt, GW), lambda g, i: (i, g)),
                   pl.BlockSpec((1, cpt, SSD_N, GW), lambda g, i: (g, i, 0, 0)), rows_spec],
        out_shape=[jax.ShapeDtypeStruct((T, D_INNER), F32), jax.ShapeDtypeStruct((SSD_G, nC, SSD_N, GW), F32),
                   jax.ShapeDtypeStruct((SSD_H, nC, Q), F32)],
        scratch_shapes=[pltpu.VMEM((SSD_N, GW), F32), pltpu.VMEM((SSD_R, cpt, Q), F32),
                        pltpu.VMEM((SSD_R, cpt, Q), F32)],
        compiler_params=_params("parallel", "arbitrary"), name=name,
    )(alog, dtb, dskip, xs, bm, cm, dtraw)


def _ssd_scan_bwd(xs, bm, cm, dt, sall, dy, alog, dskip, name):
    T = xs.shape[0]
    Q = SSD_CHUNK
    tt = min(T, 1024)
    cpt, nC, nT = tt // Q, T // Q, T // tt
    GW = SSD_R * SSD_P

    def body(alog_ref, dsk_ref, xs_ref, b_ref, c_ref, dt_ref, sall_ref, dy_ref,
             dxs_ref, db_ref, dc_ref, da_ref, ddt_ref, dd_ref, dS, acs_s, dacs_s, ddt_s, dd_s):
        gi, i = pl.program_id(0), pl.program_id(1)

        @pl.when(i == 0)
        def _():
            dS[...] = jnp.zeros_like(dS)

        tri, _, eye, r_i, c_i = _masks(Q)
        upper = jnp.where(r_i <= c_i, 1.0, 0.0)
        lower = jnp.where(r_i >= c_i, 1.0, 0.0)
        lane = lax.broadcasted_iota(jnp.int32, (1, Q), 1)
        for r in range(SSD_R):
            acs_s[r] = _dot_hi(-jnp.exp(alog_ref[SSD_R * gi + r]) * dt_ref[r], upper)

        ps = [slice(r * SSD_P, (r + 1) * SSD_P) for r in range(SSD_R)]
        ds_cur = [dS[:, ps[r]] for r in range(SSD_R)]
        grp = min(cpt, SSD_LOCKSTEP_CHUNKS_BWD)
        csum = lambda a: jnp.sum(a, axis=0, keepdims=True)
        tsum = lambda a: jnp.sum(csum(a), axis=1, keepdims=True)
        ones8 = jnp.ones((8, SSD_P), F32)
        for c0 in range(cpt - grp, -1, -grp):
            cs = list(range(c0 + grp - 1, c0 - 1, -1))
            rows = {c: slice(c * Q, (c + 1) * Q) for c in cs}
            bc_ = {c: b_ref[rows[c], :] for c in cs}
            cc_ = {c: c_ref[rows[c], :] for c in cs}
            cb = {c: _dot_nt(cc_[c], bc_[c]) for c in cs}
            cbt = {c: _dot_nt(bc_[c], cc_[c]) for c in cs}
            bct = {c: bc_[c].T for c in cs}
            cct = {c: cc_[c].T for c in cs}
            dcb = {c: jnp.zeros((Q, Q), F32) for c in cs}
            dcbt = {c: jnp.zeros((Q, Q), F32) for c in cs}
            db_acc = {c: jnp.zeros((Q, SSD_N), F32) for c in cs}
            dc_acc = {c: jnp.zeros((Q, SSD_N), F32) for c in cs}
            for h0 in range(0, SSD_R, SSD_LOCKSTEP_HEADS_BWD):
                inst = [(c, r) for c in cs for r in range(h0, h0 + SSD_LOCKSTEP_HEADS_BWD)]
                n = len(inst)
                xr = [xs_ref[rows[c], ps[r]] for c, r in inst]
                dyr = [dy_ref[rows[c], ps[r]] for c, r in inst]
                acr = [acs_s[r, c:c + 1, :] for c, r in inst]
                dtr = [dt_ref[r, c:c + 1, :] for c, r in inst]
                acc = [_col_bcast(a, Q) for a in acr]
                dtb = [_col_bcast(d, Q) for d in dtr]
                al = [a[:, Q - 1:Q] for a in acr]
                e_c = [jnp.exp(a) for a in acc]
                dl_c = [jnp.exp(al[i] - acc[i]) for i in range(n)]
                e_r = [jnp.exp(a) for a in acr]
                dl_r = [jnp.exp(al[i] - acr[i]) for i in range(n)]
                gl = [jnp.exp(a) for a in al]
                lm = [jnp.exp(jnp.where(tri, acc[i] - acr[i], -1e30)) for i in range(n)]
                lmt = [jnp.exp(jnp.where(r_i <= c_i, acr[i] - acc[i], -1e30)) for i in range(n)]
                mmt = [cbt[c] * lmt[i] for i, (c, r) in enumerate(inst)]
                sr = [sall_ref[0, c, :, ps[r]] for c, r in inst]
                dmm0 = [_dot_nt(dyr[i], xr[i]) for i in range(n)]
                dmm0t = [_dot_nt(xr[i], dyr[i]) for i in range(n)]
                dxd1 = [_dot(mmt[i], dyr[i]) for i in range(n)]
                dce = [_dot_nt(dyr[i], sr[i]) for i in range(n)]
                dcet = [_dot_nt(sr[i], dyr[i]) for i in range(n)]
                cdy = [_dot(cct[c] * e_r[i], dyr[i]) for i, (c, r) in enumerate(inst)]
                dsn = []
                for i, (c, r) in enumerate(inst):
                    dsn.append(ds_cur[r])
                    ds_cur[r] = gl[i] * ds_cur[r] + cdy[i]
                dxd = [dxd1[i] + _dot(bc_[c] * dl_c[i], dsn[i]) for i, (c, r) in enumerate(inst)]
                dbd0 = [_dot_nt(xr[i], dsn[i]) for i in range(n)]
                dbd0t = [_dot_nt(dsn[i], xr[i]) for i in range(n)]
                for i, (c, r) in enumerate(inst):
                    dgl = tsum(dsn[i] * sr[i])
                    dc_acc[c] = dc_acc[c] + dce[i] * e_c[i]
                    db_acc[c] = db_acc[c] + dbd0[i] * (dtb[i] * dl_c[i])
                    dl0 = dmm0[i] * lm[i]
                    dl0t = dmm0t[i] * (lmt[i] * dtb[i])
                    dcb[c] = dcb[c] + dl0 * dtr[i]
                    dcbt[c] = dcbt[c] + dl0t
                    csum_gm0 = csum(dl0 * cb[c])
                    rsum_gm = csum(dl0t * cbt[c])
                    r_de = csum(dcet[i] * cct[c]) * e_r[i]
                    r_dl = csum(dbd0t[i] * bct[c]) * dl_r[i]
                    dalast = jnp.sum(r_dl * dtr[i], axis=1, keepdims=True) + dgl * gl[i]
                    dacs_s[r, c:c + 1, :] = (rsum_gm + r_de - (r_dl + csum_gm0) * dtr[i]
                                             + jnp.where(lane == Q - 1, dalast, 0.0))
                    ddt_s[r, c:c + 1, :] = csum_gm0 + r_dl
                    dd_s[r, c:c + 1, :] = _dot_nt(ones8, dyr[i] * xr[i])[0:1]
                    dxs_ref[rows[c], ps[r]] = dxd[i] * dtb[i][:, :SSD_P] + dsk_ref[SSD_R * gi + r] * dyr[i]
            for c in cs:
                dc_ref[rows[c], :] = dc_acc[c] + _dot(dcb[c], bc_[c])
                db_ref[rows[c], :] = db_acc[c] + _dot(dcbt[c], cc_[c])
        for r in range(SSD_R):
            dS[:, ps[r]] = ds_cur[r]
        for r in range(SSD_R):
            da_ref[r] = _dot_hi(dacs_s[r], lower)
            ddt_ref[r] = ddt_s[r]
            dd_ref[r] = dd_s[r]

    rev = lambda i: nT - 1 - i
    smem = pl.BlockSpec(memory_space=pltpu.SMEM)
    rows_spec = pl.BlockSpec((SSD_R, cpt, Q), lambda g, i: (g, rev(i), 0))
    x_spec = pl.BlockSpec((tt, GW), lambda g, i: (rev(i), g))
    n_spec = pl.BlockSpec((tt, SSD_N), lambda g, i: (rev(i), g))
    rows_shape = jax.ShapeDtypeStruct((SSD_H, nC, Q), F32)
    return pl.pallas_call(
        body, grid=(SSD_G, nT),
        in_specs=[smem, smem, x_spec, n_spec, n_spec, rows_spec,
                  pl.BlockSpec((1, cpt, SSD_N, GW), lambda g, i: (g, rev(i), 0, 0)), x_spec],
        out_specs=[x_spec, n_spec, n_spec, rows_spec, rows_spec, rows_spec],
        out_shape=[jax.ShapeDtypeStruct((T, D_INNER), F32), jax.ShapeDtypeStruct((T, SSD_G * SSD_N), F32),
                   jax.ShapeDtypeStruct((T, SSD_G * SSD_N), F32), rows_shape, rows_shape, rows_shape],
        scratch_shapes=[pltpu.VMEM((SSD_N, GW), F32)] + [pltpu.VMEM((SSD_R, cpt, Q), F32)] * 4,
        compiler_params=_params("parallel", "arbitrary"), name=name,
    )(alog, dskip, xs, bm, cm, dt, sall, dy)


def _ssd_gate_bwd(dtraw, dt, da, ddt_direct, ddrow, alog, dtb, name):
    H, T = dtraw.shape

    def body(raw_ref, dt_ref, da_ref, ddt_ref, dd_ref, alog_ref, dtb_ref, draw_ref, dalog_ref, ddtb_ref, dD_ref):
        a = -jnp.exp(alog_ref[...])
        dav = da_ref[...]
        ddt = ddt_ref[...] + dav * a
        draw = ddt * _sigmoid(raw_ref[...] + dtb_ref[...])
        draw_ref[...] = draw
        dalog_ref[...] = jnp.sum(dav * dt_ref[...], axis=1, keepdims=True) * a
        ddtb_ref[...] = jnp.sum(draw, axis=1, keepdims=True)
        dD_ref[...] = jnp.sum(dd_ref[...], axis=1, keepdims=True)

    return pl.pallas_call(
        body,
        out_shape=[jax.ShapeDtypeStruct((H, T), F32)] + [jax.ShapeDtypeStruct((H, 1), F32)] * 3,
        compiler_params=pltpu.CompilerParams(vmem_limit_bytes=VMEM_LIMIT_BYTES), name=name,
    )(dtraw, dt, da, ddt_direct, ddrow, alog, dtb)


def _final_loss(x, fw, tgt, name):
    T = x.shape[0]
    tt = min(T, 512)
    nT = T // tt

    def body(x_ref, w_ref, t_ref, dx_ref, dw_ref, loss_ref, acc):
        i = pl.program_id(0)

        @pl.when(i == 0)
        def _():
            dw_ref[...] = jnp.zeros_like(dw_ref)
            acc[...] = jnp.zeros_like(acc)

        xv = x_ref[...]
        r = lax.rsqrt(jnp.mean(xv * xv, axis=-1, keepdims=True) + EPS)
        xh = xv * r
        err = xh * w_ref[...] - t_ref[...]
        acc[...] += jnp.sum(err * err, axis=0, keepdims=True)
        dout = err * (1.0 / D_MODEL)
        dw_ref[...] += jnp.sum(dout * xh, axis=0, keepdims=True)
        dxn = dout * w_ref[...]
        dx_ref[...] = r * (dxn - xh * jnp.mean(dxn * xh, axis=-1, keepdims=True))

        @pl.when(i == nT - 1)
        def _():
            loss_ref[...] = (0.5 / D_MODEL) * jnp.sum(acc[...], axis=1, keepdims=True)

    row = pl.BlockSpec((tt, D_MODEL), lambda i: (i, 0))
    vec = pl.BlockSpec((1, D_MODEL), lambda i: (0, 0))
    return pl.pallas_call(
        body, grid=(nT,),
        in_specs=[row, vec, row],
        out_specs=[row, vec, pl.BlockSpec((1, 1), lambda i: (0, 0))],
        out_shape=[jax.ShapeDtypeStruct((T, D_MODEL), F32), jax.ShapeDtypeStruct((1, D_MODEL), F32),
                   jax.ShapeDtypeStruct((1, 1), F32)],
        scratch_shapes=[pltpu.VMEM((1, D_MODEL), F32)],
        compiler_params=_params("arbitrary"), name=name,
    )(x, fw, tgt)


def _adamw(parts, w, m, v, name):
    R, C = w.shape
    tr = 128 if R % 128 == 0 else R

    def body(p_ref, w_ref, m_ref, v_ref, g_ref, d_ref, nm_ref, nv_ref):
        g = p_ref[0].astype(F32)
        for s in range(1, N_DEV):
            g = g + p_ref[s].astype(F32)
        mn = ADAM_B1 * m_ref[...] + (1.0 - ADAM_B1) * g
        vn = ADAM_B2 * v_ref[...] + (1.0 - ADAM_B2) * (g * g)
        mh = mn / (1.0 - ADAM_B1 ** ADAM_STEP)
        vh = vn / (1.0 - ADAM_B2 ** ADAM_STEP)
        g_ref[...] = g
        d_ref[...] = -ADAM_LR * (mh / (jnp.sqrt(vh) + ADAM_EPS) + ADAM_WD * w_ref[...])
        nm_ref[...] = mn
        nv_ref[...] = vn

    blk = pl.BlockSpec((tr, C), lambda i: (i, 0))
    return pl.pallas_call(
        body, grid=(R // tr,),
        in_specs=[pl.BlockSpec((N_DEV, tr, C), lambda i: (0, i, 0)), blk, blk, blk],
        out_specs=[blk] * 4,
        out_shape=[jax.ShapeDtypeStruct((R, C), F32)] * 4,
        compiler_params=_params("parallel"), name=name,
    )(parts, w, m, v)


def _me():
    x, y, c = lax.axis_index("x"), lax.axis_index("y"), lax.axis_index("c")
    return x, y, c


def _peer(d):
    x, y, c = _me()
    px = 1 - x if d & 4 else x
    py = 1 - y if d & 2 else y
    pc = 1 - c if d & 1 else c
    return (px, py, pc), 4 * px + 2 * py + pc


def _exchange(arrs, bcast, name):
    n = len(arrs)

    def body(*refs):
        ex = _Exchange(refs[:n], refs[n:2 * n], bcast, *refs[2 * n:])
        ex.begin()
        ex.finish()

    anyspec = pl.BlockSpec(memory_space=pl.ANY)
    return pl.pallas_call(
        body,
        in_specs=[anyspec] * n, out_specs=[anyspec] * n,
        out_shape=_exchange_out_shapes(arrs, bcast),
        scratch_shapes=_exchange_semaphores(n),
        name=name,
    )(*arrs)


def _exchange_out_shapes(arrs, bcast):
    return [jax.ShapeDtypeStruct((N_DEV,) + (a.shape if b else a.shape[1:]), a.dtype) for a, b in zip(arrs, bcast)]


def _exchange_semaphores(n):
    return [pltpu.SemaphoreType.DMA((n, N_DEV - 1)), pltpu.SemaphoreType.DMA((n, N_DEV - 1)),
            pltpu.SemaphoreType.DMA((n,))]


class _Exchange:
    def __init__(self, ins, outs, bcast, ssem, rsem, lsem):
        n = len(ins)
        x, y, c = _me()
        me = 4 * x + 2 * y + c

        def src(a, dest):
            return ins[a] if bcast[a] else ins[a].at[dest]

        self.local = [pltpu.make_async_copy(src(a, me), outs[a].at[me], lsem.at[a]) for a in range(n)]
        self.sends, self.recvs = [], []
        for a in range(n):
            for d in range(1, N_DEV):
                peer, pid = _peer(d)
                self.sends.append(pltpu.make_async_remote_copy(
                    src_ref=src(a, pid), dst_ref=outs[a].at[me], send_sem=ssem.at[a, d - 1],
                    recv_sem=rsem.at[a, d - 1], device_id=peer, device_id_type=MESH))
                self.recvs.append(pltpu.make_async_remote_copy(
                    src_ref=src(a, pid), dst_ref=outs[a].at[pid], send_sem=ssem.at[a, d - 1],
                    recv_sem=rsem.at[a, d - 1], device_id=peer, device_id_type=MESH))

    def begin(self):
        for cp in self.local + self.sends:
            cp.start()

    def finish(self):
        for cp in self.recvs:
            cp.wait_recv()
        for cp in self.sends:
            cp.wait_send()
        for cp in self.local:
            cp.wait()


def _to_rows(cols, chunk):
    T, H = cols.shape
    return cols.T.reshape(H, T // chunk, chunk)


def _from_rows(rows):
    return rows.T


def _pad_cols(a, width):
    return jnp.pad(a, ((0, 0), (0, width - a.shape[1])))


def _local_step(x, tgt, p, late_weights=None, early_grads=None):
    T = x.shape[0]
    zb = lambda n: jnp.zeros((1, n), F32)
    gw = p["gdn_w_in"]
    g_wparts = [gw[:, 0:1024], gw[:, 1024:2048], gw[:, 2048:4096], gw[:, 4096:6144], _pad_cols(gw[:, 6144:6176], PAD_W)]
    nw0, nw1 = p["norm_w"][0:1], p["norm_w"][1:2]
    gcw = p["gdn_conv_w"]
    cw_q, cw_k, cw_v = gcw[:, 0:1024], gcw[:, 1024:2048], gcw[:, 2048:4096]
    g_convs = [(cw_q, zb(1024), True, GDN_DK ** -0.5), (cw_k, zb(1024), True, 1.0), (cw_v, zb(2048), False, 1.0),
               None, None]
    if late_weights is None:
        h0, (q_pre, k_pre, v_pre, z0, ab), (q, k, v) = _norm_inproj(x, nw0, g_wparts, g_convs, "gdn_inproj")
    else:
        comm, assemble = late_weights
        h0, (q_pre, k_pre, v_pre, z0, ab), (q, k, v), gathered = _norm_inproj(x, nw0, g_wparts, g_convs, "gdn_inproj",
                                                                              comm)
        p = dict(p, **assemble(gathered))
    braw = _to_rows(ab[:, 0:GDN_HV], GDN_CHUNK)
    araw = _to_rows(ab[:, GDN_HV:2 * GDN_HV], GDN_CHUNK)
    g_alog, g_dtb = p["gdn_a_log"].reshape(-1), p["gdn_dt_bias"].reshape(-1)
    g_u, g_w, g_pm, g_ti, g_rows, beta_rows, gc_rows = _gdn_prep(q, k, v, araw, braw, g_alog, g_dtb, "gdn_prep")
    o0, g_vn, g_sall = _gdn_state_fwd(q, k, g_u, g_w, g_pm, gc_rows, "gdn_state_fwd")
    x1 = _out_fwd(o0, z0, p["gdn_norm_w"], p["gdn_w_out"], x, GDN_DK, False, "gdn_out")
    sw = p["ssd_w_in"]
    s_wparts = [sw[:, 0:2048], sw[:, 2048:4096], sw[:, 4096:5120], sw[:, 5120:6144], _pad_cols(sw[:, 6144:6176], PAD_W)]
    scw, scb = p["ssd_conv_w"], p["ssd_conv_b"]
    s_convs = [None, (scw[:, 0:2048], scb[:, 0:2048], False, 1.0), (scw[:, 2048:3072], scb[:, 2048:3072], False, 1.0),
               (scw[:, 3072:4096], scb[:, 3072:4096], False, 1.0), None]
    h1, (z1, xs_pre, b_pre, c_pre, dtp), (xs, bm, cm) = _norm_inproj(x1, nw1, s_wparts, s_convs, "ssd_inproj")
    dtraw = _to_rows(dtp[:, 0:SSD_H], SSD_CHUNK)
    s_alog, s_dtb, s_d = p["ssd_a_log"].reshape(-1), p["ssd_dt_bias"].reshape(-1), p["ssd_d"].reshape(-1)
    y1, s_sall, dt_rows = _ssd_scan_fwd(xs, bm, cm, dtraw, s_alog, s_dtb, s_d, "ssd_scan_fwd")
    x2 = _out_fwd(y1, z1, p["ssd_norm_w"], p["ssd_w_out"], x1, D_INNER // SSD_G, True, "ssd_out")
    dx2, d_fw, loss = _final_loss(x2, p["final_norm_w"].reshape(1, -1), tgt, "final_loss")
    dy1, dz1, d_snw, yn1 = _out_bwd(dx2, y1, z1, p["ssd_norm_w"], p["ssd_w_out"], D_INNER // SSD_G, True, "ssd_out_bwd")
    d_swout = _matmul_tn(yn1, dx2, "ssd_wout_grad")
    dxs, dbm, dcm, da_rows, ddt_rows, dd_rows = _ssd_scan_bwd(xs, bm, cm, dt_rows, s_sall, dy1, s_alog, s_d, "ssd_scan_bwd")
    col = lambda a: a.reshape(-1, 1)
    dtraw_g, d_salog, d_sdtb, d_sd = _ssd_gate_bwd(
        dtraw.reshape(SSD_H, T), dt_rows.reshape(SSD_H, T), da_rows.reshape(SSD_H, T),
        ddt_rows.reshape(SSD_H, T), dd_rows.reshape(SSD_H, T), col(s_alog), col(s_dtb), "ssd_gate_bwd")
    dxs_pre, dcw_x, dcb_x = _conv_bwd(xs_pre, scw[:, 0:2048], scb[:, 0:2048], dxs, False, 1.0, "ssd_conv_x_bwd")
    db_pre, dcw_b, dcb_b = _conv_bwd(b_pre, scw[:, 2048:3072], scb[:, 2048:3072], dbm, False, 1.0, "ssd_conv_b_bwd")
    dc_pre, dcw_c, dcb_c = _conv_bwd(c_pre, scw[:, 3072:4096], scb[:, 3072:4096], dcm, False, 1.0, "ssd_conv_c_bwd")
    ddtp = _pad_cols(_from_rows(dtraw_g), PAD_W)
    s_dparts = [dz1, dxs_pre, db_pre, dc_pre, ddtp]
    dx1, d_nw1 = _inproj_bwd(x1, nw1, s_dparts, s_wparts, dx2, "ssd_inproj_bwd")
    s_dw = [_matmul_tn(h1, d, "ssd_win_grad_%d" % n) for n, d in enumerate(s_dparts)]
    d_swin = jnp.concatenate(s_dw[:4] + [s_dw[4][:, 0:SSD_H]], axis=1)
    early_recv = None
    if early_grads is None:
        do0, dz0, d_gnw, yn0 = _out_bwd(dx1, o0, z0, p["gdn_norm_w"], p["gdn_w_out"], GDN_DK, False, "gdn_out_bwd")
    else:
        do0, dz0, d_gnw, yn0, early_recv = _out_bwd(dx1, o0, z0, p["gdn_norm_w"], p["gdn_w_out"], GDN_DK, False,
                                                    "gdn_out_bwd", early_grads(d_swin, d_swout))
    d_gwout = _matmul_tn(yn0, dx1, "gdn_wout_grad")
    g_dvn, g_dkd, g_dgl = _gdn_state_bwd(q, k, g_w, g_pm, g_vn, g_sall, gc_rows, do0, "gdn_state_bwd")
    dq, dk, dv, dg_rows, dbeta_rows = _gdn_local_bwd(q, k, v, gc_rows, beta_rows, g_ti, g_u, g_w, g_pm, g_vn, g_sall,
                                                     do0, g_dvn, g_dkd, g_dgl, "gdn_local_bwd")
    da_g, db_g, d_galog, d_gdtb = _gdn_gate_bwd(
        araw.reshape(GDN_HV, T), braw.reshape(GDN_HV, T), dg_rows.reshape(GDN_HV, T),
        dbeta_rows.reshape(GDN_HV, T), col(g_alog), col(g_dtb), "gdn_gate_bwd")
    dq_pre, dcw_q, _ = _conv_bwd(q_pre, cw_q, zb(1024), dq, True, GDN_DK ** -0.5, "gdn_conv_q_bwd")
    dk_pre, dcw_k, _ = _conv_bwd(k_pre, cw_k, zb(1024), dk, True, 1.0, "gdn_conv_k_bwd")
    dv_pre, dcw_v, _ = _conv_bwd(v_pre, cw_v, zb(2048), dv, False, 1.0, "gdn_conv_v_bwd")
    dab = _pad_cols(jnp.concatenate([_from_rows(db_g), _from_rows(da_g)], axis=1), PAD_W)
    g_dparts = [dq_pre, dk_pre, dv_pre, dz0, dab]
    dx0, d_nw0 = _inproj_bwd(x, nw0, g_dparts, g_wparts, dx1, "gdn_inproj_bwd")
    g_dw = [_matmul_tn(h0, d, "gdn_win_grad_%d" % n) for n, d in enumerate(g_dparts)]
    d_gwin = jnp.concatenate(g_dw[:4] + [g_dw[4][:, 0:2 * GDN_HV]], axis=1)
    grads = {
        "norm_w": jnp.concatenate([d_nw0, d_nw1], axis=0),
        "gdn_w_in": d_gwin,
        "gdn_conv_w": jnp.concatenate([dcw_q, dcw_k, dcw_v], axis=1),
        "gdn_a_log": d_galog.reshape(1, -1),
        "gdn_dt_bias": d_gdtb.reshape(1, -1),
        "gdn_norm_w": d_gnw,
        "gdn_w_out": d_gwout,
        "ssd_w_in": d_swin,
        "ssd_conv_w": jnp.concatenate([dcw_x, dcw_b, dcw_c], axis=1),
        "ssd_conv_b": jnp.concatenate([dcb_x, dcb_b, dcb_c], axis=1),
        "ssd_dt_bias": d_sdtb.reshape(1, -1),
        "ssd_a_log": d_salog.reshape(1, -1),
        "ssd_d": d_sd.reshape(1, -1),
        "ssd_norm_w": d_snw,
        "ssd_w_out": d_swout,
        "final_norm_w": d_fw,
    }
    if early_grads is not None:
        return loss, dx0, grads, early_recv
    return loss, dx0, grads


WEIGHTS = ["norm_w", "gdn_w_in", "gdn_conv_w", "gdn_a_log", "gdn_dt_bias", "gdn_norm_w", "gdn_w_out", "ssd_w_in",
           "ssd_conv_w", "ssd_conv_b", "ssd_dt_bias", "ssd_a_log", "ssd_d", "ssd_norm_w", "ssd_w_out", "final_norm_w"]
COL_SHARDED = ["gdn_w_in", "ssd_w_in"]
ROW_SHARDED = ["gdn_w_out", "ssd_w_out"]
SMALL_SHARDED = ["gdn_conv_w", "ssd_conv_w", "ssd_conv_b", "ssd_norm_w"]
REPLICATED = ["norm_w", "gdn_a_log", "gdn_dt_bias", "gdn_norm_w", "ssd_dt_bias", "ssd_a_log", "ssd_d", "final_norm_w"]


def _pack(arrs):
    return jnp.concatenate([a.reshape(-1) for a in arrs]).reshape(1, -1)


def _unpack(flat, shapes):
    out, pos = [], 0
    for s in shapes:
        n = 1
        for dim in s:
            n *= dim
        out.append(flat[pos:pos + n].reshape(s))
        pos += n
    return out


def _cols_to_shards(full):
    R, C = full.shape
    return full.reshape(R, N_DEV, C // N_DEV).transpose(1, 0, 2)


def _shards_to_cols(shards):
    n, R, c = shards.shape
    return shards.transpose(1, 0, 2).reshape(R, n * c)


def kernel(x, norm_w, gdn_w_in, gdn_conv_w, gdn_a_log, gdn_dt_bias, gdn_norm_w, gdn_w_out, ssd_w_in, ssd_conv_w, ssd_conv_b, ssd_dt_bias, ssd_a_log, ssd_d, ssd_norm_w, ssd_w_out, final_norm_w, loss_target, m_norm_w, m_gdn_w_in, m_gdn_conv_w, m_gdn_a_log, m_gdn_dt_bias, m_gdn_norm_w, m_gdn_w_out, m_ssd_w_in, m_ssd_conv_w, m_ssd_conv_b, m_ssd_dt_bias, m_ssd_a_log, m_ssd_d, m_ssd_norm_w, m_ssd_w_out, m_final_norm_w, v_norm_w, v_gdn_w_in, v_gdn_conv_w, v_gdn_a_log, v_gdn_dt_bias, v_gdn_norm_w, v_gdn_w_out, v_ssd_w_in, v_ssd_conv_w, v_ssd_conv_b, v_ssd_dt_bias, v_ssd_a_log, v_ssd_d, v_ssd_norm_w, v_ssd_w_out, v_final_norm_w):
    w = dict(norm_w=norm_w, gdn_w_in=gdn_w_in[0], gdn_conv_w=gdn_conv_w[0], gdn_a_log=gdn_a_log,
             gdn_dt_bias=gdn_dt_bias, gdn_norm_w=gdn_norm_w, gdn_w_out=gdn_w_out[0], ssd_w_in=ssd_w_in[0],
             ssd_conv_w=ssd_conv_w[0], ssd_conv_b=ssd_conv_b, ssd_dt_bias=ssd_dt_bias, ssd_a_log=ssd_a_log,
             ssd_d=ssd_d, ssd_norm_w=ssd_norm_w, ssd_w_out=ssd_w_out[0], final_norm_w=final_norm_w.reshape(1, -1))
    m = dict(norm_w=m_norm_w, gdn_w_in=m_gdn_w_in[0], gdn_conv_w=m_gdn_conv_w[0], gdn_a_log=m_gdn_a_log,
             gdn_dt_bias=m_gdn_dt_bias, gdn_norm_w=m_gdn_norm_w, gdn_w_out=m_gdn_w_out[0], ssd_w_in=m_ssd_w_in[0],
             ssd_conv_w=m_ssd_conv_w[0], ssd_conv_b=m_ssd_conv_b, ssd_dt_bias=m_ssd_dt_bias, ssd_a_log=m_ssd_a_log,
             ssd_d=m_ssd_d, ssd_norm_w=m_ssd_norm_w, ssd_w_out=m_ssd_w_out[0], final_norm_w=m_final_norm_w.reshape(1, -1))
    v = dict(norm_w=v_norm_w, gdn_w_in=v_gdn_w_in[0], gdn_conv_w=v_gdn_conv_w[0], gdn_a_log=v_gdn_a_log,
             gdn_dt_bias=v_gdn_dt_bias, gdn_norm_w=v_gdn_norm_w, gdn_w_out=v_gdn_w_out[0], ssd_w_in=v_ssd_w_in[0],
             ssd_conv_w=v_ssd_conv_w[0], ssd_conv_b=v_ssd_conv_b, ssd_dt_bias=v_ssd_dt_bias, ssd_a_log=v_ssd_a_log,
             ssd_d=v_ssd_d, ssd_norm_w=v_ssd_norm_w, ssd_w_out=v_ssd_w_out[0], final_norm_w=v_final_norm_w.reshape(1, -1))
    out_shapes = {n: a.shape for n, a in zip(
        WEIGHTS, [norm_w, gdn_w_in, gdn_conv_w, gdn_a_log, gdn_dt_bias, gdn_norm_w, gdn_w_out, ssd_w_in, ssd_conv_w,
                  ssd_conv_b, ssd_dt_bias, ssd_a_log, ssd_d, ssd_norm_w, ssd_w_out, final_norm_w])}

    small_shapes = [w[n].shape for n in SMALL_SHARDED]
    first = _exchange([_mx(w["gdn_w_in"]), _pack([w[n] for n in SMALL_SHARDED])], [True] * 2, "gather_first")
    full = dict(w)
    full["gdn_w_in"] = _shards_to_cols(first[0])
    small_all = [_unpack(first[1][s, 0], small_shapes) for s in range(N_DEV)]
    for idx, n in enumerate(SMALL_SHARDED):
        full[n] = jnp.concatenate([small_all[s][idx] for s in range(N_DEV)], axis=-1)
    late = ["gdn_w_out", "ssd_w_in", "ssd_w_out"]

    def assemble(gathered):
        return {"gdn_w_out": gathered[0].reshape(-1, D_MODEL), "ssd_w_in": _shards_to_cols(gathered[1]),
                "ssd_w_out": gathered[2].reshape(-1, D_MODEL)}

    def early_grads(d_ssd_w_in, d_ssd_w_out):
        return ([_cols_to_shards(d_ssd_w_in).astype(GRAD_WIRE_DTYPE),
                 d_ssd_w_out.reshape(N_DEV, -1, D_MODEL).astype(GRAD_WIRE_DTYPE)], [False] * 2)

    loss, dx, grads, ssd_recv = _local_step(x[0], loss_target[0], full,
                                            (([_mx(w[n]) for n in late], [True] * 3), assemble), early_grads)

    send_small = jnp.concatenate(
        [_cols_to_shards(grads[n]).reshape(N_DEV, -1) for n in SMALL_SHARDED], axis=1)[:, None, :]
    rep_shapes = [w[n].shape for n in REPLICATED]
    recv = _exchange(
        [_cols_to_shards(grads["gdn_w_in"]).astype(GRAD_WIRE_DTYPE),
         grads["gdn_w_out"].reshape(N_DEV, -1, D_MODEL).astype(GRAD_WIRE_DTYPE),
         send_small, _pack([grads[n] for n in REPLICATED])],
        [False] * 3 + [True], "exchange_grads")

    res = {}
    for n, parts in zip(["gdn_w_in", "gdn_w_out", "ssd_w_in", "ssd_w_out"], [recv[0], recv[1]] + list(ssd_recv)):
        res[n] = _adamw(parts, w[n], m[n], v[n], "adamw_" + n)
    small_res = _adamw(recv[2], *[_pack([t[n] for n in SMALL_SHARDED]) for t in (w, m, v)], "adamw_small")
    rep_res = _adamw(recv[3], *[_pack([t[n] for n in REPLICATED]) for t in (w, m, v)], "adamw_replicated")
    for k4 in range(4):
        for n, a in zip(SMALL_SHARDED, _unpack(small_res[k4][0], small_shapes)):
            res.setdefault(n, [None] * 4)[k4] = a
        for n, a in zip(REPLICATED, _unpack(rep_res[k4][0], rep_shapes)):
            res.setdefault(n, [None] * 4)[k4] = a

    loss = lax.psum(loss[0, 0], ("x", "y", "c"))
    outs = [loss, dx[None]]
    for k4 in range(4):
        outs += [res[n][k4].reshape(out_shapes[n]) for n in WEIGHTS]
    return tuple(outs)
```

```python
import jax
import jax.numpy as jnp
from jax import lax
from jax.experimental import pallas as pl
from jax.experimental.pallas import tpu as pltpu

F32 = jnp.float32
MXU_DTYPE = jnp.bfloat16
GRAD_WIRE_DTYPE = jnp.bfloat16
HI = lax.Precision.HIGHEST
EPS = 1e-6
VMEM_LIMIT_BYTES = 56 * 1024 * 1024
N_DEV = 8
MESH = pl.DeviceIdType.MESH

D_MODEL = 1024
CONV_K = 4
GDN_HV = 16
GDN_DK = 128
GDN_CHUNK = 64
SSD_H = 32
SSD_P = 64
SSD_N = 128
SSD_G = 8
SSD_R = SSD_H // SSD_G
SSD_CHUNK = 128
D_INNER = 2048
PAD_W = 128

ADAM_LR = 0.001
ADAM_B1 = 0.9
ADAM_B2 = 0.999
ADAM_EPS = 1e-08
ADAM_WD = 0.01
ADAM_STEP = 10


def _params(*sem):
    return pltpu.CompilerParams(dimension_semantics=sem, vmem_limit_bytes=VMEM_LIMIT_BYTES)


def _mx(a):
    return a.astype(MXU_DTYPE)


def _dot(a, b):
    return jnp.dot(_mx(a), _mx(b), preferred_element_type=F32)


def _dot_nt(a, b):
    return lax.dot_general(_mx(a), _mx(b), (((1,), (1,)), ((), ())), preferred_element_type=F32)


def _dot_tn(a, b):
    return lax.dot_general(_mx(a), _mx(b), (((0,), (0,)), ((), ())), preferred_element_type=F32)


def _dot_hi(a, b):
    return jnp.dot(a, b, precision=HI, preferred_element_type=F32)


def _sigmoid(x):
    return 0.5 * jnp.tanh(0.5 * x) + 0.5


def _silu(x):
    return x * _sigmoid(x)


def _dsilu(x):
    s = _sigmoid(x)
    return s * (1.0 + x * (1.0 - s))


def _softplus(x):
    return jnp.maximum(x, 0.0) + jnp.log1p(jnp.exp(-jnp.abs(x)))


def _col(r, eye):
    return jnp.sum(jnp.where(eye, r, 0.0), axis=1, keepdims=True)


def _row(c, eye):
    return jnp.sum(jnp.where(eye, c, 0.0), axis=0, keepdims=True)


def _col_bcast(r, n):
    return jnp.broadcast_to(r, (n, n)).T


def _masks(n):
    r = lax.broadcasted_iota(jnp.int32, (n, n), 0)
    c = lax.broadcasted_iota(jnp.int32, (n, n), 1)
    return r >= c, r > c, r == c, r, c


def _with_exchange(comm):
    arrs, bcast = comm if comm else ([], [])
    nc = len(arrs)
    anyspec = pl.BlockSpec(memory_space=pl.ANY)

    def wrap(compute, n_in, n_out):
        def body(*refs):
            cin, cout = refs[n_in:n_in + nc], refs[n_in + nc + n_out:n_in + 2 * nc + n_out]
            sems = refs[n_in + 2 * nc + n_out:n_in + 2 * nc + n_out + 3]
            rest = refs[:n_in] + refs[n_in + nc:n_in + nc + n_out] + refs[n_in + 2 * nc + n_out + (3 if nc else 0):]
            if nc:
                @pl.when(pl.program_id(0) == 0)
                def _():
                    _Exchange(cin, cout, bcast, *sems).begin()
            compute(*rest)
            if nc:
                @pl.when(pl.program_id(0) == pl.num_programs(0) - 1)
                def _():
                    _Exchange(cin, cout, bcast, *sems).finish()
        return body

    return dict(arrs=list(arrs), nc=nc, wrap=wrap, in_specs=[anyspec] * nc, out_specs=[anyspec] * nc,
                out_shape=_exchange_out_shapes(arrs, bcast), scratch=_exchange_semaphores(nc) if nc else [])


INPROJ_COL_BLOCK = 512


def _norm_inproj(x, nw, wparts, convs, name, comm=None):
    T = x.shape[0]
    tt = min(T, 256)
    n = len(wparts)
    ck = [k for k in range(n) if convs[k] is not None]
    nconv = len(ck)
    ex = _with_exchange(comm)

    def compute(x_ref, nw_ref, *refs):
        w_refs, cw_refs = refs[:n], refs[n:n + 2 * nconv]
        h_ref, o_refs = refs[n + 2 * nconv], refs[n + 2 * nconv + 1:2 * n + 2 * nconv + 1]
        post_refs = refs[2 * n + 2 * nconv + 1:2 * n + 3 * nconv + 1]
        p_refs = refs[2 * n + 3 * nconv + 1:]
        xv = x_ref[...]
        r = lax.rsqrt(jnp.mean(xv * xv, axis=-1, keepdims=True) + EPS)
        h = _mx(xv * r * nw_ref[...])
        h_ref[...] = h
        for m in range(nconv):
            @pl.when(pl.program_id(0) == 0)
            def _():
                p_refs[m][0:HALO, :] = jnp.zeros((HALO, p_refs[m].shape[1]), F32)

        def conv_block(k, c0, cw):
            m = ck.index(k)
            _, _, l2, scale = convs[k]
            cw_ref, cb_ref, out_ref, P = cw_refs[2 * m], cw_refs[2 * m + 1], post_refs[m], p_refs[m]
            cs = slice(c0, c0 + cw)
            acc = cb_ref[:, cs] + cw_ref[0:1, cs] * P[pl.ds(HALO - 3, tt), cs]
            for j in range(1, CONV_K):
                acc = acc + cw_ref[j:j + 1, cs] * P[pl.ds(HALO - 3 + j, tt), cs]
            s = _silu(acc)
            if l2:
                sls = [slice(g0, g0 + GDN_DK) for g0 in range(0, cw, GDN_DK)]
                rr = [lax.rsqrt(jnp.sum(s[:, sl] * s[:, sl], axis=-1, keepdims=True) + EPS) for sl in sls]
                for sl, rg in zip(sls, rr):
                    out_ref[:, c0 + sl.start:c0 + sl.stop] = s[:, sl] * rg * scale
            else:
                out_ref[:, cs] = s
            P[0:HALO, cs] = P[tt:tt + HALO, cs]

        pending = None
        for k in range(n):
            for c0 in range(0, widths[k], INPROJ_COL_BLOCK):
                cw = min(INPROJ_COL_BLOCK, widths[k] - c0)
                pre = jnp.dot(h, w_refs[k][:, c0:c0 + cw], preferred_element_type=F32)
                o_refs[k][:, c0:c0 + cw] = pre
                if convs[k] is not None:
                    p_refs[ck.index(k)][HALO:HALO + tt, c0:c0 + cw] = pre
                if pending is not None:
                    conv_block(*pending)
                pending = (k, c0, cw) if convs[k] is not None else None
        if pending is not None:
            conv_block(*pending)

    row = lambda width: pl.BlockSpec((tt, width), lambda i: (i, 0))
    full = lambda a: pl.BlockSpec(a.shape, lambda i: (0, 0))
    once = lambda a: pl.BlockSpec(a.shape, lambda i: (0, 0), pipeline_mode=pl.Buffered(1))
    conv_args = [a for k in ck for a in convs[k][:2]]
    widths = [w.shape[1] for w in wparts]
    outs = pl.pallas_call(
        ex["wrap"](compute, 2 + n + 2 * nconv, 1 + n + nconv), grid=(T // tt,),
        in_specs=[row(D_MODEL), full(nw)] + [once(w) for w in wparts] + [full(a) for a in conv_args] + ex["in_specs"],
        out_specs=[row(D_MODEL)] + [row(wd) for wd in widths] + [row(widths[k]) for k in ck] + ex["out_specs"],
        out_shape=[jax.ShapeDtypeStruct((T, D_MODEL), MXU_DTYPE)]
        + [jax.ShapeDtypeStruct((T, wd), F32) for wd in widths]
        + [jax.ShapeDtypeStruct((T, widths[k]), F32) for k in ck] + ex["out_shape"],
        scratch_shapes=ex["scratch"] + [pltpu.VMEM((HALO + tt, widths[k]), F32) for k in ck],
        compiler_params=_params("arbitrary"), name=name,
    )(x, nw, *wparts, *conv_args, *ex["arrs"])
    outs = list(outs)
    res = (outs[0], outs[1:1 + n], outs[1 + n:1 + n + nconv])
    return res + (outs[1 + n + nconv:],) if comm else res


def _inproj_bwd(x, nw, dparts, wparts, dres, name):
    T = x.shape[0]
    tt = min(T, 256)
    n = len(wparts)

    def body(x_ref, nw_ref, dres_ref, *refs):
        d_refs, w_refs, dx_ref, dnw_ref = refs[:n], refs[n:2 * n], refs[2 * n], refs[2 * n + 1]

        @pl.when(pl.program_id(0) == 0)
        def _():
            dnw_ref[...] = jnp.zeros_like(dnw_ref)

        dh = _dot_nt(d_refs[0][...], w_refs[0][...])
        for d_ref, w_ref in zip(d_refs[1:], w_refs[1:]):
            dh = dh + _dot_nt(d_ref[...], w_ref[...])
        xv = x_ref[...]
        r = lax.rsqrt(jnp.mean(xv * xv, axis=-1, keepdims=True) + EPS)
        xh = xv * r
        dnw_ref[...] += jnp.sum(dh * xh, axis=0, keepdims=True)
        dxn = dh * nw_ref[...]
        dx_ref[...] = dres_ref[...] + r * (dxn - xh * jnp.mean(dxn * xh, axis=-1, keepdims=True))

    row = lambda width: pl.BlockSpec((tt, width), lambda i: (i, 0))
    full = lambda a: pl.BlockSpec(a.shape, lambda i: (0, 0))
    return pl.pallas_call(
        body, grid=(T // tt,),
        in_specs=[row(D_MODEL), full(nw), row(D_MODEL)] + [row(d.shape[1]) for d in dparts]
        + [full(w) for w in wparts],
        out_specs=[row(D_MODEL), pl.BlockSpec((1, D_MODEL), lambda i: (0, 0))],
        out_shape=[jax.ShapeDtypeStruct((T, D_MODEL), F32), jax.ShapeDtypeStruct((1, D_MODEL), F32)],
        compiler_params=_params("arbitrary"), name=name,
    )(x, nw, dres, *dparts, *wparts)


def _matmul_tn(a, b, name):
    T, K = a.shape
    N = b.shape[1]
    tt = min(T, 512)
    tn = min(N, 1024)

    def body(a_ref, b_ref, o_ref):
        @pl.when(pl.program_id(1) == 0)
        def _():
            o_ref[...] = jnp.zeros_like(o_ref)

        o_ref[...] += _dot_tn(a_ref[...], b_ref[...])

    return pl.pallas_call(
        body, grid=(N // tn, T // tt),
        in_specs=[pl.BlockSpec((tt, K), lambda n, t: (t, 0)), pl.BlockSpec((tt, tn), lambda n, t: (t, n))],
        out_specs=pl.BlockSpec((K, tn), lambda n, t: (0, n)),
        out_shape=jax.ShapeDtypeStruct((K, N), F32),
        compiler_params=_params("parallel", "arbitrary"), name=name,
    )(a, b)


OUT_COL_BLOCK = 512


def _out_fwd(o, z, w, wout, xres, gs, gate_first, name):
    T = o.shape[0]
    tt = min(T, 256)
    wide = w.shape[1] == D_INNER

    def body(o_ref, z_ref, w_ref, wout_ref, x_ref, out_ref, yn):
        acc = x_ref[...]
        pending = None
        for b0 in range(0, D_INNER, OUT_COL_BLOCK):
            for g0 in range(b0, b0 + OUT_COL_BLOCK, gs):
                sl = slice(g0, g0 + gs)
                og, zg = o_ref[:, sl], z_ref[:, sl]
                wg = w_ref[:, sl] if wide else w_ref[...]
                if gate_first:
                    u = og * _silu(zg)
                    r = lax.rsqrt(jnp.mean(u * u, axis=-1, keepdims=True) + EPS)
                    yn[:, sl] = _mx(u * r * wg)
                else:
                    r = lax.rsqrt(jnp.mean(og * og, axis=-1, keepdims=True) + EPS)
                    yn[:, sl] = _mx(og * r * wg * _silu(zg))
            if pending is not None:
                acc = acc + jnp.dot(yn[:, pending], wout_ref[pending, :], preferred_element_type=F32)
            pending = slice(b0, b0 + OUT_COL_BLOCK)
        out_ref[...] = acc + jnp.dot(yn[:, pending], wout_ref[pending, :], preferred_element_type=F32)

    row = lambda width: pl.BlockSpec((tt, width), lambda i: (i, 0))
    full = lambda a: pl.BlockSpec(a.shape, lambda i: (0, 0))
    return pl.pallas_call(
        body, grid=(T // tt,),
        in_specs=[row(D_INNER), row(D_INNER), full(w), full(wout), row(D_MODEL)],
        out_specs=row(D_MODEL),
        out_shape=jax.ShapeDtypeStruct((T, D_MODEL), F32),
        scratch_shapes=[pltpu.VMEM((tt, D_INNER), MXU_DTYPE)],
        compiler_params=_params("parallel"), name=name,
    )(o, z, w, wout, xres)


def _out_bwd(dx, o, z, w, wout, gs, gate_first, name, comm=None):
    T = o.shape[0]
    tt = min(T, 256)
    wide = w.shape[1] == D_INNER

    def body(dx_ref, o_ref, z_ref, w_ref, wout_ref, do_ref, dz_ref, dw_ref, yn_ref):
        @pl.when(pl.program_id(0) == 0)
        def _():
            dw_ref[...] = jnp.zeros_like(dw_ref)

        dxb = _mx(dx_ref[...])
        blocks = list(range(0, D_INNER, OUT_COL_BLOCK))
        dyn_b = {b0: _dot_nt(dxb, wout_ref[b0:b0 + OUT_COL_BLOCK, :]) for b0 in blocks[:1]}
        dw_acc = jnp.zeros((1, gs), F32)
        for g0 in range(0, D_INNER, gs):
            b0 = g0 - g0 % OUT_COL_BLOCK
            if g0 == b0 and b0 + OUT_COL_BLOCK < D_INNER:
                nb = b0 + OUT_COL_BLOCK
                dyn_b[nb] = _dot_nt(dxb, wout_ref[nb:nb + OUT_COL_BLOCK, :])
            sl = slice(g0, g0 + gs)
            og, zg, dg = o_ref[:, sl], z_ref[:, sl], dyn_b[b0][:, g0 - b0:g0 - b0 + gs]
            wg = w_ref[:, sl] if wide else w_ref[...]
            sz = _silu(zg)
            if gate_first:
                u = og * sz
                r = lax.rsqrt(jnp.mean(u * u, axis=-1, keepdims=True) + EPS)
                uh = u * r
                yn_ref[:, sl] = _mx(uh * wg)
                dw_g = jnp.sum(dg * uh, axis=0, keepdims=True)
                duh = dg * wg
                du = r * (duh - uh * jnp.mean(duh * uh, axis=-1, keepdims=True))
                do_ref[:, sl] = du * sz
                dz_ref[:, sl] = _mx(du * og * _dsilu(zg))
            else:
                r = lax.rsqrt(jnp.mean(og * og, axis=-1, keepdims=True) + EPS)
                oh = og * r
                yn_ref[:, sl] = _mx(oh * wg * sz)
                dw_g = jnp.sum(dg * oh * sz, axis=0, keepdims=True)
                doh = dg * wg * sz
                dz_ref[:, sl] = _mx(dg * oh * wg * _dsilu(zg))
                do_ref[:, sl] = r * (doh - oh * jnp.mean(doh * oh, axis=-1, keepdims=True))
            if wide:
                dw_ref[:, sl] += dw_g
            else:
                dw_acc = dw_acc + dw_g
        if not wide:
            dw_ref[...] += dw_acc

    row = lambda width: pl.BlockSpec((tt, width), lambda i: (i, 0))
    full = lambda a: pl.BlockSpec(a.shape, lambda i: (0, 0))
    ex = _with_exchange(comm)
    outs = pl.pallas_call(
        ex["wrap"](body, 5, 4), grid=(T // tt,),
        in_specs=[row(D_MODEL), row(D_INNER), row(D_INNER), full(w), full(wout)] + ex["in_specs"],
        out_specs=[row(D_INNER), row(D_INNER), full(w), row(D_INNER)] + ex["out_specs"],
        out_shape=[jax.ShapeDtypeStruct((T, D_INNER), F32), jax.ShapeDtypeStruct((T, D_INNER), MXU_DTYPE),
                   jax.ShapeDtypeStruct(w.shape, F32), jax.ShapeDtypeStruct((T, D_INNER), MXU_DTYPE)]
        + ex["out_shape"],
        scratch_shapes=ex["scratch"],
        compiler_params=_params("arbitrary"), name=name,
    )(dx, o, z, w, wout, *ex["arrs"])
    outs = list(outs)
    return outs[:4] + ([outs[4:]] if comm else [])


HALO = 8
CONV_STRIP = 16


def _conv_bwd(pre, w, b, dpost, l2, scale, name):
    T, C = pre.shape
    tt = min(T, 512)
    tc = min(C, 1024 if l2 else 512)
    strip = tt if l2 else CONV_STRIP
    nT = T // tt
    ext = tt + HALO

    def body(pre_ref, hp_ref, hn_ref, dpost_ref, dn_ref, w_ref, b_ref, dpre_ref, dw_ref, db_ref, P, Q):
        i = pl.program_id(1)

        @pl.when(i == 0)
        def _():
            dw_ref[...] = jnp.zeros_like(dw_ref)
            db_ref[...] = jnp.zeros_like(db_ref)

        P[0:HALO, :] = jnp.where(i > 0, hp_ref[...], 0.0)
        P[HALO:HALO + tt, :] = pre_ref[...]
        P[HALO + tt:HALO + ext, :] = hn_ref[...]
        wj = [w_ref[j:j + 1, :] for j in range(CONV_K)]
        bias = b_ref[...]
        keep_next = jnp.where(i < nT - 1, 1.0, 0.0)
        fold = lambda a: jnp.sum(a.reshape(strip // 8, 8, tc), axis=0)
        dw_acc = [jnp.zeros((8, tc), F32) for _ in range(CONV_K)]
        db_acc = jnp.zeros((8, tc), F32)
        for r0 in list(range(0, tt, strip)) + [tt]:
            n = strip if r0 < tt else HALO
            taps = [P[pl.ds(HALO - 3 + j + r0, n), :] for j in range(CONV_K)]
            cpre = bias + wj[0] * taps[0]
            for j in range(1, CONV_K):
                cpre = cpre + wj[j] * taps[j]
            dy = dpost_ref[r0:r0 + n, :] if r0 < tt else dn_ref[...] * keep_next
            sg = _sigmoid(cpre)
            ds_c = sg * (1.0 + cpre * (1.0 - sg))
            if l2:
                s = cpre * sg
                sls = [slice(g0, g0 + GDN_DK) for g0 in range(0, tc, GDN_DK)]
                rr = [lax.rsqrt(jnp.sum(s[:, sl] * s[:, sl], axis=-1, keepdims=True) + EPS) for sl in sls]
                yh = [s[:, sl] * r for sl, r in zip(sls, rr)]
                pr = [jnp.sum(dy[:, sl] * y, axis=-1, keepdims=True) for sl, y in zip(sls, yh)]
                for sl, r, y, p in zip(sls, rr, yh, pr):
                    Q[r0:r0 + n, sl] = (scale * r) * (dy[:, sl] - y * p) * ds_c[:, sl]
                dyc = Q[r0:r0 + n, :]
            else:
                dyc = dy * ds_c
                Q[r0:r0 + n, :] = dyc
            if r0 < tt:
                for j in range(CONV_K):
                    dw_acc[j] = dw_acc[j] + fold(dyc * taps[j])
                db_acc = db_acc + fold(dyc)
        for r0 in range(0, tt, strip):
            dpre = wj[0] * Q[pl.ds(3 + r0, strip), :]
            for j in range(1, CONV_K):
                dpre = dpre + wj[j] * Q[pl.ds(3 - j + r0, strip), :]
            dpre_ref[r0:r0 + strip, :] = _mx(dpre)
        for j in range(CONV_K):
            dw_ref[j:j + 1, :] += jnp.sum(dw_acc[j], axis=0, keepdims=True)
        db_ref[...] += jnp.sum(db_acc, axis=0, keepdims=True)

    tile = pl.BlockSpec((tt, tc), lambda j, i: (i, j))
    prev = pl.BlockSpec((HALO, tc), lambda j, i: (jnp.maximum(i * (tt // HALO) - 1, 0), j))
    nxt = pl.BlockSpec((HALO, tc), lambda j, i: (jnp.minimum((i + 1) * (tt // HALO), T // HALO - 1), j))
    return pl.pallas_call(
        body, grid=(C // tc, nT),
        in_specs=[tile, prev, nxt, tile, nxt,
                  pl.BlockSpec((CONV_K, tc), lambda j, i: (0, j)), pl.BlockSpec((1, tc), lambda j, i: (0, j))],
        out_specs=[tile, pl.BlockSpec((CONV_K, tc), lambda j, i: (0, j)), pl.BlockSpec((1, tc), lambda j, i: (0, j))],
        out_shape=[jax.ShapeDtypeStruct((T, C), MXU_DTYPE), jax.ShapeDtypeStruct((CONV_K, C), F32),
                   jax.ShapeDtypeStruct((1, C), F32)],
        scratch_shapes=[pltpu.VMEM((HALO + ext, tc), F32), pltpu.VMEM((ext, tc), F32)],
        compiler_params=_params("parallel", "arbitrary"), name=name,
    )(pre, pre, pre, dpost, dpost, w, b)


GDN_LOCKSTEP_CHUNKS = 16
GDN_SCAN_HEADS = 16


def _inv_unit_lower_many(nms, eye, n):
    xs = [jnp.where(eye, 1.0, 0.0) - nm for nm in nms]
    ps = list(nms)
    k = 2
    while k < n:
        ps = [_dot(p, p) for p in ps]
        xs = [x + _dot(x, p) for x, p in zip(xs, ps)]
        k *= 2
    return xs


def _gdn_prep(q, k, v, araw, braw, alog, dtb, name):
    T = q.shape[0]
    C = GDN_CHUNK
    tt = min(T, 1024)
    cpt, nC = tt // C, T // C
    grp = min(cpt, GDN_LOCKSTEP_CHUNKS)

    def body(alog_ref, dtb_ref, q_ref, k_ref, v_ref, a_ref, b_ref,
             u_ref, w_ref, pm_ref, ti_ref, g_ref, beta_ref, gc_ref, qd_ref, kd_ref):
        j = pl.program_id(0)
        tri, strict, eye, r_i, c_i = _masks(C)
        upper = jnp.where(r_i <= c_i, 1.0, 0.0)
        gcs, bts = [], []
        for hh in range(2):
            h = 2 * j + hh
            g = -jnp.exp(alog_ref[h]) * _softplus(a_ref[hh] + dtb_ref[h])
            bt = _sigmoid(b_ref[hh])
            gc = _dot_hi(g, upper)
            g_ref[hh], beta_ref[hh], gc_ref[hh] = g, bt, gc
            gcs.append(gc)
            bts.append(bt)
        for c0 in range(0, cpt, grp):
            cs = list(range(c0, c0 + grp))
            inst = [(c, hh) for c in cs for hh in range(2)]
            rows = {c: slice(c * C, (c + 1) * C) for c in cs}
            qc = {c: q_ref[rows[c], :] for c in cs}
            kc = {c: k_ref[rows[c], :] for c in cs}
            kk = {c: _dot_nt(kc[c], kc[c]) for c in cs}
            qk = {c: _dot_nt(qc[c], kc[c]) for c in cs}
            gcr = [gcs[hh][c:c + 1, :] for c, hh in inst]
            gcc = [_col(r, eye) for r in gcr]
            bc = [_col(bts[hh][c:c + 1, :], eye) for c, hh in inst]
            lm = [jnp.exp(jnp.where(tri, cc - r, -1e30)) for cc, r in zip(gcc, gcr)]
            nm = [jnp.where(strict, kk[c] * b * l, 0.0) for (c, hh), b, l in zip(inst, bc, lm)]
            tinv = _inv_unit_lower_many(nm, eye, C)
            e_c = [jnp.exp(cc) for cc in gcc]
            rhs = [jnp.concatenate([v_ref[rows[c], hh * GDN_DK:(hh + 1) * GDN_DK] * b, kc[c] * (b * e)], axis=1)
                   for (c, hh), b, e in zip(inst, bc, e_c)]
            sol = [_dot(t, r) for t, r in zip(tinv, rhs)]
            for (c, hh), s, t, l, e, cc, r in zip(inst, sol, tinv, lm, e_c, gcc, gcr):
                hs = slice(hh * GDN_DK, (hh + 1) * GDN_DK)
                u_ref[rows[c], hs] = s[:, :GDN_DK]
                w_ref[rows[c], hs] = _mx(s[:, GDN_DK:])
                pm_ref[hh, c] = _mx(jnp.where(tri, qk[c] * l, 0.0))
                ti_ref[hh, c] = _mx(t)
                qd_ref[rows[c], hs] = _mx(qc[c] * e)
                kd_ref[rows[c], hs] = _mx(kc[c] * jnp.exp(r[:, C - 1:C] - cc))

    smem = pl.BlockSpec(memory_space=pltpu.SMEM)
    rows_spec = pl.BlockSpec((2, cpt, C), lambda j, i: (j, i, 0))
    qk_spec = pl.BlockSpec((tt, GDN_DK), lambda j, i: (i, j))
    v_spec = pl.BlockSpec((tt, 2 * GDN_DK), lambda j, i: (i, j))
    cc_spec = pl.BlockSpec((2, cpt, C, C), lambda j, i: (j, i, 0, 0))
    rows_shape = jax.ShapeDtypeStruct((GDN_HV, nC, C), F32)
    cc_shape = jax.ShapeDtypeStruct((GDN_HV, nC, C, C), MXU_DTYPE)
    return pl.pallas_call(
        body, grid=(GDN_HV // 2, T // tt),
        in_specs=[smem, smem, qk_spec, qk_spec, v_spec, rows_spec, rows_spec],
        out_specs=[v_spec, v_spec, cc_spec, cc_spec, rows_spec, rows_spec, rows_spec, v_spec, v_spec],
        out_shape=[jax.ShapeDtypeStruct((T, D_INNER), F32), jax.ShapeDtypeStruct((T, D_INNER), MXU_DTYPE),
                   cc_shape, cc_shape, rows_shape, rows_shape, rows_shape,
                   jax.ShapeDtypeStruct((T, D_INNER), MXU_DTYPE), jax.ShapeDtypeStruct((T, D_INNER), MXU_DTYPE)],
        compiler_params=_params("parallel", "parallel"), name=name,
    )(alog, dtb, q, k, v, araw, braw)


def _gdn_state_fwd(q, k, u, w, pm, gc, name):
    T = q.shape[0]
    C = GDN_CHUNK
    HG = GDN_SCAN_HEADS
    tt = min(T, 512)
    cpt, nC = tt // C, T // C

    def body(q_ref, k_ref, u_ref, w_ref, pm_ref, gc_ref, o_ref, vn_ref, sall_ref, S):
        @pl.when(pl.program_id(1) == 0)
        def _():
            S[...] = jnp.zeros_like(S)

        heads = list(range(HG))

        def chunk(c, carry):
            rows = pl.ds(pl.multiple_of(c * C, C), C)
            hs = [slice(h * GDN_DK, (h + 1) * GDN_DK) for h in heads]
            gl = [jnp.exp(gc_ref[h, pl.ds(c, 1), C - 1:C]) for h in heads]
            sv = [S[h] for h in heads]
            for h in heads:
                sall_ref[h, c] = _mx(sv[h])
            ws = [_dot(w_ref[rows, hs[h]], sv[h]) for h in heads]
            qsv = [_dot(q_ref[rows, hs[h]], sv[h]) for h in heads]
            vn = [u_ref[rows, hs[h]] - ws[h] for h in heads]
            pv = [_dot(pm_ref[h, c], vn[h]) for h in heads]
            kv = [_dot_tn(k_ref[rows, hs[h]], vn[h]) for h in heads]
            for h in heads:
                vn_ref[rows, hs[h]] = _mx(vn[h])
                o_ref[rows, hs[h]] = qsv[h] + pv[h]
                S[h] = sv[h] * gl[h] + kv[h]
            return carry

        lax.fori_loop(0, cpt, chunk, 0)

    v_spec = pl.BlockSpec((tt, HG * GDN_DK), lambda g, i: (i, g))
    return pl.pallas_call(
        body, grid=(GDN_HV // HG, T // tt),
        in_specs=[v_spec, v_spec, v_spec, v_spec,
                  pl.BlockSpec((HG, cpt, C, C), lambda g, i: (g, i, 0, 0)),
                  pl.BlockSpec((HG, cpt, C), lambda g, i: (g, i, 0))],
        out_specs=[v_spec, v_spec, pl.BlockSpec((HG, cpt, GDN_DK, GDN_DK), lambda g, i: (g, i, 0, 0))],
        out_shape=[jax.ShapeDtypeStruct((T, D_INNER), F32), jax.ShapeDtypeStruct((T, D_INNER), MXU_DTYPE),
                   jax.ShapeDtypeStruct((GDN_HV, nC, GDN_DK, GDN_DK), MXU_DTYPE)],
        scratch_shapes=[pltpu.VMEM((HG, GDN_DK, GDN_DK), F32)],
        compiler_params=_params("parallel", "arbitrary"), name=name,
    )(q, k, u, w, pm, gc)


def _gdn_state_bwd(q, k, w, pm, vn, sall, gc, do, name):
    T = q.shape[0]
    C = GDN_CHUNK
    HG = GDN_SCAN_HEADS
    tt = min(T, 512)
    cpt, nC, nT = tt // C, T // C, T // tt

    def body(q_ref, k_ref, w_ref, pm_ref, vn_ref, sall_ref, gc_ref, do_ref, dvn_ref, dkd_ref, dgl_ref, dS):
        @pl.when(pl.program_id(1) == 0)
        def _():
            dS[...] = jnp.zeros_like(dS)

        heads = list(range(HG))

        def chunk(ci, carry):
            c = cpt - 1 - ci
            rows = pl.ds(pl.multiple_of(c * C, C), C)
            hs = [slice(h * GDN_DK, (h + 1) * GDN_DK) for h in heads]
            gl = [jnp.exp(gc_ref[h, pl.ds(c, 1), C - 1:C]) for h in heads]
            dsn = [dS[h] for h in heads]
            doc = [do_ref[rows, hs[h]] for h in heads]
            kds = [_dot(k_ref[rows, hs[h]], dsn[h]) for h in heads]
            pdo = [_dot_tn(pm_ref[h, c], doc[h]) for h in heads]
            dkd = [_dot_nt(vn_ref[rows, hs[h]], dsn[h]) for h in heads]
            qdo = [_dot_tn(q_ref[rows, hs[h]], doc[h]) for h in heads]
            dvn = [pdo[h] + kds[h] for h in heads]
            wdv = [_dot_tn(w_ref[rows, hs[h]], dvn[h]) for h in heads]
            for h in heads:
                dgl = jnp.sum(jnp.sum(dsn[h] * sall_ref[h, c].astype(F32), axis=0, keepdims=True), axis=1, keepdims=True)
                dgl_ref[h, pl.ds(c, 1), :] = jnp.broadcast_to(dgl, (1, C))
                dvn_ref[rows, hs[h]] = dvn[h]
                dkd_ref[rows, hs[h]] = dkd[h]
                dS[h] = dsn[h] * gl[h] + qdo[h] - wdv[h]
            return carry

        lax.fori_loop(0, cpt, chunk, 0)

    rev = lambda i: nT - 1 - i
    v_spec = pl.BlockSpec((tt, HG * GDN_DK), lambda g, i: (rev(i), g))
    rows_spec = pl.BlockSpec((HG, cpt, C), lambda g, i: (g, rev(i), 0))
    return pl.pallas_call(
        body, grid=(GDN_HV // HG, nT),
        in_specs=[v_spec, v_spec, v_spec, pl.BlockSpec((HG, cpt, C, C), lambda g, i: (g, rev(i), 0, 0)), v_spec,
                  pl.BlockSpec((HG, cpt, GDN_DK, GDN_DK), lambda g, i: (g, rev(i), 0, 0)), rows_spec, v_spec],
        out_specs=[v_spec, v_spec, rows_spec],
        out_shape=[jax.ShapeDtypeStruct((T, D_INNER), F32), jax.ShapeDtypeStruct((T, D_INNER), F32),
                   jax.ShapeDtypeStruct((GDN_HV, nC, C), F32)],
        scratch_shapes=[pltpu.VMEM((HG, GDN_DK, GDN_DK), F32)],
        compiler_params=_params("parallel", "arbitrary"), name=name,
    )(q, k, w, pm, vn, sall, gc, do)


def _gdn_local_bwd(q, k, v, gc, beta, tinv, u, w, pm, vn, sall, do, dvn, dkd, dgl, name):
    T = q.shape[0]
    C = GDN_CHUNK
    tt = min(T, 1024)
    cpt, nC = tt // C, T // C
    grp = min(cpt, GDN_LOCKSTEP_CHUNKS)

    def body(q_ref, k_ref, v_ref, gc_ref, b_ref, ti_ref, u_ref, w_ref, pm_ref, vn_ref, sall_ref, do_ref,
             dvn_ref, dkd_ref, dgl_ref, dq_ref, dk_ref, dv_ref, dg_ref, dbeta_ref, dgc_s):
        tri, strict, eye, r_i, c_i = _masks(C)
        lower = jnp.where(r_i >= c_i, 1.0, 0.0)
        lane = lax.broadcasted_iota(jnp.int32, (1, C), 1)
        rsum = lambda a: jnp.sum(a, axis=1, keepdims=True)
        for c0 in range(0, cpt, grp):
            cs = list(range(c0, c0 + grp))
            inst = [(c, hh) for c in cs for hh in range(2)]
            n = len(inst)
            rows = {c: slice(c * C, (c + 1) * C) for c in cs}
            hsl = [slice(hh * GDN_DK, (hh + 1) * GDN_DK) for c, hh in inst]
            qc = {c: q_ref[rows[c], :] for c in cs}
            kc = {c: k_ref[rows[c], :] for c in cs}
            kk = {c: _dot_nt(kc[c], kc[c]) for c in cs}
            gcr = [gc_ref[hh, c:c + 1, :] for c, hh in inst]
            gcc = [_col(r, eye) for r in gcr]
            bc = [_col(b_ref[hh, c:c + 1, :], eye) for c, hh in inst]
            lm = [jnp.exp(jnp.where(tri, cc - r, -1e30)) for cc, r in zip(gcc, gcr)]
            e_c = [jnp.exp(cc) for cc in gcc]
            el_c = [jnp.exp(r[:, C - 1:C] - cc) for cc, r in zip(gcc, gcr)]
            gl = [jnp.exp(r[:, C - 1:C]) for r in gcr]
            doc = [do_ref[rows[c], hsl[i]] for i, (c, hh) in enumerate(inst)]
            dvn = [dvn_ref[rows[c], hsl[i]] for i, (c, hh) in enumerate(inst)]
            sv = [sall_ref[hh, c] for c, hh in inst]
            aa = [_dot_nt(jnp.concatenate([_mx(doc[i]), _mx(dvn[i])], axis=0), sv[i]) for i in range(n)]
            dpm = [jnp.where(tri, _dot_nt(doc[i], vn_ref[rows[c], hsl[i]]), 0.0) for i, (c, hh) in enumerate(inst)]
            dqd = [a[:C] for a in aa]
            drhs = [_dot_tn(ti_ref[hh, c], jnp.concatenate([dvn[i], -aa[i][C:]], axis=1))
                    for i, (c, hh) in enumerate(inst)]
            sol = [jnp.concatenate([_mx(u_ref[rows[c], hsl[i]]), w_ref[rows[c], hsl[i]]], axis=1)
                   for i, (c, hh) in enumerate(inst)]
            dnm = [-jnp.where(strict, _dot_nt(drhs[i], sol[i]), 0.0) for i in range(n)]
            dkk = [dnm[i] * bc[i] * lm[i] for i in range(n)]
            dqk = [dpm[i] * lm[i] for i in range(n)]
            dq1 = [_dot(dqk[i], kc[c]) for i, (c, hh) in enumerate(inst)]
            dk1 = [_dot(dkk[i], kc[c]) for i, (c, hh) in enumerate(inst)]
            dk2 = [_dot_tn(dkk[i], kc[c]) for i, (c, hh) in enumerate(inst)]
            dk3 = [_dot_tn(dqk[i], qc[c]) for i, (c, hh) in enumerate(inst)]
            dq_acc = {c: jnp.zeros((C, GDN_DK), F32) for c in cs}
            dk_acc = {c: jnp.zeros((C, GDN_DK), F32) for c in cs}
            for i, (c, hh) in enumerate(inst):
                k_, q_, v_ = kc[c], qc[c], v_ref[rows[c], hsl[i]]
                dvb, dkbe = drhs[i][:, :GDN_DK], drhs[i][:, GDN_DK:]
                dkd = dkd_ref[rows[c], hsl[i]]
                kb = k_ * bc[i]
                dkb = dkbe * e_c[i]
                del_el = dkd * k_ * el_c[i]
                dbc = rsum(dnm[i] * kk[c] * lm[i]) + rsum(dkb * k_ + dvb * v_)
                dq_acc[c] = dq_acc[c] + dq1[i] + dqd[i] * e_c[i]
                dk_acc[c] = dk_acc[c] + dk1[i] + dk2[i] + dk3[i] + dkd * el_c[i] + dkb * bc[i]
                dv_ref[rows[c], hsl[i]] = dvb * bc[i]
                nm = jnp.where(strict, kk[c] * bc[i] * lm[i], 0.0)
                gm = dnm[i] * nm + dpm[i] * pm_ref[hh, c].astype(F32)
                dgc_col = rsum(gm) + rsum((dkbe * kb + dqd[i] * q_) * e_c[i] - del_el)
                dglast = (jnp.sum(jnp.sum(del_el, axis=0, keepdims=True), axis=1, keepdims=True)
                          + dgl_ref[hh, c:c + 1, 0:1] * gl[i])
                dgc_s[hh, c:c + 1, :] = (_row(dgc_col, eye) - jnp.sum(gm, axis=0, keepdims=True)
                                         + jnp.where(lane == C - 1, dglast, 0.0))
                dbeta_ref[hh, c:c + 1, :] = _row(dbc, eye)
            for c in cs:
                dq_ref[rows[c], :] = dq_acc[c]
                dk_ref[rows[c], :] = dk_acc[c]
        for hh in range(2):
            dg_ref[hh] = _dot_hi(dgc_s[hh], lower)

    rows_spec = pl.BlockSpec((2, cpt, C), lambda j, i: (j, i, 0))
    qk_spec = pl.BlockSpec((tt, GDN_DK), lambda j, i: (i, j))
    v_spec = pl.BlockSpec((tt, 2 * GDN_DK), lambda j, i: (i, j))
    cc_spec = pl.BlockSpec((2, cpt, C, C), lambda j, i: (j, i, 0, 0))
    rows_shape = jax.ShapeDtypeStruct((GDN_HV, nC, C), F32)
    return pl.pallas_call(
        body, grid=(GDN_HV // 2, T // tt),
        in_specs=[qk_spec, qk_spec, v_spec, rows_spec, rows_spec, cc_spec, v_spec, v_spec, cc_spec, v_spec,
                  pl.BlockSpec((2, cpt, GDN_DK, GDN_DK), lambda j, i: (j, i, 0, 0)), v_spec, v_spec, v_spec, rows_spec],
        out_specs=[qk_spec, qk_spec, v_spec, rows_spec, rows_spec],
        out_shape=[jax.ShapeDtypeStruct((T, GDN_HV // 2 * GDN_DK), F32),
                   jax.ShapeDtypeStruct((T, GDN_HV // 2 * GDN_DK), F32),
                   jax.ShapeDtypeStruct((T, D_INNER), F32), rows_shape, rows_shape],
        scratch_shapes=[pltpu.VMEM((2, cpt, C), F32)],
        compiler_params=_params("parallel", "parallel"), name=name,
    )(q, k, v, gc, beta, tinv, u, w, pm, vn, sall, do, dvn, dkd, dgl)


def _gdn_gate_bwd(araw, braw, dg, dbeta, alog, dtb, name):
    H, T = araw.shape

    def body(a_ref, b_ref, dg_ref, dbt_ref, alog_ref, dtb_ref, da_ref, db_ref, dalog_ref, ddtb_ref):
        xa = a_ref[...] + dtb_ref[...]
        ea = jnp.exp(alog_ref[...])
        dgv = dg_ref[...]
        da = -dgv * ea * _sigmoid(xa)
        da_ref[...] = da
        dalog_ref[...] = jnp.sum(-dgv * ea * _softplus(xa), axis=1, keepdims=True)
        ddtb_ref[...] = jnp.sum(da, axis=1, keepdims=True)
        bt = _sigmoid(b_ref[...])
        db_ref[...] = dbt_ref[...] * bt * (1.0 - bt)

    return pl.pallas_call(
        body,
        out_shape=[jax.ShapeDtypeStruct((H, T), F32), jax.ShapeDtypeStruct((H, T), F32),
                   jax.ShapeDtypeStruct((H, 1), F32), jax.ShapeDtypeStruct((H, 1), F32)],
        compiler_params=pltpu.CompilerParams(vmem_limit_bytes=VMEM_LIMIT_BYTES), name=name,
    )(araw, braw, dg, dbeta, alog, dtb)


SSD_LOCKSTEP_CHUNKS = 2
SSD_LOCKSTEP_CHUNKS_BWD = 1
SSD_LOCKSTEP_HEADS_BWD = 2


def _ssd_scan_fwd(xs, bm, cm, dtraw, alog, dtb, dskip, name):
    T = xs.shape[0]
    Q = SSD_CHUNK
    tt = min(T, 1024)
    cpt, nC = tt // Q, T // Q
    GW = SSD_R * SSD_P

    def body(alog_ref, dtb_ref, dsk_ref, xs_ref, b_ref, c_ref, dt_ref, y_ref, sall_ref, dto_ref, S, dt_s, acs_s):
        gi, i = pl.program_id(0), pl.program_id(1)

        @pl.when(i == 0)
        def _():
            S[...] = jnp.zeros_like(S)

        tri, _, eye, r_i, c_i = _masks(Q)
        upper = jnp.where(r_i <= c_i, 1.0, 0.0)
        for r in range(SSD_R):
            h = SSD_R * gi + r
            dt = _softplus(dt_ref[r] + dtb_ref[h])
            dto_ref[r] = dt
            dt_s[r] = dt
            acs_s[r] = _dot_hi(-jnp.exp(alog_ref[h]) * dt, upper)

        ps = [slice(r * SSD_P, (r + 1) * SSD_P) for r in range(SSD_R)]
        s_cur = [S[:, ps[r]] for r in range(SSD_R)]
        grp = min(cpt, SSD_LOCKSTEP_CHUNKS)
        for c0 in range(0, cpt, grp):
            cs = list(range(c0, c0 + grp))
            inst = [(c, r) for c in cs for r in range(SSD_R)]
            rows = {c: slice(c * Q, (c + 1) * Q) for c in cs}
            bc_ = {c: b_ref[rows[c], :] for c in cs}
            cc_ = {c: c_ref[rows[c], :] for c in cs}
            cb = {c: _dot_nt(cc_[c], bc_[c]) for c in cs}
            xr = [xs_ref[rows[c], ps[r]] for c, r in inst]
            acr = [acs_s[r, c:c + 1, :] for c, r in inst]
            acc = [_col_bcast(a, Q) for a in acr]
            dtr = [dt_s[r, c:c + 1, :] for c, r in inst]
            mm = [cb[c] * (jnp.exp(jnp.where(tri, acc[i] - acr[i], -1e30)) * dtr[i]) for i, (c, r) in enumerate(inst)]
            bct = {c: bc_[c].T for c in cs}
            st = [_dot(bct[c] * (jnp.exp(acr[i][:, Q - 1:Q] - acr[i]) * dtr[i]), xr[i]) for i, (c, r) in enumerate(inst)]
            yd = [_dot(mm[i], xr[i]) for i in range(len(inst))]
            s_prev = []
            for i, (c, r) in enumerate(inst):
                s_prev.append(s_cur[r])
                s_cur[r] = s_cur[r] * jnp.exp(acr[i][:, Q - 1:Q]) + st[i]
            yo = [_dot(cc_[c] * jnp.exp(acc[i]), s_prev[i]) for i, (c, r) in enumerate(inst)]
            for i, (c, r) in enumerate(inst):
                sall_ref[0, c, :, ps[r]] = s_prev[i]
                y_ref[rows[c], ps[r]] = yd[i] + yo[i] + dsk_ref[SSD_R * gi + r] * xr[i]
        for r in range(SSD_R):
            S[:, ps[r]] = s_cur[r]

    smem = pl.BlockSpec(memory_space=pltpu.SMEM)
    rows_spec = pl.BlockSpec((SSD_R, cpt, Q), lambda g, i: (g, i, 0))
    return pl.pallas_call(
        body, grid=(SSD_G, T // tt),
        in_specs=[smem, smem, smem,
                  pl.BlockSpec((tt, GW), lambda g, i: (i, g)), pl.BlockSpec((tt, SSD_N), lambda g, i: (i, g)),
                  pl.BlockSpec((tt, SSD_N), lambda g, i: (i, g)), rows_spec],
        out_specs=[pl.BlockSpec((tt, GW), lambda g, i: (i, g)),
                   pl.BlockSpec((1, cpt, SSD_N, GW), lambda g, i: (g, i, 0, 0)), rows_spec],
        out_shape=[jax.ShapeDtypeStruct((T, D_INNER), F32), jax.ShapeDtypeStruct((SSD_G, nC, SSD_N, GW), F32),
                   jax.ShapeDtypeStruct((SSD_H, nC, Q), F32)],
        scratch_shapes=[pltpu.VMEM((SSD_N, GW), F32), pltpu.VMEM((SSD_R, cpt, Q), F32),
                        pltpu.VMEM((SSD_R, cpt, Q), F32)],
        compiler_params=_params("parallel", "arbitrary"), name=name,
    )(alog, dtb, dskip, xs, bm, cm, dtraw)


def _ssd_scan_bwd(xs, bm, cm, dt, sall, dy, alog, dskip, name):
    T = xs.shape[0]
    Q = SSD_CHUNK
    tt = min(T, 1024)
    cpt, nC, nT = tt // Q, T // Q, T // tt
    GW = SSD_R * SSD_P

    def body(alog_ref, dsk_ref, xs_ref, b_ref, c_ref, dt_ref, sall_ref, dy_ref,
             dxs_ref, db_ref, dc_ref, da_ref, ddt_ref, dd_ref, dS, acs_s, dacs_s, ddt_s, dd_s):
        gi, i = pl.program_id(0), pl.program_id(1)

        @pl.when(i == 0)
        def _():
            dS[...] = jnp.zeros_like(dS)

        tri, _, eye, r_i, c_i = _masks(Q)
        upper = jnp.where(r_i <= c_i, 1.0, 0.0)
        lower = jnp.where(r_i >= c_i, 1.0, 0.0)
        lane = lax.broadcasted_iota(jnp.int32, (1, Q), 1)
        for r in range(SSD_R):
            acs_s[r] = _dot_hi(-jnp.exp(alog_ref[SSD_R * gi + r]) * dt_ref[r], upper)

        ps = [slice(r * SSD_P, (r + 1) * SSD_P) for r in range(SSD_R)]
        ds_cur = [dS[:, ps[r]] for r in range(SSD_R)]
        grp = min(cpt, SSD_LOCKSTEP_CHUNKS_BWD)
        csum = lambda a: jnp.sum(a, axis=0, keepdims=True)
        tsum = lambda a: jnp.sum(csum(a), axis=1, keepdims=True)
        ones8 = jnp.ones((8, SSD_P), F32)
        for c0 in range(cpt - grp, -1, -grp):
            cs = list(range(c0 + grp - 1, c0 - 1, -1))
            rows = {c: slice(c * Q, (c + 1) * Q) for c in cs}
            bc_ = {c: b_ref[rows[c], :] for c in cs}
            cc_ = {c: c_ref[rows[c], :] for c in cs}
            cb = {c: _dot_nt(cc_[c], bc_[c]) for c in cs}
            cbt = {c: _dot_nt(bc_[c], cc_[c]) for c in cs}
            bct = {c: bc_[c].T for c in cs}
            cct = {c: cc_[c].T for c in cs}
            dcb = {c: jnp.zeros((Q, Q), F32) for c in cs}
            dcbt = {c: jnp.zeros((Q, Q), F32) for c in cs}
            db_acc = {c: jnp.zeros((Q, SSD_N), F32) for c in cs}
            dc_acc = {c: jnp.zeros((Q, SSD_N), F32) for c in cs}
            for h0 in range(0, SSD_R, SSD_LOCKSTEP_HEADS_BWD):
                inst = [(c, r) for c in cs for r in range(h0, h0 + SSD_LOCKSTEP_HEADS_BWD)]
                n = len(inst)
                xr = [xs_ref[rows[c], ps[r]] for c, r in inst]
                dyr = [dy_ref[rows[c], ps[r]] for c, r in inst]
                acr = [acs_s[r, c:c + 1, :] for c, r in inst]
                dtr = [dt_ref[r, c:c + 1, :] for c, r in inst]
                acc = [_col_bcast(a, Q) for a in acr]
                dtb = [_col_bcast(d, Q) for d in dtr]
                al = [a[:, Q - 1:Q] for a in acr]
                e_c = [jnp.exp(a) for a in acc]
                dl_c = [jnp.exp(al[i] - acc[i]) for i in range(n)]
                e_r = [jnp.exp(a) for a in acr]
                dl_r = [jnp.exp(al[i] - acr[i]) for i in range(n)]
                gl = [jnp.exp(a) for a in al]
                lm = [jnp.exp(jnp.where(tri, acc[i] - acr[i], -1e30)) for i in range(n)]
                lmt = [jnp.exp(jnp.where(r_i <= c_i, acr[i] - acc[i], -1e30)) for i in range(n)]
                mmt = [cbt[c] * lmt[i] for i, (c, r) in enumerate(inst)]
                sr = [sall_ref[0, c, :, ps[r]] for c, r in inst]
                dmm0 = [_dot_nt(dyr[i], xr[i]) for i in range(n)]
                dmm0t = [_dot_nt(xr[i], dyr[i]) for i in range(n)]
                dxd1 = [_dot(mmt[i], dyr[i]) for i in range(n)]
                dce = [_dot_nt(dyr[i], sr[i]) for i in range(n)]
                dcet = [_dot_nt(sr[i], dyr[i]) for i in range(n)]
                cdy = [_dot(cct[c] * e_r[i], dyr[i]) for i, (c, r) in enumerate(inst)]
                dsn = []
                for i, (c, r) in enumerate(inst):
                    dsn.append(ds_cur[r])
                    ds_cur[r] = gl[i] * ds_cur[r] + cdy[i]
                dxd = [dxd1[i] + _dot(bc_[c] * dl_c[i], dsn[i]) for i, (c, r) in enumerate(inst)]
                dbd0 = [_dot_nt(xr[i], dsn[i]) for i in range(n)]
                dbd0t = [_dot_nt(dsn[i], xr[i]) for i in range(n)]
                for i, (c, r) in enumerate(inst):
                    dgl = tsum(dsn[i] * sr[i])
                    dc_acc[c] = dc_acc[c] + dce[i] * e_c[i]
                    db_acc[c] = db_acc[c] + dbd0[i] * (dtb[i] * dl_c[i])
                    dl0 = dmm0[i] * lm[i]
                    dl0t = dmm0t[i] * (lmt[i] * dtb[i])
                    dcb[c] = dcb[c] + dl0 * dtr[i]
                    dcbt[c] = dcbt[c] + dl0t
                    csum_gm0 = csum(dl0 * cb[c])
                    rsum_gm = csum(dl0t * cbt[c])
                    r_de = csum(dcet[i] * cct[c]) * e_r[i]
                    r_dl = csum(dbd0t[i] * bct[c]) * dl_r[i]
                    dalast = jnp.sum(r_dl * dtr[i], axis=1, keepdims=True) + dgl * gl[i]
                    dacs_s[r, c:c + 1, :] = (rsum_gm + r_de - (r_dl + csum_gm0) * dtr[i]
                                             + jnp.where(lane == Q - 1, dalast, 0.0))
                    ddt_s[r, c:c + 1, :] = csum_gm0 + r_dl
                    dd_s[r, c:c + 1, :] = _dot_nt(ones8, dyr[i] * xr[i])[0:1]
                    dxs_ref[rows[c], ps[r]] = dxd[i] * dtb[i][:, :SSD_P] + dsk_ref[SSD_R * gi + r] * dyr[i]
            for c in cs:
                dc_ref[rows[c], :] = dc_acc[c] + _dot(dcb[c], bc_[c])
                db_ref[rows[c], :] = db_acc[c] + _dot(dcbt[c], cc_[c])
        for r in range(SSD_R):
            dS[:, ps[r]] = ds_cur[r]
        for r in range(SSD_R):
            da_ref[r] = _dot_hi(dacs_s[r], lower)
            ddt_ref[r] = ddt_s[r]
            dd_ref[r] = dd_s[r]

    rev = lambda i: nT - 1 - i
    smem = pl.BlockSpec(memory_space=pltpu.SMEM)
    rows_spec = pl.BlockSpec((SSD_R, cpt, Q), lambda g, i: (g, rev(i), 0))
    x_spec = pl.BlockSpec((tt, GW), lambda g, i: (rev(i), g))
    n_spec = pl.BlockSpec((tt, SSD_N), lambda g, i: (rev(i), g))
    rows_shape = jax.ShapeDtypeStruct((SSD_H, nC, Q), F32)
    return pl.pallas_call(
        body, grid=(SSD_G, nT),
        in_specs=[smem, smem, x_spec, n_spec, n_spec, rows_spec,
                  pl.BlockSpec((1, cpt, SSD_N, GW), lambda g, i: (g, rev(i), 0, 0)), x_spec],
        out_specs=[x_spec, n_spec, n_spec, rows_spec, rows_spec, rows_spec],
        out_shape=[jax.ShapeDtypeStruct((T, D_INNER), F32), jax.ShapeDtypeStruct((T, SSD_G * SSD_N), F32),
                   jax.ShapeDtypeStruct((T, SSD_G * SSD_N), F32), rows_shape, rows_shape, rows_shape],
        scratch_shapes=[pltpu.VMEM((SSD_N, GW), F32)] + [pltpu.VMEM((SSD_R, cpt, Q), F32)] * 4,
        compiler_params=_params("parallel", "arbitrary"), name=name,
    )(alog, dskip, xs, bm, cm, dt, sall, dy)


def _ssd_gate_bwd(dtraw, dt, da, ddt_direct, ddrow, alog, dtb, name):
    H, T = dtraw.shape

    def body(raw_ref, dt_ref, da_ref, ddt_ref, dd_ref, alog_ref, dtb_ref, draw_ref, dalog_ref, ddtb_ref, dD_ref):
        a = -jnp.exp(alog_ref[...])
        dav = da_ref[...]
        ddt = ddt_ref[...] + dav * a
        draw = ddt * _sigmoid(raw_ref[...] + dtb_ref[...])
        draw_ref[...] = draw
        dalog_ref[...] = jnp.sum(dav * dt_ref[...], axis=1, keepdims=True) * a
        ddtb_ref[...] = jnp.sum(draw, axis=1, keepdims=True)
        dD_ref[...] = jnp.sum(dd_ref[...], axis=1, keepdims=True)

    return pl.pallas_call(
        body,
        out_shape=[jax.ShapeDtypeStruct((H, T), F32)] + [jax.ShapeDtypeStruct((H, 1), F32)] * 3,
        compiler_params=pltpu.CompilerParams(vmem_limit_bytes=VMEM_LIMIT_BYTES), name=name,
    )(dtraw, dt, da, ddt_direct, ddrow, alog, dtb)


def _final_loss(x, fw, tgt, name):
    T = x.shape[0]
    tt = min(T, 512)
    nT = T // tt

    def body(x_ref, w_ref, t_ref, dx_ref, dw_ref, loss_ref, acc):
        i = pl.program_id(0)

        @pl.when(i == 0)
        def _():
            dw_ref[...] = jnp.zeros_like(dw_ref)
            acc[...] = jnp.zeros_like(acc)

        xv = x_ref[...]
        r = lax.rsqrt(jnp.mean(xv * xv, axis=-1, keepdims=True) + EPS)
        xh = xv * r
        err = xh * w_ref[...] - t_ref[...]
        acc[...] += jnp.sum(err * err, axis=0, keepdims=True)
        dout = err * (1.0 / D_MODEL)
        dw_ref[...] += jnp.sum(dout * xh, axis=0, keepdims=True)
        dxn = dout * w_ref[...]
        dx_ref[...] = r * (dxn - xh * jnp.mean(dxn * xh, axis=-1, keepdims=True))

        @pl.when(i == nT - 1)
        def _():
            loss_ref[...] = (0.5 / D_MODEL) * jnp.sum(acc[...], axis=1, keepdims=True)

    row = pl.BlockSpec((tt, D_MODEL), lambda i: (i, 0))
    vec = pl.BlockSpec((1, D_MODEL), lambda i: (0, 0))
    return pl.pallas_call(
        body, grid=(nT,),
        in_specs=[row, vec, row],
        out_specs=[row, vec, pl.BlockSpec((1, 1), lambda i: (0, 0))],
        out_shape=[jax.ShapeDtypeStruct((T, D_MODEL), F32), jax.ShapeDtypeStruct((1, D_MODEL), F32),
                   jax.ShapeDtypeStruct((1, 1), F32)],
        scratch_shapes=[pltpu.VMEM((1, D_MODEL), F32)],
        compiler_params=_params("arbitrary"), name=name,
    )(x, fw, tgt)


def _adamw(parts, w, m, v, name):
    R, C = w.shape
    tr = 128 if R % 128 == 0 else R

    def body(p_ref, w_ref, m_ref, v_ref, g_ref, d_ref, nm_ref, nv_ref):
        g = p_ref[0].astype(F32)
        for s in range(1, N_DEV):
            g = g + p_ref[s].astype(F32)
        mn = ADAM_B1 * m_ref[...] + (1.0 - ADAM_B1) * g
        vn = ADAM_B2 * v_ref[...] + (1.0 - ADAM_B2) * (g * g)
        mh = mn / (1.0 - ADAM_B1 ** ADAM_STEP)
        vh = vn / (1.0 - ADAM_B2 ** ADAM_STEP)
        g_ref[...] = g
        d_ref[...] = -ADAM_LR * (mh / (jnp.sqrt(vh) + ADAM_EPS) + ADAM_WD * w_ref[...])
        nm_ref[...] = mn
        nv_ref[...] = vn

    blk = pl.BlockSpec((tr, C), lambda i: (i, 0))
    return pl.pallas_call(
        body, grid=(R // tr,),
        in_specs=[pl.BlockSpec((N_DEV, tr, C), lambda i: (0, i, 0)), blk, blk, blk],
        out_specs=[blk] * 4,
        out_shape=[jax.ShapeDtypeStruct((R, C), F32)] * 4,
        compiler_params=_params("parallel"), name=name,
    )(parts, w, m, v)


def _me():
    x, y, c = lax.axis_index("x"), lax.axis_index("y"), lax.axis_index("c")
    return x, y, c


def _peer(d):
    x, y, c = _me()
    px = 1 - x if d & 4 else x
    py = 1 - y if d & 2 else y
    pc = 1 - c if d & 1 else c
    return (px, py, pc), 4 * px + 2 * py + pc


def _exchange(arrs, bcast, name):
    n = len(arrs)

    def body(*refs):
        ex = _Exchange(refs[:n], refs[n:2 * n], bcast, *refs[2 * n:])
        ex.begin()
        ex.finish()

    anyspec = pl.BlockSpec(memory_space=pl.ANY)
    return pl.pallas_call(
        body,
        in_specs=[anyspec] * n, out_specs=[anyspec] * n,
        out_shape=_exchange_out_shapes(arrs, bcast),
        scratch_shapes=_exchange_semaphores(n),
        name=name,
    )(*arrs)


def _exchange_out_shapes(arrs, bcast):
    return [jax.ShapeDtypeStruct((N_DEV,) + (a.shape if b else a.shape[1:]), a.dtype) for a, b in zip(arrs, bcast)]


def _exchange_semaphores(n):
    return [pltpu.SemaphoreType.DMA((n, N_DEV - 1)), pltpu.SemaphoreType.DMA((n, N_DEV - 1)),
            pltpu.SemaphoreType.DMA((n,))]


class _Exchange:
    def __init__(self, ins, outs, bcast, ssem, rsem, lsem):
        n = len(ins)
        x, y, c = _me()
        me = 4 * x + 2 * y + c

        def src(a, dest):
            return ins[a] if bcast[a] else ins[a].at[dest]

        self.local = [pltpu.make_async_copy(src(a, me), outs[a].at[me], lsem.at[a]) for a in range(n)]
        self.sends, self.recvs = [], []
        for a in range(n):
            for d in range(1, N_DEV):
                peer, pid = _peer(d)
                self.sends.append(pltpu.make_async_remote_copy(
                    src_ref=src(a, pid), dst_ref=outs[a].at[me], send_sem=ssem.at[a, d - 1],
                    recv_sem=rsem.at[a, d - 1], device_id=peer, device_id_type=MESH))
                self.recvs.append(pltpu.make_async_remote_copy(
                    src_ref=src(a, pid), dst_ref=outs[a].at[pid], send_sem=ssem.at[a, d - 1],
                    recv_sem=rsem.at[a, d - 1], device_id=peer, device_id_type=MESH))

    def begin(self):
        for cp in self.local + self.sends:
            cp.start()

    def finish(self):
        for cp in self.recvs:
            cp.wait_recv()
        for cp in self.sends:
            cp.wait_send()
        for cp in self.local:
            cp.wait()


def _to_rows(cols, chunk):
    T, H = cols.shape
    return cols.T.reshape(H, T // chunk, chunk)


def _from_rows(rows):
    return rows.T


def _pad_cols(a, width):
    return jnp.pad(a, ((0, 0), (0, width - a.shape[1])))


def _local_step(x, tgt, p, late_weights=None, early_grads=None):
    T = x.shape[0]
    zb = lambda n: jnp.zeros((1, n), F32)
    gw = p["gdn_w_in"]
    g_wparts = [gw[:, 0:1024], gw[:, 1024:2048], gw[:, 2048:4096], gw[:, 4096:6144], _pad_cols(gw[:, 6144:6176], PAD_W)]
    nw0, nw1 = p["norm_w"][0:1], p["norm_w"][1:2]
    gcw = p["gdn_conv_w"]
    cw_q, cw_k, cw_v = gcw[:, 0:1024], gcw[:, 1024:2048], gcw[:, 2048:4096]
    g_convs = [(cw_q, zb(1024), True, GDN_DK ** -0.5), (cw_k, zb(1024), True, 1.0), (cw_v, zb(2048), False, 1.0),
               None, None]
    if late_weights is None:
        h0, (q_pre, k_pre, v_pre, z0, ab), (q, k, v) = _norm_inproj(x, nw0, g_wparts, g_convs, "gdn_inproj")
    else:
        comm, assemble = late_weights
        h0, (q_pre, k_pre, v_pre, z0, ab), (q, k, v), gathered = _norm_inproj(x, nw0, g_wparts, g_convs, "gdn_inproj",
                                                                              comm)
        p = dict(p, **assemble(gathered))
    braw = _to_rows(ab[:, 0:GDN_HV], GDN_CHUNK)
    araw = _to_rows(ab[:, GDN_HV:2 * GDN_HV], GDN_CHUNK)
    g_alog, g_dtb = p["gdn_a_log"].reshape(-1), p["gdn_dt_bias"].reshape(-1)
    g_u, g_w, g_pm, g_ti, g_rows, beta_rows, gc_rows, g_qd, g_kd = _gdn_prep(q, k, v, araw, braw, g_alog, g_dtb,
                                                                             "gdn_prep")
    o0, g_vn, g_sall = _gdn_state_fwd(g_qd, g_kd, g_u, g_w, g_pm, gc_rows, "gdn_state_fwd")
    x1 = _out_fwd(o0, z0, p["gdn_norm_w"], p["gdn_w_out"], x, GDN_DK, False, "gdn_out")
    sw = p["ssd_w_in"]
    s_wparts = [sw[:, 0:2048], sw[:, 2048:4096], sw[:, 4096:5120], sw[:, 5120:6144], _pad_cols(sw[:, 6144:6176], PAD_W)]
    scw, scb = p["ssd_conv_w"], p["ssd_conv_b"]
    s_convs = [None, (scw[:, 0:2048], scb[:, 0:2048], False, 1.0), (scw[:, 2048:3072], scb[:, 2048:3072], False, 1.0),
               (scw[:, 3072:4096], scb[:, 3072:4096], False, 1.0), None]
    h1, (z1, xs_pre, b_pre, c_pre, dtp), (xs, bm, cm) = _norm_inproj(x1, nw1, s_wparts, s_convs, "ssd_inproj")
    dtraw = _to_rows(dtp[:, 0:SSD_H], SSD_CHUNK)
    s_alog, s_dtb, s_d = p["ssd_a_log"].reshape(-1), p["ssd_dt_bias"].reshape(-1), p["ssd_d"].reshape(-1)
    y1, s_sall, dt_rows = _ssd_scan_fwd(xs, bm, cm, dtraw, s_alog, s_dtb, s_d, "ssd_scan_fwd")
    x2 = _out_fwd(y1, z1, p["ssd_norm_w"], p["ssd_w_out"], x1, D_INNER // SSD_G, True, "ssd_out")
    dx2, d_fw, loss = _final_loss(x2, p["final_norm_w"].reshape(1, -1), tgt, "final_loss")
    dy1, dz1, d_snw, yn1 = _out_bwd(dx2, y1, z1, p["ssd_norm_w"], p["ssd_w_out"], D_INNER // SSD_G, True, "ssd_out_bwd")
    d_swout = _matmul_tn(yn1, dx2, "ssd_wout_grad")
    dxs, dbm, dcm, da_rows, ddt_rows, dd_rows = _ssd_scan_bwd(xs, bm, cm, dt_rows, s_sall, dy1, s_alog, s_d, "ssd_scan_bwd")
    col = lambda a: a.reshape(-1, 1)
    dtraw_g, d_salog, d_sdtb, d_sd = _ssd_gate_bwd(
        dtraw.reshape(SSD_H, T), dt_rows.reshape(SSD_H, T), da_rows.reshape(SSD_H, T),
        ddt_rows.reshape(SSD_H, T), dd_rows.reshape(SSD_H, T), col(s_alog), col(s_dtb), "ssd_gate_bwd")
    dxs_pre, dcw_x, dcb_x = _conv_bwd(xs_pre, scw[:, 0:2048], scb[:, 0:2048], dxs, False, 1.0, "ssd_conv_x_bwd")
    db_pre, dcw_b, dcb_b = _conv_bwd(b_pre, scw[:, 2048:3072], scb[:, 2048:3072], dbm, False, 1.0, "ssd_conv_b_bwd")
    dc_pre, dcw_c, dcb_c = _conv_bwd(c_pre, scw[:, 3072:4096], scb[:, 3072:4096], dcm, False, 1.0, "ssd_conv_c_bwd")
    ddtp = _pad_cols(_from_rows(dtraw_g), PAD_W)
    s_dparts = [dz1, dxs_pre, db_pre, dc_pre, ddtp]
    dx1, d_nw1 = _inproj_bwd(x1, nw1, s_dparts, s_wparts, dx2, "ssd_inproj_bwd")
    s_dw = [_matmul_tn(h1, d, "ssd_win_grad_%d" % n) for n, d in enumerate(s_dparts)]
    d_swin = jnp.concatenate(s_dw[:4] + [s_dw[4][:, 0:SSD_H]], axis=1)
    early_recv = None
    if early_grads is None:
        do0, dz0, d_gnw, yn0 = _out_bwd(dx1, o0, z0, p["gdn_norm_w"], p["gdn_w_out"], GDN_DK, False, "gdn_out_bwd")
    else:
        do0, dz0, d_gnw, yn0, early_recv = _out_bwd(dx1, o0, z0, p["gdn_norm_w"], p["gdn_w_out"], GDN_DK, False,
                                                    "gdn_out_bwd", early_grads(d_swin, d_swout))
    d_gwout = _matmul_tn(yn0, dx1, "gdn_wout_grad")
    g_dvn, g_dkd, g_dgl = _gdn_state_bwd(g_qd, g_kd, g_w, g_pm, g_vn, g_sall, gc_rows, do0, "gdn_state_bwd")
    dq, dk, dv, dg_rows, dbeta_rows = _gdn_local_bwd(q, k, v, gc_rows, beta_rows, g_ti, g_u, g_w, g_pm, g_vn, g_sall,
                                                     do0, g_dvn, g_dkd, g_dgl, "gdn_local_bwd")
    da_g, db_g, d_galog, d_gdtb = _gdn_gate_bwd(
        araw.reshape(GDN_HV, T), braw.reshape(GDN_HV, T), dg_rows.reshape(GDN_HV, T),
        dbeta_rows.reshape(GDN_HV, T), col(g_alog), col(g_dtb), "gdn_gate_bwd")
    dq_pre, dcw_q, _ = _conv_bwd(q_pre, cw_q, zb(1024), dq, True, GDN_DK ** -0.5, "gdn_conv_q_bwd")
    dk_pre, dcw_k, _ = _conv_bwd(k_pre, cw_k, zb(1024), dk, True, 1.0, "gdn_conv_k_bwd")
    dv_pre, dcw_v, _ = _conv_bwd(v_pre, cw_v, zb(2048), dv, False, 1.0, "gdn_conv_v_bwd")
    dab = _pad_cols(jnp.concatenate([_from_rows(db_g), _from_rows(da_g)], axis=1), PAD_W)
    g_dparts = [dq_pre, dk_pre, dv_pre, dz0, dab]
    dx0, d_nw0 = _inproj_bwd(x, nw0, g_dparts, g_wparts, dx1, "gdn_inproj_bwd")
    g_dw = [_matmul_tn(h0, d, "gdn_win_grad_%d" % n) for n, d in enumerate(g_dparts)]
    d_gwin = jnp.concatenate(g_dw[:4] + [g_dw[4][:, 0:2 * GDN_HV]], axis=1)
    grads = {
        "norm_w": jnp.concatenate([d_nw0, d_nw1], axis=0),
        "gdn_w_in": d_gwin,
        "gdn_conv_w": jnp.concatenate([dcw_q, dcw_k, dcw_v], axis=1),
        "gdn_a_log": d_galog.reshape(1, -1),
        "gdn_dt_bias": d_gdtb.reshape(1, -1),
        "gdn_norm_w": d_gnw,
        "gdn_w_out": d_gwout,
        "ssd_w_in": d_swin,
        "ssd_conv_w": jnp.concatenate([dcw_x, dcw_b, dcw_c], axis=1),
        "ssd_conv_b": jnp.concatenate([dcb_x, dcb_b, dcb_c], axis=1),
        "ssd_dt_bias": d_sdtb.reshape(1, -1),
        "ssd_a_log": d_salog.reshape(1, -1),
        "ssd_d": d_sd.reshape(1, -1),
        "ssd_norm_w": d_snw,
        "ssd_w_out": d_swout,
        "final_norm_w": d_fw,
    }
    if early_grads is not None:
        return loss, dx0, grads, early_recv
    return loss, dx0, grads


WEIGHTS = ["norm_w", "gdn_w_in", "gdn_conv_w", "gdn_a_log", "gdn_dt_bias", "gdn_norm_w", "gdn_w_out", "ssd_w_in",
           "ssd_conv_w", "ssd_conv_b", "ssd_dt_bias", "ssd_a_log", "ssd_d", "ssd_norm_w", "ssd_w_out", "final_norm_w"]
COL_SHARDED = ["gdn_w_in", "ssd_w_in"]
ROW_SHARDED = ["gdn_w_out", "ssd_w_out"]
SMALL_SHARDED = ["gdn_conv_w", "ssd_conv_w", "ssd_conv_b", "ssd_norm_w"]
REPLICATED = ["norm_w", "gdn_a_log", "gdn_dt_bias", "gdn_norm_w", "ssd_dt_bias", "ssd_a_log", "ssd_d", "final_norm_w"]


def _pack(arrs):
    return jnp.concatenate([a.reshape(-1) for a in arrs]).reshape(1, -1)


def _unpack(flat, shapes):
    out, pos = [], 0
    for s in shapes:
        n = 1
        for dim in s:
            n *= dim
        out.append(flat[pos:pos + n].reshape(s))
        pos += n
    return out


def _cols_to_shards(full):
    R, C = full.shape
    return full.reshape(R, N_DEV, C // N_DEV).transpose(1, 0, 2)


def _shards_to_cols(shards):
    n, R, c = shards.shape
    return shards.transpose(1, 0, 2).reshape(R, n * c)


def kernel(x, norm_w, gdn_w_in, gdn_conv_w, gdn_a_log, gdn_dt_bias, gdn_norm_w, gdn_w_out, ssd_w_in, ssd_conv_w, ssd_conv_b, ssd_dt_bias, ssd_a_log, ssd_d, ssd_norm_w, ssd_w_out, final_norm_w, loss_target, m_norm_w, m_gdn_w_in, m_gdn_conv_w, m_gdn_a_log, m_gdn_dt_bias, m_gdn_norm_w, m_gdn_w_out, m_ssd_w_in, m_ssd_conv_w, m_ssd_conv_b, m_ssd_dt_bias, m_ssd_a_log, m_ssd_d, m_ssd_norm_w, m_ssd_w_out, m_final_norm_w, v_norm_w, v_gdn_w_in, v_gdn_conv_w, v_gdn_a_log, v_gdn_dt_bias, v_gdn_norm_w, v_gdn_w_out, v_ssd_w_in, v_ssd_conv_w, v_ssd_conv_b, v_ssd_dt_bias, v_ssd_a_log, v_ssd_d, v_ssd_norm_w, v_ssd_w_out, v_final_norm_w):
    w = dict(norm_w=norm_w, gdn_w_in=gdn_w_in[0], gdn_conv_w=gdn_conv_w[0], gdn_a_log=gdn_a_log,
             gdn_dt_bias=gdn_dt_bias, gdn_norm_w=gdn_norm_w, gdn_w_out=gdn_w_out[0], ssd_w_in=ssd_w_in[0],
             ssd_conv_w=ssd_conv_w[0], ssd_conv_b=ssd_conv_b, ssd_dt_bias=ssd_dt_bias, ssd_a_log=ssd_a_log,
             ssd_d=ssd_d, ssd_norm_w=ssd_norm_w, ssd_w_out=ssd_w_out[0], final_norm_w=final_norm_w.reshape(1, -1))
    m = dict(norm_w=m_norm_w, gdn_w_in=m_gdn_w_in[0], gdn_conv_w=m_gdn_conv_w[0], gdn_a_log=m_gdn_a_log,
             gdn_dt_bias=m_gdn_dt_bias, gdn_norm_w=m_gdn_norm_w, gdn_w_out=m_gdn_w_out[0], ssd_w_in=m_ssd_w_in[0],
             ssd_conv_w=m_ssd_conv_w[0], ssd_conv_b=m_ssd_conv_b, ssd_dt_bias=m_ssd_dt_bias, ssd_a_log=m_ssd_a_log,
             ssd_d=m_ssd_d, ssd_norm_w=m_ssd_norm_w, ssd_w_out=m_ssd_w_out[0], final_norm_w=m_final_norm_w.reshape(1, -1))
    v = dict(norm_w=v_norm_w, gdn_w_in=v_gdn_w_in[0], gdn_conv_w=v_gdn_conv_w[0], gdn_a_log=v_gdn_a_log,
             gdn_dt_bias=v_gdn_dt_bias, gdn_norm_w=v_gdn_norm_w, gdn_w_out=v_gdn_w_out[0], ssd_w_in=v_ssd_w_in[0],
             ssd_conv_w=v_ssd_conv_w[0], ssd_conv_b=v_ssd_conv_b, ssd_dt_bias=v_ssd_dt_bias, ssd_a_log=v_ssd_a_log,
             ssd_d=v_ssd_d, ssd_norm_w=v_ssd_norm_w, ssd_w_out=v_ssd_w_out[0], final_norm_w=v_final_norm_w.reshape(1, -1))
    out_shapes = {n: a.shape for n, a in zip(
        WEIGHTS, [norm_w, gdn_w_in, gdn_conv_w, gdn_a_log, gdn_dt_bias, gdn_norm_w, gdn_w_out, ssd_w_in, ssd_conv_w,
                  ssd_conv_b, ssd_dt_bias, ssd_a_log, ssd_d, ssd_norm_w, ssd_w_out, final_norm_w])}

    small_shapes = [w[n].shape for n in SMALL_SHARDED]
    first = _exchange([_mx(w["gdn_w_in"]), _pack([w[n] for n in SMALL_SHARDED])], [True] * 2, "gather_first")
    full = dict(w)
    full["gdn_w_in"] = _shards_to_cols(first[0])
    small_all = [_unpack(first[1][s, 0], small_shapes) for s in range(N_DEV)]
    for idx, n in enumerate(SMALL_SHARDED):
        full[n] = jnp.concatenate([small_all[s][idx] for s in range(N_DEV)], axis=-1)
    late = ["gdn_w_out", "ssd_w_in", "ssd_w_out"]

    def assemble(gathered):
        return {"gdn_w_out": gathered[0].reshape(-1, D_MODEL), "ssd_w_in": _shards_to_cols(gathered[1]),
                "ssd_w_out": gathered[2].reshape(-1, D_MODEL)}

    def early_grads(d_ssd_w_in, d_ssd_w_out):
        return ([_cols_to_shards(d_ssd_w_in).astype(GRAD_WIRE_DTYPE),
                 d_ssd_w_out.reshape(N_DEV, -1, D_MODEL).astype(GRAD_WIRE_DTYPE)], [False] * 2)

    loss, dx, grads, ssd_recv = _local_step(x[0], loss_target[0], full,
                                            (([_mx(w[n]) for n in late], [True] * 3), assemble), early_grads)

    send_small = jnp.concatenate(
        [_cols_to_shards(grads[n]).reshape(N_DEV, -1) for n in SMALL_SHARDED], axis=1)[:, None, :]
    rep_shapes = [w[n].shape for n in REPLICATED]
    recv = _exchange(
        [_cols_to_shards(grads["gdn_w_in"]).astype(GRAD_WIRE_DTYPE),
         grads["gdn_w_out"].reshape(N_DEV, -1, D_MODEL).astype(GRAD_WIRE_DTYPE),
         send_small, _pack([grads[n] for n in REPLICATED])],
        [False] * 3 + [True], "exchange_grads")

    res = {}
    for n, parts in zip(["gdn_w_in", "gdn_w_out", "ssd_w_in", "ssd_w_out"], [recv[0], recv[1]] + list(ssd_recv)):
        res[n] = _adamw(parts, w[n], m[n], v[n], "adamw_" + n)
    small_res = _adamw(recv[2], *[_pack([t[n] for n in SMALL_SHARDED]) for t in (w, m, v)], "adamw_small")
    rep_res = _adamw(recv[3], *[_pack([t[n] for n in REPLICATED]) for t in (w, m, v)], "adamw_replicated")
    for k4 in range(4):
        for n, a in zip(SMALL_SHARDED, _unpack(small_res[k4][0], small_shapes)):
            res.setdefault(n, [None] * 4)[k4] = a
        for n, a in zip(REPLICATED, _unpack(rep_res[k4][0], rep_shapes)):
            res.setdefault(n, [None] * 4)[k4] = a

    loss = lax.psum(loss[0, 0], ("x", "y", "c"))
    outs = [loss, dx[None]]
    for k4 in range(4):
        outs += [res[n][k4].reshape(out_shapes[n]) for n in WEIGHTS]
    return tuple(outs)
```

```python
import jax
import jax.numpy as jnp
from jax import lax
from jax.experimental import pallas as pl
from jax.experimental.pallas import tpu as pltpu

F32 = jnp.float32
MXU_DTYPE = jnp.bfloat16
GRAD_WIRE_DTYPE = jnp.bfloat16
HI = lax.Precision.HIGHEST
EPS = 1e-6
VMEM_LIMIT_BYTES = 56 * 1024 * 1024
N_DEV = 8
MESH = pl.DeviceIdType.MESH

D_MODEL = 1024
CONV_K = 4
GDN_HV = 16
GDN_DK = 128
GDN_CHUNK = 64
SSD_H = 32
SSD_P = 64
SSD_N = 128
SSD_G = 8
SSD_R = SSD_H // SSD_G
SSD_CHUNK = 128
D_INNER = 2048
PAD_W = 128

ADAM_LR = 0.001
ADAM_B1 = 0.9
ADAM_B2 = 0.999
ADAM_EPS = 1e-08
ADAM_WD = 0.01
ADAM_STEP = 10


def _params(*sem):
    return pltpu.CompilerParams(dimension_semantics=sem, vmem_limit_bytes=VMEM_LIMIT_BYTES)


def _mx(a):
    return a.astype(MXU_DTYPE)


def _dot(a, b):
    return jnp.dot(_mx(a), _mx(b), preferred_element_type=F32)


def _dot_nt(a, b):
    return lax.dot_general(_mx(a), _mx(b), (((1,), (1,)), ((), ())), preferred_element_type=F32)


def _dot_tn(a, b):
    return lax.dot_general(_mx(a), _mx(b), (((0,), (0,)), ((), ())), preferred_element_type=F32)


def _dot_hi(a, b):
    return jnp.dot(a, b, precision=HI, preferred_element_type=F32)


def _sigmoid(x):
    return 0.5 * jnp.tanh(0.5 * x) + 0.5


def _silu(x):
    return x * _sigmoid(x)


def _dsilu(x):
    s = _sigmoid(x)
    return s * (1.0 + x * (1.0 - s))


def _softplus(x):
    return jnp.maximum(x, 0.0) + jnp.log1p(jnp.exp(-jnp.abs(x)))


def _col(r, eye):
    return jnp.sum(jnp.where(eye, r, 0.0), axis=1, keepdims=True)


def _row(c, eye):
    return jnp.sum(jnp.where(eye, c, 0.0), axis=0, keepdims=True)


def _col_bcast(r, n):
    return jnp.broadcast_to(r, (n, n)).T


def _masks(n):
    r = lax.broadcasted_iota(jnp.int32, (n, n), 0)
    c = lax.broadcasted_iota(jnp.int32, (n, n), 1)
    return r >= c, r > c, r == c, r, c


def _with_exchange(comm):
    arrs, bcast = comm if comm else ([], [])
    nc = len(arrs)
    anyspec = pl.BlockSpec(memory_space=pl.ANY)

    def wrap(compute, n_in, n_out):
        def body(*refs):
            cin, cout = refs[n_in:n_in + nc], refs[n_in + nc + n_out:n_in + 2 * nc + n_out]
            sems = refs[n_in + 2 * nc + n_out:n_in + 2 * nc + n_out + 3]
            rest = refs[:n_in] + refs[n_in + nc:n_in + nc + n_out] + refs[n_in + 2 * nc + n_out + (3 if nc else 0):]
            if nc:
                @pl.when(pl.program_id(0) == 0)
                def _():
                    _Exchange(cin, cout, bcast, *sems).begin()
            compute(*rest)
            if nc:
                @pl.when(pl.program_id(0) == pl.num_programs(0) - 1)
                def _():
                    _Exchange(cin, cout, bcast, *sems).finish()
        return body

    return dict(arrs=list(arrs), nc=nc, wrap=wrap, in_specs=[anyspec] * nc, out_specs=[anyspec] * nc,
                out_shape=_exchange_out_shapes(arrs, bcast), scratch=_exchange_semaphores(nc) if nc else [])


INPROJ_COL_BLOCK = 512


def _norm_inproj(x, nw, wparts, convs, name, comm=None):
    T = x.shape[0]
    tt = min(T, 256)
    n = len(wparts)
    ck = [k for k in range(n) if convs[k] is not None]
    nconv = len(ck)
    ex = _with_exchange(comm)

    def compute(x_ref, nw_ref, *refs):
        w_refs, cw_refs = refs[:n], refs[n:n + 2 * nconv]
        h_ref, o_refs = refs[n + 2 * nconv], refs[n + 2 * nconv + 1:2 * n + 2 * nconv + 1]
        post_refs = refs[2 * n + 2 * nconv + 1:2 * n + 3 * nconv + 1]
        cpre_refs = refs[2 * n + 3 * nconv + 1:2 * n + 4 * nconv + 1]
        p_refs = refs[2 * n + 4 * nconv + 1:]
        xv = x_ref[...]
        r = lax.rsqrt(jnp.mean(xv * xv, axis=-1, keepdims=True) + EPS)
        h = _mx(xv * r * nw_ref[...])
        h_ref[...] = h
        for m in range(nconv):
            @pl.when(pl.program_id(0) == 0)
            def _():
                p_refs[m][0:HALO, :] = jnp.zeros((HALO, p_refs[m].shape[1]), F32)

        def conv_block(k, c0, cw):
            m = ck.index(k)
            _, _, l2, scale = convs[k]
            cw_ref, cb_ref, out_ref, P = cw_refs[2 * m], cw_refs[2 * m + 1], post_refs[m], p_refs[m]
            cs = slice(c0, c0 + cw)
            acc = cb_ref[:, cs] + cw_ref[0:1, cs] * P[pl.ds(HALO - 3, tt), cs]
            for j in range(1, CONV_K):
                acc = acc + cw_ref[j:j + 1, cs] * P[pl.ds(HALO - 3 + j, tt), cs]
            cpre_refs[m][:, cs] = acc
            s = _silu(acc)
            if l2:
                sls = [slice(g0, g0 + GDN_DK) for g0 in range(0, cw, GDN_DK)]
                rr = [lax.rsqrt(jnp.sum(s[:, sl] * s[:, sl], axis=-1, keepdims=True) + EPS) for sl in sls]
                for sl, rg in zip(sls, rr):
                    out_ref[:, c0 + sl.start:c0 + sl.stop] = s[:, sl] * rg * scale
            else:
                out_ref[:, cs] = s
            P[0:HALO, cs] = P[tt:tt + HALO, cs]

        pending = None
        for k in range(n):
            for c0 in range(0, widths[k], INPROJ_COL_BLOCK):
                cw = min(INPROJ_COL_BLOCK, widths[k] - c0)
                pre = jnp.dot(h, w_refs[k][:, c0:c0 + cw], preferred_element_type=F32)
                o_refs[k][:, c0:c0 + cw] = pre
                if convs[k] is not None:
                    p_refs[ck.index(k)][HALO:HALO + tt, c0:c0 + cw] = pre
                if pending is not None:
                    conv_block(*pending)
                pending = (k, c0, cw) if convs[k] is not None else None
        if pending is not None:
            conv_block(*pending)

    row = lambda width: pl.BlockSpec((tt, width), lambda i: (i, 0))
    full = lambda a: pl.BlockSpec(a.shape, lambda i: (0, 0))
    once = lambda a: pl.BlockSpec(a.shape, lambda i: (0, 0), pipeline_mode=pl.Buffered(1))
    conv_args = [a for k in ck for a in convs[k][:2]]
    widths = [w.shape[1] for w in wparts]
    outs = pl.pallas_call(
        ex["wrap"](compute, 2 + n + 2 * nconv, 1 + n + 2 * nconv), grid=(T // tt,),
        in_specs=[row(D_MODEL), full(nw)] + [once(w) for w in wparts] + [full(a) for a in conv_args] + ex["in_specs"],
        out_specs=[row(D_MODEL)] + [row(wd) for wd in widths] + [row(widths[k]) for k in ck + ck] + ex["out_specs"],
        out_shape=[jax.ShapeDtypeStruct((T, D_MODEL), MXU_DTYPE)]
        + [jax.ShapeDtypeStruct((T, wd), F32) for wd in widths]
        + [jax.ShapeDtypeStruct((T, widths[k]), F32) for k in ck + ck] + ex["out_shape"],
        scratch_shapes=ex["scratch"] + [pltpu.VMEM((HALO + tt, widths[k]), F32) for k in ck],
        compiler_params=_params("arbitrary"), name=name,
    )(x, nw, *wparts, *conv_args, *ex["arrs"])
    outs = list(outs)
    res = (outs[0], outs[1:1 + n], outs[1 + n:1 + n + nconv], outs[1 + n + nconv:1 + n + 2 * nconv])
    return res + (outs[1 + n + 2 * nconv:],) if comm else res


def _inproj_bwd(x, nw, dparts, wparts, dres, name):
    T = x.shape[0]
    tt = min(T, 256)
    n = len(wparts)

    def body(x_ref, nw_ref, dres_ref, *refs):
        d_refs, w_refs, dx_ref, dnw_ref = refs[:n], refs[n:2 * n], refs[2 * n], refs[2 * n + 1]

        @pl.when(pl.program_id(0) == 0)
        def _():
            dnw_ref[...] = jnp.zeros_like(dnw_ref)

        dh = _dot_nt(d_refs[0][...], w_refs[0][...])
        for d_ref, w_ref in zip(d_refs[1:], w_refs[1:]):
            dh = dh + _dot_nt(d_ref[...], w_ref[...])
        xv = x_ref[...]
        r = lax.rsqrt(jnp.mean(xv * xv, axis=-1, keepdims=True) + EPS)
        xh = xv * r
        dnw_ref[...] += jnp.sum(dh * xh, axis=0, keepdims=True)
        dxn = dh * nw_ref[...]
        dx_ref[...] = dres_ref[...] + r * (dxn - xh * jnp.mean(dxn * xh, axis=-1, keepdims=True))

    row = lambda width: pl.BlockSpec((tt, width), lambda i: (i, 0))
    full = lambda a: pl.BlockSpec(a.shape, lambda i: (0, 0))
    return pl.pallas_call(
        body, grid=(T // tt,),
        in_specs=[row(D_MODEL), full(nw), row(D_MODEL)] + [row(d.shape[1]) for d in dparts]
        + [full(w) for w in wparts],
        out_specs=[row(D_MODEL), pl.BlockSpec((1, D_MODEL), lambda i: (0, 0))],
        out_shape=[jax.ShapeDtypeStruct((T, D_MODEL), F32), jax.ShapeDtypeStruct((1, D_MODEL), F32)],
        compiler_params=_params("arbitrary"), name=name,
    )(x, nw, dres, *dparts, *wparts)


def _matmul_tn(a, b, name):
    T, K = a.shape
    N = b.shape[1]
    tt = min(T, 512)
    tn = min(N, 1024)

    def body(a_ref, b_ref, o_ref):
        @pl.when(pl.program_id(1) == 0)
        def _():
            o_ref[...] = jnp.zeros_like(o_ref)

        o_ref[...] += _dot_tn(a_ref[...], b_ref[...])

    return pl.pallas_call(
        body, grid=(N // tn, T // tt),
        in_specs=[pl.BlockSpec((tt, K), lambda n, t: (t, 0)), pl.BlockSpec((tt, tn), lambda n, t: (t, n))],
        out_specs=pl.BlockSpec((K, tn), lambda n, t: (0, n)),
        out_shape=jax.ShapeDtypeStruct((K, N), F32),
        compiler_params=_params("parallel", "arbitrary"), name=name,
    )(a, b)


OUT_COL_BLOCK = 512


def _out_fwd(o, z, w, wout, xres, gs, gate_first, name):
    T = o.shape[0]
    tt = min(T, 256)
    wide = w.shape[1] == D_INNER

    def body(o_ref, z_ref, w_ref, wout_ref, x_ref, out_ref, yn):
        acc = x_ref[...]
        pending = None
        for b0 in range(0, D_INNER, OUT_COL_BLOCK):
            for g0 in range(b0, b0 + OUT_COL_BLOCK, gs):
                sl = slice(g0, g0 + gs)
                og, zg = o_ref[:, sl], z_ref[:, sl]
                wg = w_ref[:, sl] if wide else w_ref[...]
                if gate_first:
                    u = og * _silu(zg)
                    r = lax.rsqrt(jnp.mean(u * u, axis=-1, keepdims=True) + EPS)
                    yn[:, sl] = _mx(u * r * wg)
                else:
                    r = lax.rsqrt(jnp.mean(og * og, axis=-1, keepdims=True) + EPS)
                    yn[:, sl] = _mx(og * r * wg * _silu(zg))
            if pending is not None:
                acc = acc + jnp.dot(yn[:, pending], wout_ref[pending, :], preferred_element_type=F32)
            pending = slice(b0, b0 + OUT_COL_BLOCK)
        out_ref[...] = acc + jnp.dot(yn[:, pending], wout_ref[pending, :], preferred_element_type=F32)

    row = lambda width: pl.BlockSpec((tt, width), lambda i: (i, 0))
    full = lambda a: pl.BlockSpec(a.shape, lambda i: (0, 0))
    return pl.pallas_call(
        body, grid=(T // tt,),
        in_specs=[row(D_INNER), row(D_INNER), full(w), full(wout), row(D_MODEL)],
        out_specs=row(D_MODEL),
        out_shape=jax.ShapeDtypeStruct((T, D_MODEL), F32),
        scratch_shapes=[pltpu.VMEM((tt, D_INNER), MXU_DTYPE)],
        compiler_params=_params("parallel"), name=name,
    )(o, z, w, wout, xres)


def _out_bwd(dx, o, z, w, wout, gs, gate_first, name, comm=None):
    T = o.shape[0]
    tt = min(T, 256)
    wide = w.shape[1] == D_INNER

    def body(dx_ref, o_ref, z_ref, w_ref, wout_ref, do_ref, dz_ref, dw_ref, yn_ref):
        @pl.when(pl.program_id(0) == 0)
        def _():
            dw_ref[...] = jnp.zeros_like(dw_ref)

        dxb = _mx(dx_ref[...])
        blocks = list(range(0, D_INNER, OUT_COL_BLOCK))
        dyn_b = {b0: _dot_nt(dxb, wout_ref[b0:b0 + OUT_COL_BLOCK, :]) for b0 in blocks[:1]}
        dw_acc = jnp.zeros((1, gs), F32)
        for g0 in range(0, D_INNER, gs):
            b0 = g0 - g0 % OUT_COL_BLOCK
            if g0 == b0 and b0 + OUT_COL_BLOCK < D_INNER:
                nb = b0 + OUT_COL_BLOCK
                dyn_b[nb] = _dot_nt(dxb, wout_ref[nb:nb + OUT_COL_BLOCK, :])
            sl = slice(g0, g0 + gs)
            og, zg, dg = o_ref[:, sl], z_ref[:, sl], dyn_b[b0][:, g0 - b0:g0 - b0 + gs]
            wg = w_ref[:, sl] if wide else w_ref[...]
            sz = _silu(zg)
            if gate_first:
                u = og * sz
                r = lax.rsqrt(jnp.mean(u * u, axis=-1, keepdims=True) + EPS)
                uh = u * r
                yn_ref[:, sl] = _mx(uh * wg)
                dw_g = jnp.sum(dg * uh, axis=0, keepdims=True)
                duh = dg * wg
                du = r * (duh - uh * jnp.mean(duh * uh, axis=-1, keepdims=True))
                do_ref[:, sl] = du * sz
                dz_ref[:, sl] = _mx(du * og * _dsilu(zg))
            else:
                r = lax.rsqrt(jnp.mean(og * og, axis=-1, keepdims=True) + EPS)
                oh = og * r
                yn_ref[:, sl] = _mx(oh * wg * sz)
                dw_g = jnp.sum(dg * oh * sz, axis=0, keepdims=True)
                doh = dg * wg * sz
                dz_ref[:, sl] = _mx(dg * oh * wg * _dsilu(zg))
                do_ref[:, sl] = r * (doh - oh * jnp.mean(doh * oh, axis=-1, keepdims=True))
            if wide:
                dw_ref[:, sl] += dw_g
            else:
                dw_acc = dw_acc + dw_g
        if not wide:
            dw_ref[...] += dw_acc

    row = lambda width: pl.BlockSpec((tt, width), lambda i: (i, 0))
    full = lambda a: pl.BlockSpec(a.shape, lambda i: (0, 0))
    ex = _with_exchange(comm)
    outs = pl.pallas_call(
        ex["wrap"](body, 5, 4), grid=(T // tt,),
        in_specs=[row(D_MODEL), row(D_INNER), row(D_INNER), full(w), full(wout)] + ex["in_specs"],
        out_specs=[row(D_INNER), row(D_INNER), full(w), row(D_INNER)] + ex["out_specs"],
        out_shape=[jax.ShapeDtypeStruct((T, D_INNER), F32), jax.ShapeDtypeStruct((T, D_INNER), MXU_DTYPE),
                   jax.ShapeDtypeStruct(w.shape, F32), jax.ShapeDtypeStruct((T, D_INNER), MXU_DTYPE)]
        + ex["out_shape"],
        scratch_shapes=ex["scratch"],
        compiler_params=_params("arbitrary"), name=name,
    )(dx, o, z, w, wout, *ex["arrs"])
    outs = list(outs)
    return outs[:4] + ([outs[4:]] if comm else [])


HALO = 8
CONV_STRIP = 16


def _conv_bwd(pre, cpre_all, w, dpost, l2, scale, name):
    T, C = pre.shape
    tt = min(T, 512)
    tc = min(C, 1024 if l2 else 512)
    strip = tt if l2 else CONV_STRIP
    nT = T // tt
    ext = tt + HALO

    def body(pre_ref, cp_ref, cn_ref, dpost_ref, dn_ref, w_ref, dpre_ref, dw_ref, db_ref, Q):
        i = pl.program_id(1)

        @pl.when(i == 0)
        def _():
            dw_ref[...] = jnp.zeros_like(dw_ref)
            db_ref[...] = jnp.zeros_like(db_ref)

        wj = [w_ref[j:j + 1, :] for j in range(CONV_K)]
        keep_next = jnp.where(i < nT - 1, 1.0, 0.0)
        fold = lambda a: jnp.sum(a.reshape(strip // 8, 8, tc), axis=0)
        dw_acc = [jnp.zeros((8, tc), F32) for _ in range(CONV_K)]
        db_acc = jnp.zeros((8, tc), F32)
        for r0 in list(range(0, tt, strip)) + [tt]:
            n = strip if r0 < tt else HALO
            cpre = cp_ref[r0:r0 + n, :] if r0 < tt else cn_ref[...]
            dy = dpost_ref[r0:r0 + n, :] if r0 < tt else dn_ref[...] * keep_next
            sg = _sigmoid(cpre)
            ds_c = sg * (1.0 + cpre * (1.0 - sg))
            if l2:
                s = cpre * sg
                sls = [slice(g0, g0 + GDN_DK) for g0 in range(0, tc, GDN_DK)]
                rr = [lax.rsqrt(jnp.sum(s[:, sl] * s[:, sl], axis=-1, keepdims=True) + EPS) for sl in sls]
                yh = [s[:, sl] * r for sl, r in zip(sls, rr)]
                pr = [jnp.sum(dy[:, sl] * y, axis=-1, keepdims=True) for sl, y in zip(sls, yh)]
                for sl, r, y, p in zip(sls, rr, yh, pr):
                    Q[r0:r0 + n, sl] = (scale * r) * (dy[:, sl] - y * p) * ds_c[:, sl]
                dyc = Q[r0:r0 + n, :]
            else:
                dyc = dy * ds_c
                Q[r0:r0 + n, :] = dyc
            if r0 < tt:
                db_acc = db_acc + fold(dyc)
        for r0 in range(0, tt, strip):
            xs = pre_ref[r0:r0 + strip, :]
            dpre = jnp.zeros((strip, tc), F32)
            for j in range(CONV_K):
                qj = Q[pl.ds(3 - j + r0, strip), :]
                dpre = dpre + wj[j] * qj
                dw_acc[j] = dw_acc[j] + fold(qj * xs)
            dpre_ref[r0:r0 + strip, :] = _mx(dpre)
        for j in range(CONV_K):
            dw_ref[j:j + 1, :] += jnp.sum(dw_acc[j], axis=0, keepdims=True)
        db_ref[...] += jnp.sum(db_acc, axis=0, keepdims=True)

    tile = pl.BlockSpec((tt, tc), lambda j, i: (i, j))
    nxt = pl.BlockSpec((HALO, tc), lambda j, i: (jnp.minimum((i + 1) * (tt // HALO), T // HALO - 1), j))
    return pl.pallas_call(
        body, grid=(C // tc, nT),
        in_specs=[tile, tile, nxt, tile, nxt, pl.BlockSpec((CONV_K, tc), lambda j, i: (0, j))],
        out_specs=[tile, pl.BlockSpec((CONV_K, tc), lambda j, i: (0, j)), pl.BlockSpec((1, tc), lambda j, i: (0, j))],
        out_shape=[jax.ShapeDtypeStruct((T, C), MXU_DTYPE), jax.ShapeDtypeStruct((CONV_K, C), F32),
                   jax.ShapeDtypeStruct((1, C), F32)],
        scratch_shapes=[pltpu.VMEM((ext, tc), F32)],
        compiler_params=_params("parallel", "arbitrary"), name=name,
    )(pre, cpre_all, cpre_all, dpost, dpost, w)


GDN_LOCKSTEP_CHUNKS = 16
GDN_SCAN_HEADS = 16


def _inv_unit_lower_many(nms, eye, n):
    xs = [jnp.where(eye, 1.0, 0.0) - nm for nm in nms]
    ps = list(nms)
    k = 2
    while k < n:
        ps = [_dot(p, p) for p in ps]
        xs = [x + _dot(x, p) for x, p in zip(xs, ps)]
        k *= 2
    return xs


def _gdn_prep(q, k, v, araw, braw, alog, dtb, name):
    T = q.shape[0]
    C = GDN_CHUNK
    tt = min(T, 1024)
    cpt, nC = tt // C, T // C
    grp = min(cpt, GDN_LOCKSTEP_CHUNKS)

    def body(alog_ref, dtb_ref, q_ref, k_ref, v_ref, a_ref, b_ref,
             u_ref, w_ref, pm_ref, ti_ref, g_ref, beta_ref, gc_ref, qd_ref, kd_ref):
        j = pl.program_id(0)
        tri, strict, eye, r_i, c_i = _masks(C)
        upper = jnp.where(r_i <= c_i, 1.0, 0.0)
        gcs, bts = [], []
        for hh in range(2):
            h = 2 * j + hh
            g = -jnp.exp(alog_ref[h]) * _softplus(a_ref[hh] + dtb_ref[h])
            bt = _sigmoid(b_ref[hh])
            gc = _dot_hi(g, upper)
            g_ref[hh], beta_ref[hh], gc_ref[hh] = g, bt, gc
            gcs.append(gc)
            bts.append(bt)
        for c0 in range(0, cpt, grp):
            cs = list(range(c0, c0 + grp))
            inst = [(c, hh) for c in cs for hh in range(2)]
            rows = {c: slice(c * C, (c + 1) * C) for c in cs}
            qc = {c: q_ref[rows[c], :] for c in cs}
            kc = {c: k_ref[rows[c], :] for c in cs}
            kk = {c: _dot_nt(kc[c], kc[c]) for c in cs}
            qk = {c: _dot_nt(qc[c], kc[c]) for c in cs}
            gcr = [gcs[hh][c:c + 1, :] for c, hh in inst]
            gcc = [_col(r, eye) for r in gcr]
            bc = [_col(bts[hh][c:c + 1, :], eye) for c, hh in inst]
            lm = [jnp.exp(jnp.where(tri, cc - r, -1e30)) for cc, r in zip(gcc, gcr)]
            nm = [jnp.where(strict, kk[c] * b * l, 0.0) for (c, hh), b, l in zip(inst, bc, lm)]
            tinv = _inv_unit_lower_many(nm, eye, C)
            e_c = [jnp.exp(cc) for cc in gcc]
            rhs = [jnp.concatenate([v_ref[rows[c], hh * GDN_DK:(hh + 1) * GDN_DK] * b, kc[c] * (b * e)], axis=1)
                   for (c, hh), b, e in zip(inst, bc, e_c)]
            sol = [_dot(t, r) for t, r in zip(tinv, rhs)]
            for (c, hh), s, t, l, e, cc, r in zip(inst, sol, tinv, lm, e_c, gcc, gcr):
                hs = slice(hh * GDN_DK, (hh + 1) * GDN_DK)
                u_ref[rows[c], hs] = s[:, :GDN_DK]
                w_ref[rows[c], hs] = _mx(s[:, GDN_DK:])
                pm_ref[hh, c] = _mx(jnp.where(tri, qk[c] * l, 0.0))
                ti_ref[hh, c] = _mx(t)
                qd_ref[rows[c], hs] = _mx(qc[c] * e)
                kd_ref[rows[c], hs] = _mx(kc[c] * jnp.exp(r[:, C - 1:C] - cc))

    smem = pl.BlockSpec(memory_space=pltpu.SMEM)
    rows_spec = pl.BlockSpec((2, cpt, C), lambda j, i: (j, i, 0))
    qk_spec = pl.BlockSpec((tt, GDN_DK), lambda j, i: (i, j))
    v_spec = pl.BlockSpec((tt, 2 * GDN_DK), lambda j, i: (i, j))
    cc_spec = pl.BlockSpec((2, cpt, C, C), lambda j, i: (j, i, 0, 0))
    rows_shape = jax.ShapeDtypeStruct((GDN_HV, nC, C), F32)
    cc_shape = jax.ShapeDtypeStruct((GDN_HV, nC, C, C), MXU_DTYPE)
    return pl.pallas_call(
        body, grid=(GDN_HV // 2, T // tt),
        in_specs=[smem, smem, qk_spec, qk_spec, v_spec, rows_spec, rows_spec],
        out_specs=[v_spec, v_spec, cc_spec, cc_spec, rows_spec, rows_spec, rows_spec, v_spec, v_spec],
        out_shape=[jax.ShapeDtypeStruct((T, D_INNER), F32), jax.ShapeDtypeStruct((T, D_INNER), MXU_DTYPE),
                   cc_shape, cc_shape, rows_shape, rows_shape, rows_shape,
                   jax.ShapeDtypeStruct((T, D_INNER), MXU_DTYPE), jax.ShapeDtypeStruct((T, D_INNER), MXU_DTYPE)],
        compiler_params=_params("parallel", "parallel"), name=name,
    )(alog, dtb, q, k, v, araw, braw)


def _gdn_state_fwd(q, k, u, w, pm, gc, name):
    T = q.shape[0]
    C = GDN_CHUNK
    HG = GDN_SCAN_HEADS
    tt = min(T, 512)
    cpt, nC = tt // C, T // C

    def body(q_ref, k_ref, u_ref, w_ref, pm_ref, gc_ref, o_ref, vn_ref, sall_ref, S):
        @pl.when(pl.program_id(1) == 0)
        def _():
            S[...] = jnp.zeros_like(S)

        heads = list(range(HG))

        def chunk(c, carry):
            rows = pl.ds(pl.multiple_of(c * C, C), C)
            hs = [slice(h * GDN_DK, (h + 1) * GDN_DK) for h in heads]
            gl = [jnp.exp(gc_ref[h, pl.ds(c, 1), C - 1:C]) for h in heads]
            sv = [S[h] for h in heads]
            for h in heads:
                sall_ref[h, c] = _mx(sv[h])
            ws = [_dot(w_ref[rows, hs[h]], sv[h]) for h in heads]
            qsv = [_dot(q_ref[rows, hs[h]], sv[h]) for h in heads]
            vn = [u_ref[rows, hs[h]] - ws[h] for h in heads]
            pv = [_dot(pm_ref[h, c], vn[h]) for h in heads]
            kv = [_dot_tn(k_ref[rows, hs[h]], vn[h]) for h in heads]
            for h in heads:
                vn_ref[rows, hs[h]] = _mx(vn[h])
                o_ref[rows, hs[h]] = qsv[h] + pv[h]
                S[h] = sv[h] * gl[h] + kv[h]
            return carry

        lax.fori_loop(0, cpt, chunk, 0)

    v_spec = pl.BlockSpec((tt, HG * GDN_DK), lambda g, i: (i, g))
    return pl.pallas_call(
        body, grid=(GDN_HV // HG, T // tt),
        in_specs=[v_spec, v_spec, v_spec, v_spec,
                  pl.BlockSpec((HG, cpt, C, C), lambda g, i: (g, i, 0, 0)),
                  pl.BlockSpec((HG, cpt, C), lambda g, i: (g, i, 0))],
        out_specs=[v_spec, v_spec, pl.BlockSpec((HG, cpt, GDN_DK, GDN_DK), lambda g, i: (g, i, 0, 0))],
        out_shape=[jax.ShapeDtypeStruct((T, D_INNER), F32), jax.ShapeDtypeStruct((T, D_INNER), MXU_DTYPE),
                   jax.ShapeDtypeStruct((GDN_HV, nC, GDN_DK, GDN_DK), MXU_DTYPE)],
        scratch_shapes=[pltpu.VMEM((HG, GDN_DK, GDN_DK), F32)],
        compiler_params=_params("parallel", "arbitrary"), name=name,
    )(q, k, u, w, pm, gc)


def _gdn_state_bwd(q, k, w, pm, vn, sall, gc, do, name):
    T = q.shape[0]
    C = GDN_CHUNK
    HG = GDN_SCAN_HEADS
    tt = min(T, 512)
    cpt, nC, nT = tt // C, T // C, T // tt

    def body(q_ref, k_ref, w_ref, pm_ref, vn_ref, sall_ref, gc_ref, do_ref, dvn_ref, dkd_ref, dgl_ref, dS):
        @pl.when(pl.program_id(1) == 0)
        def _():
            dS[...] = jnp.zeros_like(dS)

        heads = list(range(HG))

        def chunk(ci, carry):
            c = cpt - 1 - ci
            rows = pl.ds(pl.multiple_of(c * C, C), C)
            hs = [slice(h * GDN_DK, (h + 1) * GDN_DK) for h in heads]
            gl = [jnp.exp(gc_ref[h, pl.ds(c, 1), C - 1:C]) for h in heads]
            dsn = [dS[h] for h in heads]
            doc = [do_ref[rows, hs[h]] for h in heads]
            kds = [_dot(k_ref[rows, hs[h]], dsn[h]) for h in heads]
            pdo = [_dot_tn(pm_ref[h, c], doc[h]) for h in heads]
            dkd = [_dot_nt(vn_ref[rows, hs[h]], dsn[h]) for h in heads]
            qdo = [_dot_tn(q_ref[rows, hs[h]], doc[h]) for h in heads]
            dvn = [pdo[h] + kds[h] for h in heads]
            wdv = [_dot_tn(w_ref[rows, hs[h]], dvn[h]) for h in heads]
            for h in heads:
                dgl = jnp.sum(jnp.sum(dsn[h] * sall_ref[h, c].astype(F32), axis=0, keepdims=True), axis=1, keepdims=True)
                dgl_ref[h, pl.ds(c, 1), :] = jnp.broadcast_to(dgl, (1, C))
                dvn_ref[rows, hs[h]] = dvn[h]
                dkd_ref[rows, hs[h]] = dkd[h]
                dS[h] = dsn[h] * gl[h] + qdo[h] - wdv[h]
            return carry

        lax.fori_loop(0, cpt, chunk, 0)

    rev = lambda i: nT - 1 - i
    v_spec = pl.BlockSpec((tt, HG * GDN_DK), lambda g, i: (rev(i), g))
    rows_spec = pl.BlockSpec((HG, cpt, C), lambda g, i: (g, rev(i), 0))
    return pl.pallas_call(
        body, grid=(GDN_HV // HG, nT),
        in_specs=[v_spec, v_spec, v_spec, pl.BlockSpec((HG, cpt, C, C), lambda g, i: (g, rev(i), 0, 0)), v_spec,
                  pl.BlockSpec((HG, cpt, GDN_DK, GDN_DK), lambda g, i: (g, rev(i), 0, 0)), rows_spec, v_spec],
        out_specs=[v_spec, v_spec, rows_spec],
        out_shape=[jax.ShapeDtypeStruct((T, D_INNER), F32), jax.ShapeDtypeStruct((T, D_INNER), F32),
                   jax.ShapeDtypeStruct((GDN_HV, nC, C), F32)],
        scratch_shapes=[pltpu.VMEM((HG, GDN_DK, GDN_DK), F32)],
        compiler_params=_params("parallel", "arbitrary"), name=name,
    )(q, k, w, pm, vn, sall, gc, do)


def _gdn_local_bwd(q, k, v, gc, beta, tinv, u, w, pm, vn, sall, do, dvn, dkd, dgl, name):
    T = q.shape[0]
    C = GDN_CHUNK
    tt = min(T, 1024)
    cpt, nC = tt // C, T // C
    grp = min(cpt, GDN_LOCKSTEP_CHUNKS)

    def body(q_ref, k_ref, v_ref, gc_ref, b_ref, ti_ref, u_ref, w_ref, pm_ref, vn_ref, sall_ref, do_ref,
             dvn_ref, dkd_ref, dgl_ref, dq_ref, dk_ref, dv_ref, dg_ref, dbeta_ref, dgc_s):
        tri, strict, eye, r_i, c_i = _masks(C)
        lower = jnp.where(r_i >= c_i, 1.0, 0.0)
        lane = lax.broadcasted_iota(jnp.int32, (1, C), 1)
        rsum = lambda a: jnp.sum(a, axis=1, keepdims=True)
        for c0 in range(0, cpt, grp):
            cs = list(range(c0, c0 + grp))
            inst = [(c, hh) for c in cs for hh in range(2)]
            n = len(inst)
            rows = {c: slice(c * C, (c + 1) * C) for c in cs}
            hsl = [slice(hh * GDN_DK, (hh + 1) * GDN_DK) for c, hh in inst]
            qc = {c: q_ref[rows[c], :] for c in cs}
            kc = {c: k_ref[rows[c], :] for c in cs}
            kk = {c: _dot_nt(kc[c], kc[c]) for c in cs}
            gcr = [gc_ref[hh, c:c + 1, :] for c, hh in inst]
            gcc = [_col(r, eye) for r in gcr]
            bc = [_col(b_ref[hh, c:c + 1, :], eye) for c, hh in inst]
            lm = [jnp.exp(jnp.where(tri, cc - r, -1e30)) for cc, r in zip(gcc, gcr)]
            e_c = [jnp.exp(cc) for cc in gcc]
            el_c = [jnp.exp(r[:, C - 1:C] - cc) for cc, r in zip(gcc, gcr)]
            gl = [jnp.exp(r[:, C - 1:C]) for r in gcr]
            doc = [do_ref[rows[c], hsl[i]] for i, (c, hh) in enumerate(inst)]
            dvn = [dvn_ref[rows[c], hsl[i]] for i, (c, hh) in enumerate(inst)]
            sv = [sall_ref[hh, c] for c, hh in inst]
            aa = [_dot_nt(jnp.concatenate([_mx(doc[i]), _mx(dvn[i])], axis=0), sv[i]) for i in range(n)]
            dpm = [jnp.where(tri, _dot_nt(doc[i], vn_ref[rows[c], hsl[i]]), 0.0) for i, (c, hh) in enumerate(inst)]
            dqd = [a[:C] for a in aa]
            drhs = [_dot_tn(ti_ref[hh, c], jnp.concatenate([dvn[i], -aa[i][C:]], axis=1))
                    for i, (c, hh) in enumerate(inst)]
            sol = [jnp.concatenate([_mx(u_ref[rows[c], hsl[i]]), w_ref[rows[c], hsl[i]]], axis=1)
                   for i, (c, hh) in enumerate(inst)]
            dnm = [-jnp.where(strict, _dot_nt(drhs[i], sol[i]), 0.0) for i in range(n)]
            dkk = [dnm[i] * bc[i] * lm[i] for i in range(n)]
            dqk = [dpm[i] * lm[i] for i in range(n)]
            dq1 = [_dot(dqk[i], kc[c]) for i, (c, hh) in enumerate(inst)]
            dk1 = [_dot(dkk[i], kc[c]) for i, (c, hh) in enumerate(inst)]
            dk2 = [_dot_tn(dkk[i], kc[c]) for i, (c, hh) in enumerate(inst)]
            dk3 = [_dot_tn(dqk[i], qc[c]) for i, (c, hh) in enumerate(inst)]
            dq_acc = {c: jnp.zeros((C, GDN_DK), F32) for c in cs}
            dk_acc = {c: jnp.zeros((C, GDN_DK), F32) for c in cs}
            for i, (c, hh) in enumerate(inst):
                k_, q_, v_ = kc[c], qc[c], v_ref[rows[c], hsl[i]]
                dvb, dkbe = drhs[i][:, :GDN_DK], drhs[i][:, GDN_DK:]
                dkd = dkd_ref[rows[c], hsl[i]]
                kb = k_ * bc[i]
                dkb = dkbe * e_c[i]
                del_el = dkd * k_ * el_c[i]
                dbc = rsum(dnm[i] * kk[c] * lm[i]) + rsum(dkb * k_ + dvb * v_)
                dq_acc[c] = dq_acc[c] + dq1[i] + dqd[i] * e_c[i]
                dk_acc[c] = dk_acc[c] + dk1[i] + dk2[i] + dk3[i] + dkd * el_c[i] + dkb * bc[i]
                dv_ref[rows[c], hsl[i]] = dvb * bc[i]
                nm = jnp.where(strict, kk[c] * bc[i] * lm[i], 0.0)
                gm = dnm[i] * nm + dpm[i] * pm_ref[hh, c].astype(F32)
                dgc_col = rsum(gm) + rsum((dkbe * kb + dqd[i] * q_) * e_c[i] - del_el)
                dglast = (jnp.sum(jnp.sum(del_el, axis=0, keepdims=True), axis=1, keepdims=True)
                          + dgl_ref[hh, c:c + 1, 0:1] * gl[i])
                dgc_s[hh, c:c + 1, :] = (_row(dgc_col, eye) - jnp.sum(gm, axis=0, keepdims=True)
                                         + jnp.where(lane == C - 1, dglast, 0.0))
                dbeta_ref[hh, c:c + 1, :] = _row(dbc, eye)
            for c in cs:
                dq_ref[rows[c], :] = dq_acc[c]
                dk_ref[rows[c], :] = dk_acc[c]
        for hh in range(2):
            dg_ref[hh] = _dot_hi(dgc_s[hh], lower)

    rows_spec = pl.BlockSpec((2, cpt, C), lambda j, i: (j, i, 0))
    qk_spec = pl.BlockSpec((tt, GDN_DK), lambda j, i: (i, j))
    v_spec = pl.BlockSpec((tt, 2 * GDN_DK), lambda j, i: (i, j))
    cc_spec = pl.BlockSpec((2, cpt, C, C), lambda j, i: (j, i, 0, 0))
    rows_shape = jax.ShapeDtypeStruct((GDN_HV, nC, C), F32)
    return pl.pallas_call(
        body, grid=(GDN_HV // 2, T // tt),
        in_specs=[qk_spec, qk_spec, v_spec, rows_spec, rows_spec, cc_spec, v_spec, v_spec, cc_spec, v_spec,
                  pl.BlockSpec((2, cpt, GDN_DK, GDN_DK), lambda j, i: (j, i, 0, 0)), v_spec, v_spec, v_spec, rows_spec],
        out_specs=[qk_spec, qk_spec, v_spec, rows_spec, rows_spec],
        out_shape=[jax.ShapeDtypeStruct((T, GDN_HV // 2 * GDN_DK), F32),
                   jax.ShapeDtypeStruct((T, GDN_HV // 2 * GDN_DK), F32),
                   jax.ShapeDtypeStruct((T, D_INNER), F32), rows_shape, rows_shape],
        scratch_shapes=[pltpu.VMEM((2, cpt, C), F32)],
        compiler_params=_params("parallel", "parallel"), name=name,
    )(q, k, v, gc, beta, tinv, u, w, pm, vn, sall, do, dvn, dkd, dgl)


def _gdn_gate_bwd(araw, braw, dg, dbeta, alog, dtb, name):
    H, T = araw.shape

    def body(a_ref, b_ref, dg_ref, dbt_ref, alog_ref, dtb_ref, da_ref, db_ref, dalog_ref, ddtb_ref):
        xa = a_ref[...] + dtb_ref[...]
        ea = jnp.exp(alog_ref[...])
        dgv = dg_ref[...]
        da = -dgv * ea * _sigmoid(xa)
        da_ref[...] = da
        dalog_ref[...] = jnp.sum(-dgv * ea * _softplus(xa), axis=1, keepdims=True)
        ddtb_ref[...] = jnp.sum(da, axis=1, keepdims=True)
        bt = _sigmoid(b_ref[...])
        db_ref[...] = dbt_ref[...] * bt * (1.0 - bt)

    return pl.pallas_call(
        body,
        out_shape=[jax.ShapeDtypeStruct((H, T), F32), jax.ShapeDtypeStruct((H, T), F32),
                   jax.ShapeDtypeStruct((H, 1), F32), jax.ShapeDtypeStruct((H, 1), F32)],
        compiler_params=pltpu.CompilerParams(vmem_limit_bytes=VMEM_LIMIT_BYTES), name=name,
    )(araw, braw, dg, dbeta, alog, dtb)


SSD_LOCKSTEP_CHUNKS = 2
SSD_LOCKSTEP_CHUNKS_BWD = 1
SSD_LOCKSTEP_HEADS_BWD = 2


def _ssd_scan_fwd(xs, bm, cm, dtraw, alog, dtb, dskip, name):
    T = xs.shape[0]
    Q = SSD_CHUNK
    tt = min(T, 1024)
    cpt, nC = tt // Q, T // Q
    GW = SSD_R * SSD_P

    def body(alog_ref, dtb_ref, dsk_ref, xs_ref, b_ref, c_ref, dt_ref, y_ref, sall_ref, dto_ref, S, dt_s, acs_s):
        gi, i = pl.program_id(0), pl.program_id(1)

        @pl.when(i == 0)
        def _():
            S[...] = jnp.zeros_like(S)

        tri, _, eye, r_i, c_i = _masks(Q)
        upper = jnp.where(r_i <= c_i, 1.0, 0.0)
        for r in range(SSD_R):
            h = SSD_R * gi + r
            dt = _softplus(dt_ref[r] + dtb_ref[h])
            dto_ref[r] = dt
            dt_s[r] = dt
            acs_s[r] = _dot_hi(-jnp.exp(alog_ref[h]) * dt, upper)

        ps = [slice(r * SSD_P, (r + 1) * SSD_P) for r in range(SSD_R)]
        s_cur = [S[:, ps[r]] for r in range(SSD_R)]
        grp = min(cpt, SSD_LOCKSTEP_CHUNKS)
        for c0 in range(0, cpt, grp):
            cs = list(range(c0, c0 + grp))
            inst = [(c, r) for c in cs for r in range(SSD_R)]
            rows = {c: slice(c * Q, (c + 1) * Q) for c in cs}
            bc_ = {c: b_ref[rows[c], :] for c in cs}
            cc_ = {c: c_ref[rows[c], :] for c in cs}
            cb = {c: _dot_nt(cc_[c], bc_[c]) for c in cs}
            xr = [xs_ref[rows[c], ps[r]] for c, r in inst]
            acr = [acs_s[r, c:c + 1, :] for c, r in inst]
            acc = [_col_bcast(a, Q) for a in acr]
            dtr = [dt_s[r, c:c + 1, :] for c, r in inst]
            mm = [cb[c] * (jnp.exp(jnp.where(tri, acc[i] - acr[i], -1e30)) * dtr[i]) for i, (c, r) in enumerate(inst)]
            bct = {c: bc_[c].T for c in cs}
            st = [_dot(bct[c] * (jnp.exp(acr[i][:, Q - 1:Q] - acr[i]) * dtr[i]), xr[i]) for i, (c, r) in enumerate(inst)]
            yd = [_dot(mm[i], xr[i]) for i in range(len(inst))]
            s_prev = []
            for i, (c, r) in enumerate(inst):
                s_prev.append(s_cur[r])
                s_cur[r] = s_cur[r] * jnp.exp(acr[i][:, Q - 1:Q]) + st[i]
            yo = [_dot(cc_[c] * jnp.exp(acc[i]), s_prev[i]) for i, (c, r) in enumerate(inst)]
            for i, (c, r) in enumerate(inst):
                sall_ref[0, c, :, ps[r]] = s_prev[i]
                y_ref[rows[c], ps[r]] = yd[i] + yo[i] + dsk_ref[SSD_R * gi + r] * xr[i]
        for r in range(SSD_R):
            S[:, ps[r]] = s_cur[r]

    smem = pl.BlockSpec(memory_space=pltpu.SMEM)
    rows_spec = pl.BlockSpec((SSD_R, cpt, Q), lambda g, i: (g, i, 0))
    return pl.pallas_call(
        body, grid=(SSD_G, T // tt),
        in_specs=[smem, smem, smem,
                  pl.BlockSpec((tt, GW), lambda g, i: (i, g)), pl.BlockSpec((tt, SSD_N), lambda g, i: (i, g)),
                  pl.BlockSpec((tt, SSD_N), lambda g, i: (i, g)), rows_spec],
        out_specs=[pl.BlockSpec((tt, GW), lambda g, i: (i, g)),
                   pl.BlockSpec((1, cpt, SSD_N, GW), lambda g, i: (g, i, 0, 0)), rows_spec],
        out_shape=[jax.ShapeDtypeStruct((T, D_INNER), F32), jax.ShapeDtypeStruct((SSD_G, nC, SSD_N, GW), F32),
                   jax.ShapeDtypeStruct((SSD_H, nC, Q), F32)],
        scratch_shapes=[pltpu.VMEM((SSD_N, GW), F32), pltpu.VMEM((SSD_R, cpt, Q), F32),
                        pltpu.VMEM((SSD_R, cpt, Q), F32)],
        compiler_params=_params("parallel", "arbitrary"), name=name,
    )(alog, dtb, dskip, xs, bm, cm, dtraw)


def _ssd_scan_bwd(xs, bm, cm, dt, sall, dy, alog, dskip, name):
    T = xs.shape[0]
    Q = SSD_CHUNK
    tt = min(T, 1024)
    cpt, nC, nT = tt // Q, T // Q, T // tt
    GW = SSD_R * SSD_P

    def body(alog_ref, dsk_ref, xs_ref, b_ref, c_ref, dt_ref, sall_ref, dy_ref,
             dxs_ref, db_ref, dc_ref, da_ref, ddt_ref, dd_ref, dS, acs_s, dacs_s, ddt_s, dd_s):
        gi, i = pl.program_id(0), pl.program_id(1)

        @pl.when(i == 0)
        def _():
            dS[...] = jnp.zeros_like(dS)

        tri, _, eye, r_i, c_i = _masks(Q)
        upper = jnp.where(r_i <= c_i, 1.0, 0.0)
        lower = jnp.where(r_i >= c_i, 1.0, 0.0)
        lane = lax.broadcasted_iota(jnp.int32, (1, Q), 1)
        for r in range(SSD_R):
            acs_s[r] = _dot_hi(-jnp.exp(alog_ref[SSD_R * gi + r]) * dt_ref[r], upper)

        ps = [slice(r * SSD_P, (r + 1) * SSD_P) for r in range(SSD_R)]
        ds_cur = [dS[:, ps[r]] for r in range(SSD_R)]
        grp = min(cpt, SSD_LOCKSTEP_CHUNKS_BWD)
        csum = lambda a: jnp.sum(a, axis=0, keepdims=True)
        tsum = lambda a: jnp.sum(csum(a), axis=1, keepdims=True)
        ones8 = jnp.ones((8, SSD_P), F32)
        for c0 in range(cpt - grp, -1, -grp):
            cs = list(range(c0 + grp - 1, c0 - 1, -1))
            rows = {c: slice(c * Q, (c + 1) * Q) for c in cs}
            bc_ = {c: b_ref[rows[c], :] for c in cs}
            cc_ = {c: c_ref[rows[c], :] for c in cs}
            cb = {c: _dot_nt(cc_[c], bc_[c]) for c in cs}
            cbt = {c: _dot_nt(bc_[c], cc_[c]) for c in cs}
            bct = {c: bc_[c].T for c in cs}
            cct = {c: cc_[c].T for c in cs}
            dcb = {c: jnp.zeros((Q, Q), F32) for c in cs}
            dcbt = {c: jnp.zeros((Q, Q), F32) for c in cs}
            db_acc = {c: jnp.zeros((Q, SSD_N), F32) for c in cs}
            dc_acc = {c: jnp.zeros((Q, SSD_N), F32) for c in cs}
            for h0 in range(0, SSD_R, SSD_LOCKSTEP_HEADS_BWD):
                inst = [(c, r) for c in cs for r in range(h0, h0 + SSD_LOCKSTEP_HEADS_BWD)]
                n = len(inst)
                xr = [xs_ref[rows[c], ps[r]] for c, r in inst]
                dyr = [dy_ref[rows[c], ps[r]] for c, r in inst]
                acr = [acs_s[r, c:c + 1, :] for c, r in inst]
                dtr = [dt_ref[r, c:c + 1, :] for c, r in inst]
                acc = [_col_bcast(a, Q) for a in acr]
                dtb = [_col_bcast(d, Q) for d in dtr]
                al = [a[:, Q - 1:Q] for a in acr]
                e_c = [jnp.exp(a) for a in acc]
                dl_c = [jnp.exp(al[i] - acc[i]) for i in range(n)]
                e_r = [jnp.exp(a) for a in acr]
                dl_r = [jnp.exp(al[i] - acr[i]) for i in range(n)]
                gl = [jnp.exp(a) for a in al]
                lm = [jnp.exp(jnp.where(tri, acc[i] - acr[i], -1e30)) for i in range(n)]
                lmt = [jnp.exp(jnp.where(r_i <= c_i, acr[i] - acc[i], -1e30)) for i in range(n)]
                mmt = [cbt[c] * lmt[i] for i, (c, r) in enumerate(inst)]
                sr = [sall_ref[0, c, :, ps[r]] for c, r in inst]
                dmm0 = [_dot_nt(dyr[i], xr[i]) for i in range(n)]
                dmm0t = [_dot_nt(xr[i], dyr[i]) for i in range(n)]
                dxd1 = [_dot(mmt[i], dyr[i]) for i in range(n)]
                dce = [_dot_nt(dyr[i], sr[i]) for i in range(n)]
                dcet = [_dot_nt(sr[i], dyr[i]) for i in range(n)]
                cdy = [_dot(cct[c] * e_r[i], dyr[i]) for i, (c, r) in enumerate(inst)]
                dsn = []
                for i, (c, r) in enumerate(inst):
                    dsn.append(ds_cur[r])
                    ds_cur[r] = gl[i] * ds_cur[r] + cdy[i]
                dxd = [dxd1[i] + _dot(bc_[c] * dl_c[i], dsn[i]) for i, (c, r) in enumerate(inst)]
                dbd0 = [_dot_nt(xr[i], dsn[i]) for i in range(n)]
                dbd0t = [_dot_nt(dsn[i], xr[i]) for i in range(n)]
                for i, (c, r) in enumerate(inst):
                    dgl = tsum(dsn[i] * sr[i])
                    dc_acc[c] = dc_acc[c] + dce[i] * e_c[i]
                    db_acc[c] = db_acc[c] + dbd0[i] * (dtb[i] * dl_c[i])
                    dl0 = dmm0[i] * lm[i]
                    dl0t = dmm0t[i] * (lmt[i] * dtb[i])
                    dcb[c] = dcb[c] + dl0 * dtr[i]
                    dcbt[c] = dcbt[c] + dl0t
                    csum_gm0 = csum(dl0 * cb[c])
                    rsum_gm = csum(dl0t * cbt[c])
                    r_de = csum(dcet[i] * cct[c]) * e_r[i]
                    r_dl = csum(dbd0t[i] * bct[c]) * dl_r[i]
                    dalast = jnp.sum(r_dl * dtr[i], axis=1, keepdims=True) + dgl * gl[i]
                    dacs_s[r, c:c + 1, :] = (rsum_gm + r_de - (r_dl + csum_gm0) * dtr[i]
                                             + jnp.where(lane == Q - 1, dalast, 0.0))
                    ddt_s[r, c:c + 1, :] = csum_gm0 + r_dl
                    dd_s[r, c:c + 1, :] = _dot_nt(ones8, dyr[i] * xr[i])[0:1]
                    dxs_ref[rows[c], ps[r]] = dxd[i] * dtb[i][:, :SSD_P] + dsk_ref[SSD_R * gi + r] * dyr[i]
            for c in cs:
                dc_ref[rows[c], :] = dc_acc[c] + _dot(dcb[c], bc_[c])
                db_ref[rows[c], :] = db_acc[c] + _dot(dcbt[c], cc_[c])
        for r in range(SSD_R):
            dS[:, ps[r]] = ds_cur[r]
        for r in range(SSD_R):
            da_ref[r] = _dot_hi(dacs_s[r], lower)
            ddt_ref[r] = ddt_s[r]
            dd_ref[r] = dd_s[r]

    rev = lambda i: nT - 1 - i
    smem = pl.BlockSpec(memory_space=pltpu.SMEM)
    rows_spec = pl.BlockSpec((SSD_R, cpt, Q), lambda g, i: (g, rev(i), 0))
    x_spec = pl.BlockSpec((tt, GW), lambda g, i: (rev(i), g))
    n_spec = pl.BlockSpec((tt, SSD_N), lambda g, i: (rev(i), g))
    rows_shape = jax.ShapeDtypeStruct((SSD_H, nC, Q), F32)
    return pl.pallas_call(
        body, grid=(SSD_G, nT),
        in_specs=[smem, smem, x_spec, n_spec, n_spec, rows_spec,
                  pl.BlockSpec((1, cpt, SSD_N, GW), lambda g, i: (g, rev(i), 0, 0)), x_spec],
        out_specs=[x_spec, n_spec, n_spec, rows_spec, rows_spec, rows_spec],
        out_shape=[jax.ShapeDtypeStruct((T, D_INNER), F32), jax.ShapeDtypeStruct((T, SSD_G * SSD_N), F32),
                   jax.ShapeDtypeStruct((T, SSD_G * SSD_N), F32), rows_shape, rows_shape, rows_shape],
        scratch_shapes=[pltpu.VMEM((SSD_N, GW), F32)] + [pltpu.VMEM((SSD_R, cpt, Q), F32)] * 4,
        compiler_params=_params("parallel", "arbitrary"), name=name,
    )(alog, dskip, xs, bm, cm, dt, sall, dy)


def _ssd_gate_bwd(dtraw, dt, da, ddt_direct, ddrow, alog, dtb, name):
    H, T = dtraw.shape

    def body(raw_ref, dt_ref, da_ref, ddt_ref, dd_ref, alog_ref, dtb_ref, draw_ref, dalog_ref, ddtb_ref, dD_ref):
        a = -jnp.exp(alog_ref[...])
        dav = da_ref[...]
        ddt = ddt_ref[...] + dav * a
        draw = ddt * _sigmoid(raw_ref[...] + dtb_ref[...])
        draw_ref[...] = draw
        dalog_ref[...] = jnp.sum(dav * dt_ref[...], axis=1, keepdims=True) * a
        ddtb_ref[...] = jnp.sum(draw, axis=1, keepdims=True)
        dD_ref[...] = jnp.sum(dd_ref[...], axis=1, keepdims=True)

    return pl.pallas_call(
        body,
        out_shape=[jax.ShapeDtypeStruct((H, T), F32)] + [jax.ShapeDtypeStruct((H, 1), F32)] * 3,
        compiler_params=pltpu.CompilerParams(vmem_limit_bytes=VMEM_LIMIT_BYTES), name=name,
    )(dtraw, dt, da, ddt_direct, ddrow, alog, dtb)


def _final_loss(x, fw, tgt, name):
    T = x.shape[0]
    tt = min(T, 512)
    nT = T // tt

    def body(x_ref, w_ref, t_ref, dx_ref, dw_ref, loss_ref, acc):
        i = pl.program_id(0)

        @pl.when(i == 0)
        def _():
            dw_ref[...] = jnp.zeros_like(dw_ref)
            acc[...] = jnp.zeros_like(acc)

        xv = x_ref[...]
        r = lax.rsqrt(jnp.mean(xv * xv, axis=-1, keepdims=True) + EPS)
        xh = xv * r
        err = xh * w_ref[...] - t_ref[...]
        acc[...] += jnp.sum(err * err, axis=0, keepdims=True)
        dout = err * (1.0 / D_MODEL)
        dw_ref[...] += jnp.sum(dout * xh, axis=0, keepdims=True)
        dxn = dout * w_ref[...]
        dx_ref[...] = r * (dxn - xh * jnp.mean(dxn * xh, axis=-1, keepdims=True))

        @pl.when(i == nT - 1)
        def _():
            loss_ref[...] = (0.5 / D_MODEL) * jnp.sum(acc[...], axis=1, keepdims=True)

    row = pl.BlockSpec((tt, D_MODEL), lambda i: (i, 0))
    vec = pl.BlockSpec((1, D_MODEL), lambda i: (0, 0))
    return pl.pallas_call(
        body, grid=(nT,),
        in_specs=[row, vec, row],
        out_specs=[row, vec, pl.BlockSpec((1, 1), lambda i: (0, 0))],
        out_shape=[jax.ShapeDtypeStruct((T, D_MODEL), F32), jax.ShapeDtypeStruct((1, D_MODEL), F32),
                   jax.ShapeDtypeStruct((1, 1), F32)],
        scratch_shapes=[pltpu.VMEM((1, D_MODEL), F32)],
        compiler_params=_params("arbitrary"), name=name,
    )(x, fw, tgt)


def _adamw(parts, w, m, v, name):
    R, C = w.shape
    tr = 128 if R % 128 == 0 else R

    def body(p_ref, w_ref, m_ref, v_ref, g_ref, d_ref, nm_ref, nv_ref):
        g = p_ref[0].astype(F32)
        for s in range(1, N_DEV):
            g = g + p_ref[s].astype(F32)
        mn = ADAM_B1 * m_ref[...] + (1.0 - ADAM_B1) * g
        vn = ADAM_B2 * v_ref[...] + (1.0 - ADAM_B2) * (g * g)
        mh = mn / (1.0 - ADAM_B1 ** ADAM_STEP)
        vh = vn / (1.0 - ADAM_B2 ** ADAM_STEP)
        g_ref[...] = g
        d_ref[...] = -ADAM_LR * (mh / (jnp.sqrt(vh) + ADAM_EPS) + ADAM_WD * w_ref[...])
        nm_ref[...] = mn
        nv_ref[...] = vn

    blk = pl.BlockSpec((tr, C), lambda i: (i, 0))
    return pl.pallas_call(
        body, grid=(R // tr,),
        in_specs=[pl.BlockSpec((N_DEV, tr, C), lambda i: (0, i, 0)), blk, blk, blk],
        out_specs=[blk] * 4,
        out_shape=[jax.ShapeDtypeStruct((R, C), F32)] * 4,
        compiler_params=_params("parallel"), name=name,
    )(parts, w, m, v)


def _me():
    x, y, c = lax.axis_index("x"), lax.axis_index("y"), lax.axis_index("c")
    return x, y, c


def _peer(d):
    x, y, c = _me()
    px = 1 - x if d & 4 else x
    py = 1 - y if d & 2 else y
    pc = 1 - c if d & 1 else c
    return (px, py, pc), 4 * px + 2 * py + pc


def _exchange(arrs, bcast, name):
    n = len(arrs)

    def body(*refs):
        ex = _Exchange(refs[:n], refs[n:2 * n], bcast, *refs[2 * n:])
        ex.begin()
        ex.finish()

    anyspec = pl.BlockSpec(memory_space=pl.ANY)
    return pl.pallas_call(
        body,
        in_specs=[anyspec] * n, out_specs=[anyspec] * n,
        out_shape=_exchange_out_shapes(arrs, bcast),
        scratch_shapes=_exchange_semaphores(n),
        name=name,
    )(*arrs)


def _exchange_out_shapes(arrs, bcast):
    return [jax.ShapeDtypeStruct((N_DEV,) + (a.shape if b else a.shape[1:]), a.dtype) for a, b in zip(arrs, bcast)]


def _exchange_semaphores(n):
    return [pltpu.SemaphoreType.DMA((n, N_DEV - 1)), pltpu.SemaphoreType.DMA((n, N_DEV - 1)),
            pltpu.SemaphoreType.DMA((n,))]


class _Exchange:
    def __init__(self, ins, outs, bcast, ssem, rsem, lsem):
        n = len(ins)
        x, y, c = _me()
        me = 4 * x + 2 * y + c

        def src(a, dest):
            return ins[a] if bcast[a] else ins[a].at[dest]

        self.local = [pltpu.make_async_copy(src(a, me), outs[a].at[me], lsem.at[a]) for a in range(n)]
        self.sends, self.recvs = [], []
        for a in range(n):
            for d in range(1, N_DEV):
                peer, pid = _peer(d)
                self.sends.append(pltpu.make_async_remote_copy(
                    src_ref=src(a, pid), dst_ref=outs[a].at[me], send_sem=ssem.at[a, d - 1],
                    recv_sem=rsem.at[a, d - 1], device_id=peer, device_id_type=MESH))
                self.recvs.append(pltpu.make_async_remote_copy(
                    src_ref=src(a, pid), dst_ref=outs[a].at[pid], send_sem=ssem.at[a, d - 1],
                    recv_sem=rsem.at[a, d - 1], device_id=peer, device_id_type=MESH))

    def begin(self):
        for cp in self.local + self.sends:
            cp.start()

    def finish(self):
        for cp in self.recvs:
            cp.wait_recv()
        for cp in self.sends:
            cp.wait_send()
        for cp in self.local:
            cp.wait()


def _to_rows(cols, chunk):
    T, H = cols.shape
    return cols.T.reshape(H, T // chunk, chunk)


def _from_rows(rows):
    return rows.T


def _pad_cols(a, width):
    return jnp.pad(a, ((0, 0), (0, width - a.shape[1])))


def _local_step(x, tgt, p, late_weights=None, early_grads=None):
    T = x.shape[0]
    zb = lambda n: jnp.zeros((1, n), F32)
    gw = p["gdn_w_in"]
    g_wparts = [gw[:, 0:1024], gw[:, 1024:2048], gw[:, 2048:4096], gw[:, 4096:6144], _pad_cols(gw[:, 6144:6176], PAD_W)]
    nw0, nw1 = p["norm_w"][0:1], p["norm_w"][1:2]
    gcw = p["gdn_conv_w"]
    cw_q, cw_k, cw_v = gcw[:, 0:1024], gcw[:, 1024:2048], gcw[:, 2048:4096]
    g_convs = [(cw_q, zb(1024), True, GDN_DK ** -0.5), (cw_k, zb(1024), True, 1.0), (cw_v, zb(2048), False, 1.0),
               None, None]
    if late_weights is None:
        h0, (q_pre, k_pre, v_pre, z0, ab), (q, k, v), g_cpre = _norm_inproj(x, nw0, g_wparts, g_convs, "gdn_inproj")
    else:
        comm, assemble = late_weights
        h0, (q_pre, k_pre, v_pre, z0, ab), (q, k, v), g_cpre, gathered = _norm_inproj(x, nw0, g_wparts, g_convs,
                                                                                      "gdn_inproj", comm)
        p = dict(p, **assemble(gathered))
    braw = _to_rows(ab[:, 0:GDN_HV], GDN_CHUNK)
    araw = _to_rows(ab[:, GDN_HV:2 * GDN_HV], GDN_CHUNK)
    g_alog, g_dtb = p["gdn_a_log"].reshape(-1), p["gdn_dt_bias"].reshape(-1)
    g_u, g_w, g_pm, g_ti, g_rows, beta_rows, gc_rows, g_qd, g_kd = _gdn_prep(q, k, v, araw, braw, g_alog, g_dtb,
                                                                             "gdn_prep")
    o0, g_vn, g_sall = _gdn_state_fwd(g_qd, g_kd, g_u, g_w, g_pm, gc_rows, "gdn_state_fwd")
    x1 = _out_fwd(o0, z0, p["gdn_norm_w"], p["gdn_w_out"], x, GDN_DK, False, "gdn_out")
    sw = p["ssd_w_in"]
    s_wparts = [sw[:, 0:2048], sw[:, 2048:4096], sw[:, 4096:5120], sw[:, 5120:6144], _pad_cols(sw[:, 6144:6176], PAD_W)]
    scw, scb = p["ssd_conv_w"], p["ssd_conv_b"]
    s_convs = [None, (scw[:, 0:2048], scb[:, 0:2048], False, 1.0), (scw[:, 2048:3072], scb[:, 2048:3072], False, 1.0),
               (scw[:, 3072:4096], scb[:, 3072:4096], False, 1.0), None]
    h1, (z1, xs_pre, b_pre, c_pre, dtp), (xs, bm, cm), s_cpre = _norm_inproj(x1, nw1, s_wparts, s_convs, "ssd_inproj")
    dtraw = _to_rows(dtp[:, 0:SSD_H], SSD_CHUNK)
    s_alog, s_dtb, s_d = p["ssd_a_log"].reshape(-1), p["ssd_dt_bias"].reshape(-1), p["ssd_d"].reshape(-1)
    y1, s_sall, dt_rows = _ssd_scan_fwd(xs, bm, cm, dtraw, s_alog, s_dtb, s_d, "ssd_scan_fwd")
    x2 = _out_fwd(y1, z1, p["ssd_norm_w"], p["ssd_w_out"], x1, D_INNER // SSD_G, True, "ssd_out")
    dx2, d_fw, loss = _final_loss(x2, p["final_norm_w"].reshape(1, -1), tgt, "final_loss")
    dy1, dz1, d_snw, yn1 = _out_bwd(dx2, y1, z1, p["ssd_norm_w"], p["ssd_w_out"], D_INNER // SSD_G, True, "ssd_out_bwd")
    d_swout = _matmul_tn(yn1, dx2, "ssd_wout_grad")
    dxs, dbm, dcm, da_rows, ddt_rows, dd_rows = _ssd_scan_bwd(xs, bm, cm, dt_rows, s_sall, dy1, s_alog, s_d, "ssd_scan_bwd")
    col = lambda a: a.reshape(-1, 1)
    dtraw_g, d_salog, d_sdtb, d_sd = _ssd_gate_bwd(
        dtraw.reshape(SSD_H, T), dt_rows.reshape(SSD_H, T), da_rows.reshape(SSD_H, T),
        ddt_rows.reshape(SSD_H, T), dd_rows.reshape(SSD_H, T), col(s_alog), col(s_dtb), "ssd_gate_bwd")
    dxs_pre, dcw_x, dcb_x = _conv_bwd(xs_pre, s_cpre[0], scw[:, 0:2048], dxs, False, 1.0, "ssd_conv_x_bwd")
    db_pre, dcw_b, dcb_b = _conv_bwd(b_pre, s_cpre[1], scw[:, 2048:3072], dbm, False, 1.0, "ssd_conv_b_bwd")
    dc_pre, dcw_c, dcb_c = _conv_bwd(c_pre, s_cpre[2], scw[:, 3072:4096], dcm, False, 1.0, "ssd_conv_c_bwd")
    ddtp = _pad_cols(_from_rows(dtraw_g), PAD_W)
    s_dparts = [dz1, dxs_pre, db_pre, dc_pre, ddtp]
    dx1, d_nw1 = _inproj_bwd(x1, nw1, s_dparts, s_wparts, dx2, "ssd_inproj_bwd")
    s_dw = [_matmul_tn(h1, d, "ssd_win_grad_%d" % n) for n, d in enumerate(s_dparts)]
    d_swin = jnp.concatenate(s_dw[:4] + [s_dw[4][:, 0:SSD_H]], axis=1)
    early_recv = None
    if early_grads is None:
        do0, dz0, d_gnw, yn0 = _out_bwd(dx1, o0, z0, p["gdn_norm_w"], p["gdn_w_out"], GDN_DK, False, "gdn_out_bwd")
    else:
        do0, dz0, d_gnw, yn0, early_recv = _out_bwd(dx1, o0, z0, p["gdn_norm_w"], p["gdn_w_out"], GDN_DK, False,
                                                    "gdn_out_bwd", early_grads(d_swin, d_swout))
    d_gwout = _matmul_tn(yn0, dx1, "gdn_wout_grad")
    g_dvn, g_dkd, g_dgl = _gdn_state_bwd(g_qd, g_kd, g_w, g_pm, g_vn, g_sall, gc_rows, do0, "gdn_state_bwd")
    dq, dk, dv, dg_rows, dbeta_rows = _gdn_local_bwd(q, k, v, gc_rows, beta_rows, g_ti, g_u, g_w, g_pm, g_vn, g_sall,
                                                     do0, g_dvn, g_dkd, g_dgl, "gdn_local_bwd")
    da_g, db_g, d_galog, d_gdtb = _gdn_gate_bwd(
        araw.reshape(GDN_HV, T), braw.reshape(GDN_HV, T), dg_rows.reshape(GDN_HV, T),
        dbeta_rows.reshape(GDN_HV, T), col(g_alog), col(g_dtb), "gdn_gate_bwd")
    dq_pre, dcw_q, _ = _conv_bwd(q_pre, g_cpre[0], cw_q, dq, True, GDN_DK ** -0.5, "gdn_conv_q_bwd")
    dk_pre, dcw_k, _ = _conv_bwd(k_pre, g_cpre[1], cw_k, dk, True, 1.0, "gdn_conv_k_bwd")
    dv_pre, dcw_v, _ = _conv_bwd(v_pre, g_cpre[2], cw_v, dv, False, 1.0, "gdn_conv_v_bwd")
    dab = _pad_cols(jnp.concatenate([_from_rows(db_g), _from_rows(da_g)], axis=1), PAD_W)
    g_dparts = [dq_pre, dk_pre, dv_pre, dz0, dab]
    dx0, d_nw0 = _inproj_bwd(x, nw0, g_dparts, g_wparts, dx1, "gdn_inproj_bwd")
    g_dw = [_matmul_tn(h0, d, "gdn_win_grad_%d" % n) for n, d in enumerate(g_dparts)]
    d_gwin = jnp.concatenate(g_dw[:4] + [g_dw[4][:, 0:2 * GDN_HV]], axis=1)
    grads = {
        "norm_w": jnp.concatenate([d_nw0, d_nw1], axis=0),
        "gdn_w_in": d_gwin,
        "gdn_conv_w": jnp.concatenate([dcw_q, dcw_k, dcw_v], axis=1),
        "gdn_a_log": d_galog.reshape(1, -1),
        "gdn_dt_bias": d_gdtb.reshape(1, -1),
        "gdn_norm_w": d_gnw,
        "gdn_w_out": d_gwout,
        "ssd_w_in": d_swin,
        "ssd_conv_w": jnp.concatenate([dcw_x, dcw_b, dcw_c], axis=1),
        "ssd_conv_b": jnp.concatenate([dcb_x, dcb_b, dcb_c], axis=1),
        "ssd_dt_bias": d_sdtb.reshape(1, -1),
        "ssd_a_log": d_salog.reshape(1, -1),
        "ssd_d": d_sd.reshape(1, -1),
        "ssd_norm_w": d_snw,
        "ssd_w_out": d_swout,
        "final_norm_w": d_fw,
    }
    if early_grads is not None:
        return loss, dx0, grads, early_recv
    return loss, dx0, grads


WEIGHTS = ["norm_w", "gdn_w_in", "gdn_conv_w", "gdn_a_log", "gdn_dt_bias", "gdn_norm_w", "gdn_w_out", "ssd_w_in",
           "ssd_conv_w", "ssd_conv_b", "ssd_dt_bias", "ssd_a_log", "ssd_d", "ssd_norm_w", "ssd_w_out", "final_norm_w"]
COL_SHARDED = ["gdn_w_in", "ssd_w_in"]
ROW_SHARDED = ["gdn_w_out", "ssd_w_out"]
SMALL_SHARDED = ["gdn_conv_w", "ssd_conv_w", "ssd_conv_b", "ssd_norm_w"]
REPLICATED = ["norm_w", "gdn_a_log", "gdn_dt_bias", "gdn_norm_w", "ssd_dt_bias", "ssd_a_log", "ssd_d", "final_norm_w"]


def _pack(arrs):
    return jnp.concatenate([a.reshape(-1) for a in arrs]).reshape(1, -1)


def _unpack(flat, shapes):
    out, pos = [], 0
    for s in shapes:
        n = 1
        for dim in s:
            n *= dim
        out.append(flat[pos:pos + n].reshape(s))
        pos += n
    return out


def _cols_to_shards(full):
    R, C = full.shape
    return full.reshape(R, N_DEV, C // N_DEV).transpose(1, 0, 2)


def _shards_to_cols(shards):
    n, R, c = shards.shape
    return shards.transpose(1, 0, 2).reshape(R, n * c)


def kernel(x, norm_w, gdn_w_in, gdn_conv_w, gdn_a_log, gdn_dt_bias, gdn_norm_w, gdn_w_out, ssd_w_in, ssd_conv_w, ssd_conv_b, ssd_dt_bias, ssd_a_log, ssd_d, ssd_norm_w, ssd_w_out, final_norm_w, loss_target, m_norm_w, m_gdn_w_in, m_gdn_conv_w, m_gdn_a_log, m_gdn_dt_bias, m_gdn_norm_w, m_gdn_w_out, m_ssd_w_in, m_ssd_conv_w, m_ssd_conv_b, m_ssd_dt_bias, m_ssd_a_log, m_ssd_d, m_ssd_norm_w, m_ssd_w_out, m_final_norm_w, v_norm_w, v_gdn_w_in, v_gdn_conv_w, v_gdn_a_log, v_gdn_dt_bias, v_gdn_norm_w, v_gdn_w_out, v_ssd_w_in, v_ssd_conv_w, v_ssd_conv_b, v_ssd_dt_bias, v_ssd_a_log, v_ssd_d, v_ssd_norm_w, v_ssd_w_out, v_final_norm_w):
    w = dict(norm_w=norm_w, gdn_w_in=gdn_w_in[0], gdn_conv_w=gdn_conv_w[0], gdn_a_log=gdn_a_log,
             gdn_dt_bias=gdn_dt_bias, gdn_norm_w=gdn_norm_w, gdn_w_out=gdn_w_out[0], ssd_w_in=ssd_w_in[0],
             ssd_conv_w=ssd_conv_w[0], ssd_conv_b=ssd_conv_b, ssd_dt_bias=ssd_dt_bias, ssd_a_log=ssd_a_log,
             ssd_d=ssd_d, ssd_norm_w=ssd_norm_w, ssd_w_out=ssd_w_out[0], final_norm_w=final_norm_w.reshape(1, -1))
    m = dict(norm_w=m_norm_w, gdn_w_in=m_gdn_w_in[0], gdn_conv_w=m_gdn_conv_w[0], gdn_a_log=m_gdn_a_log,
             gdn_dt_bias=m_gdn_dt_bias, gdn_norm_w=m_gdn_norm_w, gdn_w_out=m_gdn_w_out[0], ssd_w_in=m_ssd_w_in[0],
             ssd_conv_w=m_ssd_conv_w[0], ssd_conv_b=m_ssd_conv_b, ssd_dt_bias=m_ssd_dt_bias, ssd_a_log=m_ssd_a_log,
             ssd_d=m_ssd_d, ssd_norm_w=m_ssd_norm_w, ssd_w_out=m_ssd_w_out[0], final_norm_w=m_final_norm_w.reshape(1, -1))
    v = dict(norm_w=v_norm_w, gdn_w_in=v_gdn_w_in[0], gdn_conv_w=v_gdn_conv_w[0], gdn_a_log=v_gdn_a_log,
             gdn_dt_bias=v_gdn_dt_bias, gdn_norm_w=v_gdn_norm_w, gdn_w_out=v_gdn_w_out[0], ssd_w_in=v_ssd_w_in[0],
             ssd_conv_w=v_ssd_conv_w[0], ssd_conv_b=v_ssd_conv_b, ssd_dt_bias=v_ssd_dt_bias, ssd_a_log=v_ssd_a_log,
             ssd_d=v_ssd_d, ssd_norm_w=v_ssd_norm_w, ssd_w_out=v_ssd_w_out[0], final_norm_w=v_final_norm_w.reshape(1, -1))
    out_shapes = {n: a.shape for n, a in zip(
        WEIGHTS, [norm_w, gdn_w_in, gdn_conv_w, gdn_a_log, gdn_dt_bias, gdn_norm_w, gdn_w_out, ssd_w_in, ssd_conv_w,
                  ssd_conv_b, ssd_dt_bias, ssd_a_log, ssd_d, ssd_norm_w, ssd_w_out, final_norm_w])}

    small_shapes = [w[n].shape for n in SMALL_SHARDED]
    first = _exchange([_mx(w["gdn_w_in"]), _pack([w[n] for n in SMALL_SHARDED])], [True] * 2, "gather_first")
    full = dict(w)
    full["gdn_w_in"] = _shards_to_cols(first[0])
    small_all = [_unpack(first[1][s, 0], small_shapes) for s in range(N_DEV)]
    for idx, n in enumerate(SMALL_SHARDED):
        full[n] = jnp.concatenate([small_all[s][idx] for s in range(N_DEV)], axis=-1)
    late = ["gdn_w_out", "ssd_w_in", "ssd_w_out"]

    def assemble(gathered):
        return {"gdn_w_out": gathered[0].reshape(-1, D_MODEL), "ssd_w_in": _shards_to_cols(gathered[1]),
                "ssd_w_out": gathered[2].reshape(-1, D_MODEL)}

    def early_grads(d_ssd_w_in, d_ssd_w_out):
        return ([_cols_to_shards(d_ssd_w_in).astype(GRAD_WIRE_DTYPE),
                 d_ssd_w_out.reshape(N_DEV, -1, D_MODEL).astype(GRAD_WIRE_DTYPE)], [False] * 2)

    loss, dx, grads, ssd_recv = _local_step(x[0], loss_target[0], full,
                                            (([_mx(w[n]) for n in late], [True] * 3), assemble), early_grads)

    send_small = jnp.concatenate(
        [_cols_to_shards(grads[n]).reshape(N_DEV, -1) for n in SMALL_SHARDED], axis=1)[:, None, :]
    rep_shapes = [w[n].shape for n in REPLICATED]
    recv = _exchange(
        [_cols_to_shards(grads["gdn_w_in"]).astype(GRAD_WIRE_DTYPE),
         grads["gdn_w_out"].reshape(N_DEV, -1, D_MODEL).astype(GRAD_WIRE_DTYPE),
         send_small, _pack([grads[n] for n in REPLICATED])],
        [False] * 3 + [True], "exchange_grads")

    res = {}
    for n, parts in zip(["gdn_w_in", "gdn_w_out", "ssd_w_in", "ssd_w_out"], [recv[0], recv[1]] + list(ssd_recv)):
        res[n] = _adamw(parts, w[n], m[n], v[n], "adamw_" + n)
    small_res = _adamw(recv[2], *[_pack([t[n] for n in SMALL_SHARDED]) for t in (w, m, v)], "adamw_small")
    rep_res = _adamw(recv[3], *[_pack([t[n] for n in REPLICATED]) for t in (w, m, v)], "adamw_replicated")
    for k4 in range(4):
        for n, a in zip(SMALL_SHARDED, _unpack(small_res[k4][0], small_shapes)):
            res.setdefault(n, [None] * 4)[k4] = a
        for n, a in zip(REPLICATED, _unpack(rep_res[k4][0], rep_shapes)):
            res.setdefault(n, [None] * 4)[k4] = a

    loss = lax.psum(loss[0, 0], ("x", "y", "c"))
    outs = [loss, dx[None]]
    for k4 in range(4):
        outs += [res[n][k4].reshape(out_shapes[n]) for n in WEIGHTS]
    return tuple(outs)
```

```python
import jax
import jax.numpy as jnp
from jax import lax
from jax.experimental import pallas as pl
from jax.experimental.pallas import tpu as pltpu

F32 = jnp.float32
MXU_DTYPE = jnp.bfloat16
GRAD_WIRE_DTYPE = jnp.bfloat16
HI = lax.Precision.HIGHEST
EPS = 1e-6
VMEM_LIMIT_BYTES = 56 * 1024 * 1024
N_DEV = 8
MESH = pl.DeviceIdType.MESH

D_MODEL = 1024
CONV_K = 4
GDN_HV = 16
GDN_DK = 128
GDN_CHUNK = 64
SSD_H = 32
SSD_P = 64
SSD_N = 128
SSD_G = 8
SSD_R = SSD_H // SSD_G
SSD_CHUNK = 128
D_INNER = 2048
PAD_W = 128

ADAM_LR = 0.001
ADAM_B1 = 0.9
ADAM_B2 = 0.999
ADAM_EPS = 1e-08
ADAM_WD = 0.01
ADAM_STEP = 10


def _params(*sem):
    return pltpu.CompilerParams(dimension_semantics=sem, vmem_limit_bytes=VMEM_LIMIT_BYTES)


def _mx(a):
    return a.astype(MXU_DTYPE)


def _dot(a, b):
    return jnp.dot(_mx(a), _mx(b), preferred_element_type=F32)


def _dot_nt(a, b):
    return lax.dot_general(_mx(a), _mx(b), (((1,), (1,)), ((), ())), preferred_element_type=F32)


def _dot_tn(a, b):
    return lax.dot_general(_mx(a), _mx(b), (((0,), (0,)), ((), ())), preferred_element_type=F32)


def _dot_hi(a, b):
    return jnp.dot(a, b, precision=HI, preferred_element_type=F32)


def _sigmoid(x):
    return 0.5 * jnp.tanh(0.5 * x) + 0.5


def _silu(x):
    return x * _sigmoid(x)


def _dsilu(x):
    s = _sigmoid(x)
    return s * (1.0 + x * (1.0 - s))


def _softplus(x):
    return jnp.maximum(x, 0.0) + jnp.log1p(jnp.exp(-jnp.abs(x)))


def _col(r, eye):
    return jnp.sum(jnp.where(eye, r, 0.0), axis=1, keepdims=True)


def _row(c, eye):
    return jnp.sum(jnp.where(eye, c, 0.0), axis=0, keepdims=True)


def _col_bcast(r, n):
    return jnp.broadcast_to(r, (n, n)).T


def _masks(n):
    r = lax.broadcasted_iota(jnp.int32, (n, n), 0)
    c = lax.broadcasted_iota(jnp.int32, (n, n), 1)
    return r >= c, r > c, r == c, r, c


def _with_exchange(comm):
    arrs, bcast = comm if comm else ([], [])
    nc = len(arrs)
    anyspec = pl.BlockSpec(memory_space=pl.ANY)

    def wrap(compute, n_in, n_out):
        def body(*refs):
            cin, cout = refs[n_in:n_in + nc], refs[n_in + nc + n_out:n_in + 2 * nc + n_out]
            sems = refs[n_in + 2 * nc + n_out:n_in + 2 * nc + n_out + 3]
            rest = refs[:n_in] + refs[n_in + nc:n_in + nc + n_out] + refs[n_in + 2 * nc + n_out + (3 if nc else 0):]
            if nc:
                @pl.when(pl.program_id(0) == 0)
                def _():
                    _Exchange(cin, cout, bcast, *sems).begin()
            compute(*rest)
            if nc:
                @pl.when(pl.program_id(0) == pl.num_programs(0) - 1)
                def _():
                    _Exchange(cin, cout, bcast, *sems).finish()
        return body

    return dict(arrs=list(arrs), nc=nc, wrap=wrap, in_specs=[anyspec] * nc, out_specs=[anyspec] * nc,
                out_shape=_exchange_out_shapes(arrs, bcast), scratch=_exchange_semaphores(nc) if nc else [])


INPROJ_COL_BLOCK = 512


def _norm_inproj(x, nw, wparts, convs, name, comm=None):
    T = x.shape[0]
    tt = min(T, 256)
    n = len(wparts)
    ck = [k for k in range(n) if convs[k] is not None]
    nconv = len(ck)
    ex = _with_exchange(comm)

    def compute(x_ref, nw_ref, *refs):
        w_refs, cw_refs = refs[:n], refs[n:n + 2 * nconv]
        h_ref, o_refs = refs[n + 2 * nconv], refs[n + 2 * nconv + 1:2 * n + 2 * nconv + 1]
        post_refs = refs[2 * n + 2 * nconv + 1:2 * n + 3 * nconv + 1]
        cpre_refs = refs[2 * n + 3 * nconv + 1:2 * n + 4 * nconv + 1]
        p_refs = refs[2 * n + 4 * nconv + 1:]
        xv = x_ref[...]
        r = lax.rsqrt(jnp.mean(xv * xv, axis=-1, keepdims=True) + EPS)
        h = _mx(xv * r * nw_ref[...])
        h_ref[...] = h
        for m in range(nconv):
            @pl.when(pl.program_id(0) == 0)
            def _():
                p_refs[m][0:HALO, :] = jnp.zeros((HALO, p_refs[m].shape[1]), F32)

        def conv_block(k, c0, cw):
            m = ck.index(k)
            _, _, l2, scale = convs[k]
            cw_ref, cb_ref, out_ref, P = cw_refs[2 * m], cw_refs[2 * m + 1], post_refs[m], p_refs[m]
            cs = slice(c0, c0 + cw)
            acc = cb_ref[:, cs] + cw_ref[0:1, cs] * P[pl.ds(HALO - 3, tt), cs]
            for j in range(1, CONV_K):
                acc = acc + cw_ref[j:j + 1, cs] * P[pl.ds(HALO - 3 + j, tt), cs]
            cpre_refs[m][:, cs] = acc
            s = _silu(acc)
            if l2:
                sls = [slice(g0, g0 + GDN_DK) for g0 in range(0, cw, GDN_DK)]
                rr = [lax.rsqrt(jnp.sum(s[:, sl] * s[:, sl], axis=-1, keepdims=True) + EPS) for sl in sls]
                for sl, rg in zip(sls, rr):
                    out_ref[:, c0 + sl.start:c0 + sl.stop] = s[:, sl] * rg * scale
            else:
                out_ref[:, cs] = s
            P[0:HALO, cs] = P[tt:tt + HALO, cs]

        pending = None
        for k in range(n):
            for c0 in range(0, widths[k], INPROJ_COL_BLOCK):
                cw = min(INPROJ_COL_BLOCK, widths[k] - c0)
                pre = jnp.dot(h, w_refs[k][:, c0:c0 + cw], preferred_element_type=F32)
                o_refs[k][:, c0:c0 + cw] = pre
                if convs[k] is not None:
                    p_refs[ck.index(k)][HALO:HALO + tt, c0:c0 + cw] = pre
                if pending is not None:
                    conv_block(*pending)
                pending = (k, c0, cw) if convs[k] is not None else None
        if pending is not None:
            conv_block(*pending)

    row = lambda width: pl.BlockSpec((tt, width), lambda i: (i, 0))
    full = lambda a: pl.BlockSpec(a.shape, lambda i: (0, 0))
    once = lambda a: pl.BlockSpec(a.shape, lambda i: (0, 0), pipeline_mode=pl.Buffered(1))
    conv_args = [a for k in ck for a in convs[k][:2]]
    widths = [w.shape[1] for w in wparts]
    outs = pl.pallas_call(
        ex["wrap"](compute, 2 + n + 2 * nconv, 1 + n + 2 * nconv), grid=(T // tt,),
        in_specs=[row(D_MODEL), full(nw)] + [once(w) for w in wparts] + [full(a) for a in conv_args] + ex["in_specs"],
        out_specs=[row(D_MODEL)] + [row(wd) for wd in widths] + [row(widths[k]) for k in ck + ck] + ex["out_specs"],
        out_shape=[jax.ShapeDtypeStruct((T, D_MODEL), MXU_DTYPE)]
        + [jax.ShapeDtypeStruct((T, wd), F32) for wd in widths]
        + [jax.ShapeDtypeStruct((T, widths[k]), F32) for k in ck + ck] + ex["out_shape"],
        scratch_shapes=ex["scratch"] + [pltpu.VMEM((HALO + tt, widths[k]), F32) for k in ck],
        compiler_params=_params("arbitrary"), name=name,
    )(x, nw, *wparts, *conv_args, *ex["arrs"])
    outs = list(outs)
    res = (outs[0], outs[1:1 + n], outs[1 + n:1 + n + nconv], outs[1 + n + nconv:1 + n + 2 * nconv])
    return res + (outs[1 + n + 2 * nconv:],) if comm else res


def _inproj_bwd(x, nw, dparts, wparts, dres, name):
    T = x.shape[0]
    tt = min(T, 512)
    n = len(wparts)

    def body(x_ref, nw_ref, dres_ref, *refs):
        d_refs, w_refs, dx_ref, dnw_ref = refs[:n], refs[n:2 * n], refs[2 * n], refs[2 * n + 1]

        @pl.when(pl.program_id(0) == 0)
        def _():
            dnw_ref[...] = jnp.zeros_like(dnw_ref)

        dh = _dot_nt(d_refs[0][...], w_refs[0][...])
        for d_ref, w_ref in zip(d_refs[1:], w_refs[1:]):
            dh = dh + _dot_nt(d_ref[...], w_ref[...])
        xv = x_ref[...]
        r = lax.rsqrt(jnp.mean(xv * xv, axis=-1, keepdims=True) + EPS)
        xh = xv * r
        dnw_ref[...] += jnp.sum(dh * xh, axis=0, keepdims=True)
        dxn = dh * nw_ref[...]
        dx_ref[...] = dres_ref[...] + r * (dxn - xh * jnp.mean(dxn * xh, axis=-1, keepdims=True))

    row = lambda width: pl.BlockSpec((tt, width), lambda i: (i, 0))
    full = lambda a: pl.BlockSpec(a.shape, lambda i: (0, 0))
    return pl.pallas_call(
        body, grid=(T // tt,),
        in_specs=[row(D_MODEL), full(nw), row(D_MODEL)] + [row(d.shape[1]) for d in dparts]
        + [pl.BlockSpec(w.shape, lambda i: (0, 0), pipeline_mode=pl.Buffered(1)) for w in wparts],
        out_specs=[row(D_MODEL), pl.BlockSpec((1, D_MODEL), lambda i: (0, 0))],
        out_shape=[jax.ShapeDtypeStruct((T, D_MODEL), F32), jax.ShapeDtypeStruct((1, D_MODEL), F32)],
        compiler_params=_params("arbitrary"), name=name,
    )(x, nw, dres, *dparts, *wparts)


def _matmul_tn(a, b, name):
    T, K = a.shape
    N = b.shape[1]
    tt = min(T, 2048)
    tn = min(N, 1024)

    def body(a_ref, b_ref, o_ref):
        @pl.when(pl.program_id(1) == 0)
        def _():
            o_ref[...] = jnp.zeros_like(o_ref)

        o_ref[...] += _dot_tn(a_ref[...], b_ref[...])

    return pl.pallas_call(
        body, grid=(N // tn, T // tt),
        in_specs=[pl.BlockSpec((tt, K), lambda n, t: (t, 0)), pl.BlockSpec((tt, tn), lambda n, t: (t, n))],
        out_specs=pl.BlockSpec((K, tn), lambda n, t: (0, n)),
        out_shape=jax.ShapeDtypeStruct((K, N), F32),
        compiler_params=_params("parallel", "arbitrary"), name=name,
    )(a, b)


OUT_COL_BLOCK = 512


def _out_fwd(o, z, w, wout, xres, gs, gate_first, name):
    T = o.shape[0]
    tt = min(T, 256)
    wide = w.shape[1] == D_INNER

    def body(o_ref, z_ref, w_ref, wout_ref, x_ref, out_ref, yn):
        acc = x_ref[...]
        pending = None
        for b0 in range(0, D_INNER, OUT_COL_BLOCK):
            for g0 in range(b0, b0 + OUT_COL_BLOCK, gs):
                sl = slice(g0, g0 + gs)
                og, zg = o_ref[:, sl], z_ref[:, sl]
                wg = w_ref[:, sl] if wide else w_ref[...]
                if gate_first:
                    u = og * _silu(zg)
                    r = lax.rsqrt(jnp.mean(u * u, axis=-1, keepdims=True) + EPS)
                    yn[:, sl] = _mx(u * r * wg)
                else:
                    r = lax.rsqrt(jnp.mean(og * og, axis=-1, keepdims=True) + EPS)
                    yn[:, sl] = _mx(og * r * wg * _silu(zg))
            if pending is not None:
                acc = acc + jnp.dot(yn[:, pending], wout_ref[pending, :], preferred_element_type=F32)
            pending = slice(b0, b0 + OUT_COL_BLOCK)
        out_ref[...] = acc + jnp.dot(yn[:, pending], wout_ref[pending, :], preferred_element_type=F32)

    row = lambda width: pl.BlockSpec((tt, width), lambda i: (i, 0))
    full = lambda a: pl.BlockSpec(a.shape, lambda i: (0, 0))
    return pl.pallas_call(
        body, grid=(T // tt,),
        in_specs=[row(D_INNER), row(D_INNER), full(w), full(wout), row(D_MODEL)],
        out_specs=row(D_MODEL),
        out_shape=jax.ShapeDtypeStruct((T, D_MODEL), F32),
        scratch_shapes=[pltpu.VMEM((tt, D_INNER), MXU_DTYPE)],
        compiler_params=_params("parallel"), name=name,
    )(o, z, w, wout, xres)


def _out_bwd(dx, o, z, w, wout, gs, gate_first, name, comm=None):
    T = o.shape[0]
    tt = min(T, 256)
    wide = w.shape[1] == D_INNER

    def body(dx_ref, o_ref, z_ref, w_ref, wout_ref, do_ref, dz_ref, dw_ref, yn_ref):
        @pl.when(pl.program_id(0) == 0)
        def _():
            dw_ref[...] = jnp.zeros_like(dw_ref)

        dxb = _mx(dx_ref[...])
        blocks = list(range(0, D_INNER, OUT_COL_BLOCK))
        dyn_b = {b0: _dot_nt(dxb, wout_ref[b0:b0 + OUT_COL_BLOCK, :]) for b0 in blocks[:1]}
        dw_acc = jnp.zeros((1, gs), F32)
        for g0 in range(0, D_INNER, gs):
            b0 = g0 - g0 % OUT_COL_BLOCK
            if g0 == b0 and b0 + OUT_COL_BLOCK < D_INNER:
                nb = b0 + OUT_COL_BLOCK
                dyn_b[nb] = _dot_nt(dxb, wout_ref[nb:nb + OUT_COL_BLOCK, :])
            sl = slice(g0, g0 + gs)
            og, zg, dg = o_ref[:, sl], z_ref[:, sl], dyn_b[b0][:, g0 - b0:g0 - b0 + gs]
            wg = w_ref[:, sl] if wide else w_ref[...]
            sz = _silu(zg)
            if gate_first:
                u = og * sz
                r = lax.rsqrt(jnp.mean(u * u, axis=-1, keepdims=True) + EPS)
                uh = u * r
                yn_ref[:, sl] = _mx(uh * wg)
                dw_g = jnp.sum(dg * uh, axis=0, keepdims=True)
                duh = dg * wg
                du = r * (duh - uh * jnp.mean(duh * uh, axis=-1, keepdims=True))
                do_ref[:, sl] = du * sz
                dz_ref[:, sl] = _mx(du * og * _dsilu(zg))
            else:
                r = lax.rsqrt(jnp.mean(og * og, axis=-1, keepdims=True) + EPS)
                oh = og * r
                yn_ref[:, sl] = _mx(oh * wg * sz)
                dw_g = jnp.sum(dg * oh * sz, axis=0, keepdims=True)
                doh = dg * wg * sz
                dz_ref[:, sl] = _mx(dg * oh * wg * _dsilu(zg))
                do_ref[:, sl] = r * (doh - oh * jnp.mean(doh * oh, axis=-1, keepdims=True))
            if wide:
                dw_ref[:, sl] += dw_g
            else:
                dw_acc = dw_acc + dw_g
        if not wide:
            dw_ref[...] += dw_acc

    row = lambda width: pl.BlockSpec((tt, width), lambda i: (i, 0))
    full = lambda a: pl.BlockSpec(a.shape, lambda i: (0, 0))
    ex = _with_exchange(comm)
    outs = pl.pallas_call(
        ex["wrap"](body, 5, 4), grid=(T // tt,),
        in_specs=[row(D_MODEL), row(D_INNER), row(D_INNER), full(w), full(wout)] + ex["in_specs"],
        out_specs=[row(D_INNER), row(D_INNER), full(w), row(D_INNER)] + ex["out_specs"],
        out_shape=[jax.ShapeDtypeStruct((T, D_INNER), F32), jax.ShapeDtypeStruct((T, D_INNER), MXU_DTYPE),
                   jax.ShapeDtypeStruct(w.shape, F32), jax.ShapeDtypeStruct((T, D_INNER), MXU_DTYPE)]
        + ex["out_shape"],
        scratch_shapes=ex["scratch"],
        compiler_params=_params("arbitrary"), name=name,
    )(dx, o, z, w, wout, *ex["arrs"])
    outs = list(outs)
    return outs[:4] + ([outs[4:]] if comm else [])


HALO = 8
CONV_STRIP = 32


def _conv_bwd(pre, cpre_all, w, dpost, l2, scale, name):
    T, C = pre.shape
    tt = min(T, 512)
    tc = min(C, 1024 if l2 else 512)
    strip = 2 * CONV_STRIP if l2 else CONV_STRIP
    nT = T // tt
    ext = tt + HALO

    def body(pre_ref, cp_ref, cn_ref, dpost_ref, dn_ref, w_ref, dpre_ref, dw_ref, db_ref, Q):
        i = pl.program_id(1)

        @pl.when(i == 0)
        def _():
            dw_ref[...] = jnp.zeros_like(dw_ref)
            db_ref[...] = jnp.zeros_like(db_ref)

        wj = [w_ref[j:j + 1, :] for j in range(CONV_K)]
        keep_next = jnp.where(i < nT - 1, 1.0, 0.0)
        fold = lambda a: jnp.sum(a.reshape(strip // 8, 8, tc), axis=0)
        dw_acc = [jnp.zeros((8, tc), F32) for _ in range(CONV_K)]
        db_acc = jnp.zeros((8, tc), F32)
        for r0 in list(range(0, tt, strip)) + [tt]:
            n = strip if r0 < tt else HALO
            cpre = cp_ref[r0:r0 + n, :] if r0 < tt else cn_ref[...]
            dy = dpost_ref[r0:r0 + n, :] if r0 < tt else dn_ref[...] * keep_next
            sg = _sigmoid(cpre)
            ds_c = sg * (1.0 + cpre * (1.0 - sg))
            if l2:
                s = cpre * sg
                sls = [slice(g0, g0 + GDN_DK) for g0 in range(0, tc, GDN_DK)]
                rr = [lax.rsqrt(jnp.sum(s[:, sl] * s[:, sl], axis=-1, keepdims=True) + EPS) for sl in sls]
                yh = [s[:, sl] * r for sl, r in zip(sls, rr)]
                pr = [jnp.sum(dy[:, sl] * y, axis=-1, keepdims=True) for sl, y in zip(sls, yh)]
                for sl, r, y, p in zip(sls, rr, yh, pr):
                    Q[r0:r0 + n, sl] = (scale * r) * (dy[:, sl] - y * p) * ds_c[:, sl]
                dyc = Q[r0:r0 + n, :]
            else:
                dyc = dy * ds_c
                Q[r0:r0 + n, :] = dyc
            if r0 < tt:
                db_acc = db_acc + fold(dyc)
        for r0 in range(0, tt, strip):
            xs = pre_ref[r0:r0 + strip, :]
            dpre = jnp.zeros((strip, tc), F32)
            for j in range(CONV_K):
                qj = Q[pl.ds(3 - j + r0, strip), :]
                dpre = dpre + wj[j] * qj
                dw_acc[j] = dw_acc[j] + fold(qj * xs)
            dpre_ref[r0:r0 + strip, :] = _mx(dpre)
        for j in range(CONV_K):
            dw_ref[j:j + 1, :] += jnp.sum(dw_acc[j], axis=0, keepdims=True)
        db_ref[...] += jnp.sum(db_acc, axis=0, keepdims=True)

    tile = pl.BlockSpec((tt, tc), lambda j, i: (i, j))
    nxt = pl.BlockSpec((HALO, tc), lambda j, i: (jnp.minimum((i + 1) * (tt // HALO), T // HALO - 1), j))
    return pl.pallas_call(
        body, grid=(C // tc, nT),
        in_specs=[tile, tile, nxt, tile, nxt, pl.BlockSpec((CONV_K, tc), lambda j, i: (0, j))],
        out_specs=[tile, pl.BlockSpec((CONV_K, tc), lambda j, i: (0, j)), pl.BlockSpec((1, tc), lambda j, i: (0, j))],
        out_shape=[jax.ShapeDtypeStruct((T, C), MXU_DTYPE), jax.ShapeDtypeStruct((CONV_K, C), F32),
                   jax.ShapeDtypeStruct((1, C), F32)],
        scratch_shapes=[pltpu.VMEM((ext, tc), F32)],
        compiler_params=_params("parallel", "arbitrary"), name=name,
    )(pre, cpre_all, cpre_all, dpost, dpost, w)


GDN_LOCKSTEP_CHUNKS = 16
GDN_SCAN_HEADS = 16


def _inv_unit_lower_many(nms, eye, n):
    xs = [jnp.where(eye, 1.0, 0.0) - nm for nm in nms]
    ps = list(nms)
    k = 2
    while k < n:
        ps = [_dot(p, p) for p in ps]
        xs = [x + _dot(x, p) for x, p in zip(xs, ps)]
        k *= 2
    return xs


def _gdn_prep(q, k, v, araw, braw, alog, dtb, name):
    T = q.shape[0]
    C = GDN_CHUNK
    tt = min(T, 1024)
    cpt, nC = tt // C, T // C
    grp = min(cpt, GDN_LOCKSTEP_CHUNKS)

    def body(alog_ref, dtb_ref, q_ref, k_ref, v_ref, a_ref, b_ref,
             u_ref, w_ref, pm_ref, ti_ref, g_ref, beta_ref, gc_ref, qd_ref, kd_ref):
        j = pl.program_id(0)
        tri, strict, eye, r_i, c_i = _masks(C)
        upper = jnp.where(r_i <= c_i, 1.0, 0.0)
        gcs, bts = [], []
        for hh in range(2):
            h = 2 * j + hh
            g = -jnp.exp(alog_ref[h]) * _softplus(a_ref[hh] + dtb_ref[h])
            bt = _sigmoid(b_ref[hh])
            gc = _dot_hi(g, upper)
            g_ref[hh], beta_ref[hh], gc_ref[hh] = g, bt, gc
            gcs.append(gc)
            bts.append(bt)
        for c0 in range(0, cpt, grp):
            cs = list(range(c0, c0 + grp))
            inst = [(c, hh) for c in cs for hh in range(2)]
            rows = {c: slice(c * C, (c + 1) * C) for c in cs}
            qc = {c: q_ref[rows[c], :] for c in cs}
            kc = {c: k_ref[rows[c], :] for c in cs}
            kk = {c: _dot_nt(kc[c], kc[c]) for c in cs}
            qk = {c: _dot_nt(qc[c], kc[c]) for c in cs}
            gcr = [gcs[hh][c:c + 1, :] for c, hh in inst]
            gcc = [_col(r, eye) for r in gcr]
            bc = [_col(bts[hh][c:c + 1, :], eye) for c, hh in inst]
            lm = [jnp.exp(jnp.where(tri, cc - r, -1e30)) for cc, r in zip(gcc, gcr)]
            nm = [jnp.where(strict, kk[c] * b * l, 0.0) for (c, hh), b, l in zip(inst, bc, lm)]
            tinv = _inv_unit_lower_many(nm, eye, C)
            e_c = [jnp.exp(cc) for cc in gcc]
            rhs = [jnp.concatenate([v_ref[rows[c], hh * GDN_DK:(hh + 1) * GDN_DK] * b, kc[c] * (b * e)], axis=1)
                   for (c, hh), b, e in zip(inst, bc, e_c)]
            sol = [_dot(t, r) for t, r in zip(tinv, rhs)]
            for (c, hh), s, t, l, e, cc, r in zip(inst, sol, tinv, lm, e_c, gcc, gcr):
                hs = slice(hh * GDN_DK, (hh + 1) * GDN_DK)
                u_ref[rows[c], hs] = s[:, :GDN_DK]
                w_ref[rows[c], hs] = _mx(s[:, GDN_DK:])
                pm_ref[hh, c] = _mx(jnp.where(tri, qk[c] * l, 0.0))
                ti_ref[hh, c] = _mx(t)
                qd_ref[rows[c], hs] = _mx(qc[c] * e)
                kd_ref[rows[c], hs] = _mx(kc[c] * jnp.exp(r[:, C - 1:C] - cc))

    smem = pl.BlockSpec(memory_space=pltpu.SMEM)
    rows_spec = pl.BlockSpec((2, cpt, C), lambda j, i: (j, i, 0))
    qk_spec = pl.BlockSpec((tt, GDN_DK), lambda j, i: (i, j))
    v_spec = pl.BlockSpec((tt, 2 * GDN_DK), lambda j, i: (i, j))
    cc_spec = pl.BlockSpec((2, cpt, C, C), lambda j, i: (j, i, 0, 0))
    rows_shape = jax.ShapeDtypeStruct((GDN_HV, nC, C), F32)
    cc_shape = jax.ShapeDtypeStruct((GDN_HV, nC, C, C), MXU_DTYPE)
    return pl.pallas_call(
        body, grid=(GDN_HV // 2, T // tt),
        in_specs=[smem, smem, qk_spec, qk_spec, v_spec, rows_spec, rows_spec],
        out_specs=[v_spec, v_spec, cc_spec, cc_spec, rows_spec, rows_spec, rows_spec, v_spec, v_spec],
        out_shape=[jax.ShapeDtypeStruct((T, D_INNER), F32), jax.ShapeDtypeStruct((T, D_INNER), MXU_DTYPE),
                   cc_shape, cc_shape, rows_shape, rows_shape, rows_shape,
                   jax.ShapeDtypeStruct((T, D_INNER), MXU_DTYPE), jax.ShapeDtypeStruct((T, D_INNER), MXU_DTYPE)],
        compiler_params=_params("parallel", "parallel"), name=name,
    )(alog, dtb, q, k, v, araw, braw)


def _gdn_state_fwd(q, k, u, w, pm, gc, name):
    T = q.shape[0]
    C = GDN_CHUNK
    HG = GDN_SCAN_HEADS
    tt = min(T, 512)
    cpt, nC = tt // C, T // C

    def body(q_ref, k_ref, u_ref, w_ref, pm_ref, gc_ref, o_ref, vn_ref, sall_ref, S):
        @pl.when(pl.program_id(1) == 0)
        def _():
            S[...] = jnp.zeros_like(S)

        heads = list(range(HG))

        def chunk(c, carry):
            rows = pl.ds(pl.multiple_of(c * C, C), C)
            hs = [slice(h * GDN_DK, (h + 1) * GDN_DK) for h in heads]
            gl = [jnp.exp(gc_ref[h, pl.ds(c, 1), C - 1:C]) for h in heads]
            sv = [S[h] for h in heads]
            for h in heads:
                sall_ref[h, c] = _mx(sv[h])
            ws = [_dot(w_ref[rows, hs[h]], sv[h]) for h in heads]
            qsv = [_dot(q_ref[rows, hs[h]], sv[h]) for h in heads]
            vn = [u_ref[rows, hs[h]] - ws[h] for h in heads]
            pv = [_dot(pm_ref[h, c], vn[h]) for h in heads]
            kv = [_dot_tn(k_ref[rows, hs[h]], vn[h]) for h in heads]
            for h in heads:
                vn_ref[rows, hs[h]] = _mx(vn[h])
                o_ref[rows, hs[h]] = qsv[h] + pv[h]
                S[h] = sv[h] * gl[h] + kv[h]
            return carry

        lax.fori_loop(0, cpt, chunk, 0)

    v_spec = pl.BlockSpec((tt, HG * GDN_DK), lambda g, i: (i, g))
    return pl.pallas_call(
        body, grid=(GDN_HV // HG, T // tt),
        in_specs=[v_spec, v_spec, v_spec, v_spec,
                  pl.BlockSpec((HG, cpt, C, C), lambda g, i: (g, i, 0, 0)),
                  pl.BlockSpec((HG, cpt, C), lambda g, i: (g, i, 0))],
        out_specs=[v_spec, v_spec, pl.BlockSpec((HG, cpt, GDN_DK, GDN_DK), lambda g, i: (g, i, 0, 0))],
        out_shape=[jax.ShapeDtypeStruct((T, D_INNER), F32), jax.ShapeDtypeStruct((T, D_INNER), MXU_DTYPE),
                   jax.ShapeDtypeStruct((GDN_HV, nC, GDN_DK, GDN_DK), MXU_DTYPE)],
        scratch_shapes=[pltpu.VMEM((HG, GDN_DK, GDN_DK), F32)],
        compiler_params=_params("parallel", "arbitrary"), name=name,
    )(q, k, u, w, pm, gc)


def _gdn_state_bwd(q, k, w, pm, vn, sall, gc, do, name):
    T = q.shape[0]
    C = GDN_CHUNK
    HG = GDN_SCAN_HEADS
    tt = min(T, 512)
    cpt, nC, nT = tt // C, T // C, T // tt

    def body(q_ref, k_ref, w_ref, pm_ref, vn_ref, sall_ref, gc_ref, do_ref, dvn_ref, dkd_ref, dgl_ref, dS):
        @pl.when(pl.program_id(1) == 0)
        def _():
            dS[...] = jnp.zeros_like(dS)

        heads = list(range(HG))

        def chunk(ci, carry):
            c = cpt - 1 - ci
            rows = pl.ds(pl.multiple_of(c * C, C), C)
            hs = [slice(h * GDN_DK, (h + 1) * GDN_DK) for h in heads]
            gl = [jnp.exp(gc_ref[h, pl.ds(c, 1), C - 1:C]) for h in heads]
            dsn = [dS[h] for h in heads]
            doc = [do_ref[rows, hs[h]] for h in heads]
            kds = [_dot(k_ref[rows, hs[h]], dsn[h]) for h in heads]
            pdo = [_dot_tn(pm_ref[h, c], doc[h]) for h in heads]
            dkd = [_dot_nt(vn_ref[rows, hs[h]], dsn[h]) for h in heads]
            qdo = [_dot_tn(q_ref[rows, hs[h]], doc[h]) for h in heads]
            dvn = [pdo[h] + kds[h] for h in heads]
            wdv = [_dot_tn(w_ref[rows, hs[h]], dvn[h]) for h in heads]
            for h in heads:
                dgl = jnp.sum(jnp.sum(dsn[h] * sall_ref[h, c].astype(F32), axis=0, keepdims=True), axis=1, keepdims=True)
                dgl_ref[h, pl.ds(c, 1), :] = jnp.broadcast_to(dgl, (1, C))
                dvn_ref[rows, hs[h]] = dvn[h]
                dkd_ref[rows, hs[h]] = dkd[h]
                dS[h] = dsn[h] * gl[h] + qdo[h] - wdv[h]
            return carry

        lax.fori_loop(0, cpt, chunk, 0)

    rev = lambda i: nT - 1 - i
    v_spec = pl.BlockSpec((tt, HG * GDN_DK), lambda g, i: (rev(i), g))
    rows_spec = pl.BlockSpec((HG, cpt, C), lambda g, i: (g, rev(i), 0))
    return pl.pallas_call(
        body, grid=(GDN_HV // HG, nT),
        in_specs=[v_spec, v_spec, v_spec, pl.BlockSpec((HG, cpt, C, C), lambda g, i: (g, rev(i), 0, 0)), v_spec,
                  pl.BlockSpec((HG, cpt, GDN_DK, GDN_DK), lambda g, i: (g, rev(i), 0, 0)), rows_spec, v_spec],
        out_specs=[v_spec, v_spec, rows_spec],
        out_shape=[jax.ShapeDtypeStruct((T, D_INNER), F32), jax.ShapeDtypeStruct((T, D_INNER), F32),
                   jax.ShapeDtypeStruct((GDN_HV, nC, C), F32)],
        scratch_shapes=[pltpu.VMEM((HG, GDN_DK, GDN_DK), F32)],
        compiler_params=_params("parallel", "arbitrary"), name=name,
    )(q, k, w, pm, vn, sall, gc, do)


def _gdn_local_bwd(q, k, v, gc, beta, tinv, u, w, pm, vn, sall, do, dvn, dkd, dgl, name):
    T = q.shape[0]
    C = GDN_CHUNK
    tt = min(T, 1024)
    cpt, nC = tt // C, T // C
    grp = min(cpt, GDN_LOCKSTEP_CHUNKS)

    def body(q_ref, k_ref, v_ref, gc_ref, b_ref, ti_ref, u_ref, w_ref, pm_ref, vn_ref, sall_ref, do_ref,
             dvn_ref, dkd_ref, dgl_ref, dq_ref, dk_ref, dv_ref, dg_ref, dbeta_ref, dgc_s):
        tri, strict, eye, r_i, c_i = _masks(C)
        lower = jnp.where(r_i >= c_i, 1.0, 0.0)
        lane = lax.broadcasted_iota(jnp.int32, (1, C), 1)
        rsum = lambda a: jnp.sum(a, axis=1, keepdims=True)
        for c0 in range(0, cpt, grp):
            cs = list(range(c0, c0 + grp))
            inst = [(c, hh) for c in cs for hh in range(2)]
            n = len(inst)
            rows = {c: slice(c * C, (c + 1) * C) for c in cs}
            hsl = [slice(hh * GDN_DK, (hh + 1) * GDN_DK) for c, hh in inst]
            qc = {c: q_ref[rows[c], :] for c in cs}
            kc = {c: k_ref[rows[c], :] for c in cs}
            kk = {c: _dot_nt(kc[c], kc[c]) for c in cs}
            gcr = [gc_ref[hh, c:c + 1, :] for c, hh in inst]
            gcc = [_col(r, eye) for r in gcr]
            bc = [_col(b_ref[hh, c:c + 1, :], eye) for c, hh in inst]
            lm = [jnp.exp(jnp.where(tri, cc - r, -1e30)) for cc, r in zip(gcc, gcr)]
            e_c = [jnp.exp(cc) for cc in gcc]
            el_c = [jnp.exp(r[:, C - 1:C] - cc) for cc, r in zip(gcc, gcr)]
            gl = [jnp.exp(r[:, C - 1:C]) for r in gcr]
            doc = [do_ref[rows[c], hsl[i]] for i, (c, hh) in enumerate(inst)]
            dvn = [dvn_ref[rows[c], hsl[i]] for i, (c, hh) in enumerate(inst)]
            sv = [sall_ref[hh, c] for c, hh in inst]
            aa = [_dot_nt(jnp.concatenate([_mx(doc[i]), _mx(dvn[i])], axis=0), sv[i]) for i in range(n)]
            dpm = [jnp.where(tri, _dot_nt(doc[i], vn_ref[rows[c], hsl[i]]), 0.0) for i, (c, hh) in enumerate(inst)]
            dqd = [a[:C] for a in aa]
            drhs = [_dot_tn(ti_ref[hh, c], jnp.concatenate([dvn[i], -aa[i][C:]], axis=1))
                    for i, (c, hh) in enumerate(inst)]
            sol = [jnp.concatenate([_mx(u_ref[rows[c], hsl[i]]), w_ref[rows[c], hsl[i]]], axis=1)
                   for i, (c, hh) in enumerate(inst)]
            dnm = [-jnp.where(strict, _dot_nt(drhs[i], sol[i]), 0.0) for i in range(n)]
            dkk = [dnm[i] * bc[i] * lm[i] for i in range(n)]
            dqk = [dpm[i] * lm[i] for i in range(n)]
            dq1 = [_dot(dqk[i], kc[c]) for i, (c, hh) in enumerate(inst)]
            dk1 = [_dot(dkk[i], kc[c]) for i, (c, hh) in enumerate(inst)]
            dk2 = [_dot_tn(dkk[i], kc[c]) for i, (c, hh) in enumerate(inst)]
            dk3 = [_dot_tn(dqk[i], qc[c]) for i, (c, hh) in enumerate(inst)]
            dq_acc = {c: jnp.zeros((C, GDN_DK), F32) for c in cs}
            dk_acc = {c: jnp.zeros((C, GDN_DK), F32) for c in cs}
            for i, (c, hh) in enumerate(inst):
                k_, q_, v_ = kc[c], qc[c], v_ref[rows[c], hsl[i]]
                dvb, dkbe = drhs[i][:, :GDN_DK], drhs[i][:, GDN_DK:]
                dkd = dkd_ref[rows[c], hsl[i]]
                kb = k_ * bc[i]
                dkb = dkbe * e_c[i]
                del_el = dkd * k_ * el_c[i]
                dbc = rsum(dnm[i] * kk[c] * lm[i]) + rsum(dkb * k_ + dvb * v_)
                dq_acc[c] = dq_acc[c] + dq1[i] + dqd[i] * e_c[i]
                dk_acc[c] = dk_acc[c] + dk1[i] + dk2[i] + dk3[i] + dkd * el_c[i] + dkb * bc[i]
                dv_ref[rows[c], hsl[i]] = dvb * bc[i]
                nm = jnp.where(strict, kk[c] * bc[i] * lm[i], 0.0)
                gm = dnm[i] * nm + dpm[i] * pm_ref[hh, c].astype(F32)
                dgc_col = rsum(gm) + rsum((dkbe * kb + dqd[i] * q_) * e_c[i] - del_el)
                dglast = (jnp.sum(jnp.sum(del_el, axis=0, keepdims=True), axis=1, keepdims=True)
                          + dgl_ref[hh, c:c + 1, 0:1] * gl[i])
                dgc_s[hh, c:c + 1, :] = (_row(dgc_col, eye) - jnp.sum(gm, axis=0, keepdims=True)
                                         + jnp.where(lane == C - 1, dglast, 0.0))
                dbeta_ref[hh, c:c + 1, :] = _row(dbc, eye)
            for c in cs:
                dq_ref[rows[c], :] = dq_acc[c]
                dk_ref[rows[c], :] = dk_acc[c]
        for hh in range(2):
            dg_ref[hh] = _dot_hi(dgc_s[hh], lower)

    rows_spec = pl.BlockSpec((2, cpt, C), lambda j, i: (j, i, 0))
    qk_spec = pl.BlockSpec((tt, GDN_DK), lambda j, i: (i, j))
    v_spec = pl.BlockSpec((tt, 2 * GDN_DK), lambda j, i: (i, j))
    cc_spec = pl.BlockSpec((2, cpt, C, C), lambda j, i: (j, i, 0, 0))
    rows_shape = jax.ShapeDtypeStruct((GDN_HV, nC, C), F32)
    return pl.pallas_call(
        body, grid=(GDN_HV // 2, T // tt),
        in_specs=[qk_spec, qk_spec, v_spec, rows_spec, rows_spec, cc_spec, v_spec, v_spec, cc_spec, v_spec,
                  pl.BlockSpec((2, cpt, GDN_DK, GDN_DK), lambda j, i: (j, i, 0, 0)), v_spec, v_spec, v_spec, rows_spec],
        out_specs=[qk_spec, qk_spec, v_spec, rows_spec, rows_spec],
        out_shape=[jax.ShapeDtypeStruct((T, GDN_HV // 2 * GDN_DK), F32),
                   jax.ShapeDtypeStruct((T, GDN_HV // 2 * GDN_DK), F32),
                   jax.ShapeDtypeStruct((T, D_INNER), F32), rows_shape, rows_shape],
        scratch_shapes=[pltpu.VMEM((2, cpt, C), F32)],
        compiler_params=_params("parallel", "parallel"), name=name,
    )(q, k, v, gc, beta, tinv, u, w, pm, vn, sall, do, dvn, dkd, dgl)


def _gdn_gate_bwd(araw, braw, dg, dbeta, alog, dtb, name):
    H, T = araw.shape

    def body(a_ref, b_ref, dg_ref, dbt_ref, alog_ref, dtb_ref, da_ref, db_ref, dalog_ref, ddtb_ref):
        xa = a_ref[...] + dtb_ref[...]
        ea = jnp.exp(alog_ref[...])
        dgv = dg_ref[...]
        da = -dgv * ea * _sigmoid(xa)
        da_ref[...] = da
        dalog_ref[...] = jnp.sum(-dgv * ea * _softplus(xa), axis=1, keepdims=True)
        ddtb_ref[...] = jnp.sum(da, axis=1, keepdims=True)
        bt = _sigmoid(b_ref[...])
        db_ref[...] = dbt_ref[...] * bt * (1.0 - bt)

    return pl.pallas_call(
        body,
        out_shape=[jax.ShapeDtypeStruct((H, T), F32), jax.ShapeDtypeStruct((H, T), F32),
                   jax.ShapeDtypeStruct((H, 1), F32), jax.ShapeDtypeStruct((H, 1), F32)],
        compiler_params=pltpu.CompilerParams(vmem_limit_bytes=VMEM_LIMIT_BYTES), name=name,
    )(araw, braw, dg, dbeta, alog, dtb)


SSD_LOCKSTEP_CHUNKS = 2
SSD_LOCKSTEP_CHUNKS_BWD = 1
SSD_LOCKSTEP_HEADS_BWD = 2


def _ssd_scan_fwd(xs, bm, cm, dtraw, alog, dtb, dskip, name):
    T = xs.shape[0]
    Q = SSD_CHUNK
    tt = min(T, 1024)
    cpt, nC = tt // Q, T // Q
    GW = SSD_R * SSD_P

    def body(alog_ref, dtb_ref, dsk_ref, xs_ref, b_ref, c_ref, dt_ref, y_ref, sall_ref, dto_ref, S, dt_s, acs_s):
        gi, i = pl.program_id(0), pl.program_id(1)

        @pl.when(i == 0)
        def _():
            S[...] = jnp.zeros_like(S)

        tri, _, eye, r_i, c_i = _masks(Q)
        upper = jnp.where(r_i <= c_i, 1.0, 0.0)
        for r in range(SSD_R):
            h = SSD_R * gi + r
            dt = _softplus(dt_ref[r] + dtb_ref[h])
            dto_ref[r] = dt
            dt_s[r] = dt
            acs_s[r] = _dot_hi(-jnp.exp(alog_ref[h]) * dt, upper)

        ps = [slice(r * SSD_P, (r + 1) * SSD_P) for r in range(SSD_R)]
        s_cur = [S[:, ps[r]] for r in range(SSD_R)]
        grp = min(cpt, SSD_LOCKSTEP_CHUNKS)
        for c0 in range(0, cpt, grp):
            cs = list(range(c0, c0 + grp))
            inst = [(c, r) for c in cs for r in range(SSD_R)]
            rows = {c: slice(c * Q, (c + 1) * Q) for c in cs}
            bc_ = {c: b_ref[rows[c], :] for c in cs}
            cc_ = {c: c_ref[rows[c], :] for c in cs}
            cb = {c: _dot_nt(cc_[c], bc_[c]) for c in cs}
            xr = [xs_ref[rows[c], ps[r]] for c, r in inst]
            acr = [acs_s[r, c:c + 1, :] for c, r in inst]
            acc = [_col_bcast(a, Q) for a in acr]
            dtr = [dt_s[r, c:c + 1, :] for c, r in inst]
            mm = [cb[c] * (jnp.exp(jnp.where(tri, acc[i] - acr[i], -1e30)) * dtr[i]) for i, (c, r) in enumerate(inst)]
            bct = {c: bc_[c].T for c in cs}
            st = [_dot(bct[c] * (jnp.exp(acr[i][:, Q - 1:Q] - acr[i]) * dtr[i]), xr[i]) for i, (c, r) in enumerate(inst)]
            yd = [_dot(mm[i], xr[i]) for i in range(len(inst))]
            s_prev = []
            for i, (c, r) in enumerate(inst):
                s_prev.append(s_cur[r])
                s_cur[r] = s_cur[r] * jnp.exp(acr[i][:, Q - 1:Q]) + st[i]
            yo = [_dot(cc_[c] * jnp.exp(acc[i]), s_prev[i]) for i, (c, r) in enumerate(inst)]
            for i, (c, r) in enumerate(inst):
                sall_ref[0, c, :, ps[r]] = s_prev[i]
                y_ref[rows[c], ps[r]] = yd[i] + yo[i] + dsk_ref[SSD_R * gi + r] * xr[i]
        for r in range(SSD_R):
            S[:, ps[r]] = s_cur[r]

    smem = pl.BlockSpec(memory_space=pltpu.SMEM)
    rows_spec = pl.BlockSpec((SSD_R, cpt, Q), lambda g, i: (g, i, 0))
    return pl.pallas_call(
        body, grid=(SSD_G, T // tt),
        in_specs=[smem, smem, smem,
                  pl.BlockSpec((tt, GW), lambda g, i: (i, g)), pl.BlockSpec((tt, SSD_N), lambda g, i: (i, g)),
                  pl.BlockSpec((tt, SSD_N), lambda g, i: (i, g)), rows_spec],
        out_specs=[pl.BlockSpec((tt, GW), lambda g, i: (i, g)),
                   pl.BlockSpec((1, cpt, SSD_N, GW), lambda g, i: (g, i, 0, 0)), rows_spec],
        out_shape=[jax.ShapeDtypeStruct((T, D_INNER), F32), jax.ShapeDtypeStruct((SSD_G, nC, SSD_N, GW), F32),
                   jax.ShapeDtypeStruct((SSD_H, nC, Q), F32)],
        scratch_shapes=[pltpu.VMEM((SSD_N, GW), F32), pltpu.VMEM((SSD_R, cpt, Q), F32),
                        pltpu.VMEM((SSD_R, cpt, Q), F32)],
        compiler_params=_params("parallel", "arbitrary"), name=name,
    )(alog, dtb, dskip, xs, bm, cm, dtraw)


def _ssd_scan_bwd(xs, bm, cm, dt, sall, dy, alog, dskip, name):
    T = xs.shape[0]
    Q = SSD_CHUNK
    tt = min(T, 1024)
    cpt, nC, nT = tt // Q, T // Q, T // tt
    GW = SSD_R * SSD_P

    def body(alog_ref, dsk_ref, xs_ref, b_ref, c_ref, dt_ref, sall_ref, dy_ref,
             dxs_ref, db_ref, dc_ref, da_ref, ddt_ref, dd_ref, dS, acs_s, dacs_s, ddt_s, dd_s):
        gi, i = pl.program_id(0), pl.program_id(1)

        @pl.when(i == 0)
        def _():
            dS[...] = jnp.zeros_like(dS)

        tri, _, eye, r_i, c_i = _masks(Q)
        upper = jnp.where(r_i <= c_i, 1.0, 0.0)
        lower = jnp.where(r_i >= c_i, 1.0, 0.0)
        lane = lax.broadcasted_iota(jnp.int32, (1, Q), 1)
        for r in range(SSD_R):
            acs_s[r] = _dot_hi(-jnp.exp(alog_ref[SSD_R * gi + r]) * dt_ref[r], upper)

        ps = [slice(r * SSD_P, (r + 1) * SSD_P) for r in range(SSD_R)]
        ds_cur = [dS[:, ps[r]] for r in range(SSD_R)]
        grp = min(cpt, SSD_LOCKSTEP_CHUNKS_BWD)
        csum = lambda a: jnp.sum(a, axis=0, keepdims=True)
        tsum = lambda a: jnp.sum(csum(a), axis=1, keepdims=True)
        ones8 = jnp.ones((8, SSD_P), F32)
        for c0 in range(cpt - grp, -1, -grp):
            cs = list(range(c0 + grp - 1, c0 - 1, -1))
            rows = {c: slice(c * Q, (c + 1) * Q) for c in cs}
            bc_ = {c: b_ref[rows[c], :] for c in cs}
            cc_ = {c: c_ref[rows[c], :] for c in cs}
            cb = {c: _dot_nt(cc_[c], bc_[c]) for c in cs}
            cbt = {c: _dot_nt(bc_[c], cc_[c]) for c in cs}
            bct = {c: bc_[c].T for c in cs}
            cct = {c: cc_[c].T for c in cs}
            dcb = {c: jnp.zeros((Q, Q), F32) for c in cs}
            dcbt = {c: jnp.zeros((Q, Q), F32) for c in cs}
            db_acc = {c: jnp.zeros((Q, SSD_N), F32) for c in cs}
            dc_acc = {c: jnp.zeros((Q, SSD_N), F32) for c in cs}
            for h0 in range(0, SSD_R, SSD_LOCKSTEP_HEADS_BWD):
                inst = [(c, r) for c in cs for r in range(h0, h0 + SSD_LOCKSTEP_HEADS_BWD)]
                n = len(inst)
                xr = [xs_ref[rows[c], ps[r]] for c, r in inst]
                dyr = [dy_ref[rows[c], ps[r]] for c, r in inst]
                acr = [acs_s[r, c:c + 1, :] for c, r in inst]
                dtr = [dt_ref[r, c:c + 1, :] for c, r in inst]
                acc = [_col_bcast(a, Q) for a in acr]
                dtb = [_col_bcast(d, Q) for d in dtr]
                al = [a[:, Q - 1:Q] for a in acr]
                e_c = [jnp.exp(a) for a in acc]
                dl_c = [jnp.exp(al[i] - acc[i]) for i in range(n)]
                e_r = [jnp.exp(a) for a in acr]
                dl_r = [jnp.exp(al[i] - acr[i]) for i in range(n)]
                gl = [jnp.exp(a) for a in al]
                lm = [jnp.exp(jnp.where(tri, acc[i] - acr[i], -1e30)) for i in range(n)]
                lmt = [jnp.exp(jnp.where(r_i <= c_i, acr[i] - acc[i], -1e30)) for i in range(n)]
                mmt = [cbt[c] * lmt[i] for i, (c, r) in enumerate(inst)]
                sr = [sall_ref[0, c, :, ps[r]] for c, r in inst]
                dmm0 = [_dot_nt(dyr[i], xr[i]) for i in range(n)]
                dmm0t = [_dot_nt(xr[i], dyr[i]) for i in range(n)]
                dxd1 = [_dot(mmt[i], dyr[i]) for i in range(n)]
                dce = [_dot_nt(dyr[i], sr[i]) for i in range(n)]
                dcet = [_dot_nt(sr[i], dyr[i]) for i in range(n)]
                cdy = [_dot(cct[c] * e_r[i], dyr[i]) for i, (c, r) in enumerate(inst)]
                dsn = []
                for i, (c, r) in enumerate(inst):
                    dsn.append(ds_cur[r])
                    ds_cur[r] = gl[i] * ds_cur[r] + cdy[i]
                dxd = [dxd1[i] + _dot(bc_[c] * dl_c[i], dsn[i]) for i, (c, r) in enumerate(inst)]
                dbd0 = [_dot_nt(xr[i], dsn[i]) for i in range(n)]
                dbd0t = [_dot_nt(dsn[i], xr[i]) for i in range(n)]
                for i, (c, r) in enumerate(inst):
                    dgl = tsum(dsn[i] * sr[i])
                    dc_acc[c] = dc_acc[c] + dce[i] * e_c[i]
                    db_acc[c] = db_acc[c] + dbd0[i] * (dtb[i] * dl_c[i])
                    dl0 = dmm0[i] * lm[i]
                    dl0t = dmm0t[i] * (lmt[i] * dtb[i])
                    dcb[c] = dcb[c] + dl0 * dtr[i]
                    dcbt[c] = dcbt[c] + dl0t
                    csum_gm0 = csum(dl0 * cb[c])
                    rsum_gm = csum(dl0t * cbt[c])
                    r_de = csum(dcet[i] * cct[c]) * e_r[i]
                    r_dl = csum(dbd0t[i] * bct[c]) * dl_r[i]
                    dalast = jnp.sum(r_dl * dtr[i], axis=1, keepdims=True) + dgl * gl[i]
                    dacs_s[r, c:c + 1, :] = (rsum_gm + r_de - (r_dl + csum_gm0) * dtr[i]
                                             + jnp.where(lane == Q - 1, dalast, 0.0))
                    ddt_s[r, c:c + 1, :] = csum_gm0 + r_dl
                    dd_s[r, c:c + 1, :] = _dot_nt(ones8, dyr[i] * xr[i])[0:1]
                    dxs_ref[rows[c], ps[r]] = dxd[i] * dtb[i][:, :SSD_P] + dsk_ref[SSD_R * gi + r] * dyr[i]
            for c in cs:
                dc_ref[rows[c], :] = dc_acc[c] + _dot(dcb[c], bc_[c])
                db_ref[rows[c], :] = db_acc[c] + _dot(dcbt[c], cc_[c])
        for r in range(SSD_R):
            dS[:, ps[r]] = ds_cur[r]
        for r in range(SSD_R):
            da_ref[r] = _dot_hi(dacs_s[r], lower)
            ddt_ref[r] = ddt_s[r]
            dd_ref[r] = dd_s[r]

    rev = lambda i: nT - 1 - i
    smem = pl.BlockSpec(memory_space=pltpu.SMEM)
    rows_spec = pl.BlockSpec((SSD_R, cpt, Q), lambda g, i: (g, rev(i), 0))
    x_spec = pl.BlockSpec((tt, GW), lambda g, i: (rev(i), g))
    n_spec = pl.BlockSpec((tt, SSD_N), lambda g, i: (rev(i), g))
    rows_shape = jax.ShapeDtypeStruct((SSD_H, nC, Q), F32)
    return pl.pallas_call(
        body, grid=(SSD_G, nT),
        in_specs=[smem, smem, x_spec, n_spec, n_spec, rows_spec,
                  pl.BlockSpec((1, cpt, SSD_N, GW), lambda g, i: (g, rev(i), 0, 0)), x_spec],
        out_specs=[x_spec, n_spec, n_spec, rows_spec, rows_spec, rows_spec],
        out_shape=[jax.ShapeDtypeStruct((T, D_INNER), F32), jax.ShapeDtypeStruct((T, SSD_G * SSD_N), F32),
                   jax.ShapeDtypeStruct((T, SSD_G * SSD_N), F32), rows_shape, rows_shape, rows_shape],
        scratch_shapes=[pltpu.VMEM((SSD_N, GW), F32)] + [pltpu.VMEM((SSD_R, cpt, Q), F32)] * 4,
        compiler_params=_params("parallel", "arbitrary"), name=name,
    )(alog, dskip, xs, bm, cm, dt, sall, dy)


def _ssd_gate_bwd(dtraw, dt, da, ddt_direct, ddrow, alog, dtb, name):
    H, T = dtraw.shape

    def body(raw_ref, dt_ref, da_ref, ddt_ref, dd_ref, alog_ref, dtb_ref, draw_ref, dalog_ref, ddtb_ref, dD_ref):
        a = -jnp.exp(alog_ref[...])
        dav = da_ref[...]
        ddt = ddt_ref[...] + dav * a
        draw = ddt * _sigmoid(raw_ref[...] + dtb_ref[...])
        draw_ref[...] = draw
        dalog_ref[...] = jnp.sum(dav * dt_ref[...], axis=1, keepdims=True) * a
        ddtb_ref[...] = jnp.sum(draw, axis=1, keepdims=True)
        dD_ref[...] = jnp.sum(dd_ref[...], axis=1, keepdims=True)

    return pl.pallas_call(
        body,
        out_shape=[jax.ShapeDtypeStruct((H, T), F32)] + [jax.ShapeDtypeStruct((H, 1), F32)] * 3,
        compiler_params=pltpu.CompilerParams(vmem_limit_bytes=VMEM_LIMIT_BYTES), name=name,
    )(dtraw, dt, da, ddt_direct, ddrow, alog, dtb)


def _final_loss(x, fw, tgt, name):
    T = x.shape[0]
    tt = min(T, 512)
    nT = T // tt

    def body(x_ref, w_ref, t_ref, dx_ref, dw_ref, loss_ref, acc):
        i = pl.program_id(0)

        @pl.when(i == 0)
        def _():
            dw_ref[...] = jnp.zeros_like(dw_ref)
            acc[...] = jnp.zeros_like(acc)

        xv = x_ref[...]
        r = lax.rsqrt(jnp.mean(xv * xv, axis=-1, keepdims=True) + EPS)
        xh = xv * r
        err = xh * w_ref[...] - t_ref[...]
        acc[...] += jnp.sum(err * err, axis=0, keepdims=True)
        dout = err * (1.0 / D_MODEL)
        dw_ref[...] += jnp.sum(dout * xh, axis=0, keepdims=True)
        dxn = dout * w_ref[...]
        dx_ref[...] = r * (dxn - xh * jnp.mean(dxn * xh, axis=-1, keepdims=True))

        @pl.when(i == nT - 1)
        def _():
            loss_ref[...] = (0.5 / D_MODEL) * jnp.sum(acc[...], axis=1, keepdims=True)

    row = pl.BlockSpec((tt, D_MODEL), lambda i: (i, 0))
    vec = pl.BlockSpec((1, D_MODEL), lambda i: (0, 0))
    return pl.pallas_call(
        body, grid=(nT,),
        in_specs=[row, vec, row],
        out_specs=[row, vec, pl.BlockSpec((1, 1), lambda i: (0, 0))],
        out_shape=[jax.ShapeDtypeStruct((T, D_MODEL), F32), jax.ShapeDtypeStruct((1, D_MODEL), F32),
                   jax.ShapeDtypeStruct((1, 1), F32)],
        scratch_shapes=[pltpu.VMEM((1, D_MODEL), F32)],
        compiler_params=_params("arbitrary"), name=name,
    )(x, fw, tgt)


def _adamw(parts, w, m, v, name):
    R, C = w.shape
    tr = 128 if R % 128 == 0 else R

    def body(p_ref, w_ref, m_ref, v_ref, g_ref, d_ref, nm_ref, nv_ref):
        g = p_ref[0].astype(F32)
        for s in range(1, N_DEV):
            g = g + p_ref[s].astype(F32)
        mn = ADAM_B1 * m_ref[...] + (1.0 - ADAM_B1) * g
        vn = ADAM_B2 * v_ref[...] + (1.0 - ADAM_B2) * (g * g)
        mh = mn / (1.0 - ADAM_B1 ** ADAM_STEP)
        vh = vn / (1.0 - ADAM_B2 ** ADAM_STEP)
        g_ref[...] = g
        d_ref[...] = -ADAM_LR * (mh / (jnp.sqrt(vh) + ADAM_EPS) + ADAM_WD * w_ref[...])
        nm_ref[...] = mn
        nv_ref[...] = vn

    blk = pl.BlockSpec((tr, C), lambda i: (i, 0))
    return pl.pallas_call(
        body, grid=(R // tr,),
        in_specs=[pl.BlockSpec((N_DEV, tr, C), lambda i: (0, i, 0)), blk, blk, blk],
        out_specs=[blk] * 4,
        out_shape=[jax.ShapeDtypeStruct((R, C), F32)] * 4,
        compiler_params=_params("parallel"), name=name,
    )(parts, w, m, v)


def _me():
    x, y, c = lax.axis_index("x"), lax.axis_index("y"), lax.axis_index("c")
    return x, y, c


def _peer(d):
    x, y, c = _me()
    px = 1 - x if d & 4 else x
    py = 1 - y if d & 2 else y
    pc = 1 - c if d & 1 else c
    return (px, py, pc), 4 * px + 2 * py + pc


def _exchange(arrs, bcast, name):
    n = len(arrs)

    def body(*refs):
        ex = _Exchange(refs[:n], refs[n:2 * n], bcast, *refs[2 * n:])
        ex.begin()
        ex.finish()

    anyspec = pl.BlockSpec(memory_space=pl.ANY)
    return pl.pallas_call(
        body,
        in_specs=[anyspec] * n, out_specs=[anyspec] * n,
        out_shape=_exchange_out_shapes(arrs, bcast),
        scratch_shapes=_exchange_semaphores(n),
        name=name,
    )(*arrs)


def _exchange_out_shapes(arrs, bcast):
    return [jax.ShapeDtypeStruct((N_DEV,) + (a.shape if b else a.shape[1:]), a.dtype) for a, b in zip(arrs, bcast)]


def _exchange_semaphores(n):
    return [pltpu.SemaphoreType.DMA((n, N_DEV - 1)), pltpu.SemaphoreType.DMA((n, N_DEV - 1)),
            pltpu.SemaphoreType.DMA((n,))]


class _Exchange:
    def __init__(self, ins, outs, bcast, ssem, rsem, lsem):
        n = len(ins)
        x, y, c = _me()
        me = 4 * x + 2 * y + c

        def src(a, dest):
            return ins[a] if bcast[a] else ins[a].at[dest]

        self.local = [pltpu.make_async_copy(src(a, me), outs[a].at[me], lsem.at[a]) for a in range(n)]
        self.sends, self.recvs = [], []
        for a in range(n):
            for d in range(1, N_DEV):
                peer, pid = _peer(d)
                self.sends.append(pltpu.make_async_remote_copy(
                    src_ref=src(a, pid), dst_ref=outs[a].at[me], send_sem=ssem.at[a, d - 1],
                    recv_sem=rsem.at[a, d - 1], device_id=peer, device_id_type=MESH))
                self.recvs.append(pltpu.make_async_remote_copy(
                    src_ref=src(a, pid), dst_ref=outs[a].at[pid], send_sem=ssem.at[a, d - 1],
                    recv_sem=rsem.at[a, d - 1], device_id=peer, device_id_type=MESH))

    def begin(self):
        for cp in self.local + self.sends:
            cp.start()

    def finish(self):
        for cp in self.recvs:
            cp.wait_recv()
        for cp in self.sends:
            cp.wait_send()
        for cp in self.local:
            cp.wait()


def _to_rows(cols, chunk):
    T, H = cols.shape
    return cols.T.reshape(H, T // chunk, chunk)


def _from_rows(rows):
    return rows.T


def _pad_cols(a, width):
    return jnp.pad(a, ((0, 0), (0, width - a.shape[1])))


def _local_step(x, tgt, p, late_weights=None, early_grads=None):
    T = x.shape[0]
    zb = lambda n: jnp.zeros((1, n), F32)
    gw = p["gdn_w_in"]
    g_wparts = [gw[:, 0:1024], gw[:, 1024:2048], gw[:, 2048:4096], gw[:, 4096:6144], _pad_cols(gw[:, 6144:6176], PAD_W)]
    nw0, nw1 = p["norm_w"][0:1], p["norm_w"][1:2]
    gcw = p["gdn_conv_w"]
    cw_q, cw_k, cw_v = gcw[:, 0:1024], gcw[:, 1024:2048], gcw[:, 2048:4096]
    g_convs = [(cw_q, zb(1024), True, GDN_DK ** -0.5), (cw_k, zb(1024), True, 1.0), (cw_v, zb(2048), False, 1.0),
               None, None]
    if late_weights is None:
        h0, (q_pre, k_pre, v_pre, z0, ab), (q, k, v), g_cpre = _norm_inproj(x, nw0, g_wparts, g_convs, "gdn_inproj")
    else:
        comm, assemble = late_weights
        h0, (q_pre, k_pre, v_pre, z0, ab), (q, k, v), g_cpre, gathered = _norm_inproj(x, nw0, g_wparts, g_convs,
                                                                                      "gdn_inproj", comm)
        p = dict(p, **assemble(gathered))
    braw = _to_rows(ab[:, 0:GDN_HV], GDN_CHUNK)
    araw = _to_rows(ab[:, GDN_HV:2 * GDN_HV], GDN_CHUNK)
    g_alog, g_dtb = p["gdn_a_log"].reshape(-1), p["gdn_dt_bias"].reshape(-1)
    g_u, g_w, g_pm, g_ti, g_rows, beta_rows, gc_rows, g_qd, g_kd = _gdn_prep(q, k, v, araw, braw, g_alog, g_dtb,
                                                                             "gdn_prep")
    o0, g_vn, g_sall = _gdn_state_fwd(g_qd, g_kd, g_u, g_w, g_pm, gc_rows, "gdn_state_fwd")
    x1 = _out_fwd(o0, z0, p["gdn_norm_w"], p["gdn_w_out"], x, GDN_DK, False, "gdn_out")
    sw = p["ssd_w_in"]
    s_wparts = [sw[:, 0:2048], sw[:, 2048:4096], sw[:, 4096:5120], sw[:, 5120:6144], _pad_cols(sw[:, 6144:6176], PAD_W)]
    scw, scb = p["ssd_conv_w"], p["ssd_conv_b"]
    s_convs = [None, (scw[:, 0:2048], scb[:, 0:2048], False, 1.0), (scw[:, 2048:3072], scb[:, 2048:3072], False, 1.0),
               (scw[:, 3072:4096], scb[:, 3072:4096], False, 1.0), None]
    h1, (z1, xs_pre, b_pre, c_pre, dtp), (xs, bm, cm), s_cpre = _norm_inproj(x1, nw1, s_wparts, s_convs, "ssd_inproj")
    dtraw = _to_rows(dtp[:, 0:SSD_H], SSD_CHUNK)
    s_alog, s_dtb, s_d = p["ssd_a_log"].reshape(-1), p["ssd_dt_bias"].reshape(-1), p["ssd_d"].reshape(-1)
    y1, s_sall, dt_rows = _ssd_scan_fwd(xs, bm, cm, dtraw, s_alog, s_dtb, s_d, "ssd_scan_fwd")
    x2 = _out_fwd(y1, z1, p["ssd_norm_w"], p["ssd_w_out"], x1, D_INNER // SSD_G, True, "ssd_out")
    dx2, d_fw, loss = _final_loss(x2, p["final_norm_w"].reshape(1, -1), tgt, "final_loss")
    dy1, dz1, d_snw, yn1 = _out_bwd(dx2, y1, z1, p["ssd_norm_w"], p["ssd_w_out"], D_INNER // SSD_G, True, "ssd_out_bwd")
    d_swout = _matmul_tn(yn1, dx2, "ssd_wout_grad")
    dxs, dbm, dcm, da_rows, ddt_rows, dd_rows = _ssd_scan_bwd(xs, bm, cm, dt_rows, s_sall, dy1, s_alog, s_d, "ssd_scan_bwd")
    col = lambda a: a.reshape(-1, 1)
    dtraw_g, d_salog, d_sdtb, d_sd = _ssd_gate_bwd(
        dtraw.reshape(SSD_H, T), dt_rows.reshape(SSD_H, T), da_rows.reshape(SSD_H, T),
        ddt_rows.reshape(SSD_H, T), dd_rows.reshape(SSD_H, T), col(s_alog), col(s_dtb), "ssd_gate_bwd")
    dxs_pre, dcw_x, dcb_x = _conv_bwd(xs_pre, s_cpre[0], scw[:, 0:2048], dxs, False, 1.0, "ssd_conv_x_bwd")
    db_pre, dcw_b, dcb_b = _conv_bwd(b_pre, s_cpre[1], scw[:, 2048:3072], dbm, False, 1.0, "ssd_conv_b_bwd")
    dc_pre, dcw_c, dcb_c = _conv_bwd(c_pre, s_cpre[2], scw[:, 3072:4096], dcm, False, 1.0, "ssd_conv_c_bwd")
    ddtp = _pad_cols(_from_rows(dtraw_g), PAD_W)
    s_dparts = [dz1, dxs_pre, db_pre, dc_pre, ddtp]
    dx1, d_nw1 = _inproj_bwd(x1, nw1, s_dparts, s_wparts, dx2, "ssd_inproj_bwd")
    s_dw = [_matmul_tn(h1, d, "ssd_win_grad_%d" % n) for n, d in enumerate(s_dparts)]
    d_swin = jnp.concatenate(s_dw[:4] + [s_dw[4][:, 0:SSD_H]], axis=1)
    early_recv = None
    if early_grads is None:
        do0, dz0, d_gnw, yn0 = _out_bwd(dx1, o0, z0, p["gdn_norm_w"], p["gdn_w_out"], GDN_DK, False, "gdn_out_bwd")
    else:
        do0, dz0, d_gnw, yn0, early_recv = _out_bwd(dx1, o0, z0, p["gdn_norm_w"], p["gdn_w_out"], GDN_DK, False,
                                                    "gdn_out_bwd", early_grads(d_swin, d_swout))
    d_gwout = _matmul_tn(yn0, dx1, "gdn_wout_grad")
    g_dvn, g_dkd, g_dgl = _gdn_state_bwd(g_qd, g_kd, g_w, g_pm, g_vn, g_sall, gc_rows, do0, "gdn_state_bwd")
    dq, dk, dv, dg_rows, dbeta_rows = _gdn_local_bwd(q, k, v, gc_rows, beta_rows, g_ti, g_u, g_w, g_pm, g_vn, g_sall,
                                                     do0, g_dvn, g_dkd, g_dgl, "gdn_local_bwd")
    da_g, db_g, d_galog, d_gdtb = _gdn_gate_bwd(
        araw.reshape(GDN_HV, T), braw.reshape(GDN_HV, T), dg_rows.reshape(GDN_HV, T),
        dbeta_rows.reshape(GDN_HV, T), col(g_alog), col(g_dtb), "gdn_gate_bwd")
    dq_pre, dcw_q, _ = _conv_bwd(q_pre, g_cpre[0], cw_q, dq, True, GDN_DK ** -0.5, "gdn_conv_q_bwd")
    dk_pre, dcw_k, _ = _conv_bwd(k_pre, g_cpre[1], cw_k, dk, True, 1.0, "gdn_conv_k_bwd")
    dv_pre, dcw_v, _ = _conv_bwd(v_pre, g_cpre[2], cw_v, dv, False, 1.0, "gdn_conv_v_bwd")
    dab = _pad_cols(jnp.concatenate([_from_rows(db_g), _from_rows(da_g)], axis=1), PAD_W)
    g_dparts = [dq_pre, dk_pre, dv_pre, dz0, dab]
    dx0, d_nw0 = _inproj_bwd(x, nw0, g_dparts, g_wparts, dx1, "gdn_inproj_bwd")
    g_dw = [_matmul_tn(h0, d, "gdn_win_grad_%d" % n) for n, d in enumerate(g_dparts)]
    d_gwin = jnp.concatenate(g_dw[:4] + [g_dw[4][:, 0:2 * GDN_HV]], axis=1)
    grads = {
        "norm_w": jnp.concatenate([d_nw0, d_nw1], axis=0),
        "gdn_w_in": d_gwin,
        "gdn_conv_w": jnp.concatenate([dcw_q, dcw_k, dcw_v], axis=1),
        "gdn_a_log": d_galog.reshape(1, -1),
        "gdn_dt_bias": d_gdtb.reshape(1, -1),
        "gdn_norm_w": d_gnw,
        "gdn_w_out": d_gwout,
        "ssd_w_in": d_swin,
        "ssd_conv_w": jnp.concatenate([dcw_x, dcw_b, dcw_c], axis=1),
        "ssd_conv_b": jnp.concatenate([dcb_x, dcb_b, dcb_c], axis=1),
        "ssd_dt_bias": d_sdtb.reshape(1, -1),
        "ssd_a_log": d_salog.reshape(1, -1),
        "ssd_d": d_sd.reshape(1, -1),
        "ssd_norm_w": d_snw,
        "ssd_w_out": d_swout,
        "final_norm_w": d_fw,
    }
    if early_grads is not None:
        return loss, dx0, grads, early_recv
    return loss, dx0, grads


WEIGHTS = ["norm_w", "gdn_w_in", "gdn_conv_w", "gdn_a_log", "gdn_dt_bias", "gdn_norm_w", "gdn_w_out", "ssd_w_in",
           "ssd_conv_w", "ssd_conv_b", "ssd_dt_bias", "ssd_a_log", "ssd_d", "ssd_norm_w", "ssd_w_out", "final_norm_w"]
COL_SHARDED = ["gdn_w_in", "ssd_w_in"]
ROW_SHARDED = ["gdn_w_out", "ssd_w_out"]
SMALL_SHARDED = ["gdn_conv_w", "ssd_conv_w", "ssd_conv_b", "ssd_norm_w"]
REPLICATED = ["norm_w", "gdn_a_log", "gdn_dt_bias", "gdn_norm_w", "ssd_dt_bias", "ssd_a_log", "ssd_d", "final_norm_w"]


def _pack(arrs):
    return jnp.concatenate([a.reshape(-1) for a in arrs]).reshape(1, -1)


def _unpack(flat, shapes):
    out, pos = [], 0
    for s in shapes:
        n = 1
        for dim in s:
            n *= dim
        out.append(flat[pos:pos + n].reshape(s))
        pos += n
    return out


def _cols_to_shards(full):
    R, C = full.shape
    return full.reshape(R, N_DEV, C // N_DEV).transpose(1, 0, 2)


def _shards_to_cols(shards):
    n, R, c = shards.shape
    return shards.transpose(1, 0, 2).reshape(R, n * c)


def kernel(x, norm_w, gdn_w_in, gdn_conv_w, gdn_a_log, gdn_dt_bias, gdn_norm_w, gdn_w_out, ssd_w_in, ssd_conv_w, ssd_conv_b, ssd_dt_bias, ssd_a_log, ssd_d, ssd_norm_w, ssd_w_out, final_norm_w, loss_target, m_norm_w, m_gdn_w_in, m_gdn_conv_w, m_gdn_a_log, m_gdn_dt_bias, m_gdn_norm_w, m_gdn_w_out, m_ssd_w_in, m_ssd_conv_w, m_ssd_conv_b, m_ssd_dt_bias, m_ssd_a_log, m_ssd_d, m_ssd_norm_w, m_ssd_w_out, m_final_norm_w, v_norm_w, v_gdn_w_in, v_gdn_conv_w, v_gdn_a_log, v_gdn_dt_bias, v_gdn_norm_w, v_gdn_w_out, v_ssd_w_in, v_ssd_conv_w, v_ssd_conv_b, v_ssd_dt_bias, v_ssd_a_log, v_ssd_d, v_ssd_norm_w, v_ssd_w_out, v_final_norm_w):
    w = dict(norm_w=norm_w, gdn_w_in=gdn_w_in[0], gdn_conv_w=gdn_conv_w[0], gdn_a_log=gdn_a_log,
             gdn_dt_bias=gdn_dt_bias, gdn_norm_w=gdn_norm_w, gdn_w_out=gdn_w_out[0], ssd_w_in=ssd_w_in[0],
             ssd_conv_w=ssd_conv_w[0], ssd_conv_b=ssd_conv_b, ssd_dt_bias=ssd_dt_bias, ssd_a_log=ssd_a_log,
             ssd_d=ssd_d, ssd_norm_w=ssd_norm_w, ssd_w_out=ssd_w_out[0], final_norm_w=final_norm_w.reshape(1, -1))
    m = dict(norm_w=m_norm_w, gdn_w_in=m_gdn_w_in[0], gdn_conv_w=m_gdn_conv_w[0], gdn_a_log=m_gdn_a_log,
             gdn_dt_bias=m_gdn_dt_bias, gdn_norm_w=m_gdn_norm_w, gdn_w_out=m_gdn_w_out[0], ssd_w_in=m_ssd_w_in[0],
             ssd_conv_w=m_ssd_conv_w[0], ssd_conv_b=m_ssd_conv_b, ssd_dt_bias=m_ssd_dt_bias, ssd_a_log=m_ssd_a_log,
             ssd_d=m_ssd_d, ssd_norm_w=m_ssd_norm_w, ssd_w_out=m_ssd_w_out[0], final_norm_w=m_final_norm_w.reshape(1, -1))
    v = dict(norm_w=v_norm_w, gdn_w_in=v_gdn_w_in[0], gdn_conv_w=v_gdn_conv_w[0], gdn_a_log=v_gdn_a_log,
             gdn_dt_bias=v_gdn_dt_bias, gdn_norm_w=v_gdn_norm_w, gdn_w_out=v_gdn_w_out[0], ssd_w_in=v_ssd_w_in[0],
             ssd_conv_w=v_ssd_conv_w[0], ssd_conv_b=v_ssd_conv_b, ssd_dt_bias=v_ssd_dt_bias, ssd_a_log=v_ssd_a_log,
             ssd_d=v_ssd_d, ssd_norm_w=v_ssd_norm_w, ssd_w_out=v_ssd_w_out[0], final_norm_w=v_final_norm_w.reshape(1, -1))
    out_shapes = {n: a.shape for n, a in zip(
        WEIGHTS, [norm_w, gdn_w_in, gdn_conv_w, gdn_a_log, gdn_dt_bias, gdn_norm_w, gdn_w_out, ssd_w_in, ssd_conv_w,
                  ssd_conv_b, ssd_dt_bias, ssd_a_log, ssd_d, ssd_norm_w, ssd_w_out, final_norm_w])}

    small_shapes = [w[n].shape for n in SMALL_SHARDED]
    first = _exchange([_mx(w["gdn_w_in"]), _pack([w[n] for n in SMALL_SHARDED])], [True] * 2, "gather_first")
    full = dict(w)
    full["gdn_w_in"] = _shards_to_cols(first[0])
    small_all = [_unpack(first[1][s, 0], small_shapes) for s in range(N_DEV)]
    for idx, n in enumerate(SMALL_SHARDED):
        full[n] = jnp.concatenate([small_all[s][idx] for s in range(N_DEV)], axis=-1)
    late = ["gdn_w_out", "ssd_w_in", "ssd_w_out"]

    def assemble(gathered):
        return {"gdn_w_out": gathered[0].reshape(-1, D_MODEL), "ssd_w_in": _shards_to_cols(gathered[1]),
                "ssd_w_out": gathered[2].reshape(-1, D_MODEL)}

    def early_grads(d_ssd_w_in, d_ssd_w_out):
        return ([_cols_to_shards(d_ssd_w_in).astype(GRAD_WIRE_DTYPE),
                 d_ssd_w_out.reshape(N_DEV, -1, D_MODEL).astype(GRAD_WIRE_DTYPE)], [False] * 2)

    loss, dx, grads, ssd_recv = _local_step(x[0], loss_target[0], full,
                                            (([_mx(w[n]) for n in late], [True] * 3), assemble), early_grads)

    send_small = jnp.concatenate(
        [_cols_to_shards(grads[n]).reshape(N_DEV, -1) for n in SMALL_SHARDED], axis=1)[:, None, :]
    rep_shapes = [w[n].shape for n in REPLICATED]
    recv = _exchange(
        [_cols_to_shards(grads["gdn_w_in"]).astype(GRAD_WIRE_DTYPE),
         grads["gdn_w_out"].reshape(N_DEV, -1, D_MODEL).astype(GRAD_WIRE_DTYPE),
         send_small, _pack([grads[n] for n in REPLICATED])],
        [False] * 3 + [True], "exchange_grads")

    res = {}
    for n, parts in zip(["gdn_w_in", "gdn_w_out", "ssd_w_in", "ssd_w_out"], [recv[0], recv[1]] + list(ssd_recv)):
        res[n] = _adamw(parts, w[n], m[n], v[n], "adamw_" + n)
    small_res = _adamw(recv[2], *[_pack([t[n] for n in SMALL_SHARDED]) for t in (w, m, v)], "adamw_small")
    rep_res = _adamw(recv[3], *[_pack([t[n] for n in REPLICATED]) for t in (w, m, v)], "adamw_replicated")
    for k4 in range(4):
        for n, a in zip(SMALL_SHARDED, _unpack(small_res[k4][0], small_shapes)):
            res.setdefault(n, [None] * 4)[k4] = a
        for n, a in zip(REPLICATED, _unpack(rep_res[k4][0], rep_shapes)):
            res.setdefault(n, [None] * 4)[k4] = a

    loss = lax.psum(loss[0, 0], ("x", "y", "c"))
    outs = [loss, dx[None]]
    for k4 in range(4):
        outs += [res[n][k4].reshape(out_shapes[n]) for n in WEIGHTS]
    return tuple(outs)
```

```python
import jax
import jax.numpy as jnp
from jax import lax
from jax.experimental import pallas as pl
from jax.experimental.pallas import tpu as pltpu

F32 = jnp.float32
MXU_DTYPE = jnp.bfloat16
GRAD_WIRE_DTYPE = jnp.bfloat16
HI = lax.Precision.HIGHEST
EPS = 1e-6
VMEM_LIMIT_BYTES = 56 * 1024 * 1024
N_DEV = 8
MESH = pl.DeviceIdType.MESH

D_MODEL = 1024
CONV_K = 4
GDN_HV = 16
GDN_DK = 128
GDN_CHUNK = 64
SSD_H = 32
SSD_P = 64
SSD_N = 128
SSD_G = 8
SSD_R = SSD_H // SSD_G
SSD_CHUNK = 128
D_INNER = 2048
PAD_W = 128

ADAM_LR = 0.001
ADAM_B1 = 0.9
ADAM_B2 = 0.999
ADAM_EPS = 1e-08
ADAM_WD = 0.01
ADAM_STEP = 10


def _params(*sem):
    return pltpu.CompilerParams(dimension_semantics=sem, vmem_limit_bytes=VMEM_LIMIT_BYTES)


def _mx(a):
    return a.astype(MXU_DTYPE)


def _dot(a, b):
    return jnp.dot(_mx(a), _mx(b), preferred_element_type=F32)


def _dot_nt(a, b):
    return lax.dot_general(_mx(a), _mx(b), (((1,), (1,)), ((), ())), preferred_element_type=F32)


def _dot_tn(a, b):
    return lax.dot_general(_mx(a), _mx(b), (((0,), (0,)), ((), ())), preferred_element_type=F32)


def _dot_hi(a, b):
    return jnp.dot(a, b, precision=HI, preferred_element_type=F32)


def _sigmoid(x):
    return 0.5 * jnp.tanh(0.5 * x) + 0.5


def _silu(x):
    return x * _sigmoid(x)


def _dsilu(x):
    s = _sigmoid(x)
    return s * (1.0 + x * (1.0 - s))


def _softplus(x):
    return jnp.maximum(x, 0.0) + jnp.log1p(jnp.exp(-jnp.abs(x)))


def _col(r, eye):
    return jnp.sum(jnp.where(eye, r, 0.0), axis=1, keepdims=True)


def _row(c, eye):
    return jnp.sum(jnp.where(eye, c, 0.0), axis=0, keepdims=True)


def _col_bcast(r, n):
    return jnp.broadcast_to(r, (n, n)).T


def _masks(n):
    r = lax.broadcasted_iota(jnp.int32, (n, n), 0)
    c = lax.broadcasted_iota(jnp.int32, (n, n), 1)
    return r >= c, r > c, r == c, r, c


def _with_exchange(comm):
    arrs, bcast = comm if comm else ([], [])
    nc = len(arrs)
    anyspec = pl.BlockSpec(memory_space=pl.ANY)

    def wrap(compute, n_in, n_out):
        def body(*refs):
            cin, cout = refs[n_in:n_in + nc], refs[n_in + nc + n_out:n_in + 2 * nc + n_out]
            sems = refs[n_in + 2 * nc + n_out:n_in + 2 * nc + n_out + 3]
            rest = refs[:n_in] + refs[n_in + nc:n_in + nc + n_out] + refs[n_in + 2 * nc + n_out + (3 if nc else 0):]
            if nc:
                @pl.when(pl.program_id(0) == 0)
                def _():
                    _Exchange(cin, cout, bcast, *sems).begin()
            compute(*rest)
            if nc:
                @pl.when(pl.program_id(0) == pl.num_programs(0) - 1)
                def _():
                    _Exchange(cin, cout, bcast, *sems).finish()
        return body

    return dict(arrs=list(arrs), nc=nc, wrap=wrap, in_specs=[anyspec] * nc, out_specs=[anyspec] * nc,
                out_shape=_exchange_out_shapes(arrs, bcast), scratch=_exchange_semaphores(nc) if nc else [])


INPROJ_COL_BLOCK = 512


def _norm_inproj(x, nw, wparts, convs, name, comm=None):
    T = x.shape[0]
    tt = min(T, 256)
    n = len(wparts)
    ck = [k for k in range(n) if convs[k] is not None]
    nconv = len(ck)
    ex = _with_exchange(comm)

    def compute(x_ref, nw_ref, *refs):
        w_refs, cw_refs = refs[:n], refs[n:n + 2 * nconv]
        h_ref, o_refs = refs[n + 2 * nconv], refs[n + 2 * nconv + 1:2 * n + 2 * nconv + 1]
        post_refs = refs[2 * n + 2 * nconv + 1:2 * n + 3 * nconv + 1]
        cpre_refs = refs[2 * n + 3 * nconv + 1:2 * n + 4 * nconv + 1]
        p_refs = refs[2 * n + 4 * nconv + 1:]
        xv = x_ref[...]
        r = lax.rsqrt(jnp.mean(xv * xv, axis=-1, keepdims=True) + EPS)
        h = _mx(xv * r * nw_ref[...])
        h_ref[...] = h
        for m in range(nconv):
            @pl.when(pl.program_id(0) == 0)
            def _():
                p_refs[m][0:HALO, :] = jnp.zeros((HALO, p_refs[m].shape[1]), F32)

        def conv_block(k, c0, cw):
            m = ck.index(k)
            _, _, l2, scale = convs[k]
            cw_ref, cb_ref, out_ref, P = cw_refs[2 * m], cw_refs[2 * m + 1], post_refs[m], p_refs[m]
            cs = slice(c0, c0 + cw)
            acc = cb_ref[:, cs] + cw_ref[0:1, cs] * P[pl.ds(HALO - 3, tt), cs]
            for j in range(1, CONV_K):
                acc = acc + cw_ref[j:j + 1, cs] * P[pl.ds(HALO - 3 + j, tt), cs]
            cpre_refs[m][:, cs] = acc
            s = _silu(acc)
            if l2:
                sls = [slice(g0, g0 + GDN_DK) for g0 in range(0, cw, GDN_DK)]
                rr = [lax.rsqrt(jnp.sum(s[:, sl] * s[:, sl], axis=-1, keepdims=True) + EPS) for sl in sls]
                for sl, rg in zip(sls, rr):
                    out_ref[:, c0 + sl.start:c0 + sl.stop] = s[:, sl] * rg * scale
            else:
                out_ref[:, cs] = s
            P[0:HALO, cs] = P[tt:tt + HALO, cs]

        pending = None
        for k in range(n):
            for c0 in range(0, widths[k], INPROJ_COL_BLOCK):
                cw = min(INPROJ_COL_BLOCK, widths[k] - c0)
                pre = jnp.dot(h, w_refs[k][:, c0:c0 + cw], preferred_element_type=F32)
                o_refs[k][:, c0:c0 + cw] = pre
                if convs[k] is not None:
                    p_refs[ck.index(k)][HALO:HALO + tt, c0:c0 + cw] = pre
                if pending is not None:
                    conv_block(*pending)
                pending = (k, c0, cw) if convs[k] is not None else None
        if pending is not None:
            conv_block(*pending)

    row = lambda width: pl.BlockSpec((tt, width), lambda i: (i, 0))
    full = lambda a: pl.BlockSpec(a.shape, lambda i: (0, 0))
    once = lambda a: pl.BlockSpec(a.shape, lambda i: (0, 0), pipeline_mode=pl.Buffered(1))
    conv_args = [a for k in ck for a in convs[k][:2]]
    widths = [w.shape[1] for w in wparts]
    outs = pl.pallas_call(
        ex["wrap"](compute, 2 + n + 2 * nconv, 1 + n + 2 * nconv), grid=(T // tt,),
        in_specs=[row(D_MODEL), full(nw)] + [once(w) for w in wparts] + [full(a) for a in conv_args] + ex["in_specs"],
        out_specs=[row(D_MODEL)] + [row(wd) for wd in widths] + [row(widths[k]) for k in ck + ck] + ex["out_specs"],
        out_shape=[jax.ShapeDtypeStruct((T, D_MODEL), MXU_DTYPE)]
        + [jax.ShapeDtypeStruct((T, wd), F32) for wd in widths]
        + [jax.ShapeDtypeStruct((T, widths[k]), F32) for k in ck + ck] + ex["out_shape"],
        scratch_shapes=ex["scratch"] + [pltpu.VMEM((HALO + tt, widths[k]), F32) for k in ck],
        compiler_params=_params("arbitrary"), name=name,
    )(x, nw, *wparts, *conv_args, *ex["arrs"])
    outs = list(outs)
    res = (outs[0], outs[1:1 + n], outs[1 + n:1 + n + nconv], outs[1 + n + nconv:1 + n + 2 * nconv])
    return res + (outs[1 + n + 2 * nconv:],) if comm else res


def _inproj_bwd(x, nw, dparts, wparts, dres, name, comm=None):
    T = x.shape[0]
    tt = min(T, 512)
    n = len(wparts)
    ex = _with_exchange(comm)

    def body(x_ref, nw_ref, dres_ref, *refs):
        d_refs, w_refs, dx_ref, dnw_ref = refs[:n], refs[n:2 * n], refs[2 * n], refs[2 * n + 1]

        @pl.when(pl.program_id(0) == 0)
        def _():
            dnw_ref[...] = jnp.zeros_like(dnw_ref)

        dh = _dot_nt(d_refs[0][...], w_refs[0][...])
        for d_ref, w_ref in zip(d_refs[1:], w_refs[1:]):
            dh = dh + _dot_nt(d_ref[...], w_ref[...])
        xv = x_ref[...]
        r = lax.rsqrt(jnp.mean(xv * xv, axis=-1, keepdims=True) + EPS)
        xh = xv * r
        dnw_ref[...] += jnp.sum(dh * xh, axis=0, keepdims=True)
        dxn = dh * nw_ref[...]
        dx_ref[...] = dres_ref[...] + r * (dxn - xh * jnp.mean(dxn * xh, axis=-1, keepdims=True))

    row = lambda width: pl.BlockSpec((tt, width), lambda i: (i, 0))
    full = lambda a: pl.BlockSpec(a.shape, lambda i: (0, 0))
    outs = pl.pallas_call(
        ex["wrap"](body, 3 + 2 * n, 2), grid=(T // tt,),
        in_specs=[row(D_MODEL), full(nw), row(D_MODEL)] + [row(d.shape[1]) for d in dparts]
        + [pl.BlockSpec(w.shape, lambda i: (0, 0), pipeline_mode=pl.Buffered(1)) for w in wparts] + ex["in_specs"],
        out_specs=[row(D_MODEL), pl.BlockSpec((1, D_MODEL), lambda i: (0, 0))] + ex["out_specs"],
        out_shape=[jax.ShapeDtypeStruct((T, D_MODEL), F32), jax.ShapeDtypeStruct((1, D_MODEL), F32)]
        + ex["out_shape"],
        scratch_shapes=ex["scratch"],
        compiler_params=_params("arbitrary"), name=name,
    )(x, nw, dres, *dparts, *wparts, *ex["arrs"])
    outs = list(outs)
    return outs[:2] + ([outs[2:]] if comm else [])


def _matmul_tn(a, b, name):
    T, K = a.shape
    N = b.shape[1]
    tt = min(T, 2048)
    tn = min(N, 1024)

    def body(a_ref, b_ref, o_ref):
        @pl.when(pl.program_id(1) == 0)
        def _():
            o_ref[...] = jnp.zeros_like(o_ref)

        o_ref[...] += _dot_tn(a_ref[...], b_ref[...])

    return pl.pallas_call(
        body, grid=(N // tn, T // tt),
        in_specs=[pl.BlockSpec((tt, K), lambda n, t: (t, 0)), pl.BlockSpec((tt, tn), lambda n, t: (t, n))],
        out_specs=pl.BlockSpec((K, tn), lambda n, t: (0, n)),
        out_shape=jax.ShapeDtypeStruct((K, N), F32),
        compiler_params=_params("parallel", "arbitrary"), name=name,
    )(a, b)


OUT_COL_BLOCK = 512


def _out_fwd(o, z, w, wout, xres, gs, gate_first, name):
    T = o.shape[0]
    tt = min(T, 256)
    wide = w.shape[1] == D_INNER

    def body(o_ref, z_ref, w_ref, wout_ref, x_ref, out_ref, yn):
        acc = x_ref[...]
        pending = None
        for b0 in range(0, D_INNER, OUT_COL_BLOCK):
            for g0 in range(b0, b0 + OUT_COL_BLOCK, gs):
                sl = slice(g0, g0 + gs)
                og, zg = o_ref[:, sl], z_ref[:, sl]
                wg = w_ref[:, sl] if wide else w_ref[...]
                if gate_first:
                    u = og * _silu(zg)
                    r = lax.rsqrt(jnp.mean(u * u, axis=-1, keepdims=True) + EPS)
                    yn[:, sl] = _mx(u * r * wg)
                else:
                    r = lax.rsqrt(jnp.mean(og * og, axis=-1, keepdims=True) + EPS)
                    yn[:, sl] = _mx(og * r * wg * _silu(zg))
            if pending is not None:
                acc = acc + jnp.dot(yn[:, pending], wout_ref[pending, :], preferred_element_type=F32)
            pending = slice(b0, b0 + OUT_COL_BLOCK)
        out_ref[...] = acc + jnp.dot(yn[:, pending], wout_ref[pending, :], preferred_element_type=F32)

    row = lambda width: pl.BlockSpec((tt, width), lambda i: (i, 0))
    full = lambda a: pl.BlockSpec(a.shape, lambda i: (0, 0))
    return pl.pallas_call(
        body, grid=(T // tt,),
        in_specs=[row(D_INNER), row(D_INNER), full(w), full(wout), row(D_MODEL)],
        out_specs=row(D_MODEL),
        out_shape=jax.ShapeDtypeStruct((T, D_MODEL), F32),
        scratch_shapes=[pltpu.VMEM((tt, D_INNER), MXU_DTYPE)],
        compiler_params=_params("parallel"), name=name,
    )(o, z, w, wout, xres)


def _out_bwd(dx, o, z, w, wout, gs, gate_first, name, comm=None):
    T = o.shape[0]
    tt = min(T, 256)
    wide = w.shape[1] == D_INNER

    def body(dx_ref, o_ref, z_ref, w_ref, wout_ref, do_ref, dz_ref, dw_ref, yn_ref):
        @pl.when(pl.program_id(0) == 0)
        def _():
            dw_ref[...] = jnp.zeros_like(dw_ref)

        dxb = _mx(dx_ref[...])
        blocks = list(range(0, D_INNER, OUT_COL_BLOCK))
        dyn_b = {b0: _dot_nt(dxb, wout_ref[b0:b0 + OUT_COL_BLOCK, :]) for b0 in blocks[:1]}
        dw_acc = jnp.zeros((1, gs), F32)
        for g0 in range(0, D_INNER, gs):
            b0 = g0 - g0 % OUT_COL_BLOCK
            if g0 == b0 and b0 + OUT_COL_BLOCK < D_INNER:
                nb = b0 + OUT_COL_BLOCK
                dyn_b[nb] = _dot_nt(dxb, wout_ref[nb:nb + OUT_COL_BLOCK, :])
            sl = slice(g0, g0 + gs)
            og, zg, dg = o_ref[:, sl], z_ref[:, sl], dyn_b[b0][:, g0 - b0:g0 - b0 + gs]
            wg = w_ref[:, sl] if wide else w_ref[...]
            sz = _silu(zg)
            if gate_first:
                u = og * sz
                r = lax.rsqrt(jnp.mean(u * u, axis=-1, keepdims=True) + EPS)
                uh = u * r
                yn_ref[:, sl] = _mx(uh * wg)
                dw_g = jnp.sum(dg * uh, axis=0, keepdims=True)
                duh = dg * wg
                du = r * (duh - uh * jnp.mean(duh * uh, axis=-1, keepdims=True))
                do_ref[:, sl] = du * sz
                dz_ref[:, sl] = _mx(du * og * _dsilu(zg))
            else:
                r = lax.rsqrt(jnp.mean(og * og, axis=-1, keepdims=True) + EPS)
                oh = og * r
                yn_ref[:, sl] = _mx(oh * wg * sz)
                dw_g = jnp.sum(dg * oh * sz, axis=0, keepdims=True)
                doh = dg * wg * sz
                dz_ref[:, sl] = _mx(dg * oh * wg * _dsilu(zg))
                do_ref[:, sl] = r * (doh - oh * jnp.mean(doh * oh, axis=-1, keepdims=True))
            if wide:
                dw_ref[:, sl] += dw_g
            else:
                dw_acc = dw_acc + dw_g
        if not wide:
            dw_ref[...] += dw_acc

    row = lambda width: pl.BlockSpec((tt, width), lambda i: (i, 0))
    full = lambda a: pl.BlockSpec(a.shape, lambda i: (0, 0))
    ex = _with_exchange(comm)
    outs = pl.pallas_call(
        ex["wrap"](body, 5, 4), grid=(T // tt,),
        in_specs=[row(D_MODEL), row(D_INNER), row(D_INNER), full(w), full(wout)] + ex["in_specs"],
        out_specs=[row(D_INNER), row(D_INNER), full(w), row(D_INNER)] + ex["out_specs"],
        out_shape=[jax.ShapeDtypeStruct((T, D_INNER), F32), jax.ShapeDtypeStruct((T, D_INNER), MXU_DTYPE),
                   jax.ShapeDtypeStruct(w.shape, F32), jax.ShapeDtypeStruct((T, D_INNER), MXU_DTYPE)]
        + ex["out_shape"],
        scratch_shapes=ex["scratch"],
        compiler_params=_params("arbitrary"), name=name,
    )(dx, o, z, w, wout, *ex["arrs"])
    outs = list(outs)
    return outs[:4] + ([outs[4:]] if comm else [])


HALO = 8
CONV_STRIP = 32


def _conv_bwd(pre, cpre_all, w, dpost, l2, scale, name):
    T, C = pre.shape
    tt = min(T, 512)
    tc = min(C, 1024 if l2 else 512)
    strip = 2 * CONV_STRIP if l2 else CONV_STRIP
    nT = T // tt
    ext = tt + HALO

    def body(pre_ref, cp_ref, cn_ref, dpost_ref, dn_ref, w_ref, dpre_ref, dw_ref, db_ref, Q):
        i = pl.program_id(1)

        @pl.when(i == 0)
        def _():
            dw_ref[...] = jnp.zeros_like(dw_ref)
            db_ref[...] = jnp.zeros_like(db_ref)

        wj = [w_ref[j:j + 1, :] for j in range(CONV_K)]
        keep_next = jnp.where(i < nT - 1, 1.0, 0.0)
        fold = lambda a: jnp.sum(a.reshape(strip // 8, 8, tc), axis=0)
        dw_acc = [jnp.zeros((8, tc), F32) for _ in range(CONV_K)]
        db_acc = jnp.zeros((8, tc), F32)
        for r0 in list(range(0, tt, strip)) + [tt]:
            n = strip if r0 < tt else HALO
            cpre = cp_ref[r0:r0 + n, :] if r0 < tt else cn_ref[...]
            dy = dpost_ref[r0:r0 + n, :] if r0 < tt else dn_ref[...] * keep_next
            sg = _sigmoid(cpre)
            ds_c = sg * (1.0 + cpre * (1.0 - sg))
            if l2:
                s = cpre * sg
                sls = [slice(g0, g0 + GDN_DK) for g0 in range(0, tc, GDN_DK)]
                rr = [lax.rsqrt(jnp.sum(s[:, sl] * s[:, sl], axis=-1, keepdims=True) + EPS) for sl in sls]
                yh = [s[:, sl] * r for sl, r in zip(sls, rr)]
                pr = [jnp.sum(dy[:, sl] * y, axis=-1, keepdims=True) for sl, y in zip(sls, yh)]
                for sl, r, y, p in zip(sls, rr, yh, pr):
                    Q[r0:r0 + n, sl] = (scale * r) * (dy[:, sl] - y * p) * ds_c[:, sl]
                dyc = Q[r0:r0 + n, :]
            else:
                dyc = dy * ds_c
                Q[r0:r0 + n, :] = dyc
            if r0 < tt:
                db_acc = db_acc + fold(dyc)
        for r0 in range(0, tt, strip):
            xs = pre_ref[r0:r0 + strip, :]
            dpre = jnp.zeros((strip, tc), F32)
            for j in range(CONV_K):
                qj = Q[pl.ds(3 - j + r0, strip), :]
                dpre = dpre + wj[j] * qj
                dw_acc[j] = dw_acc[j] + fold(qj * xs)
            dpre_ref[r0:r0 + strip, :] = _mx(dpre)
        for j in range(CONV_K):
            dw_ref[j:j + 1, :] += jnp.sum(dw_acc[j], axis=0, keepdims=True)
        db_ref[...] += jnp.sum(db_acc, axis=0, keepdims=True)

    tile = pl.BlockSpec((tt, tc), lambda j, i: (i, j))
    nxt = pl.BlockSpec((HALO, tc), lambda j, i: (jnp.minimum((i + 1) * (tt // HALO), T // HALO - 1), j))
    return pl.pallas_call(
        body, grid=(C // tc, nT),
        in_specs=[tile, tile, nxt, tile, nxt, pl.BlockSpec((CONV_K, tc), lambda j, i: (0, j))],
        out_specs=[tile, pl.BlockSpec((CONV_K, tc), lambda j, i: (0, j)), pl.BlockSpec((1, tc), lambda j, i: (0, j))],
        out_shape=[jax.ShapeDtypeStruct((T, C), MXU_DTYPE), jax.ShapeDtypeStruct((CONV_K, C), F32),
                   jax.ShapeDtypeStruct((1, C), F32)],
        scratch_shapes=[pltpu.VMEM((ext, tc), F32)],
        compiler_params=_params("parallel", "arbitrary"), name=name,
    )(pre, cpre_all, cpre_all, dpost, dpost, w)


GDN_LOCKSTEP_CHUNKS = 16
GDN_SCAN_HEADS = 16


def _inv_unit_lower_many(nms, eye, n):
    xs = [jnp.where(eye, 1.0, 0.0) - nm for nm in nms]
    ps = list(nms)
    k = 2
    while k < n:
        ps = [_dot(p, p) for p in ps]
        xs = [x + _dot(x, p) for x, p in zip(xs, ps)]
        k *= 2
    return xs


def _gdn_prep(q, k, v, araw, braw, alog, dtb, name):
    T = q.shape[0]
    C = GDN_CHUNK
    tt = min(T, 1024)
    cpt, nC = tt // C, T // C
    grp = min(cpt, GDN_LOCKSTEP_CHUNKS)

    def body(alog_ref, dtb_ref, q_ref, k_ref, v_ref, a_ref, b_ref,
             u_ref, w_ref, pm_ref, ti_ref, g_ref, beta_ref, gc_ref, qd_ref, kd_ref):
        j = pl.program_id(0)
        tri, strict, eye, r_i, c_i = _masks(C)
        upper = jnp.where(r_i <= c_i, 1.0, 0.0)
        gcs, bts = [], []
        for hh in range(2):
            h = 2 * j + hh
            g = -jnp.exp(alog_ref[h]) * _softplus(a_ref[hh] + dtb_ref[h])
            bt = _sigmoid(b_ref[hh])
            gc = _dot_hi(g, upper)
            g_ref[hh], beta_ref[hh], gc_ref[hh] = g, bt, gc
            gcs.append(gc)
            bts.append(bt)
        for c0 in range(0, cpt, grp):
            cs = list(range(c0, c0 + grp))
            inst = [(c, hh) for c in cs for hh in range(2)]
            rows = {c: slice(c * C, (c + 1) * C) for c in cs}
            qc = {c: q_ref[rows[c], :] for c in cs}
            kc = {c: k_ref[rows[c], :] for c in cs}
            kk = {c: _dot_nt(kc[c], kc[c]) for c in cs}
            qk = {c: _dot_nt(qc[c], kc[c]) for c in cs}
            gcr = [gcs[hh][c:c + 1, :] for c, hh in inst]
            gcc = [_col(r, eye) for r in gcr]
            bc = [_col(bts[hh][c:c + 1, :], eye) for c, hh in inst]
            lm = [jnp.exp(jnp.where(tri, cc - r, -1e30)) for cc, r in zip(gcc, gcr)]
            nm = [jnp.where(strict, kk[c] * b * l, 0.0) for (c, hh), b, l in zip(inst, bc, lm)]
            tinv = _inv_unit_lower_many(nm, eye, C)
            e_c = [jnp.exp(cc) for cc in gcc]
            rhs = [jnp.concatenate([v_ref[rows[c], hh * GDN_DK:(hh + 1) * GDN_DK] * b, kc[c] * (b * e)], axis=1)
                   for (c, hh), b, e in zip(inst, bc, e_c)]
            sol = [_dot(t, r) for t, r in zip(tinv, rhs)]
            for (c, hh), s, t, l, e, cc, r in zip(inst, sol, tinv, lm, e_c, gcc, gcr):
                hs = slice(hh * GDN_DK, (hh + 1) * GDN_DK)
                u_ref[rows[c], hs] = s[:, :GDN_DK]
                w_ref[rows[c], hs] = _mx(s[:, GDN_DK:])
                pm_ref[hh, c] = _mx(jnp.where(tri, qk[c] * l, 0.0))
                ti_ref[hh, c] = _mx(t)
                qd_ref[rows[c], hs] = _mx(qc[c] * e)
                kd_ref[rows[c], hs] = _mx(kc[c] * jnp.exp(r[:, C - 1:C] - cc))

    smem = pl.BlockSpec(memory_space=pltpu.SMEM)
    rows_spec = pl.BlockSpec((2, cpt, C), lambda j, i: (j, i, 0))
    qk_spec = pl.BlockSpec((tt, GDN_DK), lambda j, i: (i, j))
    v_spec = pl.BlockSpec((tt, 2 * GDN_DK), lambda j, i: (i, j))
    cc_spec = pl.BlockSpec((2, cpt, C, C), lambda j, i: (j, i, 0, 0))
    rows_shape = jax.ShapeDtypeStruct((GDN_HV, nC, C), F32)
    cc_shape = jax.ShapeDtypeStruct((GDN_HV, nC, C, C), MXU_DTYPE)
    return pl.pallas_call(
        body, grid=(GDN_HV // 2, T // tt),
        in_specs=[smem, smem, qk_spec, qk_spec, v_spec, rows_spec, rows_spec],
        out_specs=[v_spec, v_spec, cc_spec, cc_spec, rows_spec, rows_spec, rows_spec, v_spec, v_spec],
        out_shape=[jax.ShapeDtypeStruct((T, D_INNER), F32), jax.ShapeDtypeStruct((T, D_INNER), MXU_DTYPE),
                   cc_shape, cc_shape, rows_shape, rows_shape, rows_shape,
                   jax.ShapeDtypeStruct((T, D_INNER), MXU_DTYPE), jax.ShapeDtypeStruct((T, D_INNER), MXU_DTYPE)],
        compiler_params=_params("parallel", "parallel"), name=name,
    )(alog, dtb, q, k, v, araw, braw)


def _gdn_state_fwd(q, k, u, w, pm, gc, name):
    T = q.shape[0]
    C = GDN_CHUNK
    HG = GDN_SCAN_HEADS
    tt = min(T, 512)
    cpt, nC = tt // C, T // C

    def body(q_ref, k_ref, u_ref, w_ref, pm_ref, gc_ref, o_ref, vn_ref, sall_ref, S):
        @pl.when(pl.program_id(1) == 0)
        def _():
            S[...] = jnp.zeros_like(S)

        heads = list(range(HG))

        def chunk(c, carry):
            rows = pl.ds(pl.multiple_of(c * C, C), C)
            hs = [slice(h * GDN_DK, (h + 1) * GDN_DK) for h in heads]
            gl = [jnp.exp(gc_ref[h, pl.ds(c, 1), C - 1:C]) for h in heads]
            sv = [S[h] for h in heads]
            for h in heads:
                sall_ref[h, c] = _mx(sv[h])
            ws = [_dot(w_ref[rows, hs[h]], sv[h]) for h in heads]
            qsv = [_dot(q_ref[rows, hs[h]], sv[h]) for h in heads]
            vn = [u_ref[rows, hs[h]] - ws[h] for h in heads]
            pv = [_dot(pm_ref[h, c], vn[h]) for h in heads]
            kv = [_dot_tn(k_ref[rows, hs[h]], vn[h]) for h in heads]
            for h in heads:
                vn_ref[rows, hs[h]] = _mx(vn[h])
                o_ref[rows, hs[h]] = qsv[h] + pv[h]
                S[h] = sv[h] * gl[h] + kv[h]
            return carry

        lax.fori_loop(0, cpt, chunk, 0)

    v_spec = pl.BlockSpec((tt, HG * GDN_DK), lambda g, i: (i, g))
    return pl.pallas_call(
        body, grid=(GDN_HV // HG, T // tt),
        in_specs=[v_spec, v_spec, v_spec, v_spec,
                  pl.BlockSpec((HG, cpt, C, C), lambda g, i: (g, i, 0, 0)),
                  pl.BlockSpec((HG, cpt, C), lambda g, i: (g, i, 0))],
        out_specs=[v_spec, v_spec, pl.BlockSpec((HG, cpt, GDN_DK, GDN_DK), lambda g, i: (g, i, 0, 0))],
        out_shape=[jax.ShapeDtypeStruct((T, D_INNER), F32), jax.ShapeDtypeStruct((T, D_INNER), MXU_DTYPE),
                   jax.ShapeDtypeStruct((GDN_HV, nC, GDN_DK, GDN_DK), MXU_DTYPE)],
        scratch_shapes=[pltpu.VMEM((HG, GDN_DK, GDN_DK), F32)],
        compiler_params=_params("parallel", "arbitrary"), name=name,
    )(q, k, u, w, pm, gc)


def _gdn_state_bwd(q, k, w, pm, vn, sall, gc, do, name):
    T = q.shape[0]
    C = GDN_CHUNK
    HG = GDN_SCAN_HEADS
    tt = min(T, 512)
    cpt, nC, nT = tt // C, T // C, T // tt

    def body(q_ref, k_ref, w_ref, pm_ref, vn_ref, sall_ref, gc_ref, do_ref, dvn_ref, dkd_ref, dgl_ref, dS):
        @pl.when(pl.program_id(1) == 0)
        def _():
            dS[...] = jnp.zeros_like(dS)

        heads = list(range(HG))

        def chunk(ci, carry):
            c = cpt - 1 - ci
            rows = pl.ds(pl.multiple_of(c * C, C), C)
            hs = [slice(h * GDN_DK, (h + 1) * GDN_DK) for h in heads]
            gl = [jnp.exp(gc_ref[h, pl.ds(c, 1), C - 1:C]) for h in heads]
            dsn = [dS[h] for h in heads]
            doc = [do_ref[rows, hs[h]] for h in heads]
            kds = [_dot(k_ref[rows, hs[h]], dsn[h]) for h in heads]
            pdo = [_dot_tn(pm_ref[h, c], doc[h]) for h in heads]
            dkd = [_dot_nt(vn_ref[rows, hs[h]], dsn[h]) for h in heads]
            qdo = [_dot_tn(q_ref[rows, hs[h]], doc[h]) for h in heads]
            dvn = [pdo[h] + kds[h] for h in heads]
            wdv = [_dot_tn(w_ref[rows, hs[h]], dvn[h]) for h in heads]
            for h in heads:
                dgl = jnp.sum(jnp.sum(dsn[h] * sall_ref[h, c].astype(F32), axis=0, keepdims=True), axis=1, keepdims=True)
                dgl_ref[h, pl.ds(c, 1), :] = jnp.broadcast_to(dgl, (1, C))
                dvn_ref[rows, hs[h]] = dvn[h]
                dkd_ref[rows, hs[h]] = dkd[h]
                dS[h] = dsn[h] * gl[h] + qdo[h] - wdv[h]
            return carry

        lax.fori_loop(0, cpt, chunk, 0)

    rev = lambda i: nT - 1 - i
    v_spec = pl.BlockSpec((tt, HG * GDN_DK), lambda g, i: (rev(i), g))
    rows_spec = pl.BlockSpec((HG, cpt, C), lambda g, i: (g, rev(i), 0))
    return pl.pallas_call(
        body, grid=(GDN_HV // HG, nT),
        in_specs=[v_spec, v_spec, v_spec, pl.BlockSpec((HG, cpt, C, C), lambda g, i: (g, rev(i), 0, 0)), v_spec,
                  pl.BlockSpec((HG, cpt, GDN_DK, GDN_DK), lambda g, i: (g, rev(i), 0, 0)), rows_spec, v_spec],
        out_specs=[v_spec, v_spec, rows_spec],
        out_shape=[jax.ShapeDtypeStruct((T, D_INNER), F32), jax.ShapeDtypeStruct((T, D_INNER), F32),
                   jax.ShapeDtypeStruct((GDN_HV, nC, C), F32)],
        scratch_shapes=[pltpu.VMEM((HG, GDN_DK, GDN_DK), F32)],
        compiler_params=_params("parallel", "arbitrary"), name=name,
    )(q, k, w, pm, vn, sall, gc, do)


def _gdn_local_bwd(q, k, v, gc, beta, tinv, u, w, pm, vn, sall, do, dvn, dkd, dgl, name):
    T = q.shape[0]
    C = GDN_CHUNK
    tt = min(T, 1024)
    cpt, nC = tt // C, T // C
    grp = min(cpt, GDN_LOCKSTEP_CHUNKS)

    def body(q_ref, k_ref, v_ref, gc_ref, b_ref, ti_ref, u_ref, w_ref, pm_ref, vn_ref, sall_ref, do_ref,
             dvn_ref, dkd_ref, dgl_ref, dq_ref, dk_ref, dv_ref, dg_ref, dbeta_ref, dgc_s):
        tri, strict, eye, r_i, c_i = _masks(C)
        lower = jnp.where(r_i >= c_i, 1.0, 0.0)
        lane = lax.broadcasted_iota(jnp.int32, (1, C), 1)
        rsum = lambda a: jnp.sum(a, axis=1, keepdims=True)
        for c0 in range(0, cpt, grp):
            cs = list(range(c0, c0 + grp))
            inst = [(c, hh) for c in cs for hh in range(2)]
            n = len(inst)
            rows = {c: slice(c * C, (c + 1) * C) for c in cs}
            hsl = [slice(hh * GDN_DK, (hh + 1) * GDN_DK) for c, hh in inst]
            qc = {c: q_ref[rows[c], :] for c in cs}
            kc = {c: k_ref[rows[c], :] for c in cs}
            kk = {c: _dot_nt(kc[c], kc[c]) for c in cs}
            gcr = [gc_ref[hh, c:c + 1, :] for c, hh in inst]
            gcc = [_col(r, eye) for r in gcr]
            bc = [_col(b_ref[hh, c:c + 1, :], eye) for c, hh in inst]
            lm = [jnp.exp(jnp.where(tri, cc - r, -1e30)) for cc, r in zip(gcc, gcr)]
            e_c = [jnp.exp(cc) for cc in gcc]
            el_c = [jnp.exp(r[:, C - 1:C] - cc) for cc, r in zip(gcc, gcr)]
            gl = [jnp.exp(r[:, C - 1:C]) for r in gcr]
            doc = [do_ref[rows[c], hsl[i]] for i, (c, hh) in enumerate(inst)]
            dvn = [dvn_ref[rows[c], hsl[i]] for i, (c, hh) in enumerate(inst)]
            sv = [sall_ref[hh, c] for c, hh in inst]
            aa = [_dot_nt(jnp.concatenate([_mx(doc[i]), _mx(dvn[i])], axis=0), sv[i]) for i in range(n)]
            dpm = [jnp.where(tri, _dot_nt(doc[i], vn_ref[rows[c], hsl[i]]), 0.0) for i, (c, hh) in enumerate(inst)]
            dqd = [a[:C] for a in aa]
            drhs = [_dot_tn(ti_ref[hh, c], jnp.concatenate([dvn[i], -aa[i][C:]], axis=1))
                    for i, (c, hh) in enumerate(inst)]
            sol = [jnp.concatenate([_mx(u_ref[rows[c], hsl[i]]), w_ref[rows[c], hsl[i]]], axis=1)
                   for i, (c, hh) in enumerate(inst)]
            dnm = [-jnp.where(strict, _dot_nt(drhs[i], sol[i]), 0.0) for i in range(n)]
            dkk = [dnm[i] * bc[i] * lm[i] for i in range(n)]
            dqk = [dpm[i] * lm[i] for i in range(n)]
            dq1 = [_dot(dqk[i], kc[c]) for i, (c, hh) in enumerate(inst)]
            dk1 = [_dot(dkk[i], kc[c]) for i, (c, hh) in enumerate(inst)]
            dk2 = [_dot_tn(dkk[i], kc[c]) for i, (c, hh) in enumerate(inst)]
            dk3 = [_dot_tn(dqk[i], qc[c]) for i, (c, hh) in enumerate(inst)]
            dq_acc = {c: jnp.zeros((C, GDN_DK), F32) for c in cs}
            dk_acc = {c: jnp.zeros((C, GDN_DK), F32) for c in cs}
            for i, (c, hh) in enumerate(inst):
                k_, q_, v_ = kc[c], qc[c], v_ref[rows[c], hsl[i]]
                dvb, dkbe = drhs[i][:, :GDN_DK], drhs[i][:, GDN_DK:]
                dkd = dkd_ref[rows[c], hsl[i]]
                kb = k_ * bc[i]
                dkb = dkbe * e_c[i]
                del_el = dkd * k_ * el_c[i]
                dbc = rsum(dnm[i] * kk[c] * lm[i]) + rsum(dkb * k_ + dvb * v_)
                dq_acc[c] = dq_acc[c] + dq1[i] + dqd[i] * e_c[i]
                dk_acc[c] = dk_acc[c] + dk1[i] + dk2[i] + dk3[i] + dkd * el_c[i] + dkb * bc[i]
                dv_ref[rows[c], hsl[i]] = dvb * bc[i]
                nm = jnp.where(strict, kk[c] * bc[i] * lm[i], 0.0)
                gm = dnm[i] * nm + dpm[i] * pm_ref[hh, c].astype(F32)
                dgc_col = rsum(gm) + rsum((dkbe * kb + dqd[i] * q_) * e_c[i] - del_el)
                dglast = (jnp.sum(jnp.sum(del_el, axis=0, keepdims=True), axis=1, keepdims=True)
                          + dgl_ref[hh, c:c + 1, 0:1] * gl[i])
                dgc_s[hh, c:c + 1, :] = (_row(dgc_col, eye) - jnp.sum(gm, axis=0, keepdims=True)
                                         + jnp.where(lane == C - 1, dglast, 0.0))
                dbeta_ref[hh, c:c + 1, :] = _row(dbc, eye)
            for c in cs:
                dq_ref[rows[c], :] = dq_acc[c]
                dk_ref[rows[c], :] = dk_acc[c]
        for hh in range(2):
            dg_ref[hh] = _dot_hi(dgc_s[hh], lower)

    rows_spec = pl.BlockSpec((2, cpt, C), lambda j, i: (j, i, 0))
    qk_spec = pl.BlockSpec((tt, GDN_DK), lambda j, i: (i, j))
    v_spec = pl.BlockSpec((tt, 2 * GDN_DK), lambda j, i: (i, j))
    cc_spec = pl.BlockSpec((2, cpt, C, C), lambda j, i: (j, i, 0, 0))
    rows_shape = jax.ShapeDtypeStruct((GDN_HV, nC, C), F32)
    return pl.pallas_call(
        body, grid=(GDN_HV // 2, T // tt),
        in_specs=[qk_spec, qk_spec, v_spec, rows_spec, rows_spec, cc_spec, v_spec, v_spec, cc_spec, v_spec,
                  pl.BlockSpec((2, cpt, GDN_DK, GDN_DK), lambda j, i: (j, i, 0, 0)), v_spec, v_spec, v_spec, rows_spec],
        out_specs=[qk_spec, qk_spec, v_spec, rows_spec, rows_spec],
        out_shape=[jax.ShapeDtypeStruct((T, GDN_HV // 2 * GDN_DK), F32),
                   jax.ShapeDtypeStruct((T, GDN_HV // 2 * GDN_DK), F32),
                   jax.ShapeDtypeStruct((T, D_INNER), F32), rows_shape, rows_shape],
        scratch_shapes=[pltpu.VMEM((2, cpt, C), F32)],
        compiler_params=_params("parallel", "parallel"), name=name,
    )(q, k, v, gc, beta, tinv, u, w, pm, vn, sall, do, dvn, dkd, dgl)


def _gdn_gate_bwd(araw, braw, dg, dbeta, alog, dtb, name):
    H, T = araw.shape

    def body(a_ref, b_ref, dg_ref, dbt_ref, alog_ref, dtb_ref, da_ref, db_ref, dalog_ref, ddtb_ref):
        xa = a_ref[...] + dtb_ref[...]
        ea = jnp.exp(alog_ref[...])
        dgv = dg_ref[...]
        da = -dgv * ea * _sigmoid(xa)
        da_ref[...] = da
        dalog_ref[...] = jnp.sum(-dgv * ea * _softplus(xa), axis=1, keepdims=True)
        ddtb_ref[...] = jnp.sum(da, axis=1, keepdims=True)
        bt = _sigmoid(b_ref[...])
        db_ref[...] = dbt_ref[...] * bt * (1.0 - bt)

    return pl.pallas_call(
        body,
        out_shape=[jax.ShapeDtypeStruct((H, T), F32), jax.ShapeDtypeStruct((H, T), F32),
                   jax.ShapeDtypeStruct((H, 1), F32), jax.ShapeDtypeStruct((H, 1), F32)],
        compiler_params=pltpu.CompilerParams(vmem_limit_bytes=VMEM_LIMIT_BYTES), name=name,
    )(araw, braw, dg, dbeta, alog, dtb)


SSD_LOCKSTEP_CHUNKS = 2
SSD_LOCKSTEP_CHUNKS_BWD = 1
SSD_LOCKSTEP_HEADS_BWD = 2


def _ssd_scan_fwd(xs, bm, cm, dtraw, alog, dtb, dskip, name):
    T = xs.shape[0]
    Q = SSD_CHUNK
    tt = min(T, 1024)
    cpt, nC = tt // Q, T // Q
    GW = SSD_R * SSD_P

    def body(alog_ref, dtb_ref, dsk_ref, xs_ref, b_ref, c_ref, dt_ref, y_ref, sall_ref, dto_ref, S, dt_s, acs_s):
        gi, i = pl.program_id(0), pl.program_id(1)

        @pl.when(i == 0)
        def _():
            S[...] = jnp.zeros_like(S)

        tri, _, eye, r_i, c_i = _masks(Q)
        upper = jnp.where(r_i <= c_i, 1.0, 0.0)
        for r in range(SSD_R):
            h = SSD_R * gi + r
            dt = _softplus(dt_ref[r] + dtb_ref[h])
            dto_ref[r] = dt
            dt_s[r] = dt
            acs_s[r] = _dot_hi(-jnp.exp(alog_ref[h]) * dt, upper)

        ps = [slice(r * SSD_P, (r + 1) * SSD_P) for r in range(SSD_R)]
        s_cur = [S[:, ps[r]] for r in range(SSD_R)]
        grp = min(cpt, SSD_LOCKSTEP_CHUNKS)
        for c0 in range(0, cpt, grp):
            cs = list(range(c0, c0 + grp))
            inst = [(c, r) for c in cs for r in range(SSD_R)]
            rows = {c: slice(c * Q, (c + 1) * Q) for c in cs}
            bc_ = {c: b_ref[rows[c], :] for c in cs}
            cc_ = {c: c_ref[rows[c], :] for c in cs}
            cb = {c: _dot_nt(cc_[c], bc_[c]) for c in cs}
            xr = [xs_ref[rows[c], ps[r]] for c, r in inst]
            acr = [acs_s[r, c:c + 1, :] for c, r in inst]
            acc = [_col_bcast(a, Q) for a in acr]
            dtr = [dt_s[r, c:c + 1, :] for c, r in inst]
            mm = [cb[c] * (jnp.exp(jnp.where(tri, acc[i] - acr[i], -1e30)) * dtr[i]) for i, (c, r) in enumerate(inst)]
            bct = {c: bc_[c].T for c in cs}
            st = [_dot(bct[c] * (jnp.exp(acr[i][:, Q - 1:Q] - acr[i]) * dtr[i]), xr[i]) for i, (c, r) in enumerate(inst)]
            yd = [_dot(mm[i], xr[i]) for i in range(len(inst))]
            s_prev = []
            for i, (c, r) in enumerate(inst):
                s_prev.append(s_cur[r])
                s_cur[r] = s_cur[r] * jnp.exp(acr[i][:, Q - 1:Q]) + st[i]
            yo = [_dot(cc_[c] * jnp.exp(acc[i]), s_prev[i]) for i, (c, r) in enumerate(inst)]
            for i, (c, r) in enumerate(inst):
                sall_ref[0, c, :, ps[r]] = s_prev[i]
                y_ref[rows[c], ps[r]] = yd[i] + yo[i] + dsk_ref[SSD_R * gi + r] * xr[i]
        for r in range(SSD_R):
            S[:, ps[r]] = s_cur[r]

    smem = pl.BlockSpec(memory_space=pltpu.SMEM)
    rows_spec = pl.BlockSpec((SSD_R, cpt, Q), lambda g, i: (g, i, 0))
    return pl.pallas_call(
        body, grid=(SSD_G, T // tt),
        in_specs=[smem, smem, smem,
                  pl.BlockSpec((tt, GW), lambda g, i: (i, g)), pl.BlockSpec((tt, SSD_N), lambda g, i: (i, g)),
                  pl.BlockSpec((tt, SSD_N), lambda g, i: (i, g)), rows_spec],
        out_specs=[pl.BlockSpec((tt, GW), lambda g, i: (i, g)),
                   pl.BlockSpec((1, cpt, SSD_N, GW), lambda g, i: (g, i, 0, 0)), rows_spec],
        out_shape=[jax.ShapeDtypeStruct((T, D_INNER), F32), jax.ShapeDtypeStruct((SSD_G, nC, SSD_N, GW), F32),
                   jax.ShapeDtypeStruct((SSD_H, nC, Q), F32)],
        scratch_shapes=[pltpu.VMEM((SSD_N, GW), F32), pltpu.VMEM((SSD_R, cpt, Q), F32),
                        pltpu.VMEM((SSD_R, cpt, Q), F32)],
        compiler_params=_params("parallel", "arbitrary"), name=name,
    )(alog, dtb, dskip, xs, bm, cm, dtraw)


def _ssd_scan_bwd(xs, bm, cm, dt, sall, dy, alog, dskip, name):
    T = xs.shape[0]
    Q = SSD_CHUNK
    tt = min(T, 1024)
    cpt, nC, nT = tt // Q, T // Q, T // tt
    GW = SSD_R * SSD_P

    def body(alog_ref, dsk_ref, xs_ref, b_ref, c_ref, dt_ref, sall_ref, dy_ref,
             dxs_ref, db_ref, dc_ref, da_ref, ddt_ref, dd_ref, dS, acs_s, dacs_s, ddt_s, dd_s):
        gi, i = pl.program_id(0), pl.program_id(1)

        @pl.when(i == 0)
        def _():
            dS[...] = jnp.zeros_like(dS)

        tri, _, eye, r_i, c_i = _masks(Q)
        upper = jnp.where(r_i <= c_i, 1.0, 0.0)
        lower = jnp.where(r_i >= c_i, 1.0, 0.0)
        lane = lax.broadcasted_iota(jnp.int32, (1, Q), 1)
        for r in range(SSD_R):
            acs_s[r] = _dot_hi(-jnp.exp(alog_ref[SSD_R * gi + r]) * dt_ref[r], upper)

        ps = [slice(r * SSD_P, (r + 1) * SSD_P) for r in range(SSD_R)]
        ds_cur = [dS[:, ps[r]] for r in range(SSD_R)]
        grp = min(cpt, SSD_LOCKSTEP_CHUNKS_BWD)
        csum = lambda a: jnp.sum(a, axis=0, keepdims=True)
        tsum = lambda a: jnp.sum(csum(a), axis=1, keepdims=True)
        ones8 = jnp.ones((8, SSD_P), F32)
        for c0 in range(cpt - grp, -1, -grp):
            cs = list(range(c0 + grp - 1, c0 - 1, -1))
            rows = {c: slice(c * Q, (c + 1) * Q) for c in cs}
            bc_ = {c: b_ref[rows[c], :] for c in cs}
            cc_ = {c: c_ref[rows[c], :] for c in cs}
            cb = {c: _dot_nt(cc_[c], bc_[c]) for c in cs}
            cbt = {c: _dot_nt(bc_[c], cc_[c]) for c in cs}
            bct = {c: bc_[c].T for c in cs}
            cct = {c: cc_[c].T for c in cs}
            dcb = {c: jnp.zeros((Q, Q), F32) for c in cs}
            dcbt = {c: jnp.zeros((Q, Q), F32) for c in cs}
            db_acc = {c: jnp.zeros((Q, SSD_N), F32) for c in cs}
            dc_acc = {c: jnp.zeros((Q, SSD_N), F32) for c in cs}
            for h0 in range(0, SSD_R, SSD_LOCKSTEP_HEADS_BWD):
                inst = [(c, r) for c in cs for r in range(h0, h0 + SSD_LOCKSTEP_HEADS_BWD)]
                n = len(inst)
                xr = [xs_ref[rows[c], ps[r]] for c, r in inst]
                dyr = [dy_ref[rows[c], ps[r]] for c, r in inst]
                acr = [acs_s[r, c:c + 1, :] for c, r in inst]
                dtr = [dt_ref[r, c:c + 1, :] for c, r in inst]
                acc = [_col_bcast(a, Q) for a in acr]
                dtb = [_col_bcast(d, Q) for d in dtr]
                al = [a[:, Q - 1:Q] for a in acr]
                e_c = [jnp.exp(a) for a in acc]
                dl_c = [jnp.exp(al[i] - acc[i]) for i in range(n)]
                e_r = [jnp.exp(a) for a in acr]
                dl_r = [jnp.exp(al[i] - acr[i]) for i in range(n)]
                gl = [jnp.exp(a) for a in al]
                lm = [jnp.exp(jnp.where(tri, acc[i] - acr[i], -1e30)) for i in range(n)]
                lmt = [jnp.exp(jnp.where(r_i <= c_i, acr[i] - acc[i], -1e30)) for i in range(n)]
                mmt = [cbt[c] * lmt[i] for i, (c, r) in enumerate(inst)]
                sr = [sall_ref[0, c, :, ps[r]] for c, r in inst]
                dmm0 = [_dot_nt(dyr[i], xr[i]) for i in range(n)]
                dmm0t = [_dot_nt(xr[i], dyr[i]) for i in range(n)]
                dxd1 = [_dot(mmt[i], dyr[i]) for i in range(n)]
                dce = [_dot_nt(dyr[i], sr[i]) for i in range(n)]
                dcet = [_dot_nt(sr[i], dyr[i]) for i in range(n)]
                cdy = [_dot(cct[c] * e_r[i], dyr[i]) for i, (c, r) in enumerate(inst)]
                dsn = []
                for i, (c, r) in enumerate(inst):
                    dsn.append(ds_cur[r])
                    ds_cur[r] = gl[i] * ds_cur[r] + cdy[i]
                dxd = [dxd1[i] + _dot(bc_[c] * dl_c[i], dsn[i]) for i, (c, r) in enumerate(inst)]
                dbd0 = [_dot_nt(xr[i], dsn[i]) for i in range(n)]
                dbd0t = [_dot_nt(dsn[i], xr[i]) for i in range(n)]
                for i, (c, r) in enumerate(inst):
                    dgl = tsum(dsn[i] * sr[i])
                    dc_acc[c] = dc_acc[c] + dce[i] * e_c[i]
                    db_acc[c] = db_acc[c] + dbd0[i] * (dtb[i] * dl_c[i])
                    dl0 = dmm0[i] * lm[i]
                    dl0t = dmm0t[i] * (lmt[i] * dtb[i])
                    dcb[c] = dcb[c] + dl0 * dtr[i]
                    dcbt[c] = dcbt[c] + dl0t
                    csum_gm0 = csum(dl0 * cb[c])
                    rsum_gm = csum(dl0t * cbt[c])
                    r_de = csum(dcet[i] * cct[c]) * e_r[i]
                    r_dl = csum(dbd0t[i] * bct[c]) * dl_r[i]
                    dalast = jnp.sum(r_dl * dtr[i], axis=1, keepdims=True) + dgl * gl[i]
                    dacs_s[r, c:c + 1, :] = (rsum_gm + r_de - (r_dl + csum_gm0) * dtr[i]
                                             + jnp.where(lane == Q - 1, dalast, 0.0))
                    ddt_s[r, c:c + 1, :] = csum_gm0 + r_dl
                    dd_s[r, c:c + 1, :] = _dot_nt(ones8, dyr[i] * xr[i])[0:1]
                    dxs_ref[rows[c], ps[r]] = dxd[i] * dtb[i][:, :SSD_P] + dsk_ref[SSD_R * gi + r] * dyr[i]
            for c in cs:
                dc_ref[rows[c], :] = dc_acc[c] + _dot(dcb[c], bc_[c])
                db_ref[rows[c], :] = db_acc[c] + _dot(dcbt[c], cc_[c])
        for r in range(SSD_R):
            dS[:, ps[r]] = ds_cur[r]
        for r in range(SSD_R):
            da_ref[r] = _dot_hi(dacs_s[r], lower)
            ddt_ref[r] = ddt_s[r]
            dd_ref[r] = dd_s[r]

    rev = lambda i: nT - 1 - i
    smem = pl.BlockSpec(memory_space=pltpu.SMEM)
    rows_spec = pl.BlockSpec((SSD_R, cpt, Q), lambda g, i: (g, rev(i), 0))
    x_spec = pl.BlockSpec((tt, GW), lambda g, i: (rev(i), g))
    n_spec = pl.BlockSpec((tt, SSD_N), lambda g, i: (rev(i), g))
    rows_shape = jax.ShapeDtypeStruct((SSD_H, nC, Q), F32)
    return pl.pallas_call(
        body, grid=(SSD_G, nT),
        in_specs=[smem, smem, x_spec, n_spec, n_spec, rows_spec,
                  pl.BlockSpec((1, cpt, SSD_N, GW), lambda g, i: (g, rev(i), 0, 0)), x_spec],
        out_specs=[x_spec, n_spec, n_spec, rows_spec, rows_spec, rows_spec],
        out_shape=[jax.ShapeDtypeStruct((T, D_INNER), F32), jax.ShapeDtypeStruct((T, SSD_G * SSD_N), F32),
                   jax.ShapeDtypeStruct((T, SSD_G * SSD_N), F32), rows_shape, rows_shape, rows_shape],
        scratch_shapes=[pltpu.VMEM((SSD_N, GW), F32)] + [pltpu.VMEM((SSD_R, cpt, Q), F32)] * 4,
        compiler_params=_params("parallel", "arbitrary"), name=name,
    )(alog, dskip, xs, bm, cm, dt, sall, dy)


def _ssd_gate_bwd(dtraw, dt, da, ddt_direct, ddrow, alog, dtb, name):
    H, T = dtraw.shape

    def body(raw_ref, dt_ref, da_ref, ddt_ref, dd_ref, alog_ref, dtb_ref, draw_ref, dalog_ref, ddtb_ref, dD_ref):
        a = -jnp.exp(alog_ref[...])
        dav = da_ref[...]
        ddt = ddt_ref[...] + dav * a
        draw = ddt * _sigmoid(raw_ref[...] + dtb_ref[...])
        draw_ref[...] = draw
        dalog_ref[...] = jnp.sum(dav * dt_ref[...], axis=1, keepdims=True) * a
        ddtb_ref[...] = jnp.sum(draw, axis=1, keepdims=True)
        dD_ref[...] = jnp.sum(dd_ref[...], axis=1, keepdims=True)

    return pl.pallas_call(
        body,
        out_shape=[jax.ShapeDtypeStruct((H, T), F32)] + [jax.ShapeDtypeStruct((H, 1), F32)] * 3,
        compiler_params=pltpu.CompilerParams(vmem_limit_bytes=VMEM_LIMIT_BYTES), name=name,
    )(dtraw, dt, da, ddt_direct, ddrow, alog, dtb)


def _final_loss(x, fw, tgt, name):
    T = x.shape[0]
    tt = min(T, 512)
    nT = T // tt

    def body(x_ref, w_ref, t_ref, dx_ref, dw_ref, loss_ref, acc):
        i = pl.program_id(0)

        @pl.when(i == 0)
        def _():
            dw_ref[...] = jnp.zeros_like(dw_ref)
            acc[...] = jnp.zeros_like(acc)

        xv = x_ref[...]
        r = lax.rsqrt(jnp.mean(xv * xv, axis=-1, keepdims=True) + EPS)
        xh = xv * r
        err = xh * w_ref[...] - t_ref[...]
        acc[...] += jnp.sum(err * err, axis=0, keepdims=True)
        dout = err * (1.0 / D_MODEL)
        dw_ref[...] += jnp.sum(dout * xh, axis=0, keepdims=True)
        dxn = dout * w_ref[...]
        dx_ref[...] = r * (dxn - xh * jnp.mean(dxn * xh, axis=-1, keepdims=True))

        @pl.when(i == nT - 1)
        def _():
            loss_ref[...] = (0.5 / D_MODEL) * jnp.sum(acc[...], axis=1, keepdims=True)

    row = pl.BlockSpec((tt, D_MODEL), lambda i: (i, 0))
    vec = pl.BlockSpec((1, D_MODEL), lambda i: (0, 0))
    return pl.pallas_call(
        body, grid=(nT,),
        in_specs=[row, vec, row],
        out_specs=[row, vec, pl.BlockSpec((1, 1), lambda i: (0, 0))],
        out_shape=[jax.ShapeDtypeStruct((T, D_MODEL), F32), jax.ShapeDtypeStruct((1, D_MODEL), F32),
                   jax.ShapeDtypeStruct((1, 1), F32)],
        scratch_shapes=[pltpu.VMEM((1, D_MODEL), F32)],
        compiler_params=_params("arbitrary"), name=name,
    )(x, fw, tgt)


def _adamw(parts, w, m, v, name):
    R, C = w.shape
    tr = 128 if R % 128 == 0 else R

    def body(p_ref, w_ref, m_ref, v_ref, g_ref, d_ref, nm_ref, nv_ref):
        g = p_ref[0].astype(F32)
        for s in range(1, N_DEV):
            g = g + p_ref[s].astype(F32)
        mn = ADAM_B1 * m_ref[...] + (1.0 - ADAM_B1) * g
        vn = ADAM_B2 * v_ref[...] + (1.0 - ADAM_B2) * (g * g)
        mh = mn / (1.0 - ADAM_B1 ** ADAM_STEP)
        vh = vn / (1.0 - ADAM_B2 ** ADAM_STEP)
        g_ref[...] = g
        d_ref[...] = -ADAM_LR * (mh / (jnp.sqrt(vh) + ADAM_EPS) + ADAM_WD * w_ref[...])
        nm_ref[...] = mn
        nv_ref[...] = vn

    blk = pl.BlockSpec((tr, C), lambda i: (i, 0))
    return pl.pallas_call(
        body, grid=(R // tr,),
        in_specs=[pl.BlockSpec((N_DEV, tr, C), lambda i: (0, i, 0)), blk, blk, blk],
        out_specs=[blk] * 4,
        out_shape=[jax.ShapeDtypeStruct((R, C), F32)] * 4,
        compiler_params=_params("parallel"), name=name,
    )(parts, w, m, v)


def _me():
    x, y, c = lax.axis_index("x"), lax.axis_index("y"), lax.axis_index("c")
    return x, y, c


def _peer(d):
    x, y, c = _me()
    px = 1 - x if d & 4 else x
    py = 1 - y if d & 2 else y
    pc = 1 - c if d & 1 else c
    return (px, py, pc), 4 * px + 2 * py + pc


def _exchange(arrs, bcast, name):
    n = len(arrs)

    def body(*refs):
        ex = _Exchange(refs[:n], refs[n:2 * n], bcast, *refs[2 * n:])
        ex.begin()
        ex.finish()

    anyspec = pl.BlockSpec(memory_space=pl.ANY)
    return pl.pallas_call(
        body,
        in_specs=[anyspec] * n, out_specs=[anyspec] * n,
        out_shape=_exchange_out_shapes(arrs, bcast),
        scratch_shapes=_exchange_semaphores(n),
        name=name,
    )(*arrs)


def _exchange_out_shapes(arrs, bcast):
    return [jax.ShapeDtypeStruct((N_DEV,) + (a.shape if b else a.shape[1:]), a.dtype) for a, b in zip(arrs, bcast)]


def _exchange_semaphores(n):
    return [pltpu.SemaphoreType.DMA((n, N_DEV - 1)), pltpu.SemaphoreType.DMA((n, N_DEV - 1)),
            pltpu.SemaphoreType.DMA((n,))]


class _Exchange:
    def __init__(self, ins, outs, bcast, ssem, rsem, lsem):
        n = len(ins)
        x, y, c = _me()
        me = 4 * x + 2 * y + c

        def src(a, dest):
            return ins[a] if bcast[a] else ins[a].at[dest]

        self.local = [pltpu.make_async_copy(src(a, me), outs[a].at[me], lsem.at[a]) for a in range(n)]
        self.sends, self.recvs = [], []
        for a in range(n):
            for d in range(1, N_DEV):
                peer, pid = _peer(d)
                self.sends.append(pltpu.make_async_remote_copy(
                    src_ref=src(a, pid), dst_ref=outs[a].at[me], send_sem=ssem.at[a, d - 1],
                    recv_sem=rsem.at[a, d - 1], device_id=peer, device_id_type=MESH))
                self.recvs.append(pltpu.make_async_remote_copy(
                    src_ref=src(a, pid), dst_ref=outs[a].at[pid], send_sem=ssem.at[a, d - 1],
                    recv_sem=rsem.at[a, d - 1], device_id=peer, device_id_type=MESH))

    def begin(self):
        for cp in self.local + self.sends:
            cp.start()

    def finish(self):
        for cp in self.recvs:
            cp.wait_recv()
        for cp in self.sends:
            cp.wait_send()
        for cp in self.local:
            cp.wait()


def _to_rows(cols, chunk):
    T, H = cols.shape
    return cols.T.reshape(H, T // chunk, chunk)


def _from_rows(rows):
    return rows.T


def _pad_cols(a, width):
    return jnp.pad(a, ((0, 0), (0, width - a.shape[1])))


def _local_step(x, tgt, p, late_weights=None, early_grads=None, late_grads=None):
    T = x.shape[0]
    zb = lambda n: jnp.zeros((1, n), F32)
    gw = p["gdn_w_in"]
    g_wparts = [gw[:, 0:1024], gw[:, 1024:2048], gw[:, 2048:4096], gw[:, 4096:6144], _pad_cols(gw[:, 6144:6176], PAD_W)]
    nw0, nw1 = p["norm_w"][0:1], p["norm_w"][1:2]
    gcw = p["gdn_conv_w"]
    cw_q, cw_k, cw_v = gcw[:, 0:1024], gcw[:, 1024:2048], gcw[:, 2048:4096]
    g_convs = [(cw_q, zb(1024), True, GDN_DK ** -0.5), (cw_k, zb(1024), True, 1.0), (cw_v, zb(2048), False, 1.0),
               None, None]
    if late_weights is None:
        h0, (q_pre, k_pre, v_pre, z0, ab), (q, k, v), g_cpre = _norm_inproj(x, nw0, g_wparts, g_convs, "gdn_inproj")
    else:
        comm, assemble = late_weights
        h0, (q_pre, k_pre, v_pre, z0, ab), (q, k, v), g_cpre, gathered = _norm_inproj(x, nw0, g_wparts, g_convs,
                                                                                      "gdn_inproj", comm)
        p = dict(p, **assemble(gathered))
    braw = _to_rows(ab[:, 0:GDN_HV], GDN_CHUNK)
    araw = _to_rows(ab[:, GDN_HV:2 * GDN_HV], GDN_CHUNK)
    g_alog, g_dtb = p["gdn_a_log"].reshape(-1), p["gdn_dt_bias"].reshape(-1)
    g_u, g_w, g_pm, g_ti, g_rows, beta_rows, gc_rows, g_qd, g_kd = _gdn_prep(q, k, v, araw, braw, g_alog, g_dtb,
                                                                             "gdn_prep")
    o0, g_vn, g_sall = _gdn_state_fwd(g_qd, g_kd, g_u, g_w, g_pm, gc_rows, "gdn_state_fwd")
    x1 = _out_fwd(o0, z0, p["gdn_norm_w"], p["gdn_w_out"], x, GDN_DK, False, "gdn_out")
    sw = p["ssd_w_in"]
    s_wparts = [sw[:, 0:2048], sw[:, 2048:4096], sw[:, 4096:5120], sw[:, 5120:6144], _pad_cols(sw[:, 6144:6176], PAD_W)]
    scw, scb = p["ssd_conv_w"], p["ssd_conv_b"]
    s_convs = [None, (scw[:, 0:2048], scb[:, 0:2048], False, 1.0), (scw[:, 2048:3072], scb[:, 2048:3072], False, 1.0),
               (scw[:, 3072:4096], scb[:, 3072:4096], False, 1.0), None]
    h1, (z1, xs_pre, b_pre, c_pre, dtp), (xs, bm, cm), s_cpre = _norm_inproj(x1, nw1, s_wparts, s_convs, "ssd_inproj")
    dtraw = _to_rows(dtp[:, 0:SSD_H], SSD_CHUNK)
    s_alog, s_dtb, s_d = p["ssd_a_log"].reshape(-1), p["ssd_dt_bias"].reshape(-1), p["ssd_d"].reshape(-1)
    y1, s_sall, dt_rows = _ssd_scan_fwd(xs, bm, cm, dtraw, s_alog, s_dtb, s_d, "ssd_scan_fwd")
    x2 = _out_fwd(y1, z1, p["ssd_norm_w"], p["ssd_w_out"], x1, D_INNER // SSD_G, True, "ssd_out")
    dx2, d_fw, loss = _final_loss(x2, p["final_norm_w"].reshape(1, -1), tgt, "final_loss")
    dy1, dz1, d_snw, yn1 = _out_bwd(dx2, y1, z1, p["ssd_norm_w"], p["ssd_w_out"], D_INNER // SSD_G, True, "ssd_out_bwd")
    d_swout = _matmul_tn(yn1, dx2, "ssd_wout_grad")
    dxs, dbm, dcm, da_rows, ddt_rows, dd_rows = _ssd_scan_bwd(xs, bm, cm, dt_rows, s_sall, dy1, s_alog, s_d, "ssd_scan_bwd")
    col = lambda a: a.reshape(-1, 1)
    dtraw_g, d_salog, d_sdtb, d_sd = _ssd_gate_bwd(
        dtraw.reshape(SSD_H, T), dt_rows.reshape(SSD_H, T), da_rows.reshape(SSD_H, T),
        ddt_rows.reshape(SSD_H, T), dd_rows.reshape(SSD_H, T), col(s_alog), col(s_dtb), "ssd_gate_bwd")
    dxs_pre, dcw_x, dcb_x = _conv_bwd(xs_pre, s_cpre[0], scw[:, 0:2048], dxs, False, 1.0, "ssd_conv_x_bwd")
    db_pre, dcw_b, dcb_b = _conv_bwd(b_pre, s_cpre[1], scw[:, 2048:3072], dbm, False, 1.0, "ssd_conv_b_bwd")
    dc_pre, dcw_c, dcb_c = _conv_bwd(c_pre, s_cpre[2], scw[:, 3072:4096], dcm, False, 1.0, "ssd_conv_c_bwd")
    ddtp = _pad_cols(_from_rows(dtraw_g), PAD_W)
    s_dparts = [dz1, dxs_pre, db_pre, dc_pre, ddtp]
    dx1, d_nw1 = _inproj_bwd(x1, nw1, s_dparts, s_wparts, dx2, "ssd_inproj_bwd")
    s_dw = [_matmul_tn(h1, d, "ssd_win_grad_%d" % n) for n, d in enumerate(s_dparts)]
    d_swin = jnp.concatenate(s_dw[:4] + [s_dw[4][:, 0:SSD_H]], axis=1)
    early_recv = None
    if early_grads is None:
        do0, dz0, d_gnw, yn0 = _out_bwd(dx1, o0, z0, p["gdn_norm_w"], p["gdn_w_out"], GDN_DK, False, "gdn_out_bwd")
    else:
        do0, dz0, d_gnw, yn0, early_recv = _out_bwd(dx1, o0, z0, p["gdn_norm_w"], p["gdn_w_out"], GDN_DK, False,
                                                    "gdn_out_bwd", early_grads(d_swin, d_swout))
    d_gwout = _matmul_tn(yn0, dx1, "gdn_wout_grad")
    g_dvn, g_dkd, g_dgl = _gdn_state_bwd(g_qd, g_kd, g_w, g_pm, g_vn, g_sall, gc_rows, do0, "gdn_state_bwd")
    dq, dk, dv, dg_rows, dbeta_rows = _gdn_local_bwd(q, k, v, gc_rows, beta_rows, g_ti, g_u, g_w, g_pm, g_vn, g_sall,
                                                     do0, g_dvn, g_dkd, g_dgl, "gdn_local_bwd")
    da_g, db_g, d_galog, d_gdtb = _gdn_gate_bwd(
        araw.reshape(GDN_HV, T), braw.reshape(GDN_HV, T), dg_rows.reshape(GDN_HV, T),
        dbeta_rows.reshape(GDN_HV, T), col(g_alog), col(g_dtb), "gdn_gate_bwd")
    dq_pre, dcw_q, _ = _conv_bwd(q_pre, g_cpre[0], cw_q, dq, True, GDN_DK ** -0.5, "gdn_conv_q_bwd")
    dk_pre, dcw_k, _ = _conv_bwd(k_pre, g_cpre[1], cw_k, dk, True, 1.0, "gdn_conv_k_bwd")
    dv_pre, dcw_v, _ = _conv_bwd(v_pre, g_cpre[2], cw_v, dv, False, 1.0, "gdn_conv_v_bwd")
    dab = _pad_cols(jnp.concatenate([_from_rows(db_g), _from_rows(da_g)], axis=1), PAD_W)
    g_dparts = [dq_pre, dk_pre, dv_pre, dz0, dab]
    g_dw = [_matmul_tn(h0, d, "gdn_win_grad_%d" % n) for n, d in enumerate(g_dparts)]
    d_gwin = jnp.concatenate(g_dw[:4] + [g_dw[4][:, 0:2 * GDN_HV]], axis=1)
    sharded_grads = {
        "gdn_w_in": d_gwin, "gdn_w_out": d_gwout,
        "gdn_conv_w": jnp.concatenate([dcw_q, dcw_k, dcw_v], axis=1),
        "ssd_conv_w": jnp.concatenate([dcw_x, dcw_b, dcw_c], axis=1),
        "ssd_conv_b": jnp.concatenate([dcb_x, dcb_b, dcb_c], axis=1), "ssd_norm_w": d_snw}
    late_recv = None
    if late_grads is None:
        dx0, d_nw0 = _inproj_bwd(x, nw0, g_dparts, g_wparts, dx1, "gdn_inproj_bwd")
    else:
        dx0, d_nw0, late_recv = _inproj_bwd(x, nw0, g_dparts, g_wparts, dx1, "gdn_inproj_bwd",
                                            late_grads(sharded_grads))
    grads = {
        "norm_w": jnp.concatenate([d_nw0, d_nw1], axis=0),
        "gdn_w_in": d_gwin,
        "gdn_conv_w": jnp.concatenate([dcw_q, dcw_k, dcw_v], axis=1),
        "gdn_a_log": d_galog.reshape(1, -1),
        "gdn_dt_bias": d_gdtb.reshape(1, -1),
        "gdn_norm_w": d_gnw,
        "gdn_w_out": d_gwout,
        "ssd_w_in": d_swin,
        "ssd_conv_w": jnp.concatenate([dcw_x, dcw_b, dcw_c], axis=1),
        "ssd_conv_b": jnp.concatenate([dcb_x, dcb_b, dcb_c], axis=1),
        "ssd_dt_bias": d_sdtb.reshape(1, -1),
        "ssd_a_log": d_salog.reshape(1, -1),
        "ssd_d": d_sd.reshape(1, -1),
        "ssd_norm_w": d_snw,
        "ssd_w_out": d_swout,
        "final_norm_w": d_fw,
    }
    if early_grads is not None:
        return loss, dx0, grads, early_recv, late_recv
    return loss, dx0, grads


WEIGHTS = ["norm_w", "gdn_w_in", "gdn_conv_w", "gdn_a_log", "gdn_dt_bias", "gdn_norm_w", "gdn_w_out", "ssd_w_in",
           "ssd_conv_w", "ssd_conv_b", "ssd_dt_bias", "ssd_a_log", "ssd_d", "ssd_norm_w", "ssd_w_out", "final_norm_w"]
COL_SHARDED = ["gdn_w_in", "ssd_w_in"]
ROW_SHARDED = ["gdn_w_out", "ssd_w_out"]
SMALL_SHARDED = ["gdn_conv_w", "ssd_conv_w", "ssd_conv_b", "ssd_norm_w"]
REPLICATED = ["norm_w", "gdn_a_log", "gdn_dt_bias", "gdn_norm_w", "ssd_dt_bias", "ssd_a_log", "ssd_d", "final_norm_w"]


def _pack(arrs):
    return jnp.concatenate([a.reshape(-1) for a in arrs]).reshape(1, -1)


def _unpack(flat, shapes):
    out, pos = [], 0
    for s in shapes:
        n = 1
        for dim in s:
            n *= dim
        out.append(flat[pos:pos + n].reshape(s))
        pos += n
    return out


def _cols_to_shards(full):
    R, C = full.shape
    return full.reshape(R, N_DEV, C // N_DEV).transpose(1, 0, 2)


def _shards_to_cols(shards):
    n, R, c = shards.shape
    return shards.transpose(1, 0, 2).reshape(R, n * c)


def kernel(x, norm_w, gdn_w_in, gdn_conv_w, gdn_a_log, gdn_dt_bias, gdn_norm_w, gdn_w_out, ssd_w_in, ssd_conv_w, ssd_conv_b, ssd_dt_bias, ssd_a_log, ssd_d, ssd_norm_w, ssd_w_out, final_norm_w, loss_target, m_norm_w, m_gdn_w_in, m_gdn_conv_w, m_gdn_a_log, m_gdn_dt_bias, m_gdn_norm_w, m_gdn_w_out, m_ssd_w_in, m_ssd_conv_w, m_ssd_conv_b, m_ssd_dt_bias, m_ssd_a_log, m_ssd_d, m_ssd_norm_w, m_ssd_w_out, m_final_norm_w, v_norm_w, v_gdn_w_in, v_gdn_conv_w, v_gdn_a_log, v_gdn_dt_bias, v_gdn_norm_w, v_gdn_w_out, v_ssd_w_in, v_ssd_conv_w, v_ssd_conv_b, v_ssd_dt_bias, v_ssd_a_log, v_ssd_d, v_ssd_norm_w, v_ssd_w_out, v_final_norm_w):
    w = dict(norm_w=norm_w, gdn_w_in=gdn_w_in[0], gdn_conv_w=gdn_conv_w[0], gdn_a_log=gdn_a_log,
             gdn_dt_bias=gdn_dt_bias, gdn_norm_w=gdn_norm_w, gdn_w_out=gdn_w_out[0], ssd_w_in=ssd_w_in[0],
             ssd_conv_w=ssd_conv_w[0], ssd_conv_b=ssd_conv_b, ssd_dt_bias=ssd_dt_bias, ssd_a_log=ssd_a_log,
             ssd_d=ssd_d, ssd_norm_w=ssd_norm_w, ssd_w_out=ssd_w_out[0], final_norm_w=final_norm_w.reshape(1, -1))
    m = dict(norm_w=m_norm_w, gdn_w_in=m_gdn_w_in[0], gdn_conv_w=m_gdn_conv_w[0], gdn_a_log=m_gdn_a_log,
             gdn_dt_bias=m_gdn_dt_bias, gdn_norm_w=m_gdn_norm_w, gdn_w_out=m_gdn_w_out[0], ssd_w_in=m_ssd_w_in[0],
             ssd_conv_w=m_ssd_conv_w[0], ssd_conv_b=m_ssd_conv_b, ssd_dt_bias=m_ssd_dt_bias, ssd_a_log=m_ssd_a_log,
             ssd_d=m_ssd_d, ssd_norm_w=m_ssd_norm_w, ssd_w_out=m_ssd_w_out[0], final_norm_w=m_final_norm_w.reshape(1, -1))
    v = dict(norm_w=v_norm_w, gdn_w_in=v_gdn_w_in[0], gdn_conv_w=v_gdn_conv_w[0], gdn_a_log=v_gdn_a_log,
             gdn_dt_bias=v_gdn_dt_bias, gdn_norm_w=v_gdn_norm_w, gdn_w_out=v_gdn_w_out[0], ssd_w_in=v_ssd_w_in[0],
             ssd_conv_w=v_ssd_conv_w[0], ssd_conv_b=v_ssd_conv_b, ssd_dt_bias=v_ssd_dt_bias, ssd_a_log=v_ssd_a_log,
             ssd_d=v_ssd_d, ssd_norm_w=v_ssd_norm_w, ssd_w_out=v_ssd_w_out[0], final_norm_w=v_final_norm_w.reshape(1, -1))
    out_shapes = {n: a.shape for n, a in zip(
        WEIGHTS, [norm_w, gdn_w_in, gdn_conv_w, gdn_a_log, gdn_dt_bias, gdn_norm_w, gdn_w_out, ssd_w_in, ssd_conv_w,
                  ssd_conv_b, ssd_dt_bias, ssd_a_log, ssd_d, ssd_norm_w, ssd_w_out, final_norm_w])}

    small_shapes = [w[n].shape for n in SMALL_SHARDED]
    first = _exchange([_mx(w["gdn_w_in"]), _pack([w[n] for n in SMALL_SHARDED])], [True] * 2, "gather_first")
    full = dict(w)
    full["gdn_w_in"] = _shards_to_cols(first[0])
    small_all = [_unpack(first[1][s, 0], small_shapes) for s in range(N_DEV)]
    for idx, n in enumerate(SMALL_SHARDED):
        full[n] = jnp.concatenate([small_all[s][idx] for s in range(N_DEV)], axis=-1)
    late = ["gdn_w_out", "ssd_w_in", "ssd_w_out"]

    def assemble(gathered):
        return {"gdn_w_out": gathered[0].reshape(-1, D_MODEL), "ssd_w_in": _shards_to_cols(gathered[1]),
                "ssd_w_out": gathered[2].reshape(-1, D_MODEL)}

    def early_grads(d_ssd_w_in, d_ssd_w_out):
        return ([_cols_to_shards(d_ssd_w_in).astype(GRAD_WIRE_DTYPE),
                 d_ssd_w_out.reshape(N_DEV, -1, D_MODEL).astype(GRAD_WIRE_DTYPE)], [False] * 2)

    def late_grads(g):
        send_small = jnp.concatenate(
            [_cols_to_shards(g[n]).reshape(N_DEV, -1) for n in SMALL_SHARDED], axis=1)[:, None, :]
        return ([_cols_to_shards(g["gdn_w_in"]).astype(GRAD_WIRE_DTYPE),
                 g["gdn_w_out"].reshape(N_DEV, -1, D_MODEL).astype(GRAD_WIRE_DTYPE), send_small], [False] * 3)

    loss, dx, grads, ssd_recv, gdn_recv = _local_step(
        x[0], loss_target[0], full, (([_mx(w[n]) for n in late], [True] * 3), assemble), early_grads, late_grads)

    rep_shapes = [w[n].shape for n in REPLICATED]
    recv_rep = _exchange([_pack([grads[n] for n in REPLICATED])], [True], "exchange_grads")[0]

    res = {}
    for n, parts in zip(["gdn_w_in", "gdn_w_out", "ssd_w_in", "ssd_w_out"], list(gdn_recv[:2]) + list(ssd_recv)):
        res[n] = _adamw(parts, w[n], m[n], v[n], "adamw_" + n)
    small_res = _adamw(gdn_recv[2], *[_pack([t[n] for n in SMALL_SHARDED]) for t in (w, m, v)], "adamw_small")
    rep_res = _adamw(recv_rep, *[_pack([t[n] for n in REPLICATED]) for t in (w, m, v)], "adamw_replicated")
    for k4 in range(4):
        for n, a in zip(SMALL_SHARDED, _unpack(small_res[k4][0], small_shapes)):
            res.setdefault(n, [None] * 4)[k4] = a
        for n, a in zip(REPLICATED, _unpack(rep_res[k4][0], rep_shapes)):
            res.setdefault(n, [None] * 4)[k4] = a

    loss = lax.psum(loss[0, 0], ("x", "y", "c"))
    outs = [loss, dx[None]]
    for k4 in range(4):
        outs += [res[n][k4].reshape(out_shapes[n]) for n in WEIGHTS]
    return tuple(outs)
```

```python
import jax
import jax.numpy as jnp
from jax import lax
from jax.experimental import pallas as pl
from jax.experimental.pallas import tpu as pltpu

F32 = jnp.float32
MXU_DTYPE = jnp.bfloat16
GRAD_WIRE_DTYPE = jnp.bfloat16
HI = lax.Precision.HIGHEST
EPS = 1e-6
VMEM_LIMIT_BYTES = 56 * 1024 * 1024
N_DEV = 8
MESH = pl.DeviceIdType.MESH

D_MODEL = 1024
CONV_K = 4
GDN_HV = 16
GDN_DK = 128
GDN_CHUNK = 64
SSD_H = 32
SSD_P = 64
SSD_N = 128
SSD_G = 8
SSD_R = SSD_H // SSD_G
SSD_CHUNK = 128
D_INNER = 2048
PAD_W = 128

ADAM_LR = 0.001
ADAM_B1 = 0.9
ADAM_B2 = 0.999
ADAM_EPS = 1e-08
ADAM_WD = 0.01
ADAM_STEP = 10


def _params(*sem):
    return pltpu.CompilerParams(dimension_semantics=sem, vmem_limit_bytes=VMEM_LIMIT_BYTES)


def _mx(a):
    return a.astype(MXU_DTYPE)


def _dot(a, b):
    return jnp.dot(_mx(a), _mx(b), preferred_element_type=F32)


def _dot_nt(a, b):
    return lax.dot_general(_mx(a), _mx(b), (((1,), (1,)), ((), ())), preferred_element_type=F32)


def _dot_tn(a, b):
    return lax.dot_general(_mx(a), _mx(b), (((0,), (0,)), ((), ())), preferred_element_type=F32)


def _dot_hi(a, b):
    return jnp.dot(a, b, precision=HI, preferred_element_type=F32)


def _sigmoid(x):
    return 0.5 * jnp.tanh(0.5 * x) + 0.5


def _silu(x):
    return x * _sigmoid(x)


def _dsilu(x):
    s = _sigmoid(x)
    return s * (1.0 + x * (1.0 - s))


def _softplus(x):
    return jnp.maximum(x, 0.0) + jnp.log1p(jnp.exp(-jnp.abs(x)))


def _col(r, eye):
    return jnp.sum(jnp.where(eye, r, 0.0), axis=1, keepdims=True)


def _row(c, eye):
    return jnp.sum(jnp.where(eye, c, 0.0), axis=0, keepdims=True)


def _col_bcast(r, n):
    return jnp.broadcast_to(r, (n, n)).T


def _masks(n):
    r = lax.broadcasted_iota(jnp.int32, (n, n), 0)
    c = lax.broadcasted_iota(jnp.int32, (n, n), 1)
    return r >= c, r > c, r == c, r, c


def _with_exchange(comm):
    arrs, bcast = comm if comm else ([], [])
    nc = len(arrs)
    anyspec = pl.BlockSpec(memory_space=pl.ANY)

    def wrap(compute, n_in, n_out):
        def body(*refs):
            cin, cout = refs[n_in:n_in + nc], refs[n_in + nc + n_out:n_in + 2 * nc + n_out]
            sems = refs[n_in + 2 * nc + n_out:n_in + 2 * nc + n_out + 3]
            rest = refs[:n_in] + refs[n_in + nc:n_in + nc + n_out] + refs[n_in + 2 * nc + n_out + (3 if nc else 0):]
            if nc:
                @pl.when(pl.program_id(0) == 0)
                def _():
                    _Exchange(cin, cout, bcast, *sems).begin()
            compute(*rest)
            if nc:
                @pl.when(pl.program_id(0) == pl.num_programs(0) - 1)
                def _():
                    _Exchange(cin, cout, bcast, *sems).finish()
        return body

    return dict(arrs=list(arrs), nc=nc, wrap=wrap, in_specs=[anyspec] * nc, out_specs=[anyspec] * nc,
                out_shape=_exchange_out_shapes(arrs, bcast), scratch=_exchange_semaphores(nc) if nc else [])


INPROJ_CONV_STRIP = 256
INPROJ_COL_BLOCK = 512


def _norm_inproj(x, nw, wparts, convs, name, comm=None):
    T = x.shape[0]
    tt = min(T, 256)
    n = len(wparts)
    ck = [k for k in range(n) if convs[k] is not None]
    nconv = len(ck)
    widths = [w.shape[1] for w in wparts]
    conv_blocks = [(k, c0) for k in ck for c0 in range(0, widths[k], INPROJ_COL_BLOCK)]
    ex = _with_exchange(comm)

    def compute(x_ref, nw_ref, *refs):
        w_refs, cw_refs = refs[:n], refs[n:n + 2 * nconv]
        h_ref, o_refs = refs[n + 2 * nconv], refs[n + 2 * nconv + 1:2 * n + 2 * nconv + 1]
        post_refs = refs[2 * n + 2 * nconv + 1:2 * n + 3 * nconv + 1]
        cpre_refs = refs[2 * n + 3 * nconv + 1:2 * n + 4 * nconv + 1]
        p_refs = refs[2 * n + 4 * nconv + 1:]
        xv = x_ref[...]
        r = lax.rsqrt(jnp.mean(xv * xv, axis=-1, keepdims=True) + EPS)
        h = _mx(xv * r * nw_ref[...])
        h_ref[...] = h
        p_of = dict(zip(conv_blocks, p_refs))
        for P in p_refs:
            @pl.when(pl.program_id(0) == 0)
            def _():
                P[0:HALO, :] = jnp.zeros((HALO, P.shape[1]), F32)

        def conv_block(k, c0, cw):
            m = ck.index(k)
            _, _, l2, scale = convs[k]
            cw_ref, cb_ref, out_ref, P = cw_refs[2 * m], cw_refs[2 * m + 1], post_refs[m], p_of[(k, c0)]
            cs = slice(c0, c0 + cw)
            for r0 in range(0, tt, INPROJ_CONV_STRIP):
                rs = slice(r0, r0 + INPROJ_CONV_STRIP)
                acc = cb_ref[:, cs] + cw_ref[0:1, cs] * P[pl.ds(HALO - 3 + r0, INPROJ_CONV_STRIP), :]
                for j in range(1, CONV_K):
                    acc = acc + cw_ref[j:j + 1, cs] * P[pl.ds(HALO - 3 + j + r0, INPROJ_CONV_STRIP), :]
                cpre_refs[m][rs, cs] = acc
                s = _silu(acc)
                if l2:
                    sls = [slice(g0, g0 + GDN_DK) for g0 in range(0, cw, GDN_DK)]
                    rr = [lax.rsqrt(jnp.sum(s[:, sl] * s[:, sl], axis=-1, keepdims=True) + EPS) for sl in sls]
                    for sl, rg in zip(sls, rr):
                        out_ref[rs, c0 + sl.start:c0 + sl.stop] = s[:, sl] * rg * scale
                else:
                    out_ref[rs, cs] = s
            P[0:HALO, :] = P[tt:tt + HALO, :]

        pending = None
        for k in range(n):
            for c0 in range(0, widths[k], INPROJ_COL_BLOCK):
                cw = min(INPROJ_COL_BLOCK, widths[k] - c0)
                pre = jnp.dot(h, w_refs[k][:, c0:c0 + cw], preferred_element_type=F32)
                o_refs[k][:, c0:c0 + cw] = pre
                if convs[k] is not None:
                    p_of[(k, c0)][HALO:HALO + tt, :] = pre
                if pending is not None:
                    conv_block(*pending)
                pending = (k, c0, cw) if convs[k] is not None else None
        if pending is not None:
            conv_block(*pending)

    row = lambda width: pl.BlockSpec((tt, width), lambda i: (i, 0))
    full = lambda a: pl.BlockSpec(a.shape, lambda i: (0, 0))
    once = lambda a: pl.BlockSpec(a.shape, lambda i: (0, 0), pipeline_mode=pl.Buffered(1))
    conv_args = [a for k in ck for a in convs[k][:2]]
    outs = pl.pallas_call(
        ex["wrap"](compute, 2 + n + 2 * nconv, 1 + n + 2 * nconv), grid=(T // tt,),
        in_specs=[row(D_MODEL), full(nw)] + [once(w) for w in wparts] + [full(a) for a in conv_args] + ex["in_specs"],
        out_specs=[row(D_MODEL)] + [row(wd) for wd in widths] + [row(widths[k]) for k in ck + ck] + ex["out_specs"],
        out_shape=[jax.ShapeDtypeStruct((T, D_MODEL), MXU_DTYPE)]
        + [jax.ShapeDtypeStruct((T, wd), F32) for wd in widths]
        + [jax.ShapeDtypeStruct((T, widths[k]), F32) for k in ck + ck] + ex["out_shape"],
        scratch_shapes=ex["scratch"] + [pltpu.VMEM((HALO + tt, min(INPROJ_COL_BLOCK, widths[k] - c0)), F32)
                                        for k, c0 in conv_blocks],
        compiler_params=_params("arbitrary"), name=name,
    )(x, nw, *wparts, *conv_args, *ex["arrs"])
    outs = list(outs)
    res = (outs[0], outs[1:1 + n], outs[1 + n:1 + n + nconv], outs[1 + n + nconv:1 + n + 2 * nconv])
    return res + (outs[1 + n + 2 * nconv:],) if comm else res


def _inproj_bwd(x, nw, dparts, wparts, dres, name, comm=None):
    T = x.shape[0]
    tt = min(T, 512)
    n = len(wparts)
    ex = _with_exchange(comm)

    def body(x_ref, nw_ref, dres_ref, *refs):
        d_refs, w_refs, dx_ref, dnw_ref = refs[:n], refs[n:2 * n], refs[2 * n], refs[2 * n + 1]

        @pl.when(pl.program_id(0) == 0)
        def _():
            dnw_ref[...] = jnp.zeros_like(dnw_ref)

        dh = _dot_nt(d_refs[0][...], w_refs[0][...])
        for d_ref, w_ref in zip(d_refs[1:], w_refs[1:]):
            dh = dh + _dot_nt(d_ref[...], w_ref[...])
        xv = x_ref[...]
        r = lax.rsqrt(jnp.mean(xv * xv, axis=-1, keepdims=True) + EPS)
        xh = xv * r
        dnw_ref[...] += jnp.sum(dh * xh, axis=0, keepdims=True)
        dxn = dh * nw_ref[...]
        dx_ref[...] = dres_ref[...] + r * (dxn - xh * jnp.mean(dxn * xh, axis=-1, keepdims=True))

    row = lambda width: pl.BlockSpec((tt, width), lambda i: (i, 0))
    full = lambda a: pl.BlockSpec(a.shape, lambda i: (0, 0))
    outs = pl.pallas_call(
        ex["wrap"](body, 3 + 2 * n, 2), grid=(T // tt,),
        in_specs=[row(D_MODEL), full(nw), row(D_MODEL)] + [row(d.shape[1]) for d in dparts]
        + [pl.BlockSpec(w.shape, lambda i: (0, 0), pipeline_mode=pl.Buffered(1)) for w in wparts] + ex["in_specs"],
        out_specs=[row(D_MODEL), pl.BlockSpec((1, D_MODEL), lambda i: (0, 0))] + ex["out_specs"],
        out_shape=[jax.ShapeDtypeStruct((T, D_MODEL), F32), jax.ShapeDtypeStruct((1, D_MODEL), F32)]
        + ex["out_shape"],
        scratch_shapes=ex["scratch"],
        compiler_params=_params("arbitrary"), name=name,
    )(x, nw, dres, *dparts, *wparts, *ex["arrs"])
    outs = list(outs)
    return outs[:2] + ([outs[2:]] if comm else [])


def _matmul_tn(a, b, name):
    T, K = a.shape
    N = b.shape[1]
    tt = min(T, 2048)
    tn = min(N, 1024)

    def body(a_ref, b_ref, o_ref):
        @pl.when(pl.program_id(1) == 0)
        def _():
            o_ref[...] = jnp.zeros_like(o_ref)

        o_ref[...] += _dot_tn(a_ref[...], b_ref[...])

    return pl.pallas_call(
        body, grid=(N // tn, T // tt),
        in_specs=[pl.BlockSpec((tt, K), lambda n, t: (t, 0)), pl.BlockSpec((tt, tn), lambda n, t: (t, n))],
        out_specs=pl.BlockSpec((K, tn), lambda n, t: (0, n)),
        out_shape=jax.ShapeDtypeStruct((K, N), F32),
        compiler_params=_params("parallel", "arbitrary"), name=name,
    )(a, b)


OUT_COL_BLOCK = 512


def _out_fwd(o, z, w, wout, xres, gs, gate_first, name):
    T = o.shape[0]
    tt = min(T, 512)
    wide = w.shape[1] == D_INNER

    def body(o_ref, z_ref, w_ref, wout_ref, x_ref, out_ref, yn):
        acc = x_ref[...]
        pending = None
        for b0 in range(0, D_INNER, OUT_COL_BLOCK):
            for g0 in range(b0, b0 + OUT_COL_BLOCK, gs):
                sl = slice(g0, g0 + gs)
                og, zg = o_ref[:, sl], z_ref[:, sl]
                wg = w_ref[:, sl] if wide else w_ref[...]
                if gate_first:
                    u = og * _silu(zg)
                    r = lax.rsqrt(jnp.mean(u * u, axis=-1, keepdims=True) + EPS)
                    yn[:, sl] = _mx(u * r * wg)
                else:
                    r = lax.rsqrt(jnp.mean(og * og, axis=-1, keepdims=True) + EPS)
                    yn[:, sl] = _mx(og * r * wg * _silu(zg))
            if pending is not None:
                acc = acc + jnp.dot(yn[:, pending], wout_ref[pending, :], preferred_element_type=F32)
            pending = slice(b0, b0 + OUT_COL_BLOCK)
        out_ref[...] = acc + jnp.dot(yn[:, pending], wout_ref[pending, :], preferred_element_type=F32)

    row = lambda width: pl.BlockSpec((tt, width), lambda i: (i, 0))
    full = lambda a: pl.BlockSpec(a.shape, lambda i: (0, 0))
    return pl.pallas_call(
        body, grid=(T // tt,),
        in_specs=[row(D_INNER), row(D_INNER), full(w), full(wout), row(D_MODEL)],
        out_specs=row(D_MODEL),
        out_shape=jax.ShapeDtypeStruct((T, D_MODEL), F32),
        scratch_shapes=[pltpu.VMEM((tt, D_INNER), MXU_DTYPE)],
        compiler_params=_params("parallel"), name=name,
    )(o, z, w, wout, xres)


def _out_bwd(dx, o, z, w, wout, gs, gate_first, name, comm=None):
    T = o.shape[0]
    tt = min(T, 256)
    wide = w.shape[1] == D_INNER

    def body(dx_ref, o_ref, z_ref, w_ref, wout_ref, do_ref, dz_ref, dw_ref, yn_ref):
        @pl.when(pl.program_id(0) == 0)
        def _():
            dw_ref[...] = jnp.zeros_like(dw_ref)

        dxb = _mx(dx_ref[...])
        blocks = list(range(0, D_INNER, OUT_COL_BLOCK))
        dyn_b = {b0: _dot_nt(dxb, wout_ref[b0:b0 + OUT_COL_BLOCK, :]) for b0 in blocks[:1]}
        dw_acc = jnp.zeros((1, gs), F32)
        for g0 in range(0, D_INNER, gs):
            b0 = g0 - g0 % OUT_COL_BLOCK
            if g0 == b0 and b0 + OUT_COL_BLOCK < D_INNER:
                nb = b0 + OUT_COL_BLOCK
                dyn_b[nb] = _dot_nt(dxb, wout_ref[nb:nb + OUT_COL_BLOCK, :])
            sl = slice(g0, g0 + gs)
            og, zg, dg = o_ref[:, sl], z_ref[:, sl], dyn_b[b0][:, g0 - b0:g0 - b0 + gs]
            wg = w_ref[:, sl] if wide else w_ref[...]
            sz = _silu(zg)
            if gate_first:
                u = og * sz
                r = lax.rsqrt(jnp.mean(u * u, axis=-1, keepdims=True) + EPS)
                uh = u * r
                yn_ref[:, sl] = _mx(uh * wg)
                dw_g = jnp.sum(dg * uh, axis=0, keepdims=True)
                duh = dg * wg
                du = r * (duh - uh * jnp.mean(duh * uh, axis=-1, keepdims=True))
                do_ref[:, sl] = du * sz
                dz_ref[:, sl] = _mx(du * og * _dsilu(zg))
            else:
                r = lax.rsqrt(jnp.mean(og * og, axis=-1, keepdims=True) + EPS)
                oh = og * r
                yn_ref[:, sl] = _mx(oh * wg * sz)
                dw_g = jnp.sum(dg * oh * sz, axis=0, keepdims=True)
                doh = dg * wg * sz
                dz_ref[:, sl] = _mx(dg * oh * wg * _dsilu(zg))
                do_ref[:, sl] = r * (doh - oh * jnp.mean(doh * oh, axis=-1, keepdims=True))
            if wide:
                dw_ref[:, sl] += dw_g
            else:
                dw_acc = dw_acc + dw_g
        if not wide:
            dw_ref[...] += dw_acc

    row = lambda width: pl.BlockSpec((tt, width), lambda i: (i, 0))
    full = lambda a: pl.BlockSpec(a.shape, lambda i: (0, 0))
    ex = _with_exchange(comm)
    outs = pl.pallas_call(
        ex["wrap"](body, 5, 4), grid=(T // tt,),
        in_specs=[row(D_MODEL), row(D_INNER), row(D_INNER), full(w), full(wout)] + ex["in_specs"],
        out_specs=[row(D_INNER), row(D_INNER), full(w), row(D_INNER)] + ex["out_specs"],
        out_shape=[jax.ShapeDtypeStruct((T, D_INNER), F32), jax.ShapeDtypeStruct((T, D_INNER), MXU_DTYPE),
                   jax.ShapeDtypeStruct(w.shape, F32), jax.ShapeDtypeStruct((T, D_INNER), MXU_DTYPE)]
        + ex["out_shape"],
        scratch_shapes=ex["scratch"],
        compiler_params=_params("arbitrary"), name=name,
    )(dx, o, z, w, wout, *ex["arrs"])
    outs = list(outs)
    return outs[:4] + ([outs[4:]] if comm else [])


HALO = 8
CONV_STRIP = 32


def _conv_bwd(pre, cpre_all, w, dpost, l2, scale, name):
    T, C = pre.shape
    tt = min(T, 512)
    tc = min(C, 1024 if l2 else 512)
    strip = 2 * CONV_STRIP if l2 else CONV_STRIP
    nT = T // tt
    ext = tt + HALO

    def body(pre_ref, cp_ref, cn_ref, dpost_ref, dn_ref, w_ref, dpre_ref, dw_ref, db_ref, Q):
        i = pl.program_id(1)

        @pl.when(i == 0)
        def _():
            dw_ref[...] = jnp.zeros_like(dw_ref)
            db_ref[...] = jnp.zeros_like(db_ref)

        wj = [w_ref[j:j + 1, :] for j in range(CONV_K)]
        keep_next = jnp.where(i < nT - 1, 1.0, 0.0)
        fold = lambda a: jnp.sum(a.reshape(strip // 8, 8, tc), axis=0)
        dw_acc = [jnp.zeros((8, tc), F32) for _ in range(CONV_K)]
        db_acc = jnp.zeros((8, tc), F32)
        for r0 in list(range(0, tt, strip)) + [tt]:
            n = strip if r0 < tt else HALO
            cpre = cp_ref[r0:r0 + n, :] if r0 < tt else cn_ref[...]
            dy = dpost_ref[r0:r0 + n, :] if r0 < tt else dn_ref[...] * keep_next
            sg = _sigmoid(cpre)
            ds_c = sg * (1.0 + cpre * (1.0 - sg))
            if l2:
                s = cpre * sg
                sls = [slice(g0, g0 + GDN_DK) for g0 in range(0, tc, GDN_DK)]
                rr = [lax.rsqrt(jnp.sum(s[:, sl] * s[:, sl], axis=-1, keepdims=True) + EPS) for sl in sls]
                yh = [s[:, sl] * r for sl, r in zip(sls, rr)]
                pr = [jnp.sum(dy[:, sl] * y, axis=-1, keepdims=True) for sl, y in zip(sls, yh)]
                for sl, r, y, p in zip(sls, rr, yh, pr):
                    Q[r0:r0 + n, sl] = (scale * r) * (dy[:, sl] - y * p) * ds_c[:, sl]
                dyc = Q[r0:r0 + n, :]
            else:
                dyc = dy * ds_c
                Q[r0:r0 + n, :] = dyc
            if r0 < tt:
                db_acc = db_acc + fold(dyc)
        for r0 in range(0, tt, strip):
            xs = pre_ref[r0:r0 + strip, :]
            dpre = jnp.zeros((strip, tc), F32)
            for j in range(CONV_K):
                qj = Q[pl.ds(3 - j + r0, strip), :]
                dpre = dpre + wj[j] * qj
                dw_acc[j] = dw_acc[j] + fold(qj * xs)
            dpre_ref[r0:r0 + strip, :] = _mx(dpre)
        for j in range(CONV_K):
            dw_ref[j:j + 1, :] += jnp.sum(dw_acc[j], axis=0, keepdims=True)
        db_ref[...] += jnp.sum(db_acc, axis=0, keepdims=True)

    tile = pl.BlockSpec((tt, tc), lambda j, i: (i, j))
    nxt = pl.BlockSpec((HALO, tc), lambda j, i: (jnp.minimum((i + 1) * (tt // HALO), T // HALO - 1), j))
    return pl.pallas_call(
        body, grid=(C // tc, nT),
        in_specs=[tile, tile, nxt, tile, nxt, pl.BlockSpec((CONV_K, tc), lambda j, i: (0, j))],
        out_specs=[tile, pl.BlockSpec((CONV_K, tc), lambda j, i: (0, j)), pl.BlockSpec((1, tc), lambda j, i: (0, j))],
        out_shape=[jax.ShapeDtypeStruct((T, C), MXU_DTYPE), jax.ShapeDtypeStruct((CONV_K, C), F32),
                   jax.ShapeDtypeStruct((1, C), F32)],
        scratch_shapes=[pltpu.VMEM((ext, tc), F32)],
        compiler_params=_params("parallel", "arbitrary"), name=name,
    )(pre, cpre_all, cpre_all, dpost, dpost, w)


GDN_LOCKSTEP_CHUNKS = 16
GDN_LOCKSTEP_CHUNKS_BWD = 16
GDN_SCAN_HEADS = 16


def _inv_unit_lower_many(nms, eye, n):
    xs = [jnp.where(eye, 1.0, 0.0) - nm for nm in nms]
    ps = list(nms)
    k = 2
    while k < n:
        ps = [_dot(p, p) for p in ps]
        xs = [x + _dot(x, p) for x, p in zip(xs, ps)]
        k *= 2
    return xs


def _gdn_prep(q, k, v, araw, braw, alog, dtb, name):
    T = q.shape[0]
    C = GDN_CHUNK
    tt = min(T, 1024)
    cpt, nC = tt // C, T // C
    grp = min(cpt, GDN_LOCKSTEP_CHUNKS)

    def body(alog_ref, dtb_ref, q_ref, k_ref, v_ref, a_ref, b_ref,
             u_ref, w_ref, pm_ref, ti_ref, g_ref, beta_ref, gc_ref, qd_ref, kd_ref):
        j = pl.program_id(0)
        tri, strict, eye, r_i, c_i = _masks(C)
        upper = jnp.where(r_i <= c_i, 1.0, 0.0)
        gcs, bts = [], []
        for hh in range(2):
            h = 2 * j + hh
            g = -jnp.exp(alog_ref[h]) * _softplus(a_ref[hh] + dtb_ref[h])
            bt = _sigmoid(b_ref[hh])
            gc = _dot_hi(g, upper)
            g_ref[hh], beta_ref[hh], gc_ref[hh] = g, bt, gc
            gcs.append(gc)
            bts.append(bt)
        for c0 in range(0, cpt, grp):
            cs = list(range(c0, c0 + grp))
            inst = [(c, hh) for c in cs for hh in range(2)]
            rows = {c: slice(c * C, (c + 1) * C) for c in cs}
            qc = {c: q_ref[rows[c], :] for c in cs}
            kc = {c: k_ref[rows[c], :] for c in cs}
            kk = {c: _dot_nt(kc[c], kc[c]) for c in cs}
            qk = {c: _dot_nt(qc[c], kc[c]) for c in cs}
            gcr = [gcs[hh][c:c + 1, :] for c, hh in inst]
            gcc = [_col(r, eye) for r in gcr]
            bc = [_col(bts[hh][c:c + 1, :], eye) for c, hh in inst]
            lm = [jnp.exp(jnp.where(tri, cc - r, -1e30)) for cc, r in zip(gcc, gcr)]
            nm = [jnp.where(strict, kk[c] * b * l, 0.0) for (c, hh), b, l in zip(inst, bc, lm)]
            tinv = _inv_unit_lower_many(nm, eye, C)
            e_c = [jnp.exp(cc) for cc in gcc]
            rhs = [jnp.concatenate([v_ref[rows[c], hh * GDN_DK:(hh + 1) * GDN_DK] * b, kc[c] * (b * e)], axis=1)
                   for (c, hh), b, e in zip(inst, bc, e_c)]
            sol = [_dot(t, r) for t, r in zip(tinv, rhs)]
            for (c, hh), s, t, l, e, cc, r in zip(inst, sol, tinv, lm, e_c, gcc, gcr):
                hs = slice(hh * GDN_DK, (hh + 1) * GDN_DK)
                u_ref[rows[c], hs] = s[:, :GDN_DK]
                w_ref[rows[c], hs] = _mx(s[:, GDN_DK:])
                pm_ref[hh, c] = _mx(jnp.where(tri, qk[c] * l, 0.0))
                ti_ref[hh, c] = _mx(t)
                qd_ref[rows[c], hs] = _mx(qc[c] * e)
                kd_ref[rows[c], hs] = _mx(kc[c] * jnp.exp(r[:, C - 1:C] - cc))

    smem = pl.BlockSpec(memory_space=pltpu.SMEM)
    rows_spec = pl.BlockSpec((2, cpt, C), lambda j, i: (j, i, 0))
    qk_spec = pl.BlockSpec((tt, GDN_DK), lambda j, i: (i, j))
    v_spec = pl.BlockSpec((tt, 2 * GDN_DK), lambda j, i: (i, j))
    cc_spec = pl.BlockSpec((2, cpt, C, C), lambda j, i: (j, i, 0, 0))
    rows_shape = jax.ShapeDtypeStruct((GDN_HV, nC, C), F32)
    cc_shape = jax.ShapeDtypeStruct((GDN_HV, nC, C, C), MXU_DTYPE)
    return pl.pallas_call(
        body, grid=(GDN_HV // 2, T // tt),
        in_specs=[smem, smem, qk_spec, qk_spec, v_spec, rows_spec, rows_spec],
        out_specs=[v_spec, v_spec, cc_spec, cc_spec, rows_spec, rows_spec, rows_spec, v_spec, v_spec],
        out_shape=[jax.ShapeDtypeStruct((T, D_INNER), F32), jax.ShapeDtypeStruct((T, D_INNER), MXU_DTYPE),
                   cc_shape, cc_shape, rows_shape, rows_shape, rows_shape,
                   jax.ShapeDtypeStruct((T, D_INNER), MXU_DTYPE), jax.ShapeDtypeStruct((T, D_INNER), MXU_DTYPE)],
        compiler_params=_params("parallel", "parallel"), name=name,
    )(alog, dtb, q, k, v, araw, braw)


def _gdn_state_fwd(q, k, u, w, pm, gc, name):
    T = q.shape[0]
    C = GDN_CHUNK
    HG = GDN_SCAN_HEADS
    tt = min(T, 512)
    cpt, nC = tt // C, T // C

    def body(q_ref, k_ref, u_ref, w_ref, pm_ref, gc_ref, o_ref, vn_ref, sall_ref, S):
        @pl.when(pl.program_id(1) == 0)
        def _():
            S[...] = jnp.zeros_like(S)

        heads = list(range(HG))

        def chunk(c, carry):
            rows = pl.ds(pl.multiple_of(c * C, C), C)
            hs = [slice(h * GDN_DK, (h + 1) * GDN_DK) for h in heads]
            gl = [jnp.exp(gc_ref[h, pl.ds(c, 1), C - 1:C]) for h in heads]
            sv = [S[h] for h in heads]
            for h in heads:
                sall_ref[h, c] = _mx(sv[h])
            ws = [_dot(w_ref[rows, hs[h]], sv[h]) for h in heads]
            qsv = [_dot(q_ref[rows, hs[h]], sv[h]) for h in heads]
            vn = [u_ref[rows, hs[h]] - ws[h] for h in heads]
            pv = [_dot(pm_ref[h, c], vn[h]) for h in heads]
            kv = [_dot_tn(k_ref[rows, hs[h]], vn[h]) for h in heads]
            for h in heads:
                vn_ref[rows, hs[h]] = _mx(vn[h])
                o_ref[rows, hs[h]] = qsv[h] + pv[h]
                S[h] = sv[h] * gl[h] + kv[h]
            return carry

        lax.fori_loop(0, cpt, chunk, 0)

    v_spec = pl.BlockSpec((tt, HG * GDN_DK), lambda g, i: (i, g))
    return pl.pallas_call(
        body, grid=(GDN_HV // HG, T // tt),
        in_specs=[v_spec, v_spec, v_spec, v_spec,
                  pl.BlockSpec((HG, cpt, C, C), lambda g, i: (g, i, 0, 0)),
                  pl.BlockSpec((HG, cpt, C), lambda g, i: (g, i, 0))],
        out_specs=[v_spec, v_spec, pl.BlockSpec((HG, cpt, GDN_DK, GDN_DK), lambda g, i: (g, i, 0, 0))],
        out_shape=[jax.ShapeDtypeStruct((T, D_INNER), F32), jax.ShapeDtypeStruct((T, D_INNER), MXU_DTYPE),
                   jax.ShapeDtypeStruct((GDN_HV, nC, GDN_DK, GDN_DK), MXU_DTYPE)],
        scratch_shapes=[pltpu.VMEM((HG, GDN_DK, GDN_DK), F32)],
        compiler_params=_params("parallel", "arbitrary"), name=name,
    )(q, k, u, w, pm, gc)


def _gdn_state_bwd(q, k, w, pm, vn, sall, gc, do, name):
    T = q.shape[0]
    C = GDN_CHUNK
    HG = GDN_SCAN_HEADS
    tt = min(T, 512)
    cpt, nC, nT = tt // C, T // C, T // tt

    def body(q_ref, k_ref, w_ref, pm_ref, vn_ref, sall_ref, gc_ref, do_ref, dvn_ref, dkd_ref, dgl_ref, dS):
        @pl.when(pl.program_id(1) == 0)
        def _():
            dS[...] = jnp.zeros_like(dS)

        heads = list(range(HG))

        def chunk(ci, carry):
            c = cpt - 1 - ci
            rows = pl.ds(pl.multiple_of(c * C, C), C)
            hs = [slice(h * GDN_DK, (h + 1) * GDN_DK) for h in heads]
            gl = [jnp.exp(gc_ref[h, pl.ds(c, 1), C - 1:C]) for h in heads]
            dsn = [dS[h] for h in heads]
            doc = [do_ref[rows, hs[h]] for h in heads]
            kds = [_dot(k_ref[rows, hs[h]], dsn[h]) for h in heads]
            pdo = [_dot_tn(pm_ref[h, c], doc[h]) for h in heads]
            dkd = [_dot_nt(vn_ref[rows, hs[h]], dsn[h]) for h in heads]
            qdo = [_dot_tn(q_ref[rows, hs[h]], doc[h]) for h in heads]
            dvn = [pdo[h] + kds[h] for h in heads]
            wdv = [_dot_tn(w_ref[rows, hs[h]], dvn[h]) for h in heads]
            for h in heads:
                dgl = jnp.sum(jnp.sum(dsn[h] * sall_ref[h, c].astype(F32), axis=0, keepdims=True), axis=1, keepdims=True)
                dgl_ref[h, pl.ds(c, 1), :] = jnp.broadcast_to(dgl, (1, C))
                dvn_ref[rows, hs[h]] = dvn[h]
                dkd_ref[rows, hs[h]] = dkd[h]
                dS[h] = dsn[h] * gl[h] + qdo[h] - wdv[h]
            return carry

        lax.fori_loop(0, cpt, chunk, 0)

    rev = lambda i: nT - 1 - i
    v_spec = pl.BlockSpec((tt, HG * GDN_DK), lambda g, i: (rev(i), g))
    rows_spec = pl.BlockSpec((HG, cpt, C), lambda g, i: (g, rev(i), 0))
    return pl.pallas_call(
        body, grid=(GDN_HV // HG, nT),
        in_specs=[v_spec, v_spec, v_spec, pl.BlockSpec((HG, cpt, C, C), lambda g, i: (g, rev(i), 0, 0)), v_spec,
                  pl.BlockSpec((HG, cpt, GDN_DK, GDN_DK), lambda g, i: (g, rev(i), 0, 0)), rows_spec, v_spec],
        out_specs=[v_spec, v_spec, rows_spec],
        out_shape=[jax.ShapeDtypeStruct((T, D_INNER), F32), jax.ShapeDtypeStruct((T, D_INNER), F32),
                   jax.ShapeDtypeStruct((GDN_HV, nC, C), F32)],
        scratch_shapes=[pltpu.VMEM((HG, GDN_DK, GDN_DK), F32)],
        compiler_params=_params("parallel", "arbitrary"), name=name,
    )(q, k, w, pm, vn, sall, gc, do)


def _gdn_local_bwd(q, k, v, gc, beta, tinv, u, w, pm, vn, sall, do, dvn, dkd, dgl, name):
    T = q.shape[0]
    C = GDN_CHUNK
    tt = min(T, 1024)
    cpt, nC = tt // C, T // C
    grp = min(cpt, GDN_LOCKSTEP_CHUNKS_BWD)

    def body(q_ref, k_ref, v_ref, gc_ref, b_ref, ti_ref, u_ref, w_ref, pm_ref, vn_ref, sall_ref, do_ref,
             dvn_ref, dkd_ref, dgl_ref, dq_ref, dk_ref, dv_ref, dg_ref, dbeta_ref, dgc_s):
        tri, strict, eye, r_i, c_i = _masks(C)
        lower = jnp.where(r_i >= c_i, 1.0, 0.0)
        lane = lax.broadcasted_iota(jnp.int32, (1, C), 1)
        rsum = lambda a: jnp.sum(a, axis=1, keepdims=True)
        for c0 in range(0, cpt, grp):
            cs = list(range(c0, c0 + grp))
            inst = [(c, hh) for c in cs for hh in range(2)]
            n = len(inst)
            rows = {c: slice(c * C, (c + 1) * C) for c in cs}
            hsl = [slice(hh * GDN_DK, (hh + 1) * GDN_DK) for c, hh in inst]
            qc = {c: q_ref[rows[c], :] for c in cs}
            kc = {c: k_ref[rows[c], :] for c in cs}
            kk = {c: _dot_nt(kc[c], kc[c]) for c in cs}
            gcr = [gc_ref[hh, c:c + 1, :] for c, hh in inst]
            gcc = [_col(r, eye) for r in gcr]
            bc = [_col(b_ref[hh, c:c + 1, :], eye) for c, hh in inst]
            lm = [jnp.exp(jnp.where(tri, cc - r, -1e30)) for cc, r in zip(gcc, gcr)]
            e_c = [jnp.exp(cc) for cc in gcc]
            el_c = [jnp.exp(r[:, C - 1:C] - cc) for cc, r in zip(gcc, gcr)]
            gl = [jnp.exp(r[:, C - 1:C]) for r in gcr]
            doc = [do_ref[rows[c], hsl[i]] for i, (c, hh) in enumerate(inst)]
            dvn = [dvn_ref[rows[c], hsl[i]] for i, (c, hh) in enumerate(inst)]
            sv = [sall_ref[hh, c] for c, hh in inst]
            aa = [_dot_nt(jnp.concatenate([_mx(doc[i]), _mx(dvn[i])], axis=0), sv[i]) for i in range(n)]
            dpm = [jnp.where(tri, _dot_nt(doc[i], vn_ref[rows[c], hsl[i]]), 0.0) for i, (c, hh) in enumerate(inst)]
            dqd = [a[:C] for a in aa]
            drhs = [_dot_tn(ti_ref[hh, c], jnp.concatenate([dvn[i], -aa[i][C:]], axis=1))
                    for i, (c, hh) in enumerate(inst)]
            sol = [jnp.concatenate([_mx(u_ref[rows[c], hsl[i]]), w_ref[rows[c], hsl[i]]], axis=1)
                   for i, (c, hh) in enumerate(inst)]
            dnm = [-jnp.where(strict, _dot_nt(drhs[i], sol[i]), 0.0) for i in range(n)]
            dkk = [dnm[i] * bc[i] * lm[i] for i in range(n)]
            dqk = [dpm[i] * lm[i] for i in range(n)]
            dq1 = [_dot(dqk[i], kc[c]) for i, (c, hh) in enumerate(inst)]
            dk1 = [_dot(dkk[i], kc[c]) for i, (c, hh) in enumerate(inst)]
            dk2 = [_dot_tn(dkk[i], kc[c]) for i, (c, hh) in enumerate(inst)]
            dk3 = [_dot_tn(dqk[i], qc[c]) for i, (c, hh) in enumerate(inst)]
            dq_acc = {c: jnp.zeros((C, GDN_DK), F32) for c in cs}
            dk_acc = {c: jnp.zeros((C, GDN_DK), F32) for c in cs}
            for i, (c, hh) in enumerate(inst):
                k_, q_, v_ = kc[c], qc[c], v_ref[rows[c], hsl[i]]
                dvb, dkbe = drhs[i][:, :GDN_DK], drhs[i][:, GDN_DK:]
                dkd = dkd_ref[rows[c], hsl[i]]
                kb = k_ * bc[i]
                dkb = dkbe * e_c[i]
                del_el = dkd * k_ * el_c[i]
                dbc = rsum(dnm[i] * kk[c] * lm[i]) + rsum(dkb * k_ + dvb * v_)
                dq_acc[c] = dq_acc[c] + dq1[i] + dqd[i] * e_c[i]
                dk_acc[c] = dk_acc[c] + dk1[i] + dk2[i] + dk3[i] + dkd * el_c[i] + dkb * bc[i]
                dv_ref[rows[c], hsl[i]] = dvb * bc[i]
                nm = jnp.where(strict, kk[c] * bc[i] * lm[i], 0.0)
                gm = dnm[i] * nm + dpm[i] * pm_ref[hh, c].astype(F32)
                dgc_col = rsum(gm) + rsum((dkbe * kb + dqd[i] * q_) * e_c[i] - del_el)
                dglast = (jnp.sum(jnp.sum(del_el, axis=0, keepdims=True), axis=1, keepdims=True)
                          + dgl_ref[hh, c:c + 1, 0:1] * gl[i])
                dgc_s[hh, c:c + 1, :] = (_row(dgc_col, eye) - jnp.sum(gm, axis=0, keepdims=True)
                                         + jnp.where(lane == C - 1, dglast, 0.0))
                dbeta_ref[hh, c:c + 1, :] = _row(dbc, eye)
            for c in cs:
                dq_ref[rows[c], :] = dq_acc[c]
                dk_ref[rows[c], :] = dk_acc[c]
        for hh in range(2):
            dg_ref[hh] = _dot_hi(dgc_s[hh], lower)

    rows_spec = pl.BlockSpec((2, cpt, C), lambda j, i: (j, i, 0))
    qk_spec = pl.BlockSpec((tt, GDN_DK), lambda j, i: (i, j))
    v_spec = pl.BlockSpec((tt, 2 * GDN_DK), lambda j, i: (i, j))
    cc_spec = pl.BlockSpec((2, cpt, C, C), lambda j, i: (j, i, 0, 0))
    rows_shape = jax.ShapeDtypeStruct((GDN_HV, nC, C), F32)
    return pl.pallas_call(
        body, grid=(GDN_HV // 2, T // tt),
        in_specs=[qk_spec, qk_spec, v_spec, rows_spec, rows_spec, cc_spec, v_spec, v_spec, cc_spec, v_spec,
                  pl.BlockSpec((2, cpt, GDN_DK, GDN_DK), lambda j, i: (j, i, 0, 0)), v_spec, v_spec, v_spec, rows_spec],
        out_specs=[qk_spec, qk_spec, v_spec, rows_spec, rows_spec],
        out_shape=[jax.ShapeDtypeStruct((T, GDN_HV // 2 * GDN_DK), F32),
                   jax.ShapeDtypeStruct((T, GDN_HV // 2 * GDN_DK), F32),
                   jax.ShapeDtypeStruct((T, D_INNER), F32), rows_shape, rows_shape],
        scratch_shapes=[pltpu.VMEM((2, cpt, C), F32)],
        compiler_params=_params("parallel", "parallel"), name=name,
    )(q, k, v, gc, beta, tinv, u, w, pm, vn, sall, do, dvn, dkd, dgl)


def _gdn_gate_bwd(araw, braw, dg, dbeta, alog, dtb, name):
    H, T = araw.shape

    def body(a_ref, b_ref, dg_ref, dbt_ref, alog_ref, dtb_ref, da_ref, db_ref, dalog_ref, ddtb_ref):
        xa = a_ref[...] + dtb_ref[...]
        ea = jnp.exp(alog_ref[...])
        dgv = dg_ref[...]
        da = -dgv * ea * _sigmoid(xa)
        da_ref[...] = da
        dalog_ref[...] = jnp.sum(-dgv * ea * _softplus(xa), axis=1, keepdims=True)
        ddtb_ref[...] = jnp.sum(da, axis=1, keepdims=True)
        bt = _sigmoid(b_ref[...])
        db_ref[...] = dbt_ref[...] * bt * (1.0 - bt)

    return pl.pallas_call(
        body,
        out_shape=[jax.ShapeDtypeStruct((H, T), F32), jax.ShapeDtypeStruct((H, T), F32),
                   jax.ShapeDtypeStruct((H, 1), F32), jax.ShapeDtypeStruct((H, 1), F32)],
        compiler_params=pltpu.CompilerParams(vmem_limit_bytes=VMEM_LIMIT_BYTES), name=name,
    )(araw, braw, dg, dbeta, alog, dtb)


SSD_LOCKSTEP_CHUNKS = 2
SSD_LOCKSTEP_CHUNKS_BWD = 1
SSD_LOCKSTEP_HEADS_BWD = 2


def _ssd_scan_fwd(xs, bm, cm, dtraw, alog, dtb, dskip, name):
    T = xs.shape[0]
    Q = SSD_CHUNK
    tt = min(T, 1024)
    cpt, nC = tt // Q, T // Q
    GW = SSD_R * SSD_P

    def body(alog_ref, dtb_ref, dsk_ref, xs_ref, b_ref, c_ref, dt_ref, y_ref, sall_ref, dto_ref, S, dt_s, acs_s):
        gi, i = pl.program_id(0), pl.program_id(1)

        @pl.when(i == 0)
        def _():
            S[...] = jnp.zeros_like(S)

        tri, _, eye, r_i, c_i = _masks(Q)
        upper = jnp.where(r_i <= c_i, 1.0, 0.0)
        for r in range(SSD_R):
            h = SSD_R * gi + r
            dt = _softplus(dt_ref[r] + dtb_ref[h])
            dto_ref[r] = dt
            dt_s[r] = dt
            acs_s[r] = _dot_hi(-jnp.exp(alog_ref[h]) * dt, upper)

        ps = [slice(r * SSD_P, (r + 1) * SSD_P) for r in range(SSD_R)]
        s_cur = [S[:, ps[r]] for r in range(SSD_R)]
        grp = min(cpt, SSD_LOCKSTEP_CHUNKS)
        for c0 in range(0, cpt, grp):
            cs = list(range(c0, c0 + grp))
            inst = [(c, r) for c in cs for r in range(SSD_R)]
            rows = {c: slice(c * Q, (c + 1) * Q) for c in cs}
            bc_ = {c: b_ref[rows[c], :] for c in cs}
            cc_ = {c: c_ref[rows[c], :] for c in cs}
            cb = {c: _dot_nt(cc_[c], bc_[c]) for c in cs}
            xr = [xs_ref[rows[c], ps[r]] for c, r in inst]
            acr = [acs_s[r, c:c + 1, :] for c, r in inst]
            acc = [_col_bcast(a, Q) for a in acr]
            dtr = [dt_s[r, c:c + 1, :] for c, r in inst]
            mm = [cb[c] * (jnp.exp(jnp.where(tri, acc[i] - acr[i], -1e30)) * dtr[i]) for i, (c, r) in enumerate(inst)]
            bct = {c: bc_[c].T for c in cs}
            st = [_dot(bct[c] * (jnp.exp(acr[i][:, Q - 1:Q] - acr[i]) * dtr[i]), xr[i]) for i, (c, r) in enumerate(inst)]
            yd = [_dot(mm[i], xr[i]) for i in range(len(inst))]
            s_prev = []
            for i, (c, r) in enumerate(inst):
                s_prev.append(s_cur[r])
                s_cur[r] = s_cur[r] * jnp.exp(acr[i][:, Q - 1:Q]) + st[i]
            yo = [_dot(cc_[c] * jnp.exp(acc[i]), s_prev[i]) for i, (c, r) in enumerate(inst)]
            for i, (c, r) in enumerate(inst):
                sall_ref[0, c, :, ps[r]] = s_prev[i]
                y_ref[rows[c], ps[r]] = yd[i] + yo[i] + dsk_ref[SSD_R * gi + r] * xr[i]
        for r in range(SSD_R):
            S[:, ps[r]] = s_cur[r]

    smem = pl.BlockSpec(memory_space=pltpu.SMEM)
    rows_spec = pl.BlockSpec((SSD_R, cpt, Q), lambda g, i: (g, i, 0))
    return pl.pallas_call(
        body, grid=(SSD_G, T // tt),
        in_specs=[smem, smem, smem,
                  pl.BlockSpec((tt, GW), lambda g, i: (i, g)), pl.BlockSpec((tt, SSD_N), lambda g, i: (i, g)),
                  pl.BlockSpec((tt, SSD_N), lambda g, i: (i, g)), rows_spec],
        out_specs=[pl.BlockSpec((tt, GW), lambda g, i: (i, g)),
                   pl.BlockSpec((1, cpt, SSD_N, GW), lambda g, i: (g, i, 0, 0)), rows_spec],
        out_shape=[jax.ShapeDtypeStruct((T, D_INNER), F32), jax.ShapeDtypeStruct((SSD_G, nC, SSD_N, GW), F32),
                   jax.ShapeDtypeStruct((SSD_H, nC, Q), F32)],
        scratch_shapes=[pltpu.VMEM((SSD_N, GW), F32), pltpu.VMEM((SSD_R, cpt, Q), F32),
                        pltpu.VMEM((SSD_R, cpt, Q), F32)],
        compiler_params=_params("parallel", "arbitrary"), name=name,
    )(alog, dtb, dskip, xs, bm, cm, dtraw)


def _ssd_scan_bwd(xs, bm, cm, dt, sall, dy, alog, dskip, name):
    T = xs.shape[0]
    Q = SSD_CHUNK
    tt = min(T, 1024)
    cpt, nC, nT = tt // Q, T // Q, T // tt
    GW = SSD_R * SSD_P

    def body(alog_ref, dsk_ref, xs_ref, b_ref, c_ref, dt_ref, sall_ref, dy_ref,
             dxs_ref, db_ref, dc_ref, da_ref, ddt_ref, dd_ref, dS, acs_s, dacs_s, ddt_s, dd_s):
        gi, i = pl.program_id(0), pl.program_id(1)

        @pl.when(i == 0)
        def _():
            dS[...] = jnp.zeros_like(dS)

        tri, _, eye, r_i, c_i = _masks(Q)
        upper = jnp.where(r_i <= c_i, 1.0, 0.0)
        lower = jnp.where(r_i >= c_i, 1.0, 0.0)
        lane = lax.broadcasted_iota(jnp.int32, (1, Q), 1)
        for r in range(SSD_R):
            acs_s[r] = _dot_hi(-jnp.exp(alog_ref[SSD_R * gi + r]) * dt_ref[r], upper)

        ps = [slice(r * SSD_P, (r + 1) * SSD_P) for r in range(SSD_R)]
        ds_cur = [dS[:, ps[r]] for r in range(SSD_R)]
        grp = min(cpt, SSD_LOCKSTEP_CHUNKS_BWD)
        csum = lambda a: jnp.sum(a, axis=0, keepdims=True)
        tsum = lambda a: jnp.sum(csum(a), axis=1, keepdims=True)
        ones8 = jnp.ones((8, SSD_P), F32)
        for c0 in range(cpt - grp, -1, -grp):
            cs = list(range(c0 + grp - 1, c0 - 1, -1))
            rows = {c: slice(c * Q, (c + 1) * Q) for c in cs}
            bc_ = {c: b_ref[rows[c], :] for c in cs}
            cc_ = {c: c_ref[rows[c], :] for c in cs}
            cb = {c: _dot_nt(cc_[c], bc_[c]) for c in cs}
            cbt = {c: _dot_nt(bc_[c], cc_[c]) for c in cs}
            bct = {c: bc_[c].T for c in cs}
            cct = {c: cc_[c].T for c in cs}
            dcb = {c: jnp.zeros((Q, Q), F32) for c in cs}
            dcbt = {c: jnp.zeros((Q, Q), F32) for c in cs}
            db_acc = {c: jnp.zeros((Q, SSD_N), F32) for c in cs}
            dc_acc = {c: jnp.zeros((Q, SSD_N), F32) for c in cs}
            for h0 in range(0, SSD_R, SSD_LOCKSTEP_HEADS_BWD):
                inst = [(c, r) for c in cs for r in range(h0, h0 + SSD_LOCKSTEP_HEADS_BWD)]
                n = len(inst)
                xr = [xs_ref[rows[c], ps[r]] for c, r in inst]
                dyr = [dy_ref[rows[c], ps[r]] for c, r in inst]
                acr = [acs_s[r, c:c + 1, :] for c, r in inst]
                dtr = [dt_ref[r, c:c + 1, :] for c, r in inst]
                acc = [_col_bcast(a, Q) for a in acr]
                dtb = [_col_bcast(d, Q) for d in dtr]
                al = [a[:, Q - 1:Q] for a in acr]
                e_c = [jnp.exp(a) for a in acc]
                dl_c = [jnp.exp(al[i] - acc[i]) for i in range(n)]
                e_r = [jnp.exp(a) for a in acr]
                dl_r = [jnp.exp(al[i] - acr[i]) for i in range(n)]
                gl = [jnp.exp(a) for a in al]
                lm = [jnp.exp(jnp.where(tri, acc[i] - acr[i], -1e30)) for i in range(n)]
                lmt = [jnp.exp(jnp.where(r_i <= c_i, acr[i] - acc[i], -1e30)) for i in range(n)]
                mmt = [cbt[c] * lmt[i] for i, (c, r) in enumerate(inst)]
                sr = [sall_ref[0, c, :, ps[r]] for c, r in inst]
                dmm0 = [_dot_nt(dyr[i], xr[i]) for i in range(n)]
                dmm0t = [_dot_nt(xr[i], dyr[i]) for i in range(n)]
                dxd1 = [_dot(mmt[i], dyr[i]) for i in range(n)]
                dce = [_dot_nt(dyr[i], sr[i]) for i in range(n)]
                dcet = [_dot_nt(sr[i], dyr[i]) for i in range(n)]
                cdy = [_dot(cct[c] * e_r[i], dyr[i]) for i, (c, r) in enumerate(inst)]
                dsn = []
                for i, (c, r) in enumerate(inst):
                    dsn.append(ds_cur[r])
                    ds_cur[r] = gl[i] * ds_cur[r] + cdy[i]
                dxd = [dxd1[i] + _dot(bc_[c] * dl_c[i], dsn[i]) for i, (c, r) in enumerate(inst)]
                dbd0 = [_dot_nt(xr[i], dsn[i]) for i in range(n)]
                dbd0t = [_dot_nt(dsn[i], xr[i]) for i in range(n)]
                for i, (c, r) in enumerate(inst):
                    dgl = tsum(dsn[i] * sr[i])
                    dc_acc[c] = dc_acc[c] + dce[i] * e_c[i]
                    db_acc[c] = db_acc[c] + dbd0[i] * (dtb[i] * dl_c[i])
                    dl0 = dmm0[i] * lm[i]
                    dl0t = dmm0t[i] * (lmt[i] * dtb[i])
                    dcb[c] = dcb[c] + dl0 * dtr[i]
                    dcbt[c] = dcbt[c] + dl0t
                    csum_gm0 = csum(dl0 * cb[c])
                    rsum_gm = csum(dl0t * cbt[c])
                    r_de = csum(dcet[i] * cct[c]) * e_r[i]
                    r_dl = csum(dbd0t[i] * bct[c]) * dl_r[i]
                    dalast = jnp.sum(r_dl * dtr[i], axis=1, keepdims=True) + dgl * gl[i]
                    dacs_s[r, c:c + 1, :] = (rsum_gm + r_de - (r_dl + csum_gm0) * dtr[i]
                                             + jnp.where(lane == Q - 1, dalast, 0.0))
                    ddt_s[r, c:c + 1, :] = csum_gm0 + r_dl
                    dd_s[r, c:c + 1, :] = _dot_nt(ones8, dyr[i] * xr[i])[0:1]
                    dxs_ref[rows[c], ps[r]] = dxd[i] * dtb[i][:, :SSD_P] + dsk_ref[SSD_R * gi + r] * dyr[i]
            for c in cs:
                dc_ref[rows[c], :] = dc_acc[c] + _dot(dcb[c], bc_[c])
                db_ref[rows[c], :] = db_acc[c] + _dot(dcbt[c], cc_[c])
        for r in range(SSD_R):
            dS[:, ps[r]] = ds_cur[r]
        for r in range(SSD_R):
            da_ref[r] = _dot_hi(dacs_s[r], lower)
            ddt_ref[r] = ddt_s[r]
            dd_ref[r] = dd_s[r]

    rev = lambda i: nT - 1 - i
    smem = pl.BlockSpec(memory_space=pltpu.SMEM)
    rows_spec = pl.BlockSpec((SSD_R, cpt, Q), lambda g, i: (g, rev(i), 0))
    x_spec = pl.BlockSpec((tt, GW), lambda g, i: (rev(i), g))
    n_spec = pl.BlockSpec((tt, SSD_N), lambda g, i: (rev(i), g))
    rows_shape = jax.ShapeDtypeStruct((SSD_H, nC, Q), F32)
    return pl.pallas_call(
        body, grid=(SSD_G, nT),
        in_specs=[smem, smem, x_spec, n_spec, n_spec, rows_spec,
                  pl.BlockSpec((1, cpt, SSD_N, GW), lambda g, i: (g, rev(i), 0, 0)), x_spec],
        out_specs=[x_spec, n_spec, n_spec, rows_spec, rows_spec, rows_spec],
        out_shape=[jax.ShapeDtypeStruct((T, D_INNER), F32), jax.ShapeDtypeStruct((T, SSD_G * SSD_N), F32),
                   jax.ShapeDtypeStruct((T, SSD_G * SSD_N), F32), rows_shape, rows_shape, rows_shape],
        scratch_shapes=[pltpu.VMEM((SSD_N, GW), F32)] + [pltpu.VMEM((SSD_R, cpt, Q), F32)] * 4,
        compiler_params=_params("parallel", "arbitrary"), name=name,
    )(alog, dskip, xs, bm, cm, dt, sall, dy)


def _ssd_gate_bwd(dtraw, dt, da, ddt_direct, ddrow, alog, dtb, name):
    H, T = dtraw.shape

    def body(raw_ref, dt_ref, da_ref, ddt_ref, dd_ref, alog_ref, dtb_ref, draw_ref, dalog_ref, ddtb_ref, dD_ref):
        a = -jnp.exp(alog_ref[...])
        dav = da_ref[...]
        ddt = ddt_ref[...] + dav * a
        draw = ddt * _sigmoid(raw_ref[...] + dtb_ref[...])
        draw_ref[...] = draw
        dalog_ref[...] = jnp.sum(dav * dt_ref[...], axis=1, keepdims=True) * a
        ddtb_ref[...] = jnp.sum(draw, axis=1, keepdims=True)
        dD_ref[...] = jnp.sum(dd_ref[...], axis=1, keepdims=True)

    return pl.pallas_call(
        body,
        out_shape=[jax.ShapeDtypeStruct((H, T), F32)] + [jax.ShapeDtypeStruct((H, 1), F32)] * 3,
        compiler_params=pltpu.CompilerParams(vmem_limit_bytes=VMEM_LIMIT_BYTES), name=name,
    )(dtraw, dt, da, ddt_direct, ddrow, alog, dtb)


def _final_loss(x, fw, tgt, name):
    T = x.shape[0]
    tt = min(T, 512)
    nT = T // tt

    def body(x_ref, w_ref, t_ref, dx_ref, dw_ref, loss_ref, acc):
        i = pl.program_id(0)

        @pl.when(i == 0)
        def _():
            dw_ref[...] = jnp.zeros_like(dw_ref)
            acc[...] = jnp.zeros_like(acc)

        xv = x_ref[...]
        r = lax.rsqrt(jnp.mean(xv * xv, axis=-1, keepdims=True) + EPS)
        xh = xv * r
        err = xh * w_ref[...] - t_ref[...]
        acc[...] += jnp.sum(err * err, axis=0, keepdims=True)
        dout = err * (1.0 / D_MODEL)
        dw_ref[...] += jnp.sum(dout * xh, axis=0, keepdims=True)
        dxn = dout * w_ref[...]
        dx_ref[...] = r * (dxn - xh * jnp.mean(dxn * xh, axis=-1, keepdims=True))

        @pl.when(i == nT - 1)
        def _():
            loss_ref[...] = (0.5 / D_MODEL) * jnp.sum(acc[...], axis=1, keepdims=True)

    row = pl.BlockSpec((tt, D_MODEL), lambda i: (i, 0))
    vec = pl.BlockSpec((1, D_MODEL), lambda i: (0, 0))
    return pl.pallas_call(
        body, grid=(nT,),
        in_specs=[row, vec, row],
        out_specs=[row, vec, pl.BlockSpec((1, 1), lambda i: (0, 0))],
        out_shape=[jax.ShapeDtypeStruct((T, D_MODEL), F32), jax.ShapeDtypeStruct((1, D_MODEL), F32),
                   jax.ShapeDtypeStruct((1, 1), F32)],
        scratch_shapes=[pltpu.VMEM((1, D_MODEL), F32)],
        compiler_params=_params("arbitrary"), name=name,
    )(x, fw, tgt)


def _adamw(parts, w, m, v, name):
    R, C = w.shape
    tr = 128 if R % 128 == 0 else R

    def body(p_ref, w_ref, m_ref, v_ref, g_ref, d_ref, nm_ref, nv_ref):
        g = p_ref[0].astype(F32)
        for s in range(1, N_DEV):
            g = g + p_ref[s].astype(F32)
        mn = ADAM_B1 * m_ref[...] + (1.0 - ADAM_B1) * g
        vn = ADAM_B2 * v_ref[...] + (1.0 - ADAM_B2) * (g * g)
        mh = mn / (1.0 - ADAM_B1 ** ADAM_STEP)
        vh = vn / (1.0 - ADAM_B2 ** ADAM_STEP)
        g_ref[...] = g
        d_ref[...] = -ADAM_LR * (mh / (jnp.sqrt(vh) + ADAM_EPS) + ADAM_WD * w_ref[...])
        nm_ref[...] = mn
        nv_ref[...] = vn

    blk = pl.BlockSpec((tr, C), lambda i: (i, 0))
    return pl.pallas_call(
        body, grid=(R // tr,),
        in_specs=[pl.BlockSpec((N_DEV, tr, C), lambda i: (0, i, 0)), blk, blk, blk],
        out_specs=[blk] * 4,
        out_shape=[jax.ShapeDtypeStruct((R, C), F32)] * 4,
        compiler_params=_params("parallel"), name=name,
    )(parts, w, m, v)


def _me():
    x, y, c = lax.axis_index("x"), lax.axis_index("y"), lax.axis_index("c")
    return x, y, c


def _peer(d):
    x, y, c = _me()
    px = 1 - x if d & 4 else x
    py = 1 - y if d & 2 else y
    pc = 1 - c if d & 1 else c
    return (px, py, pc), 4 * px + 2 * py + pc


def _gather_two_level(arrs, name):
    n = len(arrs)

    def body(*refs):
        ins, outs = refs[:n], refs[n:2 * n]
        ssem, rsem, lsem = refs[2 * n:]
        x, y, c = _me()
        me, sibling = (x, y, c), (x, y, 1 - c)
        chips = [(1 - x, y), (x, 1 - y), (1 - x, 1 - y)]

        def slot(a, block):
            px, py, pc = block
            return outs[a].at[4 * px + 2 * py + pc]

        def copy(a, k, block, to, src=None):
            return pltpu.make_async_remote_copy(
                src_ref=slot(a, block) if src is None else src, dst_ref=slot(a, block),
                send_sem=ssem.at[a, k], recv_sem=rsem.at[a, k], device_id=to, device_id_type=MESH)

        mine = [pltpu.make_async_copy(ins[a], slot(a, me), lsem.at[a]) for a in range(n)]
        for cp in mine:
            cp.start()
        first = []
        for a in range(n):
            first.append(copy(a, 0, me, sibling, src=ins[a]))
            first += [copy(a, 1 + j, me, (*chip, c), src=ins[a]) for j, chip in enumerate(chips)]
        for cp in first:
            cp.start()
        passed = [[copy(a, 4 + j, (*chip, c), sibling) for j, chip in enumerate(chips)] for a in range(n)]
        for j, chip in enumerate(chips):
            for a in range(n):
                copy(a, 1 + j, (*chip, c), me).wait_recv()
                passed[a][j].start()
        for a in range(n):
            copy(a, 0, sibling, me).wait_recv()
            for j, chip in enumerate(chips):
                copy(a, 4 + j, (*chip, 1 - c), me).wait_recv()
        for cp in first + [cp for row in passed for cp in row]:
            cp.wait_send()
        for cp in mine:
            cp.wait()

    anyspec = pl.BlockSpec(memory_space=pl.ANY)
    return pl.pallas_call(
        body,
        in_specs=[anyspec] * n, out_specs=[anyspec] * n,
        out_shape=_exchange_out_shapes(arrs, [True] * n),
        scratch_shapes=_exchange_semaphores(n),
        name=name,
    )(*arrs)


def _exchange(arrs, bcast, name):
    n = len(arrs)

    def body(*refs):
        ex = _Exchange(refs[:n], refs[n:2 * n], bcast, *refs[2 * n:])
        ex.begin()
        ex.finish()

    anyspec = pl.BlockSpec(memory_space=pl.ANY)
    return pl.pallas_call(
        body,
        in_specs=[anyspec] * n, out_specs=[anyspec] * n,
        out_shape=_exchange_out_shapes(arrs, bcast),
        scratch_shapes=_exchange_semaphores(n),
        name=name,
    )(*arrs)


def _exchange_out_shapes(arrs, bcast):
    return [jax.ShapeDtypeStruct((N_DEV,) + (a.shape if b else a.shape[1:]), a.dtype) for a, b in zip(arrs, bcast)]


def _exchange_semaphores(n):
    return [pltpu.SemaphoreType.DMA((n, N_DEV - 1)), pltpu.SemaphoreType.DMA((n, N_DEV - 1)),
            pltpu.SemaphoreType.DMA((n,))]


class _Exchange:
    def __init__(self, ins, outs, bcast, ssem, rsem, lsem):
        n = len(ins)
        x, y, c = _me()
        me = 4 * x + 2 * y + c

        def src(a, dest):
            return ins[a] if bcast[a] else ins[a].at[dest]

        self.local = [pltpu.make_async_copy(src(a, me), outs[a].at[me], lsem.at[a]) for a in range(n)]
        self.sends, self.recvs = [], []
        for a in range(n):
            for d in range(1, N_DEV):
                peer, pid = _peer(d)
                self.sends.append(pltpu.make_async_remote_copy(
                    src_ref=src(a, pid), dst_ref=outs[a].at[me], send_sem=ssem.at[a, d - 1],
                    recv_sem=rsem.at[a, d - 1], device_id=peer, device_id_type=MESH))
                self.recvs.append(pltpu.make_async_remote_copy(
                    src_ref=src(a, pid), dst_ref=outs[a].at[pid], send_sem=ssem.at[a, d - 1],
                    recv_sem=rsem.at[a, d - 1], device_id=peer, device_id_type=MESH))

    def begin(self):
        for cp in self.local + self.sends:
            cp.start()

    def finish(self):
        for cp in self.recvs:
            cp.wait_recv()
        for cp in self.sends:
            cp.wait_send()
        for cp in self.local:
            cp.wait()


def _to_rows(cols, chunk):
    T, H = cols.shape
    return cols.T.reshape(H, T // chunk, chunk)


def _from_rows(rows):
    return rows.T


def _pad_cols(a, width):
    return jnp.pad(a, ((0, 0), (0, width - a.shape[1])))


def _local_step(x, tgt, p, late_weights=None, early_grads=None, late_grads=None):
    T = x.shape[0]
    zb = lambda n: jnp.zeros((1, n), F32)
    gw = p["gdn_w_in"]
    g_wparts = [gw[:, 0:1024], gw[:, 1024:2048], gw[:, 2048:4096], gw[:, 4096:6144], _pad_cols(gw[:, 6144:6176], PAD_W)]
    nw0, nw1 = p["norm_w"][0:1], p["norm_w"][1:2]
    gcw = p["gdn_conv_w"]
    cw_q, cw_k, cw_v = gcw[:, 0:1024], gcw[:, 1024:2048], gcw[:, 2048:4096]
    g_convs = [(cw_q, zb(1024), True, GDN_DK ** -0.5), (cw_k, zb(1024), True, 1.0), (cw_v, zb(2048), False, 1.0),
               None, None]
    if late_weights is None:
        h0, (q_pre, k_pre, v_pre, z0, ab), (q, k, v), g_cpre = _norm_inproj(x, nw0, g_wparts, g_convs, "gdn_inproj")
    else:
        comm, assemble = late_weights
        h0, (q_pre, k_pre, v_pre, z0, ab), (q, k, v), g_cpre, gathered = _norm_inproj(x, nw0, g_wparts, g_convs,
                                                                                      "gdn_inproj", comm)
        p = dict(p, **assemble(gathered))
    braw = _to_rows(ab[:, 0:GDN_HV], GDN_CHUNK)
    araw = _to_rows(ab[:, GDN_HV:2 * GDN_HV], GDN_CHUNK)
    g_alog, g_dtb = p["gdn_a_log"].reshape(-1), p["gdn_dt_bias"].reshape(-1)
    g_u, g_w, g_pm, g_ti, g_rows, beta_rows, gc_rows, g_qd, g_kd = _gdn_prep(q, k, v, araw, braw, g_alog, g_dtb,
                                                                             "gdn_prep")
    o0, g_vn, g_sall = _gdn_state_fwd(g_qd, g_kd, g_u, g_w, g_pm, gc_rows, "gdn_state_fwd")
    x1 = _out_fwd(o0, z0, p["gdn_norm_w"], p["gdn_w_out"], x, GDN_DK, False, "gdn_out")
    sw = p["ssd_w_in"]
    s_wparts = [sw[:, 0:2048], sw[:, 2048:4096], sw[:, 4096:5120], sw[:, 5120:6144], _pad_cols(sw[:, 6144:6176], PAD_W)]
    scw, scb = p["ssd_conv_w"], p["ssd_conv_b"]
    s_convs = [None, (scw[:, 0:2048], scb[:, 0:2048], False, 1.0), (scw[:, 2048:3072], scb[:, 2048:3072], False, 1.0),
               (scw[:, 3072:4096], scb[:, 3072:4096], False, 1.0), None]
    h1, (z1, xs_pre, b_pre, c_pre, dtp), (xs, bm, cm), s_cpre = _norm_inproj(x1, nw1, s_wparts, s_convs, "ssd_inproj")
    dtraw = _to_rows(dtp[:, 0:SSD_H], SSD_CHUNK)
    s_alog, s_dtb, s_d = p["ssd_a_log"].reshape(-1), p["ssd_dt_bias"].reshape(-1), p["ssd_d"].reshape(-1)
    y1, s_sall, dt_rows = _ssd_scan_fwd(xs, bm, cm, dtraw, s_alog, s_dtb, s_d, "ssd_scan_fwd")
    x2 = _out_fwd(y1, z1, p["ssd_norm_w"], p["ssd_w_out"], x1, D_INNER // SSD_G, True, "ssd_out")
    dx2, d_fw, loss = _final_loss(x2, p["final_norm_w"].reshape(1, -1), tgt, "final_loss")
    dy1, dz1, d_snw, yn1 = _out_bwd(dx2, y1, z1, p["ssd_norm_w"], p["ssd_w_out"], D_INNER // SSD_G, True, "ssd_out_bwd")
    d_swout = _matmul_tn(yn1, dx2, "ssd_wout_grad")
    dxs, dbm, dcm, da_rows, ddt_rows, dd_rows = _ssd_scan_bwd(xs, bm, cm, dt_rows, s_sall, dy1, s_alog, s_d, "ssd_scan_bwd")
    col = lambda a: a.reshape(-1, 1)
    dtraw_g, d_salog, d_sdtb, d_sd = _ssd_gate_bwd(
        dtraw.reshape(SSD_H, T), dt_rows.reshape(SSD_H, T), da_rows.reshape(SSD_H, T),
        ddt_rows.reshape(SSD_H, T), dd_rows.reshape(SSD_H, T), col(s_alog), col(s_dtb), "ssd_gate_bwd")
    dxs_pre, dcw_x, dcb_x = _conv_bwd(xs_pre, s_cpre[0], scw[:, 0:2048], dxs, False, 1.0, "ssd_conv_x_bwd")
    db_pre, dcw_b, dcb_b = _conv_bwd(b_pre, s_cpre[1], scw[:, 2048:3072], dbm, False, 1.0, "ssd_conv_b_bwd")
    dc_pre, dcw_c, dcb_c = _conv_bwd(c_pre, s_cpre[2], scw[:, 3072:4096], dcm, False, 1.0, "ssd_conv_c_bwd")
    ddtp = _pad_cols(_from_rows(dtraw_g), PAD_W)
    s_dparts = [dz1, dxs_pre, db_pre, dc_pre, ddtp]
    dx1, d_nw1 = _inproj_bwd(x1, nw1, s_dparts, s_wparts, dx2, "ssd_inproj_bwd")
    s_dw = [_matmul_tn(h1, d, "ssd_win_grad_%d" % n) for n, d in enumerate(s_dparts)]
    d_swin = jnp.concatenate(s_dw[:4] + [s_dw[4][:, 0:SSD_H]], axis=1)
    early_recv = None
    if early_grads is None:
        do0, dz0, d_gnw, yn0 = _out_bwd(dx1, o0, z0, p["gdn_norm_w"], p["gdn_w_out"], GDN_DK, False, "gdn_out_bwd")
    else:
        do0, dz0, d_gnw, yn0, early_recv = _out_bwd(dx1, o0, z0, p["gdn_norm_w"], p["gdn_w_out"], GDN_DK, False,
                                                    "gdn_out_bwd", early_grads(d_swin, d_swout))
    d_gwout = _matmul_tn(yn0, dx1, "gdn_wout_grad")
    g_dvn, g_dkd, g_dgl = _gdn_state_bwd(g_qd, g_kd, g_w, g_pm, g_vn, g_sall, gc_rows, do0, "gdn_state_bwd")
    dq, dk, dv, dg_rows, dbeta_rows = _gdn_local_bwd(q, k, v, gc_rows, beta_rows, g_ti, g_u, g_w, g_pm, g_vn, g_sall,
                                                     do0, g_dvn, g_dkd, g_dgl, "gdn_local_bwd")
    da_g, db_g, d_galog, d_gdtb = _gdn_gate_bwd(
        araw.reshape(GDN_HV, T), braw.reshape(GDN_HV, T), dg_rows.reshape(GDN_HV, T),
        dbeta_rows.reshape(GDN_HV, T), col(g_alog), col(g_dtb), "gdn_gate_bwd")
    dq_pre, dcw_q, _ = _conv_bwd(q_pre, g_cpre[0], cw_q, dq, True, GDN_DK ** -0.5, "gdn_conv_q_bwd")
    dk_pre, dcw_k, _ = _conv_bwd(k_pre, g_cpre[1], cw_k, dk, True, 1.0, "gdn_conv_k_bwd")
    dv_pre, dcw_v, _ = _conv_bwd(v_pre, g_cpre[2], cw_v, dv, False, 1.0, "gdn_conv_v_bwd")
    dab = _pad_cols(jnp.concatenate([_from_rows(db_g), _from_rows(da_g)], axis=1), PAD_W)
    g_dparts = [dq_pre, dk_pre, dv_pre, dz0, dab]
    g_dw = [_matmul_tn(h0, d, "gdn_win_grad_%d" % n) for n, d in enumerate(g_dparts)]
    d_gwin = jnp.concatenate(g_dw[:4] + [g_dw[4][:, 0:2 * GDN_HV]], axis=1)
    sharded_grads = {
        "gdn_w_in": d_gwin, "gdn_w_out": d_gwout,
        "gdn_conv_w": jnp.concatenate([dcw_q, dcw_k, dcw_v], axis=1),
        "ssd_conv_w": jnp.concatenate([dcw_x, dcw_b, dcw_c], axis=1),
        "ssd_conv_b": jnp.concatenate([dcb_x, dcb_b, dcb_c], axis=1), "ssd_norm_w": d_snw}
    late_recv = None
    if late_grads is None:
        dx0, d_nw0 = _inproj_bwd(x, nw0, g_dparts, g_wparts, dx1, "gdn_inproj_bwd")
    else:
        dx0, d_nw0, late_recv = _inproj_bwd(x, nw0, g_dparts, g_wparts, dx1, "gdn_inproj_bwd",
                                            late_grads(sharded_grads))
    grads = {
        "norm_w": jnp.concatenate([d_nw0, d_nw1], axis=0),
        "gdn_w_in": d_gwin,
        "gdn_conv_w": jnp.concatenate([dcw_q, dcw_k, dcw_v], axis=1),
        "gdn_a_log": d_galog.reshape(1, -1),
        "gdn_dt_bias": d_gdtb.reshape(1, -1),
        "gdn_norm_w": d_gnw,
        "gdn_w_out": d_gwout,
        "ssd_w_in": d_swin,
        "ssd_conv_w": jnp.concatenate([dcw_x, dcw_b, dcw_c], axis=1),
        "ssd_conv_b": jnp.concatenate([dcb_x, dcb_b, dcb_c], axis=1),
        "ssd_dt_bias": d_sdtb.reshape(1, -1),
        "ssd_a_log": d_salog.reshape(1, -1),
        "ssd_d": d_sd.reshape(1, -1),
        "ssd_norm_w": d_snw,
        "ssd_w_out": d_swout,
        "final_norm_w": d_fw,
    }
    if early_grads is not None:
        return loss, dx0, grads, early_recv, late_recv
    return loss, dx0, grads


WEIGHTS = ["norm_w", "gdn_w_in", "gdn_conv_w", "gdn_a_log", "gdn_dt_bias", "gdn_norm_w", "gdn_w_out", "ssd_w_in",
           "ssd_conv_w", "ssd_conv_b", "ssd_dt_bias", "ssd_a_log", "ssd_d", "ssd_norm_w", "ssd_w_out", "final_norm_w"]
COL_SHARDED = ["gdn_w_in", "ssd_w_in"]
ROW_SHARDED = ["gdn_w_out", "ssd_w_out"]
SMALL_SHARDED = ["gdn_conv_w", "ssd_conv_w", "ssd_conv_b", "ssd_norm_w"]
REPLICATED = ["norm_w", "gdn_a_log", "gdn_dt_bias", "gdn_norm_w", "ssd_dt_bias", "ssd_a_log", "ssd_d", "final_norm_w"]


def _pack(arrs):
    return jnp.concatenate([a.reshape(-1) for a in arrs]).reshape(1, -1)


def _unpack(flat, shapes):
    out, pos = [], 0
    for s in shapes:
        n = 1
        for dim in s:
            n *= dim
        out.append(flat[pos:pos + n].reshape(s))
        pos += n
    return out


def _cols_to_shards(full):
    R, C = full.shape
    return full.reshape(R, N_DEV, C // N_DEV).transpose(1, 0, 2)


def _shards_to_cols(shards):
    n, R, c = shards.shape
    return shards.transpose(1, 0, 2).reshape(R, n * c)


def kernel(x, norm_w, gdn_w_in, gdn_conv_w, gdn_a_log, gdn_dt_bias, gdn_norm_w, gdn_w_out, ssd_w_in, ssd_conv_w, ssd_conv_b, ssd_dt_bias, ssd_a_log, ssd_d, ssd_norm_w, ssd_w_out, final_norm_w, loss_target, m_norm_w, m_gdn_w_in, m_gdn_conv_w, m_gdn_a_log, m_gdn_dt_bias, m_gdn_norm_w, m_gdn_w_out, m_ssd_w_in, m_ssd_conv_w, m_ssd_conv_b, m_ssd_dt_bias, m_ssd_a_log, m_ssd_d, m_ssd_norm_w, m_ssd_w_out, m_final_norm_w, v_norm_w, v_gdn_w_in, v_gdn_conv_w, v_gdn_a_log, v_gdn_dt_bias, v_gdn_norm_w, v_gdn_w_out, v_ssd_w_in, v_ssd_conv_w, v_ssd_conv_b, v_ssd_dt_bias, v_ssd_a_log, v_ssd_d, v_ssd_norm_w, v_ssd_w_out, v_final_norm_w):
    w = dict(norm_w=norm_w, gdn_w_in=gdn_w_in[0], gdn_conv_w=gdn_conv_w[0], gdn_a_log=gdn_a_log,
             gdn_dt_bias=gdn_dt_bias, gdn_norm_w=gdn_norm_w, gdn_w_out=gdn_w_out[0], ssd_w_in=ssd_w_in[0],
             ssd_conv_w=ssd_conv_w[0], ssd_conv_b=ssd_conv_b, ssd_dt_bias=ssd_dt_bias, ssd_a_log=ssd_a_log,
             ssd_d=ssd_d, ssd_norm_w=ssd_norm_w, ssd_w_out=ssd_w_out[0], final_norm_w=final_norm_w.reshape(1, -1))
    m = dict(norm_w=m_norm_w, gdn_w_in=m_gdn_w_in[0], gdn_conv_w=m_gdn_conv_w[0], gdn_a_log=m_gdn_a_log,
             gdn_dt_bias=m_gdn_dt_bias, gdn_norm_w=m_gdn_norm_w, gdn_w_out=m_gdn_w_out[0], ssd_w_in=m_ssd_w_in[0],
             ssd_conv_w=m_ssd_conv_w[0], ssd_conv_b=m_ssd_conv_b, ssd_dt_bias=m_ssd_dt_bias, ssd_a_log=m_ssd_a_log,
             ssd_d=m_ssd_d, ssd_norm_w=m_ssd_norm_w, ssd_w_out=m_ssd_w_out[0], final_norm_w=m_final_norm_w.reshape(1, -1))
    v = dict(norm_w=v_norm_w, gdn_w_in=v_gdn_w_in[0], gdn_conv_w=v_gdn_conv_w[0], gdn_a_log=v_gdn_a_log,
             gdn_dt_bias=v_gdn_dt_bias, gdn_norm_w=v_gdn_norm_w, gdn_w_out=v_gdn_w_out[0], ssd_w_in=v_ssd_w_in[0],
             ssd_conv_w=v_ssd_conv_w[0], ssd_conv_b=v_ssd_conv_b, ssd_dt_bias=v_ssd_dt_bias, ssd_a_log=v_ssd_a_log,
             ssd_d=v_ssd_d, ssd_norm_w=v_ssd_norm_w, ssd_w_out=v_ssd_w_out[0], final_norm_w=v_final_norm_w.reshape(1, -1))
    out_shapes = {n: a.shape for n, a in zip(
        WEIGHTS, [norm_w, gdn_w_in, gdn_conv_w, gdn_a_log, gdn_dt_bias, gdn_norm_w, gdn_w_out, ssd_w_in, ssd_conv_w,
                  ssd_conv_b, ssd_dt_bias, ssd_a_log, ssd_d, ssd_norm_w, ssd_w_out, final_norm_w])}

    small_shapes = [w[n].shape for n in SMALL_SHARDED]
    first = _gather_two_level([_mx(w["gdn_w_in"]), _pack([w[n] for n in SMALL_SHARDED])], "gather_first")
    full = dict(w)
    full["gdn_w_in"] = _shards_to_cols(first[0])
    small_all = [_unpack(first[1][s, 0], small_shapes) for s in range(N_DEV)]
    for idx, n in enumerate(SMALL_SHARDED):
        full[n] = jnp.concatenate([small_all[s][idx] for s in range(N_DEV)], axis=-1)
    late = ["gdn_w_out", "ssd_w_in", "ssd_w_out"]

    def assemble(gathered):
        return {"gdn_w_out": gathered[0].reshape(-1, D_MODEL), "ssd_w_in": _shards_to_cols(gathered[1]),
                "ssd_w_out": gathered[2].reshape(-1, D_MODEL)}

    def early_grads(d_ssd_w_in, d_ssd_w_out):
        return ([_cols_to_shards(d_ssd_w_in).astype(GRAD_WIRE_DTYPE),
                 d_ssd_w_out.reshape(N_DEV, -1, D_MODEL).astype(GRAD_WIRE_DTYPE)], [False] * 2)

    def late_grads(g):
        send_small = jnp.concatenate(
            [_cols_to_shards(g[n]).reshape(N_DEV, -1) for n in SMALL_SHARDED], axis=1)[:, None, :]
        return ([_cols_to_shards(g["gdn_w_in"]).astype(GRAD_WIRE_DTYPE),
                 g["gdn_w_out"].reshape(N_DEV, -1, D_MODEL).astype(GRAD_WIRE_DTYPE), send_small], [False] * 3)

    loss, dx, grads, ssd_recv, gdn_recv = _local_step(
        x[0], loss_target[0], full, (([_mx(w[n]) for n in late], [True] * 3), assemble), early_grads, late_grads)

    rep_shapes = [w[n].shape for n in REPLICATED]
    recv_rep = _exchange([_pack([grads[n] for n in REPLICATED])], [True], "exchange_grads")[0]

    res = {}
    for n, parts in zip(["gdn_w_in", "gdn_w_out", "ssd_w_in", "ssd_w_out"], list(gdn_recv[:2]) + list(ssd_recv)):
        res[n] = _adamw(parts, w[n], m[n], v[n], "adamw_" + n)
    small_res = _adamw(gdn_recv[2], *[_pack([t[n] for n in SMALL_SHARDED]) for t in (w, m, v)], "adamw_small")
    rep_res = _adamw(recv_rep, *[_pack([t[n] for n in REPLICATED]) for t in (w, m, v)], "adamw_replicated")
    for k4 in range(4):
        for n, a in zip(SMALL_SHARDED, _unpack(small_res[k4][0], small_shapes)):
            res.setdefault(n, [None] * 4)[k4] = a
        for n, a in zip(REPLICATED, _unpack(rep_res[k4][0], rep_shapes)):
            res.setdefault(n, [None] * 4)[k4] = a

    loss = lax.psum(loss[0, 0], ("x", "y", "c"))
    outs = [loss, dx[None]]
    for k4 in range(4):
        outs += [res[n][k4].reshape(out_shapes[n]) for n in WEIGHTS]
    return tuple(outs)
```

```python
import jax
import jax.numpy as jnp
from jax import lax
from jax.experimental import pallas as pl
from jax.experimental.pallas import tpu as pltpu

F32 = jnp.float32
MXU_DTYPE = jnp.bfloat16
GRAD_WIRE_DTYPE = jnp.bfloat16
HI = lax.Precision.HIGHEST
EPS = 1e-6
VMEM_LIMIT_BYTES = 56 * 1024 * 1024
N_DEV = 8
MESH = pl.DeviceIdType.MESH

D_MODEL = 1024
CONV_K = 4
GDN_HV = 16
GDN_DK = 128
GDN_CHUNK = 64
SSD_H = 32
SSD_P = 64
SSD_N = 128
SSD_G = 8
SSD_R = SSD_H // SSD_G
SSD_CHUNK = 128
D_INNER = 2048
PAD_W = 128

ADAM_LR = 0.001
ADAM_B1 = 0.9
ADAM_B2 = 0.999
ADAM_EPS = 1e-08
ADAM_WD = 0.01
ADAM_STEP = 10


def _params(*sem):
    return pltpu.CompilerParams(dimension_semantics=sem, vmem_limit_bytes=VMEM_LIMIT_BYTES)


def _mx(a):
    return a.astype(MXU_DTYPE)


def _dot(a, b):
    return jnp.dot(_mx(a), _mx(b), preferred_element_type=F32)


def _dot_nt(a, b):
    return lax.dot_general(_mx(a), _mx(b), (((1,), (1,)), ((), ())), preferred_element_type=F32)


def _dot_tn(a, b):
    return lax.dot_general(_mx(a), _mx(b), (((0,), (0,)), ((), ())), preferred_element_type=F32)


def _dot_hi(a, b):
    return jnp.dot(a, b, precision=HI, preferred_element_type=F32)


def _sigmoid(x):
    return 0.5 * jnp.tanh(0.5 * x) + 0.5


def _silu(x):
    return x * _sigmoid(x)


def _dsilu(x):
    s = _sigmoid(x)
    return s * (1.0 + x * (1.0 - s))


def _softplus(x):
    return jnp.maximum(x, 0.0) + jnp.log1p(jnp.exp(-jnp.abs(x)))


def _col(r, eye):
    return jnp.sum(jnp.where(eye, r, 0.0), axis=1, keepdims=True)


def _row(c, eye):
    return jnp.sum(jnp.where(eye, c, 0.0), axis=0, keepdims=True)


def _col_bcast(r, n):
    return jnp.broadcast_to(r, (n, n)).T


def _masks(n):
    r = lax.broadcasted_iota(jnp.int32, (n, n), 0)
    c = lax.broadcasted_iota(jnp.int32, (n, n), 1)
    return r >= c, r > c, r == c, r, c


def _with_exchange(comm):
    arrs, bcast = comm if comm else ([], [])
    nc = len(arrs)
    anyspec = pl.BlockSpec(memory_space=pl.ANY)

    def wrap(compute, n_in, n_out):
        def body(*refs):
            cin, cout = refs[n_in:n_in + nc], refs[n_in + nc + n_out:n_in + 2 * nc + n_out]
            sems = refs[n_in + 2 * nc + n_out:n_in + 2 * nc + n_out + 3]
            rest = refs[:n_in] + refs[n_in + nc:n_in + nc + n_out] + refs[n_in + 2 * nc + n_out + (3 if nc else 0):]
            if nc:
                @pl.when(pl.program_id(0) == 0)
                def _():
                    _Exchange(cin, cout, bcast, *sems).begin()
            compute(*rest)
            if nc:
                @pl.when(pl.program_id(0) == pl.num_programs(0) - 1)
                def _():
                    _Exchange(cin, cout, bcast, *sems).finish()
        return body

    return dict(arrs=list(arrs), nc=nc, wrap=wrap, in_specs=[anyspec] * nc, out_specs=[anyspec] * nc,
                out_shape=_exchange_out_shapes(arrs, bcast), scratch=_exchange_semaphores(nc) if nc else [])


INPROJ_CONV_STRIP = 256
INPROJ_COL_BLOCK = 512


def _norm_inproj(x, nw, wparts, convs, name, comm=None):
    T = x.shape[0]
    tt = min(T, 256)
    n = len(wparts)
    ck = [k for k in range(n) if convs[k] is not None]
    nconv = len(ck)
    widths = [w.shape[1] for w in wparts]
    conv_blocks = [(k, c0) for k in ck for c0 in range(0, widths[k], INPROJ_COL_BLOCK)]
    ex = _with_exchange(comm)

    def compute(x_ref, nw_ref, *refs):
        w_refs, cw_refs = refs[:n], refs[n:n + 2 * nconv]
        h_ref, o_refs = refs[n + 2 * nconv], refs[n + 2 * nconv + 1:2 * n + 2 * nconv + 1]
        post_refs = refs[2 * n + 2 * nconv + 1:2 * n + 3 * nconv + 1]
        cpre_refs = refs[2 * n + 3 * nconv + 1:2 * n + 4 * nconv + 1]
        p_refs = refs[2 * n + 4 * nconv + 1:]
        xv = x_ref[...]
        r = lax.rsqrt(jnp.mean(xv * xv, axis=-1, keepdims=True) + EPS)
        h = _mx(xv * r * nw_ref[...])
        h_ref[...] = h
        p_of = dict(zip(conv_blocks, p_refs))
        for P in p_refs:
            @pl.when(pl.program_id(0) == 0)
            def _():
                P[0:HALO, :] = jnp.zeros((HALO, P.shape[1]), F32)

        def conv_block(k, c0, cw):
            m = ck.index(k)
            _, _, l2, scale = convs[k]
            cw_ref, cb_ref, out_ref, P = cw_refs[2 * m], cw_refs[2 * m + 1], post_refs[m], p_of[(k, c0)]
            cs = slice(c0, c0 + cw)
            for r0 in range(0, tt, INPROJ_CONV_STRIP):
                rs = slice(r0, r0 + INPROJ_CONV_STRIP)
                acc = cb_ref[:, cs] + cw_ref[0:1, cs] * P[pl.ds(HALO - 3 + r0, INPROJ_CONV_STRIP), :]
                for j in range(1, CONV_K):
                    acc = acc + cw_ref[j:j + 1, cs] * P[pl.ds(HALO - 3 + j + r0, INPROJ_CONV_STRIP), :]
                cpre_refs[m][rs, cs] = acc
                s = _silu(acc)
                if l2:
                    sls = [slice(g0, g0 + GDN_DK) for g0 in range(0, cw, GDN_DK)]
                    rr = [lax.rsqrt(jnp.sum(s[:, sl] * s[:, sl], axis=-1, keepdims=True) + EPS) for sl in sls]
                    for sl, rg in zip(sls, rr):
                        out_ref[rs, c0 + sl.start:c0 + sl.stop] = s[:, sl] * rg * scale
                else:
                    out_ref[rs, cs] = s
            P[0:HALO, :] = P[tt:tt + HALO, :]

        pending = None
        for k in range(n):
            for c0 in range(0, widths[k], INPROJ_COL_BLOCK):
                cw = min(INPROJ_COL_BLOCK, widths[k] - c0)
                pre = jnp.dot(h, w_refs[k][:, c0:c0 + cw], preferred_element_type=F32)
                o_refs[k][:, c0:c0 + cw] = pre
                if convs[k] is not None:
                    p_of[(k, c0)][HALO:HALO + tt, :] = pre
                if pending is not None:
                    conv_block(*pending)
                pending = (k, c0, cw) if convs[k] is not None else None
        if pending is not None:
            conv_block(*pending)

    row = lambda width: pl.BlockSpec((tt, width), lambda i: (i, 0))
    full = lambda a: pl.BlockSpec(a.shape, lambda i: (0, 0))
    once = lambda a: pl.BlockSpec(a.shape, lambda i: (0, 0), pipeline_mode=pl.Buffered(1))
    conv_args = [a for k in ck for a in convs[k][:2]]
    outs = pl.pallas_call(
        ex["wrap"](compute, 2 + n + 2 * nconv, 1 + n + 2 * nconv), grid=(T // tt,),
        in_specs=[row(D_MODEL), full(nw)] + [once(w) for w in wparts] + [full(a) for a in conv_args] + ex["in_specs"],
        out_specs=[row(D_MODEL)] + [row(wd) for wd in widths] + [row(widths[k]) for k in ck + ck] + ex["out_specs"],
        out_shape=[jax.ShapeDtypeStruct((T, D_MODEL), MXU_DTYPE)]
        + [jax.ShapeDtypeStruct((T, wd), F32) for wd in widths]
        + [jax.ShapeDtypeStruct((T, widths[k]), F32) for k in ck + ck] + ex["out_shape"],
        scratch_shapes=ex["scratch"] + [pltpu.VMEM((HALO + tt, min(INPROJ_COL_BLOCK, widths[k] - c0)), F32)
                                        for k, c0 in conv_blocks],
        compiler_params=_params("arbitrary"), name=name,
    )(x, nw, *wparts, *conv_args, *ex["arrs"])
    outs = list(outs)
    res = (outs[0], outs[1:1 + n], outs[1 + n:1 + n + nconv], outs[1 + n + nconv:1 + n + 2 * nconv])
    return res + (outs[1 + n + 2 * nconv:],) if comm else res


def _inproj_bwd(x, nw, dparts, wparts, dres, name, comm=None):
    T = x.shape[0]
    tt = min(T, 512)
    n = len(wparts)
    ex = _with_exchange(comm)

    def body(x_ref, nw_ref, dres_ref, *refs):
        d_refs, w_refs, dx_ref, dnw_ref = refs[:n], refs[n:2 * n], refs[2 * n], refs[2 * n + 1]

        @pl.when(pl.program_id(0) == 0)
        def _():
            dnw_ref[...] = jnp.zeros_like(dnw_ref)

        dh = _dot_nt(d_refs[0][...], w_refs[0][...])
        for d_ref, w_ref in zip(d_refs[1:], w_refs[1:]):
            dh = dh + _dot_nt(d_ref[...], w_ref[...])
        xv = x_ref[...]
        r = lax.rsqrt(jnp.mean(xv * xv, axis=-1, keepdims=True) + EPS)
        xh = xv * r
        dnw_ref[...] += jnp.sum(dh * xh, axis=0, keepdims=True)
        dxn = dh * nw_ref[...]
        dx_ref[...] = dres_ref[...] + r * (dxn - xh * jnp.mean(dxn * xh, axis=-1, keepdims=True))

    row = lambda width: pl.BlockSpec((tt, width), lambda i: (i, 0))
    full = lambda a: pl.BlockSpec(a.shape, lambda i: (0, 0))
    outs = pl.pallas_call(
        ex["wrap"](body, 3 + 2 * n, 2), grid=(T // tt,),
        in_specs=[row(D_MODEL), full(nw), row(D_MODEL)] + [row(d.shape[1]) for d in dparts]
        + [pl.BlockSpec(w.shape, lambda i: (0, 0), pipeline_mode=pl.Buffered(1)) for w in wparts] + ex["in_specs"],
        out_specs=[row(D_MODEL), pl.BlockSpec((1, D_MODEL), lambda i: (0, 0))] + ex["out_specs"],
        out_shape=[jax.ShapeDtypeStruct((T, D_MODEL), F32), jax.ShapeDtypeStruct((1, D_MODEL), F32)]
        + ex["out_shape"],
        scratch_shapes=ex["scratch"],
        compiler_params=_params("arbitrary"), name=name,
    )(x, nw, dres, *dparts, *wparts, *ex["arrs"])
    outs = list(outs)
    return outs[:2] + ([outs[2:]] if comm else [])


def _matmul_tn(a, b, name):
    T, K = a.shape
    N = b.shape[1]
    tt = min(T, 2048)
    tn = min(N, 1024)

    def body(a_ref, b_ref, o_ref):
        @pl.when(pl.program_id(1) == 0)
        def _():
            o_ref[...] = jnp.zeros_like(o_ref)

        o_ref[...] += _dot_tn(a_ref[...], b_ref[...])

    return pl.pallas_call(
        body, grid=(N // tn, T // tt),
        in_specs=[pl.BlockSpec((tt, K), lambda n, t: (t, 0)), pl.BlockSpec((tt, tn), lambda n, t: (t, n))],
        out_specs=pl.BlockSpec((K, tn), lambda n, t: (0, n)),
        out_shape=jax.ShapeDtypeStruct((K, N), F32),
        compiler_params=_params("parallel", "arbitrary"), name=name,
    )(a, b)


OUT_COL_BLOCK = 512


def _out_fwd(o, z, w, wout, xres, gs, gate_first, name):
    T = o.shape[0]
    tt = min(T, 512)
    wide = w.shape[1] == D_INNER

    def body(o_ref, z_ref, w_ref, wout_ref, x_ref, out_ref, yn):
        acc = x_ref[...]
        pending = None
        for b0 in range(0, D_INNER, OUT_COL_BLOCK):
            for g0 in range(b0, b0 + OUT_COL_BLOCK, gs):
                sl = slice(g0, g0 + gs)
                og, zg = o_ref[:, sl], z_ref[:, sl]
                wg = w_ref[:, sl] if wide else w_ref[...]
                if gate_first:
                    u = og * _silu(zg)
                    r = lax.rsqrt(jnp.mean(u * u, axis=-1, keepdims=True) + EPS)
                    yn[:, sl] = _mx(u * r * wg)
                else:
                    r = lax.rsqrt(jnp.mean(og * og, axis=-1, keepdims=True) + EPS)
                    yn[:, sl] = _mx(og * r * wg * _silu(zg))
            if pending is not None:
                acc = acc + jnp.dot(yn[:, pending], wout_ref[pending, :], preferred_element_type=F32)
            pending = slice(b0, b0 + OUT_COL_BLOCK)
        out_ref[...] = acc + jnp.dot(yn[:, pending], wout_ref[pending, :], preferred_element_type=F32)

    row = lambda width: pl.BlockSpec((tt, width), lambda i: (i, 0))
    full = lambda a: pl.BlockSpec(a.shape, lambda i: (0, 0))
    return pl.pallas_call(
        body, grid=(T // tt,),
        in_specs=[row(D_INNER), row(D_INNER), full(w), full(wout), row(D_MODEL)],
        out_specs=row(D_MODEL),
        out_shape=jax.ShapeDtypeStruct((T, D_MODEL), F32),
        scratch_shapes=[pltpu.VMEM((tt, D_INNER), MXU_DTYPE)],
        compiler_params=_params("parallel"), name=name,
    )(o, z, w, wout, xres)


def _out_bwd(dx, o, z, w, wout, gs, gate_first, name, comm=None):
    T = o.shape[0]
    tt = min(T, 256)
    wide = w.shape[1] == D_INNER

    def body(dx_ref, o_ref, z_ref, w_ref, wout_ref, do_ref, dz_ref, dw_ref, yn_ref):
        @pl.when(pl.program_id(0) == 0)
        def _():
            dw_ref[...] = jnp.zeros_like(dw_ref)

        dxb = _mx(dx_ref[...])
        blocks = list(range(0, D_INNER, OUT_COL_BLOCK))
        dyn_b = {b0: _dot_nt(dxb, wout_ref[b0:b0 + OUT_COL_BLOCK, :]) for b0 in blocks[:1]}
        dw_acc = jnp.zeros((1, gs), F32)
        for g0 in range(0, D_INNER, gs):
            b0 = g0 - g0 % OUT_COL_BLOCK
            if g0 == b0 and b0 + OUT_COL_BLOCK < D_INNER:
                nb = b0 + OUT_COL_BLOCK
                dyn_b[nb] = _dot_nt(dxb, wout_ref[nb:nb + OUT_COL_BLOCK, :])
            sl = slice(g0, g0 + gs)
            og, zg, dg = o_ref[:, sl], z_ref[:, sl], dyn_b[b0][:, g0 - b0:g0 - b0 + gs]
            wg = w_ref[:, sl] if wide else w_ref[...]
            sz = _silu(zg)
            if gate_first:
                u = og * sz
                r = lax.rsqrt(jnp.mean(u * u, axis=-1, keepdims=True) + EPS)
                uh = u * r
                yn_ref[:, sl] = _mx(uh * wg)
                dw_g = jnp.sum(dg * uh, axis=0, keepdims=True)
                duh = dg * wg
                du = r * (duh - uh * jnp.mean(duh * uh, axis=-1, keepdims=True))
                do_ref[:, sl] = du * sz
                dz_ref[:, sl] = _mx(du * og * _dsilu(zg))
            else:
                r = lax.rsqrt(jnp.mean(og * og, axis=-1, keepdims=True) + EPS)
                oh = og * r
                yn_ref[:, sl] = _mx(oh * wg * sz)
                dw_g = jnp.sum(dg * oh * sz, axis=0, keepdims=True)
                doh = dg * wg * sz
                dz_ref[:, sl] = _mx(dg * oh * wg * _dsilu(zg))
                do_ref[:, sl] = r * (doh - oh * jnp.mean(doh * oh, axis=-1, keepdims=True))
            if wide:
                dw_ref[:, sl] += dw_g
            else:
                dw_acc = dw_acc + dw_g
        if not wide:
            dw_ref[...] += dw_acc

    row = lambda width: pl.BlockSpec((tt, width), lambda i: (i, 0))
    full = lambda a: pl.BlockSpec(a.shape, lambda i: (0, 0))
    ex = _with_exchange(comm)
    outs = pl.pallas_call(
        ex["wrap"](body, 5, 4), grid=(T // tt,),
        in_specs=[row(D_MODEL), row(D_INNER), row(D_INNER), full(w), full(wout)] + ex["in_specs"],
        out_specs=[row(D_INNER), row(D_INNER), full(w), row(D_INNER)] + ex["out_specs"],
        out_shape=[jax.ShapeDtypeStruct((T, D_INNER), F32), jax.ShapeDtypeStruct((T, D_INNER), MXU_DTYPE),
                   jax.ShapeDtypeStruct(w.shape, F32), jax.ShapeDtypeStruct((T, D_INNER), MXU_DTYPE)]
        + ex["out_shape"],
        scratch_shapes=ex["scratch"],
        compiler_params=_params("arbitrary"), name=name,
    )(dx, o, z, w, wout, *ex["arrs"])
    outs = list(outs)
    return outs[:4] + ([outs[4:]] if comm else [])


HALO = 8
CONV_STRIP = 32


def _conv_bwd(pre, cpre_all, w, dpost, l2, scale, name):
    T, C = pre.shape
    tt = min(T, 1024)
    tc = min(C, 1024 if l2 else 512)
    strip = 2 * CONV_STRIP if l2 else CONV_STRIP
    nT = T // tt
    ext = tt + HALO

    def body(pre_ref, cp_ref, cn_ref, dpost_ref, dn_ref, w_ref, dpre_ref, dw_ref, db_ref, Q):
        i = pl.program_id(1)

        @pl.when(i == 0)
        def _():
            dw_ref[...] = jnp.zeros_like(dw_ref)
            db_ref[...] = jnp.zeros_like(db_ref)

        wj = [w_ref[j:j + 1, :] for j in range(CONV_K)]
        keep_next = jnp.where(i < nT - 1, 1.0, 0.0)
        fold = lambda a: jnp.sum(a.reshape(strip // 8, 8, tc), axis=0)
        dw_acc = [jnp.zeros((8, tc), F32) for _ in range(CONV_K)]
        db_acc = jnp.zeros((8, tc), F32)
        for r0 in list(range(0, tt, strip)) + [tt]:
            n = strip if r0 < tt else HALO
            cpre = cp_ref[r0:r0 + n, :] if r0 < tt else cn_ref[...]
            dy = dpost_ref[r0:r0 + n, :] if r0 < tt else dn_ref[...] * keep_next
            sg = _sigmoid(cpre)
            ds_c = sg * (1.0 + cpre * (1.0 - sg))
            if l2:
                s = cpre * sg
                sls = [slice(g0, g0 + GDN_DK) for g0 in range(0, tc, GDN_DK)]
                rr = [lax.rsqrt(jnp.sum(s[:, sl] * s[:, sl], axis=-1, keepdims=True) + EPS) for sl in sls]
                yh = [s[:, sl] * r for sl, r in zip(sls, rr)]
                pr = [jnp.sum(dy[:, sl] * y, axis=-1, keepdims=True) for sl, y in zip(sls, yh)]
                for sl, r, y, p in zip(sls, rr, yh, pr):
                    Q[r0:r0 + n, sl] = (scale * r) * (dy[:, sl] - y * p) * ds_c[:, sl]
                dyc = Q[r0:r0 + n, :]
            else:
                dyc = dy * ds_c
                Q[r0:r0 + n, :] = dyc
            if r0 < tt:
                db_acc = db_acc + fold(dyc)
        for r0 in range(0, tt, strip):
            xs = pre_ref[r0:r0 + strip, :]
            dpre = jnp.zeros((strip, tc), F32)
            for j in range(CONV_K):
                qj = Q[pl.ds(3 - j + r0, strip), :]
                dpre = dpre + wj[j] * qj
                dw_acc[j] = dw_acc[j] + fold(qj * xs)
            dpre_ref[r0:r0 + strip, :] = _mx(dpre)
        for j in range(CONV_K):
            dw_ref[j:j + 1, :] += jnp.sum(dw_acc[j], axis=0, keepdims=True)
        db_ref[...] += jnp.sum(db_acc, axis=0, keepdims=True)

    tile = pl.BlockSpec((tt, tc), lambda j, i: (i, j))
    nxt = pl.BlockSpec((HALO, tc), lambda j, i: (jnp.minimum((i + 1) * (tt // HALO), T // HALO - 1), j))
    return pl.pallas_call(
        body, grid=(C // tc, nT),
        in_specs=[tile, tile, nxt, tile, nxt, pl.BlockSpec((CONV_K, tc), lambda j, i: (0, j))],
        out_specs=[tile, pl.BlockSpec((CONV_K, tc), lambda j, i: (0, j)), pl.BlockSpec((1, tc), lambda j, i: (0, j))],
        out_shape=[jax.ShapeDtypeStruct((T, C), MXU_DTYPE), jax.ShapeDtypeStruct((CONV_K, C), F32),
                   jax.ShapeDtypeStruct((1, C), F32)],
        scratch_shapes=[pltpu.VMEM((ext, tc), F32)],
        compiler_params=_params("parallel", "arbitrary"), name=name,
    )(pre, cpre_all, cpre_all, dpost, dpost, w)


GDN_LOCKSTEP_CHUNKS = 16
GDN_LOCKSTEP_CHUNKS_BWD = 16
GDN_SCAN_HEADS = 16


def _inv_unit_lower_many(nms, eye, n):
    xs = [jnp.where(eye, 1.0, 0.0) - nm for nm in nms]
    ps = list(nms)
    k = 2
    while k < n:
        ps = [_dot(p, p) for p in ps]
        xs = [x + _dot(x, p) for x, p in zip(xs, ps)]
        k *= 2
    return xs


def _gdn_prep(q, k, v, araw, braw, alog, dtb, name):
    T = q.shape[0]
    C = GDN_CHUNK
    tt = min(T, 1024)
    cpt, nC = tt // C, T // C
    grp = min(cpt, GDN_LOCKSTEP_CHUNKS)

    def body(alog_ref, dtb_ref, q_ref, k_ref, v_ref, a_ref, b_ref,
             u_ref, w_ref, pm_ref, ti_ref, g_ref, beta_ref, gc_ref, qd_ref, kd_ref):
        j = pl.program_id(0)
        tri, strict, eye, r_i, c_i = _masks(C)
        upper = jnp.where(r_i <= c_i, 1.0, 0.0)
        gcs, bts = [], []
        for hh in range(2):
            h = 2 * j + hh
            g = -jnp.exp(alog_ref[h]) * _softplus(a_ref[hh] + dtb_ref[h])
            bt = _sigmoid(b_ref[hh])
            gc = _dot_hi(g, upper)
            g_ref[hh], beta_ref[hh], gc_ref[hh] = g, bt, gc
            gcs.append(gc)
            bts.append(bt)
        for c0 in range(0, cpt, grp):
            cs = list(range(c0, c0 + grp))
            inst = [(c, hh) for c in cs for hh in range(2)]
            rows = {c: slice(c * C, (c + 1) * C) for c in cs}
            qc = {c: q_ref[rows[c], :] for c in cs}
            kc = {c: k_ref[rows[c], :] for c in cs}
            kk = {c: _dot_nt(kc[c], kc[c]) for c in cs}
            qk = {c: _dot_nt(qc[c], kc[c]) for c in cs}
            gcr = [gcs[hh][c:c + 1, :] for c, hh in inst]
            gcc = [_col(r, eye) for r in gcr]
            bc = [_col(bts[hh][c:c + 1, :], eye) for c, hh in inst]
            lm = [jnp.exp(jnp.where(tri, cc - r, -1e30)) for cc, r in zip(gcc, gcr)]
            nm = [jnp.where(strict, kk[c] * b * l, 0.0) for (c, hh), b, l in zip(inst, bc, lm)]
            tinv = _inv_unit_lower_many(nm, eye, C)
            e_c = [jnp.exp(cc) for cc in gcc]
            rhs = [jnp.concatenate([v_ref[rows[c], hh * GDN_DK:(hh + 1) * GDN_DK] * b, kc[c] * (b * e)], axis=1)
                   for (c, hh), b, e in zip(inst, bc, e_c)]
            sol = [_dot(t, r) for t, r in zip(tinv, rhs)]
            for (c, hh), s, t, l, e, cc, r in zip(inst, sol, tinv, lm, e_c, gcc, gcr):
                hs = slice(hh * GDN_DK, (hh + 1) * GDN_DK)
                u_ref[rows[c], hs] = s[:, :GDN_DK]
                w_ref[rows[c], hs] = _mx(s[:, GDN_DK:])
                pm_ref[hh, c] = _mx(jnp.where(tri, qk[c] * l, 0.0))
                ti_ref[hh, c] = _mx(t)
                qd_ref[rows[c], hs] = _mx(qc[c] * e)
                kd_ref[rows[c], hs] = _mx(kc[c] * jnp.exp(r[:, C - 1:C] - cc))

    smem = pl.BlockSpec(memory_space=pltpu.SMEM)
    rows_spec = pl.BlockSpec((2, cpt, C), lambda j, i: (j, i, 0))
    qk_spec = pl.BlockSpec((tt, GDN_DK), lambda j, i: (i, j))
    v_spec = pl.BlockSpec((tt, 2 * GDN_DK), lambda j, i: (i, j))
    cc_spec = pl.BlockSpec((2, cpt, C, C), lambda j, i: (j, i, 0, 0))
    rows_shape = jax.ShapeDtypeStruct((GDN_HV, nC, C), F32)
    cc_shape = jax.ShapeDtypeStruct((GDN_HV, nC, C, C), MXU_DTYPE)
    return pl.pallas_call(
        body, grid=(GDN_HV // 2, T // tt),
        in_specs=[smem, smem, qk_spec, qk_spec, v_spec, rows_spec, rows_spec],
        out_specs=[v_spec, v_spec, cc_spec, cc_spec, rows_spec, rows_spec, rows_spec, v_spec, v_spec],
        out_shape=[jax.ShapeDtypeStruct((T, D_INNER), F32), jax.ShapeDtypeStruct((T, D_INNER), MXU_DTYPE),
                   cc_shape, cc_shape, rows_shape, rows_shape, rows_shape,
                   jax.ShapeDtypeStruct((T, D_INNER), MXU_DTYPE), jax.ShapeDtypeStruct((T, D_INNER), MXU_DTYPE)],
        compiler_params=_params("parallel", "parallel"), name=name,
    )(alog, dtb, q, k, v, araw, braw)


def _gdn_state_fwd(q, k, u, w, pm, gc, name):
    T = q.shape[0]
    C = GDN_CHUNK
    HG = GDN_SCAN_HEADS
    tt = min(T, 512)
    cpt, nC = tt // C, T // C

    def body(q_ref, k_ref, u_ref, w_ref, pm_ref, gc_ref, o_ref, vn_ref, sall_ref, S):
        @pl.when(pl.program_id(1) == 0)
        def _():
            S[...] = jnp.zeros_like(S)

        heads = list(range(HG))

        def chunk(c, carry):
            rows = pl.ds(pl.multiple_of(c * C, C), C)
            hs = [slice(h * GDN_DK, (h + 1) * GDN_DK) for h in heads]
            gl = [jnp.exp(gc_ref[h, pl.ds(c, 1), C - 1:C]) for h in heads]
            sv = [S[h] for h in heads]
            for h in heads:
                sall_ref[h, c] = _mx(sv[h])
            ws = [_dot(w_ref[rows, hs[h]], sv[h]) for h in heads]
            qsv = [_dot(q_ref[rows, hs[h]], sv[h]) for h in heads]
            vn = [u_ref[rows, hs[h]] - ws[h] for h in heads]
            pv = [_dot(pm_ref[h, c], vn[h]) for h in heads]
            kv = [_dot_tn(k_ref[rows, hs[h]], vn[h]) for h in heads]
            for h in heads:
                vn_ref[rows, hs[h]] = _mx(vn[h])
                o_ref[rows, hs[h]] = qsv[h] + pv[h]
                S[h] = sv[h] * gl[h] + kv[h]
            return carry

        lax.fori_loop(0, cpt, chunk, 0)

    v_spec = pl.BlockSpec((tt, HG * GDN_DK), lambda g, i: (i, g))
    return pl.pallas_call(
        body, grid=(GDN_HV // HG, T // tt),
        in_specs=[v_spec, v_spec, v_spec, v_spec,
                  pl.BlockSpec((HG, cpt, C, C), lambda g, i: (g, i, 0, 0)),
                  pl.BlockSpec((HG, cpt, C), lambda g, i: (g, i, 0))],
        out_specs=[v_spec, v_spec, pl.BlockSpec((HG, cpt, GDN_DK, GDN_DK), lambda g, i: (g, i, 0, 0))],
        out_shape=[jax.ShapeDtypeStruct((T, D_INNER), F32), jax.ShapeDtypeStruct((T, D_INNER), MXU_DTYPE),
                   jax.ShapeDtypeStruct((GDN_HV, nC, GDN_DK, GDN_DK), MXU_DTYPE)],
        scratch_shapes=[pltpu.VMEM((HG, GDN_DK, GDN_DK), F32)],
        compiler_params=_params("parallel", "arbitrary"), name=name,
    )(q, k, u, w, pm, gc)


def _gdn_state_bwd(q, k, w, pm, vn, sall, gc, do, name):
    T = q.shape[0]
    C = GDN_CHUNK
    HG = GDN_SCAN_HEADS
    tt = min(T, 512)
    cpt, nC, nT = tt // C, T // C, T // tt

    def body(q_ref, k_ref, w_ref, pm_ref, vn_ref, sall_ref, gc_ref, do_ref, dvn_ref, dkd_ref, dgl_ref, dS):
        @pl.when(pl.program_id(1) == 0)
        def _():
            dS[...] = jnp.zeros_like(dS)

        heads = list(range(HG))

        def chunk(ci, carry):
            c = cpt - 1 - ci
            rows = pl.ds(pl.multiple_of(c * C, C), C)
            hs = [slice(h * GDN_DK, (h + 1) * GDN_DK) for h in heads]
            gl = [jnp.exp(gc_ref[h, pl.ds(c, 1), C - 1:C]) for h in heads]
            dsn = [dS[h] for h in heads]
            doc = [do_ref[rows, hs[h]] for h in heads]
            kds = [_dot(k_ref[rows, hs[h]], dsn[h]) for h in heads]
            pdo = [_dot_tn(pm_ref[h, c], doc[h]) for h in heads]
            dkd = [_dot_nt(vn_ref[rows, hs[h]], dsn[h]) for h in heads]
            qdo = [_dot_tn(q_ref[rows, hs[h]], doc[h]) for h in heads]
            dvn = [pdo[h] + kds[h] for h in heads]
            wdv = [_dot_tn(w_ref[rows, hs[h]], dvn[h]) for h in heads]
            for h in heads:
                dgl = jnp.sum(jnp.sum(dsn[h] * sall_ref[h, c].astype(F32), axis=0, keepdims=True), axis=1, keepdims=True)
                dgl_ref[h, pl.ds(c, 1), :] = jnp.broadcast_to(dgl, (1, C))
                dvn_ref[rows, hs[h]] = dvn[h]
                dkd_ref[rows, hs[h]] = dkd[h]
                dS[h] = dsn[h] * gl[h] + qdo[h] - wdv[h]
            return carry

        lax.fori_loop(0, cpt, chunk, 0)

    rev = lambda i: nT - 1 - i
    v_spec = pl.BlockSpec((tt, HG * GDN_DK), lambda g, i: (rev(i), g))
    rows_spec = pl.BlockSpec((HG, cpt, C), lambda g, i: (g, rev(i), 0))
    return pl.pallas_call(
        body, grid=(GDN_HV // HG, nT),
        in_specs=[v_spec, v_spec, v_spec, pl.BlockSpec((HG, cpt, C, C), lambda g, i: (g, rev(i), 0, 0)), v_spec,
                  pl.BlockSpec((HG, cpt, GDN_DK, GDN_DK), lambda g, i: (g, rev(i), 0, 0)), rows_spec, v_spec],
        out_specs=[v_spec, v_spec, rows_spec],
        out_shape=[jax.ShapeDtypeStruct((T, D_INNER), F32), jax.ShapeDtypeStruct((T, D_INNER), F32),
                   jax.ShapeDtypeStruct((GDN_HV, nC, C), F32)],
        scratch_shapes=[pltpu.VMEM((HG, GDN_DK, GDN_DK), F32)],
        compiler_params=_params("parallel", "arbitrary"), name=name,
    )(q, k, w, pm, vn, sall, gc, do)


def _gdn_local_bwd(q, k, v, gc, beta, tinv, u, w, pm, vn, sall, do, dvn, dkd, dgl, name):
    T = q.shape[0]
    C = GDN_CHUNK
    tt = min(T, 1024)
    cpt, nC = tt // C, T // C
    grp = min(cpt, GDN_LOCKSTEP_CHUNKS_BWD)

    def body(q_ref, k_ref, v_ref, gc_ref, b_ref, ti_ref, u_ref, w_ref, pm_ref, vn_ref, sall_ref, do_ref,
             dvn_ref, dkd_ref, dgl_ref, dq_ref, dk_ref, dv_ref, dg_ref, dbeta_ref, dgc_s):
        tri, strict, eye, r_i, c_i = _masks(C)
        lower = jnp.where(r_i >= c_i, 1.0, 0.0)
        lane = lax.broadcasted_iota(jnp.int32, (1, C), 1)
        rsum = lambda a: jnp.sum(a, axis=1, keepdims=True)
        for c0 in range(0, cpt, grp):
            cs = list(range(c0, c0 + grp))
            inst = [(c, hh) for c in cs for hh in range(2)]
            n = len(inst)
            rows = {c: slice(c * C, (c + 1) * C) for c in cs}
            hsl = [slice(hh * GDN_DK, (hh + 1) * GDN_DK) for c, hh in inst]
            qc = {c: q_ref[rows[c], :] for c in cs}
            kc = {c: k_ref[rows[c], :] for c in cs}
            kk = {c: _dot_nt(kc[c], kc[c]) for c in cs}
            gcr = [gc_ref[hh, c:c + 1, :] for c, hh in inst]
            gcc = [_col(r, eye) for r in gcr]
            bc = [_col(b_ref[hh, c:c + 1, :], eye) for c, hh in inst]
            lm = [jnp.exp(jnp.where(tri, cc - r, -1e30)) for cc, r in zip(gcc, gcr)]
            e_c = [jnp.exp(cc) for cc in gcc]
            el_c = [jnp.exp(r[:, C - 1:C] - cc) for cc, r in zip(gcc, gcr)]
            gl = [jnp.exp(r[:, C - 1:C]) for r in gcr]
            doc = [do_ref[rows[c], hsl[i]] for i, (c, hh) in enumerate(inst)]
            dvn = [dvn_ref[rows[c], hsl[i]] for i, (c, hh) in enumerate(inst)]
            sv = [sall_ref[hh, c] for c, hh in inst]
            aa = [_dot_nt(jnp.concatenate([_mx(doc[i]), _mx(dvn[i])], axis=0), sv[i]) for i in range(n)]
            dpm = [jnp.where(tri, _dot_nt(doc[i], vn_ref[rows[c], hsl[i]]), 0.0) for i, (c, hh) in enumerate(inst)]
            dqd = [a[:C] for a in aa]
            drhs = [_dot_tn(ti_ref[hh, c], jnp.concatenate([dvn[i], -aa[i][C:]], axis=1))
                    for i, (c, hh) in enumerate(inst)]
            sol = [jnp.concatenate([_mx(u_ref[rows[c], hsl[i]]), w_ref[rows[c], hsl[i]]], axis=1)
                   for i, (c, hh) in enumerate(inst)]
            dnm = [-jnp.where(strict, _dot_nt(drhs[i], sol[i]), 0.0) for i in range(n)]
            dkk = [dnm[i] * bc[i] * lm[i] for i in range(n)]
            dqk = [dpm[i] * lm[i] for i in range(n)]
            dq1 = [_dot(dqk[i], kc[c]) for i, (c, hh) in enumerate(inst)]
            dk1 = [_dot(dkk[i], kc[c]) for i, (c, hh) in enumerate(inst)]
            dk2 = [_dot_tn(dkk[i], kc[c]) for i, (c, hh) in enumerate(inst)]
            dk3 = [_dot_tn(dqk[i], qc[c]) for i, (c, hh) in enumerate(inst)]
            dq_acc = {c: jnp.zeros((C, GDN_DK), F32) for c in cs}
            dk_acc = {c: jnp.zeros((C, GDN_DK), F32) for c in cs}
            for i, (c, hh) in enumerate(inst):
                k_, q_, v_ = kc[c], qc[c], v_ref[rows[c], hsl[i]]
                dvb, dkbe = drhs[i][:, :GDN_DK], drhs[i][:, GDN_DK:]
                dkd = dkd_ref[rows[c], hsl[i]]
                kb = k_ * bc[i]
                dkb = dkbe * e_c[i]
                del_el = dkd * k_ * el_c[i]
                dbc = rsum(dnm[i] * kk[c] * lm[i]) + rsum(dkb * k_ + dvb * v_)
                dq_acc[c] = dq_acc[c] + dq1[i] + dqd[i] * e_c[i]
                dk_acc[c] = dk_acc[c] + dk1[i] + dk2[i] + dk3[i] + dkd * el_c[i] + dkb * bc[i]
                dv_ref[rows[c], hsl[i]] = dvb * bc[i]
                nm = jnp.where(strict, kk[c] * bc[i] * lm[i], 0.0)
                gm = dnm[i] * nm + dpm[i] * pm_ref[hh, c].astype(F32)
                dgc_col = rsum(gm) + rsum((dkbe * kb + dqd[i] * q_) * e_c[i] - del_el)
                dglast = (jnp.sum(jnp.sum(del_el, axis=0, keepdims=True), axis=1, keepdims=True)
                          + dgl_ref[hh, c:c + 1, 0:1] * gl[i])
                dgc_s[hh, c:c + 1, :] = (_row(dgc_col, eye) - jnp.sum(gm, axis=0, keepdims=True)
                                         + jnp.where(lane == C - 1, dglast, 0.0))
                dbeta_ref[hh, c:c + 1, :] = _row(dbc, eye)
            for c in cs:
                dq_ref[rows[c], :] = dq_acc[c]
                dk_ref[rows[c], :] = dk_acc[c]
        for hh in range(2):
            dg_ref[hh] = _dot_hi(dgc_s[hh], lower)

    rows_spec = pl.BlockSpec((2, cpt, C), lambda j, i: (j, i, 0))
    qk_spec = pl.BlockSpec((tt, GDN_DK), lambda j, i: (i, j))
    v_spec = pl.BlockSpec((tt, 2 * GDN_DK), lambda j, i: (i, j))
    cc_spec = pl.BlockSpec((2, cpt, C, C), lambda j, i: (j, i, 0, 0))
    rows_shape = jax.ShapeDtypeStruct((GDN_HV, nC, C), F32)
    return pl.pallas_call(
        body, grid=(GDN_HV // 2, T // tt),
        in_specs=[qk_spec, qk_spec, v_spec, rows_spec, rows_spec, cc_spec, v_spec, v_spec, cc_spec, v_spec,
                  pl.BlockSpec((2, cpt, GDN_DK, GDN_DK), lambda j, i: (j, i, 0, 0)), v_spec, v_spec, v_spec, rows_spec],
        out_specs=[qk_spec, qk_spec, v_spec, rows_spec, rows_spec],
        out_shape=[jax.ShapeDtypeStruct((T, GDN_HV // 2 * GDN_DK), F32),
                   jax.ShapeDtypeStruct((T, GDN_HV // 2 * GDN_DK), F32),
                   jax.ShapeDtypeStruct((T, D_INNER), F32), rows_shape, rows_shape],
        scratch_shapes=[pltpu.VMEM((2, cpt, C), F32)],
        compiler_params=_params("parallel", "parallel"), name=name,
    )(q, k, v, gc, beta, tinv, u, w, pm, vn, sall, do, dvn, dkd, dgl)


def _gdn_gate_bwd(araw, braw, dg, dbeta, alog, dtb, name):
    H, T = araw.shape

    def body(a_ref, b_ref, dg_ref, dbt_ref, alog_ref, dtb_ref, da_ref, db_ref, dalog_ref, ddtb_ref):
        xa = a_ref[...] + dtb_ref[...]
        ea = jnp.exp(alog_ref[...])
        dgv = dg_ref[...]
        da = -dgv * ea * _sigmoid(xa)
        da_ref[...] = da
        dalog_ref[...] = jnp.sum(-dgv * ea * _softplus(xa), axis=1, keepdims=True)
        ddtb_ref[...] = jnp.sum(da, axis=1, keepdims=True)
        bt = _sigmoid(b_ref[...])
        db_ref[...] = dbt_ref[...] * bt * (1.0 - bt)

    return pl.pallas_call(
        body,
        out_shape=[jax.ShapeDtypeStruct((H, T), F32), jax.ShapeDtypeStruct((H, T), F32),
                   jax.ShapeDtypeStruct((H, 1), F32), jax.ShapeDtypeStruct((H, 1), F32)],
        compiler_params=pltpu.CompilerParams(vmem_limit_bytes=VMEM_LIMIT_BYTES), name=name,
    )(araw, braw, dg, dbeta, alog, dtb)


SSD_LOCKSTEP_CHUNKS = 2
SSD_LOCKSTEP_CHUNKS_BWD = 1
SSD_LOCKSTEP_HEADS_BWD = 2


def _ssd_scan_fwd(xs, bm, cm, dtraw, alog, dtb, dskip, name):
    T = xs.shape[0]
    Q = SSD_CHUNK
    tt = min(T, 1024)
    cpt, nC = tt // Q, T // Q
    GW = SSD_R * SSD_P

    def body(alog_ref, dtb_ref, dsk_ref, xs_ref, b_ref, c_ref, dt_ref, y_ref, sall_ref, dto_ref, S, dt_s, acs_s):
        gi, i = pl.program_id(0), pl.program_id(1)

        @pl.when(i == 0)
        def _():
            S[...] = jnp.zeros_like(S)

        tri, _, eye, r_i, c_i = _masks(Q)
        upper = jnp.where(r_i <= c_i, 1.0, 0.0)
        for r in range(SSD_R):
            h = SSD_R * gi + r
            dt = _softplus(dt_ref[r] + dtb_ref[h])
            dto_ref[r] = dt
            dt_s[r] = dt
            acs_s[r] = _dot_hi(-jnp.exp(alog_ref[h]) * dt, upper)

        ps = [slice(r * SSD_P, (r + 1) * SSD_P) for r in range(SSD_R)]
        s_cur = [S[:, ps[r]] for r in range(SSD_R)]
        grp = min(cpt, SSD_LOCKSTEP_CHUNKS)
        for c0 in range(0, cpt, grp):
            cs = list(range(c0, c0 + grp))
            inst = [(c, r) for c in cs for r in range(SSD_R)]
            rows = {c: slice(c * Q, (c + 1) * Q) for c in cs}
            bc_ = {c: b_ref[rows[c], :] for c in cs}
            cc_ = {c: c_ref[rows[c], :] for c in cs}
            cb = {c: _dot_nt(cc_[c], bc_[c]) for c in cs}
            xr = [xs_ref[rows[c], ps[r]] for c, r in inst]
            acr = [acs_s[r, c:c + 1, :] for c, r in inst]
            acc = [_col_bcast(a, Q) for a in acr]
            dtr = [dt_s[r, c:c + 1, :] for c, r in inst]
            mm = [cb[c] * (jnp.exp(jnp.where(tri, acc[i] - acr[i], -1e30)) * dtr[i]) for i, (c, r) in enumerate(inst)]
            bct = {c: bc_[c].T for c in cs}
            st = [_dot(bct[c] * (jnp.exp(acr[i][:, Q - 1:Q] - acr[i]) * dtr[i]), xr[i]) for i, (c, r) in enumerate(inst)]
            yd = [_dot(mm[i], xr[i]) for i in range(len(inst))]
            s_prev = []
            for i, (c, r) in enumerate(inst):
                s_prev.append(s_cur[r])
                s_cur[r] = s_cur[r] * jnp.exp(acr[i][:, Q - 1:Q]) + st[i]
            yo = [_dot(cc_[c] * jnp.exp(acc[i]), s_prev[i]) for i, (c, r) in enumerate(inst)]
            for i, (c, r) in enumerate(inst):
                sall_ref[0, c, :, ps[r]] = s_prev[i]
                y_ref[rows[c], ps[r]] = yd[i] + yo[i] + dsk_ref[SSD_R * gi + r] * xr[i]
        for r in range(SSD_R):
            S[:, ps[r]] = s_cur[r]

    smem = pl.BlockSpec(memory_space=pltpu.SMEM)
    rows_spec = pl.BlockSpec((SSD_R, cpt, Q), lambda g, i: (g, i, 0))
    return pl.pallas_call(
        body, grid=(SSD_G, T // tt),
        in_specs=[smem, smem, smem,
                  pl.BlockSpec((tt, GW), lambda g, i: (i, g)), pl.BlockSpec((tt, SSD_N), lambda g, i: (i, g)),
                  pl.BlockSpec((tt, SSD_N), lambda g, i: (i, g)), rows_spec],
        out_specs=[pl.BlockSpec((tt, GW), lambda g, i: (i, g)),
                   pl.BlockSpec((1, cpt, SSD_N, GW), lambda g, i: (g, i, 0, 0)), rows_spec],
        out_shape=[jax.ShapeDtypeStruct((T, D_INNER), F32), jax.ShapeDtypeStruct((SSD_G, nC, SSD_N, GW), F32),
                   jax.ShapeDtypeStruct((SSD_H, nC, Q), F32)],
        scratch_shapes=[pltpu.VMEM((SSD_N, GW), F32), pltpu.VMEM((SSD_R, cpt, Q), F32),
                        pltpu.VMEM((SSD_R, cpt, Q), F32)],
        compiler_params=_params("parallel", "arbitrary"), name=name,
    )(alog, dtb, dskip, xs, bm, cm, dtraw)


def _ssd_scan_bwd(xs, bm, cm, dt, sall, dy, alog, dskip, name):
    T = xs.shape[0]
    Q = SSD_CHUNK
    tt = min(T, 1024)
    cpt, nC, nT = tt // Q, T // Q, T // tt
    GW = SSD_R * SSD_P

    def body(alog_ref, dsk_ref, xs_ref, b_ref, c_ref, dt_ref, sall_ref, dy_ref,
             dxs_ref, db_ref, dc_ref, da_ref, ddt_ref, dd_ref, dS, acs_s, dacs_s, ddt_s, dd_s):
        gi, i = pl.program_id(0), pl.program_id(1)

        @pl.when(i == 0)
        def _():
            dS[...] = jnp.zeros_like(dS)

        tri, _, eye, r_i, c_i = _masks(Q)
        upper = jnp.where(r_i <= c_i, 1.0, 0.0)
        lower = jnp.where(r_i >= c_i, 1.0, 0.0)
        lane = lax.broadcasted_iota(jnp.int32, (1, Q), 1)
        for r in range(SSD_R):
            acs_s[r] = _dot_hi(-jnp.exp(alog_ref[SSD_R * gi + r]) * dt_ref[r], upper)

        ps = [slice(r * SSD_P, (r + 1) * SSD_P) for r in range(SSD_R)]
        ds_cur = [dS[:, ps[r]] for r in range(SSD_R)]
        grp = min(cpt, SSD_LOCKSTEP_CHUNKS_BWD)
        csum = lambda a: jnp.sum(a, axis=0, keepdims=True)
        tsum = lambda a: jnp.sum(csum(a), axis=1, keepdims=True)
        ones8 = jnp.ones((8, SSD_P), F32)
        for c0 in range(cpt - grp, -1, -grp):
            cs = list(range(c0 + grp - 1, c0 - 1, -1))
            rows = {c: slice(c * Q, (c + 1) * Q) for c in cs}
            bc_ = {c: b_ref[rows[c], :] for c in cs}
            cc_ = {c: c_ref[rows[c], :] for c in cs}
            cb = {c: _dot_nt(cc_[c], bc_[c]) for c in cs}
            cbt = {c: _dot_nt(bc_[c], cc_[c]) for c in cs}
            bct = {c: bc_[c].T for c in cs}
            cct = {c: cc_[c].T for c in cs}
            dcb = {c: jnp.zeros((Q, Q), F32) for c in cs}
            dcbt = {c: jnp.zeros((Q, Q), F32) for c in cs}
            db_acc = {c: jnp.zeros((Q, SSD_N), F32) for c in cs}
            dc_acc = {c: jnp.zeros((Q, SSD_N), F32) for c in cs}
            for h0 in range(0, SSD_R, SSD_LOCKSTEP_HEADS_BWD):
                inst = [(c, r) for c in cs for r in range(h0, h0 + SSD_LOCKSTEP_HEADS_BWD)]
                n = len(inst)
                xr = [xs_ref[rows[c], ps[r]] for c, r in inst]
                dyr = [dy_ref[rows[c], ps[r]] for c, r in inst]
                acr = [acs_s[r, c:c + 1, :] for c, r in inst]
                dtr = [dt_ref[r, c:c + 1, :] for c, r in inst]
                acc = [_col_bcast(a, Q) for a in acr]
                dtb = [_col_bcast(d, Q) for d in dtr]
                al = [a[:, Q - 1:Q] for a in acr]
                e_c = [jnp.exp(a) for a in acc]
                dl_c = [jnp.exp(al[i] - acc[i]) for i in range(n)]
                e_r = [jnp.exp(a) for a in acr]
                dl_r = [jnp.exp(al[i] - acr[i]) for i in range(n)]
                gl = [jnp.exp(a) for a in al]
                lm = [jnp.exp(jnp.where(tri, acc[i] - acr[i], -1e30)) for i in range(n)]
                lmt = [jnp.exp(jnp.where(r_i <= c_i, acr[i] - acc[i], -1e30)) for i in range(n)]
                mmt = [cbt[c] * lmt[i] for i, (c, r) in enumerate(inst)]
                sr = [sall_ref[0, c, :, ps[r]] for c, r in inst]
                dmm0 = [_dot_nt(dyr[i], xr[i]) for i in range(n)]
                dmm0t = [_dot_nt(xr[i], dyr[i]) for i in range(n)]
                dxd1 = [_dot(mmt[i], dyr[i]) for i in range(n)]
                dce = [_dot_nt(dyr[i], sr[i]) for i in range(n)]
                dcet = [_dot_nt(sr[i], dyr[i]) for i in range(n)]
                cdy = [_dot(cct[c] * e_r[i], dyr[i]) for i, (c, r) in enumerate(inst)]
                dsn = []
                for i, (c, r) in enumerate(inst):
                    dsn.append(ds_cur[r])
                    ds_cur[r] = gl[i] * ds_cur[r] + cdy[i]
                dxd = [dxd1[i] + _dot(bc_[c] * dl_c[i], dsn[i]) for i, (c, r) in enumerate(inst)]
                dbd0 = [_dot_nt(xr[i], dsn[i]) for i in range(n)]
                dbd0t = [_dot_nt(dsn[i], xr[i]) for i in range(n)]
                for i, (c, r) in enumerate(inst):
                    dgl = tsum(dsn[i] * sr[i])
                    dc_acc[c] = dc_acc[c] + dce[i] * e_c[i]
                    db_acc[c] = db_acc[c] + dbd0[i] * (dtb[i] * dl_c[i])
                    dl0 = dmm0[i] * lm[i]
                    dl0t = dmm0t[i] * (lmt[i] * dtb[i])
                    dcb[c] = dcb[c] + dl0 * dtr[i]
                    dcbt[c] = dcbt[c] + dl0t
                    csum_gm0 = csum(dl0 * cb[c])
                    rsum_gm = csum(dl0t * cbt[c])
                    r_de = csum(dcet[i] * cct[c]) * e_r[i]
                    r_dl = csum(dbd0t[i] * bct[c]) * dl_r[i]
                    dalast = jnp.sum(r_dl * dtr[i], axis=1, keepdims=True) + dgl * gl[i]
                    dacs_s[r, c:c + 1, :] = (rsum_gm + r_de - (r_dl + csum_gm0) * dtr[i]
                                             + jnp.where(lane == Q - 1, dalast, 0.0))
                    ddt_s[r, c:c + 1, :] = csum_gm0 + r_dl
                    dd_s[r, c:c + 1, :] = _dot_nt(ones8, dyr[i] * xr[i])[0:1]
                    dxs_ref[rows[c], ps[r]] = dxd[i] * dtb[i][:, :SSD_P] + dsk_ref[SSD_R * gi + r] * dyr[i]
            for c in cs:
                dc_ref[rows[c], :] = dc_acc[c] + _dot(dcb[c], bc_[c])
                db_ref[rows[c], :] = db_acc[c] + _dot(dcbt[c], cc_[c])
        for r in range(SSD_R):
            dS[:, ps[r]] = ds_cur[r]
        for r in range(SSD_R):
            da_ref[r] = _dot_hi(dacs_s[r], lower)
            ddt_ref[r] = ddt_s[r]
            dd_ref[r] = dd_s[r]

    rev = lambda i: nT - 1 - i
    smem = pl.BlockSpec(memory_space=pltpu.SMEM)
    rows_spec = pl.BlockSpec((SSD_R, cpt, Q), lambda g, i: (g, rev(i), 0))
    x_spec = pl.BlockSpec((tt, GW), lambda g, i: (rev(i), g))
    n_spec = pl.BlockSpec((tt, SSD_N), lambda g, i: (rev(i), g))
    rows_shape = jax.ShapeDtypeStruct((SSD_H, nC, Q), F32)
    return pl.pallas_call(
        body, grid=(SSD_G, nT),
        in_specs=[smem, smem, x_spec, n_spec, n_spec, rows_spec,
                  pl.BlockSpec((1, cpt, SSD_N, GW), lambda g, i: (g, rev(i), 0, 0)), x_spec],
        out_specs=[x_spec, n_spec, n_spec, rows_spec, rows_spec, rows_spec],
        out_shape=[jax.ShapeDtypeStruct((T, D_INNER), F32), jax.ShapeDtypeStruct((T, SSD_G * SSD_N), F32),
                   jax.ShapeDtypeStruct((T, SSD_G * SSD_N), F32), rows_shape, rows_shape, rows_shape],
        scratch_shapes=[pltpu.VMEM((SSD_N, GW), F32)] + [pltpu.VMEM((SSD_R, cpt, Q), F32)] * 4,
        compiler_params=_params("parallel", "arbitrary"), name=name,
    )(alog, dskip, xs, bm, cm, dt, sall, dy)


def _ssd_gate_bwd(dtraw, dt, da, ddt_direct, ddrow, alog, dtb, name):
    H, T = dtraw.shape

    def body(raw_ref, dt_ref, da_ref, ddt_ref, dd_ref, alog_ref, dtb_ref, draw_ref, dalog_ref, ddtb_ref, dD_ref):
        a = -jnp.exp(alog_ref[...])
        dav = da_ref[...]
        ddt = ddt_ref[...] + dav * a
        draw = ddt * _sigmoid(raw_ref[...] + dtb_ref[...])
        draw_ref[...] = draw
        dalog_ref[...] = jnp.sum(dav * dt_ref[...], axis=1, keepdims=True) * a
        ddtb_ref[...] = jnp.sum(draw, axis=1, keepdims=True)
        dD_ref[...] = jnp.sum(dd_ref[...], axis=1, keepdims=True)

    return pl.pallas_call(
        body,
        out_shape=[jax.ShapeDtypeStruct((H, T), F32)] + [jax.ShapeDtypeStruct((H, 1), F32)] * 3,
        compiler_params=pltpu.CompilerParams(vmem_limit_bytes=VMEM_LIMIT_BYTES), name=name,
    )(dtraw, dt, da, ddt_direct, ddrow, alog, dtb)


def _final_loss(x, fw, tgt, name):
    T = x.shape[0]
    tt = min(T, 1024)
    nT = T // tt

    def body(x_ref, w_ref, t_ref, dx_ref, dw_ref, loss_ref, acc):
        i = pl.program_id(0)

        @pl.when(i == 0)
        def _():
            dw_ref[...] = jnp.zeros_like(dw_ref)
            acc[...] = jnp.zeros_like(acc)

        xv = x_ref[...]
        r = lax.rsqrt(jnp.mean(xv * xv, axis=-1, keepdims=True) + EPS)
        xh = xv * r
        err = xh * w_ref[...] - t_ref[...]
        acc[...] += jnp.sum(err * err, axis=0, keepdims=True)
        dout = err * (1.0 / D_MODEL)
        dw_ref[...] += jnp.sum(dout * xh, axis=0, keepdims=True)
        dxn = dout * w_ref[...]
        dx_ref[...] = r * (dxn - xh * jnp.mean(dxn * xh, axis=-1, keepdims=True))

        @pl.when(i == nT - 1)
        def _():
            loss_ref[...] = (0.5 / D_MODEL) * jnp.sum(acc[...], axis=1, keepdims=True)

    row = pl.BlockSpec((tt, D_MODEL), lambda i: (i, 0))
    vec = pl.BlockSpec((1, D_MODEL), lambda i: (0, 0))
    return pl.pallas_call(
        body, grid=(nT,),
        in_specs=[row, vec, row],
        out_specs=[row, vec, pl.BlockSpec((1, 1), lambda i: (0, 0))],
        out_shape=[jax.ShapeDtypeStruct((T, D_MODEL), F32), jax.ShapeDtypeStruct((1, D_MODEL), F32),
                   jax.ShapeDtypeStruct((1, 1), F32)],
        scratch_shapes=[pltpu.VMEM((1, D_MODEL), F32)],
        compiler_params=_params("arbitrary"), name=name,
    )(x, fw, tgt)


def _adamw(parts, w, m, v, name):
    R, C = w.shape
    tr = 256 if R % 256 == 0 else R

    def body(p_ref, w_ref, m_ref, v_ref, g_ref, d_ref, nm_ref, nv_ref):
        g = p_ref[0].astype(F32)
        for s in range(1, N_DEV):
            g = g + p_ref[s].astype(F32)
        mn = ADAM_B1 * m_ref[...] + (1.0 - ADAM_B1) * g
        vn = ADAM_B2 * v_ref[...] + (1.0 - ADAM_B2) * (g * g)
        mh = mn / (1.0 - ADAM_B1 ** ADAM_STEP)
        vh = vn / (1.0 - ADAM_B2 ** ADAM_STEP)
        g_ref[...] = g
        d_ref[...] = -ADAM_LR * (mh / (jnp.sqrt(vh) + ADAM_EPS) + ADAM_WD * w_ref[...])
        nm_ref[...] = mn
        nv_ref[...] = vn

    blk = pl.BlockSpec((tr, C), lambda i: (i, 0))
    return pl.pallas_call(
        body, grid=(R // tr,),
        in_specs=[pl.BlockSpec((N_DEV, tr, C), lambda i: (0, i, 0)), blk, blk, blk],
        out_specs=[blk] * 4,
        out_shape=[jax.ShapeDtypeStruct((R, C), F32)] * 4,
        compiler_params=_params("parallel"), name=name,
    )(parts, w, m, v)


def _me():
    x, y, c = lax.axis_index("x"), lax.axis_index("y"), lax.axis_index("c")
    return x, y, c


def _peer(d):
    x, y, c = _me()
    px = 1 - x if d & 4 else x
    py = 1 - y if d & 2 else y
    pc = 1 - c if d & 1 else c
    return (px, py, pc), 4 * px + 2 * py + pc


def _gather_two_level(arrs, name):
    n = len(arrs)

    def body(*refs):
        ins, outs = refs[:n], refs[n:2 * n]
        ssem, rsem, lsem = refs[2 * n:]
        x, y, c = _me()
        me, sibling = (x, y, c), (x, y, 1 - c)
        chips = [(1 - x, y), (x, 1 - y), (1 - x, 1 - y)]

        def slot(a, block):
            px, py, pc = block
            return outs[a].at[4 * px + 2 * py + pc]

        def copy(a, k, block, to, src=None):
            return pltpu.make_async_remote_copy(
                src_ref=slot(a, block) if src is None else src, dst_ref=slot(a, block),
                send_sem=ssem.at[a, k], recv_sem=rsem.at[a, k], device_id=to, device_id_type=MESH)

        mine = [pltpu.make_async_copy(ins[a], slot(a, me), lsem.at[a]) for a in range(n)]
        for cp in mine:
            cp.start()
        first = []
        for a in range(n):
            first.append(copy(a, 0, me, sibling, src=ins[a]))
            first += [copy(a, 1 + j, me, (*chip, c), src=ins[a]) for j, chip in enumerate(chips)]
        for cp in first:
            cp.start()
        passed = [[copy(a, 4 + j, (*chip, c), sibling) for j, chip in enumerate(chips)] for a in range(n)]
        for j, chip in enumerate(chips):
            for a in range(n):
                copy(a, 1 + j, (*chip, c), me).wait_recv()
                passed[a][j].start()
        for a in range(n):
            copy(a, 0, sibling, me).wait_recv()
            for j, chip in enumerate(chips):
                copy(a, 4 + j, (*chip, 1 - c), me).wait_recv()
        for cp in first + [cp for row in passed for cp in row]:
            cp.wait_send()
        for cp in mine:
            cp.wait()

    anyspec = pl.BlockSpec(memory_space=pl.ANY)
    return pl.pallas_call(
        body,
        in_specs=[anyspec] * n, out_specs=[anyspec] * n,
        out_shape=_exchange_out_shapes(arrs, [True] * n),
        scratch_shapes=_exchange_semaphores(n),
        name=name,
    )(*arrs)


def _exchange(arrs, bcast, name):
    n = len(arrs)

    def body(*refs):
        ex = _Exchange(refs[:n], refs[n:2 * n], bcast, *refs[2 * n:])
        ex.begin()
        ex.finish()

    anyspec = pl.BlockSpec(memory_space=pl.ANY)
    return pl.pallas_call(
        body,
        in_specs=[anyspec] * n, out_specs=[anyspec] * n,
        out_shape=_exchange_out_shapes(arrs, bcast),
        scratch_shapes=_exchange_semaphores(n),
        name=name,
    )(*arrs)


def _exchange_out_shapes(arrs, bcast):
    return [jax.ShapeDtypeStruct((N_DEV,) + (a.shape if b else a.shape[1:]), a.dtype) for a, b in zip(arrs, bcast)]


def _exchange_semaphores(n):
    return [pltpu.SemaphoreType.DMA((n, N_DEV - 1)), pltpu.SemaphoreType.DMA((n, N_DEV - 1)),
            pltpu.SemaphoreType.DMA((n,))]


class _Exchange:
    def __init__(self, ins, outs, bcast, ssem, rsem, lsem):
        n = len(ins)
        x, y, c = _me()
        me = 4 * x + 2 * y + c

        def src(a, dest):
            return ins[a] if bcast[a] else ins[a].at[dest]

        self.local = [pltpu.make_async_copy(src(a, me), outs[a].at[me], lsem.at[a]) for a in range(n)]
        self.sends, self.recvs = [], []
        for a in range(n):
            for d in range(1, N_DEV):
                peer, pid = _peer(d)
                self.sends.append(pltpu.make_async_remote_copy(
                    src_ref=src(a, pid), dst_ref=outs[a].at[me], send_sem=ssem.at[a, d - 1],
                    recv_sem=rsem.at[a, d - 1], device_id=peer, device_id_type=MESH))
                self.recvs.append(pltpu.make_async_remote_copy(
                    src_ref=src(a, pid), dst_ref=outs[a].at[pid], send_sem=ssem.at[a, d - 1],
                    recv_sem=rsem.at[a, d - 1], device_id=peer, device_id_type=MESH))

    def begin(self):
        for cp in self.local + self.sends:
            cp.start()

    def finish(self):
        for cp in self.recvs:
            cp.wait_recv()
        for cp in self.sends:
            cp.wait_send()
        for cp in self.local:
            cp.wait()


def _to_rows(cols, chunk):
    T, H = cols.shape
    return cols.T.reshape(H, T // chunk, chunk)


def _from_rows(rows):
    return rows.T


def _pad_cols(a, width):
    return jnp.pad(a, ((0, 0), (0, width - a.shape[1])))


def _local_step(x, tgt, p, late_weights=None, early_grads=None, late_grads=None):
    T = x.shape[0]
    zb = lambda n: jnp.zeros((1, n), F32)
    gw = p["gdn_w_in"]
    g_wparts = [gw[:, 0:1024], gw[:, 1024:2048], gw[:, 2048:4096], gw[:, 4096:6144], _pad_cols(gw[:, 6144:6176], PAD_W)]
    nw0, nw1 = p["norm_w"][0:1], p["norm_w"][1:2]
    gcw = p["gdn_conv_w"]
    cw_q, cw_k, cw_v = gcw[:, 0:1024], gcw[:, 1024:2048], gcw[:, 2048:4096]
    g_convs = [(cw_q, zb(1024), True, GDN_DK ** -0.5), (cw_k, zb(1024), True, 1.0), (cw_v, zb(2048), False, 1.0),
               None, None]
    if late_weights is None:
        h0, (q_pre, k_pre, v_pre, z0, ab), (q, k, v), g_cpre = _norm_inproj(x, nw0, g_wparts, g_convs, "gdn_inproj")
    else:
        comm, assemble = late_weights
        h0, (q_pre, k_pre, v_pre, z0, ab), (q, k, v), g_cpre, gathered = _norm_inproj(x, nw0, g_wparts, g_convs,
                                                                                      "gdn_inproj", comm)
        p = dict(p, **assemble(gathered))
    braw = _to_rows(ab[:, 0:GDN_HV], GDN_CHUNK)
    araw = _to_rows(ab[:, GDN_HV:2 * GDN_HV], GDN_CHUNK)
    g_alog, g_dtb = p["gdn_a_log"].reshape(-1), p["gdn_dt_bias"].reshape(-1)
    g_u, g_w, g_pm, g_ti, g_rows, beta_rows, gc_rows, g_qd, g_kd = _gdn_prep(q, k, v, araw, braw, g_alog, g_dtb,
                                                                             "gdn_prep")
    o0, g_vn, g_sall = _gdn_state_fwd(g_qd, g_kd, g_u, g_w, g_pm, gc_rows, "gdn_state_fwd")
    x1 = _out_fwd(o0, z0, p["gdn_norm_w"], p["gdn_w_out"], x, GDN_DK, False, "gdn_out")
    sw = p["ssd_w_in"]
    s_wparts = [sw[:, 0:2048], sw[:, 2048:4096], sw[:, 4096:5120], sw[:, 5120:6144], _pad_cols(sw[:, 6144:6176], PAD_W)]
    scw, scb = p["ssd_conv_w"], p["ssd_conv_b"]
    s_convs = [None, (scw[:, 0:2048], scb[:, 0:2048], False, 1.0), (scw[:, 2048:3072], scb[:, 2048:3072], False, 1.0),
               (scw[:, 3072:4096], scb[:, 3072:4096], False, 1.0), None]
    h1, (z1, xs_pre, b_pre, c_pre, dtp), (xs, bm, cm), s_cpre = _norm_inproj(x1, nw1, s_wparts, s_convs, "ssd_inproj")
    dtraw = _to_rows(dtp[:, 0:SSD_H], SSD_CHUNK)
    s_alog, s_dtb, s_d = p["ssd_a_log"].reshape(-1), p["ssd_dt_bias"].reshape(-1), p["ssd_d"].reshape(-1)
    y1, s_sall, dt_rows = _ssd_scan_fwd(xs, bm, cm, dtraw, s_alog, s_dtb, s_d, "ssd_scan_fwd")
    x2 = _out_fwd(y1, z1, p["ssd_norm_w"], p["ssd_w_out"], x1, D_INNER // SSD_G, True, "ssd_out")
    dx2, d_fw, loss = _final_loss(x2, p["final_norm_w"].reshape(1, -1), tgt, "final_loss")
    dy1, dz1, d_snw, yn1 = _out_bwd(dx2, y1, z1, p["ssd_norm_w"], p["ssd_w_out"], D_INNER // SSD_G, True, "ssd_out_bwd")
    d_swout = _matmul_tn(yn1, dx2, "ssd_wout_grad")
    dxs, dbm, dcm, da_rows, ddt_rows, dd_rows = _ssd_scan_bwd(xs, bm, cm, dt_rows, s_sall, dy1, s_alog, s_d, "ssd_scan_bwd")
    col = lambda a: a.reshape(-1, 1)
    dtraw_g, d_salog, d_sdtb, d_sd = _ssd_gate_bwd(
        dtraw.reshape(SSD_H, T), dt_rows.reshape(SSD_H, T), da_rows.reshape(SSD_H, T),
        ddt_rows.reshape(SSD_H, T), dd_rows.reshape(SSD_H, T), col(s_alog), col(s_dtb), "ssd_gate_bwd")
    dxs_pre, dcw_x, dcb_x = _conv_bwd(xs_pre, s_cpre[0], scw[:, 0:2048], dxs, False, 1.0, "ssd_conv_x_bwd")
    db_pre, dcw_b, dcb_b = _conv_bwd(b_pre, s_cpre[1], scw[:, 2048:3072], dbm, False, 1.0, "ssd_conv_b_bwd")
    dc_pre, dcw_c, dcb_c = _conv_bwd(c_pre, s_cpre[2], scw[:, 3072:4096], dcm, False, 1.0, "ssd_conv_c_bwd")
    ddtp = _pad_cols(_from_rows(dtraw_g), PAD_W)
    s_dparts = [dz1, dxs_pre, db_pre, dc_pre, ddtp]
    dx1, d_nw1 = _inproj_bwd(x1, nw1, s_dparts, s_wparts, dx2, "ssd_inproj_bwd")
    s_dw = [_matmul_tn(h1, d, "ssd_win_grad_%d" % n) for n, d in enumerate(s_dparts)]
    d_swin = jnp.concatenate(s_dw[:4] + [s_dw[4][:, 0:SSD_H]], axis=1)
    early_recv = None
    if early_grads is None:
        do0, dz0, d_gnw, yn0 = _out_bwd(dx1, o0, z0, p["gdn_norm_w"], p["gdn_w_out"], GDN_DK, False, "gdn_out_bwd")
    else:
        do0, dz0, d_gnw, yn0, early_recv = _out_bwd(dx1, o0, z0, p["gdn_norm_w"], p["gdn_w_out"], GDN_DK, False,
                                                    "gdn_out_bwd", early_grads(d_swin, d_swout))
    d_gwout = _matmul_tn(yn0, dx1, "gdn_wout_grad")
    g_dvn, g_dkd, g_dgl = _gdn_state_bwd(g_qd, g_kd, g_w, g_pm, g_vn, g_sall, gc_rows, do0, "gdn_state_bwd")
    dq, dk, dv, dg_rows, dbeta_rows = _gdn_local_bwd(q, k, v, gc_rows, beta_rows, g_ti, g_u, g_w, g_pm, g_vn, g_sall,
                                                     do0, g_dvn, g_dkd, g_dgl, "gdn_local_bwd")
    da_g, db_g, d_galog, d_gdtb = _gdn_gate_bwd(
        araw.reshape(GDN_HV, T), braw.reshape(GDN_HV, T), dg_rows.reshape(GDN_HV, T),
        dbeta_rows.reshape(GDN_HV, T), col(g_alog), col(g_dtb), "gdn_gate_bwd")
    dq_pre, dcw_q, _ = _conv_bwd(q_pre, g_cpre[0], cw_q, dq, True, GDN_DK ** -0.5, "gdn_conv_q_bwd")
    dk_pre, dcw_k, _ = _conv_bwd(k_pre, g_cpre[1], cw_k, dk, True, 1.0, "gdn_conv_k_bwd")
    dv_pre, dcw_v, _ = _conv_bwd(v_pre, g_cpre[2], cw_v, dv, False, 1.0, "gdn_conv_v_bwd")
    dab = _pad_cols(jnp.concatenate([_from_rows(db_g), _from_rows(da_g)], axis=1), PAD_W)
    g_dparts = [dq_pre, dk_pre, dv_pre, dz0, dab]
    g_dw = [_matmul_tn(h0, d, "gdn_win_grad_%d" % n) for n, d in enumerate(g_dparts)]
    d_gwin = jnp.concatenate(g_dw[:4] + [g_dw[4][:, 0:2 * GDN_HV]], axis=1)
    sharded_grads = {
        "gdn_w_in": d_gwin, "gdn_w_out": d_gwout,
        "gdn_conv_w": jnp.concatenate([dcw_q, dcw_k, dcw_v], axis=1),
        "ssd_conv_w": jnp.concatenate([dcw_x, dcw_b, dcw_c], axis=1),
        "ssd_conv_b": jnp.concatenate([dcb_x, dcb_b, dcb_c], axis=1), "ssd_norm_w": d_snw}
    late_recv = None
    if late_grads is None:
        dx0, d_nw0 = _inproj_bwd(x, nw0, g_dparts, g_wparts, dx1, "gdn_inproj_bwd")
    else:
        dx0, d_nw0, late_recv = _inproj_bwd(x, nw0, g_dparts, g_wparts, dx1, "gdn_inproj_bwd",
                                            late_grads(sharded_grads))
    grads = {
        "norm_w": jnp.concatenate([d_nw0, d_nw1], axis=0),
        "gdn_w_in": d_gwin,
        "gdn_conv_w": jnp.concatenate([dcw_q, dcw_k, dcw_v], axis=1),
        "gdn_a_log": d_galog.reshape(1, -1),
        "gdn_dt_bias": d_gdtb.reshape(1, -1),
        "gdn_norm_w": d_gnw,
        "gdn_w_out": d_gwout,
        "ssd_w_in": d_swin,
        "ssd_conv_w": jnp.concatenate([dcw_x, dcw_b, dcw_c], axis=1),
        "ssd_conv_b": jnp.concatenate([dcb_x, dcb_b, dcb_c], axis=1),
        "ssd_dt_bias": d_sdtb.reshape(1, -1),
        "ssd_a_log": d_salog.reshape(1, -1),
        "ssd_d": d_sd.reshape(1, -1),
        "ssd_norm_w": d_snw,
        "ssd_w_out": d_swout,
        "final_norm_w": d_fw,
    }
    if early_grads is not None:
        return loss, dx0, grads, early_recv, late_recv
    return loss, dx0, grads


WEIGHTS = ["norm_w", "gdn_w_in", "gdn_conv_w", "gdn_a_log", "gdn_dt_bias", "gdn_norm_w", "gdn_w_out", "ssd_w_in",
           "ssd_conv_w", "ssd_conv_b", "ssd_dt_bias", "ssd_a_log", "ssd_d", "ssd_norm_w", "ssd_w_out", "final_norm_w"]
COL_SHARDED = ["gdn_w_in", "ssd_w_in"]
ROW_SHARDED = ["gdn_w_out", "ssd_w_out"]
SMALL_SHARDED = ["gdn_conv_w", "ssd_conv_w", "ssd_conv_b", "ssd_norm_w"]
REPLICATED = ["norm_w", "gdn_a_log", "gdn_dt_bias", "gdn_norm_w", "ssd_dt_bias", "ssd_a_log", "ssd_d", "final_norm_w"]


def _pack(arrs):
    return jnp.concatenate([a.reshape(-1) for a in arrs]).reshape(1, -1)


def _unpack(flat, shapes):
    out, pos = [], 0
    for s in shapes:
        n = 1
        for dim in s:
            n *= dim
        out.append(flat[pos:pos + n].reshape(s))
        pos += n
    return out


def _cols_to_shards(full):
    R, C = full.shape
    return full.reshape(R, N_DEV, C // N_DEV).transpose(1, 0, 2)


def _shards_to_cols(shards):
    n, R, c = shards.shape
    return shards.transpose(1, 0, 2).reshape(R, n * c)


def kernel(x, norm_w, gdn_w_in, gdn_conv_w, gdn_a_log, gdn_dt_bias, gdn_norm_w, gdn_w_out, ssd_w_in, ssd_conv_w, ssd_conv_b, ssd_dt_bias, ssd_a_log, ssd_d, ssd_norm_w, ssd_w_out, final_norm_w, loss_target, m_norm_w, m_gdn_w_in, m_gdn_conv_w, m_gdn_a_log, m_gdn_dt_bias, m_gdn_norm_w, m_gdn_w_out, m_ssd_w_in, m_ssd_conv_w, m_ssd_conv_b, m_ssd_dt_bias, m_ssd_a_log, m_ssd_d, m_ssd_norm_w, m_ssd_w_out, m_final_norm_w, v_norm_w, v_gdn_w_in, v_gdn_conv_w, v_gdn_a_log, v_gdn_dt_bias, v_gdn_norm_w, v_gdn_w_out, v_ssd_w_in, v_ssd_conv_w, v_ssd_conv_b, v_ssd_dt_bias, v_ssd_a_log, v_ssd_d, v_ssd_norm_w, v_ssd_w_out, v_final_norm_w):
    w = dict(norm_w=norm_w, gdn_w_in=gdn_w_in[0], gdn_conv_w=gdn_conv_w[0], gdn_a_log=gdn_a_log,
             gdn_dt_bias=gdn_dt_bias, gdn_norm_w=gdn_norm_w, gdn_w_out=gdn_w_out[0], ssd_w_in=ssd_w_in[0],
             ssd_conv_w=ssd_conv_w[0], ssd_conv_b=ssd_conv_b, ssd_dt_bias=ssd_dt_bias, ssd_a_log=ssd_a_log,
             ssd_d=ssd_d, ssd_norm_w=ssd_norm_w, ssd_w_out=ssd_w_out[0], final_norm_w=final_norm_w.reshape(1, -1))
    m = dict(norm_w=m_norm_w, gdn_w_in=m_gdn_w_in[0], gdn_conv_w=m_gdn_conv_w[0], gdn_a_log=m_gdn_a_log,
             gdn_dt_bias=m_gdn_dt_bias, gdn_norm_w=m_gdn_norm_w, gdn_w_out=m_gdn_w_out[0], ssd_w_in=m_ssd_w_in[0],
             ssd_conv_w=m_ssd_conv_w[0], ssd_conv_b=m_ssd_conv_b, ssd_dt_bias=m_ssd_dt_bias, ssd_a_log=m_ssd_a_log,
             ssd_d=m_ssd_d, ssd_norm_w=m_ssd_norm_w, ssd_w_out=m_ssd_w_out[0], final_norm_w=m_final_norm_w.reshape(1, -1))
    v = dict(norm_w=v_norm_w, gdn_w_in=v_gdn_w_in[0], gdn_conv_w=v_gdn_conv_w[0], gdn_a_log=v_gdn_a_log,
             gdn_dt_bias=v_gdn_dt_bias, gdn_norm_w=v_gdn_norm_w, gdn_w_out=v_gdn_w_out[0], ssd_w_in=v_ssd_w_in[0],
             ssd_conv_w=v_ssd_conv_w[0], ssd_conv_b=v_ssd_conv_b, ssd_dt_bias=v_ssd_dt_bias, ssd_a_log=v_ssd_a_log,
             ssd_d=v_ssd_d, ssd_norm_w=v_ssd_norm_w, ssd_w_out=v_ssd_w_out[0], final_norm_w=v_final_norm_w.reshape(1, -1))
    out_shapes = {n: a.shape for n, a in zip(
        WEIGHTS, [norm_w, gdn_w_in, gdn_conv_w, gdn_a_log, gdn_dt_bias, gdn_norm_w, gdn_w_out, ssd_w_in, ssd_conv_w,
                  ssd_conv_b, ssd_dt_bias, ssd_a_log, ssd_d, ssd_norm_w, ssd_w_out, final_norm_w])}

    small_shapes = [w[n].shape for n in SMALL_SHARDED]
    first = _gather_two_level([_mx(w["gdn_w_in"]), _pack([w[n] for n in SMALL_SHARDED])], "gather_first")
    full = dict(w)
    full["gdn_w_in"] = _shards_to_cols(first[0])
    small_all = [_unpack(first[1][s, 0], small_shapes) for s in range(N_DEV)]
    for idx, n in enumerate(SMALL_SHARDED):
        full[n] = jnp.concatenate([small_all[s][idx] for s in range(N_DEV)], axis=-1)
    late = ["gdn_w_out", "ssd_w_in", "ssd_w_out"]

    def assemble(gathered):
        return {"gdn_w_out": gathered[0].reshape(-1, D_MODEL), "ssd_w_in": _shards_to_cols(gathered[1]),
                "ssd_w_out": gathered[2].reshape(-1, D_MODEL)}

    def early_grads(d_ssd_w_in, d_ssd_w_out):
        return ([_cols_to_shards(d_ssd_w_in).astype(GRAD_WIRE_DTYPE),
                 d_ssd_w_out.reshape(N_DEV, -1, D_MODEL).astype(GRAD_WIRE_DTYPE)], [False] * 2)

    def late_grads(g):
        send_small = jnp.concatenate(
            [_cols_to_shards(g[n]).reshape(N_DEV, -1) for n in SMALL_SHARDED], axis=1)[:, None, :]
        return ([_cols_to_shards(g["gdn_w_in"]).astype(GRAD_WIRE_DTYPE),
                 g["gdn_w_out"].reshape(N_DEV, -1, D_MODEL).astype(GRAD_WIRE_DTYPE), send_small], [False] * 3)

    loss, dx, grads, ssd_recv, gdn_recv = _local_step(
        x[0], loss_target[0], full, (([_mx(w[n]) for n in late], [True] * 3), assemble), early_grads, late_grads)

    rep_shapes = [w[n].shape for n in REPLICATED]
    recv_rep = _exchange([_pack([grads[n] for n in REPLICATED])], [True], "exchange_grads")[0]

    res = {}
    for n, parts in zip(["gdn_w_in", "gdn_w_out", "ssd_w_in", "ssd_w_out"], list(gdn_recv[:2]) + list(ssd_recv)):
        res[n] = _adamw(parts, w[n], m[n], v[n], "adamw_" + n)
    small_res = _adamw(gdn_recv[2], *[_pack([t[n] for n in SMALL_SHARDED]) for t in (w, m, v)], "adamw_small")
    rep_res = _adamw(recv_rep, *[_pack([t[n] for n in REPLICATED]) for t in (w, m, v)], "adamw_replicated")
    for k4 in range(4):
        for n, a in zip(SMALL_SHARDED, _unpack(small_res[k4][0], small_shapes)):
            res.setdefault(n, [None] * 4)[k4] = a
        for n, a in zip(REPLICATED, _unpack(rep_res[k4][0], rep_shapes)):
            res.setdefault(n, [None] * 4)[k4] = a

    loss = lax.psum(loss[0, 0], ("x", "y", "c"))
    outs = [loss, dx[None]]
    for k4 in range(4):
        outs += [res[n][k4].reshape(out_shapes[n]) for n in WEIGHTS]
    return tuple(outs)
```

```python
import jax
import jax.numpy as jnp
from jax import lax
from jax.experimental import pallas as pl
from jax.experimental.pallas import tpu as pltpu

F32 = jnp.float32
MXU_DTYPE = jnp.bfloat16
GRAD_WIRE_DTYPE = jnp.bfloat16
HI = lax.Precision.HIGHEST
EPS = 1e-6
VMEM_LIMIT_BYTES = 56 * 1024 * 1024
N_DEV = 8
MESH = pl.DeviceIdType.MESH

D_MODEL = 1024
CONV_K = 4
GDN_HV = 16
GDN_DK = 128
GDN_CHUNK = 64
SSD_H = 32
SSD_P = 64
SSD_N = 128
SSD_G = 8
SSD_R = SSD_H // SSD_G
SSD_CHUNK = 128
D_INNER = 2048
PAD_W = 128

ADAM_LR = 0.001
ADAM_B1 = 0.9
ADAM_B2 = 0.999
ADAM_EPS = 1e-08
ADAM_WD = 0.01
ADAM_STEP = 10


def _params(*sem):
    return pltpu.CompilerParams(dimension_semantics=sem, vmem_limit_bytes=VMEM_LIMIT_BYTES)


def _mx(a):
    return a.astype(MXU_DTYPE)


def _dot(a, b):
    return jnp.dot(_mx(a), _mx(b), preferred_element_type=F32)


def _dot_nt(a, b):
    return lax.dot_general(_mx(a), _mx(b), (((1,), (1,)), ((), ())), preferred_element_type=F32)


def _dot_tn(a, b):
    return lax.dot_general(_mx(a), _mx(b), (((0,), (0,)), ((), ())), preferred_element_type=F32)


def _dot_hi(a, b):
    return jnp.dot(a, b, precision=HI, preferred_element_type=F32)


def _sigmoid(x):
    return 0.5 * jnp.tanh(0.5 * x) + 0.5


def _silu(x):
    return x * _sigmoid(x)


def _dsilu(x):
    s = _sigmoid(x)
    return s * (1.0 + x * (1.0 - s))


def _softplus(x):
    return jnp.maximum(x, 0.0) + jnp.log1p(jnp.exp(-jnp.abs(x)))


def _col(r, eye):
    return jnp.sum(jnp.where(eye, r, 0.0), axis=1, keepdims=True)


def _row(c, eye):
    return jnp.sum(jnp.where(eye, c, 0.0), axis=0, keepdims=True)


def _col_bcast(r, n):
    return jnp.broadcast_to(r, (n, n)).T


def _masks(n):
    r = lax.broadcasted_iota(jnp.int32, (n, n), 0)
    c = lax.broadcasted_iota(jnp.int32, (n, n), 1)
    return r >= c, r > c, r == c, r, c


def _with_exchange(comm):
    arrs, bcast = comm if comm else ([], [])
    nc = len(arrs)
    anyspec = pl.BlockSpec(memory_space=pl.ANY)

    def wrap(compute, n_in, n_out):
        def body(*refs):
            cin, cout = refs[n_in:n_in + nc], refs[n_in + nc + n_out:n_in + 2 * nc + n_out]
            sems = refs[n_in + 2 * nc + n_out:n_in + 2 * nc + n_out + 3]
            rest = refs[:n_in] + refs[n_in + nc:n_in + nc + n_out] + refs[n_in + 2 * nc + n_out + (3 if nc else 0):]
            if nc:
                @pl.when(pl.program_id(0) == 0)
                def _():
                    _Exchange(cin, cout, bcast, *sems).begin()
            compute(*rest)
            if nc:
                @pl.when(pl.program_id(0) == pl.num_programs(0) - 1)
                def _():
                    _Exchange(cin, cout, bcast, *sems).finish()
        return body

    return dict(arrs=list(arrs), nc=nc, wrap=wrap, in_specs=[anyspec] * nc, out_specs=[anyspec] * nc,
                out_shape=_exchange_out_shapes(arrs, bcast), scratch=_exchange_semaphores(nc) if nc else [])


INPROJ_CONV_STRIP = 256
INPROJ_COL_BLOCK = 512


def _norm_inproj(x, nw, wparts, convs, name, comm=None):
    T = x.shape[0]
    tt = min(T, 256)
    n = len(wparts)
    ck = [k for k in range(n) if convs[k] is not None]
    nconv = len(ck)
    widths = [w.shape[1] for w in wparts]
    conv_blocks = [(k, c0) for k in ck for c0 in range(0, widths[k], INPROJ_COL_BLOCK)]
    ex = _with_exchange(comm)

    def compute(x_ref, nw_ref, *refs):
        w_refs, cw_refs = refs[:n], refs[n:n + 2 * nconv]
        h_ref, o_refs = refs[n + 2 * nconv], refs[n + 2 * nconv + 1:2 * n + 2 * nconv + 1]
        post_refs = refs[2 * n + 2 * nconv + 1:2 * n + 3 * nconv + 1]
        cpre_refs = refs[2 * n + 3 * nconv + 1:2 * n + 4 * nconv + 1]
        p_refs = refs[2 * n + 4 * nconv + 1:]
        xv = x_ref[...]
        r = lax.rsqrt(jnp.mean(xv * xv, axis=-1, keepdims=True) + EPS)
        h = _mx(xv * r * nw_ref[...])
        h_ref[...] = h
        p_of = dict(zip(conv_blocks, p_refs))
        for P in p_refs:
            @pl.when(pl.program_id(0) == 0)
            def _():
                P[0:HALO, :] = jnp.zeros((HALO, P.shape[1]), F32)

        def conv_block(k, c0, cw):
            m = ck.index(k)
            _, _, l2, scale = convs[k]
            cw_ref, cb_ref, out_ref, P = cw_refs[2 * m], cw_refs[2 * m + 1], post_refs[m], p_of[(k, c0)]
            cs = slice(c0, c0 + cw)
            for r0 in range(0, tt, INPROJ_CONV_STRIP):
                rs = slice(r0, r0 + INPROJ_CONV_STRIP)
                acc = cb_ref[:, cs] + cw_ref[0:1, cs] * P[pl.ds(HALO - 3 + r0, INPROJ_CONV_STRIP), :]
                for j in range(1, CONV_K):
                    acc = acc + cw_ref[j:j + 1, cs] * P[pl.ds(HALO - 3 + j + r0, INPROJ_CONV_STRIP), :]
                cpre_refs[m][rs, cs] = acc
                s = _silu(acc)
                if l2:
                    sls = [slice(g0, g0 + GDN_DK) for g0 in range(0, cw, GDN_DK)]
                    rr = [lax.rsqrt(jnp.sum(s[:, sl] * s[:, sl], axis=-1, keepdims=True) + EPS) for sl in sls]
                    for sl, rg in zip(sls, rr):
                        out_ref[rs, c0 + sl.start:c0 + sl.stop] = s[:, sl] * rg * scale
                else:
                    out_ref[rs, cs] = s
            P[0:HALO, :] = P[tt:tt + HALO, :]

        pending = None
        for k in range(n):
            for c0 in range(0, widths[k], INPROJ_COL_BLOCK):
                cw = min(INPROJ_COL_BLOCK, widths[k] - c0)
                pre = jnp.dot(h, w_refs[k][:, c0:c0 + cw], preferred_element_type=F32)
                o_refs[k][:, c0:c0 + cw] = pre
                if convs[k] is not None:
                    p_of[(k, c0)][HALO:HALO + tt, :] = pre
                if pending is not None:
                    conv_block(*pending)
                pending = (k, c0, cw) if convs[k] is not None else None
        if pending is not None:
            conv_block(*pending)

    row = lambda width: pl.BlockSpec((tt, width), lambda i: (i, 0))
    full = lambda a: pl.BlockSpec(a.shape, lambda i: (0, 0))
    once = lambda a: pl.BlockSpec(a.shape, lambda i: (0, 0), pipeline_mode=pl.Buffered(1))
    conv_args = [a for k in ck for a in convs[k][:2]]
    outs = pl.pallas_call(
        ex["wrap"](compute, 2 + n + 2 * nconv, 1 + n + 2 * nconv), grid=(T // tt,),
        in_specs=[row(D_MODEL), full(nw)] + [once(w) for w in wparts] + [full(a) for a in conv_args] + ex["in_specs"],
        out_specs=[row(D_MODEL)] + [row(wd) for wd in widths] + [row(widths[k]) for k in ck + ck] + ex["out_specs"],
        out_shape=[jax.ShapeDtypeStruct((T, D_MODEL), MXU_DTYPE)]
        + [jax.ShapeDtypeStruct((T, wd), F32) for wd in widths]
        + [jax.ShapeDtypeStruct((T, widths[k]), F32) for k in ck + ck] + ex["out_shape"],
        scratch_shapes=ex["scratch"] + [pltpu.VMEM((HALO + tt, min(INPROJ_COL_BLOCK, widths[k] - c0)), F32)
                                        for k, c0 in conv_blocks],
        compiler_params=_params("arbitrary"), name=name,
    )(x, nw, *wparts, *conv_args, *ex["arrs"])
    outs = list(outs)
    res = (outs[0], outs[1:1 + n], outs[1 + n:1 + n + nconv], outs[1 + n + nconv:1 + n + 2 * nconv])
    return res + (outs[1 + n + 2 * nconv:],) if comm else res


def _inproj_bwd(x, nw, dparts, wparts, dres, name, comm=None):
    T = x.shape[0]
    tt = min(T, 512)
    n = len(wparts)
    ex = _with_exchange(comm)

    def body(x_ref, nw_ref, dres_ref, *refs):
        d_refs, w_refs, dx_ref, dnw_ref = refs[:n], refs[n:2 * n], refs[2 * n], refs[2 * n + 1]

        @pl.when(pl.program_id(0) == 0)
        def _():
            dnw_ref[...] = jnp.zeros_like(dnw_ref)

        dh = _dot_nt(d_refs[0][...], w_refs[0][...])
        for d_ref, w_ref in zip(d_refs[1:], w_refs[1:]):
            dh = dh + _dot_nt(d_ref[...], w_ref[...])
        xv = x_ref[...]
        r = lax.rsqrt(jnp.mean(xv * xv, axis=-1, keepdims=True) + EPS)
        xh = xv * r
        dnw_ref[...] += jnp.sum(dh * xh, axis=0, keepdims=True)
        dxn = dh * nw_ref[...]
        dx_ref[...] = dres_ref[...] + r * (dxn - xh * jnp.mean(dxn * xh, axis=-1, keepdims=True))

    row = lambda width: pl.BlockSpec((tt, width), lambda i: (i, 0))
    full = lambda a: pl.BlockSpec(a.shape, lambda i: (0, 0))
    outs = pl.pallas_call(
        ex["wrap"](body, 3 + 2 * n, 2), grid=(T // tt,),
        in_specs=[row(D_MODEL), full(nw), row(D_MODEL)] + [row(d.shape[1]) for d in dparts]
        + [pl.BlockSpec(w.shape, lambda i: (0, 0), pipeline_mode=pl.Buffered(1)) for w in wparts] + ex["in_specs"],
        out_specs=[row(D_MODEL), pl.BlockSpec((1, D_MODEL), lambda i: (0, 0))] + ex["out_specs"],
        out_shape=[jax.ShapeDtypeStruct((T, D_MODEL), F32), jax.ShapeDtypeStruct((1, D_MODEL), F32)]
        + ex["out_shape"],
        scratch_shapes=ex["scratch"],
        compiler_params=_params("arbitrary"), name=name,
    )(x, nw, dres, *dparts, *wparts, *ex["arrs"])
    outs = list(outs)
    return outs[:2] + ([outs[2:]] if comm else [])


def _matmul_tn(a, b, name):
    T, K = a.shape
    N = b.shape[1]
    tt = min(T, 2048)
    tn = min(N, 1024)

    def body(a_ref, b_ref, o_ref):
        @pl.when(pl.program_id(1) == 0)
        def _():
            o_ref[...] = jnp.zeros_like(o_ref)

        o_ref[...] += _dot_tn(a_ref[...], b_ref[...])

    return pl.pallas_call(
        body, grid=(N // tn, T // tt),
        in_specs=[pl.BlockSpec((tt, K), lambda n, t: (t, 0)), pl.BlockSpec((tt, tn), lambda n, t: (t, n))],
        out_specs=pl.BlockSpec((K, tn), lambda n, t: (0, n)),
        out_shape=jax.ShapeDtypeStruct((K, N), F32),
        compiler_params=_params("parallel", "arbitrary"), name=name,
    )(a, b)


OUT_COL_BLOCK = 512


def _out_fwd(o, z, w, wout, xres, gs, gate_first, name, final=None):
    T = o.shape[0]
    tt = min(T, 512)
    nT = T // tt
    wide = w.shape[1] == D_INNER

    def body(o_ref, z_ref, w_ref, wout_ref, x_ref, *refs):
        if final is None:
            out_ref, yn = refs
        else:
            fw_ref, t_ref, dx_ref, dfw_ref, loss_ref, yn, lacc = refs
        acc = x_ref[...]
        pending = None
        for b0 in range(0, D_INNER, OUT_COL_BLOCK):
            for g0 in range(b0, b0 + OUT_COL_BLOCK, gs):
                sl = slice(g0, g0 + gs)
                og, zg = o_ref[:, sl], z_ref[:, sl]
                wg = w_ref[:, sl] if wide else w_ref[...]
                if gate_first:
                    u = og * _silu(zg)
                    r = lax.rsqrt(jnp.mean(u * u, axis=-1, keepdims=True) + EPS)
                    yn[:, sl] = _mx(u * r * wg)
                else:
                    r = lax.rsqrt(jnp.mean(og * og, axis=-1, keepdims=True) + EPS)
                    yn[:, sl] = _mx(og * r * wg * _silu(zg))
            if pending is not None:
                acc = acc + jnp.dot(yn[:, pending], wout_ref[pending, :], preferred_element_type=F32)
            pending = slice(b0, b0 + OUT_COL_BLOCK)
        xv = acc + jnp.dot(yn[:, pending], wout_ref[pending, :], preferred_element_type=F32)
        if final is None:
            out_ref[...] = xv
            return
        i = pl.program_id(0)

        @pl.when(i == 0)
        def _():
            dfw_ref[...] = jnp.zeros_like(dfw_ref)
            lacc[...] = jnp.zeros_like(lacc)

        r = lax.rsqrt(jnp.mean(xv * xv, axis=-1, keepdims=True) + EPS)
        xh = xv * r
        err = xh * fw_ref[...] - t_ref[...]
        lacc[...] += jnp.sum(err * err, axis=0, keepdims=True)
        dout = err * (1.0 / D_MODEL)
        dfw_ref[...] += jnp.sum(dout * xh, axis=0, keepdims=True)
        dxn = dout * fw_ref[...]
        dx_ref[...] = r * (dxn - xh * jnp.mean(dxn * xh, axis=-1, keepdims=True))

        @pl.when(i == nT - 1)
        def _():
            loss_ref[...] = (0.5 / D_MODEL) * jnp.sum(lacc[...], axis=1, keepdims=True)

    row = lambda width: pl.BlockSpec((tt, width), lambda i: (i, 0))
    full = lambda a: pl.BlockSpec(a.shape, lambda i: (0, 0))
    if final is None:
        return pl.pallas_call(
            body, grid=(nT,),
            in_specs=[row(D_INNER), row(D_INNER), full(w), full(wout), row(D_MODEL)],
            out_specs=row(D_MODEL),
            out_shape=jax.ShapeDtypeStruct((T, D_MODEL), F32),
            scratch_shapes=[pltpu.VMEM((tt, D_INNER), MXU_DTYPE)],
            compiler_params=_params("parallel"), name=name,
        )(o, z, w, wout, xres)
    vec = pl.BlockSpec((1, D_MODEL), lambda i: (0, 0))
    return pl.pallas_call(
        body, grid=(nT,),
        in_specs=[row(D_INNER), row(D_INNER), full(w), full(wout), row(D_MODEL), vec, row(D_MODEL)],
        out_specs=[row(D_MODEL), vec, pl.BlockSpec((1, 1), lambda i: (0, 0))],
        out_shape=[jax.ShapeDtypeStruct((T, D_MODEL), F32), jax.ShapeDtypeStruct((1, D_MODEL), F32),
                   jax.ShapeDtypeStruct((1, 1), F32)],
        scratch_shapes=[pltpu.VMEM((tt, D_INNER), MXU_DTYPE), pltpu.VMEM((1, D_MODEL), F32)],
        compiler_params=_params("arbitrary"), name=name,
    )(o, z, w, wout, xres, *final)


def _out_bwd(dx, o, z, w, wout, gs, gate_first, name, comm=None):
    T = o.shape[0]
    tt = min(T, 256)
    wide = w.shape[1] == D_INNER

    def body(dx_ref, o_ref, z_ref, w_ref, wout_ref, do_ref, dz_ref, dw_ref, yn_ref):
        @pl.when(pl.program_id(0) == 0)
        def _():
            dw_ref[...] = jnp.zeros_like(dw_ref)

        dxb = _mx(dx_ref[...])
        blocks = list(range(0, D_INNER, OUT_COL_BLOCK))
        dyn_b = {b0: _dot_nt(dxb, wout_ref[b0:b0 + OUT_COL_BLOCK, :]) for b0 in blocks[:1]}
        dw_acc = jnp.zeros((1, gs), F32)
        for g0 in range(0, D_INNER, gs):
            b0 = g0 - g0 % OUT_COL_BLOCK
            if g0 == b0 and b0 + OUT_COL_BLOCK < D_INNER:
                nb = b0 + OUT_COL_BLOCK
                dyn_b[nb] = _dot_nt(dxb, wout_ref[nb:nb + OUT_COL_BLOCK, :])
            sl = slice(g0, g0 + gs)
            og, zg, dg = o_ref[:, sl], z_ref[:, sl], dyn_b[b0][:, g0 - b0:g0 - b0 + gs]
            wg = w_ref[:, sl] if wide else w_ref[...]
            sz = _silu(zg)
            if gate_first:
                u = og * sz
                r = lax.rsqrt(jnp.mean(u * u, axis=-1, keepdims=True) + EPS)
                uh = u * r
                yn_ref[:, sl] = _mx(uh * wg)
                dw_g = jnp.sum(dg * uh, axis=0, keepdims=True)
                duh = dg * wg
                du = r * (duh - uh * jnp.mean(duh * uh, axis=-1, keepdims=True))
                do_ref[:, sl] = du * sz
                dz_ref[:, sl] = _mx(du * og * _dsilu(zg))
            else:
                r = lax.rsqrt(jnp.mean(og * og, axis=-1, keepdims=True) + EPS)
                oh = og * r
                yn_ref[:, sl] = _mx(oh * wg * sz)
                dw_g = jnp.sum(dg * oh * sz, axis=0, keepdims=True)
                doh = dg * wg * sz
                dz_ref[:, sl] = _mx(dg * oh * wg * _dsilu(zg))
                do_ref[:, sl] = r * (doh - oh * jnp.mean(doh * oh, axis=-1, keepdims=True))
            if wide:
                dw_ref[:, sl] += dw_g
            else:
                dw_acc = dw_acc + dw_g
        if not wide:
            dw_ref[...] += dw_acc

    row = lambda width: pl.BlockSpec((tt, width), lambda i: (i, 0))
    full = lambda a: pl.BlockSpec(a.shape, lambda i: (0, 0))
    ex = _with_exchange(comm)
    outs = pl.pallas_call(
        ex["wrap"](body, 5, 4), grid=(T // tt,),
        in_specs=[row(D_MODEL), row(D_INNER), row(D_INNER), full(w), full(wout)] + ex["in_specs"],
        out_specs=[row(D_INNER), row(D_INNER), full(w), row(D_INNER)] + ex["out_specs"],
        out_shape=[jax.ShapeDtypeStruct((T, D_INNER), F32), jax.ShapeDtypeStruct((T, D_INNER), MXU_DTYPE),
                   jax.ShapeDtypeStruct(w.shape, F32), jax.ShapeDtypeStruct((T, D_INNER), MXU_DTYPE)]
        + ex["out_shape"],
        scratch_shapes=ex["scratch"],
        compiler_params=_params("arbitrary"), name=name,
    )(dx, o, z, w, wout, *ex["arrs"])
    outs = list(outs)
    return outs[:4] + ([outs[4:]] if comm else [])


HALO = 8
CONV_STRIP = 32


def _conv_bwd(pre, cpre_all, w, dpost, l2, scale, name):
    T, C = pre.shape
    tt = min(T, 1024)
    tc = min(C, 1024 if l2 else 512)
    strip = 2 * CONV_STRIP if l2 else CONV_STRIP
    nT = T // tt
    ext = tt + HALO

    def body(pre_ref, cp_ref, cn_ref, dpost_ref, dn_ref, w_ref, dpre_ref, dw_ref, db_ref, Q):
        i = pl.program_id(1)

        @pl.when(i == 0)
        def _():
            dw_ref[...] = jnp.zeros_like(dw_ref)
            db_ref[...] = jnp.zeros_like(db_ref)

        wj = [w_ref[j:j + 1, :] for j in range(CONV_K)]
        keep_next = jnp.where(i < nT - 1, 1.0, 0.0)
        fold = lambda a: jnp.sum(a.reshape(strip // 8, 8, tc), axis=0)
        dw_acc = [jnp.zeros((8, tc), F32) for _ in range(CONV_K)]
        db_acc = jnp.zeros((8, tc), F32)
        for r0 in list(range(0, tt, strip)) + [tt]:
            n = strip if r0 < tt else HALO
            cpre = cp_ref[r0:r0 + n, :] if r0 < tt else cn_ref[...]
            dy = dpost_ref[r0:r0 + n, :] if r0 < tt else dn_ref[...] * keep_next
            sg = _sigmoid(cpre)
            ds_c = sg * (1.0 + cpre * (1.0 - sg))
            if l2:
                s = cpre * sg
                sls = [slice(g0, g0 + GDN_DK) for g0 in range(0, tc, GDN_DK)]
                rr = [lax.rsqrt(jnp.sum(s[:, sl] * s[:, sl], axis=-1, keepdims=True) + EPS) for sl in sls]
                yh = [s[:, sl] * r for sl, r in zip(sls, rr)]
                pr = [jnp.sum(dy[:, sl] * y, axis=-1, keepdims=True) for sl, y in zip(sls, yh)]
                for sl, r, y, p in zip(sls, rr, yh, pr):
                    Q[r0:r0 + n, sl] = (scale * r) * (dy[:, sl] - y * p) * ds_c[:, sl]
                dyc = Q[r0:r0 + n, :]
            else:
                dyc = dy * ds_c
                Q[r0:r0 + n, :] = dyc
            if r0 < tt:
                db_acc = db_acc + fold(dyc)
        for r0 in range(0, tt, strip):
            xs = pre_ref[r0:r0 + strip, :]
            dpre = jnp.zeros((strip, tc), F32)
            for j in range(CONV_K):
                qj = Q[pl.ds(3 - j + r0, strip), :]
                dpre = dpre + wj[j] * qj
                dw_acc[j] = dw_acc[j] + fold(qj * xs)
            dpre_ref[r0:r0 + strip, :] = _mx(dpre)
        for j in range(CONV_K):
            dw_ref[j:j + 1, :] += jnp.sum(dw_acc[j], axis=0, keepdims=True)
        db_ref[...] += jnp.sum(db_acc, axis=0, keepdims=True)

    tile = pl.BlockSpec((tt, tc), lambda j, i: (i, j))
    nxt = pl.BlockSpec((HALO, tc), lambda j, i: (jnp.minimum((i + 1) * (tt // HALO), T // HALO - 1), j))
    return pl.pallas_call(
        body, grid=(C // tc, nT),
        in_specs=[tile, tile, nxt, tile, nxt, pl.BlockSpec((CONV_K, tc), lambda j, i: (0, j))],
        out_specs=[tile, pl.BlockSpec((CONV_K, tc), lambda j, i: (0, j)), pl.BlockSpec((1, tc), lambda j, i: (0, j))],
        out_shape=[jax.ShapeDtypeStruct((T, C), MXU_DTYPE), jax.ShapeDtypeStruct((CONV_K, C), F32),
                   jax.ShapeDtypeStruct((1, C), F32)],
        scratch_shapes=[pltpu.VMEM((ext, tc), F32)],
        compiler_params=_params("parallel", "arbitrary"), name=name,
    )(pre, cpre_all, cpre_all, dpost, dpost, w)


GDN_LOCKSTEP_CHUNKS = 16
GDN_LOCKSTEP_CHUNKS_BWD = 16
GDN_SCAN_HEADS = 16


def _inv_unit_lower_many(nms, eye, n):
    xs = [jnp.where(eye, 1.0, 0.0) - nm for nm in nms]
    ps = list(nms)
    k = 2
    while k < n:
        ps = [_dot(p, p) for p in ps]
        xs = [x + _dot(x, p) for x, p in zip(xs, ps)]
        k *= 2
    return xs


def _gdn_prep(q, k, v, araw, braw, alog, dtb, name):
    T = q.shape[0]
    C = GDN_CHUNK
    tt = min(T, 1024)
    cpt, nC = tt // C, T // C
    grp = min(cpt, GDN_LOCKSTEP_CHUNKS)

    def body(alog_ref, dtb_ref, q_ref, k_ref, v_ref, a_ref, b_ref,
             u_ref, w_ref, pm_ref, ti_ref, g_ref, beta_ref, gc_ref, qd_ref, kd_ref):
        j = pl.program_id(0)
        tri, strict, eye, r_i, c_i = _masks(C)
        upper = jnp.where(r_i <= c_i, 1.0, 0.0)
        gcs, bts = [], []
        for hh in range(2):
            h = 2 * j + hh
            g = -jnp.exp(alog_ref[h]) * _softplus(a_ref[hh] + dtb_ref[h])
            bt = _sigmoid(b_ref[hh])
            gc = _dot_hi(g, upper)
            g_ref[hh], beta_ref[hh], gc_ref[hh] = g, bt, gc
            gcs.append(gc)
            bts.append(bt)
        for c0 in range(0, cpt, grp):
            cs = list(range(c0, c0 + grp))
            inst = [(c, hh) for c in cs for hh in range(2)]
            rows = {c: slice(c * C, (c + 1) * C) for c in cs}
            qc = {c: q_ref[rows[c], :] for c in cs}
            kc = {c: k_ref[rows[c], :] for c in cs}
            kk = {c: _dot_nt(kc[c], kc[c]) for c in cs}
            qk = {c: _dot_nt(qc[c], kc[c]) for c in cs}
            gcr = [gcs[hh][c:c + 1, :] for c, hh in inst]
            gcc = [_col(r, eye) for r in gcr]
            bc = [_col(bts[hh][c:c + 1, :], eye) for c, hh in inst]
            lm = [jnp.exp(jnp.where(tri, cc - r, -1e30)) for cc, r in zip(gcc, gcr)]
            nm = [jnp.where(strict, kk[c] * b * l, 0.0) for (c, hh), b, l in zip(inst, bc, lm)]
            tinv = _inv_unit_lower_many(nm, eye, C)
            e_c = [jnp.exp(cc) for cc in gcc]
            rhs = [jnp.concatenate([v_ref[rows[c], hh * GDN_DK:(hh + 1) * GDN_DK] * b, kc[c] * (b * e)], axis=1)
                   for (c, hh), b, e in zip(inst, bc, e_c)]
            sol = [_dot(t, r) for t, r in zip(tinv, rhs)]
            for (c, hh), s, t, l, e, cc, r in zip(inst, sol, tinv, lm, e_c, gcc, gcr):
                hs = slice(hh * GDN_DK, (hh + 1) * GDN_DK)
                u_ref[rows[c], hs] = s[:, :GDN_DK]
                w_ref[rows[c], hs] = _mx(s[:, GDN_DK:])
                pm_ref[hh, c] = _mx(jnp.where(tri, qk[c] * l, 0.0))
                ti_ref[hh, c] = _mx(t)
                qd_ref[rows[c], hs] = _mx(qc[c] * e)
                kd_ref[rows[c], hs] = _mx(kc[c] * jnp.exp(r[:, C - 1:C] - cc))

    smem = pl.BlockSpec(memory_space=pltpu.SMEM)
    rows_spec = pl.BlockSpec((2, cpt, C), lambda j, i: (j, i, 0))
    qk_spec = pl.BlockSpec((tt, GDN_DK), lambda j, i: (i, j))
    v_spec = pl.BlockSpec((tt, 2 * GDN_DK), lambda j, i: (i, j))
    cc_spec = pl.BlockSpec((2, cpt, C, C), lambda j, i: (j, i, 0, 0))
    rows_shape = jax.ShapeDtypeStruct((GDN_HV, nC, C), F32)
    cc_shape = jax.ShapeDtypeStruct((GDN_HV, nC, C, C), MXU_DTYPE)
    return pl.pallas_call(
        body, grid=(GDN_HV // 2, T // tt),
        in_specs=[smem, smem, qk_spec, qk_spec, v_spec, rows_spec, rows_spec],
        out_specs=[v_spec, v_spec, cc_spec, cc_spec, rows_spec, rows_spec, rows_spec, v_spec, v_spec],
        out_shape=[jax.ShapeDtypeStruct((T, D_INNER), F32), jax.ShapeDtypeStruct((T, D_INNER), MXU_DTYPE),
                   cc_shape, cc_shape, rows_shape, rows_shape, rows_shape,
                   jax.ShapeDtypeStruct((T, D_INNER), MXU_DTYPE), jax.ShapeDtypeStruct((T, D_INNER), MXU_DTYPE)],
        compiler_params=_params("parallel", "parallel"), name=name,
    )(alog, dtb, q, k, v, araw, braw)


def _gdn_state_fwd(q, k, u, w, pm, gc, name):
    T = q.shape[0]
    C = GDN_CHUNK
    HG = GDN_SCAN_HEADS
    tt = min(T, 512)
    cpt, nC = tt // C, T // C

    def body(q_ref, k_ref, u_ref, w_ref, pm_ref, gc_ref, o_ref, vn_ref, sall_ref, S):
        @pl.when(pl.program_id(1) == 0)
        def _():
            S[...] = jnp.zeros_like(S)

        heads = list(range(HG))

        def chunk(c, carry):
            rows = pl.ds(pl.multiple_of(c * C, C), C)
            hs = [slice(h * GDN_DK, (h + 1) * GDN_DK) for h in heads]
            gl = [jnp.exp(gc_ref[h, pl.ds(c, 1), C - 1:C]) for h in heads]
            sv = [S[h] for h in heads]
            for h in heads:
                sall_ref[h, c] = _mx(sv[h])
            ws = [_dot(w_ref[rows, hs[h]], sv[h]) for h in heads]
            qsv = [_dot(q_ref[rows, hs[h]], sv[h]) for h in heads]
            vn = [u_ref[rows, hs[h]] - ws[h] for h in heads]
            pv = [_dot(pm_ref[h, c], vn[h]) for h in heads]
            kv = [_dot_tn(k_ref[rows, hs[h]], vn[h]) for h in heads]
            for h in heads:
                vn_ref[rows, hs[h]] = _mx(vn[h])
                o_ref[rows, hs[h]] = qsv[h] + pv[h]
                S[h] = sv[h] * gl[h] + kv[h]
            return carry

        lax.fori_loop(0, cpt, chunk, 0)

    v_spec = pl.BlockSpec((tt, HG * GDN_DK), lambda g, i: (i, g))
    return pl.pallas_call(
        body, grid=(GDN_HV // HG, T // tt),
        in_specs=[v_spec, v_spec, v_spec, v_spec,
                  pl.BlockSpec((HG, cpt, C, C), lambda g, i: (g, i, 0, 0)),
                  pl.BlockSpec((HG, cpt, C), lambda g, i: (g, i, 0))],
        out_specs=[v_spec, v_spec, pl.BlockSpec((HG, cpt, GDN_DK, GDN_DK), lambda g, i: (g, i, 0, 0))],
        out_shape=[jax.ShapeDtypeStruct((T, D_INNER), F32), jax.ShapeDtypeStruct((T, D_INNER), MXU_DTYPE),
                   jax.ShapeDtypeStruct((GDN_HV, nC, GDN_DK, GDN_DK), MXU_DTYPE)],
        scratch_shapes=[pltpu.VMEM((HG, GDN_DK, GDN_DK), F32)],
        compiler_params=_params("parallel", "arbitrary"), name=name,
    )(q, k, u, w, pm, gc)


def _gdn_state_bwd(q, k, w, pm, vn, sall, gc, do, name):
    T = q.shape[0]
    C = GDN_CHUNK
    HG = GDN_SCAN_HEADS
    tt = min(T, 512)
    cpt, nC, nT = tt // C, T // C, T // tt

    def body(q_ref, k_ref, w_ref, pm_ref, vn_ref, sall_ref, gc_ref, do_ref, dvn_ref, dkd_ref, dgl_ref, dS):
        @pl.when(pl.program_id(1) == 0)
        def _():
            dS[...] = jnp.zeros_like(dS)

        heads = list(range(HG))

        def chunk(ci, carry):
            c = cpt - 1 - ci
            rows = pl.ds(pl.multiple_of(c * C, C), C)
            hs = [slice(h * GDN_DK, (h + 1) * GDN_DK) for h in heads]
            gl = [jnp.exp(gc_ref[h, pl.ds(c, 1), C - 1:C]) for h in heads]
            dsn = [dS[h] for h in heads]
            doc = [do_ref[rows, hs[h]] for h in heads]
            kds = [_dot(k_ref[rows, hs[h]], dsn[h]) for h in heads]
            pdo = [_dot_tn(pm_ref[h, c], doc[h]) for h in heads]
            dkd = [_dot_nt(vn_ref[rows, hs[h]], dsn[h]) for h in heads]
            qdo = [_dot_tn(q_ref[rows, hs[h]], doc[h]) for h in heads]
            dvn = [pdo[h] + kds[h] for h in heads]
            wdv = [_dot_tn(w_ref[rows, hs[h]], dvn[h]) for h in heads]
            for h in heads:
                dgl = jnp.sum(jnp.sum(dsn[h] * sall_ref[h, c].astype(F32), axis=0, keepdims=True), axis=1, keepdims=True)
                dgl_ref[h, pl.ds(c, 1), :] = jnp.broadcast_to(dgl, (1, C))
                dvn_ref[rows, hs[h]] = dvn[h]
                dkd_ref[rows, hs[h]] = dkd[h]
                dS[h] = dsn[h] * gl[h] + qdo[h] - wdv[h]
            return carry

        lax.fori_loop(0, cpt, chunk, 0)

    rev = lambda i: nT - 1 - i
    v_spec = pl.BlockSpec((tt, HG * GDN_DK), lambda g, i: (rev(i), g))
    rows_spec = pl.BlockSpec((HG, cpt, C), lambda g, i: (g, rev(i), 0))
    return pl.pallas_call(
        body, grid=(GDN_HV // HG, nT),
        in_specs=[v_spec, v_spec, v_spec, pl.BlockSpec((HG, cpt, C, C), lambda g, i: (g, rev(i), 0, 0)), v_spec,
                  pl.BlockSpec((HG, cpt, GDN_DK, GDN_DK), lambda g, i: (g, rev(i), 0, 0)), rows_spec, v_spec],
        out_specs=[v_spec, v_spec, rows_spec],
        out_shape=[jax.ShapeDtypeStruct((T, D_INNER), F32), jax.ShapeDtypeStruct((T, D_INNER), F32),
                   jax.ShapeDtypeStruct((GDN_HV, nC, C), F32)],
        scratch_shapes=[pltpu.VMEM((HG, GDN_DK, GDN_DK), F32)],
        compiler_params=_params("parallel", "arbitrary"), name=name,
    )(q, k, w, pm, vn, sall, gc, do)


def _gdn_local_bwd(q, k, v, gc, beta, tinv, u, w, pm, vn, sall, do, dvn, dkd, dgl, name):
    T = q.shape[0]
    C = GDN_CHUNK
    tt = min(T, 1024)
    cpt, nC = tt // C, T // C
    grp = min(cpt, GDN_LOCKSTEP_CHUNKS_BWD)

    def body(q_ref, k_ref, v_ref, gc_ref, b_ref, ti_ref, u_ref, w_ref, pm_ref, vn_ref, sall_ref, do_ref,
             dvn_ref, dkd_ref, dgl_ref, dq_ref, dk_ref, dv_ref, dg_ref, dbeta_ref, dgc_s):
        tri, strict, eye, r_i, c_i = _masks(C)
        lower = jnp.where(r_i >= c_i, 1.0, 0.0)
        lane = lax.broadcasted_iota(jnp.int32, (1, C), 1)
        rsum = lambda a: jnp.sum(a, axis=1, keepdims=True)
        for c0 in range(0, cpt, grp):
            cs = list(range(c0, c0 + grp))
            inst = [(c, hh) for c in cs for hh in range(2)]
            n = len(inst)
            rows = {c: slice(c * C, (c + 1) * C) for c in cs}
            hsl = [slice(hh * GDN_DK, (hh + 1) * GDN_DK) for c, hh in inst]
            qc = {c: q_ref[rows[c], :] for c in cs}
            kc = {c: k_ref[rows[c], :] for c in cs}
            kk = {c: _dot_nt(kc[c], kc[c]) for c in cs}
            gcr = [gc_ref[hh, c:c + 1, :] for c, hh in inst]
            gcc = [_col(r, eye) for r in gcr]
            bc = [_col(b_ref[hh, c:c + 1, :], eye) for c, hh in inst]
            lm = [jnp.exp(jnp.where(tri, cc - r, -1e30)) for cc, r in zip(gcc, gcr)]
            e_c = [jnp.exp(cc) for cc in gcc]
            el_c = [jnp.exp(r[:, C - 1:C] - cc) for cc, r in zip(gcc, gcr)]
            gl = [jnp.exp(r[:, C - 1:C]) for r in gcr]
            doc = [do_ref[rows[c], hsl[i]] for i, (c, hh) in enumerate(inst)]
            dvn = [dvn_ref[rows[c], hsl[i]] for i, (c, hh) in enumerate(inst)]
            sv = [sall_ref[hh, c] for c, hh in inst]
            aa = [_dot_nt(jnp.concatenate([_mx(doc[i]), _mx(dvn[i])], axis=0), sv[i]) for i in range(n)]
            dpm = [jnp.where(tri, _dot_nt(doc[i], vn_ref[rows[c], hsl[i]]), 0.0) for i, (c, hh) in enumerate(inst)]
            dqd = [a[:C] for a in aa]
            drhs = [_dot_tn(ti_ref[hh, c], jnp.concatenate([dvn[i], -aa[i][C:]], axis=1))
                    for i, (c, hh) in enumerate(inst)]
            sol = [jnp.concatenate([_mx(u_ref[rows[c], hsl[i]]), w_ref[rows[c], hsl[i]]], axis=1)
                   for i, (c, hh) in enumerate(inst)]
            dnm = [-jnp.where(strict, _dot_nt(drhs[i], sol[i]), 0.0) for i in range(n)]
            dkk = [dnm[i] * bc[i] * lm[i] for i in range(n)]
            dqk = [dpm[i] * lm[i] for i in range(n)]
            dq1 = [_dot(dqk[i], kc[c]) for i, (c, hh) in enumerate(inst)]
            dk1 = [_dot(dkk[i], kc[c]) for i, (c, hh) in enumerate(inst)]
            dk2 = [_dot_tn(dkk[i], kc[c]) for i, (c, hh) in enumerate(inst)]
            dk3 = [_dot_tn(dqk[i], qc[c]) for i, (c, hh) in enumerate(inst)]
            dq_acc = {c: jnp.zeros((C, GDN_DK), F32) for c in cs}
            dk_acc = {c: jnp.zeros((C, GDN_DK), F32) for c in cs}
            for i, (c, hh) in enumerate(inst):
                k_, q_, v_ = kc[c], qc[c], v_ref[rows[c], hsl[i]]
                dvb, dkbe = drhs[i][:, :GDN_DK], drhs[i][:, GDN_DK:]
                dkd = dkd_ref[rows[c], hsl[i]]
                kb = k_ * bc[i]
                dkb = dkbe * e_c[i]
                del_el = dkd * k_ * el_c[i]
                dbc = rsum(dnm[i] * kk[c] * lm[i]) + rsum(dkb * k_ + dvb * v_)
                dq_acc[c] = dq_acc[c] + dq1[i] + dqd[i] * e_c[i]
                dk_acc[c] = dk_acc[c] + dk1[i] + dk2[i] + dk3[i] + dkd * el_c[i] + dkb * bc[i]
                dv_ref[rows[c], hsl[i]] = dvb * bc[i]
                nm = jnp.where(strict, kk[c] * bc[i] * lm[i], 0.0)
                gm = dnm[i] * nm + dpm[i] * pm_ref[hh, c].astype(F32)
                dgc_col = rsum(gm) + rsum((dkbe * kb + dqd[i] * q_) * e_c[i] - del_el)
                dglast = (jnp.sum(jnp.sum(del_el, axis=0, keepdims=True), axis=1, keepdims=True)
                          + dgl_ref[hh, c:c + 1, 0:1] * gl[i])
                dgc_s[hh, c:c + 1, :] = (_row(dgc_col, eye) - jnp.sum(gm, axis=0, keepdims=True)
                                         + jnp.where(lane == C - 1, dglast, 0.0))
                dbeta_ref[hh, c:c + 1, :] = _row(dbc, eye)
            for c in cs:
                dq_ref[rows[c], :] = dq_acc[c]
                dk_ref[rows[c], :] = dk_acc[c]
        for hh in range(2):
            dg_ref[hh] = _dot_hi(dgc_s[hh], lower)

    rows_spec = pl.BlockSpec((2, cpt, C), lambda j, i: (j, i, 0))
    qk_spec = pl.BlockSpec((tt, GDN_DK), lambda j, i: (i, j))
    v_spec = pl.BlockSpec((tt, 2 * GDN_DK), lambda j, i: (i, j))
    cc_spec = pl.BlockSpec((2, cpt, C, C), lambda j, i: (j, i, 0, 0))
    rows_shape = jax.ShapeDtypeStruct((GDN_HV, nC, C), F32)
    return pl.pallas_call(
        body, grid=(GDN_HV // 2, T // tt),
        in_specs=[qk_spec, qk_spec, v_spec, rows_spec, rows_spec, cc_spec, v_spec, v_spec, cc_spec, v_spec,
                  pl.BlockSpec((2, cpt, GDN_DK, GDN_DK), lambda j, i: (j, i, 0, 0)), v_spec, v_spec, v_spec, rows_spec],
        out_specs=[qk_spec, qk_spec, v_spec, rows_spec, rows_spec],
        out_shape=[jax.ShapeDtypeStruct((T, GDN_HV // 2 * GDN_DK), F32),
                   jax.ShapeDtypeStruct((T, GDN_HV // 2 * GDN_DK), F32),
                   jax.ShapeDtypeStruct((T, D_INNER), F32), rows_shape, rows_shape],
        scratch_shapes=[pltpu.VMEM((2, cpt, C), F32)],
        compiler_params=_params("parallel", "parallel"), name=name,
    )(q, k, v, gc, beta, tinv, u, w, pm, vn, sall, do, dvn, dkd, dgl)


def _gdn_gate_bwd(araw, braw, dg, dbeta, alog, dtb, name):
    H, T = araw.shape

    def body(a_ref, b_ref, dg_ref, dbt_ref, alog_ref, dtb_ref, da_ref, db_ref, dalog_ref, ddtb_ref):
        xa = a_ref[...] + dtb_ref[...]
        ea = jnp.exp(alog_ref[...])
        dgv = dg_ref[...]
        da = -dgv * ea * _sigmoid(xa)
        da_ref[...] = da
        dalog_ref[...] = jnp.sum(-dgv * ea * _softplus(xa), axis=1, keepdims=True)
        ddtb_ref[...] = jnp.sum(da, axis=1, keepdims=True)
        bt = _sigmoid(b_ref[...])
        db_ref[...] = dbt_ref[...] * bt * (1.0 - bt)

    return pl.pallas_call(
        body,
        out_shape=[jax.ShapeDtypeStruct((H, T), F32), jax.ShapeDtypeStruct((H, T), F32),
                   jax.ShapeDtypeStruct((H, 1), F32), jax.ShapeDtypeStruct((H, 1), F32)],
        compiler_params=pltpu.CompilerParams(vmem_limit_bytes=VMEM_LIMIT_BYTES), name=name,
    )(araw, braw, dg, dbeta, alog, dtb)


SSD_LOCKSTEP_CHUNKS = 2
SSD_LOCKSTEP_CHUNKS_BWD = 1
SSD_LOCKSTEP_HEADS_BWD = 2


def _ssd_scan_fwd(xs, bm, cm, dtraw, alog, dtb, dskip, name):
    T = xs.shape[0]
    Q = SSD_CHUNK
    tt = min(T, 1024)
    cpt, nC = tt // Q, T // Q
    GW = SSD_R * SSD_P

    def body(alog_ref, dtb_ref, dsk_ref, xs_ref, b_ref, c_ref, dt_ref, y_ref, sall_ref, dto_ref, S, dt_s, acs_s):
        gi, i = pl.program_id(0), pl.program_id(1)

        @pl.when(i == 0)
        def _():
            S[...] = jnp.zeros_like(S)

        tri, _, eye, r_i, c_i = _masks(Q)
        upper = jnp.where(r_i <= c_i, 1.0, 0.0)
        for r in range(SSD_R):
            h = SSD_R * gi + r
            dt = _softplus(dt_ref[r] + dtb_ref[h])
            dto_ref[r] = dt
            dt_s[r] = dt
            acs_s[r] = _dot_hi(-jnp.exp(alog_ref[h]) * dt, upper)

        ps = [slice(r * SSD_P, (r + 1) * SSD_P) for r in range(SSD_R)]
        s_cur = [S[:, ps[r]] for r in range(SSD_R)]
        grp = min(cpt, SSD_LOCKSTEP_CHUNKS)
        for c0 in range(0, cpt, grp):
            cs = list(range(c0, c0 + grp))
            inst = [(c, r) for c in cs for r in range(SSD_R)]
            rows = {c: slice(c * Q, (c + 1) * Q) for c in cs}
            bc_ = {c: b_ref[rows[c], :] for c in cs}
            cc_ = {c: c_ref[rows[c], :] for c in cs}
            cb = {c: _dot_nt(cc_[c], bc_[c]) for c in cs}
            xr = [xs_ref[rows[c], ps[r]] for c, r in inst]
            acr = [acs_s[r, c:c + 1, :] for c, r in inst]
            acc = [_col_bcast(a, Q) for a in acr]
            dtr = [dt_s[r, c:c + 1, :] for c, r in inst]
            mm = [cb[c] * (jnp.exp(jnp.where(tri, acc[i] - acr[i], -1e30)) * dtr[i]) for i, (c, r) in enumerate(inst)]
            bct = {c: bc_[c].T for c in cs}
            st = [_dot(bct[c] * (jnp.exp(acr[i][:, Q - 1:Q] - acr[i]) * dtr[i]), xr[i]) for i, (c, r) in enumerate(inst)]
            yd = [_dot(mm[i], xr[i]) for i in range(len(inst))]
            s_prev = []
            for i, (c, r) in enumerate(inst):
                s_prev.append(s_cur[r])
                s_cur[r] = s_cur[r] * jnp.exp(acr[i][:, Q - 1:Q]) + st[i]
            yo = [_dot(cc_[c] * jnp.exp(acc[i]), s_prev[i]) for i, (c, r) in enumerate(inst)]
            for i, (c, r) in enumerate(inst):
                sall_ref[0, c, :, ps[r]] = s_prev[i]
                y_ref[rows[c], ps[r]] = yd[i] + yo[i] + dsk_ref[SSD_R * gi + r] * xr[i]
        for r in range(SSD_R):
            S[:, ps[r]] = s_cur[r]

    smem = pl.BlockSpec(memory_space=pltpu.SMEM)
    rows_spec = pl.BlockSpec((SSD_R, cpt, Q), lambda g, i: (g, i, 0))
    return pl.pallas_call(
        body, grid=(SSD_G, T // tt),
        in_specs=[smem, smem, smem,
                  pl.BlockSpec((tt, GW), lambda g, i: (i, g)), pl.BlockSpec((tt, SSD_N), lambda g, i: (i, g)),
                  pl.BlockSpec((tt, SSD_N), lambda g, i: (i, g)), rows_spec],
        out_specs=[pl.BlockSpec((tt, GW), lambda g, i: (i, g)),
                   pl.BlockSpec((1, cpt, SSD_N, GW), lambda g, i: (g, i, 0, 0)), rows_spec],
        out_shape=[jax.ShapeDtypeStruct((T, D_INNER), F32), jax.ShapeDtypeStruct((SSD_G, nC, SSD_N, GW), F32),
                   jax.ShapeDtypeStruct((SSD_H, nC, Q), F32)],
        scratch_shapes=[pltpu.VMEM((SSD_N, GW), F32), pltpu.VMEM((SSD_R, cpt, Q), F32),
                        pltpu.VMEM((SSD_R, cpt, Q), F32)],
        compiler_params=_params("parallel", "arbitrary"), name=name,
    )(alog, dtb, dskip, xs, bm, cm, dtraw)


def _ssd_scan_bwd(xs, bm, cm, dt, sall, dy, alog, dskip, name):
    T = xs.shape[0]
    Q = SSD_CHUNK
    tt = min(T, 1024)
    cpt, nC, nT = tt // Q, T // Q, T // tt
    GW = SSD_R * SSD_P

    def body(alog_ref, dsk_ref, xs_ref, b_ref, c_ref, dt_ref, sall_ref, dy_ref,
             dxs_ref, db_ref, dc_ref, da_ref, ddt_ref, dd_ref, dS, acs_s, dacs_s, ddt_s, dd_s):
        gi, i = pl.program_id(0), pl.program_id(1)

        @pl.when(i == 0)
        def _():
            dS[...] = jnp.zeros_like(dS)

        tri, _, eye, r_i, c_i = _masks(Q)
        upper = jnp.where(r_i <= c_i, 1.0, 0.0)
        lower = jnp.where(r_i >= c_i, 1.0, 0.0)
        lane = lax.broadcasted_iota(jnp.int32, (1, Q), 1)
        for r in range(SSD_R):
            acs_s[r] = _dot_hi(-jnp.exp(alog_ref[SSD_R * gi + r]) * dt_ref[r], upper)

        ps = [slice(r * SSD_P, (r + 1) * SSD_P) for r in range(SSD_R)]
        ds_cur = [dS[:, ps[r]] for r in range(SSD_R)]
        grp = min(cpt, SSD_LOCKSTEP_CHUNKS_BWD)
        csum = lambda a: jnp.sum(a, axis=0, keepdims=True)
        tsum = lambda a: jnp.sum(csum(a), axis=1, keepdims=True)
        ones8 = jnp.ones((8, SSD_P), F32)
        for c0 in range(cpt - grp, -1, -grp):
            cs = list(range(c0 + grp - 1, c0 - 1, -1))
            rows = {c: slice(c * Q, (c + 1) * Q) for c in cs}
            bc_ = {c: b_ref[rows[c], :] for c in cs}
            cc_ = {c: c_ref[rows[c], :] for c in cs}
            cb = {c: _dot_nt(cc_[c], bc_[c]) for c in cs}
            cbt = {c: _dot_nt(bc_[c], cc_[c]) for c in cs}
            bct = {c: bc_[c].T for c in cs}
            cct = {c: cc_[c].T for c in cs}
            dcb = {c: jnp.zeros((Q, Q), F32) for c in cs}
            dcbt = {c: jnp.zeros((Q, Q), F32) for c in cs}
            db_acc = {c: jnp.zeros((Q, SSD_N), F32) for c in cs}
            dc_acc = {c: jnp.zeros((Q, SSD_N), F32) for c in cs}
            for h0 in range(0, SSD_R, SSD_LOCKSTEP_HEADS_BWD):
                inst = [(c, r) for c in cs for r in range(h0, h0 + SSD_LOCKSTEP_HEADS_BWD)]
                n = len(inst)
                xr = [xs_ref[rows[c], ps[r]] for c, r in inst]
                dyr = [dy_ref[rows[c], ps[r]] for c, r in inst]
                acr = [acs_s[r, c:c + 1, :] for c, r in inst]
                dtr = [dt_ref[r, c:c + 1, :] for c, r in inst]
                acc = [_col_bcast(a, Q) for a in acr]
                dtb = [_col_bcast(d, Q) for d in dtr]
                al = [a[:, Q - 1:Q] for a in acr]
                e_c = [jnp.exp(a) for a in acc]
                dl_c = [jnp.exp(al[i] - acc[i]) for i in range(n)]
                e_r = [jnp.exp(a) for a in acr]
                dl_r = [jnp.exp(al[i] - acr[i]) for i in range(n)]
                gl = [jnp.exp(a) for a in al]
                lm = [jnp.exp(jnp.where(tri, acc[i] - acr[i], -1e30)) for i in range(n)]
                lmt = [jnp.exp(jnp.where(r_i <= c_i, acr[i] - acc[i], -1e30)) for i in range(n)]
                mmt = [cbt[c] * lmt[i] for i, (c, r) in enumerate(inst)]
                sr = [sall_ref[0, c, :, ps[r]] for c, r in inst]
                dmm0 = [_dot_nt(dyr[i], xr[i]) for i in range(n)]
                dmm0t = [_dot_nt(xr[i], dyr[i]) for i in range(n)]
                dxd1 = [_dot(mmt[i], dyr[i]) for i in range(n)]
                dce = [_dot_nt(dyr[i], sr[i]) for i in range(n)]
                dcet = [_dot_nt(sr[i], dyr[i]) for i in range(n)]
                cdy = [_dot(cct[c] * e_r[i], dyr[i]) for i, (c, r) in enumerate(inst)]
                dsn = []
                for i, (c, r) in enumerate(inst):
                    dsn.append(ds_cur[r])
                    ds_cur[r] = gl[i] * ds_cur[r] + cdy[i]
                dxd = [dxd1[i] + _dot(bc_[c] * dl_c[i], dsn[i]) for i, (c, r) in enumerate(inst)]
                dbd0 = [_dot_nt(xr[i], dsn[i]) for i in range(n)]
                dbd0t = [_dot_nt(dsn[i], xr[i]) for i in range(n)]
                for i, (c, r) in enumerate(inst):
                    dgl = tsum(dsn[i] * sr[i])
                    dc_acc[c] = dc_acc[c] + dce[i] * e_c[i]
                    db_acc[c] = db_acc[c] + dbd0[i] * (dtb[i] * dl_c[i])
                    dl0 = dmm0[i] * lm[i]
                    dl0t = dmm0t[i] * (lmt[i] * dtb[i])
                    dcb[c] = dcb[c] + dl0 * dtr[i]
                    dcbt[c] = dcbt[c] + dl0t
                    csum_gm0 = csum(dl0 * cb[c])
                    rsum_gm = csum(dl0t * cbt[c])
                    r_de = csum(dcet[i] * cct[c]) * e_r[i]
                    r_dl = csum(dbd0t[i] * bct[c]) * dl_r[i]
                    dalast = jnp.sum(r_dl * dtr[i], axis=1, keepdims=True) + dgl * gl[i]
                    dacs_s[r, c:c + 1, :] = (rsum_gm + r_de - (r_dl + csum_gm0) * dtr[i]
                                             + jnp.where(lane == Q - 1, dalast, 0.0))
                    ddt_s[r, c:c + 1, :] = csum_gm0 + r_dl
                    dd_s[r, c:c + 1, :] = _dot_nt(ones8, dyr[i] * xr[i])[0:1]
                    dxs_ref[rows[c], ps[r]] = dxd[i] * dtb[i][:, :SSD_P] + dsk_ref[SSD_R * gi + r] * dyr[i]
            for c in cs:
                dc_ref[rows[c], :] = dc_acc[c] + _dot(dcb[c], bc_[c])
                db_ref[rows[c], :] = db_acc[c] + _dot(dcbt[c], cc_[c])
        for r in range(SSD_R):
            dS[:, ps[r]] = ds_cur[r]
        for r in range(SSD_R):
            da_ref[r] = _dot_hi(dacs_s[r], lower)
            ddt_ref[r] = ddt_s[r]
            dd_ref[r] = dd_s[r]

    rev = lambda i: nT - 1 - i
    smem = pl.BlockSpec(memory_space=pltpu.SMEM)
    rows_spec = pl.BlockSpec((SSD_R, cpt, Q), lambda g, i: (g, rev(i), 0))
    x_spec = pl.BlockSpec((tt, GW), lambda g, i: (rev(i), g))
    n_spec = pl.BlockSpec((tt, SSD_N), lambda g, i: (rev(i), g))
    rows_shape = jax.ShapeDtypeStruct((SSD_H, nC, Q), F32)
    return pl.pallas_call(
        body, grid=(SSD_G, nT),
        in_specs=[smem, smem, x_spec, n_spec, n_spec, rows_spec,
                  pl.BlockSpec((1, cpt, SSD_N, GW), lambda g, i: (g, rev(i), 0, 0)), x_spec],
        out_specs=[x_spec, n_spec, n_spec, rows_spec, rows_spec, rows_spec],
        out_shape=[jax.ShapeDtypeStruct((T, D_INNER), F32), jax.ShapeDtypeStruct((T, SSD_G * SSD_N), F32),
                   jax.ShapeDtypeStruct((T, SSD_G * SSD_N), F32), rows_shape, rows_shape, rows_shape],
        scratch_shapes=[pltpu.VMEM((SSD_N, GW), F32)] + [pltpu.VMEM((SSD_R, cpt, Q), F32)] * 4,
        compiler_params=_params("parallel", "arbitrary"), name=name,
    )(alog, dskip, xs, bm, cm, dt, sall, dy)


def _ssd_gate_bwd(dtraw, dt, da, ddt_direct, ddrow, alog, dtb, name):
    H, T = dtraw.shape

    def body(raw_ref, dt_ref, da_ref, ddt_ref, dd_ref, alog_ref, dtb_ref, draw_ref, dalog_ref, ddtb_ref, dD_ref):
        a = -jnp.exp(alog_ref[...])
        dav = da_ref[...]
        ddt = ddt_ref[...] + dav * a
        draw = ddt * _sigmoid(raw_ref[...] + dtb_ref[...])
        draw_ref[...] = draw
        dalog_ref[...] = jnp.sum(dav * dt_ref[...], axis=1, keepdims=True) * a
        ddtb_ref[...] = jnp.sum(draw, axis=1, keepdims=True)
        dD_ref[...] = jnp.sum(dd_ref[...], axis=1, keepdims=True)

    return pl.pallas_call(
        body,
        out_shape=[jax.ShapeDtypeStruct((H, T), F32)] + [jax.ShapeDtypeStruct((H, 1), F32)] * 3,
        compiler_params=pltpu.CompilerParams(vmem_limit_bytes=VMEM_LIMIT_BYTES), name=name,
    )(dtraw, dt, da, ddt_direct, ddrow, alog, dtb)


def _adamw(parts, w, m, v, name):
    R, C = w.shape
    tr = 256 if R % 256 == 0 else R

    def body(p_ref, w_ref, m_ref, v_ref, g_ref, d_ref, nm_ref, nv_ref):
        g = p_ref[0].astype(F32)
        for s in range(1, N_DEV):
            g = g + p_ref[s].astype(F32)
        mn = ADAM_B1 * m_ref[...] + (1.0 - ADAM_B1) * g
        vn = ADAM_B2 * v_ref[...] + (1.0 - ADAM_B2) * (g * g)
        mh = mn / (1.0 - ADAM_B1 ** ADAM_STEP)
        vh = vn / (1.0 - ADAM_B2 ** ADAM_STEP)
        g_ref[...] = g
        d_ref[...] = -ADAM_LR * (mh / (jnp.sqrt(vh) + ADAM_EPS) + ADAM_WD * w_ref[...])
        nm_ref[...] = mn
        nv_ref[...] = vn

    blk = pl.BlockSpec((tr, C), lambda i: (i, 0))
    return pl.pallas_call(
        body, grid=(R // tr,),
        in_specs=[pl.BlockSpec((N_DEV, tr, C), lambda i: (0, i, 0)), blk, blk, blk],
        out_specs=[blk] * 4,
        out_shape=[jax.ShapeDtypeStruct((R, C), F32)] * 4,
        compiler_params=_params("parallel"), name=name,
    )(parts, w, m, v)


def _me():
    x, y, c = lax.axis_index("x"), lax.axis_index("y"), lax.axis_index("c")
    return x, y, c


def _peer(d):
    x, y, c = _me()
    px = 1 - x if d & 4 else x
    py = 1 - y if d & 2 else y
    pc = 1 - c if d & 1 else c
    return (px, py, pc), 4 * px + 2 * py + pc


def _gather_two_level(arrs, name):
    n = len(arrs)

    def body(*refs):
        ins, outs = refs[:n], refs[n:2 * n]
        ssem, rsem, lsem = refs[2 * n:]
        x, y, c = _me()
        me, sibling = (x, y, c), (x, y, 1 - c)
        chips = [(1 - x, y), (x, 1 - y), (1 - x, 1 - y)]

        def slot(a, block):
            px, py, pc = block
            return outs[a].at[4 * px + 2 * py + pc]

        def copy(a, k, block, to, src=None):
            return pltpu.make_async_remote_copy(
                src_ref=slot(a, block) if src is None else src, dst_ref=slot(a, block),
                send_sem=ssem.at[a, k], recv_sem=rsem.at[a, k], device_id=to, device_id_type=MESH)

        mine = [pltpu.make_async_copy(ins[a], slot(a, me), lsem.at[a]) for a in range(n)]
        for cp in mine:
            cp.start()
        first = []
        for a in range(n):
            first.append(copy(a, 0, me, sibling, src=ins[a]))
            first += [copy(a, 1 + j, me, (*chip, c), src=ins[a]) for j, chip in enumerate(chips)]
        for cp in first:
            cp.start()
        passed = [[copy(a, 4 + j, (*chip, c), sibling) for j, chip in enumerate(chips)] for a in range(n)]
        for j, chip in enumerate(chips):
            for a in range(n):
                copy(a, 1 + j, (*chip, c), me).wait_recv()
                passed[a][j].start()
        for a in range(n):
            copy(a, 0, sibling, me).wait_recv()
            for j, chip in enumerate(chips):
                copy(a, 4 + j, (*chip, 1 - c), me).wait_recv()
        for cp in first + [cp for row in passed for cp in row]:
            cp.wait_send()
        for cp in mine:
            cp.wait()

    anyspec = pl.BlockSpec(memory_space=pl.ANY)
    return pl.pallas_call(
        body,
        in_specs=[anyspec] * n, out_specs=[anyspec] * n,
        out_shape=_exchange_out_shapes(arrs, [True] * n),
        scratch_shapes=_exchange_semaphores(n),
        name=name,
    )(*arrs)


def _exchange(arrs, bcast, name):
    n = len(arrs)

    def body(*refs):
        ex = _Exchange(refs[:n], refs[n:2 * n], bcast, *refs[2 * n:])
        ex.begin()
        ex.finish()

    anyspec = pl.BlockSpec(memory_space=pl.ANY)
    return pl.pallas_call(
        body,
        in_specs=[anyspec] * n, out_specs=[anyspec] * n,
        out_shape=_exchange_out_shapes(arrs, bcast),
        scratch_shapes=_exchange_semaphores(n),
        name=name,
    )(*arrs)


def _exchange_out_shapes(arrs, bcast):
    return [jax.ShapeDtypeStruct((N_DEV,) + (a.shape if b else a.shape[1:]), a.dtype) for a, b in zip(arrs, bcast)]


def _exchange_semaphores(n):
    return [pltpu.SemaphoreType.DMA((n, N_DEV - 1)), pltpu.SemaphoreType.DMA((n, N_DEV - 1)),
            pltpu.SemaphoreType.DMA((n,))]


class _Exchange:
    def __init__(self, ins, outs, bcast, ssem, rsem, lsem):
        n = len(ins)
        x, y, c = _me()
        me = 4 * x + 2 * y + c

        def src(a, dest):
            return ins[a] if bcast[a] else ins[a].at[dest]

        self.local = [pltpu.make_async_copy(src(a, me), outs[a].at[me], lsem.at[a]) for a in range(n)]
        self.sends, self.recvs = [], []
        for a in range(n):
            for d in range(1, N_DEV):
                peer, pid = _peer(d)
                self.sends.append(pltpu.make_async_remote_copy(
                    src_ref=src(a, pid), dst_ref=outs[a].at[me], send_sem=ssem.at[a, d - 1],
                    recv_sem=rsem.at[a, d - 1], device_id=peer, device_id_type=MESH))
                self.recvs.append(pltpu.make_async_remote_copy(
                    src_ref=src(a, pid), dst_ref=outs[a].at[pid], send_sem=ssem.at[a, d - 1],
                    recv_sem=rsem.at[a, d - 1], device_id=peer, device_id_type=MESH))

    def begin(self):
        for cp in self.local + self.sends:
            cp.start()

    def finish(self):
        for cp in self.recvs:
            cp.wait_recv()
        for cp in self.sends:
            cp.wait_send()
        for cp in self.local:
            cp.wait()


def _to_rows(cols, chunk):
    T, H = cols.shape
    return cols.T.reshape(H, T // chunk, chunk)


def _from_rows(rows):
    return rows.T


def _pad_cols(a, width):
    return jnp.pad(a, ((0, 0), (0, width - a.shape[1])))


def _local_step(x, tgt, p, late_weights=None, early_grads=None, late_grads=None):
    T = x.shape[0]
    zb = lambda n: jnp.zeros((1, n), F32)
    gw = p["gdn_w_in"]
    g_wparts = [gw[:, 0:1024], gw[:, 1024:2048], gw[:, 2048:4096], gw[:, 4096:6144], _pad_cols(gw[:, 6144:6176], PAD_W)]
    nw0, nw1 = p["norm_w"][0:1], p["norm_w"][1:2]
    gcw = p["gdn_conv_w"]
    cw_q, cw_k, cw_v = gcw[:, 0:1024], gcw[:, 1024:2048], gcw[:, 2048:4096]
    g_convs = [(cw_q, zb(1024), True, GDN_DK ** -0.5), (cw_k, zb(1024), True, 1.0), (cw_v, zb(2048), False, 1.0),
               None, None]
    if late_weights is None:
        h0, (q_pre, k_pre, v_pre, z0, ab), (q, k, v), g_cpre = _norm_inproj(x, nw0, g_wparts, g_convs, "gdn_inproj")
    else:
        comm, assemble = late_weights
        h0, (q_pre, k_pre, v_pre, z0, ab), (q, k, v), g_cpre, gathered = _norm_inproj(x, nw0, g_wparts, g_convs,
                                                                                      "gdn_inproj", comm)
        p = dict(p, **assemble(gathered))
    braw = _to_rows(ab[:, 0:GDN_HV], GDN_CHUNK)
    araw = _to_rows(ab[:, GDN_HV:2 * GDN_HV], GDN_CHUNK)
    g_alog, g_dtb = p["gdn_a_log"].reshape(-1), p["gdn_dt_bias"].reshape(-1)
    g_u, g_w, g_pm, g_ti, g_rows, beta_rows, gc_rows, g_qd, g_kd = _gdn_prep(q, k, v, araw, braw, g_alog, g_dtb,
                                                                             "gdn_prep")
    o0, g_vn, g_sall = _gdn_state_fwd(g_qd, g_kd, g_u, g_w, g_pm, gc_rows, "gdn_state_fwd")
    x1 = _out_fwd(o0, z0, p["gdn_norm_w"], p["gdn_w_out"], x, GDN_DK, False, "gdn_out")
    sw = p["ssd_w_in"]
    s_wparts = [sw[:, 0:2048], sw[:, 2048:4096], sw[:, 4096:5120], sw[:, 5120:6144], _pad_cols(sw[:, 6144:6176], PAD_W)]
    scw, scb = p["ssd_conv_w"], p["ssd_conv_b"]
    s_convs = [None, (scw[:, 0:2048], scb[:, 0:2048], False, 1.0), (scw[:, 2048:3072], scb[:, 2048:3072], False, 1.0),
               (scw[:, 3072:4096], scb[:, 3072:4096], False, 1.0), None]
    h1, (z1, xs_pre, b_pre, c_pre, dtp), (xs, bm, cm), s_cpre = _norm_inproj(x1, nw1, s_wparts, s_convs, "ssd_inproj")
    dtraw = _to_rows(dtp[:, 0:SSD_H], SSD_CHUNK)
    s_alog, s_dtb, s_d = p["ssd_a_log"].reshape(-1), p["ssd_dt_bias"].reshape(-1), p["ssd_d"].reshape(-1)
    y1, s_sall, dt_rows = _ssd_scan_fwd(xs, bm, cm, dtraw, s_alog, s_dtb, s_d, "ssd_scan_fwd")
    dx2, d_fw, loss = _out_fwd(y1, z1, p["ssd_norm_w"], p["ssd_w_out"], x1, D_INNER // SSD_G, True, "ssd_out_loss",
                               (p["final_norm_w"].reshape(1, -1), tgt))
    dy1, dz1, d_snw, yn1 = _out_bwd(dx2, y1, z1, p["ssd_norm_w"], p["ssd_w_out"], D_INNER // SSD_G, True, "ssd_out_bwd")
    d_swout = _matmul_tn(yn1, dx2, "ssd_wout_grad")
    dxs, dbm, dcm, da_rows, ddt_rows, dd_rows = _ssd_scan_bwd(xs, bm, cm, dt_rows, s_sall, dy1, s_alog, s_d, "ssd_scan_bwd")
    col = lambda a: a.reshape(-1, 1)
    dtraw_g, d_salog, d_sdtb, d_sd = _ssd_gate_bwd(
        dtraw.reshape(SSD_H, T), dt_rows.reshape(SSD_H, T), da_rows.reshape(SSD_H, T),
        ddt_rows.reshape(SSD_H, T), dd_rows.reshape(SSD_H, T), col(s_alog), col(s_dtb), "ssd_gate_bwd")
    dxs_pre, dcw_x, dcb_x = _conv_bwd(xs_pre, s_cpre[0], scw[:, 0:2048], dxs, False, 1.0, "ssd_conv_x_bwd")
    db_pre, dcw_b, dcb_b = _conv_bwd(b_pre, s_cpre[1], scw[:, 2048:3072], dbm, False, 1.0, "ssd_conv_b_bwd")
    dc_pre, dcw_c, dcb_c = _conv_bwd(c_pre, s_cpre[2], scw[:, 3072:4096], dcm, False, 1.0, "ssd_conv_c_bwd")
    ddtp = _pad_cols(_from_rows(dtraw_g), PAD_W)
    s_dparts = [dz1, dxs_pre, db_pre, dc_pre, ddtp]
    dx1, d_nw1 = _inproj_bwd(x1, nw1, s_dparts, s_wparts, dx2, "ssd_inproj_bwd")
    s_dw = [_matmul_tn(h1, d, "ssd_win_grad_%d" % n) for n, d in enumerate(s_dparts)]
    d_swin = jnp.concatenate(s_dw[:4] + [s_dw[4][:, 0:SSD_H]], axis=1)
    early_recv = None
    if early_grads is None:
        do0, dz0, d_gnw, yn0 = _out_bwd(dx1, o0, z0, p["gdn_norm_w"], p["gdn_w_out"], GDN_DK, False, "gdn_out_bwd")
    else:
        do0, dz0, d_gnw, yn0, early_recv = _out_bwd(dx1, o0, z0, p["gdn_norm_w"], p["gdn_w_out"], GDN_DK, False,
                                                    "gdn_out_bwd", early_grads(d_swin, d_swout))
    d_gwout = _matmul_tn(yn0, dx1, "gdn_wout_grad")
    g_dvn, g_dkd, g_dgl = _gdn_state_bwd(g_qd, g_kd, g_w, g_pm, g_vn, g_sall, gc_rows, do0, "gdn_state_bwd")
    dq, dk, dv, dg_rows, dbeta_rows = _gdn_local_bwd(q, k, v, gc_rows, beta_rows, g_ti, g_u, g_w, g_pm, g_vn, g_sall,
                                                     do0, g_dvn, g_dkd, g_dgl, "gdn_local_bwd")
    da_g, db_g, d_galog, d_gdtb = _gdn_gate_bwd(
        araw.reshape(GDN_HV, T), braw.reshape(GDN_HV, T), dg_rows.reshape(GDN_HV, T),
        dbeta_rows.reshape(GDN_HV, T), col(g_alog), col(g_dtb), "gdn_gate_bwd")
    dq_pre, dcw_q, _ = _conv_bwd(q_pre, g_cpre[0], cw_q, dq, True, GDN_DK ** -0.5, "gdn_conv_q_bwd")
    dk_pre, dcw_k, _ = _conv_bwd(k_pre, g_cpre[1], cw_k, dk, True, 1.0, "gdn_conv_k_bwd")
    dv_pre, dcw_v, _ = _conv_bwd(v_pre, g_cpre[2], cw_v, dv, False, 1.0, "gdn_conv_v_bwd")
    dab = _pad_cols(jnp.concatenate([_from_rows(db_g), _from_rows(da_g)], axis=1), PAD_W)
    g_dparts = [dq_pre, dk_pre, dv_pre, dz0, dab]
    g_dw = [_matmul_tn(h0, d, "gdn_win_grad_%d" % n) for n, d in enumerate(g_dparts)]
    d_gwin = jnp.concatenate(g_dw[:4] + [g_dw[4][:, 0:2 * GDN_HV]], axis=1)
    sharded_grads = {
        "gdn_w_in": d_gwin, "gdn_w_out": d_gwout,
        "gdn_conv_w": jnp.concatenate([dcw_q, dcw_k, dcw_v], axis=1),
        "ssd_conv_w": jnp.concatenate([dcw_x, dcw_b, dcw_c], axis=1),
        "ssd_conv_b": jnp.concatenate([dcb_x, dcb_b, dcb_c], axis=1), "ssd_norm_w": d_snw}
    late_recv = None
    if late_grads is None:
        dx0, d_nw0 = _inproj_bwd(x, nw0, g_dparts, g_wparts, dx1, "gdn_inproj_bwd")
    else:
        dx0, d_nw0, late_recv = _inproj_bwd(x, nw0, g_dparts, g_wparts, dx1, "gdn_inproj_bwd",
                                            late_grads(sharded_grads))
    grads = {
        "norm_w": jnp.concatenate([d_nw0, d_nw1], axis=0),
        "gdn_w_in": d_gwin,
        "gdn_conv_w": jnp.concatenate([dcw_q, dcw_k, dcw_v], axis=1),
        "gdn_a_log": d_galog.reshape(1, -1),
        "gdn_dt_bias": d_gdtb.reshape(1, -1),
        "gdn_norm_w": d_gnw,
        "gdn_w_out": d_gwout,
        "ssd_w_in": d_swin,
        "ssd_conv_w": jnp.concatenate([dcw_x, dcw_b, dcw_c], axis=1),
        "ssd_conv_b": jnp.concatenate([dcb_x, dcb_b, dcb_c], axis=1),
        "ssd_dt_bias": d_sdtb.reshape(1, -1),
        "ssd_a_log": d_salog.reshape(1, -1),
        "ssd_d": d_sd.reshape(1, -1),
        "ssd_norm_w": d_snw,
        "ssd_w_out": d_swout,
        "final_norm_w": d_fw,
    }
    if early_grads is not None:
        return loss, dx0, grads, early_recv, late_recv
    return loss, dx0, grads


WEIGHTS = ["norm_w", "gdn_w_in", "gdn_conv_w", "gdn_a_log", "gdn_dt_bias", "gdn_norm_w", "gdn_w_out", "ssd_w_in",
           "ssd_conv_w", "ssd_conv_b", "ssd_dt_bias", "ssd_a_log", "ssd_d", "ssd_norm_w", "ssd_w_out", "final_norm_w"]
COL_SHARDED = ["gdn_w_in", "ssd_w_in"]
ROW_SHARDED = ["gdn_w_out", "ssd_w_out"]
SMALL_SHARDED = ["gdn_conv_w", "ssd_conv_w", "ssd_conv_b", "ssd_norm_w"]
REPLICATED = ["norm_w", "gdn_a_log", "gdn_dt_bias", "gdn_norm_w", "ssd_dt_bias", "ssd_a_log", "ssd_d", "final_norm_w"]


def _pack(arrs):
    return jnp.concatenate([a.reshape(-1) for a in arrs]).reshape(1, -1)


def _unpack(flat, shapes):
    out, pos = [], 0
    for s in shapes:
        n = 1
        for dim in s:
            n *= dim
        out.append(flat[pos:pos + n].reshape(s))
        pos += n
    return out


def _cols_to_shards(full):
    R, C = full.shape
    return full.reshape(R, N_DEV, C // N_DEV).transpose(1, 0, 2)


def _shards_to_cols(shards):
    n, R, c = shards.shape
    return shards.transpose(1, 0, 2).reshape(R, n * c)


def kernel(x, norm_w, gdn_w_in, gdn_conv_w, gdn_a_log, gdn_dt_bias, gdn_norm_w, gdn_w_out, ssd_w_in, ssd_conv_w, ssd_conv_b, ssd_dt_bias, ssd_a_log, ssd_d, ssd_norm_w, ssd_w_out, final_norm_w, loss_target, m_norm_w, m_gdn_w_in, m_gdn_conv_w, m_gdn_a_log, m_gdn_dt_bias, m_gdn_norm_w, m_gdn_w_out, m_ssd_w_in, m_ssd_conv_w, m_ssd_conv_b, m_ssd_dt_bias, m_ssd_a_log, m_ssd_d, m_ssd_norm_w, m_ssd_w_out, m_final_norm_w, v_norm_w, v_gdn_w_in, v_gdn_conv_w, v_gdn_a_log, v_gdn_dt_bias, v_gdn_norm_w, v_gdn_w_out, v_ssd_w_in, v_ssd_conv_w, v_ssd_conv_b, v_ssd_dt_bias, v_ssd_a_log, v_ssd_d, v_ssd_norm_w, v_ssd_w_out, v_final_norm_w):
    w = dict(norm_w=norm_w, gdn_w_in=gdn_w_in[0], gdn_conv_w=gdn_conv_w[0], gdn_a_log=gdn_a_log,
             gdn_dt_bias=gdn_dt_bias, gdn_norm_w=gdn_norm_w, gdn_w_out=gdn_w_out[0], ssd_w_in=ssd_w_in[0],
             ssd_conv_w=ssd_conv_w[0], ssd_conv_b=ssd_conv_b, ssd_dt_bias=ssd_dt_bias, ssd_a_log=ssd_a_log,
             ssd_d=ssd_d, ssd_norm_w=ssd_norm_w, ssd_w_out=ssd_w_out[0], final_norm_w=final_norm_w.reshape(1, -1))
    m = dict(norm_w=m_norm_w, gdn_w_in=m_gdn_w_in[0], gdn_conv_w=m_gdn_conv_w[0], gdn_a_log=m_gdn_a_log,
             gdn_dt_bias=m_gdn_dt_bias, gdn_norm_w=m_gdn_norm_w, gdn_w_out=m_gdn_w_out[0], ssd_w_in=m_ssd_w_in[0],
             ssd_conv_w=m_ssd_conv_w[0], ssd_conv_b=m_ssd_conv_b, ssd_dt_bias=m_ssd_dt_bias, ssd_a_log=m_ssd_a_log,
             ssd_d=m_ssd_d, ssd_norm_w=m_ssd_norm_w, ssd_w_out=m_ssd_w_out[0], final_norm_w=m_final_norm_w.reshape(1, -1))
    v = dict(norm_w=v_norm_w, gdn_w_in=v_gdn_w_in[0], gdn_conv_w=v_gdn_conv_w[0], gdn_a_log=v_gdn_a_log,
             gdn_dt_bias=v_gdn_dt_bias, gdn_norm_w=v_gdn_norm_w, gdn_w_out=v_gdn_w_out[0], ssd_w_in=v_ssd_w_in[0],
             ssd_conv_w=v_ssd_conv_w[0], ssd_conv_b=v_ssd_conv_b, ssd_dt_bias=v_ssd_dt_bias, ssd_a_log=v_ssd_a_log,
             ssd_d=v_ssd_d, ssd_norm_w=v_ssd_norm_w, ssd_w_out=v_ssd_w_out[0], final_norm_w=v_final_norm_w.reshape(1, -1))
    out_shapes = {n: a.shape for n, a in zip(
        WEIGHTS, [norm_w, gdn_w_in, gdn_conv_w, gdn_a_log, gdn_dt_bias, gdn_norm_w, gdn_w_out, ssd_w_in, ssd_conv_w,
                  ssd_conv_b, ssd_dt_bias, ssd_a_log, ssd_d, ssd_norm_w, ssd_w_out, final_norm_w])}

    small_shapes = [w[n].shape for n in SMALL_SHARDED]
    first = _gather_two_level([_mx(w["gdn_w_in"]), _pack([w[n] for n in SMALL_SHARDED])], "gather_first")
    full = dict(w)
    full["gdn_w_in"] = _shards_to_cols(first[0])
    small_all = [_unpack(first[1][s, 0], small_shapes) for s in range(N_DEV)]
    for idx, n in enumerate(SMALL_SHARDED):
        full[n] = jnp.concatenate([small_all[s][idx] for s in range(N_DEV)], axis=-1)
    late = ["gdn_w_out", "ssd_w_in", "ssd_w_out"]

    def assemble(gathered):
        return {"gdn_w_out": gathered[0].reshape(-1, D_MODEL), "ssd_w_in": _shards_to_cols(gathered[1]),
                "ssd_w_out": gathered[2].reshape(-1, D_MODEL)}

    def early_grads(d_ssd_w_in, d_ssd_w_out):
        return ([_cols_to_shards(d_ssd_w_in).astype(GRAD_WIRE_DTYPE),
                 d_ssd_w_out.reshape(N_DEV, -1, D_MODEL).astype(GRAD_WIRE_DTYPE)], [False] * 2)

    def late_grads(g):
        send_small = jnp.concatenate(
            [_cols_to_shards(g[n]).reshape(N_DEV, -1) for n in SMALL_SHARDED], axis=1)[:, None, :]
        return ([_cols_to_shards(g["gdn_w_in"]).astype(GRAD_WIRE_DTYPE),
                 g["gdn_w_out"].reshape(N_DEV, -1, D_MODEL).astype(GRAD_WIRE_DTYPE), send_small], [False] * 3)

    loss, dx, grads, ssd_recv, gdn_recv = _local_step(
        x[0], loss_target[0], full, (([_mx(w[n]) for n in late], [True] * 3), assemble), early_grads, late_grads)

    rep_shapes = [w[n].shape for n in REPLICATED]
    recv_rep = _exchange([_pack([grads[n] for n in REPLICATED])], [True], "exchange_grads")[0]

    res = {}
    for n, parts in zip(["gdn_w_in", "gdn_w_out", "ssd_w_in", "ssd_w_out"], list(gdn_recv[:2]) + list(ssd_recv)):
        res[n] = _adamw(parts, w[n], m[n], v[n], "adamw_" + n)
    small_res = _adamw(gdn_recv[2], *[_pack([t[n] for n in SMALL_SHARDED]) for t in (w, m, v)], "adamw_small")
    rep_res = _adamw(recv_rep, *[_pack([t[n] for n in REPLICATED]) for t in (w, m, v)], "adamw_replicated")
    for k4 in range(4):
        for n, a in zip(SMALL_SHARDED, _unpack(small_res[k4][0], small_shapes)):
            res.setdefault(n, [None] * 4)[k4] = a
        for n, a in zip(REPLICATED, _unpack(rep_res[k4][0], rep_shapes)):
            res.setdefault(n, [None] * 4)[k4] = a

    loss = lax.psum(loss[0, 0], ("x", "y", "c"))
    outs = [loss, dx[None]]
    for k4 in range(4):
        outs += [res[n][k4].reshape(out_shapes[n]) for n in WEIGHTS]
    return tuple(outs)
```

```python
import jax
import jax.numpy as jnp
from jax import lax
from jax.experimental import pallas as pl
from jax.experimental.pallas import tpu as pltpu

F32 = jnp.float32
MXU_DTYPE = jnp.bfloat16
GRAD_WIRE_DTYPE = jnp.bfloat16
HI = lax.Precision.HIGHEST
EPS = 1e-6
VMEM_LIMIT_BYTES = 56 * 1024 * 1024
N_DEV = 8
MESH = pl.DeviceIdType.MESH

D_MODEL = 1024
CONV_K = 4
GDN_HV = 16
GDN_DK = 128
GDN_CHUNK = 64
SSD_H = 32
SSD_P = 64
SSD_N = 128
SSD_G = 8
SSD_R = SSD_H // SSD_G
SSD_CHUNK = 128
D_INNER = 2048
PAD_W = 128

ADAM_LR = 0.001
ADAM_B1 = 0.9
ADAM_B2 = 0.999
ADAM_EPS = 1e-08
ADAM_WD = 0.01
ADAM_STEP = 10


def _params(*sem):
    return pltpu.CompilerParams(dimension_semantics=sem, vmem_limit_bytes=VMEM_LIMIT_BYTES)


def _mx(a):
    return a.astype(MXU_DTYPE)


def _dot(a, b):
    return jnp.dot(_mx(a), _mx(b), preferred_element_type=F32)


def _dot_nt(a, b):
    return lax.dot_general(_mx(a), _mx(b), (((1,), (1,)), ((), ())), preferred_element_type=F32)


def _dot_tn(a, b):
    return lax.dot_general(_mx(a), _mx(b), (((0,), (0,)), ((), ())), preferred_element_type=F32)


def _dot_hi(a, b):
    return jnp.dot(a, b, precision=HI, preferred_element_type=F32)


def _sigmoid(x):
    return 0.5 * jnp.tanh(0.5 * x) + 0.5


def _silu(x):
    return x * _sigmoid(x)


def _dsilu(x):
    s = _sigmoid(x)
    return s * (1.0 + x * (1.0 - s))


def _softplus(x):
    return jnp.maximum(x, 0.0) + jnp.log1p(jnp.exp(-jnp.abs(x)))


def _col(r, eye):
    return jnp.sum(jnp.where(eye, r, 0.0), axis=1, keepdims=True)


def _row(c, eye):
    return jnp.sum(jnp.where(eye, c, 0.0), axis=0, keepdims=True)


def _col_bcast(r, n):
    return jnp.broadcast_to(r, (n, n)).T


def _masks(n):
    r = lax.broadcasted_iota(jnp.int32, (n, n), 0)
    c = lax.broadcasted_iota(jnp.int32, (n, n), 1)
    return r >= c, r > c, r == c, r, c


def _with_exchange(comm):
    arrs, bcast = comm if comm else ([], [])
    nc = len(arrs)
    anyspec = pl.BlockSpec(memory_space=pl.ANY)

    def wrap(compute, n_in, n_out):
        def body(*refs):
            cin, cout = refs[n_in:n_in + nc], refs[n_in + nc + n_out:n_in + 2 * nc + n_out]
            sems = refs[n_in + 2 * nc + n_out:n_in + 2 * nc + n_out + 3]
            rest = refs[:n_in] + refs[n_in + nc:n_in + nc + n_out] + refs[n_in + 2 * nc + n_out + (3 if nc else 0):]
            if nc:
                @pl.when(pl.program_id(0) == 0)
                def _():
                    _Exchange(cin, cout, bcast, *sems).begin()
            compute(*rest)
            if nc:
                @pl.when(pl.program_id(0) == pl.num_programs(0) - 1)
                def _():
                    _Exchange(cin, cout, bcast, *sems).finish()
        return body

    return dict(arrs=list(arrs), nc=nc, wrap=wrap, in_specs=[anyspec] * nc, out_specs=[anyspec] * nc,
                out_shape=_exchange_out_shapes(arrs, bcast), scratch=_exchange_semaphores(nc) if nc else [])


INPROJ_CONV_STRIP = 256
INPROJ_COL_BLOCK = 512


def _norm_inproj(x, nw, wparts, convs, name, comm=None):
    T = x.shape[0]
    tt = min(T, 256)
    n = len(wparts)
    ck = [k for k in range(n) if convs[k] is not None]
    nconv = len(ck)
    widths = [w.shape[1] for w in wparts]
    conv_blocks = [(k, c0) for k in ck for c0 in range(0, widths[k], INPROJ_COL_BLOCK)]
    ex = _with_exchange(comm)

    def compute(x_ref, nw_ref, *refs):
        w_refs, cw_refs = refs[:n], refs[n:n + 2 * nconv]
        h_ref, o_refs = refs[n + 2 * nconv], refs[n + 2 * nconv + 1:2 * n + 2 * nconv + 1]
        post_refs = refs[2 * n + 2 * nconv + 1:2 * n + 3 * nconv + 1]
        cpre_refs = refs[2 * n + 3 * nconv + 1:2 * n + 4 * nconv + 1]
        p_refs = refs[2 * n + 4 * nconv + 1:]
        xv = x_ref[...]
        r = lax.rsqrt(jnp.mean(xv * xv, axis=-1, keepdims=True) + EPS)
        h = _mx(xv * r * nw_ref[...])
        h_ref[...] = h
        p_of = dict(zip(conv_blocks, p_refs))
        for P in p_refs:
            @pl.when(pl.program_id(0) == 0)
            def _():
                P[0:HALO, :] = jnp.zeros((HALO, P.shape[1]), F32)

        def conv_block(k, c0, cw):
            m = ck.index(k)
            _, _, l2, scale = convs[k]
            cw_ref, cb_ref, out_ref, P = cw_refs[2 * m], cw_refs[2 * m + 1], post_refs[m], p_of[(k, c0)]
            cs = slice(c0, c0 + cw)
            for r0 in range(0, tt, INPROJ_CONV_STRIP):
                rs = slice(r0, r0 + INPROJ_CONV_STRIP)
                acc = cb_ref[:, cs] + cw_ref[0:1, cs] * P[pl.ds(HALO - 3 + r0, INPROJ_CONV_STRIP), :]
                for j in range(1, CONV_K):
                    acc = acc + cw_ref[j:j + 1, cs] * P[pl.ds(HALO - 3 + j + r0, INPROJ_CONV_STRIP), :]
                cpre_refs[m][rs, cs] = acc
                s = _silu(acc)
                if l2:
                    sls = [slice(g0, g0 + GDN_DK) for g0 in range(0, cw, GDN_DK)]
                    rr = [lax.rsqrt(jnp.sum(s[:, sl] * s[:, sl], axis=-1, keepdims=True) + EPS) for sl in sls]
                    for sl, rg in zip(sls, rr):
                        out_ref[rs, c0 + sl.start:c0 + sl.stop] = s[:, sl] * rg * scale
                else:
                    out_ref[rs, cs] = s
            P[0:HALO, :] = P[tt:tt + HALO, :]

        pending = None
        for k in range(n):
            for c0 in range(0, widths[k], INPROJ_COL_BLOCK):
                cw = min(INPROJ_COL_BLOCK, widths[k] - c0)
                pre = jnp.dot(h, w_refs[k][:, c0:c0 + cw], preferred_element_type=F32)
                o_refs[k][:, c0:c0 + cw] = pre
                if convs[k] is not None:
                    p_of[(k, c0)][HALO:HALO + tt, :] = pre
                if pending is not None:
                    conv_block(*pending)
                pending = (k, c0, cw) if convs[k] is not None else None
        if pending is not None:
            conv_block(*pending)

    row = lambda width: pl.BlockSpec((tt, width), lambda i: (i, 0))
    full = lambda a: pl.BlockSpec(a.shape, lambda i: (0, 0))
    once = lambda a: pl.BlockSpec(a.shape, lambda i: (0, 0), pipeline_mode=pl.Buffered(1))
    conv_args = [a for k in ck for a in convs[k][:2]]
    outs = pl.pallas_call(
        ex["wrap"](compute, 2 + n + 2 * nconv, 1 + n + 2 * nconv), grid=(T // tt,),
        in_specs=[row(D_MODEL), full(nw)] + [once(w) for w in wparts] + [full(a) for a in conv_args] + ex["in_specs"],
        out_specs=[row(D_MODEL)] + [row(wd) for wd in widths] + [row(widths[k]) for k in ck + ck] + ex["out_specs"],
        out_shape=[jax.ShapeDtypeStruct((T, D_MODEL), MXU_DTYPE)]
        + [jax.ShapeDtypeStruct((T, wd), F32) for wd in widths]
        + [jax.ShapeDtypeStruct((T, widths[k]), F32) for k in ck + ck] + ex["out_shape"],
        scratch_shapes=ex["scratch"] + [pltpu.VMEM((HALO + tt, min(INPROJ_COL_BLOCK, widths[k] - c0)), F32)
                                        for k, c0 in conv_blocks],
        compiler_params=_params("arbitrary"), name=name,
    )(x, nw, *wparts, *conv_args, *ex["arrs"])
    outs = list(outs)
    res = (outs[0], outs[1:1 + n], outs[1 + n:1 + n + nconv], outs[1 + n + nconv:1 + n + 2 * nconv])
    return res + (outs[1 + n + 2 * nconv:],) if comm else res


def _inproj_bwd(x, nw, dparts, wparts, dres, name, comm=None):
    T = x.shape[0]
    tt = min(T, 512)
    n = len(wparts)
    ex = _with_exchange(comm)

    def body(x_ref, nw_ref, dres_ref, *refs):
        d_refs, w_refs, dx_ref, dnw_ref = refs[:n], refs[n:2 * n], refs[2 * n], refs[2 * n + 1]

        @pl.when(pl.program_id(0) == 0)
        def _():
            dnw_ref[...] = jnp.zeros_like(dnw_ref)

        dh = _dot_nt(d_refs[0][...], w_refs[0][...])
        for d_ref, w_ref in zip(d_refs[1:], w_refs[1:]):
            dh = dh + _dot_nt(d_ref[...], w_ref[...])
        xv = x_ref[...]
        r = lax.rsqrt(jnp.mean(xv * xv, axis=-1, keepdims=True) + EPS)
        xh = xv * r
        dnw_ref[...] += jnp.sum(dh * xh, axis=0, keepdims=True)
        dxn = dh * nw_ref[...]
        dx_ref[...] = dres_ref[...] + r * (dxn - xh * jnp.mean(dxn * xh, axis=-1, keepdims=True))

    row = lambda width: pl.BlockSpec((tt, width), lambda i: (i, 0))
    full = lambda a: pl.BlockSpec(a.shape, lambda i: (0, 0))
    outs = pl.pallas_call(
        ex["wrap"](body, 3 + 2 * n, 2), grid=(T // tt,),
        in_specs=[row(D_MODEL), full(nw), row(D_MODEL)] + [row(d.shape[1]) for d in dparts]
        + [pl.BlockSpec(w.shape, lambda i: (0, 0), pipeline_mode=pl.Buffered(1)) for w in wparts] + ex["in_specs"],
        out_specs=[row(D_MODEL), pl.BlockSpec((1, D_MODEL), lambda i: (0, 0))] + ex["out_specs"],
        out_shape=[jax.ShapeDtypeStruct((T, D_MODEL), F32), jax.ShapeDtypeStruct((1, D_MODEL), F32)]
        + ex["out_shape"],
        scratch_shapes=ex["scratch"],
        compiler_params=_params("arbitrary"), name=name,
    )(x, nw, dres, *dparts, *wparts, *ex["arrs"])
    outs = list(outs)
    return outs[:2] + ([outs[2:]] if comm else [])


def _matmul_tn(a, b, name):
    T, K = a.shape
    N = b.shape[1]
    tt = min(T, 2048)
    tn = min(N, 1024)

    def body(a_ref, b_ref, o_ref):
        @pl.when(pl.program_id(1) == 0)
        def _():
            o_ref[...] = jnp.zeros_like(o_ref)

        o_ref[...] += _dot_tn(a_ref[...], b_ref[...])

    return pl.pallas_call(
        body, grid=(N // tn, T // tt),
        in_specs=[pl.BlockSpec((tt, K), lambda n, t: (t, 0)), pl.BlockSpec((tt, tn), lambda n, t: (t, n))],
        out_specs=pl.BlockSpec((K, tn), lambda n, t: (0, n)),
        out_shape=jax.ShapeDtypeStruct((K, N), F32),
        compiler_params=_params("parallel", "arbitrary"), name=name,
    )(a, b)


OUT_COL_BLOCK = 512


def _out_fwd(o, z, w, wout, xres, gs, gate_first, name, final=None):
    T = o.shape[0]
    tt = min(T, 512)
    nT = T // tt
    wide = w.shape[1] == D_INNER

    def body(o_ref, z_ref, w_ref, wout_ref, x_ref, *refs):
        if final is None:
            out_ref, yn = refs
        else:
            fw_ref, t_ref, dx_ref, dfw_ref, loss_ref, yn, lacc = refs
        acc = x_ref[...]
        pending = None
        for b0 in range(0, D_INNER, OUT_COL_BLOCK):
            for g0 in range(b0, b0 + OUT_COL_BLOCK, gs):
                sl = slice(g0, g0 + gs)
                og, zg = o_ref[:, sl], z_ref[:, sl]
                wg = w_ref[:, sl] if wide else w_ref[...]
                if gate_first:
                    u = og * _silu(zg)
                    r = lax.rsqrt(jnp.mean(u * u, axis=-1, keepdims=True) + EPS)
                    yn[:, sl] = _mx(u * r * wg)
                else:
                    r = lax.rsqrt(jnp.mean(og * og, axis=-1, keepdims=True) + EPS)
                    yn[:, sl] = _mx(og * r * wg * _silu(zg))
            if pending is not None:
                acc = acc + jnp.dot(yn[:, pending], wout_ref[pending, :], preferred_element_type=F32)
            pending = slice(b0, b0 + OUT_COL_BLOCK)
        xv = acc + jnp.dot(yn[:, pending], wout_ref[pending, :], preferred_element_type=F32)
        if final is None:
            out_ref[...] = xv
            return
        i = pl.program_id(0)

        @pl.when(i == 0)
        def _():
            dfw_ref[...] = jnp.zeros_like(dfw_ref)
            lacc[...] = jnp.zeros_like(lacc)

        r = lax.rsqrt(jnp.mean(xv * xv, axis=-1, keepdims=True) + EPS)
        xh = xv * r
        err = xh * fw_ref[...] - t_ref[...]
        lacc[...] += jnp.sum(err * err, axis=0, keepdims=True)
        dout = err * (1.0 / D_MODEL)
        dfw_ref[...] += jnp.sum(dout * xh, axis=0, keepdims=True)
        dxn = dout * fw_ref[...]
        dx_ref[...] = r * (dxn - xh * jnp.mean(dxn * xh, axis=-1, keepdims=True))

        @pl.when(i == nT - 1)
        def _():
            loss_ref[...] = (0.5 / D_MODEL) * jnp.sum(lacc[...], axis=1, keepdims=True)

    row = lambda width: pl.BlockSpec((tt, width), lambda i: (i, 0))
    full = lambda a: pl.BlockSpec(a.shape, lambda i: (0, 0))
    if final is None:
        return pl.pallas_call(
            body, grid=(nT,),
            in_specs=[row(D_INNER), row(D_INNER), full(w), full(wout), row(D_MODEL)],
            out_specs=row(D_MODEL),
            out_shape=jax.ShapeDtypeStruct((T, D_MODEL), F32),
            scratch_shapes=[pltpu.VMEM((tt, D_INNER), MXU_DTYPE)],
            compiler_params=_params("parallel"), name=name,
        )(o, z, w, wout, xres)
    vec = pl.BlockSpec((1, D_MODEL), lambda i: (0, 0))
    return pl.pallas_call(
        body, grid=(nT,),
        in_specs=[row(D_INNER), row(D_INNER), full(w), full(wout), row(D_MODEL), vec, row(D_MODEL)],
        out_specs=[row(D_MODEL), vec, pl.BlockSpec((1, 1), lambda i: (0, 0))],
        out_shape=[jax.ShapeDtypeStruct((T, D_MODEL), F32), jax.ShapeDtypeStruct((1, D_MODEL), F32),
                   jax.ShapeDtypeStruct((1, 1), F32)],
        scratch_shapes=[pltpu.VMEM((tt, D_INNER), MXU_DTYPE), pltpu.VMEM((1, D_MODEL), F32)],
        compiler_params=_params("arbitrary"), name=name,
    )(o, z, w, wout, xres, *final)


def _out_bwd(dx, o, z, w, wout, gs, gate_first, name, comm=None):
    T = o.shape[0]
    tt = min(T, 256)
    wide = w.shape[1] == D_INNER

    def body(dx_ref, o_ref, z_ref, w_ref, wout_ref, do_ref, dz_ref, dw_ref, yn_ref):
        @pl.when(pl.program_id(0) == 0)
        def _():
            dw_ref[...] = jnp.zeros_like(dw_ref)

        dxb = _mx(dx_ref[...])
        blocks = list(range(0, D_INNER, OUT_COL_BLOCK))
        dyn_b = {b0: _dot_nt(dxb, wout_ref[b0:b0 + OUT_COL_BLOCK, :]) for b0 in blocks[:1]}
        dw_acc = jnp.zeros((1, gs), F32)
        for g0 in range(0, D_INNER, gs):
            b0 = g0 - g0 % OUT_COL_BLOCK
            if g0 == b0 and b0 + OUT_COL_BLOCK < D_INNER:
                nb = b0 + OUT_COL_BLOCK
                dyn_b[nb] = _dot_nt(dxb, wout_ref[nb:nb + OUT_COL_BLOCK, :])
            sl = slice(g0, g0 + gs)
            og, zg, dg = o_ref[:, sl], z_ref[:, sl], dyn_b[b0][:, g0 - b0:g0 - b0 + gs]
            wg = w_ref[:, sl] if wide else w_ref[...]
            sz = _silu(zg)
            if gate_first:
                u = og * sz
                r = lax.rsqrt(jnp.mean(u * u, axis=-1, keepdims=True) + EPS)
                uh = u * r
                yn_ref[:, sl] = _mx(uh * wg)
                dw_g = jnp.sum(dg * uh, axis=0, keepdims=True)
                duh = dg * wg
                du = r * (duh - uh * jnp.mean(duh * uh, axis=-1, keepdims=True))
                do_ref[:, sl] = du * sz
                dz_ref[:, sl] = _mx(du * og * _dsilu(zg))
            else:
                r = lax.rsqrt(jnp.mean(og * og, axis=-1, keepdims=True) + EPS)
                oh = og * r
                yn_ref[:, sl] = _mx(oh * wg * sz)
                dw_g = jnp.sum(dg * oh * sz, axis=0, keepdims=True)
                doh = dg * wg * sz
                dz_ref[:, sl] = _mx(dg * oh * wg * _dsilu(zg))
                do_ref[:, sl] = r * (doh - oh * jnp.mean(doh * oh, axis=-1, keepdims=True))
            if wide:
                dw_ref[:, sl] += dw_g
            else:
                dw_acc = dw_acc + dw_g
        if not wide:
            dw_ref[...] += dw_acc

    row = lambda width: pl.BlockSpec((tt, width), lambda i: (i, 0))
    full = lambda a: pl.BlockSpec(a.shape, lambda i: (0, 0))
    ex = _with_exchange(comm)
    outs = pl.pallas_call(
        ex["wrap"](body, 5, 4), grid=(T // tt,),
        in_specs=[row(D_MODEL), row(D_INNER), row(D_INNER), full(w), full(wout)] + ex["in_specs"],
        out_specs=[row(D_INNER), row(D_INNER), full(w), row(D_INNER)] + ex["out_specs"],
        out_shape=[jax.ShapeDtypeStruct((T, D_INNER), F32), jax.ShapeDtypeStruct((T, D_INNER), MXU_DTYPE),
                   jax.ShapeDtypeStruct(w.shape, F32), jax.ShapeDtypeStruct((T, D_INNER), MXU_DTYPE)]
        + ex["out_shape"],
        scratch_shapes=ex["scratch"],
        compiler_params=_params("arbitrary"), name=name,
    )(dx, o, z, w, wout, *ex["arrs"])
    outs = list(outs)
    return outs[:4] + ([outs[4:]] if comm else [])


HALO = 8
CONV_STRIP = 32


def _conv_bwd(pre, cpre_all, w, dpost, l2, scale, name):
    T, C = pre.shape
    tt = min(T, 1024)
    tc = min(C, 1024 if l2 else 512)
    strip = 2 * CONV_STRIP if l2 else CONV_STRIP
    nT = T // tt
    ext = tt + HALO

    def body(pre_ref, cp_ref, cn_ref, dpost_ref, dn_ref, w_ref, dpre_ref, dw_ref, db_ref, Q):
        i = pl.program_id(1)

        @pl.when(i == 0)
        def _():
            dw_ref[...] = jnp.zeros_like(dw_ref)
            db_ref[...] = jnp.zeros_like(db_ref)

        wj = [w_ref[j:j + 1, :] for j in range(CONV_K)]
        keep_next = jnp.where(i < nT - 1, 1.0, 0.0)
        fold = lambda a: jnp.sum(a.reshape(strip // 8, 8, tc), axis=0)
        dw_acc = [jnp.zeros((8, tc), F32) for _ in range(CONV_K)]
        db_acc = jnp.zeros((8, tc), F32)
        for r0 in list(range(0, tt, strip)) + [tt]:
            n = strip if r0 < tt else HALO
            cpre = cp_ref[r0:r0 + n, :] if r0 < tt else cn_ref[...]
            dy = dpost_ref[r0:r0 + n, :] if r0 < tt else dn_ref[...] * keep_next
            sg = _sigmoid(cpre)
            ds_c = sg * (1.0 + cpre * (1.0 - sg))
            if l2:
                s = cpre * sg
                sls = [slice(g0, g0 + GDN_DK) for g0 in range(0, tc, GDN_DK)]
                rr = [lax.rsqrt(jnp.sum(s[:, sl] * s[:, sl], axis=-1, keepdims=True) + EPS) for sl in sls]
                yh = [s[:, sl] * r for sl, r in zip(sls, rr)]
                pr = [jnp.sum(dy[:, sl] * y, axis=-1, keepdims=True) for sl, y in zip(sls, yh)]
                for sl, r, y, p in zip(sls, rr, yh, pr):
                    Q[r0:r0 + n, sl] = (scale * r) * (dy[:, sl] - y * p) * ds_c[:, sl]
                dyc = Q[r0:r0 + n, :]
            else:
                dyc = dy * ds_c
                Q[r0:r0 + n, :] = dyc
            if r0 < tt:
                db_acc = db_acc + fold(dyc)
        for r0 in range(0, tt, strip):
            xs = pre_ref[r0:r0 + strip, :]
            dpre = jnp.zeros((strip, tc), F32)
            for j in range(CONV_K):
                qj = Q[pl.ds(3 - j + r0, strip), :]
                dpre = dpre + wj[j] * qj
                dw_acc[j] = dw_acc[j] + fold(qj * xs)
            dpre_ref[r0:r0 + strip, :] = _mx(dpre)
        for j in range(CONV_K):
            dw_ref[j:j + 1, :] += jnp.sum(dw_acc[j], axis=0, keepdims=True)
        db_ref[...] += jnp.sum(db_acc, axis=0, keepdims=True)

    tile = pl.BlockSpec((tt, tc), lambda j, i: (i, j))
    nxt = pl.BlockSpec((HALO, tc), lambda j, i: (jnp.minimum((i + 1) * (tt // HALO), T // HALO - 1), j))
    return pl.pallas_call(
        body, grid=(C // tc, nT),
        in_specs=[tile, tile, nxt, tile, nxt, pl.BlockSpec((CONV_K, tc), lambda j, i: (0, j))],
        out_specs=[tile, pl.BlockSpec((CONV_K, tc), lambda j, i: (0, j)), pl.BlockSpec((1, tc), lambda j, i: (0, j))],
        out_shape=[jax.ShapeDtypeStruct((T, C), MXU_DTYPE), jax.ShapeDtypeStruct((CONV_K, C), F32),
                   jax.ShapeDtypeStruct((1, C), F32)],
        scratch_shapes=[pltpu.VMEM((ext, tc), F32)],
        compiler_params=_params("parallel", "arbitrary"), name=name,
    )(pre, cpre_all, cpre_all, dpost, dpost, w)


GDN_LOCKSTEP_CHUNKS = 16
GDN_LOCKSTEP_CHUNKS_BWD = 16
GDN_SCAN_HEADS = 16


def _inv_unit_lower_many(nms, eye, n):
    xs = [jnp.where(eye, 1.0, 0.0) - nm for nm in nms]
    ps = list(nms)
    k = 2
    while k < n:
        ps = [_dot(p, p) for p in ps]
        xs = [x + _dot(x, p) for x, p in zip(xs, ps)]
        k *= 2
    return xs


def _gdn_prep(q, k, v, araw, braw, alog, dtb, name):
    T = q.shape[0]
    C = GDN_CHUNK
    tt = min(T, 1024)
    cpt, nC = tt // C, T // C
    grp = min(cpt, GDN_LOCKSTEP_CHUNKS)

    def body(alog_ref, dtb_ref, q_ref, k_ref, v_ref, a_ref, b_ref,
             u_ref, w_ref, pm_ref, ti_ref, g_ref, beta_ref, gc_ref, qd_ref, kd_ref):
        j = pl.program_id(0)
        tri, strict, eye, r_i, c_i = _masks(C)
        upper = jnp.where(r_i <= c_i, 1.0, 0.0)
        gcs, bts = [], []
        for hh in range(2):
            h = 2 * j + hh
            g = -jnp.exp(alog_ref[h]) * _softplus(a_ref[hh] + dtb_ref[h])
            bt = _sigmoid(b_ref[hh])
            gc = _dot_hi(g, upper)
            g_ref[hh], beta_ref[hh], gc_ref[hh] = g, bt, gc
            gcs.append(gc)
            bts.append(bt)
        for c0 in range(0, cpt, grp):
            cs = list(range(c0, c0 + grp))
            inst = [(c, hh) for c in cs for hh in range(2)]
            rows = {c: slice(c * C, (c + 1) * C) for c in cs}
            qc = {c: q_ref[rows[c], :] for c in cs}
            kc = {c: k_ref[rows[c], :] for c in cs}
            kk = {c: _dot_nt(kc[c], kc[c]) for c in cs}
            qk = {c: _dot_nt(qc[c], kc[c]) for c in cs}
            gcr = [gcs[hh][c:c + 1, :] for c, hh in inst]
            gcc = [_col(r, eye) for r in gcr]
            bc = [_col(bts[hh][c:c + 1, :], eye) for c, hh in inst]
            lm = [jnp.exp(jnp.where(tri, cc - r, -1e30)) for cc, r in zip(gcc, gcr)]
            nm = [jnp.where(strict, kk[c] * b * l, 0.0) for (c, hh), b, l in zip(inst, bc, lm)]
            tinv = _inv_unit_lower_many(nm, eye, C)
            e_c = [jnp.exp(cc) for cc in gcc]
            rhs = [jnp.concatenate([v_ref[rows[c], hh * GDN_DK:(hh + 1) * GDN_DK] * b, kc[c] * (b * e)], axis=1)
                   for (c, hh), b, e in zip(inst, bc, e_c)]
            sol = [_dot(t, r) for t, r in zip(tinv, rhs)]
            for (c, hh), s, t, l, e, cc, r in zip(inst, sol, tinv, lm, e_c, gcc, gcr):
                hs = slice(hh * GDN_DK, (hh + 1) * GDN_DK)
                u_ref[rows[c], hs] = s[:, :GDN_DK]
                w_ref[rows[c], hs] = _mx(s[:, GDN_DK:])
                pm_ref[hh, c] = _mx(jnp.where(tri, qk[c] * l, 0.0))
                ti_ref[hh, c] = _mx(t)
                qd_ref[rows[c], hs] = _mx(qc[c] * e)
                kd_ref[rows[c], hs] = _mx(kc[c] * jnp.exp(r[:, C - 1:C] - cc))

    smem = pl.BlockSpec(memory_space=pltpu.SMEM)
    rows_spec = pl.BlockSpec((2, cpt, C), lambda j, i: (j, i, 0))
    qk_spec = pl.BlockSpec((tt, GDN_DK), lambda j, i: (i, j))
    v_spec = pl.BlockSpec((tt, 2 * GDN_DK), lambda j, i: (i, j))
    cc_spec = pl.BlockSpec((2, cpt, C, C), lambda j, i: (j, i, 0, 0))
    rows_shape = jax.ShapeDtypeStruct((GDN_HV, nC, C), F32)
    cc_shape = jax.ShapeDtypeStruct((GDN_HV, nC, C, C), MXU_DTYPE)
    return pl.pallas_call(
        body, grid=(GDN_HV // 2, T // tt),
        in_specs=[smem, smem, qk_spec, qk_spec, v_spec, rows_spec, rows_spec],
        out_specs=[v_spec, v_spec, cc_spec, cc_spec, rows_spec, rows_spec, rows_spec, v_spec, v_spec],
        out_shape=[jax.ShapeDtypeStruct((T, D_INNER), F32), jax.ShapeDtypeStruct((T, D_INNER), MXU_DTYPE),
                   cc_shape, cc_shape, rows_shape, rows_shape, rows_shape,
                   jax.ShapeDtypeStruct((T, D_INNER), MXU_DTYPE), jax.ShapeDtypeStruct((T, D_INNER), MXU_DTYPE)],
        compiler_params=_params("parallel", "parallel"), name=name,
    )(alog, dtb, q, k, v, araw, braw)


def _gdn_state_fwd(q, k, u, w, pm, gc, name):
    T = q.shape[0]
    C = GDN_CHUNK
    HG = GDN_SCAN_HEADS
    tt = min(T, 512)
    cpt, nC = tt // C, T // C

    def body(q_ref, k_ref, u_ref, w_ref, pm_ref, gc_ref, o_ref, vn_ref, sall_ref, S):
        @pl.when(pl.program_id(1) == 0)
        def _():
            S[...] = jnp.zeros_like(S)

        heads = list(range(HG))

        def chunk(c, carry):
            rows = pl.ds(pl.multiple_of(c * C, C), C)
            hs = [slice(h * GDN_DK, (h + 1) * GDN_DK) for h in heads]
            gl = [jnp.exp(gc_ref[h, pl.ds(c, 1), C - 1:C]) for h in heads]
            sv = [S[h] for h in heads]
            for h in heads:
                sall_ref[h, c] = _mx(sv[h])
            ws = [_dot(w_ref[rows, hs[h]], sv[h]) for h in heads]
            qsv = [_dot(q_ref[rows, hs[h]], sv[h]) for h in heads]
            vn = [u_ref[rows, hs[h]] - ws[h] for h in heads]
            pv = [_dot(pm_ref[h, c], vn[h]) for h in heads]
            kv = [_dot_tn(k_ref[rows, hs[h]], vn[h]) for h in heads]
            for h in heads:
                vn_ref[rows, hs[h]] = _mx(vn[h])
                o_ref[rows, hs[h]] = qsv[h] + pv[h]
                S[h] = sv[h] * gl[h] + kv[h]
            return carry

        lax.fori_loop(0, cpt, chunk, 0)

    v_spec = pl.BlockSpec((tt, HG * GDN_DK), lambda g, i: (i, g))
    return pl.pallas_call(
        body, grid=(GDN_HV // HG, T // tt),
        in_specs=[v_spec, v_spec, v_spec, v_spec,
                  pl.BlockSpec((HG, cpt, C, C), lambda g, i: (g, i, 0, 0)),
                  pl.BlockSpec((HG, cpt, C), lambda g, i: (g, i, 0))],
        out_specs=[v_spec, v_spec, pl.BlockSpec((HG, cpt, GDN_DK, GDN_DK), lambda g, i: (g, i, 0, 0))],
        out_shape=[jax.ShapeDtypeStruct((T, D_INNER), F32), jax.ShapeDtypeStruct((T, D_INNER), MXU_DTYPE),
                   jax.ShapeDtypeStruct((GDN_HV, nC, GDN_DK, GDN_DK), MXU_DTYPE)],
        scratch_shapes=[pltpu.VMEM((HG, GDN_DK, GDN_DK), F32)],
        compiler_params=_params("parallel", "arbitrary"), name=name,
    )(q, k, u, w, pm, gc)


def _gdn_state_bwd(q, k, w, pm, vn, sall, gc, do, name):
    T = q.shape[0]
    C = GDN_CHUNK
    HG = GDN_SCAN_HEADS
    tt = min(T, 512)
    cpt, nC, nT = tt // C, T // C, T // tt

    def body(q_ref, k_ref, w_ref, pm_ref, vn_ref, sall_ref, gc_ref, do_ref, dvn_ref, dkd_ref, dgl_ref, dS):
        @pl.when(pl.program_id(1) == 0)
        def _():
            dS[...] = jnp.zeros_like(dS)

        heads = list(range(HG))

        def chunk(ci, carry):
            c = cpt - 1 - ci
            rows = pl.ds(pl.multiple_of(c * C, C), C)
            hs = [slice(h * GDN_DK, (h + 1) * GDN_DK) for h in heads]
            gl = [jnp.exp(gc_ref[h, pl.ds(c, 1), C - 1:C]) for h in heads]
            dsn = [dS[h] for h in heads]
            doc = [do_ref[rows, hs[h]] for h in heads]
            kds = [_dot(k_ref[rows, hs[h]], dsn[h]) for h in heads]
            pdo = [_dot_tn(pm_ref[h, c], doc[h]) for h in heads]
            dkd = [_dot_nt(vn_ref[rows, hs[h]], dsn[h]) for h in heads]
            qdo = [_dot_tn(q_ref[rows, hs[h]], doc[h]) for h in heads]
            dvn = [pdo[h] + kds[h] for h in heads]
            wdv = [_dot_tn(w_ref[rows, hs[h]], dvn[h]) for h in heads]
            for h in heads:
                dgl = jnp.sum(jnp.sum(dsn[h] * sall_ref[h, c].astype(F32), axis=0, keepdims=True), axis=1, keepdims=True)
                dgl_ref[h, pl.ds(c, 1), :] = jnp.broadcast_to(dgl, (1, C))
                dvn_ref[rows, hs[h]] = dvn[h]
                dkd_ref[rows, hs[h]] = dkd[h]
                dS[h] = dsn[h] * gl[h] + qdo[h] - wdv[h]
            return carry

        lax.fori_loop(0, cpt, chunk, 0)

    rev = lambda i: nT - 1 - i
    v_spec = pl.BlockSpec((tt, HG * GDN_DK), lambda g, i: (rev(i), g))
    rows_spec = pl.BlockSpec((HG, cpt, C), lambda g, i: (g, rev(i), 0))
    return pl.pallas_call(
        body, grid=(GDN_HV // HG, nT),
        in_specs=[v_spec, v_spec, v_spec, pl.BlockSpec((HG, cpt, C, C), lambda g, i: (g, rev(i), 0, 0)), v_spec,
                  pl.BlockSpec((HG, cpt, GDN_DK, GDN_DK), lambda g, i: (g, rev(i), 0, 0)), rows_spec, v_spec],
        out_specs=[v_spec, v_spec, rows_spec],
        out_shape=[jax.ShapeDtypeStruct((T, D_INNER), F32), jax.ShapeDtypeStruct((T, D_INNER), F32),
                   jax.ShapeDtypeStruct((GDN_HV, nC, C), F32)],
        scratch_shapes=[pltpu.VMEM((HG, GDN_DK, GDN_DK), F32)],
        compiler_params=_params("parallel", "arbitrary"), name=name,
    )(q, k, w, pm, vn, sall, gc, do)


def _gdn_local_bwd(q, k, v, gc, beta, tinv, u, w, pm, vn, sall, do, dvn, dkd, dgl, name):
    T = q.shape[0]
    C = GDN_CHUNK
    tt = min(T, 1024)
    cpt, nC = tt // C, T // C
    grp = min(cpt, GDN_LOCKSTEP_CHUNKS_BWD)

    def body(q_ref, k_ref, v_ref, gc_ref, b_ref, ti_ref, u_ref, w_ref, pm_ref, vn_ref, sall_ref, do_ref,
             dvn_ref, dkd_ref, dgl_ref, dq_ref, dk_ref, dv_ref, dg_ref, dbeta_ref, dgc_s):
        tri, strict, eye, r_i, c_i = _masks(C)
        lower = jnp.where(r_i >= c_i, 1.0, 0.0)
        lane = lax.broadcasted_iota(jnp.int32, (1, C), 1)
        rsum = lambda a: jnp.sum(a, axis=1, keepdims=True)
        for c0 in range(0, cpt, grp):
            cs = list(range(c0, c0 + grp))
            inst = [(c, hh) for c in cs for hh in range(2)]
            n = len(inst)
            rows = {c: slice(c * C, (c + 1) * C) for c in cs}
            hsl = [slice(hh * GDN_DK, (hh + 1) * GDN_DK) for c, hh in inst]
            qc = {c: q_ref[rows[c], :] for c in cs}
            kc = {c: k_ref[rows[c], :] for c in cs}
            kk = {c: _dot_nt(kc[c], kc[c]) for c in cs}
            gcr = [gc_ref[hh, c:c + 1, :] for c, hh in inst]
            gcc = [_col(r, eye) for r in gcr]
            bc = [_col(b_ref[hh, c:c + 1, :], eye) for c, hh in inst]
            lm = [jnp.exp(jnp.where(tri, cc - r, -1e30)) for cc, r in zip(gcc, gcr)]
            e_c = [jnp.exp(cc) for cc in gcc]
            el_c = [jnp.exp(r[:, C - 1:C] - cc) for cc, r in zip(gcc, gcr)]
            gl = [jnp.exp(r[:, C - 1:C]) for r in gcr]
            doc = [do_ref[rows[c], hsl[i]] for i, (c, hh) in enumerate(inst)]
            dvn = [dvn_ref[rows[c], hsl[i]] for i, (c, hh) in enumerate(inst)]
            sv = [sall_ref[hh, c] for c, hh in inst]
            aa = [_dot_nt(jnp.concatenate([_mx(doc[i]), _mx(dvn[i])], axis=0), sv[i]) for i in range(n)]
            dpm = [jnp.where(tri, _dot_nt(doc[i], vn_ref[rows[c], hsl[i]]), 0.0) for i, (c, hh) in enumerate(inst)]
            dqd = [a[:C] for a in aa]
            drhs = [_dot_tn(ti_ref[hh, c], jnp.concatenate([dvn[i], -aa[i][C:]], axis=1))
                    for i, (c, hh) in enumerate(inst)]
            sol = [jnp.concatenate([_mx(u_ref[rows[c], hsl[i]]), w_ref[rows[c], hsl[i]]], axis=1)
                   for i, (c, hh) in enumerate(inst)]
            dnm = [-jnp.where(strict, _dot_nt(drhs[i], sol[i]), 0.0) for i in range(n)]
            dkk = [dnm[i] * bc[i] * lm[i] for i in range(n)]
            dqk = [dpm[i] * lm[i] for i in range(n)]
            dq1 = [_dot(dqk[i], kc[c]) for i, (c, hh) in enumerate(inst)]
            dk1 = [_dot(dkk[i], kc[c]) for i, (c, hh) in enumerate(inst)]
            dk2 = [_dot_tn(dkk[i], kc[c]) for i, (c, hh) in enumerate(inst)]
            dk3 = [_dot_tn(dqk[i], qc[c]) for i, (c, hh) in enumerate(inst)]
            dq_acc = {c: jnp.zeros((C, GDN_DK), F32) for c in cs}
            dk_acc = {c: jnp.zeros((C, GDN_DK), F32) for c in cs}
            for i, (c, hh) in enumerate(inst):
                k_, q_, v_ = kc[c], qc[c], v_ref[rows[c], hsl[i]]
                dvb, dkbe = drhs[i][:, :GDN_DK], drhs[i][:, GDN_DK:]
                dkd = dkd_ref[rows[c], hsl[i]]
                kb = k_ * bc[i]
                dkb = dkbe * e_c[i]
                del_el = dkd * k_ * el_c[i]
                dbc = rsum(dnm[i] * kk[c] * lm[i]) + rsum(dkb * k_ + dvb * v_)
                dq_acc[c] = dq_acc[c] + dq1[i] + dqd[i] * e_c[i]
                dk_acc[c] = dk_acc[c] + dk1[i] + dk2[i] + dk3[i] + dkd * el_c[i] + dkb * bc[i]
                dv_ref[rows[c], hsl[i]] = dvb * bc[i]
                nm = jnp.where(strict, kk[c] * bc[i] * lm[i], 0.0)
                gm = dnm[i] * nm + dpm[i] * pm_ref[hh, c].astype(F32)
                dgc_col = rsum(gm) + rsum((dkbe * kb + dqd[i] * q_) * e_c[i] - del_el)
                dglast = (jnp.sum(jnp.sum(del_el, axis=0, keepdims=True), axis=1, keepdims=True)
                          + dgl_ref[hh, c:c + 1, 0:1] * gl[i])
                dgc_s[hh, c:c + 1, :] = (_row(dgc_col, eye) - jnp.sum(gm, axis=0, keepdims=True)
                                         + jnp.where(lane == C - 1, dglast, 0.0))
                dbeta_ref[hh, c:c + 1, :] = _row(dbc, eye)
            for c in cs:
                dq_ref[rows[c], :] = dq_acc[c]
                dk_ref[rows[c], :] = dk_acc[c]
        for hh in range(2):
            dg_ref[hh] = _dot_hi(dgc_s[hh], lower)

    rows_spec = pl.BlockSpec((2, cpt, C), lambda j, i: (j, i, 0))
    qk_spec = pl.BlockSpec((tt, GDN_DK), lambda j, i: (i, j))
    v_spec = pl.BlockSpec((tt, 2 * GDN_DK), lambda j, i: (i, j))
    cc_spec = pl.BlockSpec((2, cpt, C, C), lambda j, i: (j, i, 0, 0))
    rows_shape = jax.ShapeDtypeStruct((GDN_HV, nC, C), F32)
    return pl.pallas_call(
        body, grid=(GDN_HV // 2, T // tt),
        in_specs=[qk_spec, qk_spec, v_spec, rows_spec, rows_spec, cc_spec, v_spec, v_spec, cc_spec, v_spec,
                  pl.BlockSpec((2, cpt, GDN_DK, GDN_DK), lambda j, i: (j, i, 0, 0)), v_spec, v_spec, v_spec, rows_spec],
        out_specs=[qk_spec, qk_spec, v_spec, rows_spec, rows_spec],
        out_shape=[jax.ShapeDtypeStruct((T, GDN_HV // 2 * GDN_DK), F32),
                   jax.ShapeDtypeStruct((T, GDN_HV // 2 * GDN_DK), F32),
                   jax.ShapeDtypeStruct((T, D_INNER), F32), rows_shape, rows_shape],
        scratch_shapes=[pltpu.VMEM((2, cpt, C), F32)],
        compiler_params=_params("parallel", "parallel"), name=name,
    )(q, k, v, gc, beta, tinv, u, w, pm, vn, sall, do, dvn, dkd, dgl)


def _gdn_gate_bwd(araw, braw, dg, dbeta, alog, dtb, name):
    H, T = araw.shape

    def body(a_ref, b_ref, dg_ref, dbt_ref, alog_ref, dtb_ref, da_ref, db_ref, dalog_ref, ddtb_ref):
        xa = a_ref[...] + dtb_ref[...]
        ea = jnp.exp(alog_ref[...])
        dgv = dg_ref[...]
        da = -dgv * ea * _sigmoid(xa)
        da_ref[...] = da
        dalog_ref[...] = jnp.sum(-dgv * ea * _softplus(xa), axis=1, keepdims=True)
        ddtb_ref[...] = jnp.sum(da, axis=1, keepdims=True)
        bt = _sigmoid(b_ref[...])
        db_ref[...] = dbt_ref[...] * bt * (1.0 - bt)

    return pl.pallas_call(
        body,
        out_shape=[jax.ShapeDtypeStruct((H, T), F32), jax.ShapeDtypeStruct((H, T), F32),
                   jax.ShapeDtypeStruct((H, 1), F32), jax.ShapeDtypeStruct((H, 1), F32)],
        compiler_params=pltpu.CompilerParams(vmem_limit_bytes=VMEM_LIMIT_BYTES), name=name,
    )(araw, braw, dg, dbeta, alog, dtb)


SSD_LOCKSTEP_CHUNKS = 2
SSD_LOCKSTEP_CHUNKS_BWD = 1
SSD_LOCKSTEP_HEADS_BWD = 2


def _ssd_scan_fwd(xs, bm, cm, dtraw, alog, dtb, dskip, name):
    T = xs.shape[0]
    Q = SSD_CHUNK
    tt = min(T, 1024)
    cpt, nC = tt // Q, T // Q
    GW = SSD_R * SSD_P

    def body(alog_ref, dtb_ref, dsk_ref, xs_ref, b_ref, c_ref, dt_ref, y_ref, sall_ref, dto_ref, S, dt_s, acs_s):
        gi, i = pl.program_id(0), pl.program_id(1)

        @pl.when(i == 0)
        def _():
            S[...] = jnp.zeros_like(S)

        tri, _, eye, r_i, c_i = _masks(Q)
        upper = jnp.where(r_i <= c_i, 1.0, 0.0)
        for r in range(SSD_R):
            h = SSD_R * gi + r
            dt = _softplus(dt_ref[r] + dtb_ref[h])
            dto_ref[r] = dt
            dt_s[r] = dt
            acs_s[r] = _dot_hi(-jnp.exp(alog_ref[h]) * dt, upper)

        ps = [slice(r * SSD_P, (r + 1) * SSD_P) for r in range(SSD_R)]
        s_cur = [S[:, ps[r]] for r in range(SSD_R)]
        grp = min(cpt, SSD_LOCKSTEP_CHUNKS)
        for c0 in range(0, cpt, grp):
            cs = list(range(c0, c0 + grp))
            inst = [(c, r) for c in cs for r in range(SSD_R)]
            rows = {c: slice(c * Q, (c + 1) * Q) for c in cs}
            bc_ = {c: b_ref[rows[c], :] for c in cs}
            cc_ = {c: c_ref[rows[c], :] for c in cs}
            cb = {c: _dot_nt(cc_[c], bc_[c]) for c in cs}
            xr = [xs_ref[rows[c], ps[r]] for c, r in inst]
            acr = [acs_s[r, c:c + 1, :] for c, r in inst]
            acc = [_col_bcast(a, Q) for a in acr]
            dtr = [dt_s[r, c:c + 1, :] for c, r in inst]
            mm = [cb[c] * (jnp.exp(jnp.where(tri, acc[i] - acr[i], -1e30)) * dtr[i]) for i, (c, r) in enumerate(inst)]
            bct = {c: bc_[c].T for c in cs}
            st = [_dot(bct[c] * (jnp.exp(acr[i][:, Q - 1:Q] - acr[i]) * dtr[i]), xr[i]) for i, (c, r) in enumerate(inst)]
            yd = [_dot(mm[i], xr[i]) for i in range(len(inst))]
            s_prev = []
            for i, (c, r) in enumerate(inst):
                s_prev.append(s_cur[r])
                s_cur[r] = s_cur[r] * jnp.exp(acr[i][:, Q - 1:Q]) + st[i]
            yo = [_dot(cc_[c] * jnp.exp(acc[i]), s_prev[i]) for i, (c, r) in enumerate(inst)]
            for i, (c, r) in enumerate(inst):
                sall_ref[0, c, :, ps[r]] = s_prev[i]
                y_ref[rows[c], ps[r]] = yd[i] + yo[i] + dsk_ref[SSD_R * gi + r] * xr[i]
        for r in range(SSD_R):
            S[:, ps[r]] = s_cur[r]

    smem = pl.BlockSpec(memory_space=pltpu.SMEM)
    rows_spec = pl.BlockSpec((SSD_R, cpt, Q), lambda g, i: (g, i, 0))
    return pl.pallas_call(
        body, grid=(SSD_G, T // tt),
        in_specs=[smem, smem, smem,
                  pl.BlockSpec((tt, GW), lambda g, i: (i, g)), pl.BlockSpec((tt, SSD_N), lambda g, i: (i, g)),
                  pl.BlockSpec((tt, SSD_N), lambda g, i: (i, g)), rows_spec],
        out_specs=[pl.BlockSpec((tt, GW), lambda g, i: (i, g)),
                   pl.BlockSpec((1, cpt, SSD_N, GW), lambda g, i: (g, i, 0, 0)), rows_spec],
        out_shape=[jax.ShapeDtypeStruct((T, D_INNER), F32), jax.ShapeDtypeStruct((SSD_G, nC, SSD_N, GW), F32),
                   jax.ShapeDtypeStruct((SSD_H, nC, Q), F32)],
        scratch_shapes=[pltpu.VMEM((SSD_N, GW), F32), pltpu.VMEM((SSD_R, cpt, Q), F32),
                        pltpu.VMEM((SSD_R, cpt, Q), F32)],
        compiler_params=_params("parallel", "arbitrary"), name=name,
    )(alog, dtb, dskip, xs, bm, cm, dtraw)


def _ssd_scan_bwd(xs, bm, cm, dt, sall, dy, alog, dskip, name):
    T = xs.shape[0]
    Q = SSD_CHUNK
    tt = min(T, 1024)
    cpt, nC, nT = tt // Q, T // Q, T // tt
    GW = SSD_R * SSD_P

    def body(alog_ref, dsk_ref, xs_ref, b_ref, c_ref, dt_ref, sall_ref, dy_ref,
             dxs_ref, db_ref, dc_ref, da_ref, ddt_ref, dd_ref, dS, acs_s, dacs_s, ddt_s, dd_s):
        gi, i = pl.program_id(0), pl.program_id(1)

        @pl.when(i == 0)
        def _():
            dS[...] = jnp.zeros_like(dS)

        tri, _, eye, r_i, c_i = _masks(Q)
        upper = jnp.where(r_i <= c_i, 1.0, 0.0)
        lower = jnp.where(r_i >= c_i, 1.0, 0.0)
        lane = lax.broadcasted_iota(jnp.int32, (1, Q), 1)
        for r in range(SSD_R):
            acs_s[r] = _dot_hi(-jnp.exp(alog_ref[SSD_R * gi + r]) * dt_ref[r], upper)

        ps = [slice(r * SSD_P, (r + 1) * SSD_P) for r in range(SSD_R)]
        ds_cur = [dS[:, ps[r]] for r in range(SSD_R)]
        grp = min(cpt, SSD_LOCKSTEP_CHUNKS_BWD)
        csum = lambda a: jnp.sum(a, axis=0, keepdims=True)
        tsum = lambda a: jnp.sum(csum(a), axis=1, keepdims=True)
        ones8 = jnp.ones((8, SSD_P), F32)
        for c0 in range(cpt - grp, -1, -grp):
            cs = list(range(c0 + grp - 1, c0 - 1, -1))
            rows = {c: slice(c * Q, (c + 1) * Q) for c in cs}
            bc_ = {c: b_ref[rows[c], :] for c in cs}
            cc_ = {c: c_ref[rows[c], :] for c in cs}
            cb = {c: _dot_nt(cc_[c], bc_[c]) for c in cs}
            cbt = {c: _dot_nt(bc_[c], cc_[c]) for c in cs}
            bct = {c: bc_[c].T for c in cs}
            cct = {c: cc_[c].T for c in cs}
            dcb = {c: jnp.zeros((Q, Q), F32) for c in cs}
            dcbt = {c: jnp.zeros((Q, Q), F32) for c in cs}
            db_acc = {c: jnp.zeros((Q, SSD_N), F32) for c in cs}
            dc_acc = {c: jnp.zeros((Q, SSD_N), F32) for c in cs}
            for h0 in range(0, SSD_R, SSD_LOCKSTEP_HEADS_BWD):
                inst = [(c, r) for c in cs for r in range(h0, h0 + SSD_LOCKSTEP_HEADS_BWD)]
                n = len(inst)
                xr = [xs_ref[rows[c], ps[r]] for c, r in inst]
                dyr = [dy_ref[rows[c], ps[r]] for c, r in inst]
                acr = [acs_s[r, c:c + 1, :] for c, r in inst]
                dtr = [dt_ref[r, c:c + 1, :] for c, r in inst]
                acc = [_col_bcast(a, Q) for a in acr]
                dtb = [_col_bcast(d, Q) for d in dtr]
                al = [a[:, Q - 1:Q] for a in acr]
                e_c = [jnp.exp(a) for a in acc]
                dl_c = [jnp.exp(al[i] - acc[i]) for i in range(n)]
                e_r = [jnp.exp(a) for a in acr]
                dl_r = [jnp.exp(al[i] - acr[i]) for i in range(n)]
                gl = [jnp.exp(a) for a in al]
                lm = [jnp.exp(jnp.where(tri, acc[i] - acr[i], -1e30)) for i in range(n)]
                lmt = [jnp.exp(jnp.where(r_i <= c_i, acr[i] - acc[i], -1e30)) for i in range(n)]
                mmt = [cbt[c] * lmt[i] for i, (c, r) in enumerate(inst)]
                sr = [sall_ref[0, c, :, ps[r]] for c, r in inst]
                dmm0 = [_dot_nt(dyr[i], xr[i]) for i in range(n)]
                dmm0t = [_dot_nt(xr[i], dyr[i]) for i in range(n)]
                dxd1 = [_dot(mmt[i], dyr[i]) for i in range(n)]
                dce = [_dot_nt(dyr[i], sr[i]) for i in range(n)]
                dcet = [_dot_nt(sr[i], dyr[i]) for i in range(n)]
                cdy = [_dot(cct[c] * e_r[i], dyr[i]) for i, (c, r) in enumerate(inst)]
                dsn = []
                for i, (c, r) in enumerate(inst):
                    dsn.append(ds_cur[r])
                    ds_cur[r] = gl[i] * ds_cur[r] + cdy[i]
                dxd = [dxd1[i] + _dot(bc_[c] * dl_c[i], dsn[i]) for i, (c, r) in enumerate(inst)]
                dbd0 = [_dot_nt(xr[i], dsn[i]) for i in range(n)]
                dbd0t = [_dot_nt(dsn[i], xr[i]) for i in range(n)]
                for i, (c, r) in enumerate(inst):
                    dgl = tsum(dsn[i] * sr[i])
                    dc_acc[c] = dc_acc[c] + dce[i] * e_c[i]
                    db_acc[c] = db_acc[c] + dbd0[i] * (dtb[i] * dl_c[i])
                    dl0 = dmm0[i] * lm[i]
                    dl0t = dmm0t[i] * (lmt[i] * dtb[i])
                    dcb[c] = dcb[c] + dl0 * dtr[i]
                    dcbt[c] = dcbt[c] + dl0t
                    csum_gm0 = csum(dl0 * cb[c])
                    rsum_gm = csum(dl0t * cbt[c])
                    r_de = csum(dcet[i] * cct[c]) * e_r[i]
                    r_dl = csum(dbd0t[i] * bct[c]) * dl_r[i]
                    dalast = jnp.sum(r_dl * dtr[i], axis=1, keepdims=True) + dgl * gl[i]
                    dacs_s[r, c:c + 1, :] = (rsum_gm + r_de - (r_dl + csum_gm0) * dtr[i]
                                             + jnp.where(lane == Q - 1, dalast, 0.0))
                    ddt_s[r, c:c + 1, :] = csum_gm0 + r_dl
                    dd_s[r, c:c + 1, :] = _dot_nt(ones8, dyr[i] * xr[i])[0:1]
                    dxs_ref[rows[c], ps[r]] = dxd[i] * dtb[i][:, :SSD_P] + dsk_ref[SSD_R * gi + r] * dyr[i]
            for c in cs:
                dc_ref[rows[c], :] = dc_acc[c] + _dot(dcb[c], bc_[c])
                db_ref[rows[c], :] = db_acc[c] + _dot(dcbt[c], cc_[c])
        for r in range(SSD_R):
            dS[:, ps[r]] = ds_cur[r]
        for r in range(SSD_R):
            da_ref[r] = _dot_hi(dacs_s[r], lower)
            ddt_ref[r] = ddt_s[r]
            dd_ref[r] = dd_s[r]

    rev = lambda i: nT - 1 - i
    smem = pl.BlockSpec(memory_space=pltpu.SMEM)
    rows_spec = pl.BlockSpec((SSD_R, cpt, Q), lambda g, i: (g, rev(i), 0))
    x_spec = pl.BlockSpec((tt, GW), lambda g, i: (rev(i), g))
    n_spec = pl.BlockSpec((tt, SSD_N), lambda g, i: (rev(i), g))
    rows_shape = jax.ShapeDtypeStruct((SSD_H, nC, Q), F32)
    return pl.pallas_call(
        body, grid=(SSD_G, nT),
        in_specs=[smem, smem, x_spec, n_spec, n_spec, rows_spec,
                  pl.BlockSpec((1, cpt, SSD_N, GW), lambda g, i: (g, rev(i), 0, 0)), x_spec],
        out_specs=[x_spec, n_spec, n_spec, rows_spec, rows_spec, rows_spec],
        out_shape=[jax.ShapeDtypeStruct((T, D_INNER), F32), jax.ShapeDtypeStruct((T, SSD_G * SSD_N), F32),
                   jax.ShapeDtypeStruct((T, SSD_G * SSD_N), F32), rows_shape, rows_shape, rows_shape],
        scratch_shapes=[pltpu.VMEM((SSD_N, GW), F32)] + [pltpu.VMEM((SSD_R, cpt, Q), F32)] * 4,
        compiler_params=_params("parallel", "arbitrary"), name=name,
    )(alog, dskip, xs, bm, cm, dt, sall, dy)


def _ssd_gate_bwd(dtraw, dt, da, ddt_direct, ddrow, alog, dtb, name):
    H, T = dtraw.shape

    def body(raw_ref, dt_ref, da_ref, ddt_ref, dd_ref, alog_ref, dtb_ref, draw_ref, dalog_ref, ddtb_ref, dD_ref):
        a = -jnp.exp(alog_ref[...])
        dav = da_ref[...]
        ddt = ddt_ref[...] + dav * a
        draw = ddt * _sigmoid(raw_ref[...] + dtb_ref[...])
        draw_ref[...] = draw
        dalog_ref[...] = jnp.sum(dav * dt_ref[...], axis=1, keepdims=True) * a
        ddtb_ref[...] = jnp.sum(draw, axis=1, keepdims=True)
        dD_ref[...] = jnp.sum(dd_ref[...], axis=1, keepdims=True)

    return pl.pallas_call(
        body,
        out_shape=[jax.ShapeDtypeStruct((H, T), F32)] + [jax.ShapeDtypeStruct((H, 1), F32)] * 3,
        compiler_params=pltpu.CompilerParams(vmem_limit_bytes=VMEM_LIMIT_BYTES), name=name,
    )(dtraw, dt, da, ddt_direct, ddrow, alog, dtb)


def _adamw(parts, w, m, v, name):
    R, C = w.shape
    tr = 256 if R % 256 == 0 else R

    def body(p_ref, w_ref, m_ref, v_ref, g_ref, d_ref, nm_ref, nv_ref):
        g = p_ref[0].astype(F32)
        for s in range(1, N_DEV):
            g = g + p_ref[s].astype(F32)
        mn = ADAM_B1 * m_ref[...] + (1.0 - ADAM_B1) * g
        vn = ADAM_B2 * v_ref[...] + (1.0 - ADAM_B2) * (g * g)
        mh = mn / (1.0 - ADAM_B1 ** ADAM_STEP)
        vh = vn / (1.0 - ADAM_B2 ** ADAM_STEP)
        g_ref[...] = g
        d_ref[...] = -ADAM_LR * (mh / (jnp.sqrt(vh) + ADAM_EPS) + ADAM_WD * w_ref[...])
        nm_ref[...] = mn
        nv_ref[...] = vn

    blk = pl.BlockSpec((tr, C), lambda i: (i, 0))
    return pl.pallas_call(
        body, grid=(R // tr,),
        in_specs=[pl.BlockSpec((N_DEV, tr, C), lambda i: (0, i, 0)), blk, blk, blk],
        out_specs=[blk] * 4,
        out_shape=[jax.ShapeDtypeStruct((R, C), F32)] * 4,
        compiler_params=_params("parallel"), name=name,
    )(parts, w, m, v)


def _me():
    x, y, c = lax.axis_index("x"), lax.axis_index("y"), lax.axis_index("c")
    return x, y, c


def _peer(d):
    x, y, c = _me()
    px = 1 - x if d & 4 else x
    py = 1 - y if d & 2 else y
    pc = 1 - c if d & 1 else c
    return (px, py, pc), 4 * px + 2 * py + pc


def _gather_two_level(arrs, name):
    n = len(arrs)

    def body(*refs):
        ins, outs = refs[:n], refs[n:2 * n]
        ssem, rsem, lsem = refs[2 * n:]
        x, y, c = _me()
        me, sibling = (x, y, c), (x, y, 1 - c)
        chips = [(1 - x, y), (x, 1 - y), (1 - x, 1 - y)]

        def slot(a, block):
            px, py, pc = block
            return outs[a].at[4 * px + 2 * py + pc]

        def copy(a, k, block, to, src=None):
            return pltpu.make_async_remote_copy(
                src_ref=slot(a, block) if src is None else src, dst_ref=slot(a, block),
                send_sem=ssem.at[a, k], recv_sem=rsem.at[a, k], device_id=to, device_id_type=MESH)

        mine = [pltpu.make_async_copy(ins[a], slot(a, me), lsem.at[a]) for a in range(n)]
        for cp in mine:
            cp.start()
        first = []
        for a in range(n):
            first.append(copy(a, 0, me, sibling, src=ins[a]))
            first += [copy(a, 1 + j, me, (*chip, c), src=ins[a]) for j, chip in enumerate(chips)]
        for cp in first:
            cp.start()
        passed = [[copy(a, 4 + j, (*chip, c), sibling) for j, chip in enumerate(chips)] for a in range(n)]
        for j, chip in enumerate(chips):
            for a in range(n):
                copy(a, 1 + j, (*chip, c), me).wait_recv()
                passed[a][j].start()
        for a in range(n):
            copy(a, 0, sibling, me).wait_recv()
            for j, chip in enumerate(chips):
                copy(a, 4 + j, (*chip, 1 - c), me).wait_recv()
        for cp in first + [cp for row in passed for cp in row]:
            cp.wait_send()
        for cp in mine:
            cp.wait()

    anyspec = pl.BlockSpec(memory_space=pl.ANY)
    return pl.pallas_call(
        body,
        in_specs=[anyspec] * n, out_specs=[anyspec] * n,
        out_shape=_exchange_out_shapes(arrs, [True] * n),
        scratch_shapes=_exchange_semaphores(n),
        name=name,
    )(*arrs)


def _exchange(arrs, bcast, name):
    n = len(arrs)

    def body(*refs):
        ex = _Exchange(refs[:n], refs[n:2 * n], bcast, *refs[2 * n:])
        ex.begin()
        ex.finish()

    anyspec = pl.BlockSpec(memory_space=pl.ANY)
    return pl.pallas_call(
        body,
        in_specs=[anyspec] * n, out_specs=[anyspec] * n,
        out_shape=_exchange_out_shapes(arrs, bcast),
        scratch_shapes=_exchange_semaphores(n),
        name=name,
    )(*arrs)


def _exchange_out_shapes(arrs, bcast):
    return [jax.ShapeDtypeStruct((N_DEV,) + (a.shape if b else a.shape[1:]), a.dtype) for a, b in zip(arrs, bcast)]


def _exchange_semaphores(n):
    return [pltpu.SemaphoreType.DMA((n, N_DEV - 1)), pltpu.SemaphoreType.DMA((n, N_DEV - 1)),
            pltpu.SemaphoreType.DMA((n,))]


class _Exchange:
    def __init__(self, ins, outs, bcast, ssem, rsem, lsem):
        n = len(ins)
        x, y, c = _me()
        me = 4 * x + 2 * y + c

        def src(a, dest):
            return ins[a] if bcast[a] else ins[a].at[dest]

        self.local = [pltpu.make_async_copy(src(a, me), outs[a].at[me], lsem.at[a]) for a in range(n)]
        self.sends, self.recvs = [], []
        for a in range(n):
            for d in range(1, N_DEV):
                peer, pid = _peer(d)
                self.sends.append(pltpu.make_async_remote_copy(
                    src_ref=src(a, pid), dst_ref=outs[a].at[me], send_sem=ssem.at[a, d - 1],
                    recv_sem=rsem.at[a, d - 1], device_id=peer, device_id_type=MESH))
                self.recvs.append(pltpu.make_async_remote_copy(
                    src_ref=src(a, pid), dst_ref=outs[a].at[pid], send_sem=ssem.at[a, d - 1],
                    recv_sem=rsem.at[a, d - 1], device_id=peer, device_id_type=MESH))

    def begin(self):
        for cp in self.local + self.sends:
            cp.start()

    def finish(self):
        for cp in self.recvs:
            cp.wait_recv()
        for cp in self.sends:
            cp.wait_send()
        for cp in self.local:
            cp.wait()


def _to_rows(cols, chunk):
    T, H = cols.shape
    return cols.T.reshape(H, T // chunk, chunk)


def _from_rows(rows):
    return rows.T


def _pad_cols(a, width):
    return jnp.pad(a, ((0, 0), (0, width - a.shape[1])))


def _local_step(x, tgt, p, late_weights=None, early_grads=None, late_grads=None):
    T = x.shape[0]
    zb = lambda n: jnp.zeros((1, n), F32)
    gw = p["gdn_w_in"]
    g_wparts = [gw[:, 0:1024], gw[:, 1024:2048], gw[:, 2048:4096], gw[:, 4096:6144], _pad_cols(gw[:, 6144:6176], PAD_W)]
    nw0, nw1 = p["norm_w"][0:1], p["norm_w"][1:2]
    gcw = p["gdn_conv_w"]
    cw_q, cw_k, cw_v = gcw[:, 0:1024], gcw[:, 1024:2048], gcw[:, 2048:4096]
    g_convs = [(cw_q, zb(1024), True, GDN_DK ** -0.5), (cw_k, zb(1024), True, 1.0), (cw_v, zb(2048), False, 1.0),
               None, None]
    if late_weights is None:
        h0, (q_pre, k_pre, v_pre, z0, ab), (q, k, v), g_cpre = _norm_inproj(x, nw0, g_wparts, g_convs, "gdn_inproj")
    else:
        comm, assemble = late_weights
        h0, (q_pre, k_pre, v_pre, z0, ab), (q, k, v), g_cpre, gathered = _norm_inproj(x, nw0, g_wparts, g_convs,
                                                                                      "gdn_inproj", comm)
        p = dict(p, **assemble(gathered))
    braw = _to_rows(ab[:, 0:GDN_HV], GDN_CHUNK)
    araw = _to_rows(ab[:, GDN_HV:2 * GDN_HV], GDN_CHUNK)
    g_alog, g_dtb = p["gdn_a_log"].reshape(-1), p["gdn_dt_bias"].reshape(-1)
    g_u, g_w, g_pm, g_ti, g_rows, beta_rows, gc_rows, g_qd, g_kd = _gdn_prep(q, k, v, araw, braw, g_alog, g_dtb,
                                                                             "gdn_prep")
    o0, g_vn, g_sall = _gdn_state_fwd(g_qd, g_kd, g_u, g_w, g_pm, gc_rows, "gdn_state_fwd")
    x1 = _out_fwd(o0, z0, p["gdn_norm_w"], p["gdn_w_out"], x, GDN_DK, False, "gdn_out")
    sw = p["ssd_w_in"]
    s_wparts = [sw[:, 0:2048], sw[:, 2048:4096], sw[:, 4096:5120], sw[:, 5120:6144], _pad_cols(sw[:, 6144:6176], PAD_W)]
    scw, scb = p["ssd_conv_w"], p["ssd_conv_b"]
    s_convs = [None, (scw[:, 0:2048], scb[:, 0:2048], False, 1.0), (scw[:, 2048:3072], scb[:, 2048:3072], False, 1.0),
               (scw[:, 3072:4096], scb[:, 3072:4096], False, 1.0), None]
    h1, (z1, xs_pre, b_pre, c_pre, dtp), (xs, bm, cm), s_cpre = _norm_inproj(x1, nw1, s_wparts, s_convs, "ssd_inproj")
    dtraw = _to_rows(dtp[:, 0:SSD_H], SSD_CHUNK)
    s_alog, s_dtb, s_d = p["ssd_a_log"].reshape(-1), p["ssd_dt_bias"].reshape(-1), p["ssd_d"].reshape(-1)
    y1, s_sall, dt_rows = _ssd_scan_fwd(xs, bm, cm, dtraw, s_alog, s_dtb, s_d, "ssd_scan_fwd")
    dx2, d_fw, loss = _out_fwd(y1, z1, p["ssd_norm_w"], p["ssd_w_out"], x1, D_INNER // SSD_G, True, "ssd_out_loss",
                               (p["final_norm_w"].reshape(1, -1), tgt))
    dy1, dz1, d_snw, yn1 = _out_bwd(dx2, y1, z1, p["ssd_norm_w"], p["ssd_w_out"], D_INNER // SSD_G, True, "ssd_out_bwd")
    d_swout = _matmul_tn(yn1, dx2, "ssd_wout_grad")
    dxs, dbm, dcm, da_rows, ddt_rows, dd_rows = _ssd_scan_bwd(xs, bm, cm, dt_rows, s_sall, dy1, s_alog, s_d, "ssd_scan_bwd")
    col = lambda a: a.reshape(-1, 1)
    dtraw_g, d_salog, d_sdtb, d_sd = _ssd_gate_bwd(
        dtraw.reshape(SSD_H, T), dt_rows.reshape(SSD_H, T), da_rows.reshape(SSD_H, T),
        ddt_rows.reshape(SSD_H, T), dd_rows.reshape(SSD_H, T), col(s_alog), col(s_dtb), "ssd_gate_bwd")
    dxs_pre, dcw_x, dcb_x = _conv_bwd(xs_pre, s_cpre[0], scw[:, 0:2048], dxs, False, 1.0, "ssd_conv_x_bwd")
    db_pre, dcw_b, dcb_b = _conv_bwd(b_pre, s_cpre[1], scw[:, 2048:3072], dbm, False, 1.0, "ssd_conv_b_bwd")
    dc_pre, dcw_c, dcb_c = _conv_bwd(c_pre, s_cpre[2], scw[:, 3072:4096], dcm, False, 1.0, "ssd_conv_c_bwd")
    ddtp = _pad_cols(_from_rows(dtraw_g), PAD_W)
    s_dparts = [dz1, dxs_pre, db_pre, dc_pre, ddtp]
    dx1, d_nw1 = _inproj_bwd(x1, nw1, s_dparts, s_wparts, dx2, "ssd_inproj_bwd")
    s_dw = [_matmul_tn(h1, d, "ssd_win_grad_%d" % n) for n, d in enumerate(s_dparts)]
    d_swin = jnp.concatenate(s_dw[:4] + [s_dw[4][:, 0:SSD_H]], axis=1)
    early_recv = None
    if early_grads is None:
        do0, dz0, d_gnw, yn0 = _out_bwd(dx1, o0, z0, p["gdn_norm_w"], p["gdn_w_out"], GDN_DK, False, "gdn_out_bwd")
    else:
        do0, dz0, d_gnw, yn0, early_recv = _out_bwd(dx1, o0, z0, p["gdn_norm_w"], p["gdn_w_out"], GDN_DK, False,
                                                    "gdn_out_bwd", early_grads(d_swin, d_swout))
    d_gwout = _matmul_tn(yn0, dx1, "gdn_wout_grad")
    g_dvn, g_dkd, g_dgl = _gdn_state_bwd(g_qd, g_kd, g_w, g_pm, g_vn, g_sall, gc_rows, do0, "gdn_state_bwd")
    dq, dk, dv, dg_rows, dbeta_rows = _gdn_local_bwd(q, k, v, gc_rows, beta_rows, g_ti, g_u, g_w, g_pm, g_vn, g_sall,
                                                     do0, g_dvn, g_dkd, g_dgl, "gdn_local_bwd")
    da_g, db_g, d_galog, d_gdtb = _gdn_gate_bwd(
        araw.reshape(GDN_HV, T), braw.reshape(GDN_HV, T), dg_rows.reshape(GDN_HV, T),
        dbeta_rows.reshape(GDN_HV, T), col(g_alog), col(g_dtb), "gdn_gate_bwd")
    dq_pre, dcw_q, _ = _conv_bwd(q_pre, g_cpre[0], cw_q, dq, True, GDN_DK ** -0.5, "gdn_conv_q_bwd")
    dk_pre, dcw_k, _ = _conv_bwd(k_pre, g_cpre[1], cw_k, dk, True, 1.0, "gdn_conv_k_bwd")
    dv_pre, dcw_v, _ = _conv_bwd(v_pre, g_cpre[2], cw_v, dv, False, 1.0, "gdn_conv_v_bwd")
    dab = _pad_cols(jnp.concatenate([_from_rows(db_g), _from_rows(da_g)], axis=1), PAD_W)
    g_dparts = [dq_pre, dk_pre, dv_pre, dz0, dab]
    g_dw = [_matmul_tn(h0, d, "gdn_win_grad_%d" % n) for n, d in enumerate(g_dparts)]
    d_gwin = jnp.concatenate(g_dw[:4] + [g_dw[4][:, 0:2 * GDN_HV]], axis=1)
    sharded_grads = {
        "gdn_w_in": d_gwin, "gdn_w_out": d_gwout,
        "gdn_conv_w": jnp.concatenate([dcw_q, dcw_k, dcw_v], axis=1),
        "ssd_conv_w": jnp.concatenate([dcw_x, dcw_b, dcw_c], axis=1),
        "ssd_conv_b": jnp.concatenate([dcb_x, dcb_b, dcb_c], axis=1), "ssd_norm_w": d_snw}
    late_recv = None
    if late_grads is None:
        dx0, d_nw0 = _inproj_bwd(x, nw0, g_dparts, g_wparts, dx1, "gdn_inproj_bwd")
    else:
        dx0, d_nw0, late_recv = _inproj_bwd(x, nw0, g_dparts, g_wparts, dx1, "gdn_inproj_bwd",
                                            late_grads(sharded_grads))
    grads = {
        "norm_w": jnp.concatenate([d_nw0, d_nw1], axis=0),
        "gdn_w_in": d_gwin,
        "gdn_conv_w": jnp.concatenate([dcw_q, dcw_k, dcw_v], axis=1),
        "gdn_a_log": d_galog.reshape(1, -1),
        "gdn_dt_bias": d_gdtb.reshape(1, -1),
        "gdn_norm_w": d_gnw,
        "gdn_w_out": d_gwout,
        "ssd_w_in": d_swin,
        "ssd_conv_w": jnp.concatenate([dcw_x, dcw_b, dcw_c], axis=1),
        "ssd_conv_b": jnp.concatenate([dcb_x, dcb_b, dcb_c], axis=1),
        "ssd_dt_bias": d_sdtb.reshape(1, -1),
        "ssd_a_log": d_salog.reshape(1, -1),
        "ssd_d": d_sd.reshape(1, -1),
        "ssd_norm_w": d_snw,
        "ssd_w_out": d_swout,
        "final_norm_w": d_fw,
    }
    if early_grads is not None:
        return loss, dx0, grads, early_recv, late_recv
    return loss, dx0, grads


WEIGHTS = ["norm_w", "gdn_w_in", "gdn_conv_w", "gdn_a_log", "gdn_dt_bias", "gdn_norm_w", "gdn_w_out", "ssd_w_in",
           "ssd_conv_w", "ssd_conv_b", "ssd_dt_bias", "ssd_a_log", "ssd_d", "ssd_norm_w", "ssd_w_out", "final_norm_w"]
COL_SHARDED = ["gdn_w_in", "ssd_w_in"]
ROW_SHARDED = ["gdn_w_out", "ssd_w_out"]
SMALL_SHARDED = ["gdn_conv_w", "ssd_conv_w", "ssd_conv_b", "ssd_norm_w"]
REPLICATED = ["norm_w", "gdn_a_log", "gdn_dt_bias", "gdn_norm_w", "ssd_dt_bias", "ssd_a_log", "ssd_d", "final_norm_w"]


def _pack(arrs):
    return jnp.concatenate([a.reshape(-1) for a in arrs]).reshape(1, -1)


def _unpack(flat, shapes):
    out, pos = [], 0
    for s in shapes:
        n = 1
        for dim in s:
            n *= dim
        out.append(flat[pos:pos + n].reshape(s))
        pos += n
    return out


def _cols_to_shards(full):
    R, C = full.shape
    return full.reshape(R, N_DEV, C // N_DEV).transpose(1, 0, 2)


def _shards_to_cols(shards):
    n, R, c = shards.shape
    return shards.transpose(1, 0, 2).reshape(R, n * c)


def kernel(x, norm_w, gdn_w_in, gdn_conv_w, gdn_a_log, gdn_dt_bias, gdn_norm_w, gdn_w_out, ssd_w_in, ssd_conv_w, ssd_conv_b, ssd_dt_bias, ssd_a_log, ssd_d, ssd_norm_w, ssd_w_out, final_norm_w, loss_target, m_norm_w, m_gdn_w_in, m_gdn_conv_w, m_gdn_a_log, m_gdn_dt_bias, m_gdn_norm_w, m_gdn_w_out, m_ssd_w_in, m_ssd_conv_w, m_ssd_conv_b, m_ssd_dt_bias, m_ssd_a_log, m_ssd_d, m_ssd_norm_w, m_ssd_w_out, m_final_norm_w, v_norm_w, v_gdn_w_in, v_gdn_conv_w, v_gdn_a_log, v_gdn_dt_bias, v_gdn_norm_w, v_gdn_w_out, v_ssd_w_in, v_ssd_conv_w, v_ssd_conv_b, v_ssd_dt_bias, v_ssd_a_log, v_ssd_d, v_ssd_norm_w, v_ssd_w_out, v_final_norm_w):
    w = dict(norm_w=norm_w, gdn_w_in=gdn_w_in[0], gdn_conv_w=gdn_conv_w[0], gdn_a_log=gdn_a_log,
             gdn_dt_bias=gdn_dt_bias, gdn_norm_w=gdn_norm_w, gdn_w_out=gdn_w_out[0], ssd_w_in=ssd_w_in[0],
             ssd_conv_w=ssd_conv_w[0], ssd_conv_b=ssd_conv_b, ssd_dt_bias=ssd_dt_bias, ssd_a_log=ssd_a_log,
             ssd_d=ssd_d, ssd_norm_w=ssd_norm_w, ssd_w_out=ssd_w_out[0], final_norm_w=final_norm_w.reshape(1, -1))
    m = dict(norm_w=m_norm_w, gdn_w_in=m_gdn_w_in[0], gdn_conv_w=m_gdn_conv_w[0], gdn_a_log=m_gdn_a_log,
             gdn_dt_bias=m_gdn_dt_bias, gdn_norm_w=m_gdn_norm_w, gdn_w_out=m_gdn_w_out[0], ssd_w_in=m_ssd_w_in[0],
             ssd_conv_w=m_ssd_conv_w[0], ssd_conv_b=m_ssd_conv_b, ssd_dt_bias=m_ssd_dt_bias, ssd_a_log=m_ssd_a_log,
             ssd_d=m_ssd_d, ssd_norm_w=m_ssd_norm_w, ssd_w_out=m_ssd_w_out[0], final_norm_w=m_final_norm_w.reshape(1, -1))
    v = dict(norm_w=v_norm_w, gdn_w_in=v_gdn_w_in[0], gdn_conv_w=v_gdn_conv_w[0], gdn_a_log=v_gdn_a_log,
             gdn_dt_bias=v_gdn_dt_bias, gdn_norm_w=v_gdn_norm_w, gdn_w_out=v_gdn_w_out[0], ssd_w_in=v_ssd_w_in[0],
             ssd_conv_w=v_ssd_conv_w[0], ssd_conv_b=v_ssd_conv_b, ssd_dt_bias=v_ssd_dt_bias, ssd_a_log=v_ssd_a_log,
             ssd_d=v_ssd_d, ssd_norm_w=v_ssd_norm_w, ssd_w_out=v_ssd_w_out[0], final_norm_w=v_final_norm_w.reshape(1, -1))
    out_shapes = {n: a.shape for n, a in zip(
        WEIGHTS, [norm_w, gdn_w_in, gdn_conv_w, gdn_a_log, gdn_dt_bias, gdn_norm_w, gdn_w_out, ssd_w_in, ssd_conv_w,
                  ssd_conv_b, ssd_dt_bias, ssd_a_log, ssd_d, ssd_norm_w, ssd_w_out, final_norm_w])}

    small_shapes = [w[n].shape for n in SMALL_SHARDED]
    first = _gather_two_level([_mx(w["gdn_w_in"]), _pack([w[n] for n in SMALL_SHARDED])], "gather_first")
    full = dict(w)
    full["gdn_w_in"] = _shards_to_cols(first[0])
    small_all = [_unpack(first[1][s, 0], small_shapes) for s in range(N_DEV)]
    for idx, n in enumerate(SMALL_SHARDED):
        full[n] = jnp.concatenate([small_all[s][idx] for s in range(N_DEV)], axis=-1)
    late = ["gdn_w_out", "ssd_w_in", "ssd_w_out"]

    def assemble(gathered):
        return {"gdn_w_out": gathered[0].reshape(-1, D_MODEL), "ssd_w_in": _shards_to_cols(gathered[1]),
                "ssd_w_out": gathered[2].reshape(-1, D_MODEL)}

    def early_grads(d_ssd_w_in, d_ssd_w_out):
        return ([_cols_to_shards(d_ssd_w_in).astype(GRAD_WIRE_DTYPE),
                 d_ssd_w_out.reshape(N_DEV, -1, D_MODEL).astype(GRAD_WIRE_DTYPE)], [False] * 2)

    def late_grads(g):
        send_small = jnp.concatenate(
            [_cols_to_shards(g[n]).reshape(N_DEV, -1) for n in SMALL_SHARDED], axis=1)[:, None, :]
        return ([_cols_to_shards(g["gdn_w_in"]).astype(GRAD_WIRE_DTYPE),
                 g["gdn_w_out"].reshape(N_DEV, -1, D_MODEL).astype(GRAD_WIRE_DTYPE), send_small], [False] * 3)

    loss, dx, grads, ssd_recv, gdn_recv = _local_step(
        x[0], loss_target[0], full, (([_mx(w[n]) for n in late], [True] * 3), assemble), early_grads, late_grads)

    rep_shapes = [w[n].shape for n in REPLICATED]
    tail = jnp.zeros((PAD_W - 1,), F32)
    recv_rep = _exchange([_pack([grads[n] for n in REPLICATED] + [loss, tail])], [True], "exchange_grads")[0]

    res = {}
    for n, parts in zip(["gdn_w_in", "gdn_w_out", "ssd_w_in", "ssd_w_out"], list(gdn_recv[:2]) + list(ssd_recv)):
        res[n] = _adamw(parts, w[n], m[n], v[n], "adamw_" + n)
    small_res = _adamw(gdn_recv[2], *[_pack([t[n] for n in SMALL_SHARDED]) for t in (w, m, v)], "adamw_small")
    rep_res = _adamw(recv_rep, *[_pack([t[n] for n in REPLICATED] + [jnp.zeros((PAD_W,), F32)]) for t in (w, m, v)],
                     "adamw_replicated")
    for k4 in range(4):
        for n, a in zip(SMALL_SHARDED, _unpack(small_res[k4][0], small_shapes)):
            res.setdefault(n, [None] * 4)[k4] = a
        for n, a in zip(REPLICATED, _unpack(rep_res[k4][0], rep_shapes)):
            res.setdefault(n, [None] * 4)[k4] = a

    loss = rep_res[0][0, sum(a.size for a in (w[n] for n in REPLICATED))]
    outs = [loss, dx[None]]
    for k4 in range(4):
        outs += [res[n][k4].reshape(out_shapes[n]) for n in WEIGHTS]
    return tuple(outs)
```

```python
import jax
import jax.numpy as jnp
from jax import lax
from jax.experimental import pallas as pl
from jax.experimental.pallas import tpu as pltpu

F32 = jnp.float32
MXU_DTYPE = jnp.bfloat16
GRAD_WIRE_DTYPE = jnp.bfloat16
HI = lax.Precision.HIGHEST
EPS = 1e-6
VMEM_LIMIT_BYTES = 56 * 1024 * 1024
N_DEV = 8
MESH = pl.DeviceIdType.MESH

D_MODEL = 1024
CONV_K = 4
GDN_HV = 16
GDN_DK = 128
GDN_CHUNK = 64
SSD_H = 32
SSD_P = 64
SSD_N = 128
SSD_G = 8
SSD_R = SSD_H // SSD_G
SSD_CHUNK = 128
D_INNER = 2048
PAD_W = 128

ADAM_LR = 0.001
ADAM_B1 = 0.9
ADAM_B2 = 0.999
ADAM_EPS = 1e-08
ADAM_WD = 0.01
ADAM_STEP = 10


def _params(*sem):
    return pltpu.CompilerParams(dimension_semantics=sem, vmem_limit_bytes=VMEM_LIMIT_BYTES)


def _mx(a):
    return a.astype(MXU_DTYPE)


def _dot(a, b):
    return jnp.dot(_mx(a), _mx(b), preferred_element_type=F32)


def _dot_nt(a, b):
    return lax.dot_general(_mx(a), _mx(b), (((1,), (1,)), ((), ())), preferred_element_type=F32)


def _dot_tn(a, b):
    return lax.dot_general(_mx(a), _mx(b), (((0,), (0,)), ((), ())), preferred_element_type=F32)


def _dot_hi(a, b):
    return jnp.dot(a, b, precision=HI, preferred_element_type=F32)


def _sigmoid(x):
    return 0.5 * jnp.tanh(0.5 * x) + 0.5


def _silu(x):
    return x * _sigmoid(x)


def _dsilu(x):
    s = _sigmoid(x)
    return s * (1.0 + x * (1.0 - s))


def _softplus(x):
    return jnp.maximum(x, 0.0) + jnp.log1p(jnp.exp(-jnp.abs(x)))


def _col(r, eye):
    return jnp.sum(jnp.where(eye, r, 0.0), axis=1, keepdims=True)


def _row(c, eye):
    return jnp.sum(jnp.where(eye, c, 0.0), axis=0, keepdims=True)


def _col_bcast(r, n):
    return jnp.broadcast_to(r, (n, n)).T


def _masks(n):
    r = lax.broadcasted_iota(jnp.int32, (n, n), 0)
    c = lax.broadcasted_iota(jnp.int32, (n, n), 1)
    return r >= c, r > c, r == c, r, c


def _with_exchange(comm):
    arrs, bcast = comm if comm else ([], [])
    nc = len(arrs)
    anyspec = pl.BlockSpec(memory_space=pl.ANY)

    def wrap(compute, n_in, n_out):
        def body(*refs):
            cin, cout = refs[n_in:n_in + nc], refs[n_in + nc + n_out:n_in + 2 * nc + n_out]
            sems = refs[n_in + 2 * nc + n_out:n_in + 2 * nc + n_out + 3]
            rest = refs[:n_in] + refs[n_in + nc:n_in + nc + n_out] + refs[n_in + 2 * nc + n_out + (3 if nc else 0):]
            if nc:
                @pl.when(pl.program_id(0) == 0)
                def _():
                    _Exchange(cin, cout, bcast, *sems).begin()
            compute(*rest)
            if nc:
                @pl.when(pl.program_id(0) == pl.num_programs(0) - 1)
                def _():
                    _Exchange(cin, cout, bcast, *sems).finish()
        return body

    return dict(arrs=list(arrs), nc=nc, wrap=wrap, in_specs=[anyspec] * nc, out_specs=[anyspec] * nc,
                out_shape=_exchange_out_shapes(arrs, bcast), scratch=_exchange_semaphores(nc) if nc else [])


INPROJ_CONV_STRIP = 256
INPROJ_COL_BLOCK = 512


def _norm_inproj(x, nw, wparts, convs, name, comm=None):
    T = x.shape[0]
    tt = min(T, 256)
    n = len(wparts)
    ck = [k for k in range(n) if convs[k] is not None]
    nconv = len(ck)
    widths = [w.shape[1] for w in wparts]
    conv_blocks = [(k, c0) for k in ck for c0 in range(0, widths[k], INPROJ_COL_BLOCK)]
    ex = _with_exchange(comm)

    def compute(x_ref, nw_ref, *refs):
        w_refs, cw_refs = refs[:n], refs[n:n + 2 * nconv]
        h_ref, o_refs = refs[n + 2 * nconv], refs[n + 2 * nconv + 1:2 * n + 2 * nconv + 1]
        post_refs = refs[2 * n + 2 * nconv + 1:2 * n + 3 * nconv + 1]
        cpre_refs = refs[2 * n + 3 * nconv + 1:2 * n + 4 * nconv + 1]
        p_refs = refs[2 * n + 4 * nconv + 1:]
        xv = x_ref[...]
        r = lax.rsqrt(jnp.mean(xv * xv, axis=-1, keepdims=True) + EPS)
        h = _mx(xv * r * nw_ref[...])
        h_ref[...] = h
        p_of = dict(zip(conv_blocks, p_refs))
        for P in p_refs:
            @pl.when(pl.program_id(0) == 0)
            def _():
                P[0:HALO, :] = jnp.zeros((HALO, P.shape[1]), F32)

        def conv_block(k, c0, cw):
            m = ck.index(k)
            _, _, l2, scale = convs[k]
            cw_ref, cb_ref, out_ref, P = cw_refs[2 * m], cw_refs[2 * m + 1], post_refs[m], p_of[(k, c0)]
            cs = slice(c0, c0 + cw)
            for r0 in range(0, tt, INPROJ_CONV_STRIP):
                rs = slice(r0, r0 + INPROJ_CONV_STRIP)
                acc = cb_ref[:, cs] + cw_ref[0:1, cs] * P[pl.ds(HALO - 3 + r0, INPROJ_CONV_STRIP), :]
                for j in range(1, CONV_K):
                    acc = acc + cw_ref[j:j + 1, cs] * P[pl.ds(HALO - 3 + j + r0, INPROJ_CONV_STRIP), :]
                cpre_refs[m][rs, cs] = acc
                s = _silu(acc)
                if l2:
                    sls = [slice(g0, g0 + GDN_DK) for g0 in range(0, cw, GDN_DK)]
                    rr = [lax.rsqrt(jnp.sum(s[:, sl] * s[:, sl], axis=-1, keepdims=True) + EPS) for sl in sls]
                    for sl, rg in zip(sls, rr):
                        out_ref[rs, c0 + sl.start:c0 + sl.stop] = s[:, sl] * rg * scale
                else:
                    out_ref[rs, cs] = s
            P[0:HALO, :] = P[tt:tt + HALO, :]

        pending = None
        for k in range(n):
            for c0 in range(0, widths[k], INPROJ_COL_BLOCK):
                cw = min(INPROJ_COL_BLOCK, widths[k] - c0)
                pre = jnp.dot(h, w_refs[k][:, c0:c0 + cw], preferred_element_type=F32)
                o_refs[k][:, c0:c0 + cw] = pre
                if convs[k] is not None:
                    p_of[(k, c0)][HALO:HALO + tt, :] = pre
                if pending is not None:
                    conv_block(*pending)
                pending = (k, c0, cw) if convs[k] is not None else None
        if pending is not None:
            conv_block(*pending)

    row = lambda width: pl.BlockSpec((tt, width), lambda i: (i, 0))
    full = lambda a: pl.BlockSpec(a.shape, lambda i: (0, 0))
    once = lambda a: pl.BlockSpec(a.shape, lambda i: (0, 0), pipeline_mode=pl.Buffered(1))
    conv_args = [a for k in ck for a in convs[k][:2]]
    outs = pl.pallas_call(
        ex["wrap"](compute, 2 + n + 2 * nconv, 1 + n + 2 * nconv), grid=(T // tt,),
        in_specs=[row(D_MODEL), full(nw)] + [once(w) for w in wparts] + [full(a) for a in conv_args] + ex["in_specs"],
        out_specs=[row(D_MODEL)] + [row(wd) for wd in widths] + [row(widths[k]) for k in ck + ck] + ex["out_specs"],
        out_shape=[jax.ShapeDtypeStruct((T, D_MODEL), MXU_DTYPE)]
        + [jax.ShapeDtypeStruct((T, wd), F32) for wd in widths]
        + [jax.ShapeDtypeStruct((T, widths[k]), F32) for k in ck + ck] + ex["out_shape"],
        scratch_shapes=ex["scratch"] + [pltpu.VMEM((HALO + tt, min(INPROJ_COL_BLOCK, widths[k] - c0)), F32)
                                        for k, c0 in conv_blocks],
        compiler_params=_params("arbitrary"), name=name,
    )(x, nw, *wparts, *conv_args, *ex["arrs"])
    outs = list(outs)
    res = (outs[0], outs[1:1 + n], outs[1 + n:1 + n + nconv], outs[1 + n + nconv:1 + n + 2 * nconv])
    return res + (outs[1 + n + 2 * nconv:],) if comm else res


def _inproj_bwd(x, nw, dparts, wparts, dres, name, comm=None):
    T = x.shape[0]
    tt = min(T, 512)
    n = len(wparts)
    ex = _with_exchange(comm)

    def body(x_ref, nw_ref, dres_ref, *refs):
        d_refs, w_refs, dx_ref, dnw_ref = refs[:n], refs[n:2 * n], refs[2 * n], refs[2 * n + 1]

        @pl.when(pl.program_id(0) == 0)
        def _():
            dnw_ref[...] = jnp.zeros_like(dnw_ref)

        dh = _dot_nt(d_refs[0][...], w_refs[0][...])
        for d_ref, w_ref in zip(d_refs[1:], w_refs[1:]):
            dh = dh + _dot_nt(d_ref[...], w_ref[...])
        xv = x_ref[...]
        r = lax.rsqrt(jnp.mean(xv * xv, axis=-1, keepdims=True) + EPS)
        xh = xv * r
        dnw_ref[...] += jnp.sum(dh * xh, axis=0, keepdims=True)
        dxn = dh * nw_ref[...]
        dx_ref[...] = dres_ref[...] + r * (dxn - xh * jnp.mean(dxn * xh, axis=-1, keepdims=True))

    row = lambda width: pl.BlockSpec((tt, width), lambda i: (i, 0))
    full = lambda a: pl.BlockSpec(a.shape, lambda i: (0, 0))
    outs = pl.pallas_call(
        ex["wrap"](body, 3 + 2 * n, 2), grid=(T // tt,),
        in_specs=[row(D_MODEL), full(nw), row(D_MODEL)] + [row(d.shape[1]) for d in dparts]
        + [pl.BlockSpec(w.shape, lambda i: (0, 0), pipeline_mode=pl.Buffered(1)) for w in wparts] + ex["in_specs"],
        out_specs=[row(D_MODEL), pl.BlockSpec((1, D_MODEL), lambda i: (0, 0))] + ex["out_specs"],
        out_shape=[jax.ShapeDtypeStruct((T, D_MODEL), F32), jax.ShapeDtypeStruct((1, D_MODEL), F32)]
        + ex["out_shape"],
        scratch_shapes=ex["scratch"],
        compiler_params=_params("arbitrary"), name=name,
    )(x, nw, dres, *dparts, *wparts, *ex["arrs"])
    outs = list(outs)
    return outs[:2] + ([outs[2:]] if comm else [])


def _matmul_tn(a, b, name):
    T, K = a.shape
    N = b.shape[1]
    tt = min(T, 2048)
    tn = min(N, 1024)

    def body(a_ref, b_ref, o_ref):
        @pl.when(pl.program_id(1) == 0)
        def _():
            o_ref[...] = jnp.zeros_like(o_ref)

        o_ref[...] += _dot_tn(a_ref[...], b_ref[...])

    return pl.pallas_call(
        body, grid=(N // tn, T // tt),
        in_specs=[pl.BlockSpec((tt, K), lambda n, t: (t, 0)), pl.BlockSpec((tt, tn), lambda n, t: (t, n))],
        out_specs=pl.BlockSpec((K, tn), lambda n, t: (0, n)),
        out_shape=jax.ShapeDtypeStruct((K, N), F32),
        compiler_params=_params("parallel", "arbitrary"), name=name,
    )(a, b)


OUT_COL_BLOCK = 512


def _out_fwd(o, z, w, wout, xres, gs, gate_first, name, final=None):
    T = o.shape[0]
    tt = min(T, 512)
    nT = T // tt
    wide = w.shape[1] == D_INNER

    def body(o_ref, z_ref, w_ref, wout_ref, x_ref, *refs):
        if final is None:
            out_ref, yn = refs
        else:
            fw_ref, t_ref, dx_ref, dfw_ref, loss_ref, yn, lacc = refs
        acc = x_ref[...]
        pending = None
        for b0 in range(0, D_INNER, OUT_COL_BLOCK):
            for g0 in range(b0, b0 + OUT_COL_BLOCK, gs):
                sl = slice(g0, g0 + gs)
                og, zg = o_ref[:, sl], z_ref[:, sl]
                wg = w_ref[:, sl] if wide else w_ref[...]
                if gate_first:
                    u = og * _silu(zg)
                    r = lax.rsqrt(jnp.mean(u * u, axis=-1, keepdims=True) + EPS)
                    yn[:, sl] = _mx(u * r * wg)
                else:
                    r = lax.rsqrt(jnp.mean(og * og, axis=-1, keepdims=True) + EPS)
                    yn[:, sl] = _mx(og * r * wg * _silu(zg))
            if pending is not None:
                acc = acc + jnp.dot(yn[:, pending], wout_ref[pending, :], preferred_element_type=F32)
            pending = slice(b0, b0 + OUT_COL_BLOCK)
        xv = acc + jnp.dot(yn[:, pending], wout_ref[pending, :], preferred_element_type=F32)
        if final is None:
            out_ref[...] = xv
            return
        i = pl.program_id(0)

        @pl.when(i == 0)
        def _():
            dfw_ref[...] = jnp.zeros_like(dfw_ref)
            lacc[...] = jnp.zeros_like(lacc)

        r = lax.rsqrt(jnp.mean(xv * xv, axis=-1, keepdims=True) + EPS)
        xh = xv * r
        err = xh * fw_ref[...] - t_ref[...]
        lacc[...] += jnp.sum(err * err, axis=0, keepdims=True)
        dout = err * (1.0 / D_MODEL)
        dfw_ref[...] += jnp.sum(dout * xh, axis=0, keepdims=True)
        dxn = dout * fw_ref[...]
        dx_ref[...] = r * (dxn - xh * jnp.mean(dxn * xh, axis=-1, keepdims=True))

        @pl.when(i == nT - 1)
        def _():
            loss_ref[...] = (0.5 / D_MODEL) * jnp.sum(lacc[...], axis=1, keepdims=True)

    row = lambda width: pl.BlockSpec((tt, width), lambda i: (i, 0))
    full = lambda a: pl.BlockSpec(a.shape, lambda i: (0, 0))
    if final is None:
        return pl.pallas_call(
            body, grid=(nT,),
            in_specs=[row(D_INNER), row(D_INNER), full(w), full(wout), row(D_MODEL)],
            out_specs=row(D_MODEL),
            out_shape=jax.ShapeDtypeStruct((T, D_MODEL), F32),
            scratch_shapes=[pltpu.VMEM((tt, D_INNER), MXU_DTYPE)],
            compiler_params=_params("parallel"), name=name,
        )(o, z, w, wout, xres)
    vec = pl.BlockSpec((1, D_MODEL), lambda i: (0, 0))
    return pl.pallas_call(
        body, grid=(nT,),
        in_specs=[row(D_INNER), row(D_INNER), full(w), full(wout), row(D_MODEL), vec, row(D_MODEL)],
        out_specs=[row(D_MODEL), vec, pl.BlockSpec((1, 1), lambda i: (0, 0))],
        out_shape=[jax.ShapeDtypeStruct((T, D_MODEL), F32), jax.ShapeDtypeStruct((1, D_MODEL), F32),
                   jax.ShapeDtypeStruct((1, 1), F32)],
        scratch_shapes=[pltpu.VMEM((tt, D_INNER), MXU_DTYPE), pltpu.VMEM((1, D_MODEL), F32)],
        compiler_params=_params("arbitrary"), name=name,
    )(o, z, w, wout, xres, *final)


def _out_bwd(dx, o, z, w, wout, gs, gate_first, name, comm=None):
    T = o.shape[0]
    tt = min(T, 256)
    wide = w.shape[1] == D_INNER

    def body(dx_ref, o_ref, z_ref, w_ref, wout_ref, do_ref, dz_ref, dw_ref, yn_ref):
        @pl.when(pl.program_id(0) == 0)
        def _():
            dw_ref[...] = jnp.zeros_like(dw_ref)

        dxb = _mx(dx_ref[...])
        blocks = list(range(0, D_INNER, OUT_COL_BLOCK))
        dyn_b = {b0: _dot_nt(dxb, wout_ref[b0:b0 + OUT_COL_BLOCK, :]) for b0 in blocks[:1]}
        dw_acc = jnp.zeros((1, gs), F32)
        for g0 in range(0, D_INNER, gs):
            b0 = g0 - g0 % OUT_COL_BLOCK
            if g0 == b0 and b0 + OUT_COL_BLOCK < D_INNER:
                nb = b0 + OUT_COL_BLOCK
                dyn_b[nb] = _dot_nt(dxb, wout_ref[nb:nb + OUT_COL_BLOCK, :])
            sl = slice(g0, g0 + gs)
            og, zg, dg = o_ref[:, sl], z_ref[:, sl], dyn_b[b0][:, g0 - b0:g0 - b0 + gs]
            wg = w_ref[:, sl] if wide else w_ref[...]
            sz = _silu(zg)
            if gate_first:
                u = og * sz
                r = lax.rsqrt(jnp.mean(u * u, axis=-1, keepdims=True) + EPS)
                uh = u * r
                yn_ref[:, sl] = _mx(uh * wg)
                dw_g = jnp.sum(dg * uh, axis=0, keepdims=True)
                duh = dg * wg
                du = r * (duh - uh * jnp.mean(duh * uh, axis=-1, keepdims=True))
                do_ref[:, sl] = du * sz
                dz_ref[:, sl] = _mx(du * og * _dsilu(zg))
            else:
                r = lax.rsqrt(jnp.mean(og * og, axis=-1, keepdims=True) + EPS)
                oh = og * r
                yn_ref[:, sl] = _mx(oh * wg * sz)
                dw_g = jnp.sum(dg * oh * sz, axis=0, keepdims=True)
                doh = dg * wg * sz
                dz_ref[:, sl] = _mx(dg * oh * wg * _dsilu(zg))
                do_ref[:, sl] = r * (doh - oh * jnp.mean(doh * oh, axis=-1, keepdims=True))
            if wide:
                dw_ref[:, sl] += dw_g
            else:
                dw_acc = dw_acc + dw_g
        if not wide:
            dw_ref[...] += dw_acc

    row = lambda width: pl.BlockSpec((tt, width), lambda i: (i, 0))
    full = lambda a: pl.BlockSpec(a.shape, lambda i: (0, 0))
    ex = _with_exchange(comm)
    outs = pl.pallas_call(
        ex["wrap"](body, 5, 4), grid=(T // tt,),
        in_specs=[row(D_MODEL), row(D_INNER), row(D_INNER), full(w), full(wout)] + ex["in_specs"],
        out_specs=[row(D_INNER), row(D_INNER), full(w), row(D_INNER)] + ex["out_specs"],
        out_shape=[jax.ShapeDtypeStruct((T, D_INNER), F32), jax.ShapeDtypeStruct((T, D_INNER), MXU_DTYPE),
                   jax.ShapeDtypeStruct(w.shape, F32), jax.ShapeDtypeStruct((T, D_INNER), MXU_DTYPE)]
        + ex["out_shape"],
        scratch_shapes=ex["scratch"],
        compiler_params=_params("arbitrary"), name=name,
    )(dx, o, z, w, wout, *ex["arrs"])
    outs = list(outs)
    return outs[:4] + ([outs[4:]] if comm else [])


HALO = 8
CONV_STRIP = 32


def _conv_bwd(pre, cpre_all, w, dpost, l2, scale, name):
    T, C = pre.shape
    tt = min(T, 1024)
    tc = min(C, 1024 if l2 else 512)
    strip = 2 * CONV_STRIP if l2 else CONV_STRIP
    nT = T // tt
    ext = tt + HALO

    def body(pre_ref, cp_ref, cn_ref, dpost_ref, dn_ref, w_ref, dpre_ref, dw_ref, db_ref, Q):
        i = pl.program_id(1)

        @pl.when(i == 0)
        def _():
            dw_ref[...] = jnp.zeros_like(dw_ref)
            db_ref[...] = jnp.zeros_like(db_ref)

        wj = [w_ref[j:j + 1, :] for j in range(CONV_K)]
        keep_next = jnp.where(i < nT - 1, 1.0, 0.0)
        fold = lambda a: jnp.sum(a.reshape(strip // 8, 8, tc), axis=0)
        dw_acc = [jnp.zeros((8, tc), F32) for _ in range(CONV_K)]
        db_acc = jnp.zeros((8, tc), F32)
        for r0 in list(range(0, tt, strip)) + [tt]:
            n = strip if r0 < tt else HALO
            cpre = cp_ref[r0:r0 + n, :] if r0 < tt else cn_ref[...]
            dy = dpost_ref[r0:r0 + n, :] if r0 < tt else dn_ref[...] * keep_next
            sg = _sigmoid(cpre)
            ds_c = sg * (1.0 + cpre * (1.0 - sg))
            if l2:
                s = cpre * sg
                sls = [slice(g0, g0 + GDN_DK) for g0 in range(0, tc, GDN_DK)]
                rr = [lax.rsqrt(jnp.sum(s[:, sl] * s[:, sl], axis=-1, keepdims=True) + EPS) for sl in sls]
                yh = [s[:, sl] * r for sl, r in zip(sls, rr)]
                pr = [jnp.sum(dy[:, sl] * y, axis=-1, keepdims=True) for sl, y in zip(sls, yh)]
                for sl, r, y, p in zip(sls, rr, yh, pr):
                    Q[r0:r0 + n, sl] = (scale * r) * (dy[:, sl] - y * p) * ds_c[:, sl]
                dyc = Q[r0:r0 + n, :]
            else:
                dyc = dy * ds_c
                Q[r0:r0 + n, :] = dyc
            if r0 < tt:
                db_acc = db_acc + fold(dyc)
        for r0 in range(0, tt, strip):
            xs = pre_ref[r0:r0 + strip, :]
            dpre = jnp.zeros((strip, tc), F32)
            for j in range(CONV_K):
                qj = Q[pl.ds(3 - j + r0, strip), :]
                dpre = dpre + wj[j] * qj
                dw_acc[j] = dw_acc[j] + fold(qj * xs)
            dpre_ref[r0:r0 + strip, :] = _mx(dpre)
        for j in range(CONV_K):
            dw_ref[j:j + 1, :] += jnp.sum(dw_acc[j], axis=0, keepdims=True)
        db_ref[...] += jnp.sum(db_acc, axis=0, keepdims=True)

    tile = pl.BlockSpec((tt, tc), lambda j, i: (i, j))
    nxt = pl.BlockSpec((HALO, tc), lambda j, i: (jnp.minimum((i + 1) * (tt // HALO), T // HALO - 1), j))
    return pl.pallas_call(
        body, grid=(C // tc, nT),
        in_specs=[tile, tile, nxt, tile, nxt, pl.BlockSpec((CONV_K, tc), lambda j, i: (0, j))],
        out_specs=[tile, pl.BlockSpec((CONV_K, tc), lambda j, i: (0, j)), pl.BlockSpec((1, tc), lambda j, i: (0, j))],
        out_shape=[jax.ShapeDtypeStruct((T, C), MXU_DTYPE), jax.ShapeDtypeStruct((CONV_K, C), F32),
                   jax.ShapeDtypeStruct((1, C), F32)],
        scratch_shapes=[pltpu.VMEM((ext, tc), F32)],
        compiler_params=_params("parallel", "arbitrary"), name=name,
    )(pre, cpre_all, cpre_all, dpost, dpost, w)


GDN_LOCKSTEP_CHUNKS = 16
GDN_LOCKSTEP_CHUNKS_BWD = 16
GDN_SCAN_HEADS = 16


def _inv_unit_lower_many(nms, eye, n):
    xs = [jnp.where(eye, 1.0, 0.0) - nm for nm in nms]
    ps = list(nms)
    k = 2
    while k < n:
        ps = [_dot(p, p) for p in ps]
        xs = [x + _dot(x, p) for x, p in zip(xs, ps)]
        k *= 2
    return xs


def _gdn_prep(q, k, v, araw, braw, alog, dtb, name):
    T = q.shape[0]
    C = GDN_CHUNK
    tt = min(T, 1024)
    cpt, nC = tt // C, T // C
    grp = min(cpt, GDN_LOCKSTEP_CHUNKS)

    def body(alog_ref, dtb_ref, q_ref, k_ref, v_ref, a_ref, b_ref,
             u_ref, w_ref, pm_ref, ti_ref, g_ref, beta_ref, gc_ref, qd_ref, kd_ref):
        j = pl.program_id(0)
        tri, strict, eye, r_i, c_i = _masks(C)
        upper = jnp.where(r_i <= c_i, 1.0, 0.0)
        gcs, bts = [], []
        for hh in range(2):
            h = 2 * j + hh
            g = -jnp.exp(alog_ref[h]) * _softplus(a_ref[hh] + dtb_ref[h])
            bt = _sigmoid(b_ref[hh])
            gc = _dot_hi(g, upper)
            g_ref[hh], beta_ref[hh], gc_ref[hh] = g, bt, gc
            gcs.append(gc)
            bts.append(bt)
        for c0 in range(0, cpt, grp):
            cs = list(range(c0, c0 + grp))
            inst = [(c, hh) for c in cs for hh in range(2)]
            rows = {c: slice(c * C, (c + 1) * C) for c in cs}
            qc = {c: q_ref[rows[c], :] for c in cs}
            kc = {c: k_ref[rows[c], :] for c in cs}
            kk = {c: _dot_nt(kc[c], kc[c]) for c in cs}
            qk = {c: _dot_nt(qc[c], kc[c]) for c in cs}
            gcr = [gcs[hh][c:c + 1, :] for c, hh in inst]
            gcc = [_col(r, eye) for r in gcr]
            bc = [_col(bts[hh][c:c + 1, :], eye) for c, hh in inst]
            lm = [jnp.exp(jnp.where(tri, cc - r, -1e30)) for cc, r in zip(gcc, gcr)]
            nm = [jnp.where(strict, kk[c] * b * l, 0.0) for (c, hh), b, l in zip(inst, bc, lm)]
            tinv = _inv_unit_lower_many(nm, eye, C)
            e_c = [jnp.exp(cc) for cc in gcc]
            rhs = [jnp.concatenate([v_ref[rows[c], hh * GDN_DK:(hh + 1) * GDN_DK] * b, kc[c] * (b * e)], axis=1)
                   for (c, hh), b, e in zip(inst, bc, e_c)]
            sol = [_dot(t, r) for t, r in zip(tinv, rhs)]
            for (c, hh), s, t, l, e, cc, r in zip(inst, sol, tinv, lm, e_c, gcc, gcr):
                hs = slice(hh * GDN_DK, (hh + 1) * GDN_DK)
                u_ref[rows[c], hs] = s[:, :GDN_DK]
                w_ref[rows[c], hs] = _mx(s[:, GDN_DK:])
                pm_ref[hh, c] = _mx(jnp.where(tri, qk[c] * l, 0.0))
                ti_ref[hh, c] = _mx(t)
                qd_ref[rows[c], hs] = _mx(qc[c] * e)
                kd_ref[rows[c], hs] = _mx(kc[c] * jnp.exp(r[:, C - 1:C] - cc))

    smem = pl.BlockSpec(memory_space=pltpu.SMEM)
    rows_spec = pl.BlockSpec((2, cpt, C), lambda j, i: (j, i, 0))
    qk_spec = pl.BlockSpec((tt, GDN_DK), lambda j, i: (i, j))
    v_spec = pl.BlockSpec((tt, 2 * GDN_DK), lambda j, i: (i, j))
    cc_spec = pl.BlockSpec((2, cpt, C, C), lambda j, i: (j, i, 0, 0))
    rows_shape = jax.ShapeDtypeStruct((GDN_HV, nC, C), F32)
    cc_shape = jax.ShapeDtypeStruct((GDN_HV, nC, C, C), MXU_DTYPE)
    return pl.pallas_call(
        body, grid=(GDN_HV // 2, T // tt),
        in_specs=[smem, smem, qk_spec, qk_spec, v_spec, rows_spec, rows_spec],
        out_specs=[v_spec, v_spec, cc_spec, cc_spec, rows_spec, rows_spec, rows_spec, v_spec, v_spec],
        out_shape=[jax.ShapeDtypeStruct((T, D_INNER), F32), jax.ShapeDtypeStruct((T, D_INNER), MXU_DTYPE),
                   cc_shape, cc_shape, rows_shape, rows_shape, rows_shape,
                   jax.ShapeDtypeStruct((T, D_INNER), MXU_DTYPE), jax.ShapeDtypeStruct((T, D_INNER), MXU_DTYPE)],
        compiler_params=_params("parallel", "parallel"), name=name,
    )(alog, dtb, q, k, v, araw, braw)


def _gdn_state_fwd(q, k, u, w, pm, gc, name):
    T = q.shape[0]
    C = GDN_CHUNK
    HG = GDN_SCAN_HEADS
    tt = min(T, 512)
    cpt, nC = tt // C, T // C

    def body(q_ref, k_ref, u_ref, w_ref, pm_ref, gc_ref, o_ref, vn_ref, sall_ref, S):
        @pl.when(pl.program_id(1) == 0)
        def _():
            S[...] = jnp.zeros_like(S)

        heads = list(range(HG))

        def chunk(c, carry):
            rows = pl.ds(pl.multiple_of(c * C, C), C)
            hs = [slice(h * GDN_DK, (h + 1) * GDN_DK) for h in heads]
            gl = [jnp.exp(gc_ref[h, pl.ds(c, 1), C - 1:C]) for h in heads]
            sv = [S[h] for h in heads]
            for h in heads:
                sall_ref[h, c] = _mx(sv[h])
            ws = [_dot(w_ref[rows, hs[h]], sv[h]) for h in heads]
            qsv = [_dot(q_ref[rows, hs[h]], sv[h]) for h in heads]
            vn = [u_ref[rows, hs[h]] - ws[h] for h in heads]
            pv = [_dot(pm_ref[h, c], vn[h]) for h in heads]
            kv = [_dot_tn(k_ref[rows, hs[h]], vn[h]) for h in heads]
            for h in heads:
                vn_ref[rows, hs[h]] = _mx(vn[h])
                o_ref[rows, hs[h]] = qsv[h] + pv[h]
                S[h] = sv[h] * gl[h] + kv[h]
            return carry

        lax.fori_loop(0, cpt, chunk, 0)

    v_spec = pl.BlockSpec((tt, HG * GDN_DK), lambda g, i: (i, g))
    return pl.pallas_call(
        body, grid=(GDN_HV // HG, T // tt),
        in_specs=[v_spec, v_spec, v_spec, v_spec,
                  pl.BlockSpec((HG, cpt, C, C), lambda g, i: (g, i, 0, 0)),
                  pl.BlockSpec((HG, cpt, C), lambda g, i: (g, i, 0))],
        out_specs=[v_spec, v_spec, pl.BlockSpec((HG, cpt, GDN_DK, GDN_DK), lambda g, i: (g, i, 0, 0))],
        out_shape=[jax.ShapeDtypeStruct((T, D_INNER), F32), jax.ShapeDtypeStruct((T, D_INNER), MXU_DTYPE),
                   jax.ShapeDtypeStruct((GDN_HV, nC, GDN_DK, GDN_DK), MXU_DTYPE)],
        scratch_shapes=[pltpu.VMEM((HG, GDN_DK, GDN_DK), F32)],
        compiler_params=_params("parallel", "arbitrary"), name=name,
    )(q, k, u, w, pm, gc)


def _gdn_state_bwd(q, k, w, pm, vn, sall, gc, do, name):
    T = q.shape[0]
    C = GDN_CHUNK
    HG = GDN_SCAN_HEADS
    tt = min(T, 512)
    cpt, nC, nT = tt // C, T // C, T // tt

    def body(q_ref, k_ref, w_ref, pm_ref, vn_ref, sall_ref, gc_ref, do_ref, dvn_ref, dkd_ref, dgl_ref, dS):
        @pl.when(pl.program_id(1) == 0)
        def _():
            dS[...] = jnp.zeros_like(dS)

        heads = list(range(HG))

        def chunk(ci, carry):
            c = cpt - 1 - ci
            rows = pl.ds(pl.multiple_of(c * C, C), C)
            hs = [slice(h * GDN_DK, (h + 1) * GDN_DK) for h in heads]
            gl = [jnp.exp(gc_ref[h, pl.ds(c, 1), C - 1:C]) for h in heads]
            dsn = [dS[h] for h in heads]
            doc = [do_ref[rows, hs[h]] for h in heads]
            kds = [_dot(k_ref[rows, hs[h]], dsn[h]) for h in heads]
            pdo = [_dot_tn(pm_ref[h, c], doc[h]) for h in heads]
            dkd = [_dot_nt(vn_ref[rows, hs[h]], dsn[h]) for h in heads]
            qdo = [_dot_tn(q_ref[rows, hs[h]], doc[h]) for h in heads]
            dvn = [pdo[h] + kds[h] for h in heads]
            wdv = [_dot_tn(w_ref[rows, hs[h]], dvn[h]) for h in heads]
            for h in heads:
                dgl = jnp.sum(jnp.sum(dsn[h] * sall_ref[h, c].astype(F32), axis=0, keepdims=True), axis=1, keepdims=True)
                dgl_ref[h, pl.ds(c, 1), :] = jnp.broadcast_to(dgl, (1, C))
                dvn_ref[rows, hs[h]] = dvn[h]
                dkd_ref[rows, hs[h]] = dkd[h]
                dS[h] = dsn[h] * gl[h] + qdo[h] - wdv[h]
            return carry

        lax.fori_loop(0, cpt, chunk, 0)

    rev = lambda i: nT - 1 - i
    v_spec = pl.BlockSpec((tt, HG * GDN_DK), lambda g, i: (rev(i), g))
    rows_spec = pl.BlockSpec((HG, cpt, C), lambda g, i: (g, rev(i), 0))
    return pl.pallas_call(
        body, grid=(GDN_HV // HG, nT),
        in_specs=[v_spec, v_spec, v_spec, pl.BlockSpec((HG, cpt, C, C), lambda g, i: (g, rev(i), 0, 0)), v_spec,
                  pl.BlockSpec((HG, cpt, GDN_DK, GDN_DK), lambda g, i: (g, rev(i), 0, 0)), rows_spec, v_spec],
        out_specs=[v_spec, v_spec, rows_spec],
        out_shape=[jax.ShapeDtypeStruct((T, D_INNER), F32), jax.ShapeDtypeStruct((T, D_INNER), F32),
                   jax.ShapeDtypeStruct((GDN_HV, nC, C), F32)],
        scratch_shapes=[pltpu.VMEM((HG, GDN_DK, GDN_DK), F32)],
        compiler_params=_params("parallel", "arbitrary"), name=name,
    )(q, k, w, pm, vn, sall, gc, do)


def _gdn_local_bwd(q, k, v, gc, beta, tinv, u, w, pm, vn, sall, do, dvn, dkd, dgl, name):
    T = q.shape[0]
    C = GDN_CHUNK
    tt = min(T, 1024)
    cpt, nC = tt // C, T // C
    grp = min(cpt, GDN_LOCKSTEP_CHUNKS_BWD)

    def body(q_ref, k_ref, v_ref, gc_ref, b_ref, ti_ref, u_ref, w_ref, pm_ref, vn_ref, sall_ref, do_ref,
             dvn_ref, dkd_ref, dgl_ref, dq_ref, dk_ref, dv_ref, dg_ref, dbeta_ref, dgc_s):
        tri, strict, eye, r_i, c_i = _masks(C)
        lower = jnp.where(r_i >= c_i, 1.0, 0.0)
        lane = lax.broadcasted_iota(jnp.int32, (1, C), 1)
        rsum = lambda a: jnp.sum(a, axis=1, keepdims=True)
        for c0 in range(0, cpt, grp):
            cs = list(range(c0, c0 + grp))
            inst = [(c, hh) for c in cs for hh in range(2)]
            n = len(inst)
            rows = {c: slice(c * C, (c + 1) * C) for c in cs}
            hsl = [slice(hh * GDN_DK, (hh + 1) * GDN_DK) for c, hh in inst]
            qc = {c: q_ref[rows[c], :] for c in cs}
            kc = {c: k_ref[rows[c], :] for c in cs}
            kk = {c: _dot_nt(kc[c], kc[c]) for c in cs}
            gcr = [gc_ref[hh, c:c + 1, :] for c, hh in inst]
            gcc = [_col(r, eye) for r in gcr]
            bc = [_col(b_ref[hh, c:c + 1, :], eye) for c, hh in inst]
            lm = [jnp.exp(jnp.where(tri, cc - r, -1e30)) for cc, r in zip(gcc, gcr)]
            e_c = [jnp.exp(cc) for cc in gcc]
            el_c = [jnp.exp(r[:, C - 1:C] - cc) for cc, r in zip(gcc, gcr)]
            gl = [jnp.exp(r[:, C - 1:C]) for r in gcr]
            doc = [do_ref[rows[c], hsl[i]] for i, (c, hh) in enumerate(inst)]
            dvn = [dvn_ref[rows[c], hsl[i]] for i, (c, hh) in enumerate(inst)]
            sv = [sall_ref[hh, c] for c, hh in inst]
            aa = [_dot_nt(jnp.concatenate([_mx(doc[i]), _mx(dvn[i])], axis=0), sv[i]) for i in range(n)]
            dpm = [jnp.where(tri, _dot_nt(doc[i], vn_ref[rows[c], hsl[i]]), 0.0) for i, (c, hh) in enumerate(inst)]
            dqd = [a[:C] for a in aa]
            drhs = [_dot_tn(ti_ref[hh, c], jnp.concatenate([dvn[i], -aa[i][C:]], axis=1))
                    for i, (c, hh) in enumerate(inst)]
            sol = [jnp.concatenate([_mx(u_ref[rows[c], hsl[i]]), w_ref[rows[c], hsl[i]]], axis=1)
                   for i, (c, hh) in enumerate(inst)]
            dnm = [-jnp.where(strict, _dot_nt(drhs[i], sol[i]), 0.0) for i in range(n)]
            dkk = [dnm[i] * bc[i] * lm[i] for i in range(n)]
            dqk = [dpm[i] * lm[i] for i in range(n)]
            dq1 = [_dot(dqk[i], kc[c]) for i, (c, hh) in enumerate(inst)]
            dk1 = [_dot(dkk[i], kc[c]) for i, (c, hh) in enumerate(inst)]
            dk2 = [_dot_tn(dkk[i], kc[c]) for i, (c, hh) in enumerate(inst)]
            dk3 = [_dot_tn(dqk[i], qc[c]) for i, (c, hh) in enumerate(inst)]
            dq_acc = {c: jnp.zeros((C, GDN_DK), F32) for c in cs}
            dk_acc = {c: jnp.zeros((C, GDN_DK), F32) for c in cs}
            for i, (c, hh) in enumerate(inst):
                k_, q_, v_ = kc[c], qc[c], v_ref[rows[c], hsl[i]]
                dvb, dkbe = drhs[i][:, :GDN_DK], drhs[i][:, GDN_DK:]
                dkd = dkd_ref[rows[c], hsl[i]]
                kb = k_ * bc[i]
                dkb = dkbe * e_c[i]
                del_el = dkd * k_ * el_c[i]
                dbc = rsum(dnm[i] * kk[c] * lm[i]) + rsum(dkb * k_ + dvb * v_)
                dq_acc[c] = dq_acc[c] + dq1[i] + dqd[i] * e_c[i]
                dk_acc[c] = dk_acc[c] + dk1[i] + dk2[i] + dk3[i] + dkd * el_c[i] + dkb * bc[i]
                dv_ref[rows[c], hsl[i]] = dvb * bc[i]
                nm = jnp.where(strict, kk[c] * bc[i] * lm[i], 0.0)
                gm = dnm[i] * nm + dpm[i] * pm_ref[hh, c].astype(F32)
                dgc_col = rsum(gm) + rsum((dkbe * kb + dqd[i] * q_) * e_c[i] - del_el)
                dglast = (jnp.sum(jnp.sum(del_el, axis=0, keepdims=True), axis=1, keepdims=True)
                          + dgl_ref[hh, c:c + 1, 0:1] * gl[i])
                dgc_s[hh, c:c + 1, :] = (_row(dgc_col, eye) - jnp.sum(gm, axis=0, keepdims=True)
                                         + jnp.where(lane == C - 1, dglast, 0.0))
                dbeta_ref[hh, c:c + 1, :] = _row(dbc, eye)
            for c in cs:
                dq_ref[rows[c], :] = dq_acc[c]
                dk_ref[rows[c], :] = dk_acc[c]
        for hh in range(2):
            dg_ref[hh] = _dot_hi(dgc_s[hh], lower)

    rows_spec = pl.BlockSpec((2, cpt, C), lambda j, i: (j, i, 0))
    qk_spec = pl.BlockSpec((tt, GDN_DK), lambda j, i: (i, j))
    v_spec = pl.BlockSpec((tt, 2 * GDN_DK), lambda j, i: (i, j))
    cc_spec = pl.BlockSpec((2, cpt, C, C), lambda j, i: (j, i, 0, 0))
    rows_shape = jax.ShapeDtypeStruct((GDN_HV, nC, C), F32)
    return pl.pallas_call(
        body, grid=(GDN_HV // 2, T // tt),
        in_specs=[qk_spec, qk_spec, v_spec, rows_spec, rows_spec, cc_spec, v_spec, v_spec, cc_spec, v_spec,
                  pl.BlockSpec((2, cpt, GDN_DK, GDN_DK), lambda j, i: (j, i, 0, 0)), v_spec, v_spec, v_spec, rows_spec],
        out_specs=[qk_spec, qk_spec, v_spec, rows_spec, rows_spec],
        out_shape=[jax.ShapeDtypeStruct((T, GDN_HV // 2 * GDN_DK), F32),
                   jax.ShapeDtypeStruct((T, GDN_HV // 2 * GDN_DK), F32),
                   jax.ShapeDtypeStruct((T, D_INNER), F32), rows_shape, rows_shape],
        scratch_shapes=[pltpu.VMEM((2, cpt, C), F32)],
        compiler_params=_params("parallel", "parallel"), name=name,
    )(q, k, v, gc, beta, tinv, u, w, pm, vn, sall, do, dvn, dkd, dgl)


def _gdn_gate_bwd(araw, braw, dg, dbeta, alog, dtb, name):
    H, T = araw.shape

    def body(a_ref, b_ref, dg_ref, dbt_ref, alog_ref, dtb_ref, da_ref, db_ref, dalog_ref, ddtb_ref):
        xa = a_ref[...] + dtb_ref[...]
        ea = jnp.exp(alog_ref[...])
        dgv = dg_ref[...]
        da = -dgv * ea * _sigmoid(xa)
        da_ref[...] = da
        dalog_ref[...] = jnp.sum(-dgv * ea * _softplus(xa), axis=1, keepdims=True)
        ddtb_ref[...] = jnp.sum(da, axis=1, keepdims=True)
        bt = _sigmoid(b_ref[...])
        db_ref[...] = dbt_ref[...] * bt * (1.0 - bt)

    return pl.pallas_call(
        body,
        out_shape=[jax.ShapeDtypeStruct((H, T), F32), jax.ShapeDtypeStruct((H, T), F32),
                   jax.ShapeDtypeStruct((H, 1), F32), jax.ShapeDtypeStruct((H, 1), F32)],
        compiler_params=pltpu.CompilerParams(vmem_limit_bytes=VMEM_LIMIT_BYTES), name=name,
    )(araw, braw, dg, dbeta, alog, dtb)


SSD_LOCKSTEP_CHUNKS = 2
SSD_LOCKSTEP_CHUNKS_BWD = 1
SSD_LOCKSTEP_HEADS_BWD = 2


def _ssd_scan_fwd(xs, bm, cm, dtraw, alog, dtb, dskip, name):
    T = xs.shape[0]
    Q = SSD_CHUNK
    tt = min(T, 1024)
    cpt, nC = tt // Q, T // Q
    GW = SSD_R * SSD_P

    def body(alog_ref, dtb_ref, dsk_ref, xs_ref, b_ref, c_ref, dt_ref, y_ref, sall_ref, dto_ref, S, dt_s, acs_s):
        gi, i = pl.program_id(0), pl.program_id(1)

        @pl.when(i == 0)
        def _():
            S[...] = jnp.zeros_like(S)

        tri, _, eye, r_i, c_i = _masks(Q)
        upper = jnp.where(r_i <= c_i, 1.0, 0.0)
        for r in range(SSD_R):
            h = SSD_R * gi + r
            dt = _softplus(dt_ref[r] + dtb_ref[h])
            dto_ref[r] = dt
            dt_s[r] = dt
            acs_s[r] = _dot_hi(-jnp.exp(alog_ref[h]) * dt, upper)

        ps = [slice(r * SSD_P, (r + 1) * SSD_P) for r in range(SSD_R)]
        s_cur = [S[:, ps[r]] for r in range(SSD_R)]
        grp = min(cpt, SSD_LOCKSTEP_CHUNKS)
        for c0 in range(0, cpt, grp):
            cs = list(range(c0, c0 + grp))
            inst = [(c, r) for c in cs for r in range(SSD_R)]
            rows = {c: slice(c * Q, (c + 1) * Q) for c in cs}
            bc_ = {c: b_ref[rows[c], :] for c in cs}
            cc_ = {c: c_ref[rows[c], :] for c in cs}
            cb = {c: _dot_nt(cc_[c], bc_[c]) for c in cs}
            xr = [xs_ref[rows[c], ps[r]] for c, r in inst]
            acr = [acs_s[r, c:c + 1, :] for c, r in inst]
            acc = [_col_bcast(a, Q) for a in acr]
            dtr = [dt_s[r, c:c + 1, :] for c, r in inst]
            mm = [cb[c] * (jnp.exp(jnp.where(tri, acc[i] - acr[i], -1e30)) * dtr[i]) for i, (c, r) in enumerate(inst)]
            bct = {c: bc_[c].T for c in cs}
            st = [_dot(bct[c] * (jnp.exp(acr[i][:, Q - 1:Q] - acr[i]) * dtr[i]), xr[i]) for i, (c, r) in enumerate(inst)]
            yd = [_dot(mm[i], xr[i]) for i in range(len(inst))]
            s_prev = []
            for i, (c, r) in enumerate(inst):
                s_prev.append(s_cur[r])
                s_cur[r] = s_cur[r] * jnp.exp(acr[i][:, Q - 1:Q]) + st[i]
            yo = [_dot(cc_[c] * jnp.exp(acc[i]), s_prev[i]) for i, (c, r) in enumerate(inst)]
            for i, (c, r) in enumerate(inst):
                sall_ref[0, c, :, ps[r]] = _mx(s_prev[i])
                y_ref[rows[c], ps[r]] = yd[i] + yo[i] + dsk_ref[SSD_R * gi + r] * xr[i]
        for r in range(SSD_R):
            S[:, ps[r]] = s_cur[r]

    smem = pl.BlockSpec(memory_space=pltpu.SMEM)
    rows_spec = pl.BlockSpec((SSD_R, cpt, Q), lambda g, i: (g, i, 0))
    return pl.pallas_call(
        body, grid=(SSD_G, T // tt),
        in_specs=[smem, smem, smem,
                  pl.BlockSpec((tt, GW), lambda g, i: (i, g)), pl.BlockSpec((tt, SSD_N), lambda g, i: (i, g)),
                  pl.BlockSpec((tt, SSD_N), lambda g, i: (i, g)), rows_spec],
        out_specs=[pl.BlockSpec((tt, GW), lambda g, i: (i, g)),
                   pl.BlockSpec((1, cpt, SSD_N, GW), lambda g, i: (g, i, 0, 0)), rows_spec],
        out_shape=[jax.ShapeDtypeStruct((T, D_INNER), F32), jax.ShapeDtypeStruct((SSD_G, nC, SSD_N, GW), MXU_DTYPE),
                   jax.ShapeDtypeStruct((SSD_H, nC, Q), F32)],
        scratch_shapes=[pltpu.VMEM((SSD_N, GW), F32), pltpu.VMEM((SSD_R, cpt, Q), F32),
                        pltpu.VMEM((SSD_R, cpt, Q), F32)],
        compiler_params=_params("parallel", "arbitrary"), name=name,
    )(alog, dtb, dskip, xs, bm, cm, dtraw)


def _ssd_scan_bwd(xs, bm, cm, dt, sall, dy, alog, dskip, name):
    T = xs.shape[0]
    Q = SSD_CHUNK
    tt = min(T, 1024)
    cpt, nC, nT = tt // Q, T // Q, T // tt
    GW = SSD_R * SSD_P

    def body(alog_ref, dsk_ref, xs_ref, b_ref, c_ref, dt_ref, sall_ref, dy_ref,
             dxs_ref, db_ref, dc_ref, da_ref, ddt_ref, dd_ref, dS, acs_s, dacs_s, ddt_s, dd_s):
        gi, i = pl.program_id(0), pl.program_id(1)

        @pl.when(i == 0)
        def _():
            dS[...] = jnp.zeros_like(dS)

        tri, _, eye, r_i, c_i = _masks(Q)
        upper = jnp.where(r_i <= c_i, 1.0, 0.0)
        lower = jnp.where(r_i >= c_i, 1.0, 0.0)
        lane = lax.broadcasted_iota(jnp.int32, (1, Q), 1)
        for r in range(SSD_R):
            acs_s[r] = _dot_hi(-jnp.exp(alog_ref[SSD_R * gi + r]) * dt_ref[r], upper)

        ps = [slice(r * SSD_P, (r + 1) * SSD_P) for r in range(SSD_R)]
        ds_cur = [dS[:, ps[r]] for r in range(SSD_R)]
        grp = min(cpt, SSD_LOCKSTEP_CHUNKS_BWD)
        csum = lambda a: jnp.sum(a, axis=0, keepdims=True)
        tsum = lambda a: jnp.sum(csum(a), axis=1, keepdims=True)
        ones8 = jnp.ones((8, SSD_P), F32)
        for c0 in range(cpt - grp, -1, -grp):
            cs = list(range(c0 + grp - 1, c0 - 1, -1))
            rows = {c: slice(c * Q, (c + 1) * Q) for c in cs}
            bc_ = {c: b_ref[rows[c], :] for c in cs}
            cc_ = {c: c_ref[rows[c], :] for c in cs}
            cb = {c: _dot_nt(cc_[c], bc_[c]) for c in cs}
            cbt = {c: _dot_nt(bc_[c], cc_[c]) for c in cs}
            bct = {c: bc_[c].T for c in cs}
            cct = {c: cc_[c].T for c in cs}
            dcb = {c: jnp.zeros((Q, Q), F32) for c in cs}
            dcbt = {c: jnp.zeros((Q, Q), F32) for c in cs}
            db_acc = {c: jnp.zeros((Q, SSD_N), F32) for c in cs}
            dc_acc = {c: jnp.zeros((Q, SSD_N), F32) for c in cs}
            for h0 in range(0, SSD_R, SSD_LOCKSTEP_HEADS_BWD):
                inst = [(c, r) for c in cs for r in range(h0, h0 + SSD_LOCKSTEP_HEADS_BWD)]
                n = len(inst)
                xr = [xs_ref[rows[c], ps[r]] for c, r in inst]
                dyr = [dy_ref[rows[c], ps[r]] for c, r in inst]
                acr = [acs_s[r, c:c + 1, :] for c, r in inst]
                dtr = [dt_ref[r, c:c + 1, :] for c, r in inst]
                acc = [_col_bcast(a, Q) for a in acr]
                dtb = [_col_bcast(d, Q) for d in dtr]
                al = [a[:, Q - 1:Q] for a in acr]
                e_c = [jnp.exp(a) for a in acc]
                dl_c = [jnp.exp(al[i] - acc[i]) for i in range(n)]
                e_r = [jnp.exp(a) for a in acr]
                dl_r = [jnp.exp(al[i] - acr[i]) for i in range(n)]
                gl = [jnp.exp(a) for a in al]
                lm = [jnp.exp(jnp.where(tri, acc[i] - acr[i], -1e30)) for i in range(n)]
                lmt = [jnp.exp(jnp.where(r_i <= c_i, acr[i] - acc[i], -1e30)) for i in range(n)]
                mmt = [cbt[c] * lmt[i] for i, (c, r) in enumerate(inst)]
                sr = [sall_ref[0, c, :, ps[r]] for c, r in inst]
                dmm0 = [_dot_nt(dyr[i], xr[i]) for i in range(n)]
                dmm0t = [_dot_nt(xr[i], dyr[i]) for i in range(n)]
                dxd1 = [_dot(mmt[i], dyr[i]) for i in range(n)]
                dce = [_dot_nt(dyr[i], sr[i]) for i in range(n)]
                dcet = [_dot_nt(sr[i], dyr[i]) for i in range(n)]
                cdy = [_dot(cct[c] * e_r[i], dyr[i]) for i, (c, r) in enumerate(inst)]
                dsn = []
                for i, (c, r) in enumerate(inst):
                    dsn.append(ds_cur[r])
                    ds_cur[r] = gl[i] * ds_cur[r] + cdy[i]
                dxd = [dxd1[i] + _dot(bc_[c] * dl_c[i], dsn[i]) for i, (c, r) in enumerate(inst)]
                dbd0 = [_dot_nt(xr[i], dsn[i]) for i in range(n)]
                dbd0t = [_dot_nt(dsn[i], xr[i]) for i in range(n)]
                for i, (c, r) in enumerate(inst):
                    dgl = tsum(dsn[i] * sr[i].astype(F32))
                    dc_acc[c] = dc_acc[c] + dce[i] * e_c[i]
                    db_acc[c] = db_acc[c] + dbd0[i] * (dtb[i] * dl_c[i])
                    dl0 = dmm0[i] * lm[i]
                    dl0t = dmm0t[i] * (lmt[i] * dtb[i])
                    dcb[c] = dcb[c] + dl0 * dtr[i]
                    dcbt[c] = dcbt[c] + dl0t
                    csum_gm0 = csum(dl0 * cb[c])
                    rsum_gm = csum(dl0t * cbt[c])
                    r_de = csum(dcet[i] * cct[c]) * e_r[i]
                    r_dl = csum(dbd0t[i] * bct[c]) * dl_r[i]
                    dalast = jnp.sum(r_dl * dtr[i], axis=1, keepdims=True) + dgl * gl[i]
                    dacs_s[r, c:c + 1, :] = (rsum_gm + r_de - (r_dl + csum_gm0) * dtr[i]
                                             + jnp.where(lane == Q - 1, dalast, 0.0))
                    ddt_s[r, c:c + 1, :] = csum_gm0 + r_dl
                    dd_s[r, c:c + 1, :] = _dot_nt(ones8, dyr[i] * xr[i])[0:1]
                    dxs_ref[rows[c], ps[r]] = dxd[i] * dtb[i][:, :SSD_P] + dsk_ref[SSD_R * gi + r] * dyr[i]
            for c in cs:
                dc_ref[rows[c], :] = dc_acc[c] + _dot(dcb[c], bc_[c])
                db_ref[rows[c], :] = db_acc[c] + _dot(dcbt[c], cc_[c])
        for r in range(SSD_R):
            dS[:, ps[r]] = ds_cur[r]
        for r in range(SSD_R):
            da_ref[r] = _dot_hi(dacs_s[r], lower)
            ddt_ref[r] = ddt_s[r]
            dd_ref[r] = dd_s[r]

    rev = lambda i: nT - 1 - i
    smem = pl.BlockSpec(memory_space=pltpu.SMEM)
    rows_spec = pl.BlockSpec((SSD_R, cpt, Q), lambda g, i: (g, rev(i), 0))
    x_spec = pl.BlockSpec((tt, GW), lambda g, i: (rev(i), g))
    n_spec = pl.BlockSpec((tt, SSD_N), lambda g, i: (rev(i), g))
    rows_shape = jax.ShapeDtypeStruct((SSD_H, nC, Q), F32)
    return pl.pallas_call(
        body, grid=(SSD_G, nT),
        in_specs=[smem, smem, x_spec, n_spec, n_spec, rows_spec,
                  pl.BlockSpec((1, cpt, SSD_N, GW), lambda g, i: (g, rev(i), 0, 0)), x_spec],
        out_specs=[x_spec, n_spec, n_spec, rows_spec, rows_spec, rows_spec],
        out_shape=[jax.ShapeDtypeStruct((T, D_INNER), F32), jax.ShapeDtypeStruct((T, SSD_G * SSD_N), F32),
                   jax.ShapeDtypeStruct((T, SSD_G * SSD_N), F32), rows_shape, rows_shape, rows_shape],
        scratch_shapes=[pltpu.VMEM((SSD_N, GW), F32)] + [pltpu.VMEM((SSD_R, cpt, Q), F32)] * 4,
        compiler_params=_params("parallel", "arbitrary"), name=name,
    )(alog, dskip, xs, bm, cm, dt, sall, dy)


def _ssd_gate_bwd(dtraw, dt, da, ddt_direct, ddrow, alog, dtb, name):
    H, T = dtraw.shape

    def body(raw_ref, dt_ref, da_ref, ddt_ref, dd_ref, alog_ref, dtb_ref, draw_ref, dalog_ref, ddtb_ref, dD_ref):
        a = -jnp.exp(alog_ref[...])
        dav = da_ref[...]
        ddt = ddt_ref[...] + dav * a
        draw = ddt * _sigmoid(raw_ref[...] + dtb_ref[...])
        draw_ref[...] = draw
        dalog_ref[...] = jnp.sum(dav * dt_ref[...], axis=1, keepdims=True) * a
        ddtb_ref[...] = jnp.sum(draw, axis=1, keepdims=True)
        dD_ref[...] = jnp.sum(dd_ref[...], axis=1, keepdims=True)

    return pl.pallas_call(
        body,
        out_shape=[jax.ShapeDtypeStruct((H, T), F32)] + [jax.ShapeDtypeStruct((H, 1), F32)] * 3,
        compiler_params=pltpu.CompilerParams(vmem_limit_bytes=VMEM_LIMIT_BYTES), name=name,
    )(dtraw, dt, da, ddt_direct, ddrow, alog, dtb)


def _adamw(parts, w, m, v, name):
    R, C = w.shape
    tr = 256 if R % 256 == 0 else R

    def body(p_ref, w_ref, m_ref, v_ref, g_ref, d_ref, nm_ref, nv_ref):
        g = p_ref[0].astype(F32)
        for s in range(1, N_DEV):
            g = g + p_ref[s].astype(F32)
        mn = ADAM_B1 * m_ref[...] + (1.0 - ADAM_B1) * g
        vn = ADAM_B2 * v_ref[...] + (1.0 - ADAM_B2) * (g * g)
        mh = mn / (1.0 - ADAM_B1 ** ADAM_STEP)
        vh = vn / (1.0 - ADAM_B2 ** ADAM_STEP)
        g_ref[...] = g
        d_ref[...] = -ADAM_LR * (mh / (jnp.sqrt(vh) + ADAM_EPS) + ADAM_WD * w_ref[...])
        nm_ref[...] = mn
        nv_ref[...] = vn

    blk = pl.BlockSpec((tr, C), lambda i: (i, 0))
    return pl.pallas_call(
        body, grid=(R // tr,),
        in_specs=[pl.BlockSpec((N_DEV, tr, C), lambda i: (0, i, 0)), blk, blk, blk],
        out_specs=[blk] * 4,
        out_shape=[jax.ShapeDtypeStruct((R, C), F32)] * 4,
        compiler_params=_params("parallel"), name=name,
    )(parts, w, m, v)


def _me():
    x, y, c = lax.axis_index("x"), lax.axis_index("y"), lax.axis_index("c")
    return x, y, c


def _peer(d):
    x, y, c = _me()
    px = 1 - x if d & 4 else x
    py = 1 - y if d & 2 else y
    pc = 1 - c if d & 1 else c
    return (px, py, pc), 4 * px + 2 * py + pc


def _gather_two_level(arrs, name):
    n = len(arrs)

    def body(*refs):
        ins, outs = refs[:n], refs[n:2 * n]
        ssem, rsem, lsem = refs[2 * n:]
        x, y, c = _me()
        me, sibling = (x, y, c), (x, y, 1 - c)
        chips = [(1 - x, y), (x, 1 - y), (1 - x, 1 - y)]

        def slot(a, block):
            px, py, pc = block
            return outs[a].at[4 * px + 2 * py + pc]

        def copy(a, k, block, to, src=None):
            return pltpu.make_async_remote_copy(
                src_ref=slot(a, block) if src is None else src, dst_ref=slot(a, block),
                send_sem=ssem.at[a, k], recv_sem=rsem.at[a, k], device_id=to, device_id_type=MESH)

        mine = [pltpu.make_async_copy(ins[a], slot(a, me), lsem.at[a]) for a in range(n)]
        for cp in mine:
            cp.start()
        first = []
        for a in range(n):
            first.append(copy(a, 0, me, sibling, src=ins[a]))
            first += [copy(a, 1 + j, me, (*chip, c), src=ins[a]) for j, chip in enumerate(chips)]
        for cp in first:
            cp.start()
        passed = [[copy(a, 4 + j, (*chip, c), sibling) for j, chip in enumerate(chips)] for a in range(n)]
        for j, chip in enumerate(chips):
            for a in range(n):
                copy(a, 1 + j, (*chip, c), me).wait_recv()
                passed[a][j].start()
        for a in range(n):
            copy(a, 0, sibling, me).wait_recv()
            for j, chip in enumerate(chips):
                copy(a, 4 + j, (*chip, 1 - c), me).wait_recv()
        for cp in first + [cp for row in passed for cp in row]:
            cp.wait_send()
        for cp in mine:
            cp.wait()

    anyspec = pl.BlockSpec(memory_space=pl.ANY)
    return pl.pallas_call(
        body,
        in_specs=[anyspec] * n, out_specs=[anyspec] * n,
        out_shape=_exchange_out_shapes(arrs, [True] * n),
        scratch_shapes=_exchange_semaphores(n),
        name=name,
    )(*arrs)


def _exchange(arrs, bcast, name):
    n = len(arrs)

    def body(*refs):
        ex = _Exchange(refs[:n], refs[n:2 * n], bcast, *refs[2 * n:])
        ex.begin()
        ex.finish()

    anyspec = pl.BlockSpec(memory_space=pl.ANY)
    return pl.pallas_call(
        body,
        in_specs=[anyspec] * n, out_specs=[anyspec] * n,
        out_shape=_exchange_out_shapes(arrs, bcast),
        scratch_shapes=_exchange_semaphores(n),
        name=name,
    )(*arrs)


def _exchange_out_shapes(arrs, bcast):
    return [jax.ShapeDtypeStruct((N_DEV,) + (a.shape if b else a.shape[1:]), a.dtype) for a, b in zip(arrs, bcast)]


def _exchange_semaphores(n):
    return [pltpu.SemaphoreType.DMA((n, N_DEV - 1)), pltpu.SemaphoreType.DMA((n, N_DEV - 1)),
            pltpu.SemaphoreType.DMA((n,))]


class _Exchange:
    def __init__(self, ins, outs, bcast, ssem, rsem, lsem):
        n = len(ins)
        x, y, c = _me()
        me = 4 * x + 2 * y + c

        def src(a, dest):
            return ins[a] if bcast[a] else ins[a].at[dest]

        self.local = [pltpu.make_async_copy(src(a, me), outs[a].at[me], lsem.at[a]) for a in range(n)]
        self.sends, self.recvs = [], []
        for a in range(n):
            for d in range(1, N_DEV):
                peer, pid = _peer(d)
                self.sends.append(pltpu.make_async_remote_copy(
                    src_ref=src(a, pid), dst_ref=outs[a].at[me], send_sem=ssem.at[a, d - 1],
                    recv_sem=rsem.at[a, d - 1], device_id=peer, device_id_type=MESH))
                self.recvs.append(pltpu.make_async_remote_copy(
                    src_ref=src(a, pid), dst_ref=outs[a].at[pid], send_sem=ssem.at[a, d - 1],
                    recv_sem=rsem.at[a, d - 1], device_id=peer, device_id_type=MESH))

    def begin(self):
        for cp in self.local + self.sends:
            cp.start()

    def finish(self):
        for cp in self.recvs:
            cp.wait_recv()
        for cp in self.sends:
            cp.wait_send()
        for cp in self.local:
            cp.wait()


def _to_rows(cols, chunk):
    T, H = cols.shape
    return cols.T.reshape(H, T // chunk, chunk)


def _from_rows(rows):
    return rows.T


def _pad_cols(a, width):
    return jnp.pad(a, ((0, 0), (0, width - a.shape[1])))


def _local_step(x, tgt, p, late_weights=None, early_grads=None, late_grads=None):
    T = x.shape[0]
    zb = lambda n: jnp.zeros((1, n), F32)
    gw = p["gdn_w_in"]
    g_wparts = [gw[:, 0:1024], gw[:, 1024:2048], gw[:, 2048:4096], gw[:, 4096:6144], _pad_cols(gw[:, 6144:6176], PAD_W)]
    nw0, nw1 = p["norm_w"][0:1], p["norm_w"][1:2]
    gcw = p["gdn_conv_w"]
    cw_q, cw_k, cw_v = gcw[:, 0:1024], gcw[:, 1024:2048], gcw[:, 2048:4096]
    g_convs = [(cw_q, zb(1024), True, GDN_DK ** -0.5), (cw_k, zb(1024), True, 1.0), (cw_v, zb(2048), False, 1.0),
               None, None]
    if late_weights is None:
        h0, (q_pre, k_pre, v_pre, z0, ab), (q, k, v), g_cpre = _norm_inproj(x, nw0, g_wparts, g_convs, "gdn_inproj")
    else:
        comm, assemble = late_weights
        h0, (q_pre, k_pre, v_pre, z0, ab), (q, k, v), g_cpre, gathered = _norm_inproj(x, nw0, g_wparts, g_convs,
                                                                                      "gdn_inproj", comm)
        p = dict(p, **assemble(gathered))
    braw = _to_rows(ab[:, 0:GDN_HV], GDN_CHUNK)
    araw = _to_rows(ab[:, GDN_HV:2 * GDN_HV], GDN_CHUNK)
    g_alog, g_dtb = p["gdn_a_log"].reshape(-1), p["gdn_dt_bias"].reshape(-1)
    g_u, g_w, g_pm, g_ti, g_rows, beta_rows, gc_rows, g_qd, g_kd = _gdn_prep(q, k, v, araw, braw, g_alog, g_dtb,
                                                                             "gdn_prep")
    o0, g_vn, g_sall = _gdn_state_fwd(g_qd, g_kd, g_u, g_w, g_pm, gc_rows, "gdn_state_fwd")
    x1 = _out_fwd(o0, z0, p["gdn_norm_w"], p["gdn_w_out"], x, GDN_DK, False, "gdn_out")
    sw = p["ssd_w_in"]
    s_wparts = [sw[:, 0:2048], sw[:, 2048:4096], sw[:, 4096:5120], sw[:, 5120:6144], _pad_cols(sw[:, 6144:6176], PAD_W)]
    scw, scb = p["ssd_conv_w"], p["ssd_conv_b"]
    s_convs = [None, (scw[:, 0:2048], scb[:, 0:2048], False, 1.0), (scw[:, 2048:3072], scb[:, 2048:3072], False, 1.0),
               (scw[:, 3072:4096], scb[:, 3072:4096], False, 1.0), None]
    h1, (z1, xs_pre, b_pre, c_pre, dtp), (xs, bm, cm), s_cpre = _norm_inproj(x1, nw1, s_wparts, s_convs, "ssd_inproj")
    dtraw = _to_rows(dtp[:, 0:SSD_H], SSD_CHUNK)
    s_alog, s_dtb, s_d = p["ssd_a_log"].reshape(-1), p["ssd_dt_bias"].reshape(-1), p["ssd_d"].reshape(-1)
    y1, s_sall, dt_rows = _ssd_scan_fwd(xs, bm, cm, dtraw, s_alog, s_dtb, s_d, "ssd_scan_fwd")
    dx2, d_fw, loss = _out_fwd(y1, z1, p["ssd_norm_w"], p["ssd_w_out"], x1, D_INNER // SSD_G, True, "ssd_out_loss",
                               (p["final_norm_w"].reshape(1, -1), tgt))
    dy1, dz1, d_snw, yn1 = _out_bwd(dx2, y1, z1, p["ssd_norm_w"], p["ssd_w_out"], D_INNER // SSD_G, True, "ssd_out_bwd")
    d_swout = _matmul_tn(yn1, dx2, "ssd_wout_grad")
    dxs, dbm, dcm, da_rows, ddt_rows, dd_rows = _ssd_scan_bwd(xs, bm, cm, dt_rows, s_sall, dy1, s_alog, s_d, "ssd_scan_bwd")
    col = lambda a: a.reshape(-1, 1)
    dtraw_g, d_salog, d_sdtb, d_sd = _ssd_gate_bwd(
        dtraw.reshape(SSD_H, T), dt_rows.reshape(SSD_H, T), da_rows.reshape(SSD_H, T),
        ddt_rows.reshape(SSD_H, T), dd_rows.reshape(SSD_H, T), col(s_alog), col(s_dtb), "ssd_gate_bwd")
    dxs_pre, dcw_x, dcb_x = _conv_bwd(xs_pre, s_cpre[0], scw[:, 0:2048], dxs, False, 1.0, "ssd_conv_x_bwd")
    db_pre, dcw_b, dcb_b = _conv_bwd(b_pre, s_cpre[1], scw[:, 2048:3072], dbm, False, 1.0, "ssd_conv_b_bwd")
    dc_pre, dcw_c, dcb_c = _conv_bwd(c_pre, s_cpre[2], scw[:, 3072:4096], dcm, False, 1.0, "ssd_conv_c_bwd")
    ddtp = _pad_cols(_from_rows(dtraw_g), PAD_W)
    s_dparts = [dz1, dxs_pre, db_pre, dc_pre, ddtp]
    dx1, d_nw1 = _inproj_bwd(x1, nw1, s_dparts, s_wparts, dx2, "ssd_inproj_bwd")
    s_dw = [_matmul_tn(h1, d, "ssd_win_grad_%d" % n) for n, d in enumerate(s_dparts)]
    d_swin = jnp.concatenate(s_dw[:4] + [s_dw[4][:, 0:SSD_H]], axis=1)
    early_recv = None
    if early_grads is None:
        do0, dz0, d_gnw, yn0 = _out_bwd(dx1, o0, z0, p["gdn_norm_w"], p["gdn_w_out"], GDN_DK, False, "gdn_out_bwd")
    else:
        do0, dz0, d_gnw, yn0, early_recv = _out_bwd(dx1, o0, z0, p["gdn_norm_w"], p["gdn_w_out"], GDN_DK, False,
                                                    "gdn_out_bwd", early_grads(d_swin, d_swout))
    d_gwout = _matmul_tn(yn0, dx1, "gdn_wout_grad")
    g_dvn, g_dkd, g_dgl = _gdn_state_bwd(g_qd, g_kd, g_w, g_pm, g_vn, g_sall, gc_rows, do0, "gdn_state_bwd")
    dq, dk, dv, dg_rows, dbeta_rows = _gdn_local_bwd(q, k, v, gc_rows, beta_rows, g_ti, g_u, g_w, g_pm, g_vn, g_sall,
                                                     do0, g_dvn, g_dkd, g_dgl, "gdn_local_bwd")
    da_g, db_g, d_galog, d_gdtb = _gdn_gate_bwd(
        araw.reshape(GDN_HV, T), braw.reshape(GDN_HV, T), dg_rows.reshape(GDN_HV, T),
        dbeta_rows.reshape(GDN_HV, T), col(g_alog), col(g_dtb), "gdn_gate_bwd")
    dq_pre, dcw_q, _ = _conv_bwd(q_pre, g_cpre[0], cw_q, dq, True, GDN_DK ** -0.5, "gdn_conv_q_bwd")
    dk_pre, dcw_k, _ = _conv_bwd(k_pre, g_cpre[1], cw_k, dk, True, 1.0, "gdn_conv_k_bwd")
    dv_pre, dcw_v, _ = _conv_bwd(v_pre, g_cpre[2], cw_v, dv, False, 1.0, "gdn_conv_v_bwd")
    dab = _pad_cols(jnp.concatenate([_from_rows(db_g), _from_rows(da_g)], axis=1), PAD_W)
    g_dparts = [dq_pre, dk_pre, dv_pre, dz0, dab]
    g_dw = [_matmul_tn(h0, d, "gdn_win_grad_%d" % n) for n, d in enumerate(g_dparts)]
    d_gwin = jnp.concatenate(g_dw[:4] + [g_dw[4][:, 0:2 * GDN_HV]], axis=1)
    sharded_grads = {
        "gdn_w_in": d_gwin, "gdn_w_out": d_gwout,
        "gdn_conv_w": jnp.concatenate([dcw_q, dcw_k, dcw_v], axis=1),
        "ssd_conv_w": jnp.concatenate([dcw_x, dcw_b, dcw_c], axis=1),
        "ssd_conv_b": jnp.concatenate([dcb_x, dcb_b, dcb_c], axis=1), "ssd_norm_w": d_snw}
    late_recv = None
    if late_grads is None:
        dx0, d_nw0 = _inproj_bwd(x, nw0, g_dparts, g_wparts, dx1, "gdn_inproj_bwd")
    else:
        dx0, d_nw0, late_recv = _inproj_bwd(x, nw0, g_dparts, g_wparts, dx1, "gdn_inproj_bwd",
                                            late_grads(sharded_grads))
    grads = {
        "norm_w": jnp.concatenate([d_nw0, d_nw1], axis=0),
        "gdn_w_in": d_gwin,
        "gdn_conv_w": jnp.concatenate([dcw_q, dcw_k, dcw_v], axis=1),
        "gdn_a_log": d_galog.reshape(1, -1),
        "gdn_dt_bias": d_gdtb.reshape(1, -1),
        "gdn_norm_w": d_gnw,
        "gdn_w_out": d_gwout,
        "ssd_w_in": d_swin,
        "ssd_conv_w": jnp.concatenate([dcw_x, dcw_b, dcw_c], axis=1),
        "ssd_conv_b": jnp.concatenate([dcb_x, dcb_b, dcb_c], axis=1),
        "ssd_dt_bias": d_sdtb.reshape(1, -1),
        "ssd_a_log": d_salog.reshape(1, -1),
        "ssd_d": d_sd.reshape(1, -1),
        "ssd_norm_w": d_snw,
        "ssd_w_out": d_swout,
        "final_norm_w": d_fw,
    }
    if early_grads is not None:
        return loss, dx0, grads, early_recv, late_recv
    return loss, dx0, grads


WEIGHTS = ["norm_w", "gdn_w_in", "gdn_conv_w", "gdn_a_log", "gdn_dt_bias", "gdn_norm_w", "gdn_w_out", "ssd_w_in",
           "ssd_conv_w", "ssd_conv_b", "ssd_dt_bias", "ssd_a_log", "ssd_d", "ssd_norm_w", "ssd_w_out", "final_norm_w"]
COL_SHARDED = ["gdn_w_in", "ssd_w_in"]
ROW_SHARDED = ["gdn_w_out", "ssd_w_out"]
SMALL_SHARDED = ["gdn_conv_w", "ssd_conv_w", "ssd_conv_b", "ssd_norm_w"]
REPLICATED = ["norm_w", "gdn_a_log", "gdn_dt_bias", "gdn_norm_w", "ssd_dt_bias", "ssd_a_log", "ssd_d", "final_norm_w"]


def _pack(arrs):
    return jnp.concatenate([a.reshape(-1) for a in arrs]).reshape(1, -1)


def _unpack(flat, shapes):
    out, pos = [], 0
    for s in shapes:
        n = 1
        for dim in s:
            n *= dim
        out.append(flat[pos:pos + n].reshape(s))
        pos += n
    return out


def _cols_to_shards(full):
    R, C = full.shape
    return full.reshape(R, N_DEV, C // N_DEV).transpose(1, 0, 2)


def _shards_to_cols(shards):
    n, R, c = shards.shape
    return shards.transpose(1, 0, 2).reshape(R, n * c)


def kernel(x, norm_w, gdn_w_in, gdn_conv_w, gdn_a_log, gdn_dt_bias, gdn_norm_w, gdn_w_out, ssd_w_in, ssd_conv_w, ssd_conv_b, ssd_dt_bias, ssd_a_log, ssd_d, ssd_norm_w, ssd_w_out, final_norm_w, loss_target, m_norm_w, m_gdn_w_in, m_gdn_conv_w, m_gdn_a_log, m_gdn_dt_bias, m_gdn_norm_w, m_gdn_w_out, m_ssd_w_in, m_ssd_conv_w, m_ssd_conv_b, m_ssd_dt_bias, m_ssd_a_log, m_ssd_d, m_ssd_norm_w, m_ssd_w_out, m_final_norm_w, v_norm_w, v_gdn_w_in, v_gdn_conv_w, v_gdn_a_log, v_gdn_dt_bias, v_gdn_norm_w, v_gdn_w_out, v_ssd_w_in, v_ssd_conv_w, v_ssd_conv_b, v_ssd_dt_bias, v_ssd_a_log, v_ssd_d, v_ssd_norm_w, v_ssd_w_out, v_final_norm_w):
    w = dict(norm_w=norm_w, gdn_w_in=gdn_w_in[0], gdn_conv_w=gdn_conv_w[0], gdn_a_log=gdn_a_log,
             gdn_dt_bias=gdn_dt_bias, gdn_norm_w=gdn_norm_w, gdn_w_out=gdn_w_out[0], ssd_w_in=ssd_w_in[0],
             ssd_conv_w=ssd_conv_w[0], ssd_conv_b=ssd_conv_b, ssd_dt_bias=ssd_dt_bias, ssd_a_log=ssd_a_log,
             ssd_d=ssd_d, ssd_norm_w=ssd_norm_w, ssd_w_out=ssd_w_out[0], final_norm_w=final_norm_w.reshape(1, -1))
    m = dict(norm_w=m_norm_w, gdn_w_in=m_gdn_w_in[0], gdn_conv_w=m_gdn_conv_w[0], gdn_a_log=m_gdn_a_log,
             gdn_dt_bias=m_gdn_dt_bias, gdn_norm_w=m_gdn_norm_w, gdn_w_out=m_gdn_w_out[0], ssd_w_in=m_ssd_w_in[0],
             ssd_conv_w=m_ssd_conv_w[0], ssd_conv_b=m_ssd_conv_b, ssd_dt_bias=m_ssd_dt_bias, ssd_a_log=m_ssd_a_log,
             ssd_d=m_ssd_d, ssd_norm_w=m_ssd_norm_w, ssd_w_out=m_ssd_w_out[0], final_norm_w=m_final_norm_w.reshape(1, -1))
    v = dict(norm_w=v_norm_w, gdn_w_in=v_gdn_w_in[0], gdn_conv_w=v_gdn_conv_w[0], gdn_a_log=v_gdn_a_log,
             gdn_dt_bias=v_gdn_dt_bias, gdn_norm_w=v_gdn_norm_w, gdn_w_out=v_gdn_w_out[0], ssd_w_in=v_ssd_w_in[0],
             ssd_conv_w=v_ssd_conv_w[0], ssd_conv_b=v_ssd_conv_b, ssd_dt_bias=v_ssd_dt_bias, ssd_a_log=v_ssd_a_log,
             ssd_d=v_ssd_d, ssd_norm_w=v_ssd_norm_w, ssd_w_out=v_ssd_w_out[0], final_norm_w=v_final_norm_w.reshape(1, -1))
    out_shapes = {n: a.shape for n, a in zip(
        WEIGHTS, [norm_w, gdn_w_in, gdn_conv_w, gdn_a_log, gdn_dt_bias, gdn_norm_w, gdn_w_out, ssd_w_in, ssd_conv_w,
                  ssd_conv_b, ssd_dt_bias, ssd_a_log, ssd_d, ssd_norm_w, ssd_w_out, final_norm_w])}

    small_shapes = [w[n].shape for n in SMALL_SHARDED]
    first = _gather_two_level([_mx(w["gdn_w_in"]), _pack([w[n] for n in SMALL_SHARDED])], "gather_first")
    full = dict(w)
    full["gdn_w_in"] = _shards_to_cols(first[0])
    small_all = [_unpack(first[1][s, 0], small_shapes) for s in range(N_DEV)]
    for idx, n in enumerate(SMALL_SHARDED):
        full[n] = jnp.concatenate([small_all[s][idx] for s in range(N_DEV)], axis=-1)
    late = ["gdn_w_out", "ssd_w_in", "ssd_w_out"]

    def assemble(gathered):
        return {"gdn_w_out": gathered[0].reshape(-1, D_MODEL), "ssd_w_in": _shards_to_cols(gathered[1]),
                "ssd_w_out": gathered[2].reshape(-1, D_MODEL)}

    def early_grads(d_ssd_w_in, d_ssd_w_out):
        return ([_cols_to_shards(d_ssd_w_in).astype(GRAD_WIRE_DTYPE),
                 d_ssd_w_out.reshape(N_DEV, -1, D_MODEL).astype(GRAD_WIRE_DTYPE)], [False] * 2)

    def late_grads(g):
        send_small = jnp.concatenate(
            [_cols_to_shards(g[n]).reshape(N_DEV, -1) for n in SMALL_SHARDED], axis=1)[:, None, :]
        return ([_cols_to_shards(g["gdn_w_in"]).astype(GRAD_WIRE_DTYPE),
                 g["gdn_w_out"].reshape(N_DEV, -1, D_MODEL).astype(GRAD_WIRE_DTYPE), send_small], [False] * 3)

    loss, dx, grads, ssd_recv, gdn_recv = _local_step(
        x[0], loss_target[0], full, (([_mx(w[n]) for n in late], [True] * 3), assemble), early_grads, late_grads)

    rep_shapes = [w[n].shape for n in REPLICATED]
    tail = jnp.zeros((PAD_W - 1,), F32)
    recv_rep = _exchange([_pack([grads[n] for n in REPLICATED] + [loss, tail])], [True], "exchange_grads")[0]

    res = {}
    for n, parts in zip(["gdn_w_in", "gdn_w_out", "ssd_w_in", "ssd_w_out"], list(gdn_recv[:2]) + list(ssd_recv)):
        res[n] = _adamw(parts, w[n], m[n], v[n], "adamw_" + n)
    small_res = _adamw(gdn_recv[2], *[_pack([t[n] for n in SMALL_SHARDED]) for t in (w, m, v)], "adamw_small")
    rep_res = _adamw(recv_rep, *[_pack([t[n] for n in REPLICATED] + [jnp.zeros((PAD_W,), F32)]) for t in (w, m, v)],
                     "adamw_replicated")
    for k4 in range(4):
        for n, a in zip(SMALL_SHARDED, _unpack(small_res[k4][0], small_shapes)):
            res.setdefault(n, [None] * 4)[k4] = a
        for n, a in zip(REPLICATED, _unpack(rep_res[k4][0], rep_shapes)):
            res.setdefault(n, [None] * 4)[k4] = a

    loss = rep_res[0][0, sum(a.size for a in (w[n] for n in REPLICATED))]
    outs = [loss, dx[None]]
    for k4 in range(4):
        outs += [res[n][k4].reshape(out_shapes[n]) for n in WEIGHTS]
    return tuple(outs)
```

```python
import jax
import jax.numpy as jnp
from jax import lax
from jax.experimental import pallas as pl
from jax.experimental.pallas import tpu as pltpu

F32 = jnp.float32
MXU_DTYPE = jnp.bfloat16
GRAD_WIRE_DTYPE = jnp.bfloat16
HI = lax.Precision.HIGHEST
EPS = 1e-6
VMEM_LIMIT_BYTES = 56 * 1024 * 1024
N_DEV = 8
MESH = pl.DeviceIdType.MESH

D_MODEL = 1024
CONV_K = 4
GDN_HV = 16
GDN_DK = 128
GDN_CHUNK = 64
SSD_H = 32
SSD_P = 64
SSD_N = 128
SSD_G = 8
SSD_R = SSD_H // SSD_G
SSD_CHUNK = 128
D_INNER = 2048
PAD_W = 128

ADAM_LR = 0.001
ADAM_B1 = 0.9
ADAM_B2 = 0.999
ADAM_EPS = 1e-08
ADAM_WD = 0.01
ADAM_STEP = 10


def _params(*sem):
    return pltpu.CompilerParams(dimension_semantics=sem, vmem_limit_bytes=VMEM_LIMIT_BYTES)


def _mx(a):
    return a.astype(MXU_DTYPE)


def _dot(a, b):
    return jnp.dot(_mx(a), _mx(b), preferred_element_type=F32)


def _dot_nt(a, b):
    return lax.dot_general(_mx(a), _mx(b), (((1,), (1,)), ((), ())), preferred_element_type=F32)


def _dot_tn(a, b):
    return lax.dot_general(_mx(a), _mx(b), (((0,), (0,)), ((), ())), preferred_element_type=F32)


def _dot_hi(a, b):
    return jnp.dot(a, b, precision=HI, preferred_element_type=F32)


def _sigmoid(x):
    return 0.5 * jnp.tanh(0.5 * x) + 0.5


def _silu(x):
    return x * _sigmoid(x)


def _dsilu(x):
    s = _sigmoid(x)
    return s * (1.0 + x * (1.0 - s))


def _softplus(x):
    return jnp.maximum(x, 0.0) + jnp.log1p(jnp.exp(-jnp.abs(x)))


def _col(r, eye):
    return jnp.sum(jnp.where(eye, r, 0.0), axis=1, keepdims=True)


def _row(c, eye):
    return jnp.sum(jnp.where(eye, c, 0.0), axis=0, keepdims=True)


def _col_bcast(r, n):
    return jnp.broadcast_to(r, (n, n)).T


def _masks(n):
    r = lax.broadcasted_iota(jnp.int32, (n, n), 0)
    c = lax.broadcasted_iota(jnp.int32, (n, n), 1)
    return r >= c, r > c, r == c, r, c


def _with_exchange(comm):
    arrs, bcast = comm[:2] if comm else ([], [])
    two_level = bool(comm) and len(comm) > 2
    make = lambda *a: (_TwoLevelGather if two_level else _Exchange)(*a)
    nc = len(arrs)
    anyspec = pl.BlockSpec(memory_space=pl.ANY)

    def wrap(compute, n_in, n_out):
        def body(*refs):
            cin, cout = refs[n_in:n_in + nc], refs[n_in + nc + n_out:n_in + 2 * nc + n_out]
            sems = refs[n_in + 2 * nc + n_out:n_in + 2 * nc + n_out + 3]
            rest = refs[:n_in] + refs[n_in + nc:n_in + nc + n_out] + refs[n_in + 2 * nc + n_out + (3 if nc else 0):]
            if nc:
                @pl.when(pl.program_id(0) == 0)
                def _():
                    make(cin, cout, bcast, *sems).begin()
            compute(*rest)
            if nc:
                @pl.when(pl.program_id(0) == pl.num_programs(0) - 1)
                def _():
                    make(cin, cout, bcast, *sems).finish()
        return body

    return dict(arrs=list(arrs), nc=nc, wrap=wrap, in_specs=[anyspec] * nc, out_specs=[anyspec] * nc,
                out_shape=_exchange_out_shapes(arrs, bcast), scratch=_exchange_semaphores(nc) if nc else [])


INPROJ_CONV_STRIP = 256
INPROJ_COL_BLOCK = 512


def _norm_cast(x, nw, name, comm=None):
    T = x.shape[0]
    tt = min(T, 1024)
    ex = _with_exchange(comm)

    def compute(x_ref, nw_ref, h_ref):
        xv = x_ref[...]
        r = lax.rsqrt(jnp.mean(xv * xv, axis=-1, keepdims=True) + EPS)
        h_ref[...] = _mx(xv * r * nw_ref[...])

    row = pl.BlockSpec((tt, D_MODEL), lambda i: (i, 0))
    outs = pl.pallas_call(
        ex["wrap"](compute, 2, 1), grid=(T // tt,),
        in_specs=[row, pl.BlockSpec(nw.shape, lambda i: (0, 0))] + ex["in_specs"],
        out_specs=[row] + ex["out_specs"],
        out_shape=[jax.ShapeDtypeStruct((T, D_MODEL), MXU_DTYPE)] + ex["out_shape"],
        scratch_shapes=ex["scratch"],
        compiler_params=_params("arbitrary"), name=name,
    )(x, nw, *ex["arrs"])
    outs = list(outs)
    return outs[0], outs[1:]


def _norm_inproj(x, nw, wparts, convs, name, comm=None, prenormed=False):
    T = x.shape[0]
    tt = min(T, 256)
    n = len(wparts)
    ck = [k for k in range(n) if convs[k] is not None]
    nconv = len(ck)
    widths = [w.shape[1] for w in wparts]
    conv_blocks = [(k, c0) for k in ck for c0 in range(0, widths[k], INPROJ_COL_BLOCK)]
    ex = _with_exchange(comm)

    def compute(x_ref, nw_ref, *refs):
        w_refs, cw_refs = refs[:n], refs[n:n + 2 * nconv]
        h_ref, o_refs = refs[n + 2 * nconv], refs[n + 2 * nconv + 1:2 * n + 2 * nconv + 1]
        post_refs = refs[2 * n + 2 * nconv + 1:2 * n + 3 * nconv + 1]
        cpre_refs = refs[2 * n + 3 * nconv + 1:2 * n + 4 * nconv + 1]
        p_refs = refs[2 * n + 4 * nconv + 1:]
        if prenormed:
            h = x_ref[...]
        else:
            xv = x_ref[...]
            r = lax.rsqrt(jnp.mean(xv * xv, axis=-1, keepdims=True) + EPS)
            h = _mx(xv * r * nw_ref[...])
        h_ref[...] = h
        p_of = dict(zip(conv_blocks, p_refs))
        for P in p_refs:
            @pl.when(pl.program_id(0) == 0)
            def _():
                P[0:HALO, :] = jnp.zeros((HALO, P.shape[1]), F32)

        def conv_block(k, c0, cw):
            m = ck.index(k)
            _, _, l2, scale = convs[k]
            cw_ref, cb_ref, out_ref, P = cw_refs[2 * m], cw_refs[2 * m + 1], post_refs[m], p_of[(k, c0)]
            cs = slice(c0, c0 + cw)
            for r0 in range(0, tt, INPROJ_CONV_STRIP):
                rs = slice(r0, r0 + INPROJ_CONV_STRIP)
                acc = cb_ref[:, cs] + cw_ref[0:1, cs] * P[pl.ds(HALO - 3 + r0, INPROJ_CONV_STRIP), :]
                for j in range(1, CONV_K):
                    acc = acc + cw_ref[j:j + 1, cs] * P[pl.ds(HALO - 3 + j + r0, INPROJ_CONV_STRIP), :]
                cpre_refs[m][rs, cs] = acc
                s = _silu(acc)
                if l2:
                    sls = [slice(g0, g0 + GDN_DK) for g0 in range(0, cw, GDN_DK)]
                    rr = [lax.rsqrt(jnp.sum(s[:, sl] * s[:, sl], axis=-1, keepdims=True) + EPS) for sl in sls]
                    for sl, rg in zip(sls, rr):
                        out_ref[rs, c0 + sl.start:c0 + sl.stop] = s[:, sl] * rg * scale
                else:
                    out_ref[rs, cs] = s
            P[0:HALO, :] = P[tt:tt + HALO, :]

        pending = None
        for k in range(n):
            for c0 in range(0, widths[k], INPROJ_COL_BLOCK):
                cw = min(INPROJ_COL_BLOCK, widths[k] - c0)
                pre = jnp.dot(h, w_refs[k][:, c0:c0 + cw], preferred_element_type=F32)
                o_refs[k][:, c0:c0 + cw] = pre
                if convs[k] is not None:
                    p_of[(k, c0)][HALO:HALO + tt, :] = pre
                if pending is not None:
                    conv_block(*pending)
                pending = (k, c0, cw) if convs[k] is not None else None
        if pending is not None:
            conv_block(*pending)

    row = lambda width: pl.BlockSpec((tt, width), lambda i: (i, 0))
    full = lambda a: pl.BlockSpec(a.shape, lambda i: (0, 0))
    once = lambda a: pl.BlockSpec(a.shape, lambda i: (0, 0), pipeline_mode=pl.Buffered(1))
    conv_args = [a for k in ck for a in convs[k][:2]]
    outs = pl.pallas_call(
        ex["wrap"](compute, 2 + n + 2 * nconv, 1 + n + 2 * nconv), grid=(T // tt,),
        in_specs=[row(D_MODEL), full(nw)] + [once(w) for w in wparts] + [full(a) for a in conv_args] + ex["in_specs"],
        out_specs=[row(D_MODEL)] + [row(wd) for wd in widths] + [row(widths[k]) for k in ck + ck] + ex["out_specs"],
        out_shape=[jax.ShapeDtypeStruct((T, D_MODEL), MXU_DTYPE)]
        + [jax.ShapeDtypeStruct((T, wd), F32) for wd in widths]
        + [jax.ShapeDtypeStruct((T, widths[k]), F32) for k in ck + ck] + ex["out_shape"],
        scratch_shapes=ex["scratch"] + [pltpu.VMEM((HALO + tt, min(INPROJ_COL_BLOCK, widths[k] - c0)), F32)
                                        for k, c0 in conv_blocks],
        compiler_params=_params("arbitrary"), name=name,
    )(x, nw, *wparts, *conv_args, *ex["arrs"])
    outs = list(outs)
    res = (outs[0], outs[1:1 + n], outs[1 + n:1 + n + nconv], outs[1 + n + nconv:1 + n + 2 * nconv])
    return res + (outs[1 + n + 2 * nconv:],) if comm else res


def _inproj_bwd(x, nw, dparts, wparts, dres, name, comm=None):
    T = x.shape[0]
    tt = min(T, 512)
    n = len(wparts)
    ex = _with_exchange(comm)

    def body(x_ref, nw_ref, dres_ref, *refs):
        d_refs, w_refs, dx_ref, dnw_ref = refs[:n], refs[n:2 * n], refs[2 * n], refs[2 * n + 1]

        @pl.when(pl.program_id(0) == 0)
        def _():
            dnw_ref[...] = jnp.zeros_like(dnw_ref)

        dh = _dot_nt(d_refs[0][...], w_refs[0][...])
        for d_ref, w_ref in zip(d_refs[1:], w_refs[1:]):
            dh = dh + _dot_nt(d_ref[...], w_ref[...])
        xv = x_ref[...]
        r = lax.rsqrt(jnp.mean(xv * xv, axis=-1, keepdims=True) + EPS)
        xh = xv * r
        dnw_ref[...] += jnp.sum(dh * xh, axis=0, keepdims=True)
        dxn = dh * nw_ref[...]
        dx_ref[...] = dres_ref[...] + r * (dxn - xh * jnp.mean(dxn * xh, axis=-1, keepdims=True))

    row = lambda width: pl.BlockSpec((tt, width), lambda i: (i, 0))
    full = lambda a: pl.BlockSpec(a.shape, lambda i: (0, 0))
    outs = pl.pallas_call(
        ex["wrap"](body, 3 + 2 * n, 2), grid=(T // tt,),
        in_specs=[row(D_MODEL), full(nw), row(D_MODEL)] + [row(d.shape[1]) for d in dparts]
        + [pl.BlockSpec(w.shape, lambda i: (0, 0), pipeline_mode=pl.Buffered(1)) for w in wparts] + ex["in_specs"],
        out_specs=[row(D_MODEL), pl.BlockSpec((1, D_MODEL), lambda i: (0, 0))] + ex["out_specs"],
        out_shape=[jax.ShapeDtypeStruct((T, D_MODEL), F32), jax.ShapeDtypeStruct((1, D_MODEL), F32)]
        + ex["out_shape"],
        scratch_shapes=ex["scratch"],
        compiler_params=_params("arbitrary"), name=name,
    )(x, nw, dres, *dparts, *wparts, *ex["arrs"])
    outs = list(outs)
    return outs[:2] + ([outs[2:]] if comm else [])


def _matmul_tn(a, b, name):
    T, K = a.shape
    N = b.shape[1]
    tt = min(T, 2048)
    tn = min(N, 1024)

    def body(a_ref, b_ref, o_ref):
        @pl.when(pl.program_id(1) == 0)
        def _():
            o_ref[...] = jnp.zeros_like(o_ref)

        o_ref[...] += _dot_tn(a_ref[...], b_ref[...])

    return pl.pallas_call(
        body, grid=(N // tn, T // tt),
        in_specs=[pl.BlockSpec((tt, K), lambda n, t: (t, 0)), pl.BlockSpec((tt, tn), lambda n, t: (t, n))],
        out_specs=pl.BlockSpec((K, tn), lambda n, t: (0, n)),
        out_shape=jax.ShapeDtypeStruct((K, N), F32),
        compiler_params=_params("parallel", "arbitrary"), name=name,
    )(a, b)


OUT_COL_BLOCK = 512


def _out_fwd(o, z, w, wout, xres, gs, gate_first, name, final=None):
    T = o.shape[0]
    tt = min(T, 512)
    nT = T // tt
    wide = w.shape[1] == D_INNER

    def body(o_ref, z_ref, w_ref, wout_ref, x_ref, *refs):
        if final is None:
            out_ref, yn = refs
        else:
            fw_ref, t_ref, dx_ref, dfw_ref, loss_ref, yn, lacc = refs
        acc = x_ref[...]
        pending = None
        for b0 in range(0, D_INNER, OUT_COL_BLOCK):
            for g0 in range(b0, b0 + OUT_COL_BLOCK, gs):
                sl = slice(g0, g0 + gs)
                og, zg = o_ref[:, sl], z_ref[:, sl]
                wg = w_ref[:, sl] if wide else w_ref[...]
                if gate_first:
                    u = og * _silu(zg)
                    r = lax.rsqrt(jnp.mean(u * u, axis=-1, keepdims=True) + EPS)
                    yn[:, sl] = _mx(u * r * wg)
                else:
                    r = lax.rsqrt(jnp.mean(og * og, axis=-1, keepdims=True) + EPS)
                    yn[:, sl] = _mx(og * r * wg * _silu(zg))
            if pending is not None:
                acc = acc + jnp.dot(yn[:, pending], wout_ref[pending, :], preferred_element_type=F32)
            pending = slice(b0, b0 + OUT_COL_BLOCK)
        xv = acc + jnp.dot(yn[:, pending], wout_ref[pending, :], preferred_element_type=F32)
        if final is None:
            out_ref[...] = xv
            return
        i = pl.program_id(0)

        @pl.when(i == 0)
        def _():
            dfw_ref[...] = jnp.zeros_like(dfw_ref)
            lacc[...] = jnp.zeros_like(lacc)

        r = lax.rsqrt(jnp.mean(xv * xv, axis=-1, keepdims=True) + EPS)
        xh = xv * r
        err = xh * fw_ref[...] - t_ref[...]
        lacc[...] += jnp.sum(err * err, axis=0, keepdims=True)
        dout = err * (1.0 / D_MODEL)
        dfw_ref[...] += jnp.sum(dout * xh, axis=0, keepdims=True)
        dxn = dout * fw_ref[...]
        dx_ref[...] = r * (dxn - xh * jnp.mean(dxn * xh, axis=-1, keepdims=True))

        @pl.when(i == nT - 1)
        def _():
            loss_ref[...] = (0.5 / D_MODEL) * jnp.sum(lacc[...], axis=1, keepdims=True)

    row = lambda width: pl.BlockSpec((tt, width), lambda i: (i, 0))
    full = lambda a: pl.BlockSpec(a.shape, lambda i: (0, 0))
    if final is None:
        return pl.pallas_call(
            body, grid=(nT,),
            in_specs=[row(D_INNER), row(D_INNER), full(w), full(wout), row(D_MODEL)],
            out_specs=row(D_MODEL),
            out_shape=jax.ShapeDtypeStruct((T, D_MODEL), F32),
            scratch_shapes=[pltpu.VMEM((tt, D_INNER), MXU_DTYPE)],
            compiler_params=_params("parallel"), name=name,
        )(o, z, w, wout, xres)
    vec = pl.BlockSpec((1, D_MODEL), lambda i: (0, 0))
    return pl.pallas_call(
        body, grid=(nT,),
        in_specs=[row(D_INNER), row(D_INNER), full(w), full(wout), row(D_MODEL), vec, row(D_MODEL)],
        out_specs=[row(D_MODEL), vec, pl.BlockSpec((1, 1), lambda i: (0, 0))],
        out_shape=[jax.ShapeDtypeStruct((T, D_MODEL), F32), jax.ShapeDtypeStruct((1, D_MODEL), F32),
                   jax.ShapeDtypeStruct((1, 1), F32)],
        scratch_shapes=[pltpu.VMEM((tt, D_INNER), MXU_DTYPE), pltpu.VMEM((1, D_MODEL), F32)],
        compiler_params=_params("arbitrary"), name=name,
    )(o, z, w, wout, xres, *final)


def _out_bwd(dx, o, z, w, wout, gs, gate_first, name, comm=None):
    T = o.shape[0]
    tt = min(T, 256)
    wide = w.shape[1] == D_INNER

    def body(dx_ref, o_ref, z_ref, w_ref, wout_ref, do_ref, dz_ref, dw_ref, yn_ref):
        @pl.when(pl.program_id(0) == 0)
        def _():
            dw_ref[...] = jnp.zeros_like(dw_ref)

        dxb = _mx(dx_ref[...])
        blocks = list(range(0, D_INNER, OUT_COL_BLOCK))
        dyn_b = {b0: _dot_nt(dxb, wout_ref[b0:b0 + OUT_COL_BLOCK, :]) for b0 in blocks[:1]}
        dw_acc = jnp.zeros((1, gs), F32)
        for g0 in range(0, D_INNER, gs):
            b0 = g0 - g0 % OUT_COL_BLOCK
            if g0 == b0 and b0 + OUT_COL_BLOCK < D_INNER:
                nb = b0 + OUT_COL_BLOCK
                dyn_b[nb] = _dot_nt(dxb, wout_ref[nb:nb + OUT_COL_BLOCK, :])
            sl = slice(g0, g0 + gs)
            og, zg, dg = o_ref[:, sl], z_ref[:, sl], dyn_b[b0][:, g0 - b0:g0 - b0 + gs]
            wg = w_ref[:, sl] if wide else w_ref[...]
            sz = _silu(zg)
            if gate_first:
                u = og * sz
                r = lax.rsqrt(jnp.mean(u * u, axis=-1, keepdims=True) + EPS)
                uh = u * r
                yn_ref[:, sl] = _mx(uh * wg)
                dw_g = jnp.sum(dg * uh, axis=0, keepdims=True)
                duh = dg * wg
                du = r * (duh - uh * jnp.mean(duh * uh, axis=-1, keepdims=True))
                do_ref[:, sl] = du * sz
                dz_ref[:, sl] = _mx(du * og * _dsilu(zg))
            else:
                r = lax.rsqrt(jnp.mean(og * og, axis=-1, keepdims=True) + EPS)
                oh = og * r
                yn_ref[:, sl] = _mx(oh * wg * sz)
                dw_g = jnp.sum(dg * oh * sz, axis=0, keepdims=True)
                doh = dg * wg * sz
                dz_ref[:, sl] = _mx(dg * oh * wg * _dsilu(zg))
                do_ref[:, sl] = r * (doh - oh * jnp.mean(doh * oh, axis=-1, keepdims=True))
            if wide:
                dw_ref[:, sl] += dw_g
            else:
                dw_acc = dw_acc + dw_g
        if not wide:
            dw_ref[...] += dw_acc

    row = lambda width: pl.BlockSpec((tt, width), lambda i: (i, 0))
    full = lambda a: pl.BlockSpec(a.shape, lambda i: (0, 0))
    ex = _with_exchange(comm)
    outs = pl.pallas_call(
        ex["wrap"](body, 5, 4), grid=(T // tt,),
        in_specs=[row(D_MODEL), row(D_INNER), row(D_INNER), full(w), full(wout)] + ex["in_specs"],
        out_specs=[row(D_INNER), row(D_INNER), full(w), row(D_INNER)] + ex["out_specs"],
        out_shape=[jax.ShapeDtypeStruct((T, D_INNER), F32), jax.ShapeDtypeStruct((T, D_INNER), MXU_DTYPE),
                   jax.ShapeDtypeStruct(w.shape, F32), jax.ShapeDtypeStruct((T, D_INNER), MXU_DTYPE)]
        + ex["out_shape"],
        scratch_shapes=ex["scratch"],
        compiler_params=_params("arbitrary"), name=name,
    )(dx, o, z, w, wout, *ex["arrs"])
    outs = list(outs)
    return outs[:4] + ([outs[4:]] if comm else [])


HALO = 8
CONV_STRIP = 32


def _conv_bwd(pre, cpre_all, w, dpost, l2, scale, name):
    T, C = pre.shape
    tt = min(T, 1024)
    tc = min(C, 1024 if l2 else 512)
    strip = 2 * CONV_STRIP if l2 else CONV_STRIP
    nT = T // tt
    ext = tt + HALO

    def body(pre_ref, cp_ref, cn_ref, dpost_ref, dn_ref, w_ref, dpre_ref, dw_ref, db_ref, Q):
        i = pl.program_id(1)

        @pl.when(i == 0)
        def _():
            dw_ref[...] = jnp.zeros_like(dw_ref)
            db_ref[...] = jnp.zeros_like(db_ref)

        wj = [w_ref[j:j + 1, :] for j in range(CONV_K)]
        keep_next = jnp.where(i < nT - 1, 1.0, 0.0)
        fold = lambda a: jnp.sum(a.reshape(strip // 8, 8, tc), axis=0)
        dw_acc = [jnp.zeros((8, tc), F32) for _ in range(CONV_K)]
        db_acc = jnp.zeros((8, tc), F32)
        for r0 in list(range(0, tt, strip)) + [tt]:
            n = strip if r0 < tt else HALO
            cpre = cp_ref[r0:r0 + n, :] if r0 < tt else cn_ref[...]
            dy = dpost_ref[r0:r0 + n, :] if r0 < tt else dn_ref[...] * keep_next
            sg = _sigmoid(cpre)
            ds_c = sg * (1.0 + cpre * (1.0 - sg))
            if l2:
                s = cpre * sg
                sls = [slice(g0, g0 + GDN_DK) for g0 in range(0, tc, GDN_DK)]
                rr = [lax.rsqrt(jnp.sum(s[:, sl] * s[:, sl], axis=-1, keepdims=True) + EPS) for sl in sls]
                yh = [s[:, sl] * r for sl, r in zip(sls, rr)]
                pr = [jnp.sum(dy[:, sl] * y, axis=-1, keepdims=True) for sl, y in zip(sls, yh)]
                for sl, r, y, p in zip(sls, rr, yh, pr):
                    Q[r0:r0 + n, sl] = (scale * r) * (dy[:, sl] - y * p) * ds_c[:, sl]
                dyc = Q[r0:r0 + n, :]
            else:
                dyc = dy * ds_c
                Q[r0:r0 + n, :] = dyc
            if r0 < tt:
                db_acc = db_acc + fold(dyc)
        for r0 in range(0, tt, strip):
            xs = pre_ref[r0:r0 + strip, :]
            dpre = jnp.zeros((strip, tc), F32)
            for j in range(CONV_K):
                qj = Q[pl.ds(3 - j + r0, strip), :]
                dpre = dpre + wj[j] * qj
                dw_acc[j] = dw_acc[j] + fold(qj * xs)
            dpre_ref[r0:r0 + strip, :] = _mx(dpre)
        for j in range(CONV_K):
            dw_ref[j:j + 1, :] += jnp.sum(dw_acc[j], axis=0, keepdims=True)
        db_ref[...] += jnp.sum(db_acc, axis=0, keepdims=True)

    tile = pl.BlockSpec((tt, tc), lambda j, i: (i, j))
    nxt = pl.BlockSpec((HALO, tc), lambda j, i: (jnp.minimum((i + 1) * (tt // HALO), T // HALO - 1), j))
    return pl.pallas_call(
        body, grid=(C // tc, nT),
        in_specs=[tile, tile, nxt, tile, nxt, pl.BlockSpec((CONV_K, tc), lambda j, i: (0, j))],
        out_specs=[tile, pl.BlockSpec((CONV_K, tc), lambda j, i: (0, j)), pl.BlockSpec((1, tc), lambda j, i: (0, j))],
        out_shape=[jax.ShapeDtypeStruct((T, C), MXU_DTYPE), jax.ShapeDtypeStruct((CONV_K, C), F32),
                   jax.ShapeDtypeStruct((1, C), F32)],
        scratch_shapes=[pltpu.VMEM((ext, tc), F32)],
        compiler_params=_params("parallel", "arbitrary"), name=name,
    )(pre, cpre_all, cpre_all, dpost, dpost, w)


GDN_LOCKSTEP_CHUNKS = 16
GDN_LOCKSTEP_CHUNKS_BWD = 16
GDN_SCAN_HEADS = 16


def _inv_unit_lower_many(nms, eye, n):
    xs = [jnp.where(eye, 1.0, 0.0) - nm for nm in nms]
    ps = list(nms)
    k = 2
    while k < n:
        ps = [_dot(p, p) for p in ps]
        xs = [x + _dot(x, p) for x, p in zip(xs, ps)]
        k *= 2
    return xs


def _gdn_prep(q, k, v, araw, braw, alog, dtb, name):
    T = q.shape[0]
    C = GDN_CHUNK
    tt = min(T, 1024)
    cpt, nC = tt // C, T // C
    grp = min(cpt, GDN_LOCKSTEP_CHUNKS)

    def body(alog_ref, dtb_ref, q_ref, k_ref, v_ref, a_ref, b_ref,
             u_ref, w_ref, pm_ref, ti_ref, g_ref, beta_ref, gc_ref, qd_ref, kd_ref):
        j = pl.program_id(0)
        tri, strict, eye, r_i, c_i = _masks(C)
        upper = jnp.where(r_i <= c_i, 1.0, 0.0)
        gcs, bts = [], []
        for hh in range(2):
            h = 2 * j + hh
            g = -jnp.exp(alog_ref[h]) * _softplus(a_ref[hh] + dtb_ref[h])
            bt = _sigmoid(b_ref[hh])
            gc = _dot_hi(g, upper)
            g_ref[hh], beta_ref[hh], gc_ref[hh] = g, bt, gc
            gcs.append(gc)
            bts.append(bt)
        for c0 in range(0, cpt, grp):
            cs = list(range(c0, c0 + grp))
            inst = [(c, hh) for c in cs for hh in range(2)]
            rows = {c: slice(c * C, (c + 1) * C) for c in cs}
            qc = {c: q_ref[rows[c], :] for c in cs}
            kc = {c: k_ref[rows[c], :] for c in cs}
            kk = {c: _dot_nt(kc[c], kc[c]) for c in cs}
            qk = {c: _dot_nt(qc[c], kc[c]) for c in cs}
            gcr = [gcs[hh][c:c + 1, :] for c, hh in inst]
            gcc = [_col(r, eye) for r in gcr]
            bc = [_col(bts[hh][c:c + 1, :], eye) for c, hh in inst]
            lm = [jnp.exp(jnp.where(tri, cc - r, -1e30)) for cc, r in zip(gcc, gcr)]
            nm = [jnp.where(strict, kk[c] * b * l, 0.0) for (c, hh), b, l in zip(inst, bc, lm)]
            tinv = _inv_unit_lower_many(nm, eye, C)
            e_c = [jnp.exp(cc) for cc in gcc]
            rhs = [jnp.concatenate([v_ref[rows[c], hh * GDN_DK:(hh + 1) * GDN_DK] * b, kc[c] * (b * e)], axis=1)
                   for (c, hh), b, e in zip(inst, bc, e_c)]
            sol = [_dot(t, r) for t, r in zip(tinv, rhs)]
            for (c, hh), s, t, l, e, cc, r in zip(inst, sol, tinv, lm, e_c, gcc, gcr):
                hs = slice(hh * GDN_DK, (hh + 1) * GDN_DK)
                u_ref[rows[c], hs] = s[:, :GDN_DK]
                w_ref[rows[c], hs] = _mx(s[:, GDN_DK:])
                pm_ref[hh, c] = _mx(jnp.where(tri, qk[c] * l, 0.0))
                ti_ref[hh, c] = _mx(t)
                qd_ref[rows[c], hs] = _mx(qc[c] * e)
                kd_ref[rows[c], hs] = _mx(kc[c] * jnp.exp(r[:, C - 1:C] - cc))

    smem = pl.BlockSpec(memory_space=pltpu.SMEM)
    rows_spec = pl.BlockSpec((2, cpt, C), lambda j, i: (j, i, 0))
    qk_spec = pl.BlockSpec((tt, GDN_DK), lambda j, i: (i, j))
    v_spec = pl.BlockSpec((tt, 2 * GDN_DK), lambda j, i: (i, j))
    cc_spec = pl.BlockSpec((2, cpt, C, C), lambda j, i: (j, i, 0, 0))
    rows_shape = jax.ShapeDtypeStruct((GDN_HV, nC, C), F32)
    cc_shape = jax.ShapeDtypeStruct((GDN_HV, nC, C, C), MXU_DTYPE)
    return pl.pallas_call(
        body, grid=(GDN_HV // 2, T // tt),
        in_specs=[smem, smem, qk_spec, qk_spec, v_spec, rows_spec, rows_spec],
        out_specs=[v_spec, v_spec, cc_spec, cc_spec, rows_spec, rows_spec, rows_spec, v_spec, v_spec],
        out_shape=[jax.ShapeDtypeStruct((T, D_INNER), F32), jax.ShapeDtypeStruct((T, D_INNER), MXU_DTYPE),
                   cc_shape, cc_shape, rows_shape, rows_shape, rows_shape,
                   jax.ShapeDtypeStruct((T, D_INNER), MXU_DTYPE), jax.ShapeDtypeStruct((T, D_INNER), MXU_DTYPE)],
        compiler_params=_params("parallel", "parallel"), name=name,
    )(alog, dtb, q, k, v, araw, braw)


def _gdn_state_fwd(q, k, u, w, pm, gc, name):
    T = q.shape[0]
    C = GDN_CHUNK
    HG = GDN_SCAN_HEADS
    tt = min(T, 512)
    cpt, nC = tt // C, T // C

    def body(q_ref, k_ref, u_ref, w_ref, pm_ref, gc_ref, o_ref, vn_ref, sall_ref, S):
        @pl.when(pl.program_id(1) == 0)
        def _():
            S[...] = jnp.zeros_like(S)

        heads = list(range(HG))

        def chunk(c, carry):
            rows = pl.ds(pl.multiple_of(c * C, C), C)
            hs = [slice(h * GDN_DK, (h + 1) * GDN_DK) for h in heads]
            gl = [jnp.exp(gc_ref[h, pl.ds(c, 1), C - 1:C]) for h in heads]
            sv = [S[h] for h in heads]
            for h in heads:
                sall_ref[h, c] = _mx(sv[h])
            ws = [_dot(w_ref[rows, hs[h]], sv[h]) for h in heads]
            qsv = [_dot(q_ref[rows, hs[h]], sv[h]) for h in heads]
            vn = [u_ref[rows, hs[h]] - ws[h] for h in heads]
            pv = [_dot(pm_ref[h, c], vn[h]) for h in heads]
            kv = [_dot_tn(k_ref[rows, hs[h]], vn[h]) for h in heads]
            for h in heads:
                vn_ref[rows, hs[h]] = _mx(vn[h])
                o_ref[rows, hs[h]] = qsv[h] + pv[h]
                S[h] = sv[h] * gl[h] + kv[h]
            return carry

        lax.fori_loop(0, cpt, chunk, 0)

    v_spec = pl.BlockSpec((tt, HG * GDN_DK), lambda g, i: (i, g))
    return pl.pallas_call(
        body, grid=(GDN_HV // HG, T // tt),
        in_specs=[v_spec, v_spec, v_spec, v_spec,
                  pl.BlockSpec((HG, cpt, C, C), lambda g, i: (g, i, 0, 0)),
                  pl.BlockSpec((HG, cpt, C), lambda g, i: (g, i, 0))],
        out_specs=[v_spec, v_spec, pl.BlockSpec((HG, cpt, GDN_DK, GDN_DK), lambda g, i: (g, i, 0, 0))],
        out_shape=[jax.ShapeDtypeStruct((T, D_INNER), F32), jax.ShapeDtypeStruct((T, D_INNER), MXU_DTYPE),
                   jax.ShapeDtypeStruct((GDN_HV, nC, GDN_DK, GDN_DK), MXU_DTYPE)],
        scratch_shapes=[pltpu.VMEM((HG, GDN_DK, GDN_DK), F32)],
        compiler_params=_params("parallel", "arbitrary"), name=name,
    )(q, k, u, w, pm, gc)


def _gdn_state_bwd(q, k, w, pm, vn, sall, gc, do, name):
    T = q.shape[0]
    C = GDN_CHUNK
    HG = GDN_SCAN_HEADS
    tt = min(T, 512)
    cpt, nC, nT = tt // C, T // C, T // tt

    def body(q_ref, k_ref, w_ref, pm_ref, vn_ref, sall_ref, gc_ref, do_ref, dvn_ref, dkd_ref, dgl_ref, dS):
        @pl.when(pl.program_id(1) == 0)
        def _():
            dS[...] = jnp.zeros_like(dS)

        heads = list(range(HG))

        def chunk(ci, carry):
            c = cpt - 1 - ci
            rows = pl.ds(pl.multiple_of(c * C, C), C)
            hs = [slice(h * GDN_DK, (h + 1) * GDN_DK) for h in heads]
            gl = [jnp.exp(gc_ref[h, pl.ds(c, 1), C - 1:C]) for h in heads]
            dsn = [dS[h] for h in heads]
            doc = [do_ref[rows, hs[h]] for h in heads]
            kds = [_dot(k_ref[rows, hs[h]], dsn[h]) for h in heads]
            pdo = [_dot_tn(pm_ref[h, c], doc[h]) for h in heads]
            dkd = [_dot_nt(vn_ref[rows, hs[h]], dsn[h]) for h in heads]
            qdo = [_dot_tn(q_ref[rows, hs[h]], doc[h]) for h in heads]
            dvn = [pdo[h] + kds[h] for h in heads]
            wdv = [_dot_tn(w_ref[rows, hs[h]], dvn[h]) for h in heads]
            for h in heads:
                dgl = jnp.sum(jnp.sum(dsn[h] * sall_ref[h, c].astype(F32), axis=0, keepdims=True), axis=1, keepdims=True)
                dgl_ref[h, pl.ds(c, 1), :] = jnp.broadcast_to(dgl, (1, C))
                dvn_ref[rows, hs[h]] = dvn[h]
                dkd_ref[rows, hs[h]] = dkd[h]
                dS[h] = dsn[h] * gl[h] + qdo[h] - wdv[h]
            return carry

        lax.fori_loop(0, cpt, chunk, 0)

    rev = lambda i: nT - 1 - i
    v_spec = pl.BlockSpec((tt, HG * GDN_DK), lambda g, i: (rev(i), g))
    rows_spec = pl.BlockSpec((HG, cpt, C), lambda g, i: (g, rev(i), 0))
    return pl.pallas_call(
        body, grid=(GDN_HV // HG, nT),
        in_specs=[v_spec, v_spec, v_spec, pl.BlockSpec((HG, cpt, C, C), lambda g, i: (g, rev(i), 0, 0)), v_spec,
                  pl.BlockSpec((HG, cpt, GDN_DK, GDN_DK), lambda g, i: (g, rev(i), 0, 0)), rows_spec, v_spec],
        out_specs=[v_spec, v_spec, rows_spec],
        out_shape=[jax.ShapeDtypeStruct((T, D_INNER), F32), jax.ShapeDtypeStruct((T, D_INNER), F32),
                   jax.ShapeDtypeStruct((GDN_HV, nC, C), F32)],
        scratch_shapes=[pltpu.VMEM((HG, GDN_DK, GDN_DK), F32)],
        compiler_params=_params("parallel", "arbitrary"), name=name,
    )(q, k, w, pm, vn, sall, gc, do)


def _gdn_local_bwd(q, k, v, gc, beta, tinv, u, w, pm, vn, sall, do, dvn, dkd, dgl, name):
    T = q.shape[0]
    C = GDN_CHUNK
    tt = min(T, 1024)
    cpt, nC = tt // C, T // C
    grp = min(cpt, GDN_LOCKSTEP_CHUNKS_BWD)

    def body(q_ref, k_ref, v_ref, gc_ref, b_ref, ti_ref, u_ref, w_ref, pm_ref, vn_ref, sall_ref, do_ref,
             dvn_ref, dkd_ref, dgl_ref, dq_ref, dk_ref, dv_ref, dg_ref, dbeta_ref, dgc_s):
        tri, strict, eye, r_i, c_i = _masks(C)
        lower = jnp.where(r_i >= c_i, 1.0, 0.0)
        lane = lax.broadcasted_iota(jnp.int32, (1, C), 1)
        rsum = lambda a: jnp.sum(a, axis=1, keepdims=True)
        for c0 in range(0, cpt, grp):
            cs = list(range(c0, c0 + grp))
            inst = [(c, hh) for c in cs for hh in range(2)]
            n = len(inst)
            rows = {c: slice(c * C, (c + 1) * C) for c in cs}
            hsl = [slice(hh * GDN_DK, (hh + 1) * GDN_DK) for c, hh in inst]
            qc = {c: q_ref[rows[c], :] for c in cs}
            kc = {c: k_ref[rows[c], :] for c in cs}
            kk = {c: _dot_nt(kc[c], kc[c]) for c in cs}
            gcr = [gc_ref[hh, c:c + 1, :] for c, hh in inst]
            gcc = [_col(r, eye) for r in gcr]
            bc = [_col(b_ref[hh, c:c + 1, :], eye) for c, hh in inst]
            lm = [jnp.exp(jnp.where(tri, cc - r, -1e30)) for cc, r in zip(gcc, gcr)]
            e_c = [jnp.exp(cc) for cc in gcc]
            el_c = [jnp.exp(r[:, C - 1:C] - cc) for cc, r in zip(gcc, gcr)]
            gl = [jnp.exp(r[:, C - 1:C]) for r in gcr]
            doc = [do_ref[rows[c], hsl[i]] for i, (c, hh) in enumerate(inst)]
            dvn = [dvn_ref[rows[c], hsl[i]] for i, (c, hh) in enumerate(inst)]
            sv = [sall_ref[hh, c] for c, hh in inst]
            aa = [_dot_nt(jnp.concatenate([_mx(doc[i]), _mx(dvn[i])], axis=0), sv[i]) for i in range(n)]
            dpm = [jnp.where(tri, _dot_nt(doc[i], vn_ref[rows[c], hsl[i]]), 0.0) for i, (c, hh) in enumerate(inst)]
            dqd = [a[:C] for a in aa]
            drhs = [_dot_tn(ti_ref[hh, c], jnp.concatenate([dvn[i], -aa[i][C:]], axis=1))
                    for i, (c, hh) in enumerate(inst)]
            sol = [jnp.concatenate([_mx(u_ref[rows[c], hsl[i]]), w_ref[rows[c], hsl[i]]], axis=1)
                   for i, (c, hh) in enumerate(inst)]
            dnm = [-jnp.where(strict, _dot_nt(drhs[i], sol[i]), 0.0) for i in range(n)]
            dkk = [dnm[i] * bc[i] * lm[i] for i in range(n)]
            dqk = [dpm[i] * lm[i] for i in range(n)]
            dq1 = [_dot(dqk[i], kc[c]) for i, (c, hh) in enumerate(inst)]
            dk1 = [_dot(dkk[i], kc[c]) for i, (c, hh) in enumerate(inst)]
            dk2 = [_dot_tn(dkk[i], kc[c]) for i, (c, hh) in enumerate(inst)]
            dk3 = [_dot_tn(dqk[i], qc[c]) for i, (c, hh) in enumerate(inst)]
            dq_acc = {c: jnp.zeros((C, GDN_DK), F32) for c in cs}
            dk_acc = {c: jnp.zeros((C, GDN_DK), F32) for c in cs}
            for i, (c, hh) in enumerate(inst):
                k_, q_, v_ = kc[c], qc[c], v_ref[rows[c], hsl[i]]
                dvb, dkbe = drhs[i][:, :GDN_DK], drhs[i][:, GDN_DK:]
                dkd = dkd_ref[rows[c], hsl[i]]
                kb = k_ * bc[i]
                dkb = dkbe * e_c[i]
                del_el = dkd * k_ * el_c[i]
                dbc = rsum(dnm[i] * kk[c] * lm[i]) + rsum(dkb * k_ + dvb * v_)
                dq_acc[c] = dq_acc[c] + dq1[i] + dqd[i] * e_c[i]
                dk_acc[c] = dk_acc[c] + dk1[i] + dk2[i] + dk3[i] + dkd * el_c[i] + dkb * bc[i]
                dv_ref[rows[c], hsl[i]] = dvb * bc[i]
                nm = jnp.where(strict, kk[c] * bc[i] * lm[i], 0.0)
                gm = dnm[i] * nm + dpm[i] * pm_ref[hh, c].astype(F32)
                dgc_col = rsum(gm) + rsum((dkbe * kb + dqd[i] * q_) * e_c[i] - del_el)
                dglast = (jnp.sum(jnp.sum(del_el, axis=0, keepdims=True), axis=1, keepdims=True)
                          + dgl_ref[hh, c:c + 1, 0:1] * gl[i])
                dgc_s[hh, c:c + 1, :] = (_row(dgc_col, eye) - jnp.sum(gm, axis=0, keepdims=True)
                                         + jnp.where(lane == C - 1, dglast, 0.0))
                dbeta_ref[hh, c:c + 1, :] = _row(dbc, eye)
            for c in cs:
                dq_ref[rows[c], :] = dq_acc[c]
                dk_ref[rows[c], :] = dk_acc[c]
        for hh in range(2):
            dg_ref[hh] = _dot_hi(dgc_s[hh], lower)

    rows_spec = pl.BlockSpec((2, cpt, C), lambda j, i: (j, i, 0))
    qk_spec = pl.BlockSpec((tt, GDN_DK), lambda j, i: (i, j))
    v_spec = pl.BlockSpec((tt, 2 * GDN_DK), lambda j, i: (i, j))
    cc_spec = pl.BlockSpec((2, cpt, C, C), lambda j, i: (j, i, 0, 0))
    rows_shape = jax.ShapeDtypeStruct((GDN_HV, nC, C), F32)
    return pl.pallas_call(
        body, grid=(GDN_HV // 2, T // tt),
        in_specs=[qk_spec, qk_spec, v_spec, rows_spec, rows_spec, cc_spec, v_spec, v_spec, cc_spec, v_spec,
                  pl.BlockSpec((2, cpt, GDN_DK, GDN_DK), lambda j, i: (j, i, 0, 0)), v_spec, v_spec, v_spec, rows_spec],
        out_specs=[qk_spec, qk_spec, v_spec, rows_spec, rows_spec],
        out_shape=[jax.ShapeDtypeStruct((T, GDN_HV // 2 * GDN_DK), F32),
                   jax.ShapeDtypeStruct((T, GDN_HV // 2 * GDN_DK), F32),
                   jax.ShapeDtypeStruct((T, D_INNER), F32), rows_shape, rows_shape],
        scratch_shapes=[pltpu.VMEM((2, cpt, C), F32)],
        compiler_params=_params("parallel", "parallel"), name=name,
    )(q, k, v, gc, beta, tinv, u, w, pm, vn, sall, do, dvn, dkd, dgl)


def _gdn_gate_bwd(araw, braw, dg, dbeta, alog, dtb, name):
    H, T = araw.shape

    def body(a_ref, b_ref, dg_ref, dbt_ref, alog_ref, dtb_ref, da_ref, db_ref, dalog_ref, ddtb_ref):
        xa = a_ref[...] + dtb_ref[...]
        ea = jnp.exp(alog_ref[...])
        dgv = dg_ref[...]
        da = -dgv * ea * _sigmoid(xa)
        da_ref[...] = da
        dalog_ref[...] = jnp.sum(-dgv * ea * _softplus(xa), axis=1, keepdims=True)
        ddtb_ref[...] = jnp.sum(da, axis=1, keepdims=True)
        bt = _sigmoid(b_ref[...])
        db_ref[...] = dbt_ref[...] * bt * (1.0 - bt)

    return pl.pallas_call(
        body,
        out_shape=[jax.ShapeDtypeStruct((H, T), F32), jax.ShapeDtypeStruct((H, T), F32),
                   jax.ShapeDtypeStruct((H, 1), F32), jax.ShapeDtypeStruct((H, 1), F32)],
        compiler_params=pltpu.CompilerParams(vmem_limit_bytes=VMEM_LIMIT_BYTES), name=name,
    )(araw, braw, dg, dbeta, alog, dtb)


SSD_LOCKSTEP_CHUNKS = 2
SSD_LOCKSTEP_CHUNKS_BWD = 1
SSD_LOCKSTEP_HEADS_BWD = 2


def _ssd_scan_fwd(xs, bm, cm, dtraw, alog, dtb, dskip, name):
    T = xs.shape[0]
    Q = SSD_CHUNK
    tt = min(T, 1024)
    cpt, nC = tt // Q, T // Q
    GW = SSD_R * SSD_P

    def body(alog_ref, dtb_ref, dsk_ref, xs_ref, b_ref, c_ref, dt_ref, y_ref, sall_ref, dto_ref, S, dt_s, acs_s):
        gi, i = pl.program_id(0), pl.program_id(1)

        @pl.when(i == 0)
        def _():
            S[...] = jnp.zeros_like(S)

        tri, _, eye, r_i, c_i = _masks(Q)
        upper = jnp.where(r_i <= c_i, 1.0, 0.0)
        for r in range(SSD_R):
            h = SSD_R * gi + r
            dt = _softplus(dt_ref[r] + dtb_ref[h])
            dto_ref[r] = dt
            dt_s[r] = dt
            acs_s[r] = _dot_hi(-jnp.exp(alog_ref[h]) * dt, upper)

        ps = [slice(r * SSD_P, (r + 1) * SSD_P) for r in range(SSD_R)]
        s_cur = [S[:, ps[r]] for r in range(SSD_R)]
        grp = min(cpt, SSD_LOCKSTEP_CHUNKS)
        for c0 in range(0, cpt, grp):
            cs = list(range(c0, c0 + grp))
            inst = [(c, r) for c in cs for r in range(SSD_R)]
            rows = {c: slice(c * Q, (c + 1) * Q) for c in cs}
            bc_ = {c: b_ref[rows[c], :] for c in cs}
            cc_ = {c: c_ref[rows[c], :] for c in cs}
            cb = {c: _dot_nt(cc_[c], bc_[c]) for c in cs}
            xr = [xs_ref[rows[c], ps[r]] for c, r in inst]
            acr = [acs_s[r, c:c + 1, :] for c, r in inst]
            acc = [_col_bcast(a, Q) for a in acr]
            dtr = [dt_s[r, c:c + 1, :] for c, r in inst]
            mm = [cb[c] * (jnp.exp(jnp.where(tri, acc[i] - acr[i], -1e30)) * dtr[i]) for i, (c, r) in enumerate(inst)]
            bct = {c: bc_[c].T for c in cs}
            st = [_dot(bct[c] * (jnp.exp(acr[i][:, Q - 1:Q] - acr[i]) * dtr[i]), xr[i]) for i, (c, r) in enumerate(inst)]
            yd = [_dot(mm[i], xr[i]) for i in range(len(inst))]
            s_prev = []
            for i, (c, r) in enumerate(inst):
                s_prev.append(s_cur[r])
                s_cur[r] = s_cur[r] * jnp.exp(acr[i][:, Q - 1:Q]) + st[i]
            yo = [_dot(cc_[c] * jnp.exp(acc[i]), s_prev[i]) for i, (c, r) in enumerate(inst)]
            for i, (c, r) in enumerate(inst):
                sall_ref[0, c, :, ps[r]] = _mx(s_prev[i])
                y_ref[rows[c], ps[r]] = yd[i] + yo[i] + dsk_ref[SSD_R * gi + r] * xr[i]
        for r in range(SSD_R):
            S[:, ps[r]] = s_cur[r]

    smem = pl.BlockSpec(memory_space=pltpu.SMEM)
    rows_spec = pl.BlockSpec((SSD_R, cpt, Q), lambda g, i: (g, i, 0))
    return pl.pallas_call(
        body, grid=(SSD_G, T // tt),
        in_specs=[smem, smem, smem,
                  pl.BlockSpec((tt, GW), lambda g, i: (i, g)), pl.BlockSpec((tt, SSD_N), lambda g, i: (i, g)),
                  pl.BlockSpec((tt, SSD_N), lambda g, i: (i, g)), rows_spec],
        out_specs=[pl.BlockSpec((tt, GW), lambda g, i: (i, g)),
                   pl.BlockSpec((1, cpt, SSD_N, GW), lambda g, i: (g, i, 0, 0)), rows_spec],
        out_shape=[jax.ShapeDtypeStruct((T, D_INNER), F32), jax.ShapeDtypeStruct((SSD_G, nC, SSD_N, GW), MXU_DTYPE),
                   jax.ShapeDtypeStruct((SSD_H, nC, Q), F32)],
        scratch_shapes=[pltpu.VMEM((SSD_N, GW), F32), pltpu.VMEM((SSD_R, cpt, Q), F32),
                        pltpu.VMEM((SSD_R, cpt, Q), F32)],
        compiler_params=_params("parallel", "arbitrary"), name=name,
    )(alog, dtb, dskip, xs, bm, cm, dtraw)


def _ssd_scan_bwd(xs, bm, cm, dt, sall, dy, alog, dskip, name):
    T = xs.shape[0]
    Q = SSD_CHUNK
    tt = min(T, 1024)
    cpt, nC, nT = tt // Q, T // Q, T // tt
    GW = SSD_R * SSD_P

    def body(alog_ref, dsk_ref, xs_ref, b_ref, c_ref, dt_ref, sall_ref, dy_ref,
             dxs_ref, db_ref, dc_ref, da_ref, ddt_ref, dd_ref, dS, acs_s, dacs_s, ddt_s, dd_s):
        gi, i = pl.program_id(0), pl.program_id(1)

        @pl.when(i == 0)
        def _():
            dS[...] = jnp.zeros_like(dS)

        tri, _, eye, r_i, c_i = _masks(Q)
        upper = jnp.where(r_i <= c_i, 1.0, 0.0)
        lower = jnp.where(r_i >= c_i, 1.0, 0.0)
        lane = lax.broadcasted_iota(jnp.int32, (1, Q), 1)
        for r in range(SSD_R):
            acs_s[r] = _dot_hi(-jnp.exp(alog_ref[SSD_R * gi + r]) * dt_ref[r], upper)

        ps = [slice(r * SSD_P, (r + 1) * SSD_P) for r in range(SSD_R)]
        ds_cur = [dS[:, ps[r]] for r in range(SSD_R)]
        grp = min(cpt, SSD_LOCKSTEP_CHUNKS_BWD)
        csum = lambda a: jnp.sum(a, axis=0, keepdims=True)
        tsum = lambda a: jnp.sum(csum(a), axis=1, keepdims=True)
        ones8 = jnp.ones((8, SSD_P), F32)
        for c0 in range(cpt - grp, -1, -grp):
            cs = list(range(c0 + grp - 1, c0 - 1, -1))
            rows = {c: slice(c * Q, (c + 1) * Q) for c in cs}
            bc_ = {c: b_ref[rows[c], :] for c in cs}
            cc_ = {c: c_ref[rows[c], :] for c in cs}
            cb = {c: _dot_nt(cc_[c], bc_[c]) for c in cs}
            cbt = {c: _dot_nt(bc_[c], cc_[c]) for c in cs}
            bct = {c: bc_[c].T for c in cs}
            cct = {c: cc_[c].T for c in cs}
            dcb = {c: jnp.zeros((Q, Q), F32) for c in cs}
            dcbt = {c: jnp.zeros((Q, Q), F32) for c in cs}
            db_acc = {c: jnp.zeros((Q, SSD_N), F32) for c in cs}
            dc_acc = {c: jnp.zeros((Q, SSD_N), F32) for c in cs}
            for h0 in range(0, SSD_R, SSD_LOCKSTEP_HEADS_BWD):
                inst = [(c, r) for c in cs for r in range(h0, h0 + SSD_LOCKSTEP_HEADS_BWD)]
                n = len(inst)
                xr = [xs_ref[rows[c], ps[r]] for c, r in inst]
                dyr = [dy_ref[rows[c], ps[r]] for c, r in inst]
                acr = [acs_s[r, c:c + 1, :] for c, r in inst]
                dtr = [dt_ref[r, c:c + 1, :] for c, r in inst]
                acc = [_col_bcast(a, Q) for a in acr]
                dtb = [_col_bcast(d, Q) for d in dtr]
                al = [a[:, Q - 1:Q] for a in acr]
                e_c = [jnp.exp(a) for a in acc]
                dl_c = [jnp.exp(al[i] - acc[i]) for i in range(n)]
                e_r = [jnp.exp(a) for a in acr]
                dl_r = [jnp.exp(al[i] - acr[i]) for i in range(n)]
                gl = [jnp.exp(a) for a in al]
                lm = [jnp.exp(jnp.where(tri, acc[i] - acr[i], -1e30)) for i in range(n)]
                lmt = [jnp.exp(jnp.where(r_i <= c_i, acr[i] - acc[i], -1e30)) for i in range(n)]
                mmt = [cbt[c] * lmt[i] for i, (c, r) in enumerate(inst)]
                sr = [sall_ref[0, c, :, ps[r]] for c, r in inst]
                dmm0 = [_dot_nt(dyr[i], xr[i]) for i in range(n)]
                dmm0t = [_dot_nt(xr[i], dyr[i]) for i in range(n)]
                dxd1 = [_dot(mmt[i], dyr[i]) for i in range(n)]
                dce = [_dot_nt(dyr[i], sr[i]) for i in range(n)]
                dcet = [_dot_nt(sr[i], dyr[i]) for i in range(n)]
                cdy = [_dot(cct[c] * e_r[i], dyr[i]) for i, (c, r) in enumerate(inst)]
                dsn = []
                for i, (c, r) in enumerate(inst):
                    dsn.append(ds_cur[r])
                    ds_cur[r] = gl[i] * ds_cur[r] + cdy[i]
                dxd = [dxd1[i] + _dot(bc_[c] * dl_c[i], dsn[i]) for i, (c, r) in enumerate(inst)]
                dbd0 = [_dot_nt(xr[i], dsn[i]) for i in range(n)]
                dbd0t = [_dot_nt(dsn[i], xr[i]) for i in range(n)]
                for i, (c, r) in enumerate(inst):
                    dgl = tsum(dsn[i] * sr[i].astype(F32))
                    dc_acc[c] = dc_acc[c] + dce[i] * e_c[i]
                    db_acc[c] = db_acc[c] + dbd0[i] * (dtb[i] * dl_c[i])
                    dl0 = dmm0[i] * lm[i]
                    dl0t = dmm0t[i] * (lmt[i] * dtb[i])
                    dcb[c] = dcb[c] + dl0 * dtr[i]
                    dcbt[c] = dcbt[c] + dl0t
                    csum_gm0 = csum(dl0 * cb[c])
                    rsum_gm = csum(dl0t * cbt[c])
                    r_de = csum(dcet[i] * cct[c]) * e_r[i]
                    r_dl = csum(dbd0t[i] * bct[c]) * dl_r[i]
                    dalast = jnp.sum(r_dl * dtr[i], axis=1, keepdims=True) + dgl * gl[i]
                    dacs_s[r, c:c + 1, :] = (rsum_gm + r_de - (r_dl + csum_gm0) * dtr[i]
                                             + jnp.where(lane == Q - 1, dalast, 0.0))
                    ddt_s[r, c:c + 1, :] = csum_gm0 + r_dl
                    dd_s[r, c:c + 1, :] = _dot_nt(ones8, dyr[i] * xr[i])[0:1]
                    dxs_ref[rows[c], ps[r]] = dxd[i] * dtb[i][:, :SSD_P] + dsk_ref[SSD_R * gi + r] * dyr[i]
            for c in cs:
                dc_ref[rows[c], :] = dc_acc[c] + _dot(dcb[c], bc_[c])
                db_ref[rows[c], :] = db_acc[c] + _dot(dcbt[c], cc_[c])
        for r in range(SSD_R):
            dS[:, ps[r]] = ds_cur[r]
        for r in range(SSD_R):
            da_ref[r] = _dot_hi(dacs_s[r], lower)
            ddt_ref[r] = ddt_s[r]
            dd_ref[r] = dd_s[r]

    rev = lambda i: nT - 1 - i
    smem = pl.BlockSpec(memory_space=pltpu.SMEM)
    rows_spec = pl.BlockSpec((SSD_R, cpt, Q), lambda g, i: (g, rev(i), 0))
    x_spec = pl.BlockSpec((tt, GW), lambda g, i: (rev(i), g))
    n_spec = pl.BlockSpec((tt, SSD_N), lambda g, i: (rev(i), g))
    rows_shape = jax.ShapeDtypeStruct((SSD_H, nC, Q), F32)
    return pl.pallas_call(
        body, grid=(SSD_G, nT),
        in_specs=[smem, smem, x_spec, n_spec, n_spec, rows_spec,
                  pl.BlockSpec((1, cpt, SSD_N, GW), lambda g, i: (g, rev(i), 0, 0)), x_spec],
        out_specs=[x_spec, n_spec, n_spec, rows_spec, rows_spec, rows_spec],
        out_shape=[jax.ShapeDtypeStruct((T, D_INNER), F32), jax.ShapeDtypeStruct((T, SSD_G * SSD_N), F32),
                   jax.ShapeDtypeStruct((T, SSD_G * SSD_N), F32), rows_shape, rows_shape, rows_shape],
        scratch_shapes=[pltpu.VMEM((SSD_N, GW), F32)] + [pltpu.VMEM((SSD_R, cpt, Q), F32)] * 4,
        compiler_params=_params("parallel", "arbitrary"), name=name,
    )(alog, dskip, xs, bm, cm, dt, sall, dy)


def _ssd_gate_bwd(dtraw, dt, da, ddt_direct, ddrow, alog, dtb, name):
    H, T = dtraw.shape

    def body(raw_ref, dt_ref, da_ref, ddt_ref, dd_ref, alog_ref, dtb_ref, draw_ref, dalog_ref, ddtb_ref, dD_ref):
        a = -jnp.exp(alog_ref[...])
        dav = da_ref[...]
        ddt = ddt_ref[...] + dav * a
        draw = ddt * _sigmoid(raw_ref[...] + dtb_ref[...])
        draw_ref[...] = draw
        dalog_ref[...] = jnp.sum(dav * dt_ref[...], axis=1, keepdims=True) * a
        ddtb_ref[...] = jnp.sum(draw, axis=1, keepdims=True)
        dD_ref[...] = jnp.sum(dd_ref[...], axis=1, keepdims=True)

    return pl.pallas_call(
        body,
        out_shape=[jax.ShapeDtypeStruct((H, T), F32)] + [jax.ShapeDtypeStruct((H, 1), F32)] * 3,
        compiler_params=pltpu.CompilerParams(vmem_limit_bytes=VMEM_LIMIT_BYTES), name=name,
    )(dtraw, dt, da, ddt_direct, ddrow, alog, dtb)


def _adamw(parts, w, m, v, name):
    R, C = w.shape
    tr = 256 if R % 256 == 0 else R

    def body(p_ref, w_ref, m_ref, v_ref, g_ref, d_ref, nm_ref, nv_ref):
        g = p_ref[0].astype(F32)
        for s in range(1, N_DEV):
            g = g + p_ref[s].astype(F32)
        mn = ADAM_B1 * m_ref[...] + (1.0 - ADAM_B1) * g
        vn = ADAM_B2 * v_ref[...] + (1.0 - ADAM_B2) * (g * g)
        mh = mn / (1.0 - ADAM_B1 ** ADAM_STEP)
        vh = vn / (1.0 - ADAM_B2 ** ADAM_STEP)
        g_ref[...] = g
        d_ref[...] = -ADAM_LR * (mh / (jnp.sqrt(vh) + ADAM_EPS) + ADAM_WD * w_ref[...])
        nm_ref[...] = mn
        nv_ref[...] = vn

    blk = pl.BlockSpec((tr, C), lambda i: (i, 0))
    return pl.pallas_call(
        body, grid=(R // tr,),
        in_specs=[pl.BlockSpec((N_DEV, tr, C), lambda i: (0, i, 0)), blk, blk, blk],
        out_specs=[blk] * 4,
        out_shape=[jax.ShapeDtypeStruct((R, C), F32)] * 4,
        compiler_params=_params("parallel"), name=name,
    )(parts, w, m, v)


def _me():
    x, y, c = lax.axis_index("x"), lax.axis_index("y"), lax.axis_index("c")
    return x, y, c


def _peer(d):
    x, y, c = _me()
    px = 1 - x if d & 4 else x
    py = 1 - y if d & 2 else y
    pc = 1 - c if d & 1 else c
    return (px, py, pc), 4 * px + 2 * py + pc


def _gather_two_level(arrs, name):
    n = len(arrs)

    def body(*refs):
        g = _TwoLevelGather(refs[:n], refs[n:2 * n], None, *refs[2 * n:])
        g.begin()
        g.finish()

    anyspec = pl.BlockSpec(memory_space=pl.ANY)
    return pl.pallas_call(
        body,
        in_specs=[anyspec] * n, out_specs=[anyspec] * n,
        out_shape=_exchange_out_shapes(arrs, [True] * n),
        scratch_shapes=_exchange_semaphores(n),
        name=name,
    )(*arrs)


class _TwoLevelGather:
    def __init__(self, ins, outs, bcast, ssem, rsem, lsem):
        n = self.n = len(ins)
        x, y, c = _me()
        self.c = c
        self.me, self.sibling = (x, y, c), (x, y, 1 - c)
        self.chips = [(1 - x, y), (x, 1 - y), (1 - x, 1 - y)]
        self.ins, self.outs, self.ssem, self.rsem = ins, outs, ssem, rsem
        self.mine = [pltpu.make_async_copy(ins[a], self.slot(a, self.me), lsem.at[a]) for a in range(n)]
        self.first = []
        for a in range(n):
            self.first.append(self.copy(a, 0, self.me, self.sibling, src=ins[a]))
            self.first += [self.copy(a, 1 + j, self.me, (*chip, c), src=ins[a]) for j, chip in enumerate(self.chips)]
        self.passed = [[self.copy(a, 4 + j, (*chip, c), self.sibling) for j, chip in enumerate(self.chips)]
                       for a in range(n)]

    def slot(self, a, block):
        px, py, pc = block
        return self.outs[a].at[4 * px + 2 * py + pc]

    def copy(self, a, k, block, to, src=None):
        return pltpu.make_async_remote_copy(
            src_ref=self.slot(a, block) if src is None else src, dst_ref=self.slot(a, block),
            send_sem=self.ssem.at[a, k], recv_sem=self.rsem.at[a, k], device_id=to, device_id_type=MESH)

    def begin(self):
        for cp in self.mine + self.first:
            cp.start()

    def finish(self):
        c = self.c
        for j, chip in enumerate(self.chips):
            for a in range(self.n):
                self.copy(a, 1 + j, (*chip, c), self.me).wait_recv()
                self.passed[a][j].start()
        for a in range(self.n):
            self.copy(a, 0, self.sibling, self.me).wait_recv()
            for j, chip in enumerate(self.chips):
                self.copy(a, 4 + j, (*chip, 1 - c), self.me).wait_recv()
        for cp in self.first + [cp for row in self.passed for cp in row]:
            cp.wait_send()
        for cp in self.mine:
            cp.wait()


def _exchange(arrs, bcast, name):
    n = len(arrs)

    def body(*refs):
        ex = _Exchange(refs[:n], refs[n:2 * n], bcast, *refs[2 * n:])
        ex.begin()
        ex.finish()

    anyspec = pl.BlockSpec(memory_space=pl.ANY)
    return pl.pallas_call(
        body,
        in_specs=[anyspec] * n, out_specs=[anyspec] * n,
        out_shape=_exchange_out_shapes(arrs, bcast),
        scratch_shapes=_exchange_semaphores(n),
        name=name,
    )(*arrs)


def _exchange_out_shapes(arrs, bcast):
    return [jax.ShapeDtypeStruct((N_DEV,) + (a.shape if b else a.shape[1:]), a.dtype) for a, b in zip(arrs, bcast)]


def _exchange_semaphores(n):
    return [pltpu.SemaphoreType.DMA((n, N_DEV - 1)), pltpu.SemaphoreType.DMA((n, N_DEV - 1)),
            pltpu.SemaphoreType.DMA((n,))]


class _Exchange:
    def __init__(self, ins, outs, bcast, ssem, rsem, lsem):
        n = len(ins)
        x, y, c = _me()
        me = 4 * x + 2 * y + c

        def src(a, dest):
            return ins[a] if bcast[a] else ins[a].at[dest]

        self.local = [pltpu.make_async_copy(src(a, me), outs[a].at[me], lsem.at[a]) for a in range(n)]
        self.sends, self.recvs = [], []
        for a in range(n):
            for d in range(1, N_DEV):
                peer, pid = _peer(d)
                self.sends.append(pltpu.make_async_remote_copy(
                    src_ref=src(a, pid), dst_ref=outs[a].at[me], send_sem=ssem.at[a, d - 1],
                    recv_sem=rsem.at[a, d - 1], device_id=peer, device_id_type=MESH))
                self.recvs.append(pltpu.make_async_remote_copy(
                    src_ref=src(a, pid), dst_ref=outs[a].at[pid], send_sem=ssem.at[a, d - 1],
                    recv_sem=rsem.at[a, d - 1], device_id=peer, device_id_type=MESH))

    def begin(self):
        for cp in self.local + self.sends:
            cp.start()

    def finish(self):
        for cp in self.recvs:
            cp.wait_recv()
        for cp in self.sends:
            cp.wait_send()
        for cp in self.local:
            cp.wait()


def _to_rows(cols, chunk):
    T, H = cols.shape
    return cols.T.reshape(H, T // chunk, chunk)


def _from_rows(rows):
    return rows.T


def _pad_cols(a, width):
    return jnp.pad(a, ((0, 0), (0, width - a.shape[1])))


def _local_step(x, tgt, p, late_weights=None, early_grads=None, late_grads=None, h_first=None):
    T = x.shape[0]
    zb = lambda n: jnp.zeros((1, n), F32)
    gw = p["gdn_w_in"]
    g_wparts = [gw[:, 0:1024], gw[:, 1024:2048], gw[:, 2048:4096], gw[:, 4096:6144], _pad_cols(gw[:, 6144:6176], PAD_W)]
    nw0, nw1 = p["norm_w"][0:1], p["norm_w"][1:2]
    gcw = p["gdn_conv_w"]
    cw_q, cw_k, cw_v = gcw[:, 0:1024], gcw[:, 1024:2048], gcw[:, 2048:4096]
    g_convs = [(cw_q, zb(1024), True, GDN_DK ** -0.5), (cw_k, zb(1024), True, 1.0), (cw_v, zb(2048), False, 1.0),
               None, None]
    if late_weights is None:
        h0, (q_pre, k_pre, v_pre, z0, ab), (q, k, v), g_cpre = _norm_inproj(x, nw0, g_wparts, g_convs, "gdn_inproj")
    else:
        comm, assemble = late_weights
        h0, (q_pre, k_pre, v_pre, z0, ab), (q, k, v), g_cpre, gathered = _norm_inproj(
            x if h_first is None else h_first, nw0, g_wparts, g_convs, "gdn_inproj", comm, h_first is not None)
        p = dict(p, **assemble(gathered))
    braw = _to_rows(ab[:, 0:GDN_HV], GDN_CHUNK)
    araw = _to_rows(ab[:, GDN_HV:2 * GDN_HV], GDN_CHUNK)
    g_alog, g_dtb = p["gdn_a_log"].reshape(-1), p["gdn_dt_bias"].reshape(-1)
    g_u, g_w, g_pm, g_ti, g_rows, beta_rows, gc_rows, g_qd, g_kd = _gdn_prep(q, k, v, araw, braw, g_alog, g_dtb,
                                                                             "gdn_prep")
    o0, g_vn, g_sall = _gdn_state_fwd(g_qd, g_kd, g_u, g_w, g_pm, gc_rows, "gdn_state_fwd")
    x1 = _out_fwd(o0, z0, p["gdn_norm_w"], p["gdn_w_out"], x, GDN_DK, False, "gdn_out")
    sw = p["ssd_w_in"]
    s_wparts = [sw[:, 0:2048], sw[:, 2048:4096], sw[:, 4096:5120], sw[:, 5120:6144], _pad_cols(sw[:, 6144:6176], PAD_W)]
    scw, scb = p["ssd_conv_w"], p["ssd_conv_b"]
    s_convs = [None, (scw[:, 0:2048], scb[:, 0:2048], False, 1.0), (scw[:, 2048:3072], scb[:, 2048:3072], False, 1.0),
               (scw[:, 3072:4096], scb[:, 3072:4096], False, 1.0), None]
    h1, (z1, xs_pre, b_pre, c_pre, dtp), (xs, bm, cm), s_cpre = _norm_inproj(x1, nw1, s_wparts, s_convs, "ssd_inproj")
    dtraw = _to_rows(dtp[:, 0:SSD_H], SSD_CHUNK)
    s_alog, s_dtb, s_d = p["ssd_a_log"].reshape(-1), p["ssd_dt_bias"].reshape(-1), p["ssd_d"].reshape(-1)
    y1, s_sall, dt_rows = _ssd_scan_fwd(xs, bm, cm, dtraw, s_alog, s_dtb, s_d, "ssd_scan_fwd")
    dx2, d_fw, loss = _out_fwd(y1, z1, p["ssd_norm_w"], p["ssd_w_out"], x1, D_INNER // SSD_G, True, "ssd_out_loss",
                               (p["final_norm_w"].reshape(1, -1), tgt))
    dy1, dz1, d_snw, yn1 = _out_bwd(dx2, y1, z1, p["ssd_norm_w"], p["ssd_w_out"], D_INNER // SSD_G, True, "ssd_out_bwd")
    d_swout = _matmul_tn(yn1, dx2, "ssd_wout_grad")
    dxs, dbm, dcm, da_rows, ddt_rows, dd_rows = _ssd_scan_bwd(xs, bm, cm, dt_rows, s_sall, dy1, s_alog, s_d, "ssd_scan_bwd")
    col = lambda a: a.reshape(-1, 1)
    dtraw_g, d_salog, d_sdtb, d_sd = _ssd_gate_bwd(
        dtraw.reshape(SSD_H, T), dt_rows.reshape(SSD_H, T), da_rows.reshape(SSD_H, T),
        ddt_rows.reshape(SSD_H, T), dd_rows.reshape(SSD_H, T), col(s_alog), col(s_dtb), "ssd_gate_bwd")
    dxs_pre, dcw_x, dcb_x = _conv_bwd(xs_pre, s_cpre[0], scw[:, 0:2048], dxs, False, 1.0, "ssd_conv_x_bwd")
    db_pre, dcw_b, dcb_b = _conv_bwd(b_pre, s_cpre[1], scw[:, 2048:3072], dbm, False, 1.0, "ssd_conv_b_bwd")
    dc_pre, dcw_c, dcb_c = _conv_bwd(c_pre, s_cpre[2], scw[:, 3072:4096], dcm, False, 1.0, "ssd_conv_c_bwd")
    ddtp = _pad_cols(_from_rows(dtraw_g), PAD_W)
    s_dparts = [dz1, dxs_pre, db_pre, dc_pre, ddtp]
    dx1, d_nw1 = _inproj_bwd(x1, nw1, s_dparts, s_wparts, dx2, "ssd_inproj_bwd")
    s_dw = [_matmul_tn(h1, d, "ssd_win_grad_%d" % n) for n, d in enumerate(s_dparts)]
    d_swin = jnp.concatenate(s_dw[:4] + [s_dw[4][:, 0:SSD_H]], axis=1)
    early_recv = None
    if early_grads is None:
        do0, dz0, d_gnw, yn0 = _out_bwd(dx1, o0, z0, p["gdn_norm_w"], p["gdn_w_out"], GDN_DK, False, "gdn_out_bwd")
    else:
        do0, dz0, d_gnw, yn0, early_recv = _out_bwd(dx1, o0, z0, p["gdn_norm_w"], p["gdn_w_out"], GDN_DK, False,
                                                    "gdn_out_bwd", early_grads(d_swin, d_swout))
    d_gwout = _matmul_tn(yn0, dx1, "gdn_wout_grad")
    g_dvn, g_dkd, g_dgl = _gdn_state_bwd(g_qd, g_kd, g_w, g_pm, g_vn, g_sall, gc_rows, do0, "gdn_state_bwd")
    dq, dk, dv, dg_rows, dbeta_rows = _gdn_local_bwd(q, k, v, gc_rows, beta_rows, g_ti, g_u, g_w, g_pm, g_vn, g_sall,
                                                     do0, g_dvn, g_dkd, g_dgl, "gdn_local_bwd")
    da_g, db_g, d_galog, d_gdtb = _gdn_gate_bwd(
        araw.reshape(GDN_HV, T), braw.reshape(GDN_HV, T), dg_rows.reshape(GDN_HV, T),
        dbeta_rows.reshape(GDN_HV, T), col(g_alog), col(g_dtb), "gdn_gate_bwd")
    dq_pre, dcw_q, _ = _conv_bwd(q_pre, g_cpre[0], cw_q, dq, True, GDN_DK ** -0.5, "gdn_conv_q_bwd")
    dk_pre, dcw_k, _ = _conv_bwd(k_pre, g_cpre[1], cw_k, dk, True, 1.0, "gdn_conv_k_bwd")
    dv_pre, dcw_v, _ = _conv_bwd(v_pre, g_cpre[2], cw_v, dv, False, 1.0, "gdn_conv_v_bwd")
    dab = _pad_cols(jnp.concatenate([_from_rows(db_g), _from_rows(da_g)], axis=1), PAD_W)
    g_dparts = [dq_pre, dk_pre, dv_pre, dz0, dab]
    g_dw = [_matmul_tn(h0, d, "gdn_win_grad_%d" % n) for n, d in enumerate(g_dparts)]
    d_gwin = jnp.concatenate(g_dw[:4] + [g_dw[4][:, 0:2 * GDN_HV]], axis=1)
    sharded_grads = {
        "gdn_w_in": d_gwin, "gdn_w_out": d_gwout,
        "gdn_conv_w": jnp.concatenate([dcw_q, dcw_k, dcw_v], axis=1),
        "ssd_conv_w": jnp.concatenate([dcw_x, dcw_b, dcw_c], axis=1),
        "ssd_conv_b": jnp.concatenate([dcb_x, dcb_b, dcb_c], axis=1), "ssd_norm_w": d_snw}
    late_recv = None
    if late_grads is None:
        dx0, d_nw0 = _inproj_bwd(x, nw0, g_dparts, g_wparts, dx1, "gdn_inproj_bwd")
    else:
        dx0, d_nw0, late_recv = _inproj_bwd(x, nw0, g_dparts, g_wparts, dx1, "gdn_inproj_bwd",
                                            late_grads(sharded_grads))
    grads = {
        "norm_w": jnp.concatenate([d_nw0, d_nw1], axis=0),
        "gdn_w_in": d_gwin,
        "gdn_conv_w": jnp.concatenate([dcw_q, dcw_k, dcw_v], axis=1),
        "gdn_a_log": d_galog.reshape(1, -1),
        "gdn_dt_bias": d_gdtb.reshape(1, -1),
        "gdn_norm_w": d_gnw,
        "gdn_w_out": d_gwout,
        "ssd_w_in": d_swin,
        "ssd_conv_w": jnp.concatenate([dcw_x, dcw_b, dcw_c], axis=1),
        "ssd_conv_b": jnp.concatenate([dcb_x, dcb_b, dcb_c], axis=1),
        "ssd_dt_bias": d_sdtb.reshape(1, -1),
        "ssd_a_log": d_salog.reshape(1, -1),
        "ssd_d": d_sd.reshape(1, -1),
        "ssd_norm_w": d_snw,
        "ssd_w_out": d_swout,
        "final_norm_w": d_fw,
    }
    if early_grads is not None:
        return loss, dx0, grads, early_recv, late_recv
    return loss, dx0, grads


WEIGHTS = ["norm_w", "gdn_w_in", "gdn_conv_w", "gdn_a_log", "gdn_dt_bias", "gdn_norm_w", "gdn_w_out", "ssd_w_in",
           "ssd_conv_w", "ssd_conv_b", "ssd_dt_bias", "ssd_a_log", "ssd_d", "ssd_norm_w", "ssd_w_out", "final_norm_w"]
COL_SHARDED = ["gdn_w_in", "ssd_w_in"]
ROW_SHARDED = ["gdn_w_out", "ssd_w_out"]
SMALL_SHARDED = ["gdn_conv_w", "ssd_conv_w", "ssd_conv_b", "ssd_norm_w"]
REPLICATED = ["norm_w", "gdn_a_log", "gdn_dt_bias", "gdn_norm_w", "ssd_dt_bias", "ssd_a_log", "ssd_d", "final_norm_w"]


def _pack(arrs):
    return jnp.concatenate([a.reshape(-1) for a in arrs]).reshape(1, -1)


def _unpack(flat, shapes):
    out, pos = [], 0
    for s in shapes:
        n = 1
        for dim in s:
            n *= dim
        out.append(flat[pos:pos + n].reshape(s))
        pos += n
    return out


def _cols_to_shards(full):
    R, C = full.shape
    return full.reshape(R, N_DEV, C // N_DEV).transpose(1, 0, 2)


def _shards_to_cols(shards):
    n, R, c = shards.shape
    return shards.transpose(1, 0, 2).reshape(R, n * c)


def kernel(x, norm_w, gdn_w_in, gdn_conv_w, gdn_a_log, gdn_dt_bias, gdn_norm_w, gdn_w_out, ssd_w_in, ssd_conv_w, ssd_conv_b, ssd_dt_bias, ssd_a_log, ssd_d, ssd_norm_w, ssd_w_out, final_norm_w, loss_target, m_norm_w, m_gdn_w_in, m_gdn_conv_w, m_gdn_a_log, m_gdn_dt_bias, m_gdn_norm_w, m_gdn_w_out, m_ssd_w_in, m_ssd_conv_w, m_ssd_conv_b, m_ssd_dt_bias, m_ssd_a_log, m_ssd_d, m_ssd_norm_w, m_ssd_w_out, m_final_norm_w, v_norm_w, v_gdn_w_in, v_gdn_conv_w, v_gdn_a_log, v_gdn_dt_bias, v_gdn_norm_w, v_gdn_w_out, v_ssd_w_in, v_ssd_conv_w, v_ssd_conv_b, v_ssd_dt_bias, v_ssd_a_log, v_ssd_d, v_ssd_norm_w, v_ssd_w_out, v_final_norm_w):
    w = dict(norm_w=norm_w, gdn_w_in=gdn_w_in[0], gdn_conv_w=gdn_conv_w[0], gdn_a_log=gdn_a_log,
             gdn_dt_bias=gdn_dt_bias, gdn_norm_w=gdn_norm_w, gdn_w_out=gdn_w_out[0], ssd_w_in=ssd_w_in[0],
             ssd_conv_w=ssd_conv_w[0], ssd_conv_b=ssd_conv_b, ssd_dt_bias=ssd_dt_bias, ssd_a_log=ssd_a_log,
             ssd_d=ssd_d, ssd_norm_w=ssd_norm_w, ssd_w_out=ssd_w_out[0], final_norm_w=final_norm_w.reshape(1, -1))
    m = dict(norm_w=m_norm_w, gdn_w_in=m_gdn_w_in[0], gdn_conv_w=m_gdn_conv_w[0], gdn_a_log=m_gdn_a_log,
             gdn_dt_bias=m_gdn_dt_bias, gdn_norm_w=m_gdn_norm_w, gdn_w_out=m_gdn_w_out[0], ssd_w_in=m_ssd_w_in[0],
             ssd_conv_w=m_ssd_conv_w[0], ssd_conv_b=m_ssd_conv_b, ssd_dt_bias=m_ssd_dt_bias, ssd_a_log=m_ssd_a_log,
             ssd_d=m_ssd_d, ssd_norm_w=m_ssd_norm_w, ssd_w_out=m_ssd_w_out[0], final_norm_w=m_final_norm_w.reshape(1, -1))
    v = dict(norm_w=v_norm_w, gdn_w_in=v_gdn_w_in[0], gdn_conv_w=v_gdn_conv_w[0], gdn_a_log=v_gdn_a_log,
             gdn_dt_bias=v_gdn_dt_bias, gdn_norm_w=v_gdn_norm_w, gdn_w_out=v_gdn_w_out[0], ssd_w_in=v_ssd_w_in[0],
             ssd_conv_w=v_ssd_conv_w[0], ssd_conv_b=v_ssd_conv_b, ssd_dt_bias=v_ssd_dt_bias, ssd_a_log=v_ssd_a_log,
             ssd_d=v_ssd_d, ssd_norm_w=v_ssd_norm_w, ssd_w_out=v_ssd_w_out[0], final_norm_w=v_final_norm_w.reshape(1, -1))
    out_shapes = {n: a.shape for n, a in zip(
        WEIGHTS, [norm_w, gdn_w_in, gdn_conv_w, gdn_a_log, gdn_dt_bias, gdn_norm_w, gdn_w_out, ssd_w_in, ssd_conv_w,
                  ssd_conv_b, ssd_dt_bias, ssd_a_log, ssd_d, ssd_norm_w, ssd_w_out, final_norm_w])}

    small_shapes = [w[n].shape for n in SMALL_SHARDED]
    h_first, first = _norm_cast(x[0], norm_w[0:1], "gdn_norm_gather",
                                ([_mx(w["gdn_w_in"]), _pack([w[n] for n in SMALL_SHARDED])], [True] * 2, "two_level"))
    full = dict(w)
    full["gdn_w_in"] = _shards_to_cols(first[0])
    small_all = [_unpack(first[1][s, 0], small_shapes) for s in range(N_DEV)]
    for idx, n in enumerate(SMALL_SHARDED):
        full[n] = jnp.concatenate([small_all[s][idx] for s in range(N_DEV)], axis=-1)
    late = ["gdn_w_out", "ssd_w_in", "ssd_w_out"]

    def assemble(gathered):
        return {"gdn_w_out": gathered[0].reshape(-1, D_MODEL), "ssd_w_in": _shards_to_cols(gathered[1]),
                "ssd_w_out": gathered[2].reshape(-1, D_MODEL)}

    def early_grads(d_ssd_w_in, d_ssd_w_out):
        return ([_cols_to_shards(d_ssd_w_in).astype(GRAD_WIRE_DTYPE),
                 d_ssd_w_out.reshape(N_DEV, -1, D_MODEL).astype(GRAD_WIRE_DTYPE)], [False] * 2)

    def late_grads(g):
        send_small = jnp.concatenate(
            [_cols_to_shards(g[n]).reshape(N_DEV, -1) for n in SMALL_SHARDED], axis=1)[:, None, :]
        return ([_cols_to_shards(g["gdn_w_in"]).astype(GRAD_WIRE_DTYPE),
                 g["gdn_w_out"].reshape(N_DEV, -1, D_MODEL).astype(GRAD_WIRE_DTYPE), send_small], [False] * 3)

    loss, dx, grads, ssd_recv, gdn_recv = _local_step(
        x[0], loss_target[0], full, (([_mx(w[n]) for n in late], [True] * 3), assemble), early_grads, late_grads,
        h_first)

    rep_shapes = [w[n].shape for n in REPLICATED]
    tail = jnp.zeros((PAD_W - 1,), F32)
    recv_rep = _exchange([_pack([grads[n] for n in REPLICATED] + [loss, tail])], [True], "exchange_grads")[0]

    res = {}
    for n, parts in zip(["gdn_w_in", "gdn_w_out", "ssd_w_in", "ssd_w_out"], list(gdn_recv[:2]) + list(ssd_recv)):
        res[n] = _adamw(parts, w[n], m[n], v[n], "adamw_" + n)
    small_res = _adamw(gdn_recv[2], *[_pack([t[n] for n in SMALL_SHARDED]) for t in (w, m, v)], "adamw_small")
    rep_res = _adamw(recv_rep, *[_pack([t[n] for n in REPLICATED] + [jnp.zeros((PAD_W,), F32)]) for t in (w, m, v)],
                     "adamw_replicated")
    for k4 in range(4):
        for n, a in zip(SMALL_SHARDED, _unpack(small_res[k4][0], small_shapes)):
            res.setdefault(n, [None] * 4)[k4] = a
        for n, a in zip(REPLICATED, _unpack(rep_res[k4][0], rep_shapes)):
            res.setdefault(n, [None] * 4)[k4] = a

    loss = rep_res[0][0, sum(a.size for a in (w[n] for n in REPLICATED))]
    outs = [loss, dx[None]]
    for k4 in range(4):
        outs += [res[n][k4].reshape(out_shapes[n]) for n in WEIGHTS]
    return tuple(outs)
```
